```python
import math
import jax
import jax.numpy as jnp
from jax import lax
import numpy as np

D_MODEL = 2048
BATCH = 2
SEQ = 4096
DEPTH = 1
DEC_BATCH = 128
DEC_SEQ = 1
PAST_LEN = 2048
PAGE_SIZE = 128

HEAD_DIM = 128
N_HEADS = D_MODEL // 256
N_KV = N_HEADS // 4
GQA_R = N_HEADS // N_KV
D_NSA = N_HEADS * HEAD_DIM
D_GM = D_MODEL // 2
SGU_GROUPS = D_GM // 128
SGU_GW = D_GM // SGU_GROUPS
CHUNK = 128
CMP_LEN = 32
CMP_STRIDE = 16
SEL_LEN = 64
N_SEL = 16
WINDOW = 512
N_BUCKETS = 32
MAX_DIST = 128
D_FF = 4 * D_MODEL
QBLOCK = 128
EPS = 1e-6
NEG = -1e30
FORCE = 1e6
N_IN = 2 * D_GM + D_NSA + 3 * 2 * N_KV * HEAD_DIM + 3 * N_HEADS + 2 * D_MODEL

kernel_name = 'hybrid_gmlp_nsa_decode_step'


def rmsnorm(x, g):
    xf = x.astype(jnp.float32)
    y = xf * lax.rsqrt(jnp.mean(xf * xf, axis=-1, keepdims=True) + EPS)
    return y.astype(x.dtype) * g


def t5_bucket(rel):
    n = jnp.maximum(rel, 0)
    max_exact = N_BUCKETS // 2
    nf = jnp.maximum(n, 1).astype(jnp.float32)
    large = max_exact + (jnp.log(nf / max_exact) / math.log(MAX_DIST / max_exact)
                         * (N_BUCKETS - max_exact)).astype(jnp.int32)
    large = jnp.minimum(large, N_BUCKETS - 1)
    return jnp.where(n < max_exact, n, large)


def masked_softmax(s, mask):
    s = jnp.where(mask, s.astype(jnp.float32), NEG)
    p = jax.nn.softmax(s, axis=-1)
    return jnp.where(mask, p, 0.0)


def split_in(z):
    kvw = 2 * N_KV * HEAD_DIM
    sizes = (2 * D_GM, D_NSA, kvw, kvw, kvw, 3 * N_HEADS, D_MODEL, D_MODEL)
    cuts = [int(c) for c in np.cumsum(sizes)[:-1]]
    return jnp.split(z, cuts, axis=-1)


def project(x, g_mix_norm, w_in, g_sgu, g_q):
    B, T = x.shape[:2]
    z = rmsnorm(x, g_mix_norm) @ w_in
    uv, q, kvc, kvs, kvw, g_nsa, g_a, g_b = split_in(z)
    uv = jax.nn.gelu(uv)
    u = uv[..., :D_GM]
    v = rmsnorm(uv[..., D_GM:], g_sgu)
    q = rmsnorm(q.reshape(B, T, N_KV, GQA_R, HEAD_DIM), g_q) * (HEAD_DIM ** -0.5)
    rows = lambda a: a.reshape(B, T, 2, N_KV, HEAD_DIM)
    gates = jax.nn.sigmoid(g_nsa).reshape(B, T, N_KV, GQA_R, 3)
    return u, v, q, rows(kvc), rows(kvs), rows(kvw), gates, g_a, g_b


def spatial_gate(u, v, w_sgu, b_sgu):
    B, NCH, Tc = v.shape[:3]
    mask = jnp.tril(jnp.ones((Tc, Tc), dtype=bool))
    wm = jnp.where(mask, w_sgu[:, :Tc, :Tc], 0.0)
    vg = v.reshape(B, NCH, Tc, SGU_GROUPS, SGU_GW)
    s = jnp.einsum('gts,bnsgc->bntgc', wm, vg) + b_sgu[:, :Tc].T[:, :, None]
    return u * s.reshape(B, NCH, Tc, D_GM)


def compress(k, w, pos):
    B, L = k.shape[:2]
    r = CMP_LEN // CMP_STRIDE
    n = (L - CMP_LEN) // CMP_STRIDE + 1
    seg = k[:, :(n + r - 1) * CMP_STRIDE].reshape(B, n + r - 1, CMP_STRIDE, N_KV, HEAD_DIM)
    w_seg = w.reshape(r, CMP_STRIDE, HEAD_DIM, HEAD_DIM)
    p_seg = pos.reshape(r, CMP_STRIDE, 1, HEAD_DIM)
    out = jnp.einsum('bnsgd,sde->bnge', seg[:, 0:n] + p_seg[0], w_seg[0])
    for m in range(1, r):
        out = out + jnp.einsum('bnsgd,sde->bnge', seg[:, m:m + n] + p_seg[m], w_seg[m])
    return out


def nsa_keys(kvc, kvs, g_k, pos_cmp_k, w_cmp_k, pos_cmp_v, w_cmp_v):
    kc = rmsnorm(compress(kvc[:, :, 0], w_cmp_k, pos_cmp_k), g_k[0])
    vc = compress(kvc[:, :, 1], w_cmp_v, pos_cmp_v)
    c_end = jnp.arange(kc.shape[1]) * CMP_STRIDE + (CMP_LEN - 1)
    B, L = kvs.shape[:2]
    ns = -(-L // SEL_LEN)
    kvs = jnp.pad(kvs, ((0, 0), (0, ns * SEL_LEN - L), (0, 0), (0, 0), (0, 0)))
    to_blocks = lambda a: jnp.transpose(a.reshape(B, ns, SEL_LEN, N_KV, HEAD_DIM), (0, 3, 1, 2, 4))
    ks = to_blocks(rmsnorm(kvs[:, :, 0], g_k[1]))
    vs = to_blocks(kvs[:, :, 1])
    return kc, vc, c_end, ks, vs


def overlap(nc, ns):
    i = jnp.arange(nc)[:, None] * CMP_STRIDE
    j = jnp.arange(ns)[None, :] * SEL_LEN
    return ((i < j + SEL_LEN) & (i + CMP_LEN > j)).astype(jnp.float32)


def nsa_block(q, q_pos, gates, kc, vc, c_end, ks, vs, kw, vw, kw_pos, rel_bias):
    B, Tq = q.shape[:2]
    bt = rel_bias.reshape(N_BUCKETS, N_KV, GQA_R)
    rel_c = q_pos[:, None] - c_end[None, :]
    bias_c = jnp.transpose(bt[t5_bucket(rel_c)], (2, 3, 0, 1))
    s_c = jnp.einsum('btgrd,bcgd->bgrtc', q, kc) + bias_c
    p_c = masked_softmax(s_c, rel_c >= 0)
    o_c = jnp.einsum('bgrtc,bcgd->btgrd', p_c.astype(vc.dtype), vc)
    ns = ks.shape[2]
    n_pick = min(N_SEL, ns)
    imp = jnp.einsum('bgrtc,cj->bgtj', p_c, overlap(kc.shape[1], ns))
    blk = jnp.arange(ns)[None, :]
    cur = (q_pos // SEL_LEN)[:, None]
    forced = (blk == 0) | (blk == cur) | (blk == cur - 1)
    eligible = blk * SEL_LEN <= q_pos[:, None]
    imp = jnp.where(forced, FORCE, jnp.where(eligible, imp, -1.0))
    _, idx = lax.top_k(imp, n_pick)
    bi = jnp.arange(B)[:, None, None, None]
    gi = jnp.arange(N_KV)[None, :, None, None]
    k_sel = ks[bi, gi, idx]
    v_sel = vs[bi, gi, idx]
    k_pos = idx[..., None] * SEL_LEN + jnp.arange(SEL_LEN)
    rel_s = q_pos[None, None, :, None, None] - k_pos
    bias_s = jnp.moveaxis(jnp.transpose(bt, (1, 0, 2))[gi[..., None], t5_bucket(rel_s)], -1, 2)
    s_s = jnp.einsum('btgrd,bgtnsd->bgrtns', q, k_sel) + bias_s
    p_s = masked_softmax(s_s.reshape(B, N_KV, GQA_R, Tq, -1),
                         (rel_s >= 0).reshape(B, N_KV, 1, Tq, -1))
    o_s = jnp.einsum('bgrtk,bgtkd->btgrd', p_s.astype(vs.dtype),
                     v_sel.reshape(B, N_KV, Tq, -1, HEAD_DIM))
    rel_w = q_pos[:, None] - kw_pos[None, :]
    mask_w = (rel_w >= 0) & (rel_w < WINDOW) & (kw_pos[None, :] >= 0)
    bias_w = jnp.transpose(bt[t5_bucket(rel_w)], (2, 3, 0, 1))
    s_w = jnp.einsum('btgrd,bkgd->bgrtk', q, kw) + bias_w
    p_w = masked_softmax(s_w, mask_w)
    o_w = jnp.einsum('bgrtk,bkgd->btgrd', p_w.astype(vw.dtype), vw)
    return gates[..., 0:1] * o_c + gates[..., 1:2] * o_s + gates[..., 2:3] * o_w


def merge_ffn(x, o_a, o_b, g_a, g_b, w_proj_gmlp, w_proj_nsa, w_out, g_mlp_norm, w_up, w_down):
    mix = jax.nn.sigmoid(g_a) * (o_a @ w_proj_gmlp) + jax.nn.sigmoid(g_b) * (o_b @ w_proj_nsa)
    x = x + mix @ w_out
    hm = rmsnorm(x, g_mlp_norm)
    return x + jnp.square(jax.nn.relu(hm @ w_up)) @ w_down


def layer_prompt(x, rel_bias, g_mix_norm, w_in, g_sgu, w_sgu, b_sgu, g_q, g_k, pos_cmp_k, w_cmp_k,
                 pos_cmp_v, w_cmp_v, w_proj_gmlp, w_proj_nsa, w_out, g_mlp_norm, w_up, w_down):
    B, T = x.shape[:2]
    u, v, q, kvc, kvs, kvw, gates, g_a, g_b = project(x, g_mix_norm, w_in, g_sgu, g_q)
    n_ch = T // CHUNK
    o_a = spatial_gate(u.reshape(B, n_ch, CHUNK, D_GM), v.reshape(B, n_ch, CHUNK, D_GM),
                       w_sgu, b_sgu).reshape(B, T, D_GM)
    kc, vc, c_end, ks, vs = nsa_keys(kvc, kvs, g_k, pos_cmp_k, w_cmp_k, pos_cmp_v, w_cmp_v)
    pad = ((0, 0), (WINDOW, 0), (0, 0), (0, 0))
    kw = jnp.pad(rmsnorm(kvw[:, :, 0], g_k[2]), pad)
    vw = jnp.pad(kvw[:, :, 1], pad)

    def query_block(qb):
        t0 = qb * QBLOCK
        cut = lambda a, n: lax.dynamic_slice_in_dim(a, t0, n, axis=1)
        return nsa_block(cut(q, QBLOCK), t0 + jnp.arange(QBLOCK), cut(gates, QBLOCK), kc, vc, c_end,
                         ks, vs, cut(kw, WINDOW + QBLOCK), cut(vw, WINDOW + QBLOCK),
                         t0 - WINDOW + jnp.arange(WINDOW + QBLOCK), rel_bias)

    o_b = lax.map(query_block, jnp.arange(T // QBLOCK))
    o_b = jnp.moveaxis(o_b, 0, 1).reshape(B, T, D_NSA)
    y = merge_ffn(x, o_a, o_b, g_a, g_b, w_proj_gmlp, w_proj_nsa, w_out, g_mlp_norm, w_up, w_down)
    return y, kvc, kvs, kvw[:, T - min(WINDOW, T):], v[:, ((T - 1) // CHUNK) * CHUNK:]


def layer_sample(x, cache_c, cache_s, win_buf, page_table, rel_bias, g_mix_norm, w_in, g_sgu, w_sgu,
                 b_sgu, g_q, g_k, pos_cmp_k, w_cmp_k, pos_cmp_v, w_cmp_v, w_proj_gmlp, w_proj_nsa,
                 w_out, g_mlp_norm, w_up, w_down):
    B, T = x.shape[:2]
    u, v, q, kvc, kvs, kvw, gates, g_a, g_b = project(x, g_mix_norm, w_in, g_sgu, g_q)
    o_a = spatial_gate(u[:, None], v[:, None], w_sgu, b_sgu).reshape(B, T, D_GM)
    gather = lambda cache: cache[page_table].reshape(B, -1, 2, N_KV, HEAD_DIM)
    full_c = jnp.concatenate([gather(cache_c), kvc], axis=1)
    full_s = jnp.concatenate([gather(cache_s), kvs], axis=1)
    kc, vc, c_end, ks, vs = nsa_keys(full_c, full_s, g_k, pos_cmp_k, w_cmp_k, pos_cmp_v, w_cmp_v)
    nb = win_buf.shape[1]
    full_w = jnp.concatenate([win_buf, kvw], axis=1)
    kw = rmsnorm(full_w[:, :, 0], g_k[2])
    vw = full_w[:, :, 1]
    o_b = nsa_block(q, PAST_LEN + jnp.arange(T), gates, kc, vc, c_end, ks, vs, kw, vw,
                    PAST_LEN - nb + jnp.arange(nb + T), rel_bias).reshape(B, T, D_NSA)
    y = merge_ffn(x, o_a, o_b, g_a, g_b, w_proj_gmlp, w_proj_nsa, w_out, g_mlp_norm, w_up, w_down)
    return y, kvc, kvs, full_w[:, nb + T - min(WINDOW, PAST_LEN + T):], v


def setup_inputs(seed: int = 0) -> dict:
    key = jax.random.key(seed)
    ks = jax.random.split(key, 32)
    n_pages = PAST_LEN // PAGE_SIZE
    n_pool = (5 * DEC_BATCH * n_pages) // 4
    win_buf = min(WINDOW, PAST_LEN)
    nrm = lambda k, shape, scale: jax.random.normal(k, shape, jnp.float32) * scale
    gain = lambda k, shape: 1.0 + 0.02 * jax.random.normal(k, shape, jnp.float32)
    page_table = jax.random.permutation(ks[0], n_pool)[:DEC_BATCH * n_pages]
    page_table = page_table.reshape(DEC_BATCH, n_pages).astype(jnp.int32)
    return {
        'x_prompt': nrm(ks[1], (BATCH, SEQ, D_MODEL), 1.0),
        'x_sample': nrm(ks[2], (DEC_BATCH, DEC_SEQ, D_MODEL), 1.0),
        'cache_cmp_kv': nrm(ks[3], (DEPTH, n_pool, PAGE_SIZE, 2, N_KV, HEAD_DIM), 1.0),
        'cache_sel_kv': nrm(ks[4], (DEPTH, n_pool, PAGE_SIZE, 2, N_KV, HEAD_DIM), 1.0),
        'state_win_kv': nrm(ks[5], (DEPTH, DEC_BATCH, win_buf, 2, N_KV, HEAD_DIM), 1.0),
        'page_table': page_table,
        'rel_bias': nrm(ks[6], (N_BUCKETS, N_HEADS), 0.5),
        'g_mix_norm': gain(ks[7], (DEPTH, D_MODEL)),
        'w_in': nrm(ks[8], (DEPTH, D_MODEL, N_IN), D_MODEL ** -0.5),
        'g_sgu': gain(ks[9], (DEPTH, D_GM)),
        'w_sgu': nrm(ks[10], (DEPTH, SGU_GROUPS, CHUNK, CHUNK), 0.5 * CHUNK ** -0.5),
        'b_sgu': 1.0 + nrm(ks[11], (DEPTH, SGU_GROUPS, CHUNK), 0.1),
        'g_q': gain(ks[12], (DEPTH, HEAD_DIM)),
        'g_k': gain(ks[13], (DEPTH, 3, HEAD_DIM)),
        'pos_cmp_k': nrm(ks[14], (DEPTH, CMP_LEN, HEAD_DIM), 0.1),
        'w_cmp_k': nrm(ks[15], (DEPTH, CMP_LEN, HEAD_DIM, HEAD_DIM), (CMP_LEN * HEAD_DIM) ** -0.5),
        'pos_cmp_v': nrm(ks[16], (DEPTH, CMP_LEN, HEAD_DIM), 0.1),
        'w_cmp_v': nrm(ks[17], (DEPTH, CMP_LEN, HEAD_DIM, HEAD_DIM), (CMP_LEN * HEAD_DIM) ** -0.5),
        'w_proj_gmlp': nrm(ks[18], (DEPTH, D_GM, D_MODEL), D_GM ** -0.5),
        'w_proj_nsa': nrm(ks[19], (DEPTH, D_NSA, D_MODEL), D_NSA ** -0.5),
        'w_out': nrm(ks[20], (DEPTH, D_MODEL, D_MODEL), D_MODEL ** -0.5),
        'g_mlp_norm': gain(ks[21], (DEPTH, D_MODEL)),
        'w_up': nrm(ks[22], (DEPTH, D_MODEL, D_FF), D_MODEL ** -0.5),
        'w_down': nrm(ks[23], (DEPTH, D_FF, D_MODEL), D_FF ** -0.5),
    }


def reference(x_prompt, x_sample, cache_cmp_kv, cache_sel_kv, state_win_kv, page_table, rel_bias,
              g_mix_norm, w_in, g_sgu, w_sgu, b_sgu, g_q, g_k, pos_cmp_k, w_cmp_k, pos_cmp_v, w_cmp_v,
              w_proj_gmlp, w_proj_nsa, w_out, g_mlp_norm, w_up, w_down):
    x_p, x_s = x_prompt, x_sample
    outs = [[] for _ in range(8)]
    for l in range(DEPTH):
        lw = (g_mix_norm[l], w_in[l], g_sgu[l], w_sgu[l], b_sgu[l], g_q[l], g_k[l], pos_cmp_k[l],
              w_cmp_k[l], pos_cmp_v[l], w_cmp_v[l], w_proj_gmlp[l], w_proj_nsa[l], w_out[l],
              g_mlp_norm[l], w_up[l], w_down[l])
        x_p, pc, ps, pw, pv = layer_prompt(x_p, rel_bias, *lw)
        x_s, sc, ss, sw, sv = layer_sample(x_s, cache_cmp_kv[l], cache_sel_kv[l], state_win_kv[l],
                                           page_table, rel_bias, *lw)
        for lst, a in zip(outs, (pc, ps, pw, pv, sc, ss, sw, sv)):
            lst.append(a)
    p_cmp, p_sel, p_win, p_v, s_cmp, s_sel, s_win, s_v = [jnp.stack(a) for a in outs]
    return (x_p, x_s, p_cmp, p_sel, p_win, p_v, s_cmp, s_sel, s_win, s_v)
```

```python
import functools
import math

import numpy as np
import jax
import jax.numpy as jnp
from jax import lax
from jax.experimental import pallas as pl
from jax.experimental.pallas import tpu as pltpu

F32 = jnp.float32
BF16 = jnp.bfloat16

HEAD_DIM = 128
CHUNK = 128
CMP_LEN = 32
CMP_STRIDE = 16
SEL_LEN = 64
N_SEL = 16
WINDOW = 512
N_BUCKETS = 32
MAX_DIST = 128
QBLOCK = 128
EPS = 1e-6
NEG = -1e30
HALF_NEG = -5e29
FORCE = 1e6

LANES = 128
PAD_LANE = 64
KPAD = 512
KC_FRONT = 16
KC_ROWS = 376
VMEM_LIMIT = 56 * 1024 * 1024


def _cparams(sem):
    return pltpu.CompilerParams(dimension_semantics=sem, vmem_limit_bytes=VMEM_LIMIT)


def _dot(a, b):
    return jnp.dot(a, b, preferred_element_type=F32)


def _dot_nt(a, b):
    return lax.dot_general(a, b, (((1,), (1,)), ((), ())), preferred_element_type=F32)


def _dot_hilo(a, b_bf16):
    hi = a.astype(BF16)
    lo = (a - hi.astype(F32)).astype(BF16)
    return _dot(hi, b_bf16) + _dot(lo, b_bf16)


def _rms(x):
    return x * lax.rsqrt(jnp.mean(x * x, axis=-1, keepdims=True) + EPS)


def _gelu(x):
    c = math.sqrt(2.0 / math.pi)
    return 0.5 * x * (1.0 + jnp.tanh(c * (x + 0.044715 * (x * x * x))))


def _sigmoid(x):
    return 1.0 / (1.0 + jnp.exp(-x))


def _inproj_kernel(x_ref, gmix_ref, w_ref, gsgu_ref, gq_ref, wsg_ref, bsg_ref,
                   h_ref, v_ref, oa_ref, q_ref, kvc_ref, kvs_ref, kvw_ref, gate_ref,
                   h_s, u_s, *, chunked, tm):
    j = pl.program_id(1)

    @pl.when(j == 0)
    def _():
        hb = (_rms(x_ref[...]) * gmix_ref[...]).astype(BF16)
        h_s[...] = hb
        h_ref[...] = hb

    z = _dot(h_s[...], w_ref[...])

    @pl.when(j == 0)
    def _():
        u_s[...] = _gelu(z)

    @pl.when(j == 1)
    def _():
        v = _rms(_gelu(z)) * gsgu_ref[...]
        v_ref[...] = v
        if chunked:
            row = lax.broadcasted_iota(jnp.int32, (CHUNK, CHUNK), 0)
            col = lax.broadcasted_iota(jnp.int32, (CHUNK, CHUNK), 1)
            n_groups = v.shape[1] // LANES
            for g in range(n_groups):
                wm = jnp.where(row >= col, wsg_ref[g], 0.0).astype(BF16)
                cs = slice(g * LANES, (g + 1) * LANES)
                for c in range(tm // CHUNK):
                    rs = slice(c * CHUNK, (c + 1) * CHUNK)
                    s = _dot(wm, v[rs, cs].astype(BF16)) + bsg_ref[:, cs]
                    oa_ref[rs, cs] = (u_s[rs, cs] * s).astype(BF16)
        else:
            oa_ref[...] = (u_s[...] * (v * wsg_ref[...] + bsg_ref[...])).astype(BF16)

    @pl.when(j == 2)
    def _():
        scale = HEAD_DIM ** -0.5
        for hd in range(z.shape[1] // HEAD_DIM):
            cs = slice(hd * HEAD_DIM, (hd + 1) * HEAD_DIM)
            q_ref[:, cs] = (_rms(z[:, cs]) * gq_ref[...] * scale).astype(BF16)

    @pl.when(j == 3)
    def _():
        kvc_ref[...] = z[:, :512]
        kvs_ref[...] = z[:, 512:]

    @pl.when(j == 4)
    def _():
        kvw_ref[...] = z[:, :512]
        gate_ref[...] = _sigmoid(z[:, 512:640])


def _inproj(x, gmix, w_cat, gsgu, gq, wsg, bsg, *, chunked):
    n, d = x.shape
    tm = 512 if n % 512 == 0 else n
    tn = 1024
    kern = functools.partial(_inproj_kernel, chunked=chunked, tm=tm)
    full = lambda a: pl.BlockSpec(a.shape, lambda i, j: (0,) * a.ndim)
    row = lambda w: pl.BlockSpec((tm, w), lambda i, j: (i, 0))
    out_shapes = (
        jax.ShapeDtypeStruct((n, d), BF16),
        jax.ShapeDtypeStruct((n, 1024), F32),
        jax.ShapeDtypeStruct((n, 1024), BF16),
        jax.ShapeDtypeStruct((n, 1024), BF16),
        jax.ShapeDtypeStruct((n, 512), F32),
        jax.ShapeDtypeStruct((n, 512), F32),
        jax.ShapeDtypeStruct((n, 512), F32),
        jax.ShapeDtypeStruct((n, LANES), F32),
    )
    return pl.pallas_call(
        kern,
        grid=(n // tm, w_cat.shape[1] // tn),
        in_specs=[row(d), full(gmix), pl.BlockSpec((d, tn), lambda i, j: (0, j)),
                  full(gsgu), full(gq), full(wsg), full(bsg)],
        out_specs=[row(d), row(1024), row(1024), row(1024), row(512), row(512), row(512), row(LANES)],
        out_shape=out_shapes,
        scratch_shapes=[pltpu.VMEM((tm, d), BF16), pltpu.VMEM((tm, 1024), F32)],
        compiler_params=_cparams(("arbitrary", "arbitrary")),
        name="inproj",
    )(x, gmix, w_cat, gsgu, gq, wsg, bsg)


def _kvprep_kernel(kvs_ref, kvw_ref, gk_ref, ks_ref, vs_ref, kw_ref, vw_ref):
    i = pl.program_id(1)
    rows = kvs_ref.shape[0]
    lane = lax.broadcasted_iota(jnp.int32, (rows, LANES), 1)
    row = lax.broadcasted_iota(jnp.int32, (rows, LANES), 0)

    @pl.when(i == 0)
    def _():
        aux = jnp.where(lane == PAD_LANE, 1.0, 0.0).astype(BF16)
        zk = jnp.zeros((rows, LANES), BF16)
        for g in range(2):
            ks_ref[g] = jnp.concatenate([zk, aux], axis=1)
            kw_ref[g] = jnp.concatenate([zk, aux], axis=1)
            vs_ref[g] = zk
            vw_ref[g] = zk

    @pl.when(i > 0)
    def _():
        blk = ((i - 1) * rows + row) // SEL_LEN
        onehot = jnp.where(lane == blk, 1.0, 0.0).astype(BF16)
        zaux = jnp.zeros((rows, LANES), BF16)
        for g in range(2):
            cs = slice(g * LANES, (g + 1) * LANES)
            vcs = slice(256 + g * LANES, 256 + (g + 1) * LANES)
            ks = (_rms(kvs_ref[:, cs]) * gk_ref[1:2, :]).astype(BF16)
            kw = (_rms(kvw_ref[:, cs]) * gk_ref[2:3, :]).astype(BF16)
            ks_ref[g] = jnp.concatenate([ks, onehot], axis=1)
            kw_ref[g] = jnp.concatenate([kw, zaux], axis=1)
            vs_ref[g] = kvs_ref[:, vcs].astype(BF16)
            vw_ref[g] = kvw_ref[:, vcs].astype(BF16)


def _kvprep(kvs, kvw, gk):
    b, t, _ = kvs.shape
    rows = KPAD
    nblk = t // rows
    in_map = lambda bi, i: (bi, jnp.maximum(i - 1, 0), 0)
    out_map = lambda bi, i: (bi, 0, i, 0)
    kshape = jax.ShapeDtypeStruct((b, 2, KPAD + t, 2 * LANES), BF16)
    vshape = jax.ShapeDtypeStruct((b, 2, KPAD + t, LANES), BF16)
    return pl.pallas_call(
        _kvprep_kernel,
        grid=(b, nblk + 1),
        in_specs=[pl.BlockSpec((None, rows, 512), in_map), pl.BlockSpec((None, rows, 512), in_map),
                  pl.BlockSpec(gk.shape, lambda bi, i: (0, 0))],
        out_specs=[pl.BlockSpec((None, 2, rows, 2 * LANES), out_map), pl.BlockSpec((None, 2, rows, LANES), out_map),
                   pl.BlockSpec((None, 2, rows, 2 * LANES), out_map), pl.BlockSpec((None, 2, rows, LANES), out_map)],
        out_shape=(kshape, vshape, kshape, vshape),
        compiler_params=_cparams(("arbitrary", "arbitrary")),
        name="kvprep",
    )(kvs, kvw, gk)


def _compress_core(load_x, nseg, wk_ref, wv_ref, pos_ref):
    w_refs = (wk_ref, wv_ref)
    accs = [jnp.zeros((2 * nseg + 16, 2 * LANES), F32) for _ in range(2)]
    for s2 in range(CMP_STRIDE // 2):
        for kv in range(2):
            parts = []
            for g in range(2):
                col = 2 * kv + g
                parts.append(jnp.concatenate([load_x(2 * s2, col), load_x(2 * s2 + 1, col)], axis=1).astype(BF16))
            parts.append(pos_ref[kv, s2])
            lhs = jnp.concatenate(parts, axis=0)
            accs[kv] = accs[kv] + _dot(lhs, w_refs[kv][s2])
    outs = []
    for kv in range(2):
        y = accs[kv]
        post = y[2 * nseg:2 * nseg + 1, :LANES] + y[2 * nseg + 1:2 * nseg + 2, LANES:]
        per_g = []
        for g in range(2):
            y0 = y[g * nseg:(g + 1) * nseg, :LANES]
            y1 = y[g * nseg:(g + 1) * nseg, LANES:]
            per_g.append(y0 + pltpu.roll(y1, nseg - 1, 0) + post)
        outs.append(per_g)
    return outs


def _compress_prompt_kernel(x0_ref, x1_ref, x2_ref, x3_ref, wk_ref, wv_ref, pos_ref, gk_ref, kc_ref, vc_ref):
    x_refs = (x0_ref, x1_ref, x2_ref, x3_ref)
    nseg = x0_ref.shape[0] // CMP_STRIDE
    load_x = lambda s, col: x_refs[col][pl.ds(s, nseg, stride=CMP_STRIDE), :]
    ck, cv = _compress_core(load_x, nseg, wk_ref, wv_ref, pos_ref)
    row = lax.broadcasted_iota(jnp.int32, (nseg, LANES), 0)
    lane = lax.broadcasted_iota(jnp.int32, (nseg, LANES), 1)
    valid = row < nseg - 1
    aux = jnp.where(valid, jnp.where(lane == row // 8, 1.0, 0.0), jnp.where(lane == PAD_LANE, 1.0, 0.0))
    back = KC_ROWS - KC_FRONT - nseg
    lane_f = lax.broadcasted_iota(jnp.int32, (KC_FRONT, LANES), 1)
    lane_b = lax.broadcasted_iota(jnp.int32, (back, LANES), 1)
    pad_f = jnp.concatenate([jnp.zeros((KC_FRONT, LANES), F32), jnp.where(lane_f == PAD_LANE, 1.0, 0.0)], axis=1)
    pad_b = jnp.concatenate([jnp.zeros((back, LANES), F32), jnp.where(lane_b == PAD_LANE, 1.0, 0.0)], axis=1)
    for g in range(2):
        kn = jnp.where(valid, _rms(ck[g]) * gk_ref[0:1, :], 0.0)
        kc_ref[g, 0:KC_FRONT, :] = pad_f
        kc_ref[g, KC_FRONT:KC_FRONT + nseg, :] = jnp.concatenate([kn, aux], axis=1)
        kc_ref[g, KC_FRONT + nseg:KC_ROWS, :] = pad_b
        vc_ref[g, 0:KC_FRONT, :] = jnp.zeros((KC_FRONT, LANES), F32)
        vc_ref[g, KC_FRONT:KC_FRONT + nseg, :] = jnp.where(valid, cv[g], 0.0)
        vc_ref[g, KC_FRONT + nseg:KC_ROWS, :] = jnp.zeros((back, LANES), F32)


def _compress_prompt(kvc, wk_pairs, wv_pairs, pos_rows, gk):
    b, t, _ = kvc.shape
    full = lambda a: pl.BlockSpec(a.shape, lambda bi: (0,) * a.ndim)
    return pl.pallas_call(
        _compress_prompt_kernel,
        grid=(b,),
        in_specs=[pl.BlockSpec((None, t, LANES), lambda bi, c=c: (bi, 0, c)) for c in range(4)]
        + [full(wk_pairs), full(wv_pairs), full(pos_rows), full(gk)],
        out_specs=[pl.BlockSpec((None, 2, KC_ROWS, 2 * LANES), lambda bi: (bi, 0, 0, 0)),
                   pl.BlockSpec((None, 2, KC_ROWS, LANES), lambda bi: (bi, 0, 0, 0))],
        out_shape=(jax.ShapeDtypeStruct((b, 2, KC_ROWS, 2 * LANES), F32),
                   jax.ShapeDtypeStruct((b, 2, KC_ROWS, LANES), F32)),
        compiler_params=_cparams(("arbitrary",)),
        name="compress_prompt",
    )(kvc, kvc, kvc, kvc, wk_pairs, wv_pairs, pos_rows, gk)


def _nsa_prompt_kernel(q_ref, gate_ref, kc_ref, vc_ref, ov_ref, ks_ref, vs_ref, kw_ref, vw_ref,
                       tc_ref, ts_ref, tw_ref, o_ref):
    qb = pl.program_id(2)
    t0 = qb * QBLOCK
    rq = 4 * QBLOCK
    q = q_ref[...]
    q4 = jnp.concatenate([q[:, r * LANES:(r + 1) * LANES] for r in range(4)], axis=0)
    lane = lax.broadcasted_iota(jnp.int32, (rq, LANES), 1)
    is_pad_lane = lane == PAD_LANE

    def q_aug(mb):
        return jnp.concatenate([q4, mb.astype(BF16)], axis=1)

    qa_pad = q_aug(jnp.where(is_pad_lane, NEG, 0.0))

    far_mask = ((lane < 32) & (lane >= qb - 2)) | is_pad_lane
    qa_cfar = q_aug(jnp.where(far_mask, NEG, 0.0))
    ncmp = 256
    near0 = pl.multiple_of(qb * 8, 8)
    k_far = kc_ref[KC_FRONT:KC_FRONT + ncmp, :].astype(BF16)
    k_near = kc_ref[pl.ds(near0, LANES), :].astype(BF16)
    s_far = _dot_nt(qa_cfar, k_far)
    s_near = _dot_nt(qa_pad, k_near) + tc_ref[...]
    m = jnp.maximum(jnp.max(s_far, axis=-1, keepdims=True), jnp.max(s_near, axis=-1, keepdims=True))
    p_far = jnp.where(s_far > HALF_NEG, jnp.exp(s_far - m), 0.0)
    p_near = jnp.where(s_near > HALF_NEG, jnp.exp(s_near - m), 0.0)
    l = jnp.sum(p_far, axis=-1, keepdims=True) + jnp.sum(p_near, axis=-1, keepdims=True)
    linv = 1.0 / jnp.where(l > 0.0, l, 1.0)
    v_far = vc_ref[KC_FRONT:KC_FRONT + ncmp, :].astype(BF16)
    v_near = vc_ref[pl.ds(near0, LANES), :].astype(BF16)
    o_c = (_dot(p_far.astype(BF16), v_far) + _dot(p_near.astype(BF16), v_near)) * linv

    pn_far = p_far * linv
    pn_near = p_near * linv
    ps_far = sum(pn_far[r * QBLOCK:(r + 1) * QBLOCK] for r in range(4))
    ps_near = sum(pn_near[r * QBLOCK:(r + 1) * QBLOCK] for r in range(4))
    ov_far = ov_ref[KC_FRONT:KC_FRONT + ncmp, :].astype(BF16)
    ov_near = ov_ref[pl.ds(near0, LANES), :].astype(BF16)
    imp = _dot_hilo(ps_far, ov_far) + _dot_hilo(ps_near, ov_near)
    nblk = 64
    imp_t = imp.T[:nblk]
    blk = lax.broadcasted_iota(jnp.int32, (nblk, QBLOCK), 0)
    qpos = t0 + lax.broadcasted_iota(jnp.int32, (nblk, QBLOCK), 1)
    cur = qpos // SEL_LEN
    forced = (blk == 0) | (blk == cur) | (blk == cur - 1)
    eligible = blk * SEL_LEN <= qpos
    val = jnp.where(forced, FORCE, jnp.where(eligible, imp_t, -1.0))
    rank = jnp.zeros((nblk, QBLOCK), F32)
    for i in range(nblk):
        ri = val[i:i + 1, :]
        beats = (ri > val) | ((ri == val) & (blk > i))
        rank = rank + jnp.where(beats, 1.0, 0.0)
    mb_t = jnp.where(rank < float(N_SEL), 0.0, NEG)
    row2 = lax.broadcasted_iota(jnp.int32, (LANES - nblk, QBLOCK), 0)
    mb_t = jnp.concatenate([mb_t, jnp.where(row2 == PAD_LANE - nblk, NEG, 0.0)], axis=0)
    mb = mb_t.T
    mb4 = jnp.concatenate([mb] * 4, axis=0)
    qa_snear = q_aug(mb4)
    qa_sfar = q_aug(jnp.where((lane < nblk) & (lane >= 2 * qb - 2), NEG, mb4))

    sn0 = pl.multiple_of(t0 + KPAD - QBLOCK, QBLOCK)
    s = _dot_nt(qa_snear, ks_ref[pl.ds(sn0, 2 * QBLOCK), :]) + ts_ref[...]
    m = jnp.max(s, axis=-1, keepdims=True)
    p = jnp.exp(s - m)
    l = jnp.sum(p, axis=-1, keepdims=True)
    acc = _dot(p.astype(BF16), vs_ref[pl.ds(sn0, 2 * QBLOCK), :])
    far_tile = 512
    n_far = (jnp.maximum(qb - 1, 0) * QBLOCK + far_tile - 1) // far_tile

    def far_body(j, carry):
        m, l, acc = carry
        r0 = pl.multiple_of(KPAD + j * far_tile, far_tile)
        s = _dot_nt(qa_sfar, ks_ref[pl.ds(r0, far_tile), :])
        m_new = jnp.maximum(m, jnp.max(s, axis=-1, keepdims=True))
        alpha = jnp.exp(m - m_new)
        p = jnp.exp(s - m_new)
        l = alpha * l + jnp.sum(p, axis=-1, keepdims=True)
        acc = alpha * acc + _dot(p.astype(BF16), vs_ref[pl.ds(r0, far_tile), :])
        return m_new, l, acc

    m, l, acc = lax.fori_loop(0, n_far, far_body, (m, l, acc))
    o_s = acc * (1.0 / l)

    w0 = pl.multiple_of(t0, QBLOCK)
    wlen = WINDOW + QBLOCK
    s = _dot_nt(qa_pad, kw_ref[pl.ds(w0, wlen), :]) + tw_ref[...]
    m = jnp.max(s, axis=-1, keepdims=True)
    p = jnp.exp(s - m)
    l = jnp.sum(p, axis=-1, keepdims=True)
    o_w = _dot(p.astype(BF16), vw_ref[pl.ds(w0, wlen), :]) * (1.0 / l)

    gt = gate_ref[...]
    outs = []
    for r in range(4):
        rs = slice(r * QBLOCK, (r + 1) * QBLOCK)
        outs.append(gt[:, 3 * r:3 * r + 1] * o_c[rs] + gt[:, 3 * r + 1:3 * r + 2] * o_s[rs]
                    + gt[:, 3 * r + 2:3 * r + 3] * o_w[rs])
    o_ref[...] = jnp.concatenate(outs, axis=1).astype(BF16)


def _nsa_prompt(q, gates_g, kc, vc, ov, ks, vs, kw, vw, tab_c, tab_s, tab_w):
    b, t, _ = q.shape
    nqb = t // QBLOCK
    kvspec = lambda a: pl.BlockSpec((None, None) + a.shape[2:], lambda bi, g, i: (bi, g, 0, 0))
    tabspec = lambda a: pl.BlockSpec((None,) + a.shape[1:], lambda bi, g, i: (g, 0, 0))
    return pl.pallas_call(
        _nsa_prompt_kernel,
        grid=(b, 2, nqb),
        in_specs=[pl.BlockSpec((None, QBLOCK, 512), lambda bi, g, i: (bi, i, g)),
                  pl.BlockSpec((None, None, QBLOCK, LANES), lambda bi, g, i: (bi, g, i, 0)),
                  kvspec(kc), kvspec(vc), pl.BlockSpec(ov.shape, lambda bi, g, i: (0, 0)),
                  kvspec(ks), kvspec(vs), kvspec(kw), kvspec(vw),
                  tabspec(tab_c), tabspec(tab_s), tabspec(tab_w)],
        out_specs=pl.BlockSpec((None, QBLOCK, 512), lambda bi, g, i: (bi, i, g)),
        out_shape=jax.ShapeDtypeStruct((b, t, 1024), BF16),
        compiler_params=_cparams(("arbitrary", "arbitrary", "arbitrary")),
        name="nsa_prompt",
    )(q, gates_g, kc, vc, ov, ks, vs, kw, vw, tab_c, tab_s, tab_w)


def _decode_attend(qf, g0rows, gk_row, k_tiles, v_tiles, bias, key_mask, new_row, bias_new):
    qg = qf * gk_row
    zero = jnp.zeros_like(qg)
    qbd = jnp.concatenate([jnp.where(g0rows, qg, zero), jnp.where(g0rows, zero, qg)], axis=1).astype(BF16)
    one = jnp.ones_like(qg)
    ones_bd = jnp.concatenate([jnp.where(g0rows, one, zero), jnp.where(g0rows, zero, one)], axis=1).astype(BF16)
    s_parts, q_parts = [], []
    for kt in k_tiles:
        k = kt()
        s_parts.append(_dot_nt(qbd, k.astype(BF16)))
        q_parts.append(_dot_nt(ones_bd, (k * k).astype(BF16)))
    s = jnp.concatenate(s_parts, axis=1)
    ssq = jnp.concatenate(q_parts, axis=1)
    s = s * lax.rsqrt(ssq * (1.0 / HEAD_DIM) + EPS) + bias
    if key_mask is not None:
        s = jnp.where(key_mask > 0.5, s, NEG)
    bc = lambda lo: jnp.broadcast_to(new_row[:, lo:lo + LANES], qf.shape)
    k_new = jnp.where(g0rows, bc(0), bc(LANES))
    v_new = jnp.where(g0rows, bc(2 * LANES), bc(3 * LANES))
    s_new = jnp.sum(qg * _rms(k_new), axis=-1, keepdims=True) + bias_new
    m = jnp.maximum(jnp.max(s, axis=-1, keepdims=True), s_new)
    p = jnp.exp(s - m)
    p_new = jnp.exp(s_new - m)
    l = jnp.sum(p, axis=-1, keepdims=True) + p_new
    o2 = jnp.zeros((qf.shape[0], 2 * LANES), F32)
    for i, vt in enumerate(v_tiles):
        o2 = o2 + _dot(p[:, i * LANES:(i + 1) * LANES].astype(BF16), vt().astype(BF16))
    o = jnp.where(g0rows, o2[:, :LANES], o2[:, LANES:]) + p_new * v_new
    return o * (1.0 / l)


def _nsa_sample_kernel(pt_ref, *refs, n_pages, n_sel_blocks):
    del pt_ref
    pages_c = [refs[4 * p:4 * p + 4] for p in range(n_pages)]
    pages_s = refs[4 * n_pages:5 * n_pages]
    (win_ref, q_ref, gate_ref, ksn_ref, kwn_ref, wk_ref, wv_ref, pos_ref, gk_ref,
     bc_ref, bs_ref, bw_ref, bnew_ref, ov_ref, e_ref, o_ref) = refs[5 * n_pages:]
    qf = q_ref[...]
    nh = qf.shape[0]
    g0rows = lax.broadcasted_iota(jnp.int32, (nh, LANES), 0) < nh // 2
    lane = lax.broadcasted_iota(jnp.int32, (nh, LANES), 1)

    seg_per_page = pages_c[0][0].shape[0] // CMP_STRIDE
    nseg = n_pages * seg_per_page
    load_x = lambda s, col: jnp.concatenate(
        [pg[col][pl.ds(s, seg_per_page, stride=CMP_STRIDE), :] for pg in pages_c], axis=0)
    ck, cv = _compress_core(load_x, nseg, wk_ref, wv_ref, pos_ref)
    kcn = jnp.concatenate([(_rms(ck[g]) * gk_ref[0:1, :]).astype(BF16) for g in range(2)], axis=0)
    s2 = _dot_nt(qf.astype(BF16), kcn)
    s = jnp.where(g0rows, s2[:, :nseg], s2[:, nseg:]) + bc_ref[...]
    m = jnp.max(s, axis=-1, keepdims=True)
    p = jnp.exp(s - m)
    linv = 1.0 / jnp.sum(p, axis=-1, keepdims=True)
    rowv = lax.broadcasted_iota(jnp.int32, (nseg, LANES), 0) < nseg - 1
    pb = p.astype(BF16)
    oc = [_dot(pb, jnp.where(rowv, cv[g], 0.0).astype(BF16)) for g in range(2)]
    o_c = jnp.where(g0rows, oc[0], oc[1]) * linv

    pn = p * linv
    s0 = jnp.sum(jnp.where(g0rows, pn, 0.0), axis=0, keepdims=True)
    s1 = jnp.sum(jnp.where(g0rows, 0.0, pn), axis=0, keepdims=True)
    psum = jnp.where(g0rows, jnp.broadcast_to(s0, pn.shape), jnp.broadcast_to(s1, pn.shape))
    imp = _dot_hilo(psum, ov_ref[...].astype(BF16))
    cur = n_sel_blocks - 1
    forced = (lane == 0) | (lane == cur) | (lane == cur - 1)
    val = jnp.where(lane >= n_sel_blocks, -2.0, jnp.where(forced, FORCE, imp))
    rank = jnp.zeros_like(val)
    for i in range(n_sel_blocks):
        ci = jnp.broadcast_to(val[:, i:i + 1], val.shape)
        beats = (ci > val) | ((ci == val) & (lane > i))
        rank = rank + jnp.where(beats, 1.0, 0.0)
    sel = jnp.where((rank < float(N_SEL)) & (lane < n_sel_blocks), 1.0, 0.0)
    key_mask = _dot(sel.astype(BF16), e_ref[...])

    def tiles(ref_list, lo):
        def tile(rf, i):
            return lambda: rf[i * LANES:(i + 1) * LANES, lo:lo + 2 * LANES]
        return [tile(rf, i) for rf in ref_list for i in range(rf.shape[0] // LANES)]

    b_new = bnew_ref[:, 0:1]
    o_s = _decode_attend(qf, g0rows, gk_ref[1:2, :], tiles(pages_s, 0), tiles(pages_s, 2 * LANES),
                         bs_ref[...], key_mask, ksn_ref[...], b_new)
    o_w = _decode_attend(qf, g0rows, gk_ref[2:3, :], tiles([win_ref], 0), tiles([win_ref], 2 * LANES),
                         bw_ref[...], None, kwn_ref[...], b_new)
    gt = gate_ref[...]
    o_ref[...] = gt[:, 0:1] * o_c + gt[:, 1:2] * o_s + gt[:, 2:3] * o_w


def _nsa_sample(page_table, cache_c, cache_s, win, q, gates, ks_new, kw_new, wk_pairs, wv_pairs, pos_rows, gk,
                bias_c, bias_s, bias_w, bias_new, ov, expand):
    nb, n_pages = page_table.shape
    page = cache_c.shape[1]
    n_sel_blocks = -(-(n_pages * page + 1) // SEL_LEN)
    kern = functools.partial(_nsa_sample_kernel, n_pages=n_pages, n_sel_blocks=n_sel_blocks)
    page_spec = lambda p: pl.BlockSpec((None, page, 512), lambda bi, pt, p=p: (pt[bi, p], 0, 0))
    col_spec = lambda p, c: pl.BlockSpec((None, page, LANES), lambda bi, pt, p=p, c=c: (pt[bi, p], 0, c))
    full = lambda a: pl.BlockSpec(a.shape, lambda bi, pt: (0,) * a.ndim)
    per_b = lambda a: pl.BlockSpec((None,) + a.shape[1:], lambda bi, pt: (bi,) + (0,) * (a.ndim - 1))
    grid_spec = pltpu.PrefetchScalarGridSpec(
        num_scalar_prefetch=1,
        grid=(nb,),
        in_specs=([col_spec(p, c) for p in range(n_pages) for c in range(4)] + [page_spec(p) for p in range(n_pages)]
                  + [per_b(win), per_b(q), per_b(gates), per_b(ks_new), per_b(kw_new),
                     full(wk_pairs), full(wv_pairs), full(pos_rows), full(gk),
                     full(bias_c), full(bias_s), full(bias_w), full(bias_new), full(ov), full(expand)]),
        out_specs=pl.BlockSpec((None,) + q.shape[1:], lambda bi, pt: (bi, 0, 0)),
    )
    return pl.pallas_call(
        kern,
        grid_spec=grid_spec,
        out_shape=jax.ShapeDtypeStruct(q.shape, F32),
        compiler_params=_cparams(("arbitrary",)),
        name="nsa_sample",
    )(page_table, *([cache_c] * (4 * n_pages)), *([cache_s] * n_pages), win, q, gates, ks_new, kw_new,
      wk_pairs, wv_pairs, pos_rows, gk, bias_c, bias_s, bias_w, bias_new, ov, expand)


def _merge_kernel(h_ref, oa_ref, ob_ref, wga_ref, wgb_ref, wg_ref, wn_ref, mix_ref):
    h = h_ref[...]
    ga = _sigmoid(_dot(h, wga_ref[...]))
    gb = _sigmoid(_dot(h, wgb_ref[...]))
    mix_ref[...] = (ga * _dot(oa_ref[...], wg_ref[...]) + gb * _dot(ob_ref[...], wn_ref[...])).astype(BF16)


def _merge(h, oa, ob, wga, wgb, wg, wn):
    n, d = h.shape
    tm = 512 if n % 512 == 0 else n
    tn = 512
    row = lambda w: pl.BlockSpec((tm, w), lambda i, j: (i, 0))
    col = lambda k: pl.BlockSpec((k, tn), lambda i, j: (0, j))
    return pl.pallas_call(
        _merge_kernel,
        grid=(n // tm, d // tn),
        in_specs=[row(d), row(oa.shape[1]), row(ob.shape[1]), col(d), col(d), col(wg.shape[0]), col(wn.shape[0])],
        out_specs=pl.BlockSpec((tm, tn), lambda i, j: (i, j)),
        out_shape=jax.ShapeDtypeStruct((n, d), BF16),
        compiler_params=_cparams(("arbitrary", "arbitrary")),
        name="merge",
    )(h, oa, ob, wga, wgb, wg, wn)


def _outproj_kernel(x_ref, mix_ref, wout_ref, gmlp_ref, x1_ref, hm_ref):
    x1 = x_ref[...] + _dot(mix_ref[...], wout_ref[...])
    x1_ref[...] = x1
    hm_ref[...] = (_rms(x1) * gmlp_ref[...]).astype(BF16)


def _outproj(x, mix, wout, gmlp):
    n, d = x.shape
    tm = 512 if n % 512 == 0 else n
    row = pl.BlockSpec((tm, d), lambda i: (i, 0))
    return pl.pallas_call(
        _outproj_kernel,
        grid=(n // tm,),
        in_specs=[row, row, pl.BlockSpec(wout.shape, lambda i: (0, 0)), pl.BlockSpec(gmlp.shape, lambda i: (0, 0))],
        out_specs=[row, row],
        out_shape=(jax.ShapeDtypeStruct((n, d), F32), jax.ShapeDtypeStruct((n, d), BF16)),
        compiler_params=_cparams(("arbitrary",)),
        name="outproj",
    )(x, mix, wout, gmlp)


def _ffn_kernel(hm_ref, x1_ref, wup_ref, wdown_ref, y_ref):
    f = pl.program_id(1)

    @pl.when(f == 0)
    def _():
        y_ref[...] = x1_ref[...]

    hid = jnp.maximum(_dot(hm_ref[...], wup_ref[...]), 0.0)
    y_ref[...] += _dot((hid * hid).astype(BF16), wdown_ref[...])


def _ffn(hm, x1, wup, wdown):
    n, d = hm.shape
    dff = wup.shape[1]
    tm = 512 if n % 512 == 0 else n
    tf = 1024
    row = pl.BlockSpec((tm, d), lambda i, f: (i, 0))
    return pl.pallas_call(
        _ffn_kernel,
        grid=(n // tm, dff // tf),
        in_specs=[row, row, pl.BlockSpec((d, tf), lambda i, f: (0, f)), pl.BlockSpec((tf, d), lambda i, f: (f, 0))],
        out_specs=row,
        out_shape=jax.ShapeDtypeStruct((n, d), F32),
        compiler_params=_cparams(("arbitrary", "arbitrary")),
        name="ffn",
    )(hm, x1, wup, wdown)


def _bucket(rel):
    n = jnp.maximum(rel, 0)
    max_exact = N_BUCKETS // 2
    nf = jnp.maximum(n, 1).astype(F32)
    large = max_exact + (jnp.log(nf / max_exact) / math.log(MAX_DIST / max_exact)
                         * (N_BUCKETS - max_exact)).astype(jnp.int32)
    large = jnp.minimum(large, N_BUCKETS - 1)
    return jnp.where(n < max_exact, n, large)


def _bias_table(rel_bias, rel, valid, shift):
    nh = rel_bias.shape[1]
    tb = rel_bias[_bucket(rel)]
    tb = jnp.transpose(tb, (2, 0, 1)) - shift[:, None, None]
    tb = jnp.where(valid[None], tb, NEG)
    rows, cols = rel.shape
    return tb.reshape(2, (nh // 2) * rows, cols)


def _overlap(nc, ns):
    i = jnp.arange(nc)[:, None] * CMP_STRIDE
    j = jnp.arange(ns)[None, :] * SEL_LEN
    return ((i < j + SEL_LEN) & (i + CMP_LEN > j)).astype(F32)


def _compress_weights(w, pos):
    s = np.arange(0, CMP_STRIDE, 2)
    top = jnp.concatenate([w[s], w[CMP_STRIDE + s]], axis=2)
    bot = jnp.concatenate([w[s + 1], w[CMP_STRIDE + s + 1]], axis=2)
    tiles = jnp.concatenate([top, bot], axis=1).astype(BF16)
    row_a = jnp.concatenate([pos[s], pos[s + 1]], axis=1)
    row_b = jnp.concatenate([pos[CMP_STRIDE + s], pos[CMP_STRIDE + s + 1]], axis=1)
    rows = jnp.zeros((len(s), 16, 2 * HEAD_DIM), F32).at[:, 0].set(row_a).at[:, 1].set(row_b)
    return tiles, rows.astype(BF16)


def _dense_tail(x, h, oa, ob, wga, wgb, wg, wn, wout, gmlp, wup, wdown):
    mix = _merge(h, oa, ob, wga, wgb, wg, wn)
    x1, hm = _outproj(x, mix, wout, gmlp)
    return _ffn(hm, x1, wup, wdown)


def kernel(x_prompt, x_sample, cache_cmp_kv, cache_sel_kv, state_win_kv, page_table, rel_bias, g_mix_norm, w_in,
           g_sgu, w_sgu, b_sgu, g_q, g_k, pos_cmp_k, w_cmp_k, pos_cmp_v, w_cmp_v, w_proj_gmlp, w_proj_nsa, w_out,
           g_mlp_norm, w_up, w_down):
    depth = g_mix_norm.shape[0]
    assert depth == 1
    l = 0
    bsz, seq, d = x_prompt.shape
    nb = x_sample.shape[0]
    assert x_sample.shape[1] == 1 and seq % KPAD == 0
    d_gm = g_sgu.shape[1]
    n_heads = rel_bias.shape[1]
    d_nsa = n_heads * HEAD_DIM
    kvw_cols = 2 * (n_heads // 4) * HEAD_DIM
    n_gate = 3 * n_heads
    c_q = 2 * d_gm
    c_kv = c_q + d_nsa
    c_gate = c_kv + 3 * kvw_cols
    c_ga = c_gate + n_gate
    c_gb = c_ga + d

    w = w_in[l]
    w_cat = jnp.pad(w[:, :c_ga], ((0, 0), (0, 5 * 1024 - c_ga))).astype(BF16)
    wga = w[:, c_ga:c_gb].astype(BF16)
    wgb = w[:, c_gb:].astype(BF16)
    wg = w_proj_gmlp[l].astype(BF16)
    wn = w_proj_nsa[l].astype(BF16)
    wout = w_out[l].astype(BF16)
    wup = w_up[l].astype(BF16)
    wdown = w_down[l].astype(BF16)
    gmix = g_mix_norm[l][None]
    gsgu = g_sgu[l][None]
    gq = g_q[l][None]
    gk = g_k[l]
    gmlp = g_mlp_norm[l][None]
    wk_pairs, posk_rows = _compress_weights(w_cmp_k[l], pos_cmp_k[l])
    wv_pairs, posv_rows = _compress_weights(w_cmp_v[l], pos_cmp_v[l])
    pos_rows = jnp.stack([posk_rows, posv_rows])
    n_groups = w_sgu.shape[1]
    b_exp = jnp.repeat(b_sgu[l].T, d_gm // n_groups, axis=1)
    w00 = jnp.repeat(w_sgu[l][:, 0, 0], d_gm // n_groups)[None]
    b00 = jnp.repeat(b_sgu[l][:, 0], d_gm // n_groups)[None]

    xp = x_prompt.reshape(bsz * seq, d)
    h_p, v_p, oa_p, q_p, kvc_p, kvs_p, kvw_p, gates_p = _inproj(
        xp, gmix, w_cat, gsgu, gq, w_sgu[l], b_exp, chunked=True)
    kvs3 = kvs_p.reshape(bsz, seq, kvw_cols)
    kvw3 = kvw_p.reshape(bsz, seq, kvw_cols)
    kvc3 = kvc_p.reshape(bsz, seq, kvw_cols)
    ks, vs, kw, vw = _kvprep(kvs3, kvw3, gk)
    kc, vc = _compress_prompt(kvc3, wk_pairs, wv_pairs, pos_rows, gk)

    shift = rel_bias[N_BUCKETS - 1]
    qi = jnp.arange(QBLOCK)[:, None]
    rel = qi + QBLOCK - jnp.arange(2 * QBLOCK)[None, :]
    tab_s = _bias_table(rel_bias, rel, rel >= 0, shift)
    rel = qi + WINDOW - jnp.arange(WINDOW + QBLOCK)[None, :]
    tab_w = _bias_table(rel_bias, rel, (rel >= 0) & (rel < WINDOW), jnp.zeros_like(shift))
    rel = qi + (KC_FRONT * CMP_STRIDE - CMP_LEN + 1) - CMP_STRIDE * jnp.arange(LANES)[None, :]
    tab_c = _bias_table(rel_bias, rel, rel >= 0, shift)
    n_cmp = (seq - CMP_LEN) // CMP_STRIDE + 1
    n_selb = seq // SEL_LEN
    ov_p = jnp.zeros((KC_ROWS, LANES), F32).at[KC_FRONT:KC_FRONT + n_cmp, :n_selb].set(_overlap(n_cmp, n_selb))
    gates_g = gates_p[:, :n_gate].reshape(bsz, seq, 2, n_gate // 2).transpose(0, 2, 1, 3)
    gates_g = jnp.pad(gates_g, ((0, 0), (0, 0), (0, 0), (0, LANES - n_gate // 2)))
    ob_p = _nsa_prompt(q_p.reshape(bsz, seq, d_nsa), gates_g, kc, vc, ov_p, ks, vs, kw, vw, tab_c, tab_s, tab_w)
    y_p = _dense_tail(xp, h_p, oa_p, ob_p.reshape(bsz * seq, d_nsa), wga, wgb, wg, wn, wout, gmlp, wup, wdown)

    xs = x_sample.reshape(nb, d)
    h_s, v_s, oa_s, q_s, kvc_s, kvs_s, kvw_s, gates_s = _inproj(
        xs, gmix, w_cat, gsgu, gq, w00, b00, chunked=False)
    n_pool, page = cache_cmp_kv.shape[1], cache_cmp_kv.shape[2]
    n_pages = page_table.shape[1]
    past = n_pages * page
    nwin = state_win_kv.shape[2]
    n_cmp_s = (past + 1 - CMP_LEN) // CMP_STRIDE + 1
    n_sel_s = -(-(past + 1) // SEL_LEN)
    nseg_s = past // CMP_STRIDE
    hsel = lambda tb: jnp.transpose(tb, (1, 0))
    rel_c = past - (jnp.arange(nseg_s) * CMP_STRIDE + CMP_LEN - 1)
    bias_c = jnp.where(((jnp.arange(nseg_s) < n_cmp_s) & (rel_c >= 0))[None], hsel(rel_bias[_bucket(rel_c)]), NEG)
    bias_s = hsel(rel_bias[_bucket(past - jnp.arange(past))])
    rel_w = past - (past - nwin + jnp.arange(nwin))
    bias_w = jnp.where((rel_w < WINDOW)[None], hsel(rel_bias[_bucket(rel_w)]), NEG)
    bias_new = jnp.broadcast_to(rel_bias[0][:, None], (n_heads, LANES))
    ov_s = jnp.zeros((nseg_s, LANES), F32).at[:n_cmp_s, :n_sel_s].set(_overlap(n_cmp_s, n_sel_s))
    expand = (jnp.arange(LANES)[:, None] == (jnp.arange(past)[None, :] // SEL_LEN)).astype(BF16)
    gates_h = jnp.pad(gates_s[:, :n_gate].reshape(nb, n_heads, 3), ((0, 0), (0, 0), (0, LANES - 3)))
    ob_s = _nsa_sample(
        page_table, cache_cmp_kv[l].reshape(n_pool, page, kvw_cols), cache_sel_kv[l].reshape(n_pool, page, kvw_cols),
        state_win_kv[l].reshape(nb, nwin, kvw_cols), q_s.astype(F32).reshape(nb, n_heads, HEAD_DIM), gates_h,
        kvs_s[:, None, :], kvw_s[:, None, :], wk_pairs, wv_pairs, pos_rows, gk,
        bias_c, bias_s, bias_w, bias_new, ov_s, expand)
    y_s = _dense_tail(xs, h_s, oa_s, ob_s.reshape(nb, d_nsa).astype(BF16), wga, wgb, wg, wn, wout, gmlp, wup, wdown)

    n_kv = n_heads // 4
    kv6 = lambda a, b_, t_: a.reshape(1, b_, t_, 2, n_kv, HEAD_DIM)
    nw_p = min(WINDOW, seq)
    last = ((seq - 1) // CHUNK) * CHUNK
    win_new = jnp.concatenate([state_win_kv[l].reshape(nb, nwin, kvw_cols), kvw_s[:, None, :]], axis=1)
    nw_s = min(WINDOW, past + 1)
    win_new = win_new[:, nwin + 1 - nw_s:]
    return (y_p.reshape(bsz, seq, d), y_s.reshape(nb, 1, d),
            kv6(kvc3, bsz, seq), kv6(kvs3, bsz, seq), kv6(kvw3[:, seq - nw_p:], bsz, nw_p),
            v_p.reshape(bsz, seq, d_gm)[:, last:][None],
            kv6(kvc_s, nb, 1), kv6(kvs_s, nb, 1), kv6(win_new, nb, nw_s),
            v_s.reshape(1, nb, 1, d_gm))
```

```python
import functools
import math

import numpy as np
import jax
import jax.numpy as jnp
from jax import lax
from jax.experimental import pallas as pl
from jax.experimental.pallas import tpu as pltpu

F32 = jnp.float32
BF16 = jnp.bfloat16

HEAD_DIM = 128
CHUNK = 128
CMP_LEN = 32
CMP_STRIDE = 16
SEL_LEN = 64
N_SEL = 16
WINDOW = 512
N_BUCKETS = 32
MAX_DIST = 128
QBLOCK = 128
EPS = 1e-6
NEG = -1e30
HALF_NEG = -5e29
FORCE = 1e6

LANES = 128
PAD_LANE = 64
KPAD = 512
KC_FRONT = 16
KC_ROWS = 376
VMEM_LIMIT = 56 * 1024 * 1024


def _cparams(sem):
    return pltpu.CompilerParams(dimension_semantics=sem, vmem_limit_bytes=VMEM_LIMIT)


def _dot(a, b):
    return jnp.dot(a, b, preferred_element_type=F32)


def _dot_nt(a, b):
    return lax.dot_general(a, b, (((1,), (1,)), ((), ())), preferred_element_type=F32)


def _dot_hilo(a, b_bf16):
    hi = a.astype(BF16)
    lo = (a - hi.astype(F32)).astype(BF16)
    return _dot(hi, b_bf16) + _dot(lo, b_bf16)


def _rms(x):
    return x * lax.rsqrt(jnp.mean(x * x, axis=-1, keepdims=True) + EPS)


def _gelu(x):
    c = math.sqrt(2.0 / math.pi)
    return 0.5 * x * (1.0 + jnp.tanh(c * (x + 0.044715 * (x * x * x))))


def _sigmoid(x):
    return 1.0 / (1.0 + jnp.exp(-x))


def _inproj_kernel(x_ref, gmix_ref, w_ref, gsgu_ref, gq_ref, wsg_ref, bsg_ref,
                   h_ref, v_ref, oa_ref, q_ref, kvc_ref, kvs_ref, kvw_ref, gate_ref,
                   h_s, u_s, *, chunked, tm):
    j = pl.program_id(1)

    @pl.when(j == 0)
    def _():
        hb = (_rms(x_ref[...]) * gmix_ref[...]).astype(BF16)
        h_s[...] = hb
        h_ref[...] = hb

    z = _dot(h_s[...], w_ref[...])

    @pl.when(j == 0)
    def _():
        u_s[...] = _gelu(z)

    @pl.when(j == 1)
    def _():
        v = _rms(_gelu(z)) * gsgu_ref[...]
        v_ref[...] = v
        if chunked:
            row = lax.broadcasted_iota(jnp.int32, (CHUNK, CHUNK), 0)
            col = lax.broadcasted_iota(jnp.int32, (CHUNK, CHUNK), 1)
            n_groups = v.shape[1] // LANES
            for g in range(n_groups):
                wm = jnp.where(row >= col, wsg_ref[g], 0.0).astype(BF16)
                cs = slice(g * LANES, (g + 1) * LANES)
                for c in range(tm // CHUNK):
                    rs = slice(c * CHUNK, (c + 1) * CHUNK)
                    s = _dot(wm, v[rs, cs].astype(BF16)) + bsg_ref[:, cs]
                    oa_ref[rs, cs] = (u_s[rs, cs] * s).astype(BF16)
        else:
            oa_ref[...] = (u_s[...] * (v * wsg_ref[...] + bsg_ref[...])).astype(BF16)

    @pl.when(j == 2)
    def _():
        scale = HEAD_DIM ** -0.5
        for hd in range(z.shape[1] // HEAD_DIM):
            cs = slice(hd * HEAD_DIM, (hd + 1) * HEAD_DIM)
            q_ref[:, cs] = (_rms(z[:, cs]) * gq_ref[...] * scale).astype(BF16)

    @pl.when(j == 3)
    def _():
        kvc_ref[...] = z[:, :512]
        kvs_ref[...] = z[:, 512:]

    @pl.when(j == 4)
    def _():
        kvw_ref[...] = z[:, :512]
        gate_ref[...] = _sigmoid(z[:, 512:640])


def _inproj(x, gmix, w_cat, gsgu, gq, wsg, bsg, *, chunked):
    n, d = x.shape
    tm = 512 if n % 512 == 0 else n
    tn = 1024
    kern = functools.partial(_inproj_kernel, chunked=chunked, tm=tm)
    full = lambda a: pl.BlockSpec(a.shape, lambda i, j: (0,) * a.ndim)
    row = lambda w: pl.BlockSpec((tm, w), lambda i, j: (i, 0))
    out_shapes = (
        jax.ShapeDtypeStruct((n, d), BF16),
        jax.ShapeDtypeStruct((n, 1024), F32),
        jax.ShapeDtypeStruct((n, 1024), BF16),
        jax.ShapeDtypeStruct((n, 1024), BF16),
        jax.ShapeDtypeStruct((n, 512), F32),
        jax.ShapeDtypeStruct((n, 512), F32),
        jax.ShapeDtypeStruct((n, 512), F32),
        jax.ShapeDtypeStruct((n, LANES), F32),
    )
    return pl.pallas_call(
        kern,
        grid=(n // tm, w_cat.shape[1] // tn),
        in_specs=[row(d), full(gmix), pl.BlockSpec((d, tn), lambda i, j: (0, j)),
                  full(gsgu), full(gq), full(wsg), full(bsg)],
        out_specs=[row(d), row(1024), row(1024), row(1024), row(512), row(512), row(512), row(LANES)],
        out_shape=out_shapes,
        scratch_shapes=[pltpu.VMEM((tm, d), BF16), pltpu.VMEM((tm, 1024), F32)],
        compiler_params=_cparams(("arbitrary", "arbitrary")),
        name="inproj",
    )(x, gmix, w_cat, gsgu, gq, wsg, bsg)


def _kvprep_kernel(kvs_ref, kvw_ref, gk_ref, ks_ref, vs_ref, kw_ref, vw_ref):
    i = pl.program_id(1)
    rows = kvs_ref.shape[0]
    lane = lax.broadcasted_iota(jnp.int32, (rows, LANES), 1)
    row = lax.broadcasted_iota(jnp.int32, (rows, LANES), 0)

    @pl.when(i == 0)
    def _():
        aux = jnp.where(lane == PAD_LANE, 1.0, 0.0).astype(BF16)
        zk = jnp.zeros((rows, LANES), BF16)
        for g in range(2):
            ks_ref[g] = jnp.concatenate([zk, aux], axis=1)
            kw_ref[g] = jnp.concatenate([zk, aux], axis=1)
            vs_ref[g] = zk
            vw_ref[g] = zk

    @pl.when(i > 0)
    def _():
        blk = ((i - 1) * rows + row) // SEL_LEN
        onehot = jnp.where(lane == blk, 1.0, 0.0).astype(BF16)
        zaux = jnp.zeros((rows, LANES), BF16)
        for g in range(2):
            cs = slice(g * LANES, (g + 1) * LANES)
            vcs = slice(256 + g * LANES, 256 + (g + 1) * LANES)
            ks = (_rms(kvs_ref[:, cs]) * gk_ref[1:2, :]).astype(BF16)
            kw = (_rms(kvw_ref[:, cs]) * gk_ref[2:3, :]).astype(BF16)
            ks_ref[g] = jnp.concatenate([ks, onehot], axis=1)
            kw_ref[g] = jnp.concatenate([kw, zaux], axis=1)
            vs_ref[g] = kvs_ref[:, vcs].astype(BF16)
            vw_ref[g] = kvw_ref[:, vcs].astype(BF16)


def _kvprep(kvs, kvw, gk):
    b, t, _ = kvs.shape
    rows = KPAD
    nblk = t // rows
    in_map = lambda bi, i: (bi, jnp.maximum(i - 1, 0), 0)
    out_map = lambda bi, i: (bi, 0, i, 0)
    kshape = jax.ShapeDtypeStruct((b, 2, KPAD + t, 2 * LANES), BF16)
    vshape = jax.ShapeDtypeStruct((b, 2, KPAD + t, LANES), BF16)
    return pl.pallas_call(
        _kvprep_kernel,
        grid=(b, nblk + 1),
        in_specs=[pl.BlockSpec((None, rows, 512), in_map), pl.BlockSpec((None, rows, 512), in_map),
                  pl.BlockSpec(gk.shape, lambda bi, i: (0, 0))],
        out_specs=[pl.BlockSpec((None, 2, rows, 2 * LANES), out_map), pl.BlockSpec((None, 2, rows, LANES), out_map),
                   pl.BlockSpec((None, 2, rows, 2 * LANES), out_map), pl.BlockSpec((None, 2, rows, LANES), out_map)],
        out_shape=(kshape, vshape, kshape, vshape),
        compiler_params=_cparams(("arbitrary", "arbitrary")),
        name="kvprep",
    )(kvs, kvw, gk)


def _compress_core(load_x, nseg, wk_ref, wv_ref, pos_ref):
    w_refs = (wk_ref, wv_ref)
    accs = [jnp.zeros((2 * nseg + 16, 2 * LANES), F32) for _ in range(2)]
    for s2 in range(CMP_STRIDE // 2):
        for kv in range(2):
            parts = []
            for g in range(2):
                col = 2 * kv + g
                parts.append(jnp.concatenate([load_x(2 * s2, col), load_x(2 * s2 + 1, col)], axis=1).astype(BF16))
            parts.append(pos_ref[kv, s2])
            lhs = jnp.concatenate(parts, axis=0)
            accs[kv] = accs[kv] + _dot(lhs, w_refs[kv][s2])
    outs = []
    for kv in range(2):
        y = accs[kv]
        post = y[2 * nseg:2 * nseg + 1, :LANES] + y[2 * nseg + 1:2 * nseg + 2, LANES:]
        per_g = []
        for g in range(2):
            y0 = y[g * nseg:(g + 1) * nseg, :LANES]
            y1 = y[g * nseg:(g + 1) * nseg, LANES:]
            per_g.append(y0 + pltpu.roll(y1, nseg - 1, 0) + post)
        outs.append(per_g)
    return outs


def _compress_prompt_kernel(x0_ref, x1_ref, x2_ref, x3_ref, wk_ref, wv_ref, pos_ref, gk_ref, kc_ref, vc_ref):
    x_refs = (x0_ref, x1_ref, x2_ref, x3_ref)
    nseg = x0_ref.shape[0] // CMP_STRIDE
    load_x = lambda s, col: x_refs[col][pl.ds(s, nseg, stride=CMP_STRIDE), :]
    ck, cv = _compress_core(load_x, nseg, wk_ref, wv_ref, pos_ref)
    row = lax.broadcasted_iota(jnp.int32, (nseg, LANES), 0)
    lane = lax.broadcasted_iota(jnp.int32, (nseg, LANES), 1)
    valid = row < nseg - 1
    aux = jnp.where(valid, jnp.where(lane == row // 8, 1.0, 0.0), jnp.where(lane == PAD_LANE, 1.0, 0.0))
    back = KC_ROWS - KC_FRONT - nseg
    lane_f = lax.broadcasted_iota(jnp.int32, (KC_FRONT, LANES), 1)
    lane_b = lax.broadcasted_iota(jnp.int32, (back, LANES), 1)
    pad_f = jnp.concatenate([jnp.zeros((KC_FRONT, LANES), F32), jnp.where(lane_f == PAD_LANE, 1.0, 0.0)], axis=1)
    pad_b = jnp.concatenate([jnp.zeros((back, LANES), F32), jnp.where(lane_b == PAD_LANE, 1.0, 0.0)], axis=1)
    for g in range(2):
        kn = jnp.where(valid, _rms(ck[g]) * gk_ref[0:1, :], 0.0)
        kc_ref[g, 0:KC_FRONT, :] = pad_f
        kc_ref[g, KC_FRONT:KC_FRONT + nseg, :] = jnp.concatenate([kn, aux], axis=1)
        kc_ref[g, KC_FRONT + nseg:KC_ROWS, :] = pad_b
        vc_ref[g, 0:KC_FRONT, :] = jnp.zeros((KC_FRONT, LANES), F32)
        vc_ref[g, KC_FRONT:KC_FRONT + nseg, :] = jnp.where(valid, cv[g], 0.0)
        vc_ref[g, KC_FRONT + nseg:KC_ROWS, :] = jnp.zeros((back, LANES), F32)


def _compress_prompt(kvc, wk_pairs, wv_pairs, pos_rows, gk):
    b, t, _ = kvc.shape
    full = lambda a: pl.BlockSpec(a.shape, lambda bi: (0,) * a.ndim)
    return pl.pallas_call(
        _compress_prompt_kernel,
        grid=(b,),
        in_specs=[pl.BlockSpec((None, t, LANES), lambda bi, c=c: (bi, 0, c)) for c in range(4)]
        + [full(wk_pairs), full(wv_pairs), full(pos_rows), full(gk)],
        out_specs=[pl.BlockSpec((None, 2, KC_ROWS, 2 * LANES), lambda bi: (bi, 0, 0, 0)),
                   pl.BlockSpec((None, 2, KC_ROWS, LANES), lambda bi: (bi, 0, 0, 0))],
        out_shape=(jax.ShapeDtypeStruct((b, 2, KC_ROWS, 2 * LANES), F32),
                   jax.ShapeDtypeStruct((b, 2, KC_ROWS, LANES), F32)),
        compiler_params=_cparams(("arbitrary",)),
        name="compress_prompt",
    )(kvc, kvc, kvc, kvc, wk_pairs, wv_pairs, pos_rows, gk)


def _nsa_prompt_kernel(q_ref, gate_ref, kc_ref, vc_ref, ov_ref, ks_ref, vs_ref, kw_ref, vw_ref,
                       tc_ref, ts_ref, tw_ref, o_ref):
    qb = pl.program_id(2)
    t0 = qb * QBLOCK
    rq = 4 * QBLOCK
    q = q_ref[...]
    q4 = jnp.concatenate([q[:, r * LANES:(r + 1) * LANES] for r in range(4)], axis=0)
    lane = lax.broadcasted_iota(jnp.int32, (rq, LANES), 1)
    is_pad_lane = lane == PAD_LANE

    def q_aug(mb):
        return jnp.concatenate([q4, mb.astype(BF16)], axis=1)

    qa_pad = q_aug(jnp.where(is_pad_lane, NEG, 0.0))

    far_mask = ((lane < 32) & (lane >= qb - 2)) | is_pad_lane
    qa_cfar = q_aug(jnp.where(far_mask, NEG, 0.0))
    ncmp = 256
    near0 = pl.multiple_of(qb * 8, 8)
    k_far = kc_ref[KC_FRONT:KC_FRONT + ncmp, :].astype(BF16)
    k_near = kc_ref[pl.ds(near0, LANES), :].astype(BF16)
    s_far = _dot_nt(qa_cfar, k_far)
    s_near = _dot_nt(qa_pad, k_near) + tc_ref[...]
    m = jnp.maximum(jnp.max(s_far, axis=-1, keepdims=True), jnp.max(s_near, axis=-1, keepdims=True))
    p_far = jnp.where(s_far > HALF_NEG, jnp.exp(s_far - m), 0.0)
    p_near = jnp.where(s_near > HALF_NEG, jnp.exp(s_near - m), 0.0)
    l = jnp.sum(p_far, axis=-1, keepdims=True) + jnp.sum(p_near, axis=-1, keepdims=True)
    linv = 1.0 / jnp.where(l > 0.0, l, 1.0)
    v_far = vc_ref[KC_FRONT:KC_FRONT + ncmp, :].astype(BF16)
    v_near = vc_ref[pl.ds(near0, LANES), :].astype(BF16)
    o_c = (_dot(p_far.astype(BF16), v_far) + _dot(p_near.astype(BF16), v_near)) * linv

    pn_far = p_far * linv
    pn_near = p_near * linv
    ps_far = sum(pn_far[r * QBLOCK:(r + 1) * QBLOCK] for r in range(4))
    ps_near = sum(pn_near[r * QBLOCK:(r + 1) * QBLOCK] for r in range(4))
    ov_far = ov_ref[KC_FRONT:KC_FRONT + ncmp, :].astype(BF16)
    ov_near = ov_ref[pl.ds(near0, LANES), :].astype(BF16)
    imp = _dot_hilo(ps_far, ov_far) + _dot_hilo(ps_near, ov_near)
    nblk = 64
    imp_t = imp.T[:nblk]
    blk = lax.broadcasted_iota(jnp.int32, (nblk, QBLOCK), 0)
    qpos = t0 + lax.broadcasted_iota(jnp.int32, (nblk, QBLOCK), 1)
    cur = qpos // SEL_LEN
    forced = (blk == 0) | (blk == cur) | (blk == cur - 1)
    eligible = blk * SEL_LEN <= qpos
    val = jnp.where(forced, FORCE, jnp.where(eligible, imp_t, -1.0))
    rank = jnp.zeros((nblk, QBLOCK), F32)
    for i in range(nblk):
        ri = val[i:i + 1, :]
        beats = (ri > val) | ((ri == val) & (blk > i))
        rank = rank + jnp.where(beats, 1.0, 0.0)
    mb_t = jnp.where(rank < float(N_SEL), 0.0, NEG)
    row2 = lax.broadcasted_iota(jnp.int32, (LANES - nblk, QBLOCK), 0)
    mb_t = jnp.concatenate([mb_t, jnp.where(row2 == PAD_LANE - nblk, NEG, 0.0)], axis=0)
    mb = mb_t.T
    mb4 = jnp.concatenate([mb] * 4, axis=0)
    qa_snear = q_aug(mb4)
    qa_sfar = q_aug(jnp.where((lane < nblk) & (lane >= 2 * qb - 2), NEG, mb4))

    sn0 = pl.multiple_of(t0 + KPAD - QBLOCK, QBLOCK)
    s = _dot_nt(qa_snear, ks_ref[pl.ds(sn0, 2 * QBLOCK), :]) + ts_ref[...]
    m = jnp.max(s, axis=-1, keepdims=True)
    p = jnp.exp(s - m)
    l = jnp.sum(p, axis=-1, keepdims=True)
    acc = _dot(p.astype(BF16), vs_ref[pl.ds(sn0, 2 * QBLOCK), :])
    far_tile = 512
    n_far = (jnp.maximum(qb - 1, 0) * QBLOCK + far_tile - 1) // far_tile

    def far_body(j, carry):
        m, l, acc = carry
        r0 = pl.multiple_of(KPAD + j * far_tile, far_tile)
        s = _dot_nt(qa_sfar, ks_ref[pl.ds(r0, far_tile), :])
        m_new = jnp.maximum(m, jnp.max(s, axis=-1, keepdims=True))
        alpha = jnp.exp(m - m_new)
        p = jnp.exp(s - m_new)
        l = alpha * l + jnp.sum(p, axis=-1, keepdims=True)
        acc = alpha * acc + _dot(p.astype(BF16), vs_ref[pl.ds(r0, far_tile), :])
        return m_new, l, acc

    m, l, acc = lax.fori_loop(0, n_far, far_body, (m, l, acc))
    o_s = acc * (1.0 / l)

    w0 = pl.multiple_of(t0, QBLOCK)
    wlen = WINDOW + QBLOCK
    s = _dot_nt(qa_pad, kw_ref[pl.ds(w0, wlen), :]) + tw_ref[...]
    m = jnp.max(s, axis=-1, keepdims=True)
    p = jnp.exp(s - m)
    l = jnp.sum(p, axis=-1, keepdims=True)
    o_w = _dot(p.astype(BF16), vw_ref[pl.ds(w0, wlen), :]) * (1.0 / l)

    gt = gate_ref[...]
    outs = []
    for r in range(4):
        rs = slice(r * QBLOCK, (r + 1) * QBLOCK)
        outs.append(gt[:, 3 * r:3 * r + 1] * o_c[rs] + gt[:, 3 * r + 1:3 * r + 2] * o_s[rs]
                    + gt[:, 3 * r + 2:3 * r + 3] * o_w[rs])
    o_ref[...] = jnp.concatenate(outs, axis=1).astype(BF16)


def _nsa_prompt(q, gates_g, kc, vc, ov, ks, vs, kw, vw, tab_c, tab_s, tab_w):
    b, t, _ = q.shape
    nqb = t // QBLOCK
    kvspec = lambda a: pl.BlockSpec((None, None) + a.shape[2:], lambda bi, g, i: (bi, g, 0, 0))
    tabspec = lambda a: pl.BlockSpec((None,) + a.shape[1:], lambda bi, g, i: (g, 0, 0))
    return pl.pallas_call(
        _nsa_prompt_kernel,
        grid=(b, 2, nqb),
        in_specs=[pl.BlockSpec((None, QBLOCK, 512), lambda bi, g, i: (bi, i, g)),
                  pl.BlockSpec((None, None, QBLOCK, LANES), lambda bi, g, i: (bi, g, i, 0)),
                  kvspec(kc), kvspec(vc), pl.BlockSpec(ov.shape, lambda bi, g, i: (0, 0)),
                  kvspec(ks), kvspec(vs), kvspec(kw), kvspec(vw),
                  tabspec(tab_c), tabspec(tab_s), tabspec(tab_w)],
        out_specs=pl.BlockSpec((None, QBLOCK, 512), lambda bi, g, i: (bi, i, g)),
        out_shape=jax.ShapeDtypeStruct((b, t, 1024), BF16),
        compiler_params=_cparams(("arbitrary", "arbitrary", "arbitrary")),
        name="nsa_prompt",
    )(q, gates_g, kc, vc, ov, ks, vs, kw, vw, tab_c, tab_s, tab_w)


def _decode_attend(qf, g0rows, gk_row, k_tiles, v_tiles, bias, key_mask, new_row, bias_new):
    qg = qf * gk_row
    zero = jnp.zeros_like(qg)
    qbd = jnp.concatenate([jnp.where(g0rows, qg, zero), jnp.where(g0rows, zero, qg)], axis=1).astype(BF16)
    one = jnp.ones_like(qg)
    ones_bd = jnp.concatenate([jnp.where(g0rows, one, zero), jnp.where(g0rows, zero, one)], axis=1).astype(BF16)
    s_parts, q_parts = [], []
    for kt in k_tiles:
        k = kt()
        s_parts.append(_dot_nt(qbd, k.astype(BF16)))
        q_parts.append(_dot_nt(ones_bd, (k * k).astype(BF16)))
    s = jnp.concatenate(s_parts, axis=1)
    ssq = jnp.concatenate(q_parts, axis=1)
    s = s * lax.rsqrt(ssq * (1.0 / HEAD_DIM) + EPS) + bias
    if key_mask is not None:
        s = jnp.where(key_mask > 0.5, s, NEG)
    bc = lambda lo: jnp.broadcast_to(new_row[:, lo:lo + LANES], qf.shape)
    k_new = jnp.where(g0rows, bc(0), bc(LANES))
    v_new = jnp.where(g0rows, bc(2 * LANES), bc(3 * LANES))
    s_new = jnp.sum(qg * _rms(k_new), axis=-1, keepdims=True) + bias_new
    m = jnp.maximum(jnp.max(s, axis=-1, keepdims=True), s_new)
    p = jnp.exp(s - m)
    p_new = jnp.exp(s_new - m)
    l = jnp.sum(p, axis=-1, keepdims=True) + p_new
    o2 = jnp.zeros((qf.shape[0], 2 * LANES), F32)
    for i, vt in enumerate(v_tiles):
        o2 = o2 + _dot(p[:, i * LANES:(i + 1) * LANES].astype(BF16), vt().astype(BF16))
    o = jnp.where(g0rows, o2[:, :LANES], o2[:, LANES:]) + p_new * v_new
    return o * (1.0 / l)


def _nsa_sample_kernel(pt_ref, *refs, n_pages, n_sel_blocks):
    del pt_ref
    pages_c = refs[:n_pages]
    pages_s = refs[n_pages:2 * n_pages]
    (win_ref, q_ref, gate_ref, ksn_ref, kwn_ref, wk_ref, wv_ref, pos_ref, gk_ref,
     bc_ref, bs_ref, bw_ref, bnew_ref, ov_ref, e_ref, o_ref) = refs[2 * n_pages:]
    qf = q_ref[...]
    nh = qf.shape[0]
    g0rows = lax.broadcasted_iota(jnp.int32, (nh, LANES), 0) < nh // 2
    lane = lax.broadcasted_iota(jnp.int32, (nh, LANES), 1)

    seg_per_page = pages_c[0].shape[0] // (4 * CMP_STRIDE)
    nseg = n_pages * seg_per_page
    load_x = lambda s, col: jnp.concatenate(
        [pg[pl.ds(4 * s + col, seg_per_page, stride=4 * CMP_STRIDE), :] for pg in pages_c], axis=0)
    ck, cv = _compress_core(load_x, nseg, wk_ref, wv_ref, pos_ref)
    kcn = jnp.concatenate([(_rms(ck[g]) * gk_ref[0:1, :]).astype(BF16) for g in range(2)], axis=0)
    s2 = _dot_nt(qf.astype(BF16), kcn)
    s = jnp.where(g0rows, s2[:, :nseg], s2[:, nseg:]) + bc_ref[...]
    m = jnp.max(s, axis=-1, keepdims=True)
    p = jnp.exp(s - m)
    linv = 1.0 / jnp.sum(p, axis=-1, keepdims=True)
    rowv = lax.broadcasted_iota(jnp.int32, (nseg, LANES), 0) < nseg - 1
    pb = p.astype(BF16)
    oc = [_dot(pb, jnp.where(rowv, cv[g], 0.0).astype(BF16)) for g in range(2)]
    o_c = jnp.where(g0rows, oc[0], oc[1]) * linv

    pn = p * linv
    s0 = jnp.sum(jnp.where(g0rows, pn, 0.0), axis=0, keepdims=True)
    s1 = jnp.sum(jnp.where(g0rows, 0.0, pn), axis=0, keepdims=True)
    psum = jnp.where(g0rows, jnp.broadcast_to(s0, pn.shape), jnp.broadcast_to(s1, pn.shape))
    imp = _dot_hilo(psum, ov_ref[...].astype(BF16))
    cur = n_sel_blocks - 1
    forced = (lane == 0) | (lane == cur) | (lane == cur - 1)
    val = jnp.where(lane >= n_sel_blocks, -2.0, jnp.where(forced, FORCE, imp))
    rank = jnp.zeros_like(val)
    for i in range(n_sel_blocks):
        ci = jnp.broadcast_to(val[:, i:i + 1], val.shape)
        beats = (ci > val) | ((ci == val) & (lane > i))
        rank = rank + jnp.where(beats, 1.0, 0.0)
    sel = jnp.where((rank < float(N_SEL)) & (lane < n_sel_blocks), 1.0, 0.0)
    key_mask = _dot(sel.astype(BF16), e_ref[...])

    def tiles(ref_list, col):
        def tile(rf, i):
            rows = lambda c: rf[pl.ds(4 * LANES * i + c, LANES, stride=4), :]
            return lambda: jnp.concatenate([rows(col), rows(col + 1)], axis=1)
        return [tile(rf, i) for rf in ref_list for i in range(rf.shape[0] // (4 * LANES))]

    b_new = bnew_ref[:, 0:1]
    o_s = _decode_attend(qf, g0rows, gk_ref[1:2, :], tiles(pages_s, 0), tiles(pages_s, 2),
                         bs_ref[...], key_mask, ksn_ref[...], b_new)
    o_w = _decode_attend(qf, g0rows, gk_ref[2:3, :], tiles([win_ref], 0), tiles([win_ref], 2),
                         bw_ref[...], None, kwn_ref[...], b_new)
    gt = gate_ref[...]
    o_ref[...] = gt[:, 0:1] * o_c + gt[:, 1:2] * o_s + gt[:, 2:3] * o_w


def _nsa_sample(page_table, cache_c, cache_s, win, q, gates, ks_new, kw_new, wk_pairs, wv_pairs, pos_rows, gk,
                bias_c, bias_s, bias_w, bias_new, ov, expand):
    nb, n_pages = page_table.shape
    page_rows = cache_c.shape[1]
    n_sel_blocks = -(-(n_pages * (page_rows // 4) + 1) // SEL_LEN)
    kern = functools.partial(_nsa_sample_kernel, n_pages=n_pages, n_sel_blocks=n_sel_blocks)
    page_spec = lambda p: pl.BlockSpec((None, page_rows, LANES), lambda bi, pt, p=p: (pt[bi, p], 0, 0))
    full = lambda a: pl.BlockSpec(a.shape, lambda bi, pt: (0,) * a.ndim)
    per_b = lambda a: pl.BlockSpec((None,) + a.shape[1:], lambda bi, pt: (bi,) + (0,) * (a.ndim - 1))
    grid_spec = pltpu.PrefetchScalarGridSpec(
        num_scalar_prefetch=1,
        grid=(nb,),
        in_specs=([page_spec(p) for p in range(n_pages)] + [page_spec(p) for p in range(n_pages)]
                  + [per_b(win), per_b(q), per_b(gates), per_b(ks_new), per_b(kw_new),
                     full(wk_pairs), full(wv_pairs), full(pos_rows), full(gk),
                     full(bias_c), full(bias_s), full(bias_w), full(bias_new), full(ov), full(expand)]),
        out_specs=pl.BlockSpec((None,) + q.shape[1:], lambda bi, pt: (bi, 0, 0)),
    )
    return pl.pallas_call(
        kern,
        grid_spec=grid_spec,
        out_shape=jax.ShapeDtypeStruct(q.shape, F32),
        compiler_params=_cparams(("arbitrary",)),
        name="nsa_sample",
    )(page_table, *([cache_c] * n_pages), *([cache_s] * n_pages), win, q, gates, ks_new, kw_new,
      wk_pairs, wv_pairs, pos_rows, gk, bias_c, bias_s, bias_w, bias_new, ov, expand)


def _merge_kernel(h_ref, oa_ref, ob_ref, wga_ref, wgb_ref, wg_ref, wn_ref, mix_ref):
    h = h_ref[...]
    ga = _sigmoid(_dot(h, wga_ref[...]))
    gb = _sigmoid(_dot(h, wgb_ref[...]))
    mix_ref[...] = (ga * _dot(oa_ref[...], wg_ref[...]) + gb * _dot(ob_ref[...], wn_ref[...])).astype(BF16)


def _merge(h, oa, ob, wga, wgb, wg, wn):
    n, d = h.shape
    tm = 512 if n % 512 == 0 else n
    tn = 512
    row = lambda w: pl.BlockSpec((tm, w), lambda i, j: (i, 0))
    col = lambda k: pl.BlockSpec((k, tn), lambda i, j: (0, j))
    return pl.pallas_call(
        _merge_kernel,
        grid=(n // tm, d // tn),
        in_specs=[row(d), row(oa.shape[1]), row(ob.shape[1]), col(d), col(d), col(wg.shape[0]), col(wn.shape[0])],
        out_specs=pl.BlockSpec((tm, tn), lambda i, j: (i, j)),
        out_shape=jax.ShapeDtypeStruct((n, d), BF16),
        compiler_params=_cparams(("arbitrary", "arbitrary")),
        name="merge",
    )(h, oa, ob, wga, wgb, wg, wn)


def _outproj_kernel(x_ref, mix_ref, wout_ref, gmlp_ref, x1_ref, hm_ref):
    x1 = x_ref[...] + _dot(mix_ref[...], wout_ref[...])
    x1_ref[...] = x1
    hm_ref[...] = (_rms(x1) * gmlp_ref[...]).astype(BF16)


def _outproj(x, mix, wout, gmlp):
    n, d = x.shape
    tm = 512 if n % 512 == 0 else n
    row = pl.BlockSpec((tm, d), lambda i: (i, 0))
    return pl.pallas_call(
        _outproj_kernel,
        grid=(n // tm,),
        in_specs=[row, row, pl.BlockSpec(wout.shape, lambda i: (0, 0)), pl.BlockSpec(gmlp.shape, lambda i: (0, 0))],
        out_specs=[row, row],
        out_shape=(jax.ShapeDtypeStruct((n, d), F32), jax.ShapeDtypeStruct((n, d), BF16)),
        compiler_params=_cparams(("arbitrary",)),
        name="outproj",
    )(x, mix, wout, gmlp)


def _ffn_kernel(hm_ref, x1_ref, wup_ref, wdown_ref, y_ref):
    f = pl.program_id(1)

    @pl.when(f == 0)
    def _():
        y_ref[...] = x1_ref[...]

    hid = jnp.maximum(_dot(hm_ref[...], wup_ref[...]), 0.0)
    y_ref[...] += _dot((hid * hid).astype(BF16), wdown_ref[...])


def _ffn(hm, x1, wup, wdown):
    n, d = hm.shape
    dff = wup.shape[1]
    tm = 512 if n % 512 == 0 else n
    tf = 1024
    row = pl.BlockSpec((tm, d), lambda i, f: (i, 0))
    return pl.pallas_call(
        _ffn_kernel,
        grid=(n // tm, dff // tf),
        in_specs=[row, row, pl.BlockSpec((d, tf), lambda i, f: (0, f)), pl.BlockSpec((tf, d), lambda i, f: (f, 0))],
        out_specs=row,
        out_shape=jax.ShapeDtypeStruct((n, d), F32),
        compiler_params=_cparams(("arbitrary", "arbitrary")),
        name="ffn",
    )(hm, x1, wup, wdown)


def _bucket(rel, valid):
    n = np.maximum(rel, 0)
    max_exact = N_BUCKETS // 2
    nf = np.maximum(n, 1).astype(np.float32)
    large = max_exact + (np.log(nf / np.float32(max_exact)) / np.float32(math.log(MAX_DIST / max_exact))
                         * np.float32(N_BUCKETS - max_exact)).astype(np.int32)
    large = np.minimum(large, N_BUCKETS - 1)
    return np.where(valid, np.where(n < max_exact, n, large), -1).astype(np.int32)


def _bias_tables_kernel(rb_ref, *refs, shifts):
    n = len(shifts)
    nbk, nh = rb_ref.shape
    for b_ref, o_ref, shift in zip(refs[:n], refs[n:], shifts):
        b = b_ref[...]
        rows = []
        for h in range(nh):
            sh = rb_ref[nbk - 1, h] if shift else 0.0
            acc = jnp.full(b.shape, NEG, F32)
            for k in range(nbk):
                acc = jnp.where(b == k, rb_ref[k, h] - sh, acc)
            if len(o_ref.shape) == 3:
                o_ref[h] = acc
            else:
                rows.append(acc)
        if rows:
            o_ref[...] = jnp.concatenate(rows, axis=0)


def _bias_tables(rel_bias, buckets, shifts):
    nh = rel_bias.shape[1]
    shapes = [jax.ShapeDtypeStruct((nh,) + (b.shape if b.shape[0] > 1 else b.shape[1:]), F32) for b in buckets]
    vm = pl.BlockSpec(memory_space=pltpu.VMEM)
    return pl.pallas_call(
        functools.partial(_bias_tables_kernel, shifts=tuple(shifts)),
        in_specs=[pl.BlockSpec(memory_space=pltpu.SMEM)] + [vm] * len(buckets),
        out_specs=[vm] * len(buckets),
        out_shape=shapes,
        name="bias_tables",
    )(rel_bias, *[jnp.asarray(b) for b in buckets])


def _overlap(nc, ns):
    i = np.arange(nc)[:, None] * CMP_STRIDE
    j = np.arange(ns)[None, :] * SEL_LEN
    return ((i < j + SEL_LEN) & (i + CMP_LEN > j)).astype(np.float32)


def _compress_weights(w, pos):
    s = np.arange(0, CMP_STRIDE, 2)
    top = jnp.concatenate([w[s], w[CMP_STRIDE + s]], axis=2)
    bot = jnp.concatenate([w[s + 1], w[CMP_STRIDE + s + 1]], axis=2)
    tiles = jnp.concatenate([top, bot], axis=1).astype(BF16)
    row_a = jnp.concatenate([pos[s], pos[s + 1]], axis=1)
    row_b = jnp.concatenate([pos[CMP_STRIDE + s], pos[CMP_STRIDE + s + 1]], axis=1)
    rows = jnp.zeros((len(s), 16, 2 * HEAD_DIM), F32).at[:, 0].set(row_a).at[:, 1].set(row_b)
    return tiles, rows.astype(BF16)


def _dense_tail(x, h, oa, ob, wga, wgb, wg, wn, wout, gmlp, wup, wdown):
    mix = _merge(h, oa, ob, wga, wgb, wg, wn)
    x1, hm = _outproj(x, mix, wout, gmlp)
    return _ffn(hm, x1, wup, wdown)


def kernel(x_prompt, x_sample, cache_cmp_kv, cache_sel_kv, state_win_kv, page_table, rel_bias, g_mix_norm, w_in,
           g_sgu, w_sgu, b_sgu, g_q, g_k, pos_cmp_k, w_cmp_k, pos_cmp_v, w_cmp_v, w_proj_gmlp, w_proj_nsa, w_out,
           g_mlp_norm, w_up, w_down):
    depth = g_mix_norm.shape[0]
    assert depth == 1
    l = 0
    bsz, seq, d = x_prompt.shape
    nb = x_sample.shape[0]
    assert x_sample.shape[1] == 1 and seq % KPAD == 0
    d_gm = g_sgu.shape[1]
    n_heads = rel_bias.shape[1]
    d_nsa = n_heads * HEAD_DIM
    kvw_cols = 2 * (n_heads // 4) * HEAD_DIM
    n_gate = 3 * n_heads
    c_q = 2 * d_gm
    c_kv = c_q + d_nsa
    c_gate = c_kv + 3 * kvw_cols
    c_ga = c_gate + n_gate
    c_gb = c_ga + d

    w = w_in[l]
    w_cat = jnp.pad(w[:, :c_ga], ((0, 0), (0, 5 * 1024 - c_ga))).astype(BF16)
    wga = w[:, c_ga:c_gb].astype(BF16)
    wgb = w[:, c_gb:].astype(BF16)
    wg = w_proj_gmlp[l].astype(BF16)
    wn = w_proj_nsa[l].astype(BF16)
    wout = w_out[l].astype(BF16)
    wup = w_up[l].astype(BF16)
    wdown = w_down[l].astype(BF16)
    gmix = g_mix_norm[l][None]
    gsgu = g_sgu[l][None]
    gq = g_q[l][None]
    gk = g_k[l]
    gmlp = g_mlp_norm[l][None]
    wk_pairs, posk_rows = _compress_weights(w_cmp_k[l], pos_cmp_k[l])
    wv_pairs, posv_rows = _compress_weights(w_cmp_v[l], pos_cmp_v[l])
    pos_rows = jnp.stack([posk_rows, posv_rows])
    n_groups = w_sgu.shape[1]
    b_exp = jnp.repeat(b_sgu[l].T, d_gm // n_groups, axis=1)
    w00 = jnp.repeat(w_sgu[l][:, 0, 0], d_gm // n_groups)[None]
    b00 = jnp.repeat(b_sgu[l][:, 0], d_gm // n_groups)[None]

    xp = x_prompt.reshape(bsz * seq, d)
    h_p, v_p, oa_p, q_p, kvc_p, kvs_p, kvw_p, gates_p = _inproj(
        xp, gmix, w_cat, gsgu, gq, w_sgu[l], b_exp, chunked=True)
    kvs3 = kvs_p.reshape(bsz, seq, kvw_cols)
    kvw3 = kvw_p.reshape(bsz, seq, kvw_cols)
    kvc3 = kvc_p.reshape(bsz, seq, kvw_cols)
    ks, vs, kw, vw = _kvprep(kvs3, kvw3, gk)
    kc, vc = _compress_prompt(kvc3, wk_pairs, wv_pairs, pos_rows, gk)

    n_pool, page = cache_cmp_kv.shape[1], cache_cmp_kv.shape[2]
    n_pages = page_table.shape[1]
    past = n_pages * page
    nwin = state_win_kv.shape[2]
    n_cmp_s = (past + 1 - CMP_LEN) // CMP_STRIDE + 1
    n_sel_s = -(-(past + 1) // SEL_LEN)
    nseg_s = past // CMP_STRIDE
    qi = np.arange(QBLOCK)[:, None]
    rel_s = qi + QBLOCK - np.arange(2 * QBLOCK)[None, :]
    rel_w = qi + WINDOW - np.arange(WINDOW + QBLOCK)[None, :]
    rel_c = qi + (KC_FRONT * CMP_STRIDE - CMP_LEN + 1) - CMP_STRIDE * np.arange(LANES)[None, :]
    srel_c = (past - (np.arange(nseg_s) * CMP_STRIDE + CMP_LEN - 1))[None]
    srel_s = (past - np.arange(past))[None]
    srel_w = (nwin - np.arange(nwin))[None]
    buckets = [
        _bucket(rel_c, rel_c >= 0), _bucket(rel_s, rel_s >= 0), _bucket(rel_w, (rel_w >= 0) & (rel_w < WINDOW)),
        _bucket(srel_c, (srel_c >= 0) & (np.arange(nseg_s)[None] < n_cmp_s)), _bucket(srel_s, srel_s >= 0),
        _bucket(srel_w, srel_w < WINDOW), np.zeros((1, LANES), np.int32)]
    tab_c, tab_s, tab_w, bias_c, bias_s, bias_w, bias_new = _bias_tables(
        rel_bias, buckets, [True, True, False, False, False, False, False])
    tab_c, tab_s, tab_w = [t.reshape(2, (n_heads // 2) * QBLOCK, t.shape[-1]) for t in (tab_c, tab_s, tab_w)]
    n_cmp = (seq - CMP_LEN) // CMP_STRIDE + 1
    n_selb = seq // SEL_LEN
    ov_p = np.zeros((KC_ROWS, LANES), np.float32)
    ov_p[KC_FRONT:KC_FRONT + n_cmp, :n_selb] = _overlap(n_cmp, n_selb)
    ov_p = jnp.asarray(ov_p)
    gates_g = gates_p[:, :n_gate].reshape(bsz, seq, 2, n_gate // 2).transpose(0, 2, 1, 3)
    gates_g = jnp.pad(gates_g, ((0, 0), (0, 0), (0, 0), (0, LANES - n_gate // 2)))
    ob_p = _nsa_prompt(q_p.reshape(bsz, seq, d_nsa), gates_g, kc, vc, ov_p, ks, vs, kw, vw, tab_c, tab_s, tab_w)
    y_p = _dense_tail(xp, h_p, oa_p, ob_p.reshape(bsz * seq, d_nsa), wga, wgb, wg, wn, wout, gmlp, wup, wdown)

    xs = x_sample.reshape(nb, d)
    h_s, v_s, oa_s, q_s, kvc_s, kvs_s, kvw_s, gates_s = _inproj(
        xs, gmix, w_cat, gsgu, gq, w00, b00, chunked=False)
    ov_s = np.zeros((nseg_s, LANES), np.float32)
    ov_s[:n_cmp_s, :n_sel_s] = _overlap(n_cmp_s, n_sel_s)
    expand = jnp.asarray(np.arange(LANES)[:, None] == (np.arange(past)[None, :] // SEL_LEN), BF16)
    gates_h = jnp.pad(gates_s[:, :n_gate].reshape(nb, n_heads, 3), ((0, 0), (0, 0), (0, LANES - 3)))
    lin = lambda a: a[l].reshape(a.shape[1], -1, HEAD_DIM)
    win_lin = lin(state_win_kv)
    ob_s = _nsa_sample(
        page_table, lin(cache_cmp_kv), lin(cache_sel_kv), win_lin,
        q_s.astype(F32).reshape(nb, n_heads, HEAD_DIM), gates_h,
        kvs_s[:, None, :], kvw_s[:, None, :], wk_pairs, wv_pairs, pos_rows, gk,
        bias_c, bias_s, bias_w, bias_new, jnp.asarray(ov_s), expand)
    y_s = _dense_tail(xs, h_s, oa_s, ob_s.reshape(nb, d_nsa).astype(BF16), wga, wgb, wg, wn, wout, gmlp, wup, wdown)

    n_kv = n_heads // 4
    kv6 = lambda a, b_, t_: a.reshape(1, b_, t_, 2, n_kv, HEAD_DIM)
    nw_p = min(WINDOW, seq)
    last = ((seq - 1) // CHUNK) * CHUNK
    nw_s = min(WINDOW, past + 1)
    win_new = jnp.concatenate([win_lin[:, 4 * (nwin + 1 - nw_s):], kvw_s.reshape(nb, 4, HEAD_DIM)], axis=1)
    return (y_p.reshape(bsz, seq, d), y_s.reshape(nb, 1, d),
            kv6(kvc3, bsz, seq), kv6(kvs3, bsz, seq), kv6(kvw3[:, seq - nw_p:], bsz, nw_p),
            v_p.reshape(bsz, seq, d_gm)[:, last:][None],
            kv6(kvc_s, nb, 1), kv6(kvs_s, nb, 1), kv6(win_new, nb, nw_s),
            v_s.reshape(1, nb, 1, d_gm))
```

```python
import functools
import math

import numpy as np
import jax
import jax.numpy as jnp
from jax import lax
from jax.experimental import pallas as pl
from jax.experimental.pallas import tpu as pltpu

F32 = jnp.float32
BF16 = jnp.bfloat16

HEAD_DIM = 128
CHUNK = 128
CMP_LEN = 32
CMP_STRIDE = 16
SEL_LEN = 64
N_SEL = 16
WINDOW = 512
N_BUCKETS = 32
MAX_DIST = 128
QBLOCK = 128
EPS = 1e-6
NEG = -1e30
HALF_NEG = -5e29
FORCE = 1e6

LANES = 128
PAD_LANE = 64
KPAD = 512
KC_FRONT = 16
KC_ROWS = 376
VMEM_LIMIT = 56 * 1024 * 1024


def _cparams(sem):
    return pltpu.CompilerParams(dimension_semantics=sem, vmem_limit_bytes=VMEM_LIMIT)


def _dot(a, b):
    return jnp.dot(a, b, preferred_element_type=F32)


def _dot_nt(a, b):
    return lax.dot_general(a, b, (((1,), (1,)), ((), ())), preferred_element_type=F32)


def _dot_hilo(a, b_bf16):
    hi = a.astype(BF16)
    lo = (a - hi.astype(F32)).astype(BF16)
    return _dot(hi, b_bf16) + _dot(lo, b_bf16)


def _rms(x):
    return x * lax.rsqrt(jnp.mean(x * x, axis=-1, keepdims=True) + EPS)


def _gelu(x):
    c = math.sqrt(2.0 / math.pi)
    return 0.5 * x * (1.0 + jnp.tanh(c * (x + 0.044715 * (x * x * x))))


def _sigmoid(x):
    return 1.0 / (1.0 + jnp.exp(-x))


def _inproj_kernel(x_ref, gmix_ref, w_ref, gsgu_ref, gq_ref, wsg_ref, bsg_ref,
                   h_ref, v_ref, oa_ref, q_ref, kvc_ref, kvs_ref, kvw_ref, gate_ref,
                   h_s, u_s, *, chunked, tm):
    j = pl.program_id(1)

    @pl.when(j == 0)
    def _():
        hb = (_rms(x_ref[...]) * gmix_ref[...]).astype(BF16)
        h_s[...] = hb
        h_ref[...] = hb

    z = _dot(h_s[...], w_ref[...])

    @pl.when(j == 0)
    def _():
        u_s[...] = _gelu(z)

    @pl.when(j == 1)
    def _():
        v = _rms(_gelu(z)) * gsgu_ref[...]
        v_ref[...] = v
        if chunked:
            row = lax.broadcasted_iota(jnp.int32, (CHUNK, CHUNK), 0)
            col = lax.broadcasted_iota(jnp.int32, (CHUNK, CHUNK), 1)
            n_groups = v.shape[1] // LANES
            for g in range(n_groups):
                wm = jnp.where(row >= col, wsg_ref[g], 0.0).astype(BF16)
                cs = slice(g * LANES, (g + 1) * LANES)
                for c in range(tm // CHUNK):
                    rs = slice(c * CHUNK, (c + 1) * CHUNK)
                    s = _dot(wm, v[rs, cs].astype(BF16)) + bsg_ref[:, cs]
                    oa_ref[rs, cs] = (u_s[rs, cs] * s).astype(BF16)
        else:
            oa_ref[...] = (u_s[...] * (v * wsg_ref[...] + bsg_ref[...])).astype(BF16)

    @pl.when(j == 2)
    def _():
        scale = HEAD_DIM ** -0.5
        for hd in range(z.shape[1] // HEAD_DIM):
            cs = slice(hd * HEAD_DIM, (hd + 1) * HEAD_DIM)
            q_ref[:, cs] = (_rms(z[:, cs]) * gq_ref[...] * scale).astype(BF16)

    def store_kv(ref, zz):
        for c in range(4):
            ref[pl.ds(c, tm, stride=4), :] = zz[:, c * LANES:(c + 1) * LANES]

    @pl.when(j == 3)
    def _():
        store_kv(kvc_ref, z[:, :512])
        store_kv(kvs_ref, z[:, 512:])

    @pl.when(j == 4)
    def _():
        store_kv(kvw_ref, z[:, :512])
        gate_ref[...] = _sigmoid(z[:, 512:640])


def _inproj(x, gmix, w_cat, gsgu, gq, wsg, bsg, *, chunked):
    n, d = x.shape
    tm = 512 if n % 512 == 0 else n
    tn = 1024
    kern = functools.partial(_inproj_kernel, chunked=chunked, tm=tm)
    full = lambda a: pl.BlockSpec(a.shape, lambda i, j: (0,) * a.ndim)
    row = lambda w: pl.BlockSpec((tm, w), lambda i, j: (i, 0))
    kvrow = pl.BlockSpec((4 * tm, LANES), lambda i, j: (i, 0))
    out_shapes = (
        jax.ShapeDtypeStruct((n, d), BF16),
        jax.ShapeDtypeStruct((n, 1024), F32),
        jax.ShapeDtypeStruct((n, 1024), BF16),
        jax.ShapeDtypeStruct((n, 1024), BF16),
        jax.ShapeDtypeStruct((4 * n, LANES), F32),
        jax.ShapeDtypeStruct((4 * n, LANES), F32),
        jax.ShapeDtypeStruct((4 * n, LANES), F32),
        jax.ShapeDtypeStruct((n, LANES), F32),
    )
    return pl.pallas_call(
        kern,
        grid=(n // tm, w_cat.shape[1] // tn),
        in_specs=[row(d), full(gmix), pl.BlockSpec((d, tn), lambda i, j: (0, j)),
                  full(gsgu), full(gq), full(wsg), full(bsg)],
        out_specs=[row(d), row(1024), row(1024), row(1024), kvrow, kvrow, kvrow, row(LANES)],
        out_shape=out_shapes,
        scratch_shapes=[pltpu.VMEM((tm, d), BF16), pltpu.VMEM((tm, 1024), F32)],
        compiler_params=_cparams(("arbitrary", "arbitrary")),
        name="inproj",
    )(x, gmix, w_cat, gsgu, gq, wsg, bsg)


def _kvprep_kernel(kvs_ref, kvw_ref, gk_ref, ks_ref, vs_ref, kw_ref, vw_ref):
    i = pl.program_id(1)
    rows = kvs_ref.shape[0] // 4
    col = lambda ref, c: ref[pl.ds(c, rows, stride=4), :]
    lane = lax.broadcasted_iota(jnp.int32, (rows, LANES), 1)
    row = lax.broadcasted_iota(jnp.int32, (rows, LANES), 0)

    @pl.when(i == 0)
    def _():
        aux = jnp.where(lane == PAD_LANE, 1.0, 0.0).astype(BF16)
        zk = jnp.zeros((rows, LANES), BF16)
        for g in range(2):
            ks_ref[g] = jnp.concatenate([zk, aux], axis=1)
            kw_ref[g] = jnp.concatenate([zk, aux], axis=1)
            vs_ref[g] = zk
            vw_ref[g] = zk

    @pl.when(i > 0)
    def _():
        blk = ((i - 1) * rows + row) // SEL_LEN
        onehot = jnp.where(lane == blk, 1.0, 0.0).astype(BF16)
        zaux = jnp.zeros((rows, LANES), BF16)
        for g in range(2):
            ks = (_rms(col(kvs_ref, g)) * gk_ref[1:2, :]).astype(BF16)
            kw = (_rms(col(kvw_ref, g)) * gk_ref[2:3, :]).astype(BF16)
            ks_ref[g] = jnp.concatenate([ks, onehot], axis=1)
            kw_ref[g] = jnp.concatenate([kw, zaux], axis=1)
            vs_ref[g] = col(kvs_ref, 2 + g).astype(BF16)
            vw_ref[g] = col(kvw_ref, 2 + g).astype(BF16)


def _kvprep(kvs, kvw, gk):
    b, t4, _ = kvs.shape
    t = t4 // 4
    rows = KPAD
    nblk = t // rows
    in_map = lambda bi, i: (bi, jnp.maximum(i - 1, 0), 0)
    out_map = lambda bi, i: (bi, 0, i, 0)
    kshape = jax.ShapeDtypeStruct((b, 2, KPAD + t, 2 * LANES), BF16)
    vshape = jax.ShapeDtypeStruct((b, 2, KPAD + t, LANES), BF16)
    return pl.pallas_call(
        _kvprep_kernel,
        grid=(b, nblk + 1),
        in_specs=[pl.BlockSpec((None, 4 * rows, LANES), in_map), pl.BlockSpec((None, 4 * rows, LANES), in_map),
                  pl.BlockSpec(gk.shape, lambda bi, i: (0, 0))],
        out_specs=[pl.BlockSpec((None, 2, rows, 2 * LANES), out_map), pl.BlockSpec((None, 2, rows, LANES), out_map),
                   pl.BlockSpec((None, 2, rows, 2 * LANES), out_map), pl.BlockSpec((None, 2, rows, LANES), out_map)],
        out_shape=(kshape, vshape, kshape, vshape),
        compiler_params=_cparams(("arbitrary", "arbitrary")),
        name="kvprep",
    )(kvs, kvw, gk)


def _compress_core(load_x, nseg, wk_ref, wv_ref, pos_ref):
    w_refs = (wk_ref, wv_ref)
    accs = [jnp.zeros((2 * nseg + 16, 2 * LANES), F32) for _ in range(2)]
    for s2 in range(CMP_STRIDE // 2):
        for kv in range(2):
            parts = []
            for g in range(2):
                col = 2 * kv + g
                parts.append(jnp.concatenate([load_x(2 * s2, col), load_x(2 * s2 + 1, col)], axis=1).astype(BF16))
            parts.append(pos_ref[kv, s2])
            lhs = jnp.concatenate(parts, axis=0)
            accs[kv] = accs[kv] + _dot(lhs, w_refs[kv][s2])
    outs = []
    for kv in range(2):
        y = accs[kv]
        post = y[2 * nseg:2 * nseg + 1, :LANES] + y[2 * nseg + 1:2 * nseg + 2, LANES:]
        per_g = []
        for g in range(2):
            y0 = y[g * nseg:(g + 1) * nseg, :LANES]
            y1 = y[g * nseg:(g + 1) * nseg, LANES:]
            per_g.append(y0 + pltpu.roll(y1, nseg - 1, 0) + post)
        outs.append(per_g)
    return outs


def _compress_prompt_kernel(x_ref, wk_ref, wv_ref, pos_ref, gk_ref, kc_ref, vc_ref):
    nseg = x_ref.shape[0] // (4 * CMP_STRIDE)
    load_x = lambda s, col: x_ref[pl.ds(4 * s + col, nseg, stride=4 * CMP_STRIDE), :]
    ck, cv = _compress_core(load_x, nseg, wk_ref, wv_ref, pos_ref)
    row = lax.broadcasted_iota(jnp.int32, (nseg, LANES), 0)
    lane = lax.broadcasted_iota(jnp.int32, (nseg, LANES), 1)
    valid = row < nseg - 1
    aux = jnp.where(valid, jnp.where(lane == row // 8, 1.0, 0.0), jnp.where(lane == PAD_LANE, 1.0, 0.0))
    back = KC_ROWS - KC_FRONT - nseg
    lane_f = lax.broadcasted_iota(jnp.int32, (KC_FRONT, LANES), 1)
    lane_b = lax.broadcasted_iota(jnp.int32, (back, LANES), 1)
    pad_f = jnp.concatenate([jnp.zeros((KC_FRONT, LANES), F32), jnp.where(lane_f == PAD_LANE, 1.0, 0.0)], axis=1)
    pad_b = jnp.concatenate([jnp.zeros((back, LANES), F32), jnp.where(lane_b == PAD_LANE, 1.0, 0.0)], axis=1)
    for g in range(2):
        kn = jnp.where(valid, _rms(ck[g]) * gk_ref[0:1, :], 0.0)
        kc_ref[g, 0:KC_FRONT, :] = pad_f
        kc_ref[g, KC_FRONT:KC_FRONT + nseg, :] = jnp.concatenate([kn, aux], axis=1)
        kc_ref[g, KC_FRONT + nseg:KC_ROWS, :] = pad_b
        vc_ref[g, 0:KC_FRONT, :] = jnp.zeros((KC_FRONT, LANES), F32)
        vc_ref[g, KC_FRONT:KC_FRONT + nseg, :] = jnp.where(valid, cv[g], 0.0)
        vc_ref[g, KC_FRONT + nseg:KC_ROWS, :] = jnp.zeros((back, LANES), F32)


def _compress_prompt(kvc, wk_pairs, wv_pairs, pos_rows, gk):
    b, t4, _ = kvc.shape
    full = lambda a: pl.BlockSpec(a.shape, lambda bi: (0,) * a.ndim)
    return pl.pallas_call(
        _compress_prompt_kernel,
        grid=(b,),
        in_specs=[pl.BlockSpec((None, t4, LANES), lambda bi: (bi, 0, 0)),
                  full(wk_pairs), full(wv_pairs), full(pos_rows), full(gk)],
        out_specs=[pl.BlockSpec((None, 2, KC_ROWS, 2 * LANES), lambda bi: (bi, 0, 0, 0)),
                   pl.BlockSpec((None, 2, KC_ROWS, LANES), lambda bi: (bi, 0, 0, 0))],
        out_shape=(jax.ShapeDtypeStruct((b, 2, KC_ROWS, 2 * LANES), F32),
                   jax.ShapeDtypeStruct((b, 2, KC_ROWS, LANES), F32)),
        compiler_params=_cparams(("arbitrary",)),
        name="compress_prompt",
    )(kvc, wk_pairs, wv_pairs, pos_rows, gk)


def _nsa_prompt_kernel(q_ref, gate_ref, kc_ref, vc_ref, ov_ref, ks_ref, vs_ref, kw_ref, vw_ref,
                       tc_ref, ts_ref, tw_ref, o_ref):
    qb = pl.program_id(2)
    t0 = qb * QBLOCK
    rq = 4 * QBLOCK
    q = q_ref[...]
    q4 = jnp.concatenate([q[:, r * LANES:(r + 1) * LANES] for r in range(4)], axis=0)
    lane = lax.broadcasted_iota(jnp.int32, (rq, LANES), 1)
    is_pad_lane = lane == PAD_LANE

    def q_aug(mb):
        return jnp.concatenate([q4, mb.astype(BF16)], axis=1)

    qa_pad = q_aug(jnp.where(is_pad_lane, NEG, 0.0))

    far_mask = ((lane < 32) & (lane >= qb - 2)) | is_pad_lane
    qa_cfar = q_aug(jnp.where(far_mask, NEG, 0.0))
    ncmp = 256
    near0 = pl.multiple_of(qb * 8, 8)
    k_far = kc_ref[KC_FRONT:KC_FRONT + ncmp, :].astype(BF16)
    k_near = kc_ref[pl.ds(near0, LANES), :].astype(BF16)
    s_far = _dot_nt(qa_cfar, k_far)
    s_near = _dot_nt(qa_pad, k_near) + tc_ref[...]
    m = jnp.maximum(jnp.max(s_far, axis=-1, keepdims=True), jnp.max(s_near, axis=-1, keepdims=True))
    p_far = jnp.where(s_far > HALF_NEG, jnp.exp(s_far - m), 0.0)
    p_near = jnp.where(s_near > HALF_NEG, jnp.exp(s_near - m), 0.0)
    l = jnp.sum(p_far, axis=-1, keepdims=True) + jnp.sum(p_near, axis=-1, keepdims=True)
    linv = 1.0 / jnp.where(l > 0.0, l, 1.0)
    v_far = vc_ref[KC_FRONT:KC_FRONT + ncmp, :].astype(BF16)
    v_near = vc_ref[pl.ds(near0, LANES), :].astype(BF16)
    o_c = (_dot(p_far.astype(BF16), v_far) + _dot(p_near.astype(BF16), v_near)) * linv

    pn_far = p_far * linv
    pn_near = p_near * linv
    ps_far = sum(pn_far[r * QBLOCK:(r + 1) * QBLOCK] for r in range(4))
    ps_near = sum(pn_near[r * QBLOCK:(r + 1) * QBLOCK] for r in range(4))
    ov_far = ov_ref[KC_FRONT:KC_FRONT + ncmp, :].astype(BF16)
    ov_near = ov_ref[pl.ds(near0, LANES), :].astype(BF16)
    imp = _dot_hilo(ps_far, ov_far) + _dot_hilo(ps_near, ov_near)
    nblk = 64
    imp_t = imp.T[:nblk]
    blk = lax.broadcasted_iota(jnp.int32, (nblk, QBLOCK), 0)
    qpos = t0 + lax.broadcasted_iota(jnp.int32, (nblk, QBLOCK), 1)
    cur = qpos // SEL_LEN
    forced = (blk == 0) | (blk == cur) | (blk == cur - 1)
    eligible = blk * SEL_LEN <= qpos
    val = jnp.where(forced, FORCE, jnp.where(eligible, imp_t, -1.0))
    rank = jnp.zeros((nblk, QBLOCK), F32)
    for i in range(nblk):
        ri = val[i:i + 1, :]
        beats = (ri > val) | ((ri == val) & (blk > i))
        rank = rank + jnp.where(beats, 1.0, 0.0)
    mb_t = jnp.where(rank < float(N_SEL), 0.0, NEG)
    row2 = lax.broadcasted_iota(jnp.int32, (LANES - nblk, QBLOCK), 0)
    mb_t = jnp.concatenate([mb_t, jnp.where(row2 == PAD_LANE - nblk, NEG, 0.0)], axis=0)
    mb = mb_t.T
    mb4 = jnp.concatenate([mb] * 4, axis=0)
    qa_snear = q_aug(mb4)
    qa_sfar = q_aug(jnp.where((lane < nblk) & (lane >= 2 * qb - 2), NEG, mb4))

    sn0 = pl.multiple_of(t0 + KPAD - QBLOCK, QBLOCK)
    s = _dot_nt(qa_snear, ks_ref[pl.ds(sn0, 2 * QBLOCK), :]) + ts_ref[...]
    m = jnp.max(s, axis=-1, keepdims=True)
    p = jnp.exp(s - m)
    l = jnp.sum(p, axis=-1, keepdims=True)
    acc = _dot(p.astype(BF16), vs_ref[pl.ds(sn0, 2 * QBLOCK), :])
    far_tile = 512
    n_far = (jnp.maximum(qb - 1, 0) * QBLOCK + far_tile - 1) // far_tile

    def far_body(j, carry):
        m, l, acc = carry
        r0 = pl.multiple_of(KPAD + j * far_tile, far_tile)
        s = _dot_nt(qa_sfar, ks_ref[pl.ds(r0, far_tile), :])
        m_new = jnp.maximum(m, jnp.max(s, axis=-1, keepdims=True))
        alpha = jnp.exp(m - m_new)
        p = jnp.exp(s - m_new)
        l = alpha * l + jnp.sum(p, axis=-1, keepdims=True)
        acc = alpha * acc + _dot(p.astype(BF16), vs_ref[pl.ds(r0, far_tile), :])
        return m_new, l, acc

    m, l, acc = lax.fori_loop(0, n_far, far_body, (m, l, acc))
    o_s = acc * (1.0 / l)

    w0 = pl.multiple_of(t0, QBLOCK)
    wlen = WINDOW + QBLOCK
    s = _dot_nt(qa_pad, kw_ref[pl.ds(w0, wlen), :]) + tw_ref[...]
    m = jnp.max(s, axis=-1, keepdims=True)
    p = jnp.exp(s - m)
    l = jnp.sum(p, axis=-1, keepdims=True)
    o_w = _dot(p.astype(BF16), vw_ref[pl.ds(w0, wlen), :]) * (1.0 / l)

    gt = gate_ref[...]
    outs = []
    for r in range(4):
        rs = slice(r * QBLOCK, (r + 1) * QBLOCK)
        outs.append(gt[:, 3 * r:3 * r + 1] * o_c[rs] + gt[:, 3 * r + 1:3 * r + 2] * o_s[rs]
                    + gt[:, 3 * r + 2:3 * r + 3] * o_w[rs])
    o_ref[...] = jnp.concatenate(outs, axis=1).astype(BF16)


def _nsa_prompt(q, gates_g, kc, vc, ov, ks, vs, kw, vw, tab_c, tab_s, tab_w):
    b, t, _ = q.shape
    nqb = t // QBLOCK
    kvspec = lambda a: pl.BlockSpec((None, None) + a.shape[2:], lambda bi, g, i: (bi, g, 0, 0))
    tabspec = lambda a: pl.BlockSpec((None,) + a.shape[1:], lambda bi, g, i: (g, 0, 0))
    return pl.pallas_call(
        _nsa_prompt_kernel,
        grid=(b, 2, nqb),
        in_specs=[pl.BlockSpec((None, QBLOCK, 512), lambda bi, g, i: (bi, i, g)),
                  pl.BlockSpec((None, None, QBLOCK, LANES), lambda bi, g, i: (bi, g, i, 0)),
                  kvspec(kc), kvspec(vc), pl.BlockSpec(ov.shape, lambda bi, g, i: (0, 0)),
                  kvspec(ks), kvspec(vs), kvspec(kw), kvspec(vw),
                  tabspec(tab_c), tabspec(tab_s), tabspec(tab_w)],
        out_specs=pl.BlockSpec((None, QBLOCK, 512), lambda bi, g, i: (bi, i, g)),
        out_shape=jax.ShapeDtypeStruct((b, t, 1024), BF16),
        compiler_params=_cparams(("arbitrary", "arbitrary", "arbitrary")),
        name="nsa_prompt",
    )(q, gates_g, kc, vc, ov, ks, vs, kw, vw, tab_c, tab_s, tab_w)


def _decode_attend(qf, g0rows, gk_row, k_tiles, v_tiles, bias, key_mask, new_row, bias_new):
    qg = qf * gk_row
    zero = jnp.zeros_like(qg)
    qbd = jnp.concatenate([jnp.where(g0rows, qg, zero), jnp.where(g0rows, zero, qg)], axis=1).astype(BF16)
    one = jnp.ones_like(qg)
    ones_bd = jnp.concatenate([jnp.where(g0rows, one, zero), jnp.where(g0rows, zero, one)], axis=1).astype(BF16)
    s_parts, q_parts = [], []
    for kt in k_tiles:
        k = kt()
        s_parts.append(_dot_nt(qbd, k.astype(BF16)))
        q_parts.append(_dot_nt(ones_bd, (k * k).astype(BF16)))
    s = jnp.concatenate(s_parts, axis=1)
    ssq = jnp.concatenate(q_parts, axis=1)
    s = s * lax.rsqrt(ssq * (1.0 / HEAD_DIM) + EPS) + bias
    if key_mask is not None:
        s = jnp.where(key_mask > 0.5, s, NEG)
    bc = lambda c: jnp.broadcast_to(new_row[c:c + 1, :], qf.shape)
    k_new = jnp.where(g0rows, bc(0), bc(1))
    v_new = jnp.where(g0rows, bc(2), bc(3))
    s_new = jnp.sum(qg * _rms(k_new), axis=-1, keepdims=True) + bias_new
    m = jnp.maximum(jnp.max(s, axis=-1, keepdims=True), s_new)
    p = jnp.exp(s - m)
    p_new = jnp.exp(s_new - m)
    l = jnp.sum(p, axis=-1, keepdims=True) + p_new
    o2 = jnp.zeros((qf.shape[0], 2 * LANES), F32)
    for i, vt in enumerate(v_tiles):
        o2 = o2 + _dot(p[:, i * LANES:(i + 1) * LANES].astype(BF16), vt().astype(BF16))
    o = jnp.where(g0rows, o2[:, :LANES], o2[:, LANES:]) + p_new * v_new
    return o * (1.0 / l)


def _nsa_sample_kernel(pt_ref, *refs, n_pages, n_sel_blocks):
    del pt_ref
    pages_c = refs[:n_pages]
    pages_s = refs[n_pages:2 * n_pages]
    (win_ref, q_ref, gate_ref, ksn_ref, kwn_ref, wk_ref, wv_ref, pos_ref, gk_ref,
     bc_ref, bs_ref, bw_ref, bnew_ref, ov_ref, e_ref, o_ref, wout_ref) = refs[2 * n_pages:]
    qf = q_ref[...]
    nh = qf.shape[0]
    g0rows = lax.broadcasted_iota(jnp.int32, (nh, LANES), 0) < nh // 2
    lane = lax.broadcasted_iota(jnp.int32, (nh, LANES), 1)

    seg_per_page = pages_c[0].shape[0] // (4 * CMP_STRIDE)
    nseg = n_pages * seg_per_page
    load_x = lambda s, col: jnp.concatenate(
        [pg[pl.ds(4 * s + col, seg_per_page, stride=4 * CMP_STRIDE), :] for pg in pages_c], axis=0)
    ck, cv = _compress_core(load_x, nseg, wk_ref, wv_ref, pos_ref)
    kcn = jnp.concatenate([(_rms(ck[g]) * gk_ref[0:1, :]).astype(BF16) for g in range(2)], axis=0)
    s2 = _dot_nt(qf.astype(BF16), kcn)
    s = jnp.where(g0rows, s2[:, :nseg], s2[:, nseg:]) + bc_ref[...]
    m = jnp.max(s, axis=-1, keepdims=True)
    p = jnp.exp(s - m)
    linv = 1.0 / jnp.sum(p, axis=-1, keepdims=True)
    rowv = lax.broadcasted_iota(jnp.int32, (nseg, LANES), 0) < nseg - 1
    pb = p.astype(BF16)
    oc = [_dot(pb, jnp.where(rowv, cv[g], 0.0).astype(BF16)) for g in range(2)]
    o_c = jnp.where(g0rows, oc[0], oc[1]) * linv

    pn = p * linv
    s0 = jnp.sum(jnp.where(g0rows, pn, 0.0), axis=0, keepdims=True)
    s1 = jnp.sum(jnp.where(g0rows, 0.0, pn), axis=0, keepdims=True)
    psum = jnp.where(g0rows, jnp.broadcast_to(s0, pn.shape), jnp.broadcast_to(s1, pn.shape))
    imp = _dot_hilo(psum, ov_ref[...].astype(BF16))
    cur = n_sel_blocks - 1
    forced = (lane == 0) | (lane == cur) | (lane == cur - 1)
    val = jnp.where(lane >= n_sel_blocks, -2.0, jnp.where(forced, FORCE, imp))
    rank = jnp.zeros_like(val)
    for i in range(n_sel_blocks):
        ci = jnp.broadcast_to(val[:, i:i + 1], val.shape)
        beats = (ci > val) | ((ci == val) & (lane > i))
        rank = rank + jnp.where(beats, 1.0, 0.0)
    sel = jnp.where((rank < float(N_SEL)) & (lane < n_sel_blocks), 1.0, 0.0)
    key_mask = _dot(sel.astype(BF16), e_ref[...])

    def tiles(ref_list, col):
        def tile(rf, i):
            rows = lambda c: rf[pl.ds(4 * LANES * i + c, LANES, stride=4), :]
            return lambda: jnp.concatenate([rows(col), rows(col + 1)], axis=1)
        return [tile(rf, i) for rf in ref_list for i in range(rf.shape[0] // (4 * LANES))]

    b_new = bnew_ref[:, 0:1]
    o_s = _decode_attend(qf, g0rows, gk_ref[1:2, :], tiles(pages_s, 0), tiles(pages_s, 2),
                         bs_ref[...], key_mask, ksn_ref[...], b_new)
    o_w = _decode_attend(qf, g0rows, gk_ref[2:3, :], tiles([win_ref], 0), tiles([win_ref], 2),
                         bw_ref[...], None, kwn_ref[...], b_new)
    gt = gate_ref[...]
    o_ref[...] = gt[:, 0:1] * o_c + gt[:, 1:2] * o_s + gt[:, 2:3] * o_w

    keep = wout_ref.shape[0] - 4
    drop = win_ref.shape[0] - keep
    wout_ref[0:keep, :] = win_ref[drop:drop + keep, :]
    wout_ref[keep:keep + 4, :] = kwn_ref[...]


def _nsa_sample(page_table, cache_c, cache_s, win, q, gates, ks_new, kw_new, wk_pairs, wv_pairs, pos_rows, gk,
                bias_c, bias_s, bias_w, bias_new, ov, expand):
    nb, n_pages = page_table.shape
    page_rows = cache_c.shape[1]
    n_sel_blocks = -(-(n_pages * (page_rows // 4) + 1) // SEL_LEN)
    kern = functools.partial(_nsa_sample_kernel, n_pages=n_pages, n_sel_blocks=n_sel_blocks)
    win_rows_out = 4 * min(WINDOW, n_pages * (page_rows // 4) + 1)
    page_spec = lambda p: pl.BlockSpec((None, page_rows, LANES), lambda bi, pt, p=p: (pt[bi, p], 0, 0))
    full = lambda a: pl.BlockSpec(a.shape, lambda bi, pt: (0,) * a.ndim)
    per_b = lambda a: pl.BlockSpec((None,) + a.shape[1:], lambda bi, pt: (bi,) + (0,) * (a.ndim - 1))
    grid_spec = pltpu.PrefetchScalarGridSpec(
        num_scalar_prefetch=1,
        grid=(nb,),
        in_specs=([page_spec(p) for p in range(n_pages)] + [page_spec(p) for p in range(n_pages)]
                  + [per_b(win), per_b(q), per_b(gates), per_b(ks_new), per_b(kw_new),
                     full(wk_pairs), full(wv_pairs), full(pos_rows), full(gk),
                     full(bias_c), full(bias_s), full(bias_w), full(bias_new), full(ov), full(expand)]),
        out_specs=[pl.BlockSpec((None,) + q.shape[1:], lambda bi, pt: (bi, 0, 0)),
                   pl.BlockSpec((None, win_rows_out, LANES), lambda bi, pt: (bi, 0, 0))],
    )
    return pl.pallas_call(
        kern,
        grid_spec=grid_spec,
        out_shape=(jax.ShapeDtypeStruct(q.shape, F32), jax.ShapeDtypeStruct((nb, win_rows_out, LANES), F32)),
        compiler_params=_cparams(("arbitrary",)),
        name="nsa_sample",
    )(page_table, *([cache_c] * n_pages), *([cache_s] * n_pages), win, q, gates, ks_new, kw_new,
      wk_pairs, wv_pairs, pos_rows, gk, bias_c, bias_s, bias_w, bias_new, ov, expand)


def _merge_kernel(h_ref, oa_ref, ob_ref, wga_ref, wgb_ref, wg_ref, wn_ref, mix_ref):
    h = h_ref[...]
    ga = _sigmoid(_dot(h, wga_ref[...]))
    gb = _sigmoid(_dot(h, wgb_ref[...]))
    mix_ref[...] = (ga * _dot(oa_ref[...], wg_ref[...]) + gb * _dot(ob_ref[...], wn_ref[...])).astype(BF16)


def _merge(h, oa, ob, wga, wgb, wg, wn):
    n, d = h.shape
    tm = 512 if n % 512 == 0 else n
    tn = 512
    row = lambda w: pl.BlockSpec((tm, w), lambda i, j: (i, 0))
    col = lambda k: pl.BlockSpec((k, tn), lambda i, j: (0, j))
    return pl.pallas_call(
        _merge_kernel,
        grid=(n // tm, d // tn),
        in_specs=[row(d), row(oa.shape[1]), row(ob.shape[1]), col(d), col(d), col(wg.shape[0]), col(wn.shape[0])],
        out_specs=pl.BlockSpec((tm, tn), lambda i, j: (i, j)),
        out_shape=jax.ShapeDtypeStruct((n, d), BF16),
        compiler_params=_cparams(("arbitrary", "arbitrary")),
        name="merge",
    )(h, oa, ob, wga, wgb, wg, wn)


def _outproj_kernel(x_ref, mix_ref, wout_ref, gmlp_ref, x1_ref, hm_ref):
    x1 = x_ref[...] + _dot(mix_ref[...], wout_ref[...])
    x1_ref[...] = x1
    hm_ref[...] = (_rms(x1) * gmlp_ref[...]).astype(BF16)


def _outproj(x, mix, wout, gmlp):
    n, d = x.shape
    tm = 512 if n % 512 == 0 else n
    row = pl.BlockSpec((tm, d), lambda i: (i, 0))
    return pl.pallas_call(
        _outproj_kernel,
        grid=(n // tm,),
        in_specs=[row, row, pl.BlockSpec(wout.shape, lambda i: (0, 0)), pl.BlockSpec(gmlp.shape, lambda i: (0, 0))],
        out_specs=[row, row],
        out_shape=(jax.ShapeDtypeStruct((n, d), F32), jax.ShapeDtypeStruct((n, d), BF16)),
        compiler_params=_cparams(("arbitrary",)),
        name="outproj",
    )(x, mix, wout, gmlp)


def _ffn_kernel(hm_ref, x1_ref, wup_ref, wdown_ref, y_ref):
    f = pl.program_id(1)

    @pl.when(f == 0)
    def _():
        y_ref[...] = x1_ref[...]

    hid = jnp.maximum(_dot(hm_ref[...], wup_ref[...]), 0.0)
    y_ref[...] += _dot((hid * hid).astype(BF16), wdown_ref[...])


def _ffn(hm, x1, wup, wdown):
    n, d = hm.shape
    dff = wup.shape[1]
    tm = 512 if n % 512 == 0 else n
    tf = 1024
    row = pl.BlockSpec((tm, d), lambda i, f: (i, 0))
    return pl.pallas_call(
        _ffn_kernel,
        grid=(n // tm, dff // tf),
        in_specs=[row, row, pl.BlockSpec((d, tf), lambda i, f: (0, f)), pl.BlockSpec((tf, d), lambda i, f: (f, 0))],
        out_specs=row,
        out_shape=jax.ShapeDtypeStruct((n, d), F32),
        compiler_params=_cparams(("arbitrary", "arbitrary")),
        name="ffn",
    )(hm, x1, wup, wdown)


def _bucket(rel, valid):
    n = np.maximum(rel, 0)
    max_exact = N_BUCKETS // 2
    nf = np.maximum(n, 1).astype(np.float32)
    large = max_exact + (np.log(nf / np.float32(max_exact)) / np.float32(math.log(MAX_DIST / max_exact))
                         * np.float32(N_BUCKETS - max_exact)).astype(np.int32)
    large = np.minimum(large, N_BUCKETS - 1)
    return np.where(valid, np.where(n < max_exact, n, large), -1).astype(np.int32)


def _bias_tables_kernel(rb_ref, *refs, shifts):
    n = len(shifts)
    nbk, nh = rb_ref.shape
    for b_ref, o_ref, shift in zip(refs[:n], refs[n:], shifts):
        b = b_ref[...]
        rows = []
        for h in range(nh):
            sh = rb_ref[nbk - 1, h] if shift else 0.0
            acc = jnp.full(b.shape, NEG, F32)
            for k in range(nbk):
                acc = jnp.where(b == k, rb_ref[k, h] - sh, acc)
            if len(o_ref.shape) == 3:
                o_ref[h] = acc
            else:
                rows.append(acc)
        if rows:
            o_ref[...] = jnp.concatenate(rows, axis=0)


def _bias_tables(rel_bias, buckets, shifts):
    nh = rel_bias.shape[1]
    shapes = [jax.ShapeDtypeStruct((nh,) + (b.shape if b.shape[0] > 1 else b.shape[1:]), F32) for b in buckets]
    vm = pl.BlockSpec(memory_space=pltpu.VMEM)
    return pl.pallas_call(
        functools.partial(_bias_tables_kernel, shifts=tuple(shifts)),
        in_specs=[pl.BlockSpec(memory_space=pltpu.SMEM)] + [vm] * len(buckets),
        out_specs=[vm] * len(buckets),
        out_shape=shapes,
        name="bias_tables",
    )(rel_bias, *[jnp.asarray(b) for b in buckets])


def _overlap(nc, ns):
    i = np.arange(nc)[:, None] * CMP_STRIDE
    j = np.arange(ns)[None, :] * SEL_LEN
    return ((i < j + SEL_LEN) & (i + CMP_LEN > j)).astype(np.float32)


def _compress_weights(w, pos):
    s = np.arange(0, CMP_STRIDE, 2)
    top = jnp.concatenate([w[s], w[CMP_STRIDE + s]], axis=2)
    bot = jnp.concatenate([w[s + 1], w[CMP_STRIDE + s + 1]], axis=2)
    tiles = jnp.concatenate([top, bot], axis=1).astype(BF16)
    row_a = jnp.concatenate([pos[s], pos[s + 1]], axis=1)
    row_b = jnp.concatenate([pos[CMP_STRIDE + s], pos[CMP_STRIDE + s + 1]], axis=1)
    rows = jnp.zeros((len(s), 16, 2 * HEAD_DIM), F32).at[:, 0].set(row_a).at[:, 1].set(row_b)
    return tiles, rows.astype(BF16)


def _dense_tail(x, h, oa, ob, wga, wgb, wg, wn, wout, gmlp, wup, wdown):
    mix = _merge(h, oa, ob, wga, wgb, wg, wn)
    x1, hm = _outproj(x, mix, wout, gmlp)
    return _ffn(hm, x1, wup, wdown)


def kernel(x_prompt, x_sample, cache_cmp_kv, cache_sel_kv, state_win_kv, page_table, rel_bias, g_mix_norm, w_in,
           g_sgu, w_sgu, b_sgu, g_q, g_k, pos_cmp_k, w_cmp_k, pos_cmp_v, w_cmp_v, w_proj_gmlp, w_proj_nsa, w_out,
           g_mlp_norm, w_up, w_down):
    depth = g_mix_norm.shape[0]
    assert depth == 1
    l = 0
    bsz, seq, d = x_prompt.shape
    nb = x_sample.shape[0]
    assert x_sample.shape[1] == 1 and seq % KPAD == 0
    d_gm = g_sgu.shape[1]
    n_heads = rel_bias.shape[1]
    d_nsa = n_heads * HEAD_DIM
    kvw_cols = 2 * (n_heads // 4) * HEAD_DIM
    n_gate = 3 * n_heads
    c_q = 2 * d_gm
    c_kv = c_q + d_nsa
    c_gate = c_kv + 3 * kvw_cols
    c_ga = c_gate + n_gate
    c_gb = c_ga + d

    w = w_in[l]
    w_cat = jnp.pad(w[:, :c_ga], ((0, 0), (0, 5 * 1024 - c_ga))).astype(BF16)
    wga = w[:, c_ga:c_gb].astype(BF16)
    wgb = w[:, c_gb:].astype(BF16)
    wg = w_proj_gmlp[l].astype(BF16)
    wn = w_proj_nsa[l].astype(BF16)
    wout = w_out[l].astype(BF16)
    wup = w_up[l].astype(BF16)
    wdown = w_down[l].astype(BF16)
    gmix = g_mix_norm[l][None]
    gsgu = g_sgu[l][None]
    gq = g_q[l][None]
    gk = g_k[l]
    gmlp = g_mlp_norm[l][None]
    wk_pairs, posk_rows = _compress_weights(w_cmp_k[l], pos_cmp_k[l])
    wv_pairs, posv_rows = _compress_weights(w_cmp_v[l], pos_cmp_v[l])
    pos_rows = jnp.stack([posk_rows, posv_rows])
    n_groups = w_sgu.shape[1]
    b_exp = jnp.repeat(b_sgu[l].T, d_gm // n_groups, axis=1)
    w00 = jnp.repeat(w_sgu[l][:, 0, 0], d_gm // n_groups)[None]
    b00 = jnp.repeat(b_sgu[l][:, 0], d_gm // n_groups)[None]

    xp = x_prompt.reshape(bsz * seq, d)
    h_p, v_p, oa_p, q_p, kvc_p, kvs_p, kvw_p, gates_p = _inproj(
        xp, gmix, w_cat, gsgu, gq, w_sgu[l], b_exp, chunked=True)
    kvs3 = kvs_p.reshape(bsz, 4 * seq, HEAD_DIM)
    kvw3 = kvw_p.reshape(bsz, 4 * seq, HEAD_DIM)
    kvc3 = kvc_p.reshape(bsz, 4 * seq, HEAD_DIM)
    ks, vs, kw, vw = _kvprep(kvs3, kvw3, gk)
    kc, vc = _compress_prompt(kvc3, wk_pairs, wv_pairs, pos_rows, gk)

    n_pool, page = cache_cmp_kv.shape[1], cache_cmp_kv.shape[2]
    n_pages = page_table.shape[1]
    past = n_pages * page
    nwin = state_win_kv.shape[2]
    n_cmp_s = (past + 1 - CMP_LEN) // CMP_STRIDE + 1
    n_sel_s = -(-(past + 1) // SEL_LEN)
    nseg_s = past // CMP_STRIDE
    qi = np.arange(QBLOCK)[:, None]
    rel_s = qi + QBLOCK - np.arange(2 * QBLOCK)[None, :]
    rel_w = qi + WINDOW - np.arange(WINDOW + QBLOCK)[None, :]
    rel_c = qi + (KC_FRONT * CMP_STRIDE - CMP_LEN + 1) - CMP_STRIDE * np.arange(LANES)[None, :]
    srel_c = (past - (np.arange(nseg_s) * CMP_STRIDE + CMP_LEN - 1))[None]
    srel_s = (past - np.arange(past))[None]
    srel_w = (nwin - np.arange(nwin))[None]
    buckets = [
        _bucket(rel_c, rel_c >= 0), _bucket(rel_s, rel_s >= 0), _bucket(rel_w, (rel_w >= 0) & (rel_w < WINDOW)),
        _bucket(srel_c, (srel_c >= 0) & (np.arange(nseg_s)[None] < n_cmp_s)), _bucket(srel_s, srel_s >= 0),
        _bucket(srel_w, srel_w < WINDOW), np.zeros((1, LANES), np.int32)]
    tab_c, tab_s, tab_w, bias_c, bias_s, bias_w, bias_new = _bias_tables(
        rel_bias, buckets, [True, True, False, False, False, False, False])
    tab_c, tab_s, tab_w = [t.reshape(2, (n_heads // 2) * QBLOCK, t.shape[-1]) for t in (tab_c, tab_s, tab_w)]
    n_cmp = (seq - CMP_LEN) // CMP_STRIDE + 1
    n_selb = seq // SEL_LEN
    ov_p = np.zeros((KC_ROWS, LANES), np.float32)
    ov_p[KC_FRONT:KC_FRONT + n_cmp, :n_selb] = _overlap(n_cmp, n_selb)
    ov_p = jnp.asarray(ov_p)
    gates_g = gates_p[:, :n_gate].reshape(bsz, seq, 2, n_gate // 2).transpose(0, 2, 1, 3)
    gates_g = jnp.pad(gates_g, ((0, 0), (0, 0), (0, 0), (0, LANES - n_gate // 2)))
    ob_p = _nsa_prompt(q_p.reshape(bsz, seq, d_nsa), gates_g, kc, vc, ov_p, ks, vs, kw, vw, tab_c, tab_s, tab_w)
    y_p = _dense_tail(xp, h_p, oa_p, ob_p.reshape(bsz * seq, d_nsa), wga, wgb, wg, wn, wout, gmlp, wup, wdown)

    xs = x_sample.reshape(nb, d)
    h_s, v_s, oa_s, q_s, kvc_s, kvs_s, kvw_s, gates_s = _inproj(
        xs, gmix, w_cat, gsgu, gq, w00, b00, chunked=False)
    ov_s = np.zeros((nseg_s, LANES), np.float32)
    ov_s[:n_cmp_s, :n_sel_s] = _overlap(n_cmp_s, n_sel_s)
    expand = jnp.asarray(np.arange(LANES)[:, None] == (np.arange(past)[None, :] // SEL_LEN), BF16)
    gates_h = jnp.pad(gates_s[:, :n_gate].reshape(nb, n_heads, 3), ((0, 0), (0, 0), (0, LANES - 3)))
    lin = lambda a: a[l].reshape(a.shape[1], -1, HEAD_DIM)
    win_lin = lin(state_win_kv)
    ob_s, win_new = _nsa_sample(
        page_table, lin(cache_cmp_kv), lin(cache_sel_kv), win_lin,
        q_s.astype(F32).reshape(nb, n_heads, HEAD_DIM), gates_h,
        kvs_s.reshape(nb, 4, HEAD_DIM), kvw_s.reshape(nb, 4, HEAD_DIM), wk_pairs, wv_pairs, pos_rows, gk,
        bias_c, bias_s, bias_w, bias_new, jnp.asarray(ov_s), expand)
    y_s = _dense_tail(xs, h_s, oa_s, ob_s.reshape(nb, d_nsa).astype(BF16), wga, wgb, wg, wn, wout, gmlp, wup, wdown)

    n_kv = n_heads // 4
    kv6 = lambda a, b_, t_: a.reshape(1, b_, t_, 2, n_kv, HEAD_DIM)
    nw_p = min(WINDOW, seq)
    last = ((seq - 1) // CHUNK) * CHUNK
    nw_s = min(WINDOW, past + 1)
    return (y_p.reshape(bsz, seq, d), y_s.reshape(nb, 1, d),
            kv6(kvc3, bsz, seq), kv6(kvs3, bsz, seq), kv6(kvw3[:, 4 * (seq - nw_p):], bsz, nw_p),
            v_p.reshape(bsz, seq, d_gm)[:, last:][None],
            kv6(kvc_s, nb, 1), kv6(kvs_s, nb, 1), kv6(win_new, nb, nw_s),
            v_s.reshape(1, nb, 1, d_gm))
```

```python
import functools
import math

import numpy as np
import jax
import jax.numpy as jnp
from jax import lax
from jax.experimental import pallas as pl
from jax.experimental.pallas import tpu as pltpu

F32 = jnp.float32
BF16 = jnp.bfloat16

HEAD_DIM = 128
CHUNK = 128
CMP_LEN = 32
CMP_STRIDE = 16
SEL_LEN = 64
N_SEL = 16
WINDOW = 512
N_BUCKETS = 32
MAX_DIST = 128
QBLOCK = 128
EPS = 1e-6
NEG = -1e30
HALF_NEG = -5e29
FORCE = 1e6

LANES = 128
PAD_LANE = 64
KPAD = 512
KC_FRONT = 16
KC_ROWS = 376
SAMPLE_GROUP = 2
VMEM_LIMIT = 56 * 1024 * 1024


def _cparams(sem):
    return pltpu.CompilerParams(dimension_semantics=sem, vmem_limit_bytes=VMEM_LIMIT)


def _dot(a, b):
    return jnp.dot(a, b, preferred_element_type=F32)


def _dot_nt(a, b):
    return lax.dot_general(a, b, (((1,), (1,)), ((), ())), preferred_element_type=F32)


def _dot_hilo(a, b_bf16):
    hi = a.astype(BF16)
    lo = (a - hi.astype(F32)).astype(BF16)
    return _dot(hi, b_bf16) + _dot(lo, b_bf16)


def _rms(x):
    return x * lax.rsqrt(jnp.mean(x * x, axis=-1, keepdims=True) + EPS)


def _gelu(x):
    c = math.sqrt(2.0 / math.pi)
    return 0.5 * x * (1.0 + jnp.tanh(c * (x + 0.044715 * (x * x * x))))


def _sigmoid(x):
    return 1.0 / (1.0 + jnp.exp(-x))


def _inproj_kernel(x_ref, gmix_ref, w_ref, gsgu_ref, gq_ref, wsg_ref, bsg_ref,
                   h_ref, v_ref, oa_ref, q_ref, kvc_ref, kvs_ref, kvw_ref, gate_ref,
                   h_s, u_s, *, chunked, tm):
    j = pl.program_id(1)

    @pl.when(j == 0)
    def _():
        hb = (_rms(x_ref[...]) * gmix_ref[...]).astype(BF16)
        h_s[...] = hb
        h_ref[...] = hb

    z = _dot(h_s[...], w_ref[...])

    @pl.when(j == 0)
    def _():
        u_s[...] = _gelu(z)

    @pl.when(j == 1)
    def _():
        v = _rms(_gelu(z)) * gsgu_ref[...]
        v_ref[...] = v
        if chunked:
            row = lax.broadcasted_iota(jnp.int32, (CHUNK, CHUNK), 0)
            col = lax.broadcasted_iota(jnp.int32, (CHUNK, CHUNK), 1)
            n_groups = v.shape[1] // LANES
            for g in range(n_groups):
                wm = jnp.where(row >= col, wsg_ref[g], 0.0).astype(BF16)
                cs = slice(g * LANES, (g + 1) * LANES)
                for c in range(tm // CHUNK):
                    rs = slice(c * CHUNK, (c + 1) * CHUNK)
                    s = _dot(wm, v[rs, cs].astype(BF16)) + bsg_ref[:, cs]
                    oa_ref[rs, cs] = (u_s[rs, cs] * s).astype(BF16)
        else:
            oa_ref[...] = (u_s[...] * (v * wsg_ref[...] + bsg_ref[...])).astype(BF16)

    @pl.when(j == 2)
    def _():
        scale = HEAD_DIM ** -0.5
        for hd in range(z.shape[1] // HEAD_DIM):
            cs = slice(hd * HEAD_DIM, (hd + 1) * HEAD_DIM)
            q_ref[:, cs] = (_rms(z[:, cs]) * gq_ref[...] * scale).astype(BF16)

    def store_kv(ref, zz):
        for c in range(4):
            ref[pl.ds(c, tm, stride=4), :] = zz[:, c * LANES:(c + 1) * LANES]

    @pl.when(j == 3)
    def _():
        store_kv(kvc_ref, z[:, :512])
        store_kv(kvs_ref, z[:, 512:])

    @pl.when(j == 4)
    def _():
        store_kv(kvw_ref, z[:, :512])
        gate_ref[...] = _sigmoid(z[:, 512:640])


def _inproj(x, gmix, w_cat, gsgu, gq, wsg, bsg, *, chunked):
    n, d = x.shape
    tm = 512 if n % 512 == 0 else n
    tn = 1024
    kern = functools.partial(_inproj_kernel, chunked=chunked, tm=tm)
    full = lambda a: pl.BlockSpec(a.shape, lambda i, j: (0,) * a.ndim)
    row = lambda w: pl.BlockSpec((tm, w), lambda i, j: (i, 0))
    kvrow = pl.BlockSpec((4 * tm, LANES), lambda i, j: (i, 0))
    out_shapes = (
        jax.ShapeDtypeStruct((n, d), BF16),
        jax.ShapeDtypeStruct((n, 1024), F32),
        jax.ShapeDtypeStruct((n, 1024), BF16),
        jax.ShapeDtypeStruct((n, 1024), BF16),
        jax.ShapeDtypeStruct((4 * n, LANES), F32),
        jax.ShapeDtypeStruct((4 * n, LANES), F32),
        jax.ShapeDtypeStruct((4 * n, LANES), F32),
        jax.ShapeDtypeStruct((n, LANES), F32),
    )
    return pl.pallas_call(
        kern,
        grid=(n // tm, w_cat.shape[1] // tn),
        in_specs=[row(d), full(gmix), pl.BlockSpec((d, tn), lambda i, j: (0, j)),
                  full(gsgu), full(gq), full(wsg), full(bsg)],
        out_specs=[row(d), row(1024), row(1024), row(1024), kvrow, kvrow, kvrow, row(LANES)],
        out_shape=out_shapes,
        scratch_shapes=[pltpu.VMEM((tm, d), BF16), pltpu.VMEM((tm, 1024), F32)],
        compiler_params=_cparams(("arbitrary", "arbitrary")),
        name="inproj",
    )(x, gmix, w_cat, gsgu, gq, wsg, bsg)


def _kvprep_kernel(kvs_ref, kvw_ref, gk_ref, ks_ref, vs_ref, kw_ref, vw_ref):
    i = pl.program_id(1)
    rows = kvs_ref.shape[0] // 4
    col = lambda ref, c: ref[pl.ds(c, rows, stride=4), :]
    lane = lax.broadcasted_iota(jnp.int32, (rows, LANES), 1)
    row = lax.broadcasted_iota(jnp.int32, (rows, LANES), 0)

    @pl.when(i == 0)
    def _():
        aux = jnp.where(lane == PAD_LANE, 1.0, 0.0).astype(BF16)
        zk = jnp.zeros((rows, LANES), BF16)
        for g in range(2):
            ks_ref[g] = jnp.concatenate([zk, aux], axis=1)
            kw_ref[g] = jnp.concatenate([zk, aux], axis=1)
            vs_ref[g] = zk
            vw_ref[g] = zk

    @pl.when(i > 0)
    def _():
        blk = ((i - 1) * rows + row) // SEL_LEN
        onehot = jnp.where(lane == blk, 1.0, 0.0).astype(BF16)
        zaux = jnp.zeros((rows, LANES), BF16)
        for g in range(2):
            ks = (_rms(col(kvs_ref, g)) * gk_ref[1:2, :]).astype(BF16)
            kw = (_rms(col(kvw_ref, g)) * gk_ref[2:3, :]).astype(BF16)
            ks_ref[g] = jnp.concatenate([ks, onehot], axis=1)
            kw_ref[g] = jnp.concatenate([kw, zaux], axis=1)
            vs_ref[g] = col(kvs_ref, 2 + g).astype(BF16)
            vw_ref[g] = col(kvw_ref, 2 + g).astype(BF16)


def _kvprep(kvs, kvw, gk):
    b, t4, _ = kvs.shape
    t = t4 // 4
    rows = KPAD
    nblk = t // rows
    in_map = lambda bi, i: (bi, jnp.maximum(i - 1, 0), 0)
    out_map = lambda bi, i: (bi, 0, i, 0)
    kshape = jax.ShapeDtypeStruct((b, 2, KPAD + t, 2 * LANES), BF16)
    vshape = jax.ShapeDtypeStruct((b, 2, KPAD + t, LANES), BF16)
    return pl.pallas_call(
        _kvprep_kernel,
        grid=(b, nblk + 1),
        in_specs=[pl.BlockSpec((None, 4 * rows, LANES), in_map), pl.BlockSpec((None, 4 * rows, LANES), in_map),
                  pl.BlockSpec(gk.shape, lambda bi, i: (0, 0))],
        out_specs=[pl.BlockSpec((None, 2, rows, 2 * LANES), out_map), pl.BlockSpec((None, 2, rows, LANES), out_map),
                   pl.BlockSpec((None, 2, rows, 2 * LANES), out_map), pl.BlockSpec((None, 2, rows, LANES), out_map)],
        out_shape=(kshape, vshape, kshape, vshape),
        compiler_params=_cparams(("arbitrary", "arbitrary")),
        name="kvprep",
    )(kvs, kvw, gk)


def _compress_core(load_pair, nseg, wk_ref, wv_ref, pos_ref):
    w_refs = (wk_ref, wv_ref)
    accs = [jnp.zeros((2 * nseg + 16, 2 * LANES), F32) for _ in range(2)]
    for s2 in range(CMP_STRIDE // 2):
        for kv in range(2):
            parts = [load_pair(s2, 2 * kv + g) for g in range(2)]
            parts.append(pos_ref[kv, s2])
            lhs = jnp.concatenate(parts, axis=0)
            accs[kv] = accs[kv] + _dot(lhs, w_refs[kv][s2])
    outs = []
    for kv in range(2):
        y = accs[kv]
        post = y[2 * nseg:2 * nseg + 1, :LANES] + y[2 * nseg + 1:2 * nseg + 2, LANES:]
        per_g = []
        for g in range(2):
            y0 = y[g * nseg:(g + 1) * nseg, :LANES]
            y1 = y[g * nseg:(g + 1) * nseg, LANES:]
            per_g.append(y0 + pltpu.roll(y1, nseg - 1, 0) + post)
        outs.append(per_g)
    return outs


def _compress_prompt_kernel(x_ref, wk_ref, wv_ref, pos_ref, gk_ref, kc_ref, vc_ref):
    nseg = x_ref.shape[0] // (4 * CMP_STRIDE)
    load_x = lambda s, col: x_ref[pl.ds(4 * s + col, nseg, stride=4 * CMP_STRIDE), :]
    load_pair = lambda s2, col: jnp.concatenate([load_x(2 * s2, col), load_x(2 * s2 + 1, col)], axis=1).astype(BF16)
    ck, cv = _compress_core(load_pair, nseg, wk_ref, wv_ref, pos_ref)
    row = lax.broadcasted_iota(jnp.int32, (nseg, LANES), 0)
    lane = lax.broadcasted_iota(jnp.int32, (nseg, LANES), 1)
    valid = row < nseg - 1
    aux = jnp.where(valid, jnp.where(lane == row // 8, 1.0, 0.0), jnp.where(lane == PAD_LANE, 1.0, 0.0))
    back = KC_ROWS - KC_FRONT - nseg
    lane_f = lax.broadcasted_iota(jnp.int32, (KC_FRONT, LANES), 1)
    lane_b = lax.broadcasted_iota(jnp.int32, (back, LANES), 1)
    pad_f = jnp.concatenate([jnp.zeros((KC_FRONT, LANES), F32), jnp.where(lane_f == PAD_LANE, 1.0, 0.0)], axis=1)
    pad_b = jnp.concatenate([jnp.zeros((back, LANES), F32), jnp.where(lane_b == PAD_LANE, 1.0, 0.0)], axis=1)
    for g in range(2):
        kn = jnp.where(valid, _rms(ck[g]) * gk_ref[0:1, :], 0.0)
        kc_ref[g, 0:KC_FRONT, :] = pad_f
        kc_ref[g, KC_FRONT:KC_FRONT + nseg, :] = jnp.concatenate([kn, aux], axis=1)
        kc_ref[g, KC_FRONT + nseg:KC_ROWS, :] = pad_b
        vc_ref[g, 0:KC_FRONT, :] = jnp.zeros((KC_FRONT, LANES), F32)
        vc_ref[g, KC_FRONT:KC_FRONT + nseg, :] = jnp.where(valid, cv[g], 0.0)
        vc_ref[g, KC_FRONT + nseg:KC_ROWS, :] = jnp.zeros((back, LANES), F32)


def _compress_prompt(kvc, wk_pairs, wv_pairs, pos_rows, gk):
    b, t4, _ = kvc.shape
    full = lambda a: pl.BlockSpec(a.shape, lambda bi: (0,) * a.ndim)
    return pl.pallas_call(
        _compress_prompt_kernel,
        grid=(b,),
        in_specs=[pl.BlockSpec((None, t4, LANES), lambda bi: (bi, 0, 0)),
                  full(wk_pairs), full(wv_pairs), full(pos_rows), full(gk)],
        out_specs=[pl.BlockSpec((None, 2, KC_ROWS, 2 * LANES), lambda bi: (bi, 0, 0, 0)),
                   pl.BlockSpec((None, 2, KC_ROWS, LANES), lambda bi: (bi, 0, 0, 0))],
        out_shape=(jax.ShapeDtypeStruct((b, 2, KC_ROWS, 2 * LANES), F32),
                   jax.ShapeDtypeStruct((b, 2, KC_ROWS, LANES), F32)),
        compiler_params=_cparams(("arbitrary",)),
        name="compress_prompt",
    )(kvc, wk_pairs, wv_pairs, pos_rows, gk)


def _nsa_prompt_kernel(q_ref, gate_ref, kc_ref, vc_ref, ov_ref, ks_ref, vs_ref, kw_ref, vw_ref,
                       tc_ref, ts_ref, tw_ref, o_ref):
    qb = pl.program_id(2)
    t0 = qb * QBLOCK
    rq = 4 * QBLOCK
    q = q_ref[...]
    q4 = jnp.concatenate([q[:, r * LANES:(r + 1) * LANES] for r in range(4)], axis=0)
    lane = lax.broadcasted_iota(jnp.int32, (rq, LANES), 1)
    is_pad_lane = lane == PAD_LANE

    def q_aug(mb):
        return jnp.concatenate([q4, mb.astype(BF16)], axis=1)

    qa_pad = q_aug(jnp.where(is_pad_lane, NEG, 0.0))

    far_mask = ((lane < 32) & (lane >= qb - 2)) | is_pad_lane
    qa_cfar = q_aug(jnp.where(far_mask, NEG, 0.0))
    ncmp = 256
    near0 = pl.multiple_of(qb * 8, 8)
    k_far = kc_ref[KC_FRONT:KC_FRONT + ncmp, :].astype(BF16)
    k_near = kc_ref[pl.ds(near0, LANES), :].astype(BF16)
    s_far = _dot_nt(qa_cfar, k_far)
    s_near = _dot_nt(qa_pad, k_near) + tc_ref[...]
    m = jnp.maximum(jnp.max(s_far, axis=-1, keepdims=True), jnp.max(s_near, axis=-1, keepdims=True))
    p_far = jnp.where(s_far > HALF_NEG, jnp.exp(s_far - m), 0.0)
    p_near = jnp.where(s_near > HALF_NEG, jnp.exp(s_near - m), 0.0)
    l = jnp.sum(p_far, axis=-1, keepdims=True) + jnp.sum(p_near, axis=-1, keepdims=True)
    linv = 1.0 / jnp.where(l > 0.0, l, 1.0)
    v_far = vc_ref[KC_FRONT:KC_FRONT + ncmp, :].astype(BF16)
    v_near = vc_ref[pl.ds(near0, LANES), :].astype(BF16)
    o_c = (_dot(p_far.astype(BF16), v_far) + _dot(p_near.astype(BF16), v_near)) * linv

    pn_far = p_far * linv
    pn_near = p_near * linv
    ps_far = sum(pn_far[r * QBLOCK:(r + 1) * QBLOCK] for r in range(4))
    ps_near = sum(pn_near[r * QBLOCK:(r + 1) * QBLOCK] for r in range(4))
    ov_far = ov_ref[KC_FRONT:KC_FRONT + ncmp, :].astype(BF16)
    ov_near = ov_ref[pl.ds(near0, LANES), :].astype(BF16)
    imp = _dot_hilo(ps_far, ov_far) + _dot_hilo(ps_near, ov_near)
    nblk = 64
    imp_t = imp.T[:nblk]
    blk = lax.broadcasted_iota(jnp.int32, (nblk, QBLOCK), 0)
    qpos = t0 + lax.broadcasted_iota(jnp.int32, (nblk, QBLOCK), 1)
    cur = qpos // SEL_LEN
    forced = (blk == 0) | (blk == cur) | (blk == cur - 1)
    eligible = blk * SEL_LEN <= qpos
    val = jnp.where(forced, FORCE, jnp.where(eligible, imp_t, -1.0))
    rank = jnp.zeros((nblk, QBLOCK), F32)
    for i in range(nblk):
        ri = val[i:i + 1, :]
        beats = (ri > val) | ((ri == val) & (blk > i))
        rank = rank + jnp.where(beats, 1.0, 0.0)
    mb_t = jnp.where(rank < float(N_SEL), 0.0, NEG)
    row2 = lax.broadcasted_iota(jnp.int32, (LANES - nblk, QBLOCK), 0)
    mb_t = jnp.concatenate([mb_t, jnp.where(row2 == PAD_LANE - nblk, NEG, 0.0)], axis=0)
    mb = mb_t.T
    mb4 = jnp.concatenate([mb] * 4, axis=0)
    qa_snear = q_aug(mb4)
    qa_sfar = q_aug(jnp.where((lane < nblk) & (lane >= 2 * qb - 2), NEG, mb4))

    sn0 = pl.multiple_of(t0 + KPAD - QBLOCK, QBLOCK)
    s = _dot_nt(qa_snear, ks_ref[pl.ds(sn0, 2 * QBLOCK), :]) + ts_ref[...]
    m = jnp.max(s, axis=-1, keepdims=True)
    p = jnp.exp(s - m)
    l = jnp.sum(p, axis=-1, keepdims=True)
    acc = _dot(p.astype(BF16), vs_ref[pl.ds(sn0, 2 * QBLOCK), :])
    far_tile = 512
    n_far = (jnp.maximum(qb - 1, 0) * QBLOCK + far_tile - 1) // far_tile

    def far_body(j, carry):
        m, l, acc = carry
        r0 = pl.multiple_of(KPAD + j * far_tile, far_tile)
        s = _dot_nt(qa_sfar, ks_ref[pl.ds(r0, far_tile), :])
        m_new = jnp.maximum(m, jnp.max(s, axis=-1, keepdims=True))
        alpha = jnp.exp(m - m_new)
        p = jnp.exp(s - m_new)
        l = alpha * l + jnp.sum(p, axis=-1, keepdims=True)
        acc = alpha * acc + _dot(p.astype(BF16), vs_ref[pl.ds(r0, far_tile), :])
        return m_new, l, acc

    m, l, acc = lax.fori_loop(0, n_far, far_body, (m, l, acc))
    o_s = acc * (1.0 / l)

    w0 = pl.multiple_of(t0, QBLOCK)
    wlen = WINDOW + QBLOCK
    s = _dot_nt(qa_pad, kw_ref[pl.ds(w0, wlen), :]) + tw_ref[...]
    m = jnp.max(s, axis=-1, keepdims=True)
    p = jnp.exp(s - m)
    l = jnp.sum(p, axis=-1, keepdims=True)
    o_w = _dot(p.astype(BF16), vw_ref[pl.ds(w0, wlen), :]) * (1.0 / l)

    gt = gate_ref[...]
    outs = []
    for r in range(4):
        rs = slice(r * QBLOCK, (r + 1) * QBLOCK)
        outs.append(gt[:, 3 * r:3 * r + 1] * o_c[rs] + gt[:, 3 * r + 1:3 * r + 2] * o_s[rs]
                    + gt[:, 3 * r + 2:3 * r + 3] * o_w[rs])
    o_ref[...] = jnp.concatenate(outs, axis=1).astype(BF16)


def _nsa_prompt(q, gates_g, kc, vc, ov, ks, vs, kw, vw, tab_c, tab_s, tab_w):
    b, t, _ = q.shape
    nqb = t // QBLOCK
    kvspec = lambda a: pl.BlockSpec((None, None) + a.shape[2:], lambda bi, g, i: (bi, g, 0, 0))
    tabspec = lambda a: pl.BlockSpec((None,) + a.shape[1:], lambda bi, g, i: (g, 0, 0))
    return pl.pallas_call(
        _nsa_prompt_kernel,
        grid=(b, 2, nqb),
        in_specs=[pl.BlockSpec((None, QBLOCK, 512), lambda bi, g, i: (bi, i, g)),
                  pl.BlockSpec((None, None, QBLOCK, LANES), lambda bi, g, i: (bi, g, i, 0)),
                  kvspec(kc), kvspec(vc), pl.BlockSpec(ov.shape, lambda bi, g, i: (0, 0)),
                  kvspec(ks), kvspec(vs), kvspec(kw), kvspec(vw),
                  tabspec(tab_c), tabspec(tab_s), tabspec(tab_w)],
        out_specs=pl.BlockSpec((None, QBLOCK, 512), lambda bi, g, i: (bi, i, g)),
        out_shape=jax.ShapeDtypeStruct((b, t, 1024), BF16),
        compiler_params=_cparams(("arbitrary", "arbitrary", "arbitrary")),
        name="nsa_prompt",
    )(q, gates_g, kc, vc, ov, ks, vs, kw, vw, tab_c, tab_s, tab_w)


def _decode_attend(qf, g0rows, gk_row, k_tiles, v_tiles, bias, key_mask, new_row, bias_new):
    qg = qf * gk_row
    zero = jnp.zeros_like(qg)
    qbd = jnp.concatenate([jnp.where(g0rows, qg, zero), jnp.where(g0rows, zero, qg)], axis=1).astype(BF16)
    one = jnp.ones_like(qg)
    ones_bd = jnp.concatenate([jnp.where(g0rows, one, zero), jnp.where(g0rows, zero, one)], axis=1).astype(BF16)
    s_parts, q_parts = [], []
    for kt in k_tiles:
        k = kt()
        s_parts.append(_dot_nt(qbd, k.astype(BF16)))
        q_parts.append(_dot_nt(ones_bd, (k * k).astype(BF16)))
    s = jnp.concatenate(s_parts, axis=1)
    ssq = jnp.concatenate(q_parts, axis=1)
    s = s * lax.rsqrt(ssq * (1.0 / HEAD_DIM) + EPS) + bias
    if key_mask is not None:
        s = jnp.where(key_mask > 0.5, s, NEG)
    bc = lambda c: jnp.broadcast_to(new_row[c:c + 1, :], qf.shape)
    k_new = jnp.where(g0rows, bc(0), bc(1))
    v_new = jnp.where(g0rows, bc(2), bc(3))
    s_new = jnp.sum(qg * _rms(k_new), axis=-1, keepdims=True) + bias_new
    m = jnp.maximum(jnp.max(s, axis=-1, keepdims=True), s_new)
    p = jnp.exp(s - m)
    p_new = jnp.exp(s_new - m)
    l = jnp.sum(p, axis=-1, keepdims=True) + p_new
    o2 = jnp.zeros((qf.shape[0], 2 * LANES), F32)
    for i, vt in enumerate(v_tiles):
        o2 = o2 + _dot(p[:, i * LANES:(i + 1) * LANES].astype(BF16), vt().astype(BF16))
    o = jnp.where(g0rows, o2[:, :LANES], o2[:, LANES:]) + p_new * v_new
    return o * (1.0 / l)


def _nsa_sample_kernel(pt_ref, *refs, n_pages, n_sel_blocks, group):
    del pt_ref
    pages_c = refs[:group * n_pages]
    pages_s = refs[group * n_pages:2 * group * n_pages]
    rest = refs[2 * group * n_pages:]
    per_b_in, consts, outs = rest[:5], rest[5:-2], rest[-2:]
    chains = [_nsa_sample_one(pages_c[bb * n_pages:(bb + 1) * n_pages], pages_s[bb * n_pages:(bb + 1) * n_pages],
                              *[r.at[bb] for r in per_b_in], *consts, *[r.at[bb] for r in outs],
                              n_pages=n_pages, n_sel_blocks=n_sel_blocks) for bb in range(group)]
    while chains:
        chains = [c for c in chains if next(c, "done") != "done"]


def _nsa_sample_one(pages_c, pages_s, win_ref, q_ref, gate_ref, ksn_ref, kwn_ref, wk_ref, wv_ref, pos_ref, gk_ref,
                    bc_ref, bs_ref, bw_ref, bnew_ref, ov_ref, e_ref, perm_ref, o_ref, wout_ref, *,
                    n_pages, n_sel_blocks):
    qf = q_ref[...]
    nh = qf.shape[0]
    g0rows = lax.broadcasted_iota(jnp.int32, (nh, LANES), 0) < nh // 2
    lane = lax.broadcasted_iota(jnp.int32, (nh, LANES), 1)

    page = pages_c[0].shape[0] // 4
    nseg = n_pages * page // CMP_STRIDE
    perm = perm_ref[...]
    regrouped = []
    for pp in range(n_pages // 2):
        per_cp = []
        for cp in range(2):
            blk = jnp.concatenate(
                [jnp.concatenate([pages_c[2 * pp + i][pl.ds(2 * cp + c, page, stride=4), :] for c in range(2)], axis=1)
                 for i in range(2)], axis=0).astype(BF16)
            per_cp.append(_dot(perm, blk).astype(BF16))
        regrouped.append(per_cp)
    rows_per_tap = 2 * page // CMP_STRIDE
    yield

    def tap_rows(s, col):
        lo, hi = s * rows_per_tap, (s + 1) * rows_per_tap
        return jnp.concatenate([regrouped[pp][col // 2][lo:hi, (col % 2) * LANES:(col % 2 + 1) * LANES]
                                for pp in range(n_pages // 2)], axis=0)

    load_pair = lambda s2, col: jnp.concatenate([tap_rows(2 * s2, col), tap_rows(2 * s2 + 1, col)], axis=1)
    ck, cv = _compress_core(load_pair, nseg, wk_ref, wv_ref, pos_ref)
    yield
    kcn = jnp.concatenate([(_rms(ck[g]) * gk_ref[0:1, :]).astype(BF16) for g in range(2)], axis=0)
    s2 = _dot_nt(qf.astype(BF16), kcn)
    s = jnp.where(g0rows, s2[:, :nseg], s2[:, nseg:]) + bc_ref[...]
    m = jnp.max(s, axis=-1, keepdims=True)
    p = jnp.exp(s - m)
    linv = 1.0 / jnp.sum(p, axis=-1, keepdims=True)
    rowv = lax.broadcasted_iota(jnp.int32, (nseg, LANES), 0) < nseg - 1
    pb = p.astype(BF16)
    oc = [_dot(pb, jnp.where(rowv, cv[g], 0.0).astype(BF16)) for g in range(2)]
    o_c = jnp.where(g0rows, oc[0], oc[1]) * linv

    yield
    pn = p * linv
    s0 = jnp.sum(jnp.where(g0rows, pn, 0.0), axis=0, keepdims=True)
    s1 = jnp.sum(jnp.where(g0rows, 0.0, pn), axis=0, keepdims=True)
    psum = jnp.where(g0rows, jnp.broadcast_to(s0, pn.shape), jnp.broadcast_to(s1, pn.shape))
    imp = _dot_hilo(psum, ov_ref[...].astype(BF16))
    cur = n_sel_blocks - 1
    forced = (lane == 0) | (lane == cur) | (lane == cur - 1)
    val = jnp.where(lane >= n_sel_blocks, -2.0, jnp.where(forced, FORCE, imp))
    rank = jnp.zeros_like(val)
    for i in range(n_sel_blocks):
        ci = jnp.broadcast_to(val[:, i:i + 1], val.shape)
        beats = (ci > val) | ((ci == val) & (lane > i))
        rank = rank + jnp.where(beats, 1.0, 0.0)
    sel = jnp.where((rank < float(N_SEL)) & (lane < n_sel_blocks), 1.0, 0.0)
    key_mask = _dot(sel.astype(BF16), e_ref[...])

    yield
    def tiles(ref_list, col):
        def tile(rf, i):
            rows = lambda c: rf[pl.ds(4 * LANES * i + c, LANES, stride=4), :]
            return lambda: jnp.concatenate([rows(col), rows(col + 1)], axis=1)
        return [tile(rf, i) for rf in ref_list for i in range(rf.shape[0] // (4 * LANES))]

    b_new = bnew_ref[:, 0:1]
    o_s = _decode_attend(qf, g0rows, gk_ref[1:2, :], tiles(pages_s, 0), tiles(pages_s, 2),
                         bs_ref[...], key_mask, ksn_ref[...], b_new)
    yield
    o_w = _decode_attend(qf, g0rows, gk_ref[2:3, :], tiles([win_ref], 0), tiles([win_ref], 2),
                         bw_ref[...], None, kwn_ref[...], b_new)
    yield
    gt = gate_ref[...]
    o_ref[...] = gt[:, 0:1] * o_c + gt[:, 1:2] * o_s + gt[:, 2:3] * o_w

    keep = wout_ref.shape[0] - 4
    drop = win_ref.shape[0] - keep
    wout_ref[0:keep, :] = win_ref[drop:drop + keep, :]
    wout_ref[keep:keep + 4, :] = kwn_ref[...]


def _nsa_sample(page_table, cache_c, cache_s, win, q, gates, ks_new, kw_new, wk_pairs, wv_pairs, pos_rows, gk,
                bias_c, bias_s, bias_w, bias_new, ov, expand):
    nb, n_pages = page_table.shape
    page_rows = cache_c.shape[1]
    n_sel_blocks = -(-(n_pages * (page_rows // 4) + 1) // SEL_LEN)
    group = SAMPLE_GROUP if nb % SAMPLE_GROUP == 0 else 1
    kern = functools.partial(_nsa_sample_kernel, n_pages=n_pages, n_sel_blocks=n_sel_blocks, group=group)
    win_rows_out = 4 * min(WINDOW, n_pages * (page_rows // 4) + 1)
    page = page_rows // 4
    assert n_pages % 2 == 0 and page % CMP_STRIDE == 0
    segs = page // CMP_STRIDE
    i_, n_, s_ = np.meshgrid(np.arange(2), np.arange(segs), np.arange(CMP_STRIDE), indexing="ij")
    perm_np = np.zeros((2 * page, 2 * page), np.float32)
    perm_np[(s_ * 2 * segs + i_ * segs + n_).ravel(), (i_ * page + CMP_STRIDE * n_ + s_).ravel()] = 1.0
    perm = jnp.asarray(perm_np, BF16)
    page_spec = lambda bb, p: pl.BlockSpec((None, page_rows, LANES),
                                           lambda bi, pt, bb=bb, p=p: (pt[group * bi + bb, p], 0, 0))
    pages = [page_spec(bb, p) for bb in range(group) for p in range(n_pages)]
    full = lambda a: pl.BlockSpec(a.shape, lambda bi, pt: (0,) * a.ndim)
    per_b = lambda a: pl.BlockSpec((group,) + a.shape[1:], lambda bi, pt: (bi,) + (0,) * (a.ndim - 1))
    grid_spec = pltpu.PrefetchScalarGridSpec(
        num_scalar_prefetch=1,
        grid=(nb // group,),
        in_specs=(pages + pages
                  + [per_b(win), per_b(q), per_b(gates), per_b(ks_new), per_b(kw_new),
                     full(wk_pairs), full(wv_pairs), full(pos_rows), full(gk),
                     full(bias_c), full(bias_s), full(bias_w), full(bias_new), full(ov), full(expand), full(perm)]),
        out_specs=[pl.BlockSpec((group,) + q.shape[1:], lambda bi, pt: (bi, 0, 0)),
                   pl.BlockSpec((group, win_rows_out, LANES), lambda bi, pt: (bi, 0, 0))],
    )
    return pl.pallas_call(
        kern,
        grid_spec=grid_spec,
        out_shape=(jax.ShapeDtypeStruct(q.shape, F32), jax.ShapeDtypeStruct((nb, win_rows_out, LANES), F32)),
        compiler_params=_cparams(("arbitrary",)),
        name="nsa_sample",
    )(page_table, *([cache_c] * (group * n_pages)), *([cache_s] * (group * n_pages)), win, q, gates, ks_new, kw_new,
      wk_pairs, wv_pairs, pos_rows, gk, bias_c, bias_s, bias_w, bias_new, ov, expand, perm)


def _merge_kernel(h_ref, oa_ref, ob_ref, wga_ref, wgb_ref, wg_ref, wn_ref, mix_ref):
    h = h_ref[...]
    ga = _sigmoid(_dot(h, wga_ref[...]))
    gb = _sigmoid(_dot(h, wgb_ref[...]))
    mix_ref[...] = (ga * _dot(oa_ref[...], wg_ref[...]) + gb * _dot(ob_ref[...], wn_ref[...])).astype(BF16)


def _merge(h, oa, ob, wga, wgb, wg, wn):
    n, d = h.shape
    tm = 512 if n % 512 == 0 else n
    tn = 512
    row = lambda w: pl.BlockSpec((tm, w), lambda i, j: (i, 0))
    col = lambda k: pl.BlockSpec((k, tn), lambda i, j: (0, j))
    return pl.pallas_call(
        _merge_kernel,
        grid=(n // tm, d // tn),
        in_specs=[row(d), row(oa.shape[1]), row(ob.shape[1]), col(d), col(d), col(wg.shape[0]), col(wn.shape[0])],
        out_specs=pl.BlockSpec((tm, tn), lambda i, j: (i, j)),
        out_shape=jax.ShapeDtypeStruct((n, d), BF16),
        compiler_params=_cparams(("arbitrary", "arbitrary")),
        name="merge",
    )(h, oa, ob, wga, wgb, wg, wn)


def _outproj_kernel(x_ref, mix_ref, wout_ref, gmlp_ref, x1_ref, hm_ref):
    x1 = x_ref[...] + _dot(mix_ref[...], wout_ref[...])
    x1_ref[...] = x1
    hm_ref[...] = (_rms(x1) * gmlp_ref[...]).astype(BF16)


def _outproj(x, mix, wout, gmlp):
    n, d = x.shape
    tm = 512 if n % 512 == 0 else n
    row = pl.BlockSpec((tm, d), lambda i: (i, 0))
    return pl.pallas_call(
        _outproj_kernel,
        grid=(n // tm,),
        in_specs=[row, row, pl.BlockSpec(wout.shape, lambda i: (0, 0)), pl.BlockSpec(gmlp.shape, lambda i: (0, 0))],
        out_specs=[row, row],
        out_shape=(jax.ShapeDtypeStruct((n, d), F32), jax.ShapeDtypeStruct((n, d), BF16)),
        compiler_params=_cparams(("arbitrary",)),
        name="outproj",
    )(x, mix, wout, gmlp)


def _ffn_kernel(hm_ref, x1_ref, wup_ref, wdown_ref, y_ref):
    f = pl.program_id(1)

    @pl.when(f == 0)
    def _():
        y_ref[...] = x1_ref[...]

    hid = jnp.maximum(_dot(hm_ref[...], wup_ref[...]), 0.0)
    y_ref[...] += _dot((hid * hid).astype(BF16), wdown_ref[...])


def _ffn(hm, x1, wup, wdown):
    n, d = hm.shape
    dff = wup.shape[1]
    tm = 512 if n % 512 == 0 else n
    tf = 1024
    row = pl.BlockSpec((tm, d), lambda i, f: (i, 0))
    return pl.pallas_call(
        _ffn_kernel,
        grid=(n // tm, dff // tf),
        in_specs=[row, row, pl.BlockSpec((d, tf), lambda i, f: (0, f)), pl.BlockSpec((tf, d), lambda i, f: (f, 0))],
        out_specs=row,
        out_shape=jax.ShapeDtypeStruct((n, d), F32),
        compiler_params=_cparams(("arbitrary", "arbitrary")),
        name="ffn",
    )(hm, x1, wup, wdown)


def _bucket(rel, valid):
    n = np.maximum(rel, 0)
    max_exact = N_BUCKETS // 2
    nf = np.maximum(n, 1).astype(np.float32)
    large = max_exact + (np.log(nf / np.float32(max_exact)) / np.float32(math.log(MAX_DIST / max_exact))
                         * np.float32(N_BUCKETS - max_exact)).astype(np.int32)
    large = np.minimum(large, N_BUCKETS - 1)
    return np.where(valid, np.where(n < max_exact, n, large), -1).astype(np.int32)


def _bias_tables_kernel(rb_ref, *refs, shifts):
    n = len(shifts)
    nbk, nh = rb_ref.shape
    for b_ref, o_ref, shift in zip(refs[:n], refs[n:], shifts):
        b = b_ref[...]
        rows = []
        for h in range(nh):
            sh = rb_ref[nbk - 1, h] if shift else 0.0
            acc = jnp.full(b.shape, NEG, F32)
            for k in range(nbk):
                acc = jnp.where(b == k, rb_ref[k, h] - sh, acc)
            if len(o_ref.shape) == 3:
                o_ref[h] = acc
            else:
                rows.append(acc)
        if rows:
            o_ref[...] = jnp.concatenate(rows, axis=0)


def _bias_tables(rel_bias, buckets, shifts):
    nh = rel_bias.shape[1]
    shapes = [jax.ShapeDtypeStruct((nh,) + (b.shape if b.shape[0] > 1 else b.shape[1:]), F32) for b in buckets]
    vm = pl.BlockSpec(memory_space=pltpu.VMEM)
    return pl.pallas_call(
        functools.partial(_bias_tables_kernel, shifts=tuple(shifts)),
        in_specs=[pl.BlockSpec(memory_space=pltpu.SMEM)] + [vm] * len(buckets),
        out_specs=[vm] * len(buckets),
        out_shape=shapes,
        name="bias_tables",
    )(rel_bias, *[jnp.asarray(b) for b in buckets])


def _overlap(nc, ns):
    i = np.arange(nc)[:, None] * CMP_STRIDE
    j = np.arange(ns)[None, :] * SEL_LEN
    return ((i < j + SEL_LEN) & (i + CMP_LEN > j)).astype(np.float32)


def _compress_weights(w, pos):
    s = np.arange(0, CMP_STRIDE, 2)
    top = jnp.concatenate([w[s], w[CMP_STRIDE + s]], axis=2)
    bot = jnp.concatenate([w[s + 1], w[CMP_STRIDE + s + 1]], axis=2)
    tiles = jnp.concatenate([top, bot], axis=1).astype(BF16)
    row_a = jnp.concatenate([pos[s], pos[s + 1]], axis=1)
    row_b = jnp.concatenate([pos[CMP_STRIDE + s], pos[CMP_STRIDE + s + 1]], axis=1)
    rows = jnp.zeros((len(s), 16, 2 * HEAD_DIM), F32).at[:, 0].set(row_a).at[:, 1].set(row_b)
    return tiles, rows.astype(BF16)


def _dense_tail(x, h, oa, ob, wga, wgb, wg, wn, wout, gmlp, wup, wdown):
    mix = _merge(h, oa, ob, wga, wgb, wg, wn)
    x1, hm = _outproj(x, mix, wout, gmlp)
    return _ffn(hm, x1, wup, wdown)


def kernel(x_prompt, x_sample, cache_cmp_kv, cache_sel_kv, state_win_kv, page_table, rel_bias, g_mix_norm, w_in,
           g_sgu, w_sgu, b_sgu, g_q, g_k, pos_cmp_k, w_cmp_k, pos_cmp_v, w_cmp_v, w_proj_gmlp, w_proj_nsa, w_out,
           g_mlp_norm, w_up, w_down):
    depth = g_mix_norm.shape[0]
    assert depth == 1
    l = 0
    bsz, seq, d = x_prompt.shape
    nb = x_sample.shape[0]
    assert x_sample.shape[1] == 1 and seq % KPAD == 0
    d_gm = g_sgu.shape[1]
    n_heads = rel_bias.shape[1]
    d_nsa = n_heads * HEAD_DIM
    kvw_cols = 2 * (n_heads // 4) * HEAD_DIM
    n_gate = 3 * n_heads
    c_q = 2 * d_gm
    c_kv = c_q + d_nsa
    c_gate = c_kv + 3 * kvw_cols
    c_ga = c_gate + n_gate
    c_gb = c_ga + d

    w = w_in[l]
    w_cat = jnp.pad(w[:, :c_ga], ((0, 0), (0, 5 * 1024 - c_ga))).astype(BF16)
    wga = w[:, c_ga:c_gb].astype(BF16)
    wgb = w[:, c_gb:].astype(BF16)
    wg = w_proj_gmlp[l].astype(BF16)
    wn = w_proj_nsa[l].astype(BF16)
    wout = w_out[l].astype(BF16)
    wup = w_up[l].astype(BF16)
    wdown = w_down[l].astype(BF16)
    gmix = g_mix_norm[l][None]
    gsgu = g_sgu[l][None]
    gq = g_q[l][None]
    gk = g_k[l]
    gmlp = g_mlp_norm[l][None]
    wk_pairs, posk_rows = _compress_weights(w_cmp_k[l], pos_cmp_k[l])
    wv_pairs, posv_rows = _compress_weights(w_cmp_v[l], pos_cmp_v[l])
    pos_rows = jnp.stack([posk_rows, posv_rows])
    n_groups = w_sgu.shape[1]
    b_exp = jnp.repeat(b_sgu[l].T, d_gm // n_groups, axis=1)
    w00 = jnp.repeat(w_sgu[l][:, 0, 0], d_gm // n_groups)[None]
    b00 = jnp.repeat(b_sgu[l][:, 0], d_gm // n_groups)[None]

    xp = x_prompt.reshape(bsz * seq, d)
    h_p, v_p, oa_p, q_p, kvc_p, kvs_p, kvw_p, gates_p = _inproj(
        xp, gmix, w_cat, gsgu, gq, w_sgu[l], b_exp, chunked=True)
    kvs3 = kvs_p.reshape(bsz, 4 * seq, HEAD_DIM)
    kvw3 = kvw_p.reshape(bsz, 4 * seq, HEAD_DIM)
    kvc3 = kvc_p.reshape(bsz, 4 * seq, HEAD_DIM)
    ks, vs, kw, vw = _kvprep(kvs3, kvw3, gk)
    kc, vc = _compress_prompt(kvc3, wk_pairs, wv_pairs, pos_rows, gk)

    n_pool, page = cache_cmp_kv.shape[1], cache_cmp_kv.shape[2]
    n_pages = page_table.shape[1]
    past = n_pages * page
    nwin = state_win_kv.shape[2]
    n_cmp_s = (past + 1 - CMP_LEN) // CMP_STRIDE + 1
    n_sel_s = -(-(past + 1) // SEL_LEN)
    nseg_s = past // CMP_STRIDE
    qi = np.arange(QBLOCK)[:, None]
    rel_s = qi + QBLOCK - np.arange(2 * QBLOCK)[None, :]
    rel_w = qi + WINDOW - np.arange(WINDOW + QBLOCK)[None, :]
    rel_c = qi + (KC_FRONT * CMP_STRIDE - CMP_LEN + 1) - CMP_STRIDE * np.arange(LANES)[None, :]
    srel_c = (past - (np.arange(nseg_s) * CMP_STRIDE + CMP_LEN - 1))[None]
    srel_s = (past - np.arange(past))[None]
    srel_w = (nwin - np.arange(nwin))[None]
    buckets = [
        _bucket(rel_c, rel_c >= 0), _bucket(rel_s, rel_s >= 0), _bucket(rel_w, (rel_w >= 0) & (rel_w < WINDOW)),
        _bucket(srel_c, (srel_c >= 0) & (np.arange(nseg_s)[None] < n_cmp_s)), _bucket(srel_s, srel_s >= 0),
        _bucket(srel_w, srel_w < WINDOW), np.zeros((1, LANES), np.int32)]
    tab_c, tab_s, tab_w, bias_c, bias_s, bias_w, bias_new = _bias_tables(
        rel_bias, buckets, [True, True, False, False, False, False, False])
    tab_c, tab_s, tab_w = [t.reshape(2, (n_heads // 2) * QBLOCK, t.shape[-1]) for t in (tab_c, tab_s, tab_w)]
    n_cmp = (seq - CMP_LEN) // CMP_STRIDE + 1
    n_selb = seq // SEL_LEN
    ov_p = np.zeros((KC_ROWS, LANES), np.float32)
    ov_p[KC_FRONT:KC_FRONT + n_cmp, :n_selb] = _overlap(n_cmp, n_selb)
    ov_p = jnp.asarray(ov_p)
    gates_g = gates_p[:, :n_gate].reshape(bsz, seq, 2, n_gate // 2).transpose(0, 2, 1, 3)
    gates_g = jnp.pad(gates_g, ((0, 0), (0, 0), (0, 0), (0, LANES - n_gate // 2)))
    ob_p = _nsa_prompt(q_p.reshape(bsz, seq, d_nsa), gates_g, kc, vc, ov_p, ks, vs, kw, vw, tab_c, tab_s, tab_w)
    y_p = _dense_tail(xp, h_p, oa_p, ob_p.reshape(bsz * seq, d_nsa), wga, wgb, wg, wn, wout, gmlp, wup, wdown)

    xs = x_sample.reshape(nb, d)
    h_s, v_s, oa_s, q_s, kvc_s, kvs_s, kvw_s, gates_s = _inproj(
        xs, gmix, w_cat, gsgu, gq, w00, b00, chunked=False)
    ov_s = np.zeros((nseg_s, LANES), np.float32)
    ov_s[:n_cmp_s, :n_sel_s] = _overlap(n_cmp_s, n_sel_s)
    expand = jnp.asarray(np.arange(LANES)[:, None] == (np.arange(past)[None, :] // SEL_LEN), BF16)
    gates_h = jnp.pad(gates_s[:, :n_gate].reshape(nb, n_heads, 3), ((0, 0), (0, 0), (0, LANES - 3)))
    lin = lambda a: a[l].reshape(a.shape[1], -1, HEAD_DIM)
    win_lin = lin(state_win_kv)
    ob_s, win_new = _nsa_sample(
        page_table, lin(cache_cmp_kv), lin(cache_sel_kv), win_lin,
        q_s.astype(F32).reshape(nb, n_heads, HEAD_DIM), gates_h,
        kvs_s.reshape(nb, 4, HEAD_DIM), kvw_s.reshape(nb, 4, HEAD_DIM), wk_pairs, wv_pairs, pos_rows, gk,
        bias_c, bias_s, bias_w, bias_new, jnp.asarray(ov_s), expand)
    y_s = _dense_tail(xs, h_s, oa_s, ob_s.reshape(nb, d_nsa).astype(BF16), wga, wgb, wg, wn, wout, gmlp, wup, wdown)

    n_kv = n_heads // 4
    kv6 = lambda a, b_, t_: a.reshape(1, b_, t_, 2, n_kv, HEAD_DIM)
    nw_p = min(WINDOW, seq)
    last = ((seq - 1) // CHUNK) * CHUNK
    nw_s = min(WINDOW, past + 1)
    return (y_p.reshape(bsz, seq, d), y_s.reshape(nb, 1, d),
            kv6(kvc3, bsz, seq), kv6(kvs3, bsz, seq), kv6(kvw3[:, 4 * (seq - nw_p):], bsz, nw_p),
            v_p.reshape(bsz, seq, d_gm)[:, last:][None],
            kv6(kvc_s, nb, 1), kv6(kvs_s, nb, 1), kv6(win_new, nb, nw_s),
            v_s.reshape(1, nb, 1, d_gm))
```

```python
import functools
import math

import numpy as np
import jax
import jax.numpy as jnp
from jax import lax
from jax.experimental import pallas as pl
from jax.experimental.pallas import tpu as pltpu

F32 = jnp.float32
BF16 = jnp.bfloat16

HEAD_DIM = 128
CHUNK = 128
CMP_LEN = 32
CMP_STRIDE = 16
SEL_LEN = 64
N_SEL = 16
WINDOW = 512
N_BUCKETS = 32
MAX_DIST = 128
QBLOCK = 128
EPS = 1e-6
NEG = -1e30
HALF_NEG = -5e29
FORCE = 1e6

LANES = 128
PAD_LANE = 64
KPAD = 512
KC_FRONT = 16
KC_ROWS = 376
SAMPLE_GROUP = 2
VMEM_LIMIT = 56 * 1024 * 1024


def _cparams(sem):
    return pltpu.CompilerParams(dimension_semantics=sem, vmem_limit_bytes=VMEM_LIMIT)


def _dot(a, b):
    return jnp.dot(a, b, preferred_element_type=F32)


def _dot_nt(a, b):
    return lax.dot_general(a, b, (((1,), (1,)), ((), ())), preferred_element_type=F32)


def _dot_hilo(a, b_bf16):
    hi = a.astype(BF16)
    lo = (a - hi.astype(F32)).astype(BF16)
    return _dot(hi, b_bf16) + _dot(lo, b_bf16)


def _rms(x):
    return x * lax.rsqrt(jnp.mean(x * x, axis=-1, keepdims=True) + EPS)


def _gelu(x):
    c = math.sqrt(2.0 / math.pi)
    return 0.5 * x * (1.0 + jnp.tanh(c * (x + 0.044715 * (x * x * x))))


def _sigmoid(x):
    return 1.0 / (1.0 + jnp.exp(-x))


def _inproj_kernel(x_ref, gmix_ref, w_ref, gsgu_ref, gq_ref, wsg_ref, bsg_ref,
                   h_ref, v_ref, oa_ref, q_ref, kvc_ref, kvs_ref, kvw_ref, gate_ref, *rest, chunked, tm):
    h_s, u_s = rest[-2:]
    j = pl.program_id(1)
    w = w_ref[...].astype(BF16)
    if len(rest) == 3:
        rest[0][...] = w

    @pl.when(j == 0)
    def _():
        hb = (_rms(x_ref[...]) * gmix_ref[...]).astype(BF16)
        h_s[...] = hb
        h_ref[...] = hb

    z = _dot(h_s[...], w)

    @pl.when(j == 0)
    def _():
        u_s[...] = _gelu(z)

    @pl.when(j == 1)
    def _():
        v = _rms(_gelu(z)) * gsgu_ref[...]
        v_ref[...] = v
        if chunked:
            row = lax.broadcasted_iota(jnp.int32, (CHUNK, CHUNK), 0)
            col = lax.broadcasted_iota(jnp.int32, (CHUNK, CHUNK), 1)
            n_groups = v.shape[1] // LANES
            for g in range(n_groups):
                wm = jnp.where(row >= col, wsg_ref[g], 0.0).astype(BF16)
                cs = slice(g * LANES, (g + 1) * LANES)
                for c in range(tm // CHUNK):
                    rs = slice(c * CHUNK, (c + 1) * CHUNK)
                    s = _dot(wm, v[rs, cs].astype(BF16)) + bsg_ref[:, cs]
                    oa_ref[rs, cs] = (u_s[rs, cs] * s).astype(BF16)
        else:
            oa_ref[...] = (u_s[...] * (v * wsg_ref[...] + bsg_ref[...])).astype(BF16)

    @pl.when(j == 2)
    def _():
        scale = HEAD_DIM ** -0.5
        for hd in range(z.shape[1] // HEAD_DIM):
            cs = slice(hd * HEAD_DIM, (hd + 1) * HEAD_DIM)
            q_ref[:, cs] = (_rms(z[:, cs]) * gq_ref[...] * scale).astype(BF16)

    def store_kv(ref, zz):
        for c in range(4):
            ref[pl.ds(c, tm, stride=4), :] = zz[:, c * LANES:(c + 1) * LANES]

    @pl.when(j == 3)
    def _():
        store_kv(kvc_ref, z[:, :512])
        store_kv(kvs_ref, z[:, 512:])

    @pl.when(j == 4)
    def _():
        store_kv(kvw_ref, z[:, :512])
        gate_ref[...] = _sigmoid(z[:, 512:640])


def _inproj(x, gmix, w_cat, gsgu, gq, wsg, bsg, *, chunked, n_col_tiles):
    n, d = x.shape
    tm = 512 if n % 512 == 0 else n
    tn = 1024
    emit = w_cat.dtype != BF16
    assert not emit or n == tm
    kern = functools.partial(_inproj_kernel, chunked=chunked, tm=tm)
    full = lambda a: pl.BlockSpec(a.shape, lambda i, j: (0,) * a.ndim)
    row = lambda w: pl.BlockSpec((tm, w), lambda i, j: (i, 0))
    kvrow = pl.BlockSpec((4 * tm, LANES), lambda i, j: (i, 0))
    out_shapes = (
        jax.ShapeDtypeStruct((n, d), BF16),
        jax.ShapeDtypeStruct((n, 1024), F32),
        jax.ShapeDtypeStruct((n, 1024), BF16),
        jax.ShapeDtypeStruct((n, 1024), BF16),
        jax.ShapeDtypeStruct((4 * n, LANES), F32),
        jax.ShapeDtypeStruct((4 * n, LANES), F32),
        jax.ShapeDtypeStruct((4 * n, LANES), F32),
        jax.ShapeDtypeStruct((n, LANES), F32),
    )
    wspec = pl.BlockSpec((d, tn), lambda i, j: (0, j))
    out_specs = [row(d), row(1024), row(1024), row(1024), kvrow, kvrow, kvrow, row(LANES)]
    if emit:
        out_shapes += (jax.ShapeDtypeStruct((d, n_col_tiles * tn), BF16),)
        out_specs.append(wspec)
    return pl.pallas_call(
        kern,
        grid=(n // tm, n_col_tiles),
        in_specs=[row(d), full(gmix), wspec, full(gsgu), full(gq), full(wsg), full(bsg)],
        out_specs=out_specs,
        out_shape=out_shapes,
        scratch_shapes=[pltpu.VMEM((tm, d), BF16), pltpu.VMEM((tm, 1024), F32)],
        compiler_params=_cparams(("arbitrary", "arbitrary")),
        name="inproj",
    )(x, gmix, w_cat, gsgu, gq, wsg, bsg)


def _kvprep_kernel(kvs_ref, kvw_ref, gk_ref, ks_ref, vs_ref, kw_ref, vw_ref):
    i = pl.program_id(1)
    rows = kvs_ref.shape[0] // 4
    col = lambda ref, c: ref[pl.ds(c, rows, stride=4), :]
    lane = lax.broadcasted_iota(jnp.int32, (rows, LANES), 1)
    row = lax.broadcasted_iota(jnp.int32, (rows, LANES), 0)

    @pl.when(i == 0)
    def _():
        aux = jnp.where(lane == PAD_LANE, 1.0, 0.0).astype(BF16)
        zk = jnp.zeros((rows, LANES), BF16)
        for g in range(2):
            ks_ref[g] = jnp.concatenate([zk, aux], axis=1)
            kw_ref[g] = jnp.concatenate([zk, aux], axis=1)
            vs_ref[g] = zk
            vw_ref[g] = zk

    @pl.when(i > 0)
    def _():
        blk = ((i - 1) * rows + row) // SEL_LEN
        onehot = jnp.where(lane == blk, 1.0, 0.0).astype(BF16)
        zaux = jnp.zeros((rows, LANES), BF16)
        for g in range(2):
            ks = (_rms(col(kvs_ref, g)) * gk_ref[1:2, :]).astype(BF16)
            kw = (_rms(col(kvw_ref, g)) * gk_ref[2:3, :]).astype(BF16)
            ks_ref[g] = jnp.concatenate([ks, onehot], axis=1)
            kw_ref[g] = jnp.concatenate([kw, zaux], axis=1)
            vs_ref[g] = col(kvs_ref, 2 + g).astype(BF16)
            vw_ref[g] = col(kvw_ref, 2 + g).astype(BF16)


def _kvprep(kvs, kvw, gk):
    b, t4, _ = kvs.shape
    t = t4 // 4
    rows = KPAD
    nblk = t // rows
    in_map = lambda bi, i: (bi, jnp.maximum(i - 1, 0), 0)
    out_map = lambda bi, i: (bi, 0, i, 0)
    kshape = jax.ShapeDtypeStruct((b, 2, KPAD + t, 2 * LANES), BF16)
    vshape = jax.ShapeDtypeStruct((b, 2, KPAD + t, LANES), BF16)
    return pl.pallas_call(
        _kvprep_kernel,
        grid=(b, nblk + 1),
        in_specs=[pl.BlockSpec((None, 4 * rows, LANES), in_map), pl.BlockSpec((None, 4 * rows, LANES), in_map),
                  pl.BlockSpec(gk.shape, lambda bi, i: (0, 0))],
        out_specs=[pl.BlockSpec((None, 2, rows, 2 * LANES), out_map), pl.BlockSpec((None, 2, rows, LANES), out_map),
                   pl.BlockSpec((None, 2, rows, 2 * LANES), out_map), pl.BlockSpec((None, 2, rows, LANES), out_map)],
        out_shape=(kshape, vshape, kshape, vshape),
        compiler_params=_cparams(("arbitrary", "arbitrary")),
        name="kvprep",
    )(kvs, kvw, gk)


def _compress_core(load_pair, nseg, wk_ref, wv_ref, pos_ref):
    w_refs = (wk_ref, wv_ref)
    accs = [jnp.zeros((2 * nseg + 16, 2 * LANES), F32) for _ in range(2)]
    for s2 in range(CMP_STRIDE // 2):
        for kv in range(2):
            parts = [load_pair(s2, 2 * kv + g) for g in range(2)]
            parts.append(pos_ref[kv, s2])
            lhs = jnp.concatenate(parts, axis=0)
            accs[kv] = accs[kv] + _dot(lhs, w_refs[kv][s2])
    outs = []
    for kv in range(2):
        y = accs[kv]
        post = y[2 * nseg:2 * nseg + 1, :LANES] + y[2 * nseg + 1:2 * nseg + 2, LANES:]
        per_g = []
        for g in range(2):
            y0 = y[g * nseg:(g + 1) * nseg, :LANES]
            y1 = y[g * nseg:(g + 1) * nseg, LANES:]
            per_g.append(y0 + pltpu.roll(y1, nseg - 1, 0) + post)
        outs.append(per_g)
    return outs


def _compress_prompt_kernel(x_ref, wk_ref, wv_ref, pos_ref, gk_ref, kc_ref, vc_ref):
    nseg = x_ref.shape[0] // (4 * CMP_STRIDE)
    load_x = lambda s, col: x_ref[pl.ds(4 * s + col, nseg, stride=4 * CMP_STRIDE), :]
    load_pair = lambda s2, col: jnp.concatenate([load_x(2 * s2, col), load_x(2 * s2 + 1, col)], axis=1).astype(BF16)
    ck, cv = _compress_core(load_pair, nseg, wk_ref, wv_ref, pos_ref)
    row = lax.broadcasted_iota(jnp.int32, (nseg, LANES), 0)
    lane = lax.broadcasted_iota(jnp.int32, (nseg, LANES), 1)
    valid = row < nseg - 1
    aux = jnp.where(valid, jnp.where(lane == row // 8, 1.0, 0.0), jnp.where(lane == PAD_LANE, 1.0, 0.0))
    back = KC_ROWS - KC_FRONT - nseg
    lane_f = lax.broadcasted_iota(jnp.int32, (KC_FRONT, LANES), 1)
    lane_b = lax.broadcasted_iota(jnp.int32, (back, LANES), 1)
    pad_f = jnp.concatenate([jnp.zeros((KC_FRONT, LANES), F32), jnp.where(lane_f == PAD_LANE, 1.0, 0.0)], axis=1)
    pad_b = jnp.concatenate([jnp.zeros((back, LANES), F32), jnp.where(lane_b == PAD_LANE, 1.0, 0.0)], axis=1)
    for g in range(2):
        kn = jnp.where(valid, _rms(ck[g]) * gk_ref[0:1, :], 0.0)
        kc_ref[g, 0:KC_FRONT, :] = pad_f
        kc_ref[g, KC_FRONT:KC_FRONT + nseg, :] = jnp.concatenate([kn, aux], axis=1)
        kc_ref[g, KC_FRONT + nseg:KC_ROWS, :] = pad_b
        vc_ref[g, 0:KC_FRONT, :] = jnp.zeros((KC_FRONT, LANES), F32)
        vc_ref[g, KC_FRONT:KC_FRONT + nseg, :] = jnp.where(valid, cv[g], 0.0)
        vc_ref[g, KC_FRONT + nseg:KC_ROWS, :] = jnp.zeros((back, LANES), F32)


def _compress_prompt(kvc, wk_pairs, wv_pairs, pos_rows, gk):
    b, t4, _ = kvc.shape
    full = lambda a: pl.BlockSpec(a.shape, lambda bi: (0,) * a.ndim)
    return pl.pallas_call(
        _compress_prompt_kernel,
        grid=(b,),
        in_specs=[pl.BlockSpec((None, t4, LANES), lambda bi: (bi, 0, 0)),
                  full(wk_pairs), full(wv_pairs), full(pos_rows), full(gk)],
        out_specs=[pl.BlockSpec((None, 2, KC_ROWS, 2 * LANES), lambda bi: (bi, 0, 0, 0)),
                   pl.BlockSpec((None, 2, KC_ROWS, LANES), lambda bi: (bi, 0, 0, 0))],
        out_shape=(jax.ShapeDtypeStruct((b, 2, KC_ROWS, 2 * LANES), F32),
                   jax.ShapeDtypeStruct((b, 2, KC_ROWS, LANES), F32)),
        compiler_params=_cparams(("arbitrary",)),
        name="compress_prompt",
    )(kvc, wk_pairs, wv_pairs, pos_rows, gk)


def _nsa_prompt_kernel(q_ref, gate_ref, kc_ref, vc_ref, ov_ref, ks_ref, vs_ref, kw_ref, vw_ref,
                       tc_ref, ts_ref, tw_ref, o_ref):
    qb = pl.program_id(2)
    t0 = qb * QBLOCK
    rq = 4 * QBLOCK
    q = q_ref[...]
    q4 = jnp.concatenate([q[:, r * LANES:(r + 1) * LANES] for r in range(4)], axis=0)
    lane = lax.broadcasted_iota(jnp.int32, (rq, LANES), 1)
    is_pad_lane = lane == PAD_LANE

    def q_aug(mb):
        return jnp.concatenate([q4, mb.astype(BF16)], axis=1)

    qa_pad = q_aug(jnp.where(is_pad_lane, NEG, 0.0))

    far_mask = ((lane < 32) & (lane >= qb - 2)) | is_pad_lane
    qa_cfar = q_aug(jnp.where(far_mask, NEG, 0.0))
    ncmp = 256
    near0 = pl.multiple_of(qb * 8, 8)
    k_far = kc_ref[KC_FRONT:KC_FRONT + ncmp, :].astype(BF16)
    k_near = kc_ref[pl.ds(near0, LANES), :].astype(BF16)
    s_far = _dot_nt(qa_cfar, k_far)
    s_near = _dot_nt(qa_pad, k_near) + tc_ref[...]
    m = jnp.maximum(jnp.max(s_far, axis=-1, keepdims=True), jnp.max(s_near, axis=-1, keepdims=True))
    p_far = jnp.where(s_far > HALF_NEG, jnp.exp(s_far - m), 0.0)
    p_near = jnp.where(s_near > HALF_NEG, jnp.exp(s_near - m), 0.0)
    l = jnp.sum(p_far, axis=-1, keepdims=True) + jnp.sum(p_near, axis=-1, keepdims=True)
    linv = 1.0 / jnp.where(l > 0.0, l, 1.0)
    v_far = vc_ref[KC_FRONT:KC_FRONT + ncmp, :].astype(BF16)
    v_near = vc_ref[pl.ds(near0, LANES), :].astype(BF16)
    o_c = (_dot(p_far.astype(BF16), v_far) + _dot(p_near.astype(BF16), v_near)) * linv

    pn_far = p_far * linv
    pn_near = p_near * linv
    ps_far = sum(pn_far[r * QBLOCK:(r + 1) * QBLOCK] for r in range(4))
    ps_near = sum(pn_near[r * QBLOCK:(r + 1) * QBLOCK] for r in range(4))
    ov_far = ov_ref[KC_FRONT:KC_FRONT + ncmp, :].astype(BF16)
    ov_near = ov_ref[pl.ds(near0, LANES), :].astype(BF16)
    imp = _dot_hilo(ps_far, ov_far) + _dot_hilo(ps_near, ov_near)
    nblk = 64
    imp_t = imp.T[:nblk]
    blk = lax.broadcasted_iota(jnp.int32, (nblk, QBLOCK), 0)
    qpos = t0 + lax.broadcasted_iota(jnp.int32, (nblk, QBLOCK), 1)
    cur = qpos // SEL_LEN
    forced = (blk == 0) | (blk == cur) | (blk == cur - 1)
    eligible = blk * SEL_LEN <= qpos
    val = jnp.where(forced, FORCE, jnp.where(eligible, imp_t, -1.0))
    rank = jnp.zeros((nblk, QBLOCK), F32)
    for i in range(nblk):
        ri = val[i:i + 1, :]
        beats = (ri > val) | ((ri == val) & (blk > i))
        rank = rank + jnp.where(beats, 1.0, 0.0)
    mb_t = jnp.where(rank < float(N_SEL), 0.0, NEG)
    row2 = lax.broadcasted_iota(jnp.int32, (LANES - nblk, QBLOCK), 0)
    mb_t = jnp.concatenate([mb_t, jnp.where(row2 == PAD_LANE - nblk, NEG, 0.0)], axis=0)
    mb = mb_t.T
    mb4 = jnp.concatenate([mb] * 4, axis=0)
    qa_snear = q_aug(mb4)
    qa_sfar = q_aug(jnp.where((lane < nblk) & (lane >= 2 * qb - 2), NEG, mb4))

    sn0 = pl.multiple_of(t0 + KPAD - QBLOCK, QBLOCK)
    s = _dot_nt(qa_snear, ks_ref[pl.ds(sn0, 2 * QBLOCK), :]) + ts_ref[...]
    m = jnp.max(s, axis=-1, keepdims=True)
    p = jnp.exp(s - m)
    l = jnp.sum(p, axis=-1, keepdims=True)
    acc = _dot(p.astype(BF16), vs_ref[pl.ds(sn0, 2 * QBLOCK), :])
    far_tile = 512
    n_far = (jnp.maximum(qb - 1, 0) * QBLOCK + far_tile - 1) // far_tile

    def far_body(j, carry):
        m, l, acc = carry
        r0 = pl.multiple_of(KPAD + j * far_tile, far_tile)
        s = _dot_nt(qa_sfar, ks_ref[pl.ds(r0, far_tile), :])
        m_new = jnp.maximum(m, jnp.max(s, axis=-1, keepdims=True))
        alpha = jnp.exp(m - m_new)
        p = jnp.exp(s - m_new)
        l = alpha * l + jnp.sum(p, axis=-1, keepdims=True)
        acc = alpha * acc + _dot(p.astype(BF16), vs_ref[pl.ds(r0, far_tile), :])
        return m_new, l, acc

    m, l, acc = lax.fori_loop(0, n_far, far_body, (m, l, acc))
    o_s = acc * (1.0 / l)

    w0 = pl.multiple_of(t0, QBLOCK)
    wlen = WINDOW + QBLOCK
    s = _dot_nt(qa_pad, kw_ref[pl.ds(w0, wlen), :]) + tw_ref[...]
    m = jnp.max(s, axis=-1, keepdims=True)
    p = jnp.exp(s - m)
    l = jnp.sum(p, axis=-1, keepdims=True)
    o_w = _dot(p.astype(BF16), vw_ref[pl.ds(w0, wlen), :]) * (1.0 / l)

    gt = gate_ref[...]
    outs = []
    for r in range(4):
        rs = slice(r * QBLOCK, (r + 1) * QBLOCK)
        outs.append(gt[:, 3 * r:3 * r + 1] * o_c[rs] + gt[:, 3 * r + 1:3 * r + 2] * o_s[rs]
                    + gt[:, 3 * r + 2:3 * r + 3] * o_w[rs])
    o_ref[...] = jnp.concatenate(outs, axis=1).astype(BF16)


def _nsa_prompt(q, gates_g, kc, vc, ov, ks, vs, kw, vw, tab_c, tab_s, tab_w):
    b, t, _ = q.shape
    nqb = t // QBLOCK
    kvspec = lambda a: pl.BlockSpec((None, None) + a.shape[2:], lambda bi, g, i: (bi, g, 0, 0))
    tabspec = lambda a: pl.BlockSpec((None,) + a.shape[1:], lambda bi, g, i: (g, 0, 0))
    return pl.pallas_call(
        _nsa_prompt_kernel,
        grid=(b, 2, nqb),
        in_specs=[pl.BlockSpec((None, QBLOCK, 512), lambda bi, g, i: (bi, i, g)),
                  pl.BlockSpec((None, None, QBLOCK, LANES), lambda bi, g, i: (bi, g, i, 0)),
                  kvspec(kc), kvspec(vc), pl.BlockSpec(ov.shape, lambda bi, g, i: (0, 0)),
                  kvspec(ks), kvspec(vs), kvspec(kw), kvspec(vw),
                  tabspec(tab_c), tabspec(tab_s), tabspec(tab_w)],
        out_specs=pl.BlockSpec((None, QBLOCK, 512), lambda bi, g, i: (bi, i, g)),
        out_shape=jax.ShapeDtypeStruct((b, t, 1024), BF16),
        compiler_params=_cparams(("arbitrary", "arbitrary", "arbitrary")),
        name="nsa_prompt",
    )(q, gates_g, kc, vc, ov, ks, vs, kw, vw, tab_c, tab_s, tab_w)


def _decode_attend(qf, g0rows, gk_row, k_tiles, v_tiles, bias, key_mask, new_row, bias_new):
    qg = qf * gk_row
    zero = jnp.zeros_like(qg)
    qbd = jnp.concatenate([jnp.where(g0rows, qg, zero), jnp.where(g0rows, zero, qg)], axis=1).astype(BF16)
    one = jnp.ones_like(qg)
    ones_bd = jnp.concatenate([jnp.where(g0rows, one, zero), jnp.where(g0rows, zero, one)], axis=1).astype(BF16)
    s_parts, q_parts = [], []
    for kt in k_tiles:
        k = kt()
        s_parts.append(_dot_nt(qbd, k.astype(BF16)))
        q_parts.append(_dot_nt(ones_bd, (k * k).astype(BF16)))
    s = jnp.concatenate(s_parts, axis=1)
    ssq = jnp.concatenate(q_parts, axis=1)
    s = s * lax.rsqrt(ssq * (1.0 / HEAD_DIM) + EPS) + bias
    if key_mask is not None:
        s = jnp.where(key_mask > 0.5, s, NEG)
    bc = lambda c: jnp.broadcast_to(new_row[c:c + 1, :], qf.shape)
    k_new = jnp.where(g0rows, bc(0), bc(1))
    v_new = jnp.where(g0rows, bc(2), bc(3))
    s_new = jnp.sum(qg * _rms(k_new), axis=-1, keepdims=True) + bias_new
    m = jnp.maximum(jnp.max(s, axis=-1, keepdims=True), s_new)
    p = jnp.exp(s - m)
    p_new = jnp.exp(s_new - m)
    l = jnp.sum(p, axis=-1, keepdims=True) + p_new
    o2 = jnp.zeros((qf.shape[0], 2 * LANES), F32)
    for i, vt in enumerate(v_tiles):
        o2 = o2 + _dot(p[:, i * LANES:(i + 1) * LANES].astype(BF16), vt().astype(BF16))
    o = jnp.where(g0rows, o2[:, :LANES], o2[:, LANES:]) + p_new * v_new
    return o * (1.0 / l)


def _nsa_sample_kernel(pt_ref, *refs, n_pages, n_sel_blocks, group):
    del pt_ref
    pages_c = refs[:group * n_pages]
    pages_s = refs[group * n_pages:2 * group * n_pages]
    rest = refs[2 * group * n_pages:]
    per_b_in, consts, outs = rest[:5], rest[5:-2], rest[-2:]
    chains = [_nsa_sample_one(pages_c[bb * n_pages:(bb + 1) * n_pages], pages_s[bb * n_pages:(bb + 1) * n_pages],
                              *[r.at[bb] for r in per_b_in], *consts, *[r.at[bb] for r in outs],
                              n_pages=n_pages, n_sel_blocks=n_sel_blocks) for bb in range(group)]
    while chains:
        chains = [c for c in chains if next(c, "done") != "done"]


def _nsa_sample_one(pages_c, pages_s, win_ref, q_ref, gate_ref, ksn_ref, kwn_ref, wk_ref, wv_ref, pos_ref, gk_ref,
                    bc_ref, bs_ref, bw_ref, bnew_ref, ov_ref, e_ref, perm_ref, o_ref, wout_ref, *,
                    n_pages, n_sel_blocks):
    qf = q_ref[...]
    nh = qf.shape[0]
    g0rows = lax.broadcasted_iota(jnp.int32, (nh, LANES), 0) < nh // 2
    lane = lax.broadcasted_iota(jnp.int32, (nh, LANES), 1)

    page = pages_c[0].shape[0] // 4
    nseg = n_pages * page // CMP_STRIDE
    perm = perm_ref[...]
    regrouped = []
    for pp in range(n_pages // 2):
        per_cp = []
        for cp in range(2):
            blk = jnp.concatenate(
                [jnp.concatenate([pages_c[2 * pp + i][pl.ds(2 * cp + c, page, stride=4), :] for c in range(2)], axis=1)
                 for i in range(2)], axis=0).astype(BF16)
            per_cp.append(_dot(perm, blk).astype(BF16))
        regrouped.append(per_cp)
    rows_per_tap = 2 * page // CMP_STRIDE
    yield

    def tap_rows(s, col):
        lo, hi = s * rows_per_tap, (s + 1) * rows_per_tap
        return jnp.concatenate([regrouped[pp][col // 2][lo:hi, (col % 2) * LANES:(col % 2 + 1) * LANES]
                                for pp in range(n_pages // 2)], axis=0)

    load_pair = lambda s2, col: jnp.concatenate([tap_rows(2 * s2, col), tap_rows(2 * s2 + 1, col)], axis=1)
    ck, cv = _compress_core(load_pair, nseg, wk_ref, wv_ref, pos_ref)
    yield
    kcn = jnp.concatenate([(_rms(ck[g]) * gk_ref[0:1, :]).astype(BF16) for g in range(2)], axis=0)
    s2 = _dot_nt(qf.astype(BF16), kcn)
    s = jnp.where(g0rows, s2[:, :nseg], s2[:, nseg:]) + bc_ref[...]
    m = jnp.max(s, axis=-1, keepdims=True)
    p = jnp.exp(s - m)
    linv = 1.0 / jnp.sum(p, axis=-1, keepdims=True)
    rowv = lax.broadcasted_iota(jnp.int32, (nseg, LANES), 0) < nseg - 1
    pb = p.astype(BF16)
    oc = [_dot(pb, jnp.where(rowv, cv[g], 0.0).astype(BF16)) for g in range(2)]
    o_c = jnp.where(g0rows, oc[0], oc[1]) * linv

    yield
    pn = p * linv
    s0 = jnp.sum(jnp.where(g0rows, pn, 0.0), axis=0, keepdims=True)
    s1 = jnp.sum(jnp.where(g0rows, 0.0, pn), axis=0, keepdims=True)
    psum = jnp.where(g0rows, jnp.broadcast_to(s0, pn.shape), jnp.broadcast_to(s1, pn.shape))
    imp = _dot_hilo(psum, ov_ref[...].astype(BF16))
    cur = n_sel_blocks - 1
    forced = (lane == 0) | (lane == cur) | (lane == cur - 1)
    val = jnp.where(lane >= n_sel_blocks, -2.0, jnp.where(forced, FORCE, imp))
    rank = jnp.zeros_like(val)
    for i in range(n_sel_blocks):
        ci = jnp.broadcast_to(val[:, i:i + 1], val.shape)
        beats = (ci > val) | ((ci == val) & (lane > i))
        rank = rank + jnp.where(beats, 1.0, 0.0)
    sel = jnp.where((rank < float(N_SEL)) & (lane < n_sel_blocks), 1.0, 0.0)
    key_mask = _dot(sel.astype(BF16), e_ref[...])

    yield
    def tiles(ref_list, col):
        def tile(rf, i):
            rows = lambda c: rf[pl.ds(4 * LANES * i + c, LANES, stride=4), :]
            return lambda: jnp.concatenate([rows(col), rows(col + 1)], axis=1)
        return [tile(rf, i) for rf in ref_list for i in range(rf.shape[0] // (4 * LANES))]

    b_new = bnew_ref[:, 0:1]
    o_s = _decode_attend(qf, g0rows, gk_ref[1:2, :], tiles(pages_s, 0), tiles(pages_s, 2),
                         bs_ref[...], key_mask, ksn_ref[...], b_new)
    yield
    o_w = _decode_attend(qf, g0rows, gk_ref[2:3, :], tiles([win_ref], 0), tiles([win_ref], 2),
                         bw_ref[...], None, kwn_ref[...], b_new)
    yield
    gt = gate_ref[...]
    o_ref[...] = gt[:, 0:1] * o_c + gt[:, 1:2] * o_s + gt[:, 2:3] * o_w

    keep = wout_ref.shape[0] - 4
    drop = win_ref.shape[0] - keep
    wout_ref[0:keep, :] = win_ref[drop:drop + keep, :]
    wout_ref[keep:keep + 4, :] = kwn_ref[...]


def _nsa_sample(page_table, cache_c, cache_s, win, q, gates, ks_new, kw_new, wk_pairs, wv_pairs, pos_rows, gk,
                bias_c, bias_s, bias_w, bias_new, ov, expand):
    nb, n_pages = page_table.shape
    page_rows = cache_c.shape[1]
    n_sel_blocks = -(-(n_pages * (page_rows // 4) + 1) // SEL_LEN)
    group = SAMPLE_GROUP if nb % SAMPLE_GROUP == 0 else 1
    kern = functools.partial(_nsa_sample_kernel, n_pages=n_pages, n_sel_blocks=n_sel_blocks, group=group)
    win_rows_out = 4 * min(WINDOW, n_pages * (page_rows // 4) + 1)
    page = page_rows // 4
    assert n_pages % 2 == 0 and page % CMP_STRIDE == 0
    segs = page // CMP_STRIDE
    i_, n_, s_ = np.meshgrid(np.arange(2), np.arange(segs), np.arange(CMP_STRIDE), indexing="ij")
    perm_np = np.zeros((2 * page, 2 * page), np.float32)
    perm_np[(s_ * 2 * segs + i_ * segs + n_).ravel(), (i_ * page + CMP_STRIDE * n_ + s_).ravel()] = 1.0
    perm = jnp.asarray(perm_np, BF16)
    page_spec = lambda bb, p: pl.BlockSpec((None, page_rows, LANES),
                                           lambda bi, pt, bb=bb, p=p: (pt[group * bi + bb, p], 0, 0))
    pages = [page_spec(bb, p) for bb in range(group) for p in range(n_pages)]
    full = lambda a: pl.BlockSpec(a.shape, lambda bi, pt: (0,) * a.ndim)
    per_b = lambda a: pl.BlockSpec((group,) + a.shape[1:], lambda bi, pt: (bi,) + (0,) * (a.ndim - 1))
    grid_spec = pltpu.PrefetchScalarGridSpec(
        num_scalar_prefetch=1,
        grid=(nb // group,),
        in_specs=(pages + pages
                  + [per_b(win), per_b(q), per_b(gates), per_b(ks_new), per_b(kw_new),
                     full(wk_pairs), full(wv_pairs), full(pos_rows), full(gk),
                     full(bias_c), full(bias_s), full(bias_w), full(bias_new), full(ov), full(expand), full(perm)]),
        out_specs=[pl.BlockSpec((group,) + q.shape[1:], lambda bi, pt: (bi, 0, 0)),
                   pl.BlockSpec((group, win_rows_out, LANES), lambda bi, pt: (bi, 0, 0))],
    )
    return pl.pallas_call(
        kern,
        grid_spec=grid_spec,
        out_shape=(jax.ShapeDtypeStruct(q.shape, F32), jax.ShapeDtypeStruct((nb, win_rows_out, LANES), F32)),
        compiler_params=_cparams(("arbitrary",)),
        name="nsa_sample",
    )(page_table, *([cache_c] * (group * n_pages)), *([cache_s] * (group * n_pages)), win, q, gates, ks_new, kw_new,
      wk_pairs, wv_pairs, pos_rows, gk, bias_c, bias_s, bias_w, bias_new, ov, expand, perm)


def _merge_kernel(h_ref, oa_ref, ob_ref, wga_ref, wgb_ref, wg_ref, wn_ref, mix_ref, *cast_refs):
    h = h_ref[...]
    wg = wg_ref[...].astype(BF16)
    wn = wn_ref[...].astype(BF16)
    ga = _sigmoid(_dot(h, wga_ref[...]))
    gb = _sigmoid(_dot(h, wgb_ref[...]))
    mix_ref[...] = (ga * _dot(oa_ref[...], wg) + gb * _dot(ob_ref[...], wn)).astype(BF16)
    if cast_refs:
        cast_refs[0][...] = wg
        cast_refs[1][...] = wn


def _merge(h, oa, ob, wga, wgb, wg, wn):
    n, d = h.shape
    tm = 512 if n % 512 == 0 else n
    tn = 512
    emit = wg.dtype != BF16
    assert not emit or n == tm
    row = lambda w: pl.BlockSpec((tm, w), lambda i, j: (i, 0))
    col = lambda k: pl.BlockSpec((k, tn), lambda i, j: (0, j))
    out_specs = [pl.BlockSpec((tm, tn), lambda i, j: (i, j))]
    out_shape = [jax.ShapeDtypeStruct((n, d), BF16)]
    if emit:
        out_specs += [col(wg.shape[0]), col(wn.shape[0])]
        out_shape += [jax.ShapeDtypeStruct(wg.shape, BF16), jax.ShapeDtypeStruct(wn.shape, BF16)]
    return pl.pallas_call(
        _merge_kernel,
        grid=(n // tm, d // tn),
        in_specs=[row(d), row(oa.shape[1]), row(ob.shape[1]), col(d), col(d), col(wg.shape[0]), col(wn.shape[0])],
        out_specs=out_specs,
        out_shape=out_shape,
        compiler_params=_cparams(("arbitrary", "arbitrary")),
        name="merge",
    )(h, oa, ob, wga, wgb, wg, wn)


def _outproj_kernel(x_ref, mix_ref, wout_ref, gmlp_ref, x1_ref, hm_ref):
    x1 = x_ref[...] + _dot(mix_ref[...], wout_ref[...])
    x1_ref[...] = x1
    hm_ref[...] = (_rms(x1) * gmlp_ref[...]).astype(BF16)


def _outproj(x, mix, wout, gmlp):
    n, d = x.shape
    tm = 512 if n % 512 == 0 else n
    row = pl.BlockSpec((tm, d), lambda i: (i, 0))
    return pl.pallas_call(
        _outproj_kernel,
        grid=(n // tm,),
        in_specs=[row, row, pl.BlockSpec(wout.shape, lambda i: (0, 0)), pl.BlockSpec(gmlp.shape, lambda i: (0, 0))],
        out_specs=[row, row],
        out_shape=(jax.ShapeDtypeStruct((n, d), F32), jax.ShapeDtypeStruct((n, d), BF16)),
        compiler_params=_cparams(("arbitrary",)),
        name="outproj",
    )(x, mix, wout, gmlp)


def _ffn_kernel(hm_ref, x1_ref, wup_ref, wdown_ref, y_ref, *cast_refs):
    f = pl.program_id(1)

    @pl.when(f == 0)
    def _():
        y_ref[...] = x1_ref[...]

    wup = wup_ref[...].astype(BF16)
    wdown = wdown_ref[...].astype(BF16)
    hid = jnp.maximum(_dot(hm_ref[...], wup), 0.0)
    y_ref[...] += _dot((hid * hid).astype(BF16), wdown)
    if cast_refs:
        cast_refs[0][...] = wup
        cast_refs[1][...] = wdown


def _ffn(hm, x1, wup, wdown):
    n, d = hm.shape
    dff = wup.shape[1]
    tm = 512 if n % 512 == 0 else n
    emit = wup.dtype != BF16
    assert not emit or n == tm
    tf = 512 if emit else 1024
    row = pl.BlockSpec((tm, d), lambda i, f: (i, 0))
    up_spec = pl.BlockSpec((d, tf), lambda i, f: (0, f))
    down_spec = pl.BlockSpec((tf, d), lambda i, f: (f, 0))
    out_specs = [row]
    out_shape = [jax.ShapeDtypeStruct((n, d), F32)]
    if emit:
        out_specs += [up_spec, down_spec]
        out_shape += [jax.ShapeDtypeStruct(wup.shape, BF16), jax.ShapeDtypeStruct(wdown.shape, BF16)]
    return pl.pallas_call(
        _ffn_kernel,
        grid=(n // tm, dff // tf),
        in_specs=[row, row, up_spec, down_spec],
        out_specs=out_specs,
        out_shape=out_shape,
        compiler_params=_cparams(("arbitrary", "arbitrary")),
        name="ffn",
    )(hm, x1, wup, wdown)


def _bucket(rel, valid):
    n = np.maximum(rel, 0)
    max_exact = N_BUCKETS // 2
    nf = np.maximum(n, 1).astype(np.float32)
    large = max_exact + (np.log(nf / np.float32(max_exact)) / np.float32(math.log(MAX_DIST / max_exact))
                         * np.float32(N_BUCKETS - max_exact)).astype(np.int32)
    large = np.minimum(large, N_BUCKETS - 1)
    return np.where(valid, np.where(n < max_exact, n, large), -1).astype(np.int32)


def _bias_tables_kernel(rb_ref, *refs, shifts):
    n = len(shifts)
    nbk, nh = rb_ref.shape
    for b_ref, o_ref, shift in zip(refs[:n], refs[n:], shifts):
        b = b_ref[...]
        rows = []
        for h in range(nh):
            sh = rb_ref[nbk - 1, h] if shift else 0.0
            acc = jnp.full(b.shape, NEG, F32)
            for k in range(nbk):
                acc = jnp.where(b == k, rb_ref[k, h] - sh, acc)
            if len(o_ref.shape) == 3:
                o_ref[h] = acc
            else:
                rows.append(acc)
        if rows:
            o_ref[...] = jnp.concatenate(rows, axis=0)


def _bias_tables(rel_bias, buckets, shifts):
    nh = rel_bias.shape[1]
    shapes = [jax.ShapeDtypeStruct((nh,) + (b.shape if b.shape[0] > 1 else b.shape[1:]), F32) for b in buckets]
    vm = pl.BlockSpec(memory_space=pltpu.VMEM)
    return pl.pallas_call(
        functools.partial(_bias_tables_kernel, shifts=tuple(shifts)),
        in_specs=[pl.BlockSpec(memory_space=pltpu.SMEM)] + [vm] * len(buckets),
        out_specs=[vm] * len(buckets),
        out_shape=shapes,
        name="bias_tables",
    )(rel_bias, *[jnp.asarray(b) for b in buckets])


def _overlap(nc, ns):
    i = np.arange(nc)[:, None] * CMP_STRIDE
    j = np.arange(ns)[None, :] * SEL_LEN
    return ((i < j + SEL_LEN) & (i + CMP_LEN > j)).astype(np.float32)


def _compress_weights(w, pos):
    s = np.arange(0, CMP_STRIDE, 2)
    top = jnp.concatenate([w[s], w[CMP_STRIDE + s]], axis=2)
    bot = jnp.concatenate([w[s + 1], w[CMP_STRIDE + s + 1]], axis=2)
    tiles = jnp.concatenate([top, bot], axis=1).astype(BF16)
    row_a = jnp.concatenate([pos[s], pos[s + 1]], axis=1)
    row_b = jnp.concatenate([pos[CMP_STRIDE + s], pos[CMP_STRIDE + s + 1]], axis=1)
    rows = jnp.zeros((len(s), 16, 2 * HEAD_DIM), F32).at[:, 0].set(row_a).at[:, 1].set(row_b)
    return tiles, rows.astype(BF16)


def _dense_tail(x, h, oa, ob, wga, wgb, wg, wn, wout, gmlp, wup, wdown):
    mix, *cast_m = _merge(h, oa, ob, wga, wgb, wg, wn)
    x1, hm = _outproj(x, mix, wout, gmlp)
    y, *cast_f = _ffn(hm, x1, wup, wdown)
    return y, (cast_m or [wg, wn]) + (cast_f or [wup, wdown])


def kernel(x_prompt, x_sample, cache_cmp_kv, cache_sel_kv, state_win_kv, page_table, rel_bias, g_mix_norm, w_in,
           g_sgu, w_sgu, b_sgu, g_q, g_k, pos_cmp_k, w_cmp_k, pos_cmp_v, w_cmp_v, w_proj_gmlp, w_proj_nsa, w_out,
           g_mlp_norm, w_up, w_down):
    depth = g_mix_norm.shape[0]
    assert depth == 1
    l = 0
    bsz, seq, d = x_prompt.shape
    nb = x_sample.shape[0]
    assert x_sample.shape[1] == 1 and seq % KPAD == 0
    d_gm = g_sgu.shape[1]
    n_heads = rel_bias.shape[1]
    d_nsa = n_heads * HEAD_DIM
    kvw_cols = 2 * (n_heads // 4) * HEAD_DIM
    n_gate = 3 * n_heads
    c_q = 2 * d_gm
    c_kv = c_q + d_nsa
    c_gate = c_kv + 3 * kvw_cols
    c_ga = c_gate + n_gate
    c_gb = c_ga + d

    w = w_in[l]
    n_col_tiles = -(-c_ga // 1024)
    wga = w[:, c_ga:c_gb].astype(BF16)
    wgb = w[:, c_gb:].astype(BF16)
    wout = w_out[l].astype(BF16)
    gmix = g_mix_norm[l][None]
    gsgu = g_sgu[l][None]
    gq = g_q[l][None]
    gk = g_k[l]
    gmlp = g_mlp_norm[l][None]
    wk_pairs, posk_rows = _compress_weights(w_cmp_k[l], pos_cmp_k[l])
    wv_pairs, posv_rows = _compress_weights(w_cmp_v[l], pos_cmp_v[l])
    pos_rows = jnp.stack([posk_rows, posv_rows])
    n_groups = w_sgu.shape[1]
    b_exp = jnp.repeat(b_sgu[l].T, d_gm // n_groups, axis=1)
    w00 = jnp.repeat(w_sgu[l][:, 0, 0], d_gm // n_groups)[None]
    b00 = jnp.repeat(b_sgu[l][:, 0], d_gm // n_groups)[None]

    n_pool, page = cache_cmp_kv.shape[1], cache_cmp_kv.shape[2]
    n_pages = page_table.shape[1]
    past = n_pages * page
    nwin = state_win_kv.shape[2]
    n_cmp_s = (past + 1 - CMP_LEN) // CMP_STRIDE + 1
    n_sel_s = -(-(past + 1) // SEL_LEN)
    nseg_s = past // CMP_STRIDE
    qi = np.arange(QBLOCK)[:, None]
    rel_s = qi + QBLOCK - np.arange(2 * QBLOCK)[None, :]
    rel_w = qi + WINDOW - np.arange(WINDOW + QBLOCK)[None, :]
    rel_c = qi + (KC_FRONT * CMP_STRIDE - CMP_LEN + 1) - CMP_STRIDE * np.arange(LANES)[None, :]
    srel_c = (past - (np.arange(nseg_s) * CMP_STRIDE + CMP_LEN - 1))[None]
    srel_s = (past - np.arange(past))[None]
    srel_w = (nwin - np.arange(nwin))[None]
    buckets = [
        _bucket(rel_c, rel_c >= 0), _bucket(rel_s, rel_s >= 0), _bucket(rel_w, (rel_w >= 0) & (rel_w < WINDOW)),
        _bucket(srel_c, (srel_c >= 0) & (np.arange(nseg_s)[None] < n_cmp_s)), _bucket(srel_s, srel_s >= 0),
        _bucket(srel_w, srel_w < WINDOW), np.zeros((1, LANES), np.int32)]
    tab_c, tab_s, tab_w, bias_c, bias_s, bias_w, bias_new = _bias_tables(
        rel_bias, buckets, [True, True, False, False, False, False, False])
    tab_c, tab_s, tab_w = [t.reshape(2, (n_heads // 2) * QBLOCK, t.shape[-1]) for t in (tab_c, tab_s, tab_w)]

    xs = x_sample.reshape(nb, d)
    h_s, v_s, oa_s, q_s, kvc_s, kvs_s, kvw_s, gates_s, w_cat = _inproj(
        xs, gmix, w, gsgu, gq, w00, b00, chunked=False, n_col_tiles=n_col_tiles)
    ov_s = np.zeros((nseg_s, LANES), np.float32)
    ov_s[:n_cmp_s, :n_sel_s] = _overlap(n_cmp_s, n_sel_s)
    expand = jnp.asarray(np.arange(LANES)[:, None] == (np.arange(past)[None, :] // SEL_LEN), BF16)
    gates_h = jnp.pad(gates_s[:, :n_gate].reshape(nb, n_heads, 3), ((0, 0), (0, 0), (0, LANES - 3)))
    lin = lambda a: a[l].reshape(a.shape[1], -1, HEAD_DIM)
    win_lin = lin(state_win_kv)
    ob_s, win_new = _nsa_sample(
        page_table, lin(cache_cmp_kv), lin(cache_sel_kv), win_lin,
        q_s.astype(F32).reshape(nb, n_heads, HEAD_DIM), gates_h,
        kvs_s.reshape(nb, 4, HEAD_DIM), kvw_s.reshape(nb, 4, HEAD_DIM), wk_pairs, wv_pairs, pos_rows, gk,
        bias_c, bias_s, bias_w, bias_new, jnp.asarray(ov_s), expand)
    y_s, (wg, wn, wup, wdown) = _dense_tail(xs, h_s, oa_s, ob_s.reshape(nb, d_nsa).astype(BF16), wga, wgb,
                                            w_proj_gmlp[l], w_proj_nsa[l], wout, gmlp, w_up[l], w_down[l])

    xp = x_prompt.reshape(bsz * seq, d)
    h_p, v_p, oa_p, q_p, kvc_p, kvs_p, kvw_p, gates_p = _inproj(
        xp, gmix, w_cat, gsgu, gq, w_sgu[l], b_exp, chunked=True, n_col_tiles=n_col_tiles)
    kvs3 = kvs_p.reshape(bsz, 4 * seq, HEAD_DIM)
    kvw3 = kvw_p.reshape(bsz, 4 * seq, HEAD_DIM)
    kvc3 = kvc_p.reshape(bsz, 4 * seq, HEAD_DIM)
    ks, vs, kw, vw = _kvprep(kvs3, kvw3, gk)
    kc, vc = _compress_prompt(kvc3, wk_pairs, wv_pairs, pos_rows, gk)
    n_cmp = (seq - CMP_LEN) // CMP_STRIDE + 1
    n_selb = seq // SEL_LEN
    ov_p = np.zeros((KC_ROWS, LANES), np.float32)
    ov_p[KC_FRONT:KC_FRONT + n_cmp, :n_selb] = _overlap(n_cmp, n_selb)
    ov_p = jnp.asarray(ov_p)
    gates_g = gates_p[:, :n_gate].reshape(bsz, seq, 2, n_gate // 2).transpose(0, 2, 1, 3)
    gates_g = jnp.pad(gates_g, ((0, 0), (0, 0), (0, 0), (0, LANES - n_gate // 2)))
    ob_p = _nsa_prompt(q_p.reshape(bsz, seq, d_nsa), gates_g, kc, vc, ov_p, ks, vs, kw, vw, tab_c, tab_s, tab_w)
    y_p, _ = _dense_tail(xp, h_p, oa_p, ob_p.reshape(bsz * seq, d_nsa), wga, wgb, wg, wn, wout, gmlp, wup, wdown)

    n_kv = n_heads // 4
    kv6 = lambda a, b_, t_: a.reshape(1, b_, t_, 2, n_kv, HEAD_DIM)
    nw_p = min(WINDOW, seq)
    last = ((seq - 1) // CHUNK) * CHUNK
    nw_s = min(WINDOW, past + 1)
    return (y_p.reshape(bsz, seq, d), y_s.reshape(nb, 1, d),
            kv6(kvc3, bsz, seq), kv6(kvs3, bsz, seq), kv6(kvw3[:, 4 * (seq - nw_p):], bsz, nw_p),
            v_p.reshape(bsz, seq, d_gm)[:, last:][None],
            kv6(kvc_s, nb, 1), kv6(kvs_s, nb, 1), kv6(win_new, nb, nw_s),
            v_s.reshape(1, nb, 1, d_gm))
```

```python
import functools
import math

import numpy as np
import jax
import jax.numpy as jnp
from jax import lax
from jax.experimental import pallas as pl
from jax.experimental.pallas import tpu as pltpu

F32 = jnp.float32
BF16 = jnp.bfloat16

HEAD_DIM = 128
CHUNK = 128
CMP_LEN = 32
CMP_STRIDE = 16
SEL_LEN = 64
N_SEL = 16
WINDOW = 512
N_BUCKETS = 32
MAX_DIST = 128
QBLOCK = 128
EPS = 1e-6
NEG = -1e30
HALF_NEG = -5e29
FORCE = 1e6

LANES = 128
PAD_LANE = 64
KPAD = 512
KC_FRONT = 16
KC_ROWS = 376
SAMPLE_GROUP = 2
VMEM_LIMIT = 56 * 1024 * 1024


def _cparams(sem):
    return pltpu.CompilerParams(dimension_semantics=sem, vmem_limit_bytes=VMEM_LIMIT)


def _dot(a, b):
    return jnp.dot(a, b, preferred_element_type=F32)


def _dot_nt(a, b):
    return lax.dot_general(a, b, (((1,), (1,)), ((), ())), preferred_element_type=F32)


def _dot_hilo(a, b_bf16):
    hi = a.astype(BF16)
    lo = (a - hi.astype(F32)).astype(BF16)
    return _dot(hi, b_bf16) + _dot(lo, b_bf16)


def _rms(x):
    return x * lax.rsqrt(jnp.mean(x * x, axis=-1, keepdims=True) + EPS)


def _gelu(x):
    c = math.sqrt(2.0 / math.pi)
    return 0.5 * x * (1.0 + jnp.tanh(c * (x + 0.044715 * (x * x * x))))


def _sigmoid(x):
    return 1.0 / (1.0 + jnp.exp(-x))


def _inproj_kernel(x_ref, gmix_ref, w_ref, gsgu_ref, gq_ref, wsg_ref, bsg_ref,
                   h_ref, v_ref, oa_ref, q_ref, kvc_ref, kvs_ref, kvw_ref, gate_ref, *rest, chunked, tm):
    h_s, u_s = rest[-2:]
    j = pl.program_id(1)
    w = w_ref[...].astype(BF16)
    if len(rest) == 3:
        rest[0][...] = w

    @pl.when(j == 0)
    def _():
        hb = (_rms(x_ref[...]) * gmix_ref[...]).astype(BF16)
        h_s[...] = hb
        h_ref[...] = hb

    z = _dot_nt(h_s[...], w)

    @pl.when(j == 0)
    def _():
        u_s[...] = _gelu(z)

    @pl.when(j == 1)
    def _():
        v = _rms(_gelu(z)) * gsgu_ref[...]
        v_ref[...] = v
        if chunked:
            row = lax.broadcasted_iota(jnp.int32, (CHUNK, CHUNK), 0)
            col = lax.broadcasted_iota(jnp.int32, (CHUNK, CHUNK), 1)
            n_groups = v.shape[1] // LANES
            for g in range(n_groups):
                wm = jnp.where(row >= col, wsg_ref[g], 0.0).astype(BF16)
                cs = slice(g * LANES, (g + 1) * LANES)
                for c in range(tm // CHUNK):
                    rs = slice(c * CHUNK, (c + 1) * CHUNK)
                    s = _dot(wm, v[rs, cs].astype(BF16)) + bsg_ref[:, cs]
                    oa_ref[rs, cs] = (u_s[rs, cs] * s).astype(BF16)
        else:
            oa_ref[...] = (u_s[...] * (v * wsg_ref[...] + bsg_ref[...])).astype(BF16)

    @pl.when(j == 2)
    def _():
        scale = HEAD_DIM ** -0.5
        for hd in range(z.shape[1] // HEAD_DIM):
            cs = slice(hd * HEAD_DIM, (hd + 1) * HEAD_DIM)
            q_ref[:, cs] = (_rms(z[:, cs]) * gq_ref[...] * scale).astype(BF16)

    def store_kv(ref, zz):
        for c in range(4):
            ref[pl.ds(c, tm, stride=4), :] = zz[:, c * LANES:(c + 1) * LANES]

    @pl.when(j == 3)
    def _():
        store_kv(kvc_ref, z[:, :512])
        store_kv(kvs_ref, z[:, 512:])

    @pl.when(j == 4)
    def _():
        store_kv(kvw_ref, z[:, :512])
        gate_ref[...] = _sigmoid(z[:, 512:640])


def _inproj(x, gmix, w_cat, gsgu, gq, wsg, bsg, *, chunked, n_col_tiles):
    n, d = x.shape
    tm = 512 if n % 512 == 0 else n
    tn = 1024
    emit = w_cat.dtype != BF16
    assert not emit or n == tm
    kern = functools.partial(_inproj_kernel, chunked=chunked, tm=tm)
    full = lambda a: pl.BlockSpec(a.shape, lambda i, j: (0,) * a.ndim)
    row = lambda w: pl.BlockSpec((tm, w), lambda i, j: (i, 0))
    kvrow = pl.BlockSpec((4 * tm, LANES), lambda i, j: (i, 0))
    out_shapes = (
        jax.ShapeDtypeStruct((n, d), BF16),
        jax.ShapeDtypeStruct((n, 1024), F32),
        jax.ShapeDtypeStruct((n, 1024), BF16),
        jax.ShapeDtypeStruct((n, 1024), BF16),
        jax.ShapeDtypeStruct((4 * n, LANES), F32),
        jax.ShapeDtypeStruct((4 * n, LANES), F32),
        jax.ShapeDtypeStruct((4 * n, LANES), F32),
        jax.ShapeDtypeStruct((n, LANES), F32),
    )
    wspec = pl.BlockSpec((tn, d), lambda i, j: (j, 0))
    out_specs = [row(d), row(1024), row(1024), row(1024), kvrow, kvrow, kvrow, row(LANES)]
    if emit:
        out_shapes += (jax.ShapeDtypeStruct((n_col_tiles * tn, d), BF16),)
        out_specs.append(wspec)
    return pl.pallas_call(
        kern,
        grid=(n // tm, n_col_tiles),
        in_specs=[row(d), full(gmix), wspec, full(gsgu), full(gq), full(wsg), full(bsg)],
        out_specs=out_specs,
        out_shape=out_shapes,
        scratch_shapes=[pltpu.VMEM((tm, d), BF16), pltpu.VMEM((tm, 1024), F32)],
        compiler_params=_cparams(("arbitrary", "arbitrary")),
        name="inproj",
    )(x, gmix, w_cat, gsgu, gq, wsg, bsg)


def _kvprep_kernel(kvs_ref, kvw_ref, gk_ref, ks_ref, vs_ref, kw_ref, vw_ref):
    i = pl.program_id(1)
    rows = kvs_ref.shape[0] // 4
    col = lambda ref, c: ref[pl.ds(c, rows, stride=4), :]
    lane = lax.broadcasted_iota(jnp.int32, (rows, LANES), 1)
    row = lax.broadcasted_iota(jnp.int32, (rows, LANES), 0)

    @pl.when(i == 0)
    def _():
        aux = jnp.where(lane == PAD_LANE, 1.0, 0.0).astype(BF16)
        zk = jnp.zeros((rows, LANES), BF16)
        for g in range(2):
            ks_ref[g] = jnp.concatenate([zk, aux], axis=1)
            kw_ref[g] = jnp.concatenate([zk, aux], axis=1)
            vs_ref[g] = zk
            vw_ref[g] = zk

    @pl.when(i > 0)
    def _():
        blk = ((i - 1) * rows + row) // SEL_LEN
        onehot = jnp.where(lane == blk, 1.0, 0.0).astype(BF16)
        zaux = jnp.zeros((rows, LANES), BF16)
        for g in range(2):
            ks = (_rms(col(kvs_ref, g)) * gk_ref[1:2, :]).astype(BF16)
            kw = (_rms(col(kvw_ref, g)) * gk_ref[2:3, :]).astype(BF16)
            ks_ref[g] = jnp.concatenate([ks, onehot], axis=1)
            kw_ref[g] = jnp.concatenate([kw, zaux], axis=1)
            vs_ref[g] = col(kvs_ref, 2 + g).astype(BF16)
            vw_ref[g] = col(kvw_ref, 2 + g).astype(BF16)


def _kvprep(kvs, kvw, gk):
    b, t4, _ = kvs.shape
    t = t4 // 4
    rows = KPAD
    nblk = t // rows
    in_map = lambda bi, i: (bi, jnp.maximum(i - 1, 0), 0)
    out_map = lambda bi, i: (bi, 0, i, 0)
    kshape = jax.ShapeDtypeStruct((b, 2, KPAD + t, 2 * LANES), BF16)
    vshape = jax.ShapeDtypeStruct((b, 2, KPAD + t, LANES), BF16)
    return pl.pallas_call(
        _kvprep_kernel,
        grid=(b, nblk + 1),
        in_specs=[pl.BlockSpec((None, 4 * rows, LANES), in_map), pl.BlockSpec((None, 4 * rows, LANES), in_map),
                  pl.BlockSpec(gk.shape, lambda bi, i: (0, 0))],
        out_specs=[pl.BlockSpec((None, 2, rows, 2 * LANES), out_map), pl.BlockSpec((None, 2, rows, LANES), out_map),
                   pl.BlockSpec((None, 2, rows, 2 * LANES), out_map), pl.BlockSpec((None, 2, rows, LANES), out_map)],
        out_shape=(kshape, vshape, kshape, vshape),
        compiler_params=_cparams(("arbitrary", "arbitrary")),
        name="kvprep",
    )(kvs, kvw, gk)


def _compress_core(load_pair, nseg, wk_ref, wv_ref, pos_ref):
    w_refs = (wk_ref, wv_ref)
    accs = [jnp.zeros((2 * nseg + 16, 2 * LANES), F32) for _ in range(2)]
    for s2 in range(CMP_STRIDE // 2):
        for kv in range(2):
            parts = [load_pair(s2, 2 * kv + g) for g in range(2)]
            parts.append(pos_ref[kv, s2])
            lhs = jnp.concatenate(parts, axis=0)
            accs[kv] = accs[kv] + _dot(lhs, w_refs[kv][s2])
    outs = []
    for kv in range(2):
        y = accs[kv]
        post = y[2 * nseg:2 * nseg + 1, :LANES] + y[2 * nseg + 1:2 * nseg + 2, LANES:]
        per_g = []
        for g in range(2):
            y0 = y[g * nseg:(g + 1) * nseg, :LANES]
            y1 = y[g * nseg:(g + 1) * nseg, LANES:]
            per_g.append(y0 + pltpu.roll(y1, nseg - 1, 0) + post)
        outs.append(per_g)
    return outs


def _compress_prompt_kernel(x_ref, wk_ref, wv_ref, pos_ref, gk_ref, kc_ref, vc_ref):
    nseg = x_ref.shape[0] // (4 * CMP_STRIDE)
    load_x = lambda s, col: x_ref[pl.ds(4 * s + col, nseg, stride=4 * CMP_STRIDE), :]
    load_pair = lambda s2, col: jnp.concatenate([load_x(2 * s2, col), load_x(2 * s2 + 1, col)], axis=1).astype(BF16)
    ck, cv = _compress_core(load_pair, nseg, wk_ref, wv_ref, pos_ref)
    row = lax.broadcasted_iota(jnp.int32, (nseg, LANES), 0)
    lane = lax.broadcasted_iota(jnp.int32, (nseg, LANES), 1)
    valid = row < nseg - 1
    aux = jnp.where(valid, jnp.where(lane == row // 8, 1.0, 0.0), jnp.where(lane == PAD_LANE, 1.0, 0.0))
    back = KC_ROWS - KC_FRONT - nseg
    lane_f = lax.broadcasted_iota(jnp.int32, (KC_FRONT, LANES), 1)
    lane_b = lax.broadcasted_iota(jnp.int32, (back, LANES), 1)
    pad_f = jnp.concatenate([jnp.zeros((KC_FRONT, LANES), F32), jnp.where(lane_f == PAD_LANE, 1.0, 0.0)], axis=1)
    pad_b = jnp.concatenate([jnp.zeros((back, LANES), F32), jnp.where(lane_b == PAD_LANE, 1.0, 0.0)], axis=1)
    for g in range(2):
        kn = jnp.where(valid, _rms(ck[g]) * gk_ref[0:1, :], 0.0)
        kc_ref[g, 0:KC_FRONT, :] = pad_f
        kc_ref[g, KC_FRONT:KC_FRONT + nseg, :] = jnp.concatenate([kn, aux], axis=1)
        kc_ref[g, KC_FRONT + nseg:KC_ROWS, :] = pad_b
        vc_ref[g, 0:KC_FRONT, :] = jnp.zeros((KC_FRONT, LANES), F32)
        vc_ref[g, KC_FRONT:KC_FRONT + nseg, :] = jnp.where(valid, cv[g], 0.0)
        vc_ref[g, KC_FRONT + nseg:KC_ROWS, :] = jnp.zeros((back, LANES), F32)


def _compress_prompt(kvc, wk_pairs, wv_pairs, pos_rows, gk):
    b, t4, _ = kvc.shape
    full = lambda a: pl.BlockSpec(a.shape, lambda bi: (0,) * a.ndim)
    return pl.pallas_call(
        _compress_prompt_kernel,
        grid=(b,),
        in_specs=[pl.BlockSpec((None, t4, LANES), lambda bi: (bi, 0, 0)),
                  full(wk_pairs), full(wv_pairs), full(pos_rows), full(gk)],
        out_specs=[pl.BlockSpec((None, 2, KC_ROWS, 2 * LANES), lambda bi: (bi, 0, 0, 0)),
                   pl.BlockSpec((None, 2, KC_ROWS, LANES), lambda bi: (bi, 0, 0, 0))],
        out_shape=(jax.ShapeDtypeStruct((b, 2, KC_ROWS, 2 * LANES), F32),
                   jax.ShapeDtypeStruct((b, 2, KC_ROWS, LANES), F32)),
        compiler_params=_cparams(("arbitrary",)),
        name="compress_prompt",
    )(kvc, wk_pairs, wv_pairs, pos_rows, gk)


def _nsa_prompt_kernel(q_ref, gate_ref, kc_ref, vc_ref, ov_ref, ks_ref, vs_ref, kw_ref, vw_ref,
                       tc_ref, ts_ref, tw_ref, o_ref):
    qb = pl.program_id(2)
    t0 = qb * QBLOCK
    rq = 4 * QBLOCK
    q = q_ref[...]
    q4 = jnp.concatenate([q[:, r * LANES:(r + 1) * LANES] for r in range(4)], axis=0)
    lane = lax.broadcasted_iota(jnp.int32, (rq, LANES), 1)
    is_pad_lane = lane == PAD_LANE

    def q_aug(mb):
        return jnp.concatenate([q4, mb.astype(BF16)], axis=1)

    qa_pad = q_aug(jnp.where(is_pad_lane, NEG, 0.0))

    far_mask = ((lane < 32) & (lane >= qb - 2)) | is_pad_lane
    qa_cfar = q_aug(jnp.where(far_mask, NEG, 0.0))
    ncmp = 256
    near0 = pl.multiple_of(qb * 8, 8)
    k_far = kc_ref[KC_FRONT:KC_FRONT + ncmp, :].astype(BF16)
    k_near = kc_ref[pl.ds(near0, LANES), :].astype(BF16)
    s_far = _dot_nt(qa_cfar, k_far)
    s_near = _dot_nt(qa_pad, k_near) + tc_ref[...]
    m = jnp.maximum(jnp.max(s_far, axis=-1, keepdims=True), jnp.max(s_near, axis=-1, keepdims=True))
    p_far = jnp.where(s_far > HALF_NEG, jnp.exp(s_far - m), 0.0)
    p_near = jnp.where(s_near > HALF_NEG, jnp.exp(s_near - m), 0.0)
    l = jnp.sum(p_far, axis=-1, keepdims=True) + jnp.sum(p_near, axis=-1, keepdims=True)
    linv = 1.0 / jnp.where(l > 0.0, l, 1.0)
    v_far = vc_ref[KC_FRONT:KC_FRONT + ncmp, :].astype(BF16)
    v_near = vc_ref[pl.ds(near0, LANES), :].astype(BF16)
    o_c = (_dot(p_far.astype(BF16), v_far) + _dot(p_near.astype(BF16), v_near)) * linv

    pn_far = p_far * linv
    pn_near = p_near * linv
    ps_far = sum(pn_far[r * QBLOCK:(r + 1) * QBLOCK] for r in range(4))
    ps_near = sum(pn_near[r * QBLOCK:(r + 1) * QBLOCK] for r in range(4))
    ov_far = ov_ref[KC_FRONT:KC_FRONT + ncmp, :].astype(BF16)
    ov_near = ov_ref[pl.ds(near0, LANES), :].astype(BF16)
    imp = _dot_hilo(ps_far, ov_far) + _dot_hilo(ps_near, ov_near)
    nblk = 64
    imp_t = imp.T[:nblk]
    blk = lax.broadcasted_iota(jnp.int32, (nblk, QBLOCK), 0)
    qpos = t0 + lax.broadcasted_iota(jnp.int32, (nblk, QBLOCK), 1)
    cur = qpos // SEL_LEN
    forced = (blk == 0) | (blk == cur) | (blk == cur - 1)
    eligible = blk * SEL_LEN <= qpos
    val = jnp.where(forced, FORCE, jnp.where(eligible, imp_t, -1.0))
    rank = jnp.zeros((nblk, QBLOCK), F32)
    for i in range(nblk):
        ri = val[i:i + 1, :]
        beats = (ri > val) | ((ri == val) & (blk > i))
        rank = rank + jnp.where(beats, 1.0, 0.0)
    mb_t = jnp.where(rank < float(N_SEL), 0.0, NEG)
    row2 = lax.broadcasted_iota(jnp.int32, (LANES - nblk, QBLOCK), 0)
    mb_t = jnp.concatenate([mb_t, jnp.where(row2 == PAD_LANE - nblk, NEG, 0.0)], axis=0)
    mb = mb_t.T
    mb4 = jnp.concatenate([mb] * 4, axis=0)
    qa_snear = q_aug(mb4)
    qa_sfar = q_aug(jnp.where((lane < nblk) & (lane >= 2 * qb - 2), NEG, mb4))

    sn0 = pl.multiple_of(t0 + KPAD - QBLOCK, QBLOCK)
    s = _dot_nt(qa_snear, ks_ref[pl.ds(sn0, 2 * QBLOCK), :]) + ts_ref[...]
    m = jnp.max(s, axis=-1, keepdims=True)
    p = jnp.exp(s - m)
    l = jnp.sum(p, axis=-1, keepdims=True)
    acc = _dot(p.astype(BF16), vs_ref[pl.ds(sn0, 2 * QBLOCK), :])
    far_tile = 512
    n_far = (jnp.maximum(qb - 1, 0) * QBLOCK + far_tile - 1) // far_tile

    def far_body(j, carry):
        m, l, acc = carry
        r0 = pl.multiple_of(KPAD + j * far_tile, far_tile)
        s = _dot_nt(qa_sfar, ks_ref[pl.ds(r0, far_tile), :])
        m_new = jnp.maximum(m, jnp.max(s, axis=-1, keepdims=True))
        alpha = jnp.exp(m - m_new)
        p = jnp.exp(s - m_new)
        l = alpha * l + jnp.sum(p, axis=-1, keepdims=True)
        acc = alpha * acc + _dot(p.astype(BF16), vs_ref[pl.ds(r0, far_tile), :])
        return m_new, l, acc

    m, l, acc = lax.fori_loop(0, n_far, far_body, (m, l, acc))
    o_s = acc * (1.0 / l)

    w0 = pl.multiple_of(t0, QBLOCK)
    wlen = WINDOW + QBLOCK
    s = _dot_nt(qa_pad, kw_ref[pl.ds(w0, wlen), :]) + tw_ref[...]
    m = jnp.max(s, axis=-1, keepdims=True)
    p = jnp.exp(s - m)
    l = jnp.sum(p, axis=-1, keepdims=True)
    o_w = _dot(p.astype(BF16), vw_ref[pl.ds(w0, wlen), :]) * (1.0 / l)

    gt = gate_ref[...]
    outs = []
    for r in range(4):
        rs = slice(r * QBLOCK, (r + 1) * QBLOCK)
        outs.append(gt[:, 3 * r:3 * r + 1] * o_c[rs] + gt[:, 3 * r + 1:3 * r + 2] * o_s[rs]
                    + gt[:, 3 * r + 2:3 * r + 3] * o_w[rs])
    o_ref[...] = jnp.concatenate(outs, axis=1).astype(BF16)


def _nsa_prompt(q, gates_g, kc, vc, ov, ks, vs, kw, vw, tab_c, tab_s, tab_w):
    b, t, _ = q.shape
    nqb = t // QBLOCK
    kvspec = lambda a: pl.BlockSpec((None, None) + a.shape[2:], lambda bi, g, i: (bi, g, 0, 0))
    tabspec = lambda a: pl.BlockSpec((None,) + a.shape[1:], lambda bi, g, i: (g, 0, 0))
    return pl.pallas_call(
        _nsa_prompt_kernel,
        grid=(b, 2, nqb),
        in_specs=[pl.BlockSpec((None, QBLOCK, 512), lambda bi, g, i: (bi, i, g)),
                  pl.BlockSpec((None, None, QBLOCK, LANES), lambda bi, g, i: (bi, g, i, 0)),
                  kvspec(kc), kvspec(vc), pl.BlockSpec(ov.shape, lambda bi, g, i: (0, 0)),
                  kvspec(ks), kvspec(vs), kvspec(kw), kvspec(vw),
                  tabspec(tab_c), tabspec(tab_s), tabspec(tab_w)],
        out_specs=pl.BlockSpec((None, QBLOCK, 512), lambda bi, g, i: (bi, i, g)),
        out_shape=jax.ShapeDtypeStruct((b, t, 1024), BF16),
        compiler_params=_cparams(("arbitrary", "arbitrary", "arbitrary")),
        name="nsa_prompt",
    )(q, gates_g, kc, vc, ov, ks, vs, kw, vw, tab_c, tab_s, tab_w)


def _decode_attend(qf, g0rows, gk_row, k_tiles, v_tiles, bias, key_mask, new_row, bias_new):
    qg = qf * gk_row
    zero = jnp.zeros_like(qg)
    qbd = jnp.concatenate([jnp.where(g0rows, qg, zero), jnp.where(g0rows, zero, qg)], axis=1).astype(BF16)
    one = jnp.ones_like(qg)
    ones_bd = jnp.concatenate([jnp.where(g0rows, one, zero), jnp.where(g0rows, zero, one)], axis=1).astype(BF16)
    s_parts, q_parts = [], []
    for kt in k_tiles:
        k = kt()
        s_parts.append(_dot_nt(qbd, k.astype(BF16)))
        q_parts.append(_dot_nt(ones_bd, (k * k).astype(BF16)))
    s = jnp.concatenate(s_parts, axis=1)
    ssq = jnp.concatenate(q_parts, axis=1)
    s = s * lax.rsqrt(ssq * (1.0 / HEAD_DIM) + EPS) + bias
    if key_mask is not None:
        s = jnp.where(key_mask > 0.5, s, NEG)
    bc = lambda c: jnp.broadcast_to(new_row[c:c + 1, :], qf.shape)
    k_new = jnp.where(g0rows, bc(0), bc(1))
    v_new = jnp.where(g0rows, bc(2), bc(3))
    s_new = jnp.sum(qg * _rms(k_new), axis=-1, keepdims=True) + bias_new
    m = jnp.maximum(jnp.max(s, axis=-1, keepdims=True), s_new)
    p = jnp.exp(s - m)
    p_new = jnp.exp(s_new - m)
    l = jnp.sum(p, axis=-1, keepdims=True) + p_new
    o2 = jnp.zeros((qf.shape[0], 2 * LANES), F32)
    for i, vt in enumerate(v_tiles):
        o2 = o2 + _dot(p[:, i * LANES:(i + 1) * LANES].astype(BF16), vt().astype(BF16))
    o = jnp.where(g0rows, o2[:, :LANES], o2[:, LANES:]) + p_new * v_new
    return o * (1.0 / l)


def _nsa_sample_kernel(pt_ref, *refs, n_pages, n_sel_blocks, group):
    del pt_ref
    pages_c = refs[:group * n_pages]
    pages_s = refs[group * n_pages:2 * group * n_pages]
    rest = refs[2 * group * n_pages:]
    per_b_in, consts, outs = rest[:5], rest[5:-2], rest[-2:]
    chains = [_nsa_sample_one(pages_c[bb * n_pages:(bb + 1) * n_pages], pages_s[bb * n_pages:(bb + 1) * n_pages],
                              *[r.at[bb] for r in per_b_in], *consts, *[r.at[bb] for r in outs],
                              n_pages=n_pages, n_sel_blocks=n_sel_blocks) for bb in range(group)]
    while chains:
        chains = [c for c in chains if next(c, "done") != "done"]


def _nsa_sample_one(pages_c, pages_s, win_ref, q_ref, gate_ref, ksn_ref, kwn_ref, wk_ref, wv_ref, pos_ref, gk_ref,
                    bc_ref, bs_ref, bw_ref, bnew_ref, ov_ref, e_ref, perm_ref, o_ref, wout_ref, *,
                    n_pages, n_sel_blocks):
    qf = q_ref[...]
    nh = qf.shape[0]
    g0rows = lax.broadcasted_iota(jnp.int32, (nh, LANES), 0) < nh // 2
    lane = lax.broadcasted_iota(jnp.int32, (nh, LANES), 1)

    page = pages_c[0].shape[0] // 4
    nseg = n_pages * page // CMP_STRIDE
    perm = perm_ref[...]
    regrouped = []
    for pp in range(n_pages // 2):
        per_cp = []
        for cp in range(2):
            blk = jnp.concatenate(
                [jnp.concatenate([pages_c[2 * pp + i][pl.ds(2 * cp + c, page, stride=4), :] for c in range(2)], axis=1)
                 for i in range(2)], axis=0).astype(BF16)
            per_cp.append(_dot(perm, blk).astype(BF16))
        regrouped.append(per_cp)
    rows_per_tap = 2 * page // CMP_STRIDE
    yield

    def tap_rows(s, col):
        lo, hi = s * rows_per_tap, (s + 1) * rows_per_tap
        return jnp.concatenate([regrouped[pp][col // 2][lo:hi, (col % 2) * LANES:(col % 2 + 1) * LANES]
                                for pp in range(n_pages // 2)], axis=0)

    load_pair = lambda s2, col: jnp.concatenate([tap_rows(2 * s2, col), tap_rows(2 * s2 + 1, col)], axis=1)
    ck, cv = _compress_core(load_pair, nseg, wk_ref, wv_ref, pos_ref)
    yield
    kcn = jnp.concatenate([(_rms(ck[g]) * gk_ref[0:1, :]).astype(BF16) for g in range(2)], axis=0)
    s2 = _dot_nt(qf.astype(BF16), kcn)
    s = jnp.where(g0rows, s2[:, :nseg], s2[:, nseg:]) + bc_ref[...]
    m = jnp.max(s, axis=-1, keepdims=True)
    p = jnp.exp(s - m)
    linv = 1.0 / jnp.sum(p, axis=-1, keepdims=True)
    rowv = lax.broadcasted_iota(jnp.int32, (nseg, LANES), 0) < nseg - 1
    pb = p.astype(BF16)
    oc = [_dot(pb, jnp.where(rowv, cv[g], 0.0).astype(BF16)) for g in range(2)]
    o_c = jnp.where(g0rows, oc[0], oc[1]) * linv

    yield
    pn = p * linv
    s0 = jnp.sum(jnp.where(g0rows, pn, 0.0), axis=0, keepdims=True)
    s1 = jnp.sum(jnp.where(g0rows, 0.0, pn), axis=0, keepdims=True)
    psum = jnp.where(g0rows, jnp.broadcast_to(s0, pn.shape), jnp.broadcast_to(s1, pn.shape))
    imp = _dot_hilo(psum, ov_ref[...].astype(BF16))
    cur = n_sel_blocks - 1
    forced = (lane == 0) | (lane == cur) | (lane == cur - 1)
    val = jnp.where(lane >= n_sel_blocks, -2.0, jnp.where(forced, FORCE, imp))
    rank = jnp.zeros_like(val)
    for i in range(n_sel_blocks):
        ci = jnp.broadcast_to(val[:, i:i + 1], val.shape)
        beats = (ci > val) | ((ci == val) & (lane > i))
        rank = rank + jnp.where(beats, 1.0, 0.0)
    sel = jnp.where((rank < float(N_SEL)) & (lane < n_sel_blocks), 1.0, 0.0)
    key_mask = _dot(sel.astype(BF16), e_ref[...])

    yield
    def tiles(ref_list, col):
        def tile(rf, i):
            rows = lambda c: rf[pl.ds(4 * LANES * i + c, LANES, stride=4), :]
            return lambda: jnp.concatenate([rows(col), rows(col + 1)], axis=1)
        return [tile(rf, i) for rf in ref_list for i in range(rf.shape[0] // (4 * LANES))]

    b_new = bnew_ref[:, 0:1]
    o_s = _decode_attend(qf, g0rows, gk_ref[1:2, :], tiles(pages_s, 0), tiles(pages_s, 2),
                         bs_ref[...], key_mask, ksn_ref[...], b_new)
    yield
    o_w = _decode_attend(qf, g0rows, gk_ref[2:3, :], tiles([win_ref], 0), tiles([win_ref], 2),
                         bw_ref[...], None, kwn_ref[...], b_new)
    yield
    gt = gate_ref[...]
    o_ref[...] = gt[:, 0:1] * o_c + gt[:, 1:2] * o_s + gt[:, 2:3] * o_w

    keep = wout_ref.shape[0] - 4
    drop = win_ref.shape[0] - keep
    wout_ref[0:keep, :] = win_ref[drop:drop + keep, :]
    wout_ref[keep:keep + 4, :] = kwn_ref[...]


def _nsa_sample(page_table, cache_c, cache_s, win, q, gates, ks_new, kw_new, wk_pairs, wv_pairs, pos_rows, gk,
                bias_c, bias_s, bias_w, bias_new, ov, expand):
    nb, n_pages = page_table.shape
    page_rows = cache_c.shape[1]
    n_sel_blocks = -(-(n_pages * (page_rows // 4) + 1) // SEL_LEN)
    group = SAMPLE_GROUP if nb % SAMPLE_GROUP == 0 else 1
    kern = functools.partial(_nsa_sample_kernel, n_pages=n_pages, n_sel_blocks=n_sel_blocks, group=group)
    win_rows_out = 4 * min(WINDOW, n_pages * (page_rows // 4) + 1)
    page = page_rows // 4
    assert n_pages % 2 == 0 and page % CMP_STRIDE == 0
    segs = page // CMP_STRIDE
    i_, n_, s_ = np.meshgrid(np.arange(2), np.arange(segs), np.arange(CMP_STRIDE), indexing="ij")
    perm_np = np.zeros((2 * page, 2 * page), np.float32)
    perm_np[(s_ * 2 * segs + i_ * segs + n_).ravel(), (i_ * page + CMP_STRIDE * n_ + s_).ravel()] = 1.0
    perm = jnp.asarray(perm_np, BF16)
    page_spec = lambda bb, p: pl.BlockSpec((None, page_rows, LANES),
                                           lambda bi, pt, bb=bb, p=p: (pt[group * bi + bb, p], 0, 0))
    pages = [page_spec(bb, p) for bb in range(group) for p in range(n_pages)]
    full = lambda a: pl.BlockSpec(a.shape, lambda bi, pt: (0,) * a.ndim)
    per_b = lambda a: pl.BlockSpec((group,) + a.shape[1:], lambda bi, pt: (bi,) + (0,) * (a.ndim - 1))
    grid_spec = pltpu.PrefetchScalarGridSpec(
        num_scalar_prefetch=1,
        grid=(nb // group,),
        in_specs=(pages + pages
                  + [per_b(win), per_b(q), per_b(gates), per_b(ks_new), per_b(kw_new),
                     full(wk_pairs), full(wv_pairs), full(pos_rows), full(gk),
                     full(bias_c), full(bias_s), full(bias_w), full(bias_new), full(ov), full(expand), full(perm)]),
        out_specs=[pl.BlockSpec((group,) + q.shape[1:], lambda bi, pt: (bi, 0, 0)),
                   pl.BlockSpec((group, win_rows_out, LANES), lambda bi, pt: (bi, 0, 0))],
    )
    return pl.pallas_call(
        kern,
        grid_spec=grid_spec,
        out_shape=(jax.ShapeDtypeStruct(q.shape, F32), jax.ShapeDtypeStruct((nb, win_rows_out, LANES), F32)),
        compiler_params=_cparams(("arbitrary",)),
        name="nsa_sample",
    )(page_table, *([cache_c] * (group * n_pages)), *([cache_s] * (group * n_pages)), win, q, gates, ks_new, kw_new,
      wk_pairs, wv_pairs, pos_rows, gk, bias_c, bias_s, bias_w, bias_new, ov, expand, perm)


def _merge_kernel(h_ref, oa_ref, ob_ref, wga_ref, wgb_ref, wg_ref, wn_ref, mix_ref, *cast_refs):
    h = h_ref[...]
    wga = wga_ref[...].astype(BF16)
    wgb = wgb_ref[...].astype(BF16)
    wg = wg_ref[...].astype(BF16)
    wn = wn_ref[...].astype(BF16)
    ga = _sigmoid(_dot_nt(h, wga))
    gb = _sigmoid(_dot_nt(h, wgb))
    mix_ref[...] = (ga * _dot(oa_ref[...], wg) + gb * _dot(ob_ref[...], wn)).astype(BF16)
    for ref, val in zip(cast_refs, (wga, wgb, wg, wn)):
        ref[...] = val


def _merge(h, oa, ob, wga, wgb, wg, wn, gate_rows=None):
    n, d = h.shape
    tm = 512 if n % 512 == 0 else n
    tn = 512
    emit = gate_rows is not None
    assert not emit or n == tm
    row = lambda w: pl.BlockSpec((tm, w), lambda i, j: (i, 0))
    col = lambda k: pl.BlockSpec((k, tn), lambda i, j: (0, j))
    colt = pl.BlockSpec((tn, d), lambda i, j: (j, 0))
    out_specs = [pl.BlockSpec((tm, tn), lambda i, j: (i, j))]
    out_shape = [jax.ShapeDtypeStruct((n, d), BF16)]
    gate_specs = [colt, colt]
    if emit:
        assert all(r % 8 == 0 for r in gate_rows)
        gate_specs = [pl.BlockSpec((pl.Element(tn), pl.Element(d)),
                                   lambda i, j, r=r: ((r // 8 + j * (tn // 8)) * 8, 0)) for r in gate_rows]
        out_specs += [colt, colt, col(wg.shape[0]), col(wn.shape[0])]
        out_shape += [jax.ShapeDtypeStruct((d, d), BF16), jax.ShapeDtypeStruct((d, d), BF16),
                      jax.ShapeDtypeStruct(wg.shape, BF16), jax.ShapeDtypeStruct(wn.shape, BF16)]
    return pl.pallas_call(
        _merge_kernel,
        grid=(n // tm, d // tn),
        in_specs=[row(d), row(oa.shape[1]), row(ob.shape[1])] + gate_specs + [col(wg.shape[0]), col(wn.shape[0])],
        out_specs=out_specs,
        out_shape=out_shape,
        compiler_params=_cparams(("arbitrary", "arbitrary")),
        name="merge",
    )(h, oa, ob, wga, wgb, wg, wn)


def _outproj_kernel(x_ref, mix_ref, wout_ref, gmlp_ref, x1_ref, hm_ref):
    x1 = x_ref[...] + _dot(mix_ref[...], wout_ref[...])
    x1_ref[...] = x1
    hm_ref[...] = (_rms(x1) * gmlp_ref[...]).astype(BF16)


def _outproj(x, mix, wout, gmlp):
    n, d = x.shape
    tm = 512 if n % 512 == 0 else n
    row = pl.BlockSpec((tm, d), lambda i: (i, 0))
    return pl.pallas_call(
        _outproj_kernel,
        grid=(n // tm,),
        in_specs=[row, row, pl.BlockSpec(wout.shape, lambda i: (0, 0)), pl.BlockSpec(gmlp.shape, lambda i: (0, 0))],
        out_specs=[row, row],
        out_shape=(jax.ShapeDtypeStruct((n, d), F32), jax.ShapeDtypeStruct((n, d), BF16)),
        compiler_params=_cparams(("arbitrary",)),
        name="outproj",
    )(x, mix, wout, gmlp)


def _ffn_kernel(hm_ref, x1_ref, wup_ref, wdown_ref, y_ref, *cast_refs):
    f = pl.program_id(1)

    @pl.when(f == 0)
    def _():
        y_ref[...] = x1_ref[...]

    wup = wup_ref[...].astype(BF16)
    wdown = wdown_ref[...].astype(BF16)
    hid = jnp.maximum(_dot(hm_ref[...], wup), 0.0)
    y_ref[...] += _dot((hid * hid).astype(BF16), wdown)
    if cast_refs:
        cast_refs[0][...] = wup
        cast_refs[1][...] = wdown


def _ffn(hm, x1, wup, wdown):
    n, d = hm.shape
    dff = wup.shape[1]
    tm = 512 if n % 512 == 0 else n
    emit = wup.dtype != BF16
    assert not emit or n == tm
    tf = 512 if emit else 1024
    row = pl.BlockSpec((tm, d), lambda i, f: (i, 0))
    up_spec = pl.BlockSpec((d, tf), lambda i, f: (0, f))
    down_spec = pl.BlockSpec((tf, d), lambda i, f: (f, 0))
    out_specs = [row]
    out_shape = [jax.ShapeDtypeStruct((n, d), F32)]
    if emit:
        out_specs += [up_spec, down_spec]
        out_shape += [jax.ShapeDtypeStruct(wup.shape, BF16), jax.ShapeDtypeStruct(wdown.shape, BF16)]
    return pl.pallas_call(
        _ffn_kernel,
        grid=(n // tm, dff // tf),
        in_specs=[row, row, up_spec, down_spec],
        out_specs=out_specs,
        out_shape=out_shape,
        compiler_params=_cparams(("arbitrary", "arbitrary")),
        name="ffn",
    )(hm, x1, wup, wdown)


def _bucket(rel, valid):
    n = np.maximum(rel, 0)
    max_exact = N_BUCKETS // 2
    nf = np.maximum(n, 1).astype(np.float32)
    large = max_exact + (np.log(nf / np.float32(max_exact)) / np.float32(math.log(MAX_DIST / max_exact))
                         * np.float32(N_BUCKETS - max_exact)).astype(np.int32)
    large = np.minimum(large, N_BUCKETS - 1)
    return np.where(valid, np.where(n < max_exact, n, large), -1).astype(np.int32)


def _bias_tables_kernel(rb_ref, *refs, shifts):
    n = len(shifts)
    nbk, nh = rb_ref.shape
    for b_ref, o_ref, shift in zip(refs[:n], refs[n:], shifts):
        b = b_ref[...]
        rows = []
        for h in range(nh):
            sh = rb_ref[nbk - 1, h] if shift else 0.0
            acc = jnp.full(b.shape, NEG, F32)
            for k in range(nbk):
                acc = jnp.where(b == k, rb_ref[k, h] - sh, acc)
            if len(o_ref.shape) == 3:
                o_ref[h] = acc
            else:
                rows.append(acc)
        if rows:
            o_ref[...] = jnp.concatenate(rows, axis=0)


def _bias_tables(rel_bias, buckets, shifts):
    nh = rel_bias.shape[1]
    shapes = [jax.ShapeDtypeStruct((nh,) + (b.shape if b.shape[0] > 1 else b.shape[1:]), F32) for b in buckets]
    vm = pl.BlockSpec(memory_space=pltpu.VMEM)
    return pl.pallas_call(
        functools.partial(_bias_tables_kernel, shifts=tuple(shifts)),
        in_specs=[pl.BlockSpec(memory_space=pltpu.SMEM)] + [vm] * len(buckets),
        out_specs=[vm] * len(buckets),
        out_shape=shapes,
        name="bias_tables",
    )(rel_bias, *[jnp.asarray(b) for b in buckets])


def _overlap(nc, ns):
    i = np.arange(nc)[:, None] * CMP_STRIDE
    j = np.arange(ns)[None, :] * SEL_LEN
    return ((i < j + SEL_LEN) & (i + CMP_LEN > j)).astype(np.float32)


def _compress_weights(w, pos):
    s = np.arange(0, CMP_STRIDE, 2)
    top = jnp.concatenate([w[s], w[CMP_STRIDE + s]], axis=2)
    bot = jnp.concatenate([w[s + 1], w[CMP_STRIDE + s + 1]], axis=2)
    tiles = jnp.concatenate([top, bot], axis=1).astype(BF16)
    row_a = jnp.concatenate([pos[s], pos[s + 1]], axis=1)
    row_b = jnp.concatenate([pos[CMP_STRIDE + s], pos[CMP_STRIDE + s + 1]], axis=1)
    rows = jnp.zeros((len(s), 16, 2 * HEAD_DIM), F32).at[:, 0].set(row_a).at[:, 1].set(row_b)
    return tiles, rows.astype(BF16)


def _dense_tail(x, h, oa, ob, wga, wgb, wg, wn, wout, gmlp, wup, wdown, gate_rows=None):
    mix, *cast_m = _merge(h, oa, ob, wga, wgb, wg, wn, gate_rows)
    x1, hm = _outproj(x, mix, wout, gmlp)
    y, *cast_f = _ffn(hm, x1, wup, wdown)
    return y, (cast_m or [wga, wgb, wg, wn]) + (cast_f or [wup, wdown])


def kernel(x_prompt, x_sample, cache_cmp_kv, cache_sel_kv, state_win_kv, page_table, rel_bias, g_mix_norm, w_in,
           g_sgu, w_sgu, b_sgu, g_q, g_k, pos_cmp_k, w_cmp_k, pos_cmp_v, w_cmp_v, w_proj_gmlp, w_proj_nsa, w_out,
           g_mlp_norm, w_up, w_down):
    depth = g_mix_norm.shape[0]
    assert depth == 1
    l = 0
    bsz, seq, d = x_prompt.shape
    nb = x_sample.shape[0]
    assert x_sample.shape[1] == 1 and seq % KPAD == 0
    d_gm = g_sgu.shape[1]
    n_heads = rel_bias.shape[1]
    d_nsa = n_heads * HEAD_DIM
    kvw_cols = 2 * (n_heads // 4) * HEAD_DIM
    n_gate = 3 * n_heads
    c_q = 2 * d_gm
    c_kv = c_q + d_nsa
    c_gate = c_kv + 3 * kvw_cols
    c_ga = c_gate + n_gate
    c_gb = c_ga + d

    w = jnp.swapaxes(w_in[l], 0, 1)
    n_col_tiles = -(-c_ga // 1024)
    wout = w_out[l].astype(BF16)
    gmix = g_mix_norm[l][None]
    gsgu = g_sgu[l][None]
    gq = g_q[l][None]
    gk = g_k[l]
    gmlp = g_mlp_norm[l][None]
    wk_pairs, posk_rows = _compress_weights(w_cmp_k[l], pos_cmp_k[l])
    wv_pairs, posv_rows = _compress_weights(w_cmp_v[l], pos_cmp_v[l])
    pos_rows = jnp.stack([posk_rows, posv_rows])
    n_groups = w_sgu.shape[1]
    b_exp = jnp.repeat(b_sgu[l].T, d_gm // n_groups, axis=1)
    w00 = jnp.repeat(w_sgu[l][:, 0, 0], d_gm // n_groups)[None]
    b00 = jnp.repeat(b_sgu[l][:, 0], d_gm // n_groups)[None]

    n_pool, page = cache_cmp_kv.shape[1], cache_cmp_kv.shape[2]
    n_pages = page_table.shape[1]
    past = n_pages * page
    nwin = state_win_kv.shape[2]
    n_cmp_s = (past + 1 - CMP_LEN) // CMP_STRIDE + 1
    n_sel_s = -(-(past + 1) // SEL_LEN)
    nseg_s = past // CMP_STRIDE
    qi = np.arange(QBLOCK)[:, None]
    rel_s = qi + QBLOCK - np.arange(2 * QBLOCK)[None, :]
    rel_w = qi + WINDOW - np.arange(WINDOW + QBLOCK)[None, :]
    rel_c = qi + (KC_FRONT * CMP_STRIDE - CMP_LEN + 1) - CMP_STRIDE * np.arange(LANES)[None, :]
    srel_c = (past - (np.arange(nseg_s) * CMP_STRIDE + CMP_LEN - 1))[None]
    srel_s = (past - np.arange(past))[None]
    srel_w = (nwin - np.arange(nwin))[None]
    buckets = [
        _bucket(rel_c, rel_c >= 0), _bucket(rel_s, rel_s >= 0), _bucket(rel_w, (rel_w >= 0) & (rel_w < WINDOW)),
        _bucket(srel_c, (srel_c >= 0) & (np.arange(nseg_s)[None] < n_cmp_s)), _bucket(srel_s, srel_s >= 0),
        _bucket(srel_w, srel_w < WINDOW), np.zeros((1, LANES), np.int32)]
    tab_c, tab_s, tab_w, bias_c, bias_s, bias_w, bias_new = _bias_tables(
        rel_bias, buckets, [True, True, False, False, False, False, False])
    tab_c, tab_s, tab_w = [t.reshape(2, (n_heads // 2) * QBLOCK, t.shape[-1]) for t in (tab_c, tab_s, tab_w)]

    xs = x_sample.reshape(nb, d)
    h_s, v_s, oa_s, q_s, kvc_s, kvs_s, kvw_s, gates_s, w_cat = _inproj(
        xs, gmix, w, gsgu, gq, w00, b00, chunked=False, n_col_tiles=n_col_tiles)
    ov_s = np.zeros((nseg_s, LANES), np.float32)
    ov_s[:n_cmp_s, :n_sel_s] = _overlap(n_cmp_s, n_sel_s)
    expand = jnp.asarray(np.arange(LANES)[:, None] == (np.arange(past)[None, :] // SEL_LEN), BF16)
    gates_h = jnp.pad(gates_s[:, :n_gate].reshape(nb, n_heads, 3), ((0, 0), (0, 0), (0, LANES - 3)))
    lin = lambda a: a[l].reshape(a.shape[1], -1, HEAD_DIM)
    win_lin = lin(state_win_kv)
    ob_s, win_new = _nsa_sample(
        page_table, lin(cache_cmp_kv), lin(cache_sel_kv), win_lin,
        q_s.astype(F32).reshape(nb, n_heads, HEAD_DIM), gates_h,
        kvs_s.reshape(nb, 4, HEAD_DIM), kvw_s.reshape(nb, 4, HEAD_DIM), wk_pairs, wv_pairs, pos_rows, gk,
        bias_c, bias_s, bias_w, bias_new, jnp.asarray(ov_s), expand)
    y_s, (wga, wgb, wg, wn, wup, wdown) = _dense_tail(
        xs, h_s, oa_s, ob_s.reshape(nb, d_nsa).astype(BF16), w, w, w_proj_gmlp[l], w_proj_nsa[l], wout, gmlp,
        w_up[l], w_down[l], gate_rows=(c_ga, c_gb))

    xp = x_prompt.reshape(bsz * seq, d)
    h_p, v_p, oa_p, q_p, kvc_p, kvs_p, kvw_p, gates_p = _inproj(
        xp, gmix, w_cat, gsgu, gq, w_sgu[l], b_exp, chunked=True, n_col_tiles=n_col_tiles)
    kvs3 = kvs_p.reshape(bsz, 4 * seq, HEAD_DIM)
    kvw3 = kvw_p.reshape(bsz, 4 * seq, HEAD_DIM)
    kvc3 = kvc_p.reshape(bsz, 4 * seq, HEAD_DIM)
    ks, vs, kw, vw = _kvprep(kvs3, kvw3, gk)
    kc, vc = _compress_prompt(kvc3, wk_pairs, wv_pairs, pos_rows, gk)
    n_cmp = (seq - CMP_LEN) // CMP_STRIDE + 1
    n_selb = seq // SEL_LEN
    ov_p = np.zeros((KC_ROWS, LANES), np.float32)
    ov_p[KC_FRONT:KC_FRONT + n_cmp, :n_selb] = _overlap(n_cmp, n_selb)
    ov_p = jnp.asarray(ov_p)
    gates_g = gates_p[:, :n_gate].reshape(bsz, seq, 2, n_gate // 2).transpose(0, 2, 1, 3)
    gates_g = jnp.pad(gates_g, ((0, 0), (0, 0), (0, 0), (0, LANES - n_gate // 2)))
    ob_p = _nsa_prompt(q_p.reshape(bsz, seq, d_nsa), gates_g, kc, vc, ov_p, ks, vs, kw, vw, tab_c, tab_s, tab_w)
    y_p, _ = _dense_tail(xp, h_p, oa_p, ob_p.reshape(bsz * seq, d_nsa), wga, wgb, wg, wn, wout, gmlp, wup, wdown)

    n_kv = n_heads // 4
    kv6 = lambda a, b_, t_: a.reshape(1, b_, t_, 2, n_kv, HEAD_DIM)
    nw_p = min(WINDOW, seq)
    last = ((seq - 1) // CHUNK) * CHUNK
    nw_s = min(WINDOW, past + 1)
    return (y_p.reshape(bsz, seq, d), y_s.reshape(nb, 1, d),
            kv6(kvc3, bsz, seq), kv6(kvs3, bsz, seq), kv6(kvw3[:, 4 * (seq - nw_p):], bsz, nw_p),
            v_p.reshape(bsz, seq, d_gm)[:, last:][None],
            kv6(kvc_s, nb, 1), kv6(kvs_s, nb, 1), kv6(win_new, nb, nw_s),
            v_s.reshape(1, nb, 1, d_gm))
```

```python
import functools
import math

import numpy as np
import jax
import jax.numpy as jnp
from jax import lax
from jax.experimental import pallas as pl
from jax.experimental.pallas import tpu as pltpu

F32 = jnp.float32
BF16 = jnp.bfloat16

HEAD_DIM = 128
CHUNK = 128
CMP_LEN = 32
CMP_STRIDE = 16
SEL_LEN = 64
N_SEL = 16
WINDOW = 512
N_BUCKETS = 32
MAX_DIST = 128
QBLOCK = 128
EPS = 1e-6
NEG = -1e30
HALF_NEG = -5e29
FORCE = 1e6

LANES = 128
PAD_LANE = 64
KPAD = 512
KC_FRONT = 16
KC_ROWS = 376
SAMPLE_GROUP = 2
VMEM_LIMIT = 56 * 1024 * 1024


def _cparams(sem):
    return pltpu.CompilerParams(dimension_semantics=sem, vmem_limit_bytes=VMEM_LIMIT)


def _dot(a, b):
    return jnp.dot(a, b, preferred_element_type=F32)


def _dot_nt(a, b):
    return lax.dot_general(a, b, (((1,), (1,)), ((), ())), preferred_element_type=F32)


def _dot_hilo(a, b_bf16):
    hi = a.astype(BF16)
    lo = (a - hi.astype(F32)).astype(BF16)
    return _dot(hi, b_bf16) + _dot(lo, b_bf16)


def _rms(x):
    return x * lax.rsqrt(jnp.mean(x * x, axis=-1, keepdims=True) + EPS)


def _gelu(x):
    c = math.sqrt(2.0 / math.pi)
    return 0.5 * x * (1.0 + jnp.tanh(c * (x + 0.044715 * (x * x * x))))


def _sigmoid(x):
    return 1.0 / (1.0 + jnp.exp(-x))


def _inproj_kernel(x_ref, gmix_ref, w_ref, gsgu_ref, gq_ref, wsg_ref, bsg_ref,
                   h_ref, v_ref, oa_ref, q_ref, kvc_ref, kvs_ref, kvw_ref, gate_ref, *rest, chunked, tm):
    h_s, u_s = rest[-2:]
    j = pl.program_id(1)

    @pl.when(j == 0)
    def _():
        hb = (_rms(x_ref[...]) * gmix_ref[...]).astype(BF16)
        h_s[...] = hb
        h_ref[...] = hb

    if len(rest) == 3:
        w = w_ref[...].astype(BF16)
        rest[0][...] = w
        z = _dot_nt(h_s[...], w)
    else:
        z = _dot_nt(h_s[...], w_ref[...])

    @pl.when(j == 0)
    def _():
        u_s[...] = _gelu(z)

    @pl.when(j == 1)
    def _():
        v = _rms(_gelu(z)) * gsgu_ref[...]
        v_ref[...] = v
        if chunked:
            row = lax.broadcasted_iota(jnp.int32, (CHUNK, CHUNK), 0)
            col = lax.broadcasted_iota(jnp.int32, (CHUNK, CHUNK), 1)
            n_groups = v.shape[1] // LANES
            for g in range(n_groups):
                wm = jnp.where(row >= col, wsg_ref[g], 0.0).astype(BF16)
                cs = slice(g * LANES, (g + 1) * LANES)
                for c in range(tm // CHUNK):
                    rs = slice(c * CHUNK, (c + 1) * CHUNK)
                    s = _dot(wm, v[rs, cs].astype(BF16)) + bsg_ref[:, cs]
                    oa_ref[rs, cs] = (u_s[rs, cs] * s).astype(BF16)
        else:
            oa_ref[...] = (u_s[...] * (v * wsg_ref[...] + bsg_ref[...])).astype(BF16)

    @pl.when(j == 2)
    def _():
        scale = HEAD_DIM ** -0.5
        for hd in range(z.shape[1] // HEAD_DIM):
            cs = slice(hd * HEAD_DIM, (hd + 1) * HEAD_DIM)
            q_ref[:, cs] = (_rms(z[:, cs]) * gq_ref[...] * scale).astype(BF16)

    def store_kv(ref, zz):
        for c in range(4):
            ref[pl.ds(c, tm, stride=4), :] = zz[:, c * LANES:(c + 1) * LANES]

    @pl.when(j == 3)
    def _():
        store_kv(kvc_ref, z[:, :512])
        store_kv(kvs_ref, z[:, 512:])

    @pl.when(j == 4)
    def _():
        store_kv(kvw_ref, z[:, :512])
        gate_ref[...] = _sigmoid(z[:, 512:640])


def _inproj(x, gmix, w_cat, gsgu, gq, wsg, bsg, *, chunked, n_col_tiles):
    n, d = x.shape
    tm = 512 if n % 512 == 0 else n
    tn = 1024
    emit = w_cat.dtype != BF16
    assert not emit or n == tm
    kern = functools.partial(_inproj_kernel, chunked=chunked, tm=tm)
    full = lambda a: pl.BlockSpec(a.shape, lambda i, j: (0,) * a.ndim)
    row = lambda w: pl.BlockSpec((tm, w), lambda i, j: (i, 0))
    kvrow = pl.BlockSpec((4 * tm, LANES), lambda i, j: (i, 0))
    out_shapes = (
        jax.ShapeDtypeStruct((n, d), BF16),
        jax.ShapeDtypeStruct((n, 1024), F32),
        jax.ShapeDtypeStruct((n, 1024), BF16),
        jax.ShapeDtypeStruct((n, 1024), BF16),
        jax.ShapeDtypeStruct((4 * n, LANES), F32),
        jax.ShapeDtypeStruct((4 * n, LANES), F32),
        jax.ShapeDtypeStruct((4 * n, LANES), F32),
        jax.ShapeDtypeStruct((n, LANES), F32),
    )
    wspec = pl.BlockSpec((tn, d), lambda i, j: (j, 0))
    out_specs = [row(d), row(1024), row(1024), row(1024), kvrow, kvrow, kvrow, row(LANES)]
    if emit:
        out_shapes += (jax.ShapeDtypeStruct((n_col_tiles * tn, d), BF16),)
        out_specs.append(wspec)
    return pl.pallas_call(
        kern,
        grid=(n // tm, n_col_tiles),
        in_specs=[row(d), full(gmix), wspec, full(gsgu), full(gq), full(wsg), full(bsg)],
        out_specs=out_specs,
        out_shape=out_shapes,
        scratch_shapes=[pltpu.VMEM((tm, d), BF16), pltpu.VMEM((tm, 1024), F32)],
        compiler_params=_cparams(("arbitrary", "arbitrary")),
        name="inproj",
    )(x, gmix, w_cat, gsgu, gq, wsg, bsg)


def _kvprep_kernel(kvs_ref, kvw_ref, gk_ref, ks_ref, vs_ref, kw_ref, vw_ref):
    i = pl.program_id(1)
    rows = kvs_ref.shape[0] // 4
    col = lambda ref, c: ref[pl.ds(c, rows, stride=4), :]
    lane = lax.broadcasted_iota(jnp.int32, (rows, LANES), 1)
    row = lax.broadcasted_iota(jnp.int32, (rows, LANES), 0)

    @pl.when(i == 0)
    def _():
        aux = jnp.where(lane == PAD_LANE, 1.0, 0.0).astype(BF16)
        zk = jnp.zeros((rows, LANES), BF16)
        for g in range(2):
            ks_ref[g] = jnp.concatenate([zk, aux], axis=1)
            kw_ref[g] = jnp.concatenate([zk, aux], axis=1)
            vs_ref[g] = zk
            vw_ref[g] = zk

    @pl.when(i > 0)
    def _():
        blk = ((i - 1) * rows + row) // SEL_LEN
        onehot = jnp.where(lane == blk, 1.0, 0.0).astype(BF16)
        zaux = jnp.zeros((rows, LANES), BF16)
        for g in range(2):
            ks = (_rms(col(kvs_ref, g)) * gk_ref[1:2, :]).astype(BF16)
            kw = (_rms(col(kvw_ref, g)) * gk_ref[2:3, :]).astype(BF16)
            ks_ref[g] = jnp.concatenate([ks, onehot], axis=1)
            kw_ref[g] = jnp.concatenate([kw, zaux], axis=1)
            vs_ref[g] = col(kvs_ref, 2 + g).astype(BF16)
            vw_ref[g] = col(kvw_ref, 2 + g).astype(BF16)


def _kvprep(kvs, kvw, gk):
    b, t4, _ = kvs.shape
    t = t4 // 4
    rows = KPAD
    nblk = t // rows
    in_map = lambda bi, i: (bi, jnp.maximum(i - 1, 0), 0)
    out_map = lambda bi, i: (bi, 0, i, 0)
    kshape = jax.ShapeDtypeStruct((b, 2, KPAD + t, 2 * LANES), BF16)
    vshape = jax.ShapeDtypeStruct((b, 2, KPAD + t, LANES), BF16)
    return pl.pallas_call(
        _kvprep_kernel,
        grid=(b, nblk + 1),
        in_specs=[pl.BlockSpec((None, 4 * rows, LANES), in_map), pl.BlockSpec((None, 4 * rows, LANES), in_map),
                  pl.BlockSpec(gk.shape, lambda bi, i: (0, 0))],
        out_specs=[pl.BlockSpec((None, 2, rows, 2 * LANES), out_map), pl.BlockSpec((None, 2, rows, LANES), out_map),
                   pl.BlockSpec((None, 2, rows, 2 * LANES), out_map), pl.BlockSpec((None, 2, rows, LANES), out_map)],
        out_shape=(kshape, vshape, kshape, vshape),
        compiler_params=_cparams(("arbitrary", "arbitrary")),
        name="kvprep",
    )(kvs, kvw, gk)


def _compress_core(load_pair, nseg, wk_ref, wv_ref, pos_ref):
    w_refs = (wk_ref, wv_ref)
    accs = [jnp.zeros((2 * nseg + 16, 2 * LANES), F32) for _ in range(2)]
    for s2 in range(CMP_STRIDE // 2):
        for kv in range(2):
            parts = [load_pair(s2, 2 * kv + g) for g in range(2)]
            parts.append(pos_ref[kv, s2])
            lhs = jnp.concatenate(parts, axis=0)
            accs[kv] = accs[kv] + _dot(lhs, w_refs[kv][s2])
    outs = []
    for kv in range(2):
        y = accs[kv]
        post = y[2 * nseg:2 * nseg + 1, :LANES] + y[2 * nseg + 1:2 * nseg + 2, LANES:]
        per_g = []
        for g in range(2):
            y0 = y[g * nseg:(g + 1) * nseg, :LANES]
            y1 = y[g * nseg:(g + 1) * nseg, LANES:]
            per_g.append(y0 + pltpu.roll(y1, nseg - 1, 0) + post)
        outs.append(per_g)
    return outs


def _compress_prompt_kernel(x_ref, wk_ref, wv_ref, pos_ref, gk_ref, kc_ref, vc_ref):
    nseg = x_ref.shape[0] // (4 * CMP_STRIDE)
    load_x = lambda s, col: x_ref[pl.ds(4 * s + col, nseg, stride=4 * CMP_STRIDE), :]
    load_pair = lambda s2, col: jnp.concatenate([load_x(2 * s2, col), load_x(2 * s2 + 1, col)], axis=1).astype(BF16)
    ck, cv = _compress_core(load_pair, nseg, wk_ref, wv_ref, pos_ref)
    row = lax.broadcasted_iota(jnp.int32, (nseg, LANES), 0)
    lane = lax.broadcasted_iota(jnp.int32, (nseg, LANES), 1)
    valid = row < nseg - 1
    aux = jnp.where(valid, jnp.where(lane == row // 8, 1.0, 0.0), jnp.where(lane == PAD_LANE, 1.0, 0.0))
    back = KC_ROWS - KC_FRONT - nseg
    lane_f = lax.broadcasted_iota(jnp.int32, (KC_FRONT, LANES), 1)
    lane_b = lax.broadcasted_iota(jnp.int32, (back, LANES), 1)
    pad_f = jnp.concatenate([jnp.zeros((KC_FRONT, LANES), F32), jnp.where(lane_f == PAD_LANE, 1.0, 0.0)], axis=1)
    pad_b = jnp.concatenate([jnp.zeros((back, LANES), F32), jnp.where(lane_b == PAD_LANE, 1.0, 0.0)], axis=1)
    for g in range(2):
        kn = jnp.where(valid, _rms(ck[g]) * gk_ref[0:1, :], 0.0)
        kc_ref[g, 0:KC_FRONT, :] = pad_f
        kc_ref[g, KC_FRONT:KC_FRONT + nseg, :] = jnp.concatenate([kn, aux], axis=1)
        kc_ref[g, KC_FRONT + nseg:KC_ROWS, :] = pad_b
        vc_ref[g, 0:KC_FRONT, :] = jnp.zeros((KC_FRONT, LANES), F32)
        vc_ref[g, KC_FRONT:KC_FRONT + nseg, :] = jnp.where(valid, cv[g], 0.0)
        vc_ref[g, KC_FRONT + nseg:KC_ROWS, :] = jnp.zeros((back, LANES), F32)


def _compress_prompt(kvc, wk_pairs, wv_pairs, pos_rows, gk):
    b, t4, _ = kvc.shape
    full = lambda a: pl.BlockSpec(a.shape, lambda bi: (0,) * a.ndim)
    return pl.pallas_call(
        _compress_prompt_kernel,
        grid=(b,),
        in_specs=[pl.BlockSpec((None, t4, LANES), lambda bi: (bi, 0, 0)),
                  full(wk_pairs), full(wv_pairs), full(pos_rows), full(gk)],
        out_specs=[pl.BlockSpec((None, 2, KC_ROWS, 2 * LANES), lambda bi: (bi, 0, 0, 0)),
                   pl.BlockSpec((None, 2, KC_ROWS, LANES), lambda bi: (bi, 0, 0, 0))],
        out_shape=(jax.ShapeDtypeStruct((b, 2, KC_ROWS, 2 * LANES), F32),
                   jax.ShapeDtypeStruct((b, 2, KC_ROWS, LANES), F32)),
        compiler_params=_cparams(("arbitrary",)),
        name="compress_prompt",
    )(kvc, wk_pairs, wv_pairs, pos_rows, gk)


def _nsa_prompt_kernel(q_ref, gate_ref, kc_ref, vc_ref, ov_ref, ks_ref, vs_ref, kw_ref, vw_ref,
                       tc_ref, ts_ref, tw_ref, o_ref):
    qb = pl.program_id(2)
    t0 = qb * QBLOCK
    rq = 4 * QBLOCK
    q = q_ref[...]
    q4 = jnp.concatenate([q[:, r * LANES:(r + 1) * LANES] for r in range(4)], axis=0)
    lane = lax.broadcasted_iota(jnp.int32, (rq, LANES), 1)
    is_pad_lane = lane == PAD_LANE

    def q_aug(mb):
        return jnp.concatenate([q4, mb.astype(BF16)], axis=1)

    qa_pad = q_aug(jnp.where(is_pad_lane, NEG, 0.0))

    far_mask = ((lane < 32) & (lane >= qb - 2)) | is_pad_lane
    qa_cfar = q_aug(jnp.where(far_mask, NEG, 0.0))
    ncmp = 256
    near0 = pl.multiple_of(qb * 8, 8)
    k_far = kc_ref[KC_FRONT:KC_FRONT + ncmp, :].astype(BF16)
    k_near = kc_ref[pl.ds(near0, LANES), :].astype(BF16)
    s_far = _dot_nt(qa_cfar, k_far)
    s_near = _dot_nt(qa_pad, k_near) + tc_ref[...]
    m = jnp.maximum(jnp.max(s_far, axis=-1, keepdims=True), jnp.max(s_near, axis=-1, keepdims=True))
    p_far = jnp.where(s_far > HALF_NEG, jnp.exp(s_far - m), 0.0)
    p_near = jnp.where(s_near > HALF_NEG, jnp.exp(s_near - m), 0.0)
    l = jnp.sum(p_far, axis=-1, keepdims=True) + jnp.sum(p_near, axis=-1, keepdims=True)
    linv = 1.0 / jnp.where(l > 0.0, l, 1.0)
    v_far = vc_ref[KC_FRONT:KC_FRONT + ncmp, :].astype(BF16)
    v_near = vc_ref[pl.ds(near0, LANES), :].astype(BF16)
    o_c = (_dot(p_far.astype(BF16), v_far) + _dot(p_near.astype(BF16), v_near)) * linv

    pn_far = p_far * linv
    pn_near = p_near * linv
    ps_far = sum(pn_far[r * QBLOCK:(r + 1) * QBLOCK] for r in range(4))
    ps_near = sum(pn_near[r * QBLOCK:(r + 1) * QBLOCK] for r in range(4))
    ov_far = ov_ref[KC_FRONT:KC_FRONT + ncmp, :].astype(BF16)
    ov_near = ov_ref[pl.ds(near0, LANES), :].astype(BF16)
    imp = _dot_hilo(ps_far, ov_far) + _dot_hilo(ps_near, ov_near)
    nblk = 64
    imp_t = imp.T[:nblk]
    blk = lax.broadcasted_iota(jnp.int32, (nblk, QBLOCK), 0)
    qpos = t0 + lax.broadcasted_iota(jnp.int32, (nblk, QBLOCK), 1)
    cur = qpos // SEL_LEN
    forced = (blk == 0) | (blk == cur) | (blk == cur - 1)
    eligible = blk * SEL_LEN <= qpos
    val = jnp.where(forced, FORCE, jnp.where(eligible, imp_t, -1.0))
    rank = jnp.zeros((nblk, QBLOCK), F32)
    for i in range(nblk):
        ri = val[i:i + 1, :]
        beats = (ri > val) | ((ri == val) & (blk > i))
        rank = rank + jnp.where(beats, 1.0, 0.0)
    mb_t = jnp.where(rank < float(N_SEL), 0.0, NEG)
    row2 = lax.broadcasted_iota(jnp.int32, (LANES - nblk, QBLOCK), 0)
    mb_t = jnp.concatenate([mb_t, jnp.where(row2 == PAD_LANE - nblk, NEG, 0.0)], axis=0)
    mb = mb_t.T
    mb4 = jnp.concatenate([mb] * 4, axis=0)
    qa_snear = q_aug(mb4)
    qa_sfar = q_aug(jnp.where((lane < nblk) & (lane >= 2 * qb - 2), NEG, mb4))

    sn0 = pl.multiple_of(t0 + KPAD - QBLOCK, QBLOCK)
    s = _dot_nt(qa_snear, ks_ref[pl.ds(sn0, 2 * QBLOCK), :]) + ts_ref[...]
    m = jnp.max(s, axis=-1, keepdims=True)
    p = jnp.exp(s - m)
    l = jnp.sum(p, axis=-1, keepdims=True)
    acc = _dot(p.astype(BF16), vs_ref[pl.ds(sn0, 2 * QBLOCK), :])
    far_tile = 512
    n_far = (jnp.maximum(qb - 1, 0) * QBLOCK + far_tile - 1) // far_tile

    def far_body(j, carry):
        m, l, acc = carry
        r0 = pl.multiple_of(KPAD + j * far_tile, far_tile)
        s = _dot_nt(qa_sfar, ks_ref[pl.ds(r0, far_tile), :])
        m_new = jnp.maximum(m, jnp.max(s, axis=-1, keepdims=True))
        alpha = jnp.exp(m - m_new)
        p = jnp.exp(s - m_new)
        l = alpha * l + jnp.sum(p, axis=-1, keepdims=True)
        acc = alpha * acc + _dot(p.astype(BF16), vs_ref[pl.ds(r0, far_tile), :])
        return m_new, l, acc

    m, l, acc = lax.fori_loop(0, n_far, far_body, (m, l, acc))
    o_s = acc * (1.0 / l)

    w0 = pl.multiple_of(t0, QBLOCK)
    wlen = WINDOW + QBLOCK
    s = _dot_nt(qa_pad, kw_ref[pl.ds(w0, wlen), :]) + tw_ref[...]
    m = jnp.max(s, axis=-1, keepdims=True)
    p = jnp.exp(s - m)
    l = jnp.sum(p, axis=-1, keepdims=True)
    o_w = _dot(p.astype(BF16), vw_ref[pl.ds(w0, wlen), :]) * (1.0 / l)

    gt = gate_ref[...]
    outs = []
    for r in range(4):
        rs = slice(r * QBLOCK, (r + 1) * QBLOCK)
        outs.append(gt[:, 3 * r:3 * r + 1] * o_c[rs] + gt[:, 3 * r + 1:3 * r + 2] * o_s[rs]
                    + gt[:, 3 * r + 2:3 * r + 3] * o_w[rs])
    o_ref[...] = jnp.concatenate(outs, axis=1).astype(BF16)


def _nsa_prompt(q, gates_g, kc, vc, ov, ks, vs, kw, vw, tab_c, tab_s, tab_w):
    b, t, _ = q.shape
    nqb = t // QBLOCK
    kvspec = lambda a: pl.BlockSpec((None, None) + a.shape[2:], lambda bi, g, i: (bi, g, 0, 0))
    tabspec = lambda a: pl.BlockSpec((None,) + a.shape[1:], lambda bi, g, i: (g, 0, 0))
    return pl.pallas_call(
        _nsa_prompt_kernel,
        grid=(b, 2, nqb),
        in_specs=[pl.BlockSpec((None, QBLOCK, 512), lambda bi, g, i: (bi, i, g)),
                  pl.BlockSpec((None, None, QBLOCK, LANES), lambda bi, g, i: (bi, g, i, 0)),
                  kvspec(kc), kvspec(vc), pl.BlockSpec(ov.shape, lambda bi, g, i: (0, 0)),
                  kvspec(ks), kvspec(vs), kvspec(kw), kvspec(vw),
                  tabspec(tab_c), tabspec(tab_s), tabspec(tab_w)],
        out_specs=pl.BlockSpec((None, QBLOCK, 512), lambda bi, g, i: (bi, i, g)),
        out_shape=jax.ShapeDtypeStruct((b, t, 1024), BF16),
        compiler_params=_cparams(("arbitrary", "arbitrary", "arbitrary")),
        name="nsa_prompt",
    )(q, gates_g, kc, vc, ov, ks, vs, kw, vw, tab_c, tab_s, tab_w)


def _decode_attend(qf, g0rows, gk_row, k_tiles, v_tiles, bias, key_mask, new_row, bias_new):
    qg = qf * gk_row
    zero = jnp.zeros_like(qg)
    qbd = jnp.concatenate([jnp.where(g0rows, qg, zero), jnp.where(g0rows, zero, qg)], axis=1).astype(BF16)
    one = jnp.ones_like(qg)
    ones_bd = jnp.concatenate([jnp.where(g0rows, one, zero), jnp.where(g0rows, zero, one)], axis=1).astype(BF16)
    s_parts, q_parts = [], []
    for kt in k_tiles:
        k = kt()
        s_parts.append(_dot_nt(qbd, k.astype(BF16)))
        q_parts.append(_dot_nt(ones_bd, (k * k).astype(BF16)))
    s = jnp.concatenate(s_parts, axis=1)
    ssq = jnp.concatenate(q_parts, axis=1)
    s = s * lax.rsqrt(ssq * (1.0 / HEAD_DIM) + EPS) + bias
    if key_mask is not None:
        s = jnp.where(key_mask > 0.5, s, NEG)
    bc = lambda c: jnp.broadcast_to(new_row[c:c + 1, :], qf.shape)
    k_new = jnp.where(g0rows, bc(0), bc(1))
    v_new = jnp.where(g0rows, bc(2), bc(3))
    s_new = jnp.sum(qg * _rms(k_new), axis=-1, keepdims=True) + bias_new
    m = jnp.maximum(jnp.max(s, axis=-1, keepdims=True), s_new)
    p = jnp.exp(s - m)
    p_new = jnp.exp(s_new - m)
    l = jnp.sum(p, axis=-1, keepdims=True) + p_new
    o2 = jnp.zeros((qf.shape[0], 2 * LANES), F32)
    for i, vt in enumerate(v_tiles):
        o2 = o2 + _dot(p[:, i * LANES:(i + 1) * LANES].astype(BF16), vt().astype(BF16))
    o = jnp.where(g0rows, o2[:, :LANES], o2[:, LANES:]) + p_new * v_new
    return o * (1.0 / l)


def _nsa_sample_kernel(pt_ref, cache_c_hbm, cache_s_hbm, *refs, n_pages, n_sel_blocks, group):
    per_b_in, consts, outs = refs[:5], refs[5:-5], refs[-5:-3]
    buf_c, buf_s, sem = refs[-3:]
    step = pl.program_id(0)
    slot = step % 2

    def page_copies(src_step, dst_slot, for_wait):
        copies = []
        for e in range(group):
            for p in range(n_pages):
                pg = 0 if for_wait else pt_ref[src_step * group + e, p]
                copies.append(pltpu.make_async_copy(cache_c_hbm.at[pg], buf_c.at[dst_slot, e, p], sem.at[dst_slot, 0]))
                copies.append(pltpu.make_async_copy(cache_s_hbm.at[pg], buf_s.at[dst_slot, e, p], sem.at[dst_slot, 1]))
        return copies

    @pl.when(step == 0)
    def _():
        for c in page_copies(0, 0, False):
            c.start()

    @pl.when(step + 1 < pl.num_programs(0))
    def _():
        for c in page_copies(step + 1, 1 - slot, False):
            c.start()

    for c in page_copies(step, slot, True):
        c.wait()

    chains = [_nsa_sample_one([buf_c.at[slot, bb, p] for p in range(n_pages)],
                              [buf_s.at[slot, bb, p] for p in range(n_pages)],
                              *[r.at[bb] for r in per_b_in], *consts, *[r.at[bb] for r in outs],
                              n_pages=n_pages, n_sel_blocks=n_sel_blocks) for bb in range(group)]
    while chains:
        chains = [c for c in chains if next(c, "done") != "done"]


def _nsa_sample_one(pages_c, pages_s, win_ref, q_ref, gate_ref, ksn_ref, kwn_ref, wk_ref, wv_ref, pos_ref, gk_ref,
                    bc_ref, bs_ref, bw_ref, bnew_ref, ov_ref, e_ref, perm_ref, o_ref, wout_ref, *,
                    n_pages, n_sel_blocks):
    qf = q_ref[...]
    nh = qf.shape[0]
    g0rows = lax.broadcasted_iota(jnp.int32, (nh, LANES), 0) < nh // 2
    lane = lax.broadcasted_iota(jnp.int32, (nh, LANES), 1)

    page = pages_c[0].shape[0] // 4
    nseg = n_pages * page // CMP_STRIDE
    perm = perm_ref[...]
    regrouped = []
    for pp in range(n_pages // 2):
        per_cp = []
        for cp in range(2):
            blk = jnp.concatenate(
                [jnp.concatenate([pages_c[2 * pp + i][pl.ds(2 * cp + c, page, stride=4), :] for c in range(2)], axis=1)
                 for i in range(2)], axis=0).astype(BF16)
            per_cp.append(_dot(perm, blk).astype(BF16))
        regrouped.append(per_cp)
    rows_per_tap = 2 * page // CMP_STRIDE
    yield

    def tap_rows(s, col):
        lo, hi = s * rows_per_tap, (s + 1) * rows_per_tap
        return jnp.concatenate([regrouped[pp][col // 2][lo:hi, (col % 2) * LANES:(col % 2 + 1) * LANES]
                                for pp in range(n_pages // 2)], axis=0)

    load_pair = lambda s2, col: jnp.concatenate([tap_rows(2 * s2, col), tap_rows(2 * s2 + 1, col)], axis=1)
    ck, cv = _compress_core(load_pair, nseg, wk_ref, wv_ref, pos_ref)
    yield
    kcn = jnp.concatenate([(_rms(ck[g]) * gk_ref[0:1, :]).astype(BF16) for g in range(2)], axis=0)
    s2 = _dot_nt(qf.astype(BF16), kcn)
    s = jnp.where(g0rows, s2[:, :nseg], s2[:, nseg:]) + bc_ref[...]
    m = jnp.max(s, axis=-1, keepdims=True)
    p = jnp.exp(s - m)
    linv = 1.0 / jnp.sum(p, axis=-1, keepdims=True)
    rowv = lax.broadcasted_iota(jnp.int32, (nseg, LANES), 0) < nseg - 1
    pb = p.astype(BF16)
    oc = [_dot(pb, jnp.where(rowv, cv[g], 0.0).astype(BF16)) for g in range(2)]
    o_c = jnp.where(g0rows, oc[0], oc[1]) * linv

    yield
    pn = p * linv
    s0 = jnp.sum(jnp.where(g0rows, pn, 0.0), axis=0, keepdims=True)
    s1 = jnp.sum(jnp.where(g0rows, 0.0, pn), axis=0, keepdims=True)
    psum = jnp.where(g0rows, jnp.broadcast_to(s0, pn.shape), jnp.broadcast_to(s1, pn.shape))
    imp = _dot_hilo(psum, ov_ref[...].astype(BF16))
    cur = n_sel_blocks - 1
    forced = (lane == 0) | (lane == cur) | (lane == cur - 1)
    val = jnp.where(lane >= n_sel_blocks, -2.0, jnp.where(forced, FORCE, imp))
    rank = jnp.zeros_like(val)
    for i in range(n_sel_blocks):
        ci = jnp.broadcast_to(val[:, i:i + 1], val.shape)
        beats = (ci > val) | ((ci == val) & (lane > i))
        rank = rank + jnp.where(beats, 1.0, 0.0)
    sel = jnp.where((rank < float(N_SEL)) & (lane < n_sel_blocks), 1.0, 0.0)
    key_mask = _dot(sel.astype(BF16), e_ref[...])

    yield
    def tiles(ref_list, col):
        def tile(rf, i):
            rows = lambda c: rf[pl.ds(4 * LANES * i + c, LANES, stride=4), :]
            return lambda: jnp.concatenate([rows(col), rows(col + 1)], axis=1)
        return [tile(rf, i) for rf in ref_list for i in range(rf.shape[0] // (4 * LANES))]

    b_new = bnew_ref[:, 0:1]
    o_s = _decode_attend(qf, g0rows, gk_ref[1:2, :], tiles(pages_s, 0), tiles(pages_s, 2),
                         bs_ref[...], key_mask, ksn_ref[...], b_new)
    yield
    o_w = _decode_attend(qf, g0rows, gk_ref[2:3, :], tiles([win_ref], 0), tiles([win_ref], 2),
                         bw_ref[...], None, kwn_ref[...], b_new)
    yield
    gt = gate_ref[...]
    o_ref[...] = gt[:, 0:1] * o_c + gt[:, 1:2] * o_s + gt[:, 2:3] * o_w

    keep = wout_ref.shape[0] - 4
    drop = win_ref.shape[0] - keep
    wout_ref[0:keep, :] = win_ref[drop:drop + keep, :]
    wout_ref[keep:keep + 4, :] = kwn_ref[...]


def _nsa_sample(page_table, cache_c, cache_s, win, q, gates, ks_new, kw_new, wk_pairs, wv_pairs, pos_rows, gk,
                bias_c, bias_s, bias_w, bias_new, ov, expand):
    nb, n_pages = page_table.shape
    page_rows = cache_c.shape[1]
    n_sel_blocks = -(-(n_pages * (page_rows // 4) + 1) // SEL_LEN)
    group = SAMPLE_GROUP if nb % SAMPLE_GROUP == 0 else 1
    kern = functools.partial(_nsa_sample_kernel, n_pages=n_pages, n_sel_blocks=n_sel_blocks, group=group)
    win_rows_out = 4 * min(WINDOW, n_pages * (page_rows // 4) + 1)
    page = page_rows // 4
    assert n_pages % 2 == 0 and page % CMP_STRIDE == 0
    segs = page // CMP_STRIDE
    i_, n_, s_ = np.meshgrid(np.arange(2), np.arange(segs), np.arange(CMP_STRIDE), indexing="ij")
    perm_np = np.zeros((2 * page, 2 * page), np.float32)
    perm_np[(s_ * 2 * segs + i_ * segs + n_).ravel(), (i_ * page + CMP_STRIDE * n_ + s_).ravel()] = 1.0
    perm = jnp.asarray(perm_np, BF16)
    hbm = pl.BlockSpec(memory_space=pl.ANY)
    full = lambda a: pl.BlockSpec(a.shape, lambda bi, pt: (0,) * a.ndim)
    per_b = lambda a: pl.BlockSpec((group,) + a.shape[1:], lambda bi, pt: (bi,) + (0,) * (a.ndim - 1))
    page_buf = pltpu.VMEM((2, group, n_pages, page_rows, LANES), F32)
    grid_spec = pltpu.PrefetchScalarGridSpec(
        num_scalar_prefetch=1,
        grid=(nb // group,),
        scratch_shapes=[page_buf, page_buf, pltpu.SemaphoreType.DMA((2, 2))],
        in_specs=([hbm, hbm]
                  + [per_b(win), per_b(q), per_b(gates), per_b(ks_new), per_b(kw_new),
                     full(wk_pairs), full(wv_pairs), full(pos_rows), full(gk),
                     full(bias_c), full(bias_s), full(bias_w), full(bias_new), full(ov), full(expand), full(perm)]),
        out_specs=[pl.BlockSpec((group,) + q.shape[1:], lambda bi, pt: (bi, 0, 0)),
                   pl.BlockSpec((group, win_rows_out, LANES), lambda bi, pt: (bi, 0, 0))],
    )
    return pl.pallas_call(
        kern,
        grid_spec=grid_spec,
        out_shape=(jax.ShapeDtypeStruct(q.shape, F32), jax.ShapeDtypeStruct((nb, win_rows_out, LANES), F32)),
        compiler_params=_cparams(("arbitrary",)),
        name="nsa_sample",
    )(page_table, cache_c, cache_s, win, q, gates, ks_new, kw_new,
      wk_pairs, wv_pairs, pos_rows, gk, bias_c, bias_s, bias_w, bias_new, ov, expand, perm)


def _merge_kernel(h_ref, oa_ref, ob_ref, wga_ref, wgb_ref, wg_ref, wn_ref, mix_ref, *cast_refs):
    h = h_ref[...]
    wga = wga_ref[...].astype(BF16)
    wgb = wgb_ref[...].astype(BF16)
    wg = wg_ref[...].astype(BF16)
    wn = wn_ref[...].astype(BF16)
    ga = _sigmoid(_dot_nt(h, wga))
    gb = _sigmoid(_dot_nt(h, wgb))
    mix_ref[...] = (ga * _dot(oa_ref[...], wg) + gb * _dot(ob_ref[...], wn)).astype(BF16)
    for ref, val in zip(cast_refs, (wga, wgb, wg, wn)):
        ref[...] = val


def _merge(h, oa, ob, wga, wgb, wg, wn, gate_rows=None):
    n, d = h.shape
    tm = 512 if n % 512 == 0 else n
    tn = 512
    emit = gate_rows is not None
    assert not emit or n == tm
    row = lambda w: pl.BlockSpec((tm, w), lambda i, j: (i, 0))
    col = lambda k: pl.BlockSpec((k, tn), lambda i, j: (0, j))
    colt = pl.BlockSpec((tn, d), lambda i, j: (j, 0))
    out_specs = [pl.BlockSpec((tm, tn), lambda i, j: (i, j))]
    out_shape = [jax.ShapeDtypeStruct((n, d), BF16)]
    gate_specs = [colt, colt]
    if emit:
        assert all(r % 8 == 0 for r in gate_rows)
        gate_specs = [pl.BlockSpec((pl.Element(tn), pl.Element(d)),
                                   lambda i, j, r=r: ((r // 8 + j * (tn // 8)) * 8, 0)) for r in gate_rows]
        out_specs += [colt, colt, col(wg.shape[0]), col(wn.shape[0])]
        out_shape += [jax.ShapeDtypeStruct((d, d), BF16), jax.ShapeDtypeStruct((d, d), BF16),
                      jax.ShapeDtypeStruct(wg.shape, BF16), jax.ShapeDtypeStruct(wn.shape, BF16)]
    return pl.pallas_call(
        _merge_kernel,
        grid=(n // tm, d // tn),
        in_specs=[row(d), row(oa.shape[1]), row(ob.shape[1])] + gate_specs + [col(wg.shape[0]), col(wn.shape[0])],
        out_specs=out_specs,
        out_shape=out_shape,
        compiler_params=_cparams(("arbitrary", "arbitrary")),
        name="merge",
    )(h, oa, ob, wga, wgb, wg, wn)


def _outproj_kernel(x_ref, mix_ref, wout_ref, gmlp_ref, x1_ref, hm_ref):
    x1 = x_ref[...] + _dot(mix_ref[...], wout_ref[...])
    x1_ref[...] = x1
    hm_ref[...] = (_rms(x1) * gmlp_ref[...]).astype(BF16)


def _outproj(x, mix, wout, gmlp):
    n, d = x.shape
    tm = 512 if n % 512 == 0 else n
    row = pl.BlockSpec((tm, d), lambda i: (i, 0))
    return pl.pallas_call(
        _outproj_kernel,
        grid=(n // tm,),
        in_specs=[row, row, pl.BlockSpec(wout.shape, lambda i: (0, 0)), pl.BlockSpec(gmlp.shape, lambda i: (0, 0))],
        out_specs=[row, row],
        out_shape=(jax.ShapeDtypeStruct((n, d), F32), jax.ShapeDtypeStruct((n, d), BF16)),
        compiler_params=_cparams(("arbitrary",)),
        name="outproj",
    )(x, mix, wout, gmlp)


def _ffn_kernel(hm_ref, x1_ref, wup_ref, wdown_ref, y_ref, *cast_refs):
    f = pl.program_id(1)

    @pl.when(f == 0)
    def _():
        y_ref[...] = x1_ref[...]

    wup = wup_ref[...].astype(BF16)
    wdown = wdown_ref[...].astype(BF16)
    hid = jnp.maximum(_dot(hm_ref[...], wup), 0.0)
    y_ref[...] += _dot((hid * hid).astype(BF16), wdown)
    if cast_refs:
        cast_refs[0][...] = wup
        cast_refs[1][...] = wdown


def _ffn(hm, x1, wup, wdown):
    n, d = hm.shape
    dff = wup.shape[1]
    tm = 512 if n % 512 == 0 else n
    emit = wup.dtype != BF16
    assert not emit or n == tm
    tf = 512 if emit else 1024
    row = pl.BlockSpec((tm, d), lambda i, f: (i, 0))
    up_spec = pl.BlockSpec((d, tf), lambda i, f: (0, f))
    down_spec = pl.BlockSpec((tf, d), lambda i, f: (f, 0))
    out_specs = [row]
    out_shape = [jax.ShapeDtypeStruct((n, d), F32)]
    if emit:
        out_specs += [up_spec, down_spec]
        out_shape += [jax.ShapeDtypeStruct(wup.shape, BF16), jax.ShapeDtypeStruct(wdown.shape, BF16)]
    return pl.pallas_call(
        _ffn_kernel,
        grid=(n // tm, dff // tf),
        in_specs=[row, row, up_spec, down_spec],
        out_specs=out_specs,
        out_shape=out_shape,
        compiler_params=_cparams(("arbitrary", "arbitrary")),
        name="ffn",
    )(hm, x1, wup, wdown)


def _bucket(rel, valid):
    n = np.maximum(rel, 0)
    max_exact = N_BUCKETS // 2
    nf = np.maximum(n, 1).astype(np.float32)
    large = max_exact + (np.log(nf / np.float32(max_exact)) / np.float32(math.log(MAX_DIST / max_exact))
                         * np.float32(N_BUCKETS - max_exact)).astype(np.int32)
    large = np.minimum(large, N_BUCKETS - 1)
    return np.where(valid, np.where(n < max_exact, n, large), -1).astype(np.int32)


def _bias_tables_kernel(rb_ref, *refs, shifts):
    n = len(shifts)
    nbk, nh = rb_ref.shape
    for b_ref, o_ref, shift in zip(refs[:n], refs[n:], shifts):
        b = b_ref[...]
        rows = []
        for h in range(nh):
            sh = rb_ref[nbk - 1, h] if shift else 0.0
            acc = jnp.full(b.shape, NEG, F32)
            for k in range(nbk):
                acc = jnp.where(b == k, rb_ref[k, h] - sh, acc)
            if len(o_ref.shape) == 3:
                o_ref[h] = acc
            else:
                rows.append(acc)
        if rows:
            o_ref[...] = jnp.concatenate(rows, axis=0)


def _bias_tables(rel_bias, buckets, shifts):
    nh = rel_bias.shape[1]
    shapes = [jax.ShapeDtypeStruct((nh,) + (b.shape if b.shape[0] > 1 else b.shape[1:]), F32) for b in buckets]
    vm = pl.BlockSpec(memory_space=pltpu.VMEM)
    return pl.pallas_call(
        functools.partial(_bias_tables_kernel, shifts=tuple(shifts)),
        in_specs=[pl.BlockSpec(memory_space=pltpu.SMEM)] + [vm] * len(buckets),
        out_specs=[vm] * len(buckets),
        out_shape=shapes,
        name="bias_tables",
    )(rel_bias, *[jnp.asarray(b) for b in buckets])


def _overlap(nc, ns):
    i = np.arange(nc)[:, None] * CMP_STRIDE
    j = np.arange(ns)[None, :] * SEL_LEN
    return ((i < j + SEL_LEN) & (i + CMP_LEN > j)).astype(np.float32)


def _compress_weights(w, pos):
    s = np.arange(0, CMP_STRIDE, 2)
    top = jnp.concatenate([w[s], w[CMP_STRIDE + s]], axis=2)
    bot = jnp.concatenate([w[s + 1], w[CMP_STRIDE + s + 1]], axis=2)
    tiles = jnp.concatenate([top, bot], axis=1).astype(BF16)
    row_a = jnp.concatenate([pos[s], pos[s + 1]], axis=1)
    row_b = jnp.concatenate([pos[CMP_STRIDE + s], pos[CMP_STRIDE + s + 1]], axis=1)
    rows = jnp.zeros((len(s), 16, 2 * HEAD_DIM), F32).at[:, 0].set(row_a).at[:, 1].set(row_b)
    return tiles, rows.astype(BF16)


def _dense_tail(x, h, oa, ob, wga, wgb, wg, wn, wout, gmlp, wup, wdown, gate_rows=None):
    mix, *cast_m = _merge(h, oa, ob, wga, wgb, wg, wn, gate_rows)
    x1, hm = _outproj(x, mix, wout, gmlp)
    y, *cast_f = _ffn(hm, x1, wup, wdown)
    return y, (cast_m or [wga, wgb, wg, wn]) + (cast_f or [wup, wdown])


def kernel(x_prompt, x_sample, cache_cmp_kv, cache_sel_kv, state_win_kv, page_table, rel_bias, g_mix_norm, w_in,
           g_sgu, w_sgu, b_sgu, g_q, g_k, pos_cmp_k, w_cmp_k, pos_cmp_v, w_cmp_v, w_proj_gmlp, w_proj_nsa, w_out,
           g_mlp_norm, w_up, w_down):
    depth = g_mix_norm.shape[0]
    assert depth == 1
    l = 0
    bsz, seq, d = x_prompt.shape
    nb = x_sample.shape[0]
    assert x_sample.shape[1] == 1 and seq % KPAD == 0
    d_gm = g_sgu.shape[1]
    n_heads = rel_bias.shape[1]
    d_nsa = n_heads * HEAD_DIM
    kvw_cols = 2 * (n_heads // 4) * HEAD_DIM
    n_gate = 3 * n_heads
    c_q = 2 * d_gm
    c_kv = c_q + d_nsa
    c_gate = c_kv + 3 * kvw_cols
    c_ga = c_gate + n_gate
    c_gb = c_ga + d

    w = jnp.swapaxes(w_in[l], 0, 1)
    n_col_tiles = -(-c_ga // 1024)
    wout = w_out[l].astype(BF16)
    gmix = g_mix_norm[l][None]
    gsgu = g_sgu[l][None]
    gq = g_q[l][None]
    gk = g_k[l]
    gmlp = g_mlp_norm[l][None]
    wk_pairs, posk_rows = _compress_weights(w_cmp_k[l], pos_cmp_k[l])
    wv_pairs, posv_rows = _compress_weights(w_cmp_v[l], pos_cmp_v[l])
    pos_rows = jnp.stack([posk_rows, posv_rows])
    n_groups = w_sgu.shape[1]
    b_exp = jnp.repeat(b_sgu[l].T, d_gm // n_groups, axis=1)
    w00 = jnp.repeat(w_sgu[l][:, 0, 0], d_gm // n_groups)[None]
    b00 = jnp.repeat(b_sgu[l][:, 0], d_gm // n_groups)[None]

    n_pool, page = cache_cmp_kv.shape[1], cache_cmp_kv.shape[2]
    n_pages = page_table.shape[1]
    past = n_pages * page
    nwin = state_win_kv.shape[2]
    n_cmp_s = (past + 1 - CMP_LEN) // CMP_STRIDE + 1
    n_sel_s = -(-(past + 1) // SEL_LEN)
    nseg_s = past // CMP_STRIDE
    qi = np.arange(QBLOCK)[:, None]
    rel_s = qi + QBLOCK - np.arange(2 * QBLOCK)[None, :]
    rel_w = qi + WINDOW - np.arange(WINDOW + QBLOCK)[None, :]
    rel_c = qi + (KC_FRONT * CMP_STRIDE - CMP_LEN + 1) - CMP_STRIDE * np.arange(LANES)[None, :]
    srel_c = (past - (np.arange(nseg_s) * CMP_STRIDE + CMP_LEN - 1))[None]
    srel_s = (past - np.arange(past))[None]
    srel_w = (nwin - np.arange(nwin))[None]
    buckets = [
        _bucket(rel_c, rel_c >= 0), _bucket(rel_s, rel_s >= 0), _bucket(rel_w, (rel_w >= 0) & (rel_w < WINDOW)),
        _bucket(srel_c, (srel_c >= 0) & (np.arange(nseg_s)[None] < n_cmp_s)), _bucket(srel_s, srel_s >= 0),
        _bucket(srel_w, srel_w < WINDOW), np.zeros((1, LANES), np.int32)]
    tab_c, tab_s, tab_w, bias_c, bias_s, bias_w, bias_new = _bias_tables(
        rel_bias, buckets, [True, True, False, False, False, False, False])
    tab_c, tab_s, tab_w = [t.reshape(2, (n_heads // 2) * QBLOCK, t.shape[-1]) for t in (tab_c, tab_s, tab_w)]

    xs = x_sample.reshape(nb, d)
    h_s, v_s, oa_s, q_s, kvc_s, kvs_s, kvw_s, gates_s, w_cat = _inproj(
        xs, gmix, w, gsgu, gq, w00, b00, chunked=False, n_col_tiles=n_col_tiles)
    ov_s = np.zeros((nseg_s, LANES), np.float32)
    ov_s[:n_cmp_s, :n_sel_s] = _overlap(n_cmp_s, n_sel_s)
    expand = jnp.asarray(np.arange(LANES)[:, None] == (np.arange(past)[None, :] // SEL_LEN), BF16)
    gates_h = jnp.pad(gates_s[:, :n_gate].reshape(nb, n_heads, 3), ((0, 0), (0, 0), (0, LANES - 3)))
    lin = lambda a: a[l].reshape(a.shape[1], -1, HEAD_DIM)
    win_lin = lin(state_win_kv)
    ob_s, win_new = _nsa_sample(
        page_table, lin(cache_cmp_kv), lin(cache_sel_kv), win_lin,
        q_s.astype(F32).reshape(nb, n_heads, HEAD_DIM), gates_h,
        kvs_s.reshape(nb, 4, HEAD_DIM), kvw_s.reshape(nb, 4, HEAD_DIM), wk_pairs, wv_pairs, pos_rows, gk,
        bias_c, bias_s, bias_w, bias_new, jnp.asarray(ov_s), expand)
    y_s, (wga, wgb, wg, wn, wup, wdown) = _dense_tail(
        xs, h_s, oa_s, ob_s.reshape(nb, d_nsa).astype(BF16), w, w, w_proj_gmlp[l], w_proj_nsa[l], wout, gmlp,
        w_up[l], w_down[l], gate_rows=(c_ga, c_gb))

    xp = x_prompt.reshape(bsz * seq, d)
    h_p, v_p, oa_p, q_p, kvc_p, kvs_p, kvw_p, gates_p = _inproj(
        xp, gmix, w_cat, gsgu, gq, w_sgu[l], b_exp, chunked=True, n_col_tiles=n_col_tiles)
    kvs3 = kvs_p.reshape(bsz, 4 * seq, HEAD_DIM)
    kvw3 = kvw_p.reshape(bsz, 4 * seq, HEAD_DIM)
    kvc3 = kvc_p.reshape(bsz, 4 * seq, HEAD_DIM)
    ks, vs, kw, vw = _kvprep(kvs3, kvw3, gk)
    kc, vc = _compress_prompt(kvc3, wk_pairs, wv_pairs, pos_rows, gk)
    n_cmp = (seq - CMP_LEN) // CMP_STRIDE + 1
    n_selb = seq // SEL_LEN
    ov_p = np.zeros((KC_ROWS, LANES), np.float32)
    ov_p[KC_FRONT:KC_FRONT + n_cmp, :n_selb] = _overlap(n_cmp, n_selb)
    ov_p = jnp.asarray(ov_p)
    gates_g = gates_p[:, :n_gate].reshape(bsz, seq, 2, n_gate // 2).transpose(0, 2, 1, 3)
    gates_g = jnp.pad(gates_g, ((0, 0), (0, 0), (0, 0), (0, LANES - n_gate // 2)))
    ob_p = _nsa_prompt(q_p.reshape(bsz, seq, d_nsa), gates_g, kc, vc, ov_p, ks, vs, kw, vw, tab_c, tab_s, tab_w)
    y_p, _ = _dense_tail(xp, h_p, oa_p, ob_p.reshape(bsz * seq, d_nsa), wga, wgb, wg, wn, wout, gmlp, wup, wdown)

    n_kv = n_heads // 4
    kv6 = lambda a, b_, t_: a.reshape(1, b_, t_, 2, n_kv, HEAD_DIM)
    nw_p = min(WINDOW, seq)
    last = ((seq - 1) // CHUNK) * CHUNK
    nw_s = min(WINDOW, past + 1)
    return (y_p.reshape(bsz, seq, d), y_s.reshape(nb, 1, d),
            kv6(kvc3, bsz, seq), kv6(kvs3, bsz, seq), kv6(kvw3[:, 4 * (seq - nw_p):], bsz, nw_p),
            v_p.reshape(bsz, seq, d_gm)[:, last:][None],
            kv6(kvc_s, nb, 1), kv6(kvs_s, nb, 1), kv6(win_new, nb, nw_s),
            v_s.reshape(1, nb, 1, d_gm))
```

```python
import functools
import math

import numpy as np
import jax
import jax.numpy as jnp
from jax import lax
from jax.experimental import pallas as pl
from jax.experimental.pallas import tpu as pltpu

F32 = jnp.float32
BF16 = jnp.bfloat16

HEAD_DIM = 128
CHUNK = 128
CMP_LEN = 32
CMP_STRIDE = 16
SEL_LEN = 64
N_SEL = 16
WINDOW = 512
N_BUCKETS = 32
MAX_DIST = 128
QBLOCK = 128
EPS = 1e-6
NEG = -1e30
HALF_NEG = -5e29
FORCE = 1e6

LANES = 128
PAD_LANE = 64
KPAD = 512
KC_FRONT = 16
KC_ROWS = 376
SAMPLE_GROUP = 2
VMEM_LIMIT = 56 * 1024 * 1024


def _cparams(sem):
    return pltpu.CompilerParams(dimension_semantics=sem, vmem_limit_bytes=VMEM_LIMIT)


def _dot(a, b):
    return jnp.dot(a, b, preferred_element_type=F32)


def _dot_nt(a, b):
    return lax.dot_general(a, b, (((1,), (1,)), ((), ())), preferred_element_type=F32)


def _dot_hilo(a, b_bf16):
    hi = a.astype(BF16)
    lo = (a - hi.astype(F32)).astype(BF16)
    return _dot(hi, b_bf16) + _dot(lo, b_bf16)


def _rms(x):
    return x * lax.rsqrt(jnp.mean(x * x, axis=-1, keepdims=True) + EPS)


def _gelu(x):
    c = math.sqrt(2.0 / math.pi)
    return 0.5 * x * (1.0 + jnp.tanh(c * (x + 0.044715 * (x * x * x))))


def _sigmoid(x):
    return 1.0 / (1.0 + jnp.exp(-x))


def _inproj_kernel(x_ref, gmix_ref, w_ref, gsgu_ref, gq_ref, wsg_ref, bsg_ref,
                   h_ref, v_ref, oa_ref, q_ref, kvc_ref, kvs_ref, kvw_ref, gate_ref, *rest, chunked, tm):
    h_s, u_s = rest[-2:]
    j = pl.program_id(1)

    @pl.when(j == 0)
    def _():
        hb = (_rms(x_ref[...]) * gmix_ref[...]).astype(BF16)
        h_s[...] = hb
        h_ref[...] = hb

    if len(rest) == 3:
        w = w_ref[...].astype(BF16)
        rest[0][...] = w
        z = _dot_nt(h_s[...], w)
    else:
        z = _dot_nt(h_s[...], w_ref[...])

    @pl.when(j == 0)
    def _():
        u_s[...] = _gelu(z)

    @pl.when(j == 1)
    def _():
        v = _rms(_gelu(z)) * gsgu_ref[...]
        v_ref[...] = v
        if chunked:
            row = lax.broadcasted_iota(jnp.int32, (CHUNK, CHUNK), 0)
            col = lax.broadcasted_iota(jnp.int32, (CHUNK, CHUNK), 1)
            n_groups = v.shape[1] // LANES
            for g in range(n_groups):
                wm = jnp.where(row >= col, wsg_ref[g], 0.0).astype(BF16)
                cs = slice(g * LANES, (g + 1) * LANES)
                for c in range(tm // CHUNK):
                    rs = slice(c * CHUNK, (c + 1) * CHUNK)
                    s = _dot(wm, v[rs, cs].astype(BF16)) + bsg_ref[:, cs]
                    oa_ref[rs, cs] = (u_s[rs, cs] * s).astype(BF16)
        else:
            oa_ref[...] = (u_s[...] * (v * wsg_ref[...] + bsg_ref[...])).astype(BF16)

    @pl.when(j == 2)
    def _():
        scale = HEAD_DIM ** -0.5
        for hd in range(z.shape[1] // HEAD_DIM):
            cs = slice(hd * HEAD_DIM, (hd + 1) * HEAD_DIM)
            q_ref[:, cs] = (_rms(z[:, cs]) * gq_ref[...] * scale).astype(BF16)

    def store_kv(ref, zz):
        for c in range(4):
            ref[pl.ds(c, tm, stride=4), :] = zz[:, c * LANES:(c + 1) * LANES]

    @pl.when(j == 3)
    def _():
        store_kv(kvc_ref, z[:, :512])
        store_kv(kvs_ref, z[:, 512:])

    @pl.when(j == 4)
    def _():
        store_kv(kvw_ref, z[:, :512])
        gate_ref[...] = _sigmoid(z[:, 512:640])


def _inproj(x, gmix, w_cat, gsgu, gq, wsg, bsg, *, chunked, n_col_tiles):
    n, d = x.shape
    tm = 512 if n % 512 == 0 else n
    tn = 1024
    emit = w_cat.dtype != BF16
    assert not emit or n == tm
    kern = functools.partial(_inproj_kernel, chunked=chunked, tm=tm)
    full = lambda a: pl.BlockSpec(a.shape, lambda i, j: (0,) * a.ndim)
    row = lambda w: pl.BlockSpec((tm, w), lambda i, j: (i, 0))
    kvrow = pl.BlockSpec((4 * tm, LANES), lambda i, j: (i, 0))
    out_shapes = (
        jax.ShapeDtypeStruct((n, d), BF16),
        jax.ShapeDtypeStruct((n, 1024), F32),
        jax.ShapeDtypeStruct((n, 1024), BF16),
        jax.ShapeDtypeStruct((n, 1024), BF16),
        jax.ShapeDtypeStruct((4 * n, LANES), F32),
        jax.ShapeDtypeStruct((4 * n, LANES), F32),
        jax.ShapeDtypeStruct((4 * n, LANES), F32),
        jax.ShapeDtypeStruct((n, LANES), F32),
    )
    wspec = pl.BlockSpec((tn, d), lambda i, j: (j, 0))
    out_specs = [row(d), row(1024), row(1024), row(1024), kvrow, kvrow, kvrow, row(LANES)]
    if emit:
        out_shapes += (jax.ShapeDtypeStruct((n_col_tiles * tn, d), BF16),)
        out_specs.append(wspec)
    return pl.pallas_call(
        kern,
        grid=(n // tm, n_col_tiles),
        in_specs=[row(d), full(gmix), wspec, full(gsgu), full(gq), full(wsg), full(bsg)],
        out_specs=out_specs,
        out_shape=out_shapes,
        scratch_shapes=[pltpu.VMEM((tm, d), BF16), pltpu.VMEM((tm, 1024), F32)],
        compiler_params=_cparams(("arbitrary", "arbitrary")),
        name="inproj",
    )(x, gmix, w_cat, gsgu, gq, wsg, bsg)


def _kvprep_kernel(kvs_ref, kvw_ref, gk_ref, ks_ref, vs_ref, kw_ref, vw_ref):
    i = pl.program_id(1)
    rows = kvs_ref.shape[0] // 4
    col = lambda ref, c: ref[pl.ds(c, rows, stride=4), :]
    lane = lax.broadcasted_iota(jnp.int32, (rows, LANES), 1)
    row = lax.broadcasted_iota(jnp.int32, (rows, LANES), 0)

    @pl.when(i == 0)
    def _():
        aux = jnp.where(lane == PAD_LANE, 1.0, 0.0).astype(BF16)
        zk = jnp.zeros((rows, LANES), BF16)
        for g in range(2):
            ks_ref[g] = jnp.concatenate([zk, aux], axis=1)
            kw_ref[g] = jnp.concatenate([zk, aux], axis=1)
            vs_ref[g] = zk
            vw_ref[g] = zk

    @pl.when(i > 0)
    def _():
        blk = ((i - 1) * rows + row) // SEL_LEN
        onehot = jnp.where(lane == blk, 1.0, 0.0).astype(BF16)
        zaux = jnp.zeros((rows, LANES), BF16)
        for g in range(2):
            ks = (_rms(col(kvs_ref, g)) * gk_ref[1:2, :]).astype(BF16)
            kw = (_rms(col(kvw_ref, g)) * gk_ref[2:3, :]).astype(BF16)
            ks_ref[g] = jnp.concatenate([ks, onehot], axis=1)
            kw_ref[g] = jnp.concatenate([kw, zaux], axis=1)
            vs_ref[g] = col(kvs_ref, 2 + g).astype(BF16)
            vw_ref[g] = col(kvw_ref, 2 + g).astype(BF16)


def _kvprep(kvs, kvw, gk):
    b, t4, _ = kvs.shape
    t = t4 // 4
    rows = KPAD
    nblk = t // rows
    in_map = lambda bi, i: (bi, jnp.maximum(i - 1, 0), 0)
    out_map = lambda bi, i: (bi, 0, i, 0)
    kshape = jax.ShapeDtypeStruct((b, 2, KPAD + t, 2 * LANES), BF16)
    vshape = jax.ShapeDtypeStruct((b, 2, KPAD + t, LANES), BF16)
    return pl.pallas_call(
        _kvprep_kernel,
        grid=(b, nblk + 1),
        in_specs=[pl.BlockSpec((None, 4 * rows, LANES), in_map), pl.BlockSpec((None, 4 * rows, LANES), in_map),
                  pl.BlockSpec(gk.shape, lambda bi, i: (0, 0))],
        out_specs=[pl.BlockSpec((None, 2, rows, 2 * LANES), out_map), pl.BlockSpec((None, 2, rows, LANES), out_map),
                   pl.BlockSpec((None, 2, rows, 2 * LANES), out_map), pl.BlockSpec((None, 2, rows, LANES), out_map)],
        out_shape=(kshape, vshape, kshape, vshape),
        compiler_params=_cparams(("arbitrary", "arbitrary")),
        name="kvprep",
    )(kvs, kvw, gk)


def _compress_core(load_pair, nseg, wk_ref, wv_ref, pos_ref):
    w_refs = (wk_ref, wv_ref)
    accs = [jnp.zeros((2 * nseg + 16, 2 * LANES), F32) for _ in range(2)]
    for s2 in range(CMP_STRIDE // 2):
        for kv in range(2):
            parts = [load_pair(s2, 2 * kv + g) for g in range(2)]
            parts.append(pos_ref[kv, s2])
            lhs = jnp.concatenate(parts, axis=0)
            accs[kv] = accs[kv] + _dot(lhs, w_refs[kv][s2])
    outs = []
    for kv in range(2):
        y = accs[kv]
        post = y[2 * nseg:2 * nseg + 1, :LANES] + y[2 * nseg + 1:2 * nseg + 2, LANES:]
        per_g = []
        for g in range(2):
            y0 = y[g * nseg:(g + 1) * nseg, :LANES]
            y1 = y[g * nseg:(g + 1) * nseg, LANES:]
            per_g.append(y0 + pltpu.roll(y1, nseg - 1, 0) + post)
        outs.append(per_g)
    return outs


def _compress_prompt_kernel(x_ref, wk_ref, wv_ref, pos_ref, gk_ref, kc_ref, vc_ref):
    nseg = x_ref.shape[0] // (4 * CMP_STRIDE)
    load_x = lambda s, col: x_ref[pl.ds(4 * s + col, nseg, stride=4 * CMP_STRIDE), :]
    load_pair = lambda s2, col: jnp.concatenate([load_x(2 * s2, col), load_x(2 * s2 + 1, col)], axis=1).astype(BF16)
    ck, cv = _compress_core(load_pair, nseg, wk_ref, wv_ref, pos_ref)
    row = lax.broadcasted_iota(jnp.int32, (nseg, LANES), 0)
    lane = lax.broadcasted_iota(jnp.int32, (nseg, LANES), 1)
    valid = row < nseg - 1
    aux = jnp.where(valid, jnp.where(lane == row // 8, 1.0, 0.0), jnp.where(lane == PAD_LANE, 1.0, 0.0))
    back = KC_ROWS - KC_FRONT - nseg
    lane_f = lax.broadcasted_iota(jnp.int32, (KC_FRONT, LANES), 1)
    lane_b = lax.broadcasted_iota(jnp.int32, (back, LANES), 1)
    pad_f = jnp.concatenate([jnp.zeros((KC_FRONT, LANES), F32), jnp.where(lane_f == PAD_LANE, 1.0, 0.0)], axis=1)
    pad_b = jnp.concatenate([jnp.zeros((back, LANES), F32), jnp.where(lane_b == PAD_LANE, 1.0, 0.0)], axis=1)
    for g in range(2):
        kn = jnp.where(valid, _rms(ck[g]) * gk_ref[0:1, :], 0.0)
        kc_ref[g, 0:KC_FRONT, :] = pad_f
        kc_ref[g, KC_FRONT:KC_FRONT + nseg, :] = jnp.concatenate([kn, aux], axis=1)
        kc_ref[g, KC_FRONT + nseg:KC_ROWS, :] = pad_b
        vc_ref[g, 0:KC_FRONT, :] = jnp.zeros((KC_FRONT, LANES), F32)
        vc_ref[g, KC_FRONT:KC_FRONT + nseg, :] = jnp.where(valid, cv[g], 0.0)
        vc_ref[g, KC_FRONT + nseg:KC_ROWS, :] = jnp.zeros((back, LANES), F32)


def _compress_prompt(kvc, wk_pairs, wv_pairs, pos_rows, gk):
    b, t4, _ = kvc.shape
    full = lambda a: pl.BlockSpec(a.shape, lambda bi: (0,) * a.ndim)
    return pl.pallas_call(
        _compress_prompt_kernel,
        grid=(b,),
        in_specs=[pl.BlockSpec((None, t4, LANES), lambda bi: (bi, 0, 0)),
                  full(wk_pairs), full(wv_pairs), full(pos_rows), full(gk)],
        out_specs=[pl.BlockSpec((None, 2, KC_ROWS, 2 * LANES), lambda bi: (bi, 0, 0, 0)),
                   pl.BlockSpec((None, 2, KC_ROWS, LANES), lambda bi: (bi, 0, 0, 0))],
        out_shape=(jax.ShapeDtypeStruct((b, 2, KC_ROWS, 2 * LANES), F32),
                   jax.ShapeDtypeStruct((b, 2, KC_ROWS, LANES), F32)),
        compiler_params=_cparams(("arbitrary",)),
        name="compress_prompt",
    )(kvc, wk_pairs, wv_pairs, pos_rows, gk)


def _nsa_prompt_kernel(q_ref, gate_ref, kc_ref, vc_ref, ov_ref, ks_ref, vs_ref, kw_ref, vw_ref,
                       tc_ref, ts_ref, tw_ref, o_ref, m_s, l_s, acc_s, sa_s, sb_s):
    qb = pl.program_id(2)
    t0 = qb * QBLOCK
    rq = 4 * QBLOCK
    q = q_ref[...]
    q4 = jnp.concatenate([q[:, r * LANES:(r + 1) * LANES] for r in range(4)], axis=0)
    lane = lax.broadcasted_iota(jnp.int32, (rq, LANES), 1)
    is_pad_lane = lane == PAD_LANE

    def q_aug(mb):
        return jnp.concatenate([q4, mb.astype(BF16)], axis=1)

    qa_pad = q_aug(jnp.where(is_pad_lane, NEG, 0.0))

    far_mask = ((lane < 32) & (lane >= qb - 2)) | is_pad_lane
    qa_cfar = q_aug(jnp.where(far_mask, NEG, 0.0))
    ncmp = 256
    near0 = pl.multiple_of(qb * 8, 8)
    k_far = kc_ref[KC_FRONT:KC_FRONT + ncmp, :].astype(BF16)
    k_near = kc_ref[pl.ds(near0, LANES), :].astype(BF16)
    s_far = _dot_nt(qa_cfar, k_far)
    s_near = _dot_nt(qa_pad, k_near) + tc_ref[...]
    m = jnp.maximum(jnp.max(s_far, axis=-1, keepdims=True), jnp.max(s_near, axis=-1, keepdims=True))
    p_far = jnp.where(s_far > HALF_NEG, jnp.exp(s_far - m), 0.0)
    p_near = jnp.where(s_near > HALF_NEG, jnp.exp(s_near - m), 0.0)
    l = jnp.sum(p_far, axis=-1, keepdims=True) + jnp.sum(p_near, axis=-1, keepdims=True)
    linv = 1.0 / jnp.where(l > 0.0, l, 1.0)
    v_far = vc_ref[KC_FRONT:KC_FRONT + ncmp, :].astype(BF16)
    v_near = vc_ref[pl.ds(near0, LANES), :].astype(BF16)
    o_c = (_dot(p_far.astype(BF16), v_far) + _dot(p_near.astype(BF16), v_near)) * linv

    pn_far = p_far * linv
    pn_near = p_near * linv
    ps_far = sum(pn_far[r * QBLOCK:(r + 1) * QBLOCK] for r in range(4))
    ps_near = sum(pn_near[r * QBLOCK:(r + 1) * QBLOCK] for r in range(4))
    ov_far = ov_ref[KC_FRONT:KC_FRONT + ncmp, :].astype(BF16)
    ov_near = ov_ref[pl.ds(near0, LANES), :].astype(BF16)
    imp = _dot_hilo(ps_far, ov_far) + _dot_hilo(ps_near, ov_near)
    nblk = 64
    imp_t = imp.T[:nblk]
    blk = lax.broadcasted_iota(jnp.int32, (nblk, QBLOCK), 0)
    qpos = t0 + lax.broadcasted_iota(jnp.int32, (nblk, QBLOCK), 1)
    cur = qpos // SEL_LEN
    forced = (blk == 0) | (blk == cur) | (blk == cur - 1)
    eligible = blk * SEL_LEN <= qpos
    val = jnp.where(forced, FORCE, jnp.where(eligible, imp_t, -1.0))
    rank = jnp.zeros((nblk, QBLOCK), F32)
    for i in range(nblk):
        ri = val[i:i + 1, :]
        beats = (ri > val) | ((ri == val) & (blk > i))
        rank = rank + jnp.where(beats, 1.0, 0.0)
    mb_t = jnp.where(rank < float(N_SEL), 0.0, NEG)
    row2 = lax.broadcasted_iota(jnp.int32, (LANES - nblk, QBLOCK), 0)
    mb_t = jnp.concatenate([mb_t, jnp.where(row2 == PAD_LANE - nblk, NEG, 0.0)], axis=0)
    mb = mb_t.T
    mb4 = jnp.concatenate([mb] * 4, axis=0)
    qa_snear = q_aug(mb4)
    qa_sfar = q_aug(jnp.where((lane < nblk) & (lane >= 2 * qb - 2), NEG, mb4))

    sn0 = pl.multiple_of(t0 + KPAD - QBLOCK, QBLOCK)
    s = _dot_nt(qa_snear, ks_ref[pl.ds(sn0, 2 * QBLOCK), :]) + ts_ref[...]
    m = jnp.max(s, axis=-1, keepdims=True)
    p = jnp.exp(s - m)
    l = jnp.sum(p, axis=-1, keepdims=True)
    acc = _dot(p.astype(BF16), vs_ref[pl.ds(sn0, 2 * QBLOCK), :])
    far_tile = 512
    n_far = (jnp.maximum(qb - 1, 0) * QBLOCK + far_tile - 1) // far_tile

    m_s[...] = m
    l_s[...] = l
    acc_s[...] = acc
    last_tile = ks_ref.shape[0] // far_tile - 2

    def far_logits(j):
        r0 = pl.multiple_of(KPAD + j * far_tile, far_tile)
        return _dot_nt(qa_sfar, ks_ref[pl.ds(r0, far_tile), :])

    def far_update(j, s_ref):
        r0 = pl.multiple_of(KPAD + j * far_tile, far_tile)
        s = s_ref[...]
        m_old = m_s[...]
        m_new = jnp.maximum(m_old, jnp.max(s, axis=-1, keepdims=True))
        alpha = jnp.exp(m_old - m_new)
        p = jnp.exp(s - m_new)
        l_s[...] = alpha * l_s[...] + jnp.sum(p, axis=-1, keepdims=True)
        acc_s[...] = alpha * acc_s[...] + _dot(p.astype(BF16), vs_ref[pl.ds(r0, far_tile), :])
        m_s[...] = m_new

    sa_s[...] = far_logits(0)

    def far_body(i, carry):
        a = 2 * i
        sb_s[...] = far_logits(a + 1)
        far_update(a, sa_s)
        sa_s[...] = far_logits(jnp.minimum(a + 2, last_tile))
        far_update(a + 1, sb_s)
        return carry

    lax.fori_loop(0, (n_far + 1) // 2, far_body, 0)
    o_s = acc_s[...] * (1.0 / l_s[...])

    w0 = pl.multiple_of(t0, QBLOCK)
    wlen = WINDOW + QBLOCK
    s = _dot_nt(qa_pad, kw_ref[pl.ds(w0, wlen), :]) + tw_ref[...]
    m = jnp.max(s, axis=-1, keepdims=True)
    p = jnp.exp(s - m)
    l = jnp.sum(p, axis=-1, keepdims=True)
    o_w = _dot(p.astype(BF16), vw_ref[pl.ds(w0, wlen), :]) * (1.0 / l)

    gt = gate_ref[...]
    outs = []
    for r in range(4):
        rs = slice(r * QBLOCK, (r + 1) * QBLOCK)
        outs.append(gt[:, 3 * r:3 * r + 1] * o_c[rs] + gt[:, 3 * r + 1:3 * r + 2] * o_s[rs]
                    + gt[:, 3 * r + 2:3 * r + 3] * o_w[rs])
    o_ref[...] = jnp.concatenate(outs, axis=1).astype(BF16)


def _nsa_prompt(q, gates_g, kc, vc, ov, ks, vs, kw, vw, tab_c, tab_s, tab_w):
    b, t, _ = q.shape
    nqb = t // QBLOCK
    kvspec = lambda a: pl.BlockSpec((None, None) + a.shape[2:], lambda bi, g, i: (bi, g, 0, 0))
    tabspec = lambda a: pl.BlockSpec((None,) + a.shape[1:], lambda bi, g, i: (g, 0, 0))
    return pl.pallas_call(
        _nsa_prompt_kernel,
        grid=(b, 2, nqb),
        in_specs=[pl.BlockSpec((None, QBLOCK, 512), lambda bi, g, i: (bi, i, g)),
                  pl.BlockSpec((None, None, QBLOCK, LANES), lambda bi, g, i: (bi, g, i, 0)),
                  kvspec(kc), kvspec(vc), pl.BlockSpec(ov.shape, lambda bi, g, i: (0, 0)),
                  kvspec(ks), kvspec(vs), kvspec(kw), kvspec(vw),
                  tabspec(tab_c), tabspec(tab_s), tabspec(tab_w)],
        out_specs=pl.BlockSpec((None, QBLOCK, 512), lambda bi, g, i: (bi, i, g)),
        out_shape=jax.ShapeDtypeStruct((b, t, 1024), BF16),
        scratch_shapes=[pltpu.VMEM((4 * QBLOCK, 1), F32), pltpu.VMEM((4 * QBLOCK, 1), F32),
                        pltpu.VMEM((4 * QBLOCK, LANES), F32),
                        pltpu.VMEM((4 * QBLOCK, 512), F32), pltpu.VMEM((4 * QBLOCK, 512), F32)],
        compiler_params=_cparams(("arbitrary", "arbitrary", "arbitrary")),
        name="nsa_prompt",
    )(q, gates_g, kc, vc, ov, ks, vs, kw, vw, tab_c, tab_s, tab_w)


def _decode_attend(qf, g0rows, gk_row, k_tiles, v_tiles, bias, key_mask, new_row, bias_new):
    qg = qf * gk_row
    zero = jnp.zeros_like(qg)
    qbd = jnp.concatenate([jnp.where(g0rows, qg, zero), jnp.where(g0rows, zero, qg)], axis=1).astype(BF16)
    one = jnp.ones_like(qg)
    ones_bd = jnp.concatenate([jnp.where(g0rows, one, zero), jnp.where(g0rows, zero, one)], axis=1).astype(BF16)
    s_parts, q_parts = [], []
    for kt in k_tiles:
        k = kt()
        s_parts.append(_dot_nt(qbd, k.astype(BF16)))
        q_parts.append(_dot_nt(ones_bd, (k * k).astype(BF16)))
    s = jnp.concatenate(s_parts, axis=1)
    ssq = jnp.concatenate(q_parts, axis=1)
    s = s * lax.rsqrt(ssq * (1.0 / HEAD_DIM) + EPS) + bias
    if key_mask is not None:
        s = jnp.where(key_mask > 0.5, s, NEG)
    bc = lambda c: jnp.broadcast_to(new_row[c:c + 1, :], qf.shape)
    k_new = jnp.where(g0rows, bc(0), bc(1))
    v_new = jnp.where(g0rows, bc(2), bc(3))
    s_new = jnp.sum(qg * _rms(k_new), axis=-1, keepdims=True) + bias_new
    m = jnp.maximum(jnp.max(s, axis=-1, keepdims=True), s_new)
    p = jnp.exp(s - m)
    p_new = jnp.exp(s_new - m)
    l = jnp.sum(p, axis=-1, keepdims=True) + p_new
    o2 = jnp.zeros((qf.shape[0], 2 * LANES), F32)
    for i, vt in enumerate(v_tiles):
        o2 = o2 + _dot(p[:, i * LANES:(i + 1) * LANES].astype(BF16), vt().astype(BF16))
    o = jnp.where(g0rows, o2[:, :LANES], o2[:, LANES:]) + p_new * v_new
    return o * (1.0 / l)


def _nsa_sample_kernel(pt_ref, cache_c_hbm, cache_s_hbm, *refs, n_pages, n_sel_blocks, group):
    per_b_in, consts, outs = refs[:5], refs[5:-5], refs[-5:-3]
    buf_c, buf_s, sem = refs[-3:]
    step = pl.program_id(0)
    slot = step % 2

    def page_copies(src_step, dst_slot, for_wait):
        copies = []
        for e in range(group):
            for p in range(n_pages):
                pg = 0 if for_wait else pt_ref[src_step * group + e, p]
                copies.append(pltpu.make_async_copy(cache_c_hbm.at[pg], buf_c.at[dst_slot, e, p], sem.at[dst_slot, 0]))
                copies.append(pltpu.make_async_copy(cache_s_hbm.at[pg], buf_s.at[dst_slot, e, p], sem.at[dst_slot, 1]))
        return copies

    @pl.when(step == 0)
    def _():
        for c in page_copies(0, 0, False):
            c.start()

    @pl.when(step + 1 < pl.num_programs(0))
    def _():
        for c in page_copies(step + 1, 1 - slot, False):
            c.start()

    for c in page_copies(step, slot, True):
        c.wait()

    chains = [_nsa_sample_one([buf_c.at[slot, bb, p] for p in range(n_pages)],
                              [buf_s.at[slot, bb, p] for p in range(n_pages)],
                              *[r.at[bb] for r in per_b_in], *consts, *[r.at[bb] for r in outs],
                              n_pages=n_pages, n_sel_blocks=n_sel_blocks) for bb in range(group)]
    while chains:
        chains = [c for c in chains if next(c, "done") != "done"]


def _nsa_sample_one(pages_c, pages_s, win_ref, q_ref, gate_ref, ksn_ref, kwn_ref, wk_ref, wv_ref, pos_ref, gk_ref,
                    bc_ref, bs_ref, bw_ref, bnew_ref, ov_ref, e_ref, perm_ref, o_ref, wout_ref, *,
                    n_pages, n_sel_blocks):
    qf = q_ref[...]
    nh = qf.shape[0]
    g0rows = lax.broadcasted_iota(jnp.int32, (nh, LANES), 0) < nh // 2
    lane = lax.broadcasted_iota(jnp.int32, (nh, LANES), 1)

    page = pages_c[0].shape[0] // 4
    nseg = n_pages * page // CMP_STRIDE
    perm = perm_ref[...]
    regrouped = []
    for pp in range(n_pages // 2):
        per_cp = []
        for cp in range(2):
            blk = jnp.concatenate(
                [jnp.concatenate([pages_c[2 * pp + i][pl.ds(2 * cp + c, page, stride=4), :] for c in range(2)], axis=1)
                 for i in range(2)], axis=0).astype(BF16)
            per_cp.append(_dot(perm, blk).astype(BF16))
        regrouped.append(per_cp)
    rows_per_tap = 2 * page // CMP_STRIDE
    yield

    def tap_rows(s, col):
        lo, hi = s * rows_per_tap, (s + 1) * rows_per_tap
        return jnp.concatenate([regrouped[pp][col // 2][lo:hi, (col % 2) * LANES:(col % 2 + 1) * LANES]
                                for pp in range(n_pages // 2)], axis=0)

    load_pair = lambda s2, col: jnp.concatenate([tap_rows(2 * s2, col), tap_rows(2 * s2 + 1, col)], axis=1)
    ck, cv = _compress_core(load_pair, nseg, wk_ref, wv_ref, pos_ref)
    yield
    kcn = jnp.concatenate([(_rms(ck[g]) * gk_ref[0:1, :]).astype(BF16) for g in range(2)], axis=0)
    s2 = _dot_nt(qf.astype(BF16), kcn)
    s = jnp.where(g0rows, s2[:, :nseg], s2[:, nseg:]) + bc_ref[...]
    m = jnp.max(s, axis=-1, keepdims=True)
    p = jnp.exp(s - m)
    linv = 1.0 / jnp.sum(p, axis=-1, keepdims=True)
    rowv = lax.broadcasted_iota(jnp.int32, (nseg, LANES), 0) < nseg - 1
    pb = p.astype(BF16)
    oc = [_dot(pb, jnp.where(rowv, cv[g], 0.0).astype(BF16)) for g in range(2)]
    o_c = jnp.where(g0rows, oc[0], oc[1]) * linv

    yield
    pn = p * linv
    s0 = jnp.sum(jnp.where(g0rows, pn, 0.0), axis=0, keepdims=True)
    s1 = jnp.sum(jnp.where(g0rows, 0.0, pn), axis=0, keepdims=True)
    psum = jnp.where(g0rows, jnp.broadcast_to(s0, pn.shape), jnp.broadcast_to(s1, pn.shape))
    imp = _dot_hilo(psum, ov_ref[...].astype(BF16))
    cur = n_sel_blocks - 1
    forced = (lane == 0) | (lane == cur) | (lane == cur - 1)
    val = jnp.where(lane >= n_sel_blocks, -2.0, jnp.where(forced, FORCE, imp))
    rank = jnp.zeros_like(val)
    for i in range(n_sel_blocks):
        ci = jnp.broadcast_to(val[:, i:i + 1], val.shape)
        beats = (ci > val) | ((ci == val) & (lane > i))
        rank = rank + jnp.where(beats, 1.0, 0.0)
    sel = jnp.where((rank < float(N_SEL)) & (lane < n_sel_blocks), 1.0, 0.0)
    key_mask = _dot(sel.astype(BF16), e_ref[...])

    yield
    def tiles(ref_list, col):
        def tile(rf, i):
            rows = lambda c: rf[pl.ds(4 * LANES * i + c, LANES, stride=4), :]
            return lambda: jnp.concatenate([rows(col), rows(col + 1)], axis=1)
        return [tile(rf, i) for rf in ref_list for i in range(rf.shape[0] // (4 * LANES))]

    b_new = bnew_ref[:, 0:1]
    o_s = _decode_attend(qf, g0rows, gk_ref[1:2, :], tiles(pages_s, 0), tiles(pages_s, 2),
                         bs_ref[...], key_mask, ksn_ref[...], b_new)
    yield
    o_w = _decode_attend(qf, g0rows, gk_ref[2:3, :], tiles([win_ref], 0), tiles([win_ref], 2),
                         bw_ref[...], None, kwn_ref[...], b_new)
    yield
    gt = gate_ref[...]
    o_ref[...] = gt[:, 0:1] * o_c + gt[:, 1:2] * o_s + gt[:, 2:3] * o_w

    keep = wout_ref.shape[0] - 4
    drop = win_ref.shape[0] - keep
    wout_ref[0:keep, :] = win_ref[drop:drop + keep, :]
    wout_ref[keep:keep + 4, :] = kwn_ref[...]


def _nsa_sample(page_table, cache_c, cache_s, win, q, gates, ks_new, kw_new, wk_pairs, wv_pairs, pos_rows, gk,
                bias_c, bias_s, bias_w, bias_new, ov, expand):
    nb, n_pages = page_table.shape
    page_rows = cache_c.shape[1]
    n_sel_blocks = -(-(n_pages * (page_rows // 4) + 1) // SEL_LEN)
    group = SAMPLE_GROUP if nb % SAMPLE_GROUP == 0 else 1
    kern = functools.partial(_nsa_sample_kernel, n_pages=n_pages, n_sel_blocks=n_sel_blocks, group=group)
    win_rows_out = 4 * min(WINDOW, n_pages * (page_rows // 4) + 1)
    page = page_rows // 4
    assert n_pages % 2 == 0 and page % CMP_STRIDE == 0
    segs = page // CMP_STRIDE
    i_, n_, s_ = np.meshgrid(np.arange(2), np.arange(segs), np.arange(CMP_STRIDE), indexing="ij")
    perm_np = np.zeros((2 * page, 2 * page), np.float32)
    perm_np[(s_ * 2 * segs + i_ * segs + n_).ravel(), (i_ * page + CMP_STRIDE * n_ + s_).ravel()] = 1.0
    perm = jnp.asarray(perm_np, BF16)
    hbm = pl.BlockSpec(memory_space=pl.ANY)
    full = lambda a: pl.BlockSpec(a.shape, lambda bi, pt: (0,) * a.ndim)
    per_b = lambda a: pl.BlockSpec((group,) + a.shape[1:], lambda bi, pt: (bi,) + (0,) * (a.ndim - 1))
    page_buf = pltpu.VMEM((2, group, n_pages, page_rows, LANES), F32)
    grid_spec = pltpu.PrefetchScalarGridSpec(
        num_scalar_prefetch=1,
        grid=(nb // group,),
        scratch_shapes=[page_buf, page_buf, pltpu.SemaphoreType.DMA((2, 2))],
        in_specs=([hbm, hbm]
                  + [per_b(win), per_b(q), per_b(gates), per_b(ks_new), per_b(kw_new),
                     full(wk_pairs), full(wv_pairs), full(pos_rows), full(gk),
                     full(bias_c), full(bias_s), full(bias_w), full(bias_new), full(ov), full(expand), full(perm)]),
        out_specs=[pl.BlockSpec((group,) + q.shape[1:], lambda bi, pt: (bi, 0, 0)),
                   pl.BlockSpec((group, win_rows_out, LANES), lambda bi, pt: (bi, 0, 0))],
    )
    return pl.pallas_call(
        kern,
        grid_spec=grid_spec,
        out_shape=(jax.ShapeDtypeStruct(q.shape, F32), jax.ShapeDtypeStruct((nb, win_rows_out, LANES), F32)),
        compiler_params=_cparams(("arbitrary",)),
        name="nsa_sample",
    )(page_table, cache_c, cache_s, win, q, gates, ks_new, kw_new,
      wk_pairs, wv_pairs, pos_rows, gk, bias_c, bias_s, bias_w, bias_new, ov, expand, perm)


def _merge_kernel(h_ref, oa_ref, ob_ref, wga_ref, wgb_ref, wg_ref, wn_ref, mix_ref, *cast_refs):
    h = h_ref[...]
    wga = wga_ref[...].astype(BF16)
    wgb = wgb_ref[...].astype(BF16)
    wg = wg_ref[...].astype(BF16)
    wn = wn_ref[...].astype(BF16)
    ga = _sigmoid(_dot_nt(h, wga))
    gb = _sigmoid(_dot_nt(h, wgb))
    mix_ref[...] = (ga * _dot(oa_ref[...], wg) + gb * _dot(ob_ref[...], wn)).astype(BF16)
    for ref, val in zip(cast_refs, (wga, wgb, wg, wn)):
        ref[...] = val


def _merge(h, oa, ob, wga, wgb, wg, wn, gate_rows=None):
    n, d = h.shape
    tm = 512 if n % 512 == 0 else n
    tn = 512
    emit = gate_rows is not None
    assert not emit or n == tm
    row = lambda w: pl.BlockSpec((tm, w), lambda i, j: (i, 0))
    col = lambda k: pl.BlockSpec((k, tn), lambda i, j: (0, j))
    colt = pl.BlockSpec((tn, d), lambda i, j: (j, 0))
    out_specs = [pl.BlockSpec((tm, tn), lambda i, j: (i, j))]
    out_shape = [jax.ShapeDtypeStruct((n, d), BF16)]
    gate_specs = [colt, colt]
    if emit:
        assert all(r % 8 == 0 for r in gate_rows)
        gate_specs = [pl.BlockSpec((pl.Element(tn), pl.Element(d)),
                                   lambda i, j, r=r: ((r // 8 + j * (tn // 8)) * 8, 0)) for r in gate_rows]
        out_specs += [colt, colt, col(wg.shape[0]), col(wn.shape[0])]
        out_shape += [jax.ShapeDtypeStruct((d, d), BF16), jax.ShapeDtypeStruct((d, d), BF16),
                      jax.ShapeDtypeStruct(wg.shape, BF16), jax.ShapeDtypeStruct(wn.shape, BF16)]
    return pl.pallas_call(
        _merge_kernel,
        grid=(n // tm, d // tn),
        in_specs=[row(d), row(oa.shape[1]), row(ob.shape[1])] + gate_specs + [col(wg.shape[0]), col(wn.shape[0])],
        out_specs=out_specs,
        out_shape=out_shape,
        compiler_params=_cparams(("arbitrary", "arbitrary")),
        name="merge",
    )(h, oa, ob, wga, wgb, wg, wn)


def _outproj_kernel(x_ref, mix_ref, wout_ref, gmlp_ref, x1_ref, hm_ref):
    x1 = x_ref[...] + _dot(mix_ref[...], wout_ref[...])
    x1_ref[...] = x1
    hm_ref[...] = (_rms(x1) * gmlp_ref[...]).astype(BF16)


def _outproj(x, mix, wout, gmlp):
    n, d = x.shape
    tm = 512 if n % 512 == 0 else n
    row = pl.BlockSpec((tm, d), lambda i: (i, 0))
    return pl.pallas_call(
        _outproj_kernel,
        grid=(n // tm,),
        in_specs=[row, row, pl.BlockSpec(wout.shape, lambda i: (0, 0)), pl.BlockSpec(gmlp.shape, lambda i: (0, 0))],
        out_specs=[row, row],
        out_shape=(jax.ShapeDtypeStruct((n, d), F32), jax.ShapeDtypeStruct((n, d), BF16)),
        compiler_params=_cparams(("arbitrary",)),
        name="outproj",
    )(x, mix, wout, gmlp)


def _ffn_kernel(hm_ref, x1_ref, wup_ref, wdown_ref, y_ref, *cast_refs):
    f = pl.program_id(1)

    @pl.when(f == 0)
    def _():
        y_ref[...] = x1_ref[...]

    wup = wup_ref[...].astype(BF16)
    wdown = wdown_ref[...].astype(BF16)
    hid = jnp.maximum(_dot(hm_ref[...], wup), 0.0)
    y_ref[...] += _dot((hid * hid).astype(BF16), wdown)
    if cast_refs:
        cast_refs[0][...] = wup
        cast_refs[1][...] = wdown


def _ffn(hm, x1, wup, wdown):
    n, d = hm.shape
    dff = wup.shape[1]
    tm = 512 if n % 512 == 0 else n
    emit = wup.dtype != BF16
    assert not emit or n == tm
    tf = 512 if emit else 1024
    row = pl.BlockSpec((tm, d), lambda i, f: (i, 0))
    up_spec = pl.BlockSpec((d, tf), lambda i, f: (0, f))
    down_spec = pl.BlockSpec((tf, d), lambda i, f: (f, 0))
    out_specs = [row]
    out_shape = [jax.ShapeDtypeStruct((n, d), F32)]
    if emit:
        out_specs += [up_spec, down_spec]
        out_shape += [jax.ShapeDtypeStruct(wup.shape, BF16), jax.ShapeDtypeStruct(wdown.shape, BF16)]
    return pl.pallas_call(
        _ffn_kernel,
        grid=(n // tm, dff // tf),
        in_specs=[row, row, up_spec, down_spec],
        out_specs=out_specs,
        out_shape=out_shape,
        compiler_params=_cparams(("arbitrary", "arbitrary")),
        name="ffn",
    )(hm, x1, wup, wdown)


def _bucket(rel, valid):
    n = np.maximum(rel, 0)
    max_exact = N_BUCKETS // 2
    nf = np.maximum(n, 1).astype(np.float32)
    large = max_exact + (np.log(nf / np.float32(max_exact)) / np.float32(math.log(MAX_DIST / max_exact))
                         * np.float32(N_BUCKETS - max_exact)).astype(np.int32)
    large = np.minimum(large, N_BUCKETS - 1)
    return np.where(valid, np.where(n < max_exact, n, large), -1).astype(np.int32)


def _bias_tables_kernel(rb_ref, *refs, shifts):
    n = len(shifts)
    nbk, nh = rb_ref.shape
    for b_ref, o_ref, shift in zip(refs[:n], refs[n:], shifts):
        b = b_ref[...]
        rows = []
        for h in range(nh):
            sh = rb_ref[nbk - 1, h] if shift else 0.0
            acc = jnp.full(b.shape, NEG, F32)
            for k in range(nbk):
                acc = jnp.where(b == k, rb_ref[k, h] - sh, acc)
            if len(o_ref.shape) == 3:
                o_ref[h] = acc
            else:
                rows.append(acc)
        if rows:
            o_ref[...] = jnp.concatenate(rows, axis=0)


def _bias_tables(rel_bias, buckets, shifts):
    nh = rel_bias.shape[1]
    shapes = [jax.ShapeDtypeStruct((nh,) + (b.shape if b.shape[0] > 1 else b.shape[1:]), F32) for b in buckets]
    vm = pl.BlockSpec(memory_space=pltpu.VMEM)
    return pl.pallas_call(
        functools.partial(_bias_tables_kernel, shifts=tuple(shifts)),
        in_specs=[pl.BlockSpec(memory_space=pltpu.SMEM)] + [vm] * len(buckets),
        out_specs=[vm] * len(buckets),
        out_shape=shapes,
        name="bias_tables",
    )(rel_bias, *[jnp.asarray(b) for b in buckets])


def _overlap(nc, ns):
    i = np.arange(nc)[:, None] * CMP_STRIDE
    j = np.arange(ns)[None, :] * SEL_LEN
    return ((i < j + SEL_LEN) & (i + CMP_LEN > j)).astype(np.float32)


def _compress_weights(w, pos):
    s = np.arange(0, CMP_STRIDE, 2)
    top = jnp.concatenate([w[s], w[CMP_STRIDE + s]], axis=2)
    bot = jnp.concatenate([w[s + 1], w[CMP_STRIDE + s + 1]], axis=2)
    tiles = jnp.concatenate([top, bot], axis=1).astype(BF16)
    row_a = jnp.concatenate([pos[s], pos[s + 1]], axis=1)
    row_b = jnp.concatenate([pos[CMP_STRIDE + s], pos[CMP_STRIDE + s + 1]], axis=1)
    rows = jnp.zeros((len(s), 16, 2 * HEAD_DIM), F32).at[:, 0].set(row_a).at[:, 1].set(row_b)
    return tiles, rows.astype(BF16)


def _dense_tail(x, h, oa, ob, wga, wgb, wg, wn, wout, gmlp, wup, wdown, gate_rows=None):
    mix, *cast_m = _merge(h, oa, ob, wga, wgb, wg, wn, gate_rows)
    x1, hm = _outproj(x, mix, wout, gmlp)
    y, *cast_f = _ffn(hm, x1, wup, wdown)
    return y, (cast_m or [wga, wgb, wg, wn]) + (cast_f or [wup, wdown])


def kernel(x_prompt, x_sample, cache_cmp_kv, cache_sel_kv, state_win_kv, page_table, rel_bias, g_mix_norm, w_in,
           g_sgu, w_sgu, b_sgu, g_q, g_k, pos_cmp_k, w_cmp_k, pos_cmp_v, w_cmp_v, w_proj_gmlp, w_proj_nsa, w_out,
           g_mlp_norm, w_up, w_down):
    depth = g_mix_norm.shape[0]
    assert depth == 1
    l = 0
    bsz, seq, d = x_prompt.shape
    nb = x_sample.shape[0]
    assert x_sample.shape[1] == 1 and seq % KPAD == 0
    d_gm = g_sgu.shape[1]
    n_heads = rel_bias.shape[1]
    d_nsa = n_heads * HEAD_DIM
    kvw_cols = 2 * (n_heads // 4) * HEAD_DIM
    n_gate = 3 * n_heads
    c_q = 2 * d_gm
    c_kv = c_q + d_nsa
    c_gate = c_kv + 3 * kvw_cols
    c_ga = c_gate + n_gate
    c_gb = c_ga + d

    w = jnp.swapaxes(w_in[l], 0, 1)
    n_col_tiles = -(-c_ga // 1024)
    wout = w_out[l].astype(BF16)
    gmix = g_mix_norm[l][None]
    gsgu = g_sgu[l][None]
    gq = g_q[l][None]
    gk = g_k[l]
    gmlp = g_mlp_norm[l][None]
    wk_pairs, posk_rows = _compress_weights(w_cmp_k[l], pos_cmp_k[l])
    wv_pairs, posv_rows = _compress_weights(w_cmp_v[l], pos_cmp_v[l])
    pos_rows = jnp.stack([posk_rows, posv_rows])
    n_groups = w_sgu.shape[1]
    b_exp = jnp.repeat(b_sgu[l].T, d_gm // n_groups, axis=1)
    w00 = jnp.repeat(w_sgu[l][:, 0, 0], d_gm // n_groups)[None]
    b00 = jnp.repeat(b_sgu[l][:, 0], d_gm // n_groups)[None]

    n_pool, page = cache_cmp_kv.shape[1], cache_cmp_kv.shape[2]
    n_pages = page_table.shape[1]
    past = n_pages * page
    nwin = state_win_kv.shape[2]
    n_cmp_s = (past + 1 - CMP_LEN) // CMP_STRIDE + 1
    n_sel_s = -(-(past + 1) // SEL_LEN)
    nseg_s = past // CMP_STRIDE
    qi = np.arange(QBLOCK)[:, None]
    rel_s = qi + QBLOCK - np.arange(2 * QBLOCK)[None, :]
    rel_w = qi + WINDOW - np.arange(WINDOW + QBLOCK)[None, :]
    rel_c = qi + (KC_FRONT * CMP_STRIDE - CMP_LEN + 1) - CMP_STRIDE * np.arange(LANES)[None, :]
    srel_c = (past - (np.arange(nseg_s) * CMP_STRIDE + CMP_LEN - 1))[None]
    srel_s = (past - np.arange(past))[None]
    srel_w = (nwin - np.arange(nwin))[None]
    buckets = [
        _bucket(rel_c, rel_c >= 0), _bucket(rel_s, rel_s >= 0), _bucket(rel_w, (rel_w >= 0) & (rel_w < WINDOW)),
        _bucket(srel_c, (srel_c >= 0) & (np.arange(nseg_s)[None] < n_cmp_s)), _bucket(srel_s, srel_s >= 0),
        _bucket(srel_w, srel_w < WINDOW), np.zeros((1, LANES), np.int32)]
    tab_c, tab_s, tab_w, bias_c, bias_s, bias_w, bias_new = _bias_tables(
        rel_bias, buckets, [True, True, False, False, False, False, False])
    tab_c, tab_s, tab_w = [t.reshape(2, (n_heads // 2) * QBLOCK, t.shape[-1]) for t in (tab_c, tab_s, tab_w)]

    xs = x_sample.reshape(nb, d)
    h_s, v_s, oa_s, q_s, kvc_s, kvs_s, kvw_s, gates_s, w_cat = _inproj(
        xs, gmix, w, gsgu, gq, w00, b00, chunked=False, n_col_tiles=n_col_tiles)
    ov_s = np.zeros((nseg_s, LANES), np.float32)
    ov_s[:n_cmp_s, :n_sel_s] = _overlap(n_cmp_s, n_sel_s)
    expand = jnp.asarray(np.arange(LANES)[:, None] == (np.arange(past)[None, :] // SEL_LEN), BF16)
    gates_h = jnp.pad(gates_s[:, :n_gate].reshape(nb, n_heads, 3), ((0, 0), (0, 0), (0, LANES - 3)))
    lin = lambda a: a[l].reshape(a.shape[1], -1, HEAD_DIM)
    win_lin = lin(state_win_kv)
    ob_s, win_new = _nsa_sample(
        page_table, lin(cache_cmp_kv), lin(cache_sel_kv), win_lin,
        q_s.astype(F32).reshape(nb, n_heads, HEAD_DIM), gates_h,
        kvs_s.reshape(nb, 4, HEAD_DIM), kvw_s.reshape(nb, 4, HEAD_DIM), wk_pairs, wv_pairs, pos_rows, gk,
        bias_c, bias_s, bias_w, bias_new, jnp.asarray(ov_s), expand)
    y_s, (wga, wgb, wg, wn, wup, wdown) = _dense_tail(
        xs, h_s, oa_s, ob_s.reshape(nb, d_nsa).astype(BF16), w, w, w_proj_gmlp[l], w_proj_nsa[l], wout, gmlp,
        w_up[l], w_down[l], gate_rows=(c_ga, c_gb))

    xp = x_prompt.reshape(bsz * seq, d)
    h_p, v_p, oa_p, q_p, kvc_p, kvs_p, kvw_p, gates_p = _inproj(
        xp, gmix, w_cat, gsgu, gq, w_sgu[l], b_exp, chunked=True, n_col_tiles=n_col_tiles)
    kvs3 = kvs_p.reshape(bsz, 4 * seq, HEAD_DIM)
    kvw3 = kvw_p.reshape(bsz, 4 * seq, HEAD_DIM)
    kvc3 = kvc_p.reshape(bsz, 4 * seq, HEAD_DIM)
    ks, vs, kw, vw = _kvprep(kvs3, kvw3, gk)
    kc, vc = _compress_prompt(kvc3, wk_pairs, wv_pairs, pos_rows, gk)
    n_cmp = (seq - CMP_LEN) // CMP_STRIDE + 1
    n_selb = seq // SEL_LEN
    ov_p = np.zeros((KC_ROWS, LANES), np.float32)
    ov_p[KC_FRONT:KC_FRONT + n_cmp, :n_selb] = _overlap(n_cmp, n_selb)
    ov_p = jnp.asarray(ov_p)
    gates_g = gates_p[:, :n_gate].reshape(bsz, seq, 2, n_gate // 2).transpose(0, 2, 1, 3)
    gates_g = jnp.pad(gates_g, ((0, 0), (0, 0), (0, 0), (0, LANES - n_gate // 2)))
    ob_p = _nsa_prompt(q_p.reshape(bsz, seq, d_nsa), gates_g, kc, vc, ov_p, ks, vs, kw, vw, tab_c, tab_s, tab_w)
    y_p, _ = _dense_tail(xp, h_p, oa_p, ob_p.reshape(bsz * seq, d_nsa), wga, wgb, wg, wn, wout, gmlp, wup, wdown)

    n_kv = n_heads // 4
    kv6 = lambda a, b_, t_: a.reshape(1, b_, t_, 2, n_kv, HEAD_DIM)
    nw_p = min(WINDOW, seq)
    last = ((seq - 1) // CHUNK) * CHUNK
    nw_s = min(WINDOW, past + 1)
    return (y_p.reshape(bsz, seq, d), y_s.reshape(nb, 1, d),
            kv6(kvc3, bsz, seq), kv6(kvs3, bsz, seq), kv6(kvw3[:, 4 * (seq - nw_p):], bsz, nw_p),
            v_p.reshape(bsz, seq, d_gm)[:, last:][None],
            kv6(kvc_s, nb, 1), kv6(kvs_s, nb, 1), kv6(win_new, nb, nw_s),
            v_s.reshape(1, nb, 1, d_gm))
```

```python
import functools
import math

import numpy as np
import jax
import jax.numpy as jnp
from jax import lax
from jax.experimental import pallas as pl
from jax.experimental.pallas import tpu as pltpu

F32 = jnp.float32
BF16 = jnp.bfloat16

HEAD_DIM = 128
CHUNK = 128
CMP_LEN = 32
CMP_STRIDE = 16
SEL_LEN = 64
N_SEL = 16
WINDOW = 512
N_BUCKETS = 32
MAX_DIST = 128
QBLOCK = 128
EPS = 1e-6
NEG = -1e30
HALF_NEG = -5e29
FORCE = 1e6

LANES = 128
PAD_LANE = 64
KPAD = 512
KC_FRONT = 16
KC_ROWS = 376
SAMPLE_GROUP = 2
VMEM_LIMIT = 56 * 1024 * 1024


def _cparams(sem):
    return pltpu.CompilerParams(dimension_semantics=sem, vmem_limit_bytes=VMEM_LIMIT)


def _dot(a, b):
    return jnp.dot(a, b, preferred_element_type=F32)


def _dot_nt(a, b):
    return lax.dot_general(a, b, (((1,), (1,)), ((), ())), preferred_element_type=F32)


def _dot_hilo(a, b_bf16):
    hi = a.astype(BF16)
    lo = (a - hi.astype(F32)).astype(BF16)
    return _dot(hi, b_bf16) + _dot(lo, b_bf16)


def _rms(x):
    return x * lax.rsqrt(jnp.mean(x * x, axis=-1, keepdims=True) + EPS)


def _gelu(x):
    c = math.sqrt(2.0 / math.pi)
    return 0.5 * x * (1.0 + jnp.tanh(c * (x + 0.044715 * (x * x * x))))


def _sigmoid(x):
    return 1.0 / (1.0 + jnp.exp(-x))


def _inproj_kernel(x_ref, gmix_ref, w_ref, gsgu_ref, gq_ref, wsg_ref, bsg_ref,
                   h_ref, v_ref, oa_ref, q_ref, kvc_ref, kvs_ref, kvw_ref, gate_ref, *rest, chunked, tm):
    h_s, u_s = rest[-2:]
    j = pl.program_id(1)

    @pl.when(j == 0)
    def _():
        hb = (_rms(x_ref[...]) * gmix_ref[...]).astype(BF16)
        h_s[...] = hb
        h_ref[...] = hb

    if len(rest) == 3:
        w = w_ref[...].astype(BF16)
        rest[0][...] = w
        z = _dot_nt(h_s[...], w)
    else:
        z = _dot_nt(h_s[...], w_ref[...])

    @pl.when(j == 0)
    def _():
        u_s[...] = _gelu(z)

    @pl.when(j == 1)
    def _():
        v = _rms(_gelu(z)) * gsgu_ref[...]
        v_ref[...] = v
        if chunked:
            row = lax.broadcasted_iota(jnp.int32, (CHUNK, CHUNK), 0)
            col = lax.broadcasted_iota(jnp.int32, (CHUNK, CHUNK), 1)
            n_groups = v.shape[1] // LANES
            for g in range(n_groups):
                wm = jnp.where(row >= col, wsg_ref[g], 0.0).astype(BF16)
                cs = slice(g * LANES, (g + 1) * LANES)
                for c in range(tm // CHUNK):
                    rs = slice(c * CHUNK, (c + 1) * CHUNK)
                    s = _dot(wm, v[rs, cs].astype(BF16)) + bsg_ref[:, cs]
                    oa_ref[rs, cs] = (u_s[rs, cs] * s).astype(BF16)
        else:
            oa_ref[...] = (u_s[...] * (v * wsg_ref[...] + bsg_ref[...])).astype(BF16)

    @pl.when(j == 2)
    def _():
        scale = HEAD_DIM ** -0.5
        for hd in range(z.shape[1] // HEAD_DIM):
            cs = slice(hd * HEAD_DIM, (hd + 1) * HEAD_DIM)
            q_ref[:, cs] = (_rms(z[:, cs]) * gq_ref[...] * scale).astype(BF16)

    def store_kv(ref, zz):
        for c in range(4):
            ref[pl.ds(c, tm, stride=4), :] = zz[:, c * LANES:(c + 1) * LANES]

    @pl.when(j == 3)
    def _():
        store_kv(kvc_ref, z[:, :512])
        store_kv(kvs_ref, z[:, 512:])

    @pl.when(j == 4)
    def _():
        store_kv(kvw_ref, z[:, :512])
        gate_ref[...] = _sigmoid(z[:, 512:640])


def _inproj(x, gmix, w_cat, gsgu, gq, wsg, bsg, *, chunked, n_col_tiles):
    n, d = x.shape
    tm = 512 if n % 512 == 0 else n
    tn = 1024
    emit = w_cat.dtype != BF16
    assert not emit or n == tm
    kern = functools.partial(_inproj_kernel, chunked=chunked, tm=tm)
    full = lambda a: pl.BlockSpec(a.shape, lambda i, j: (0,) * a.ndim)
    row = lambda w: pl.BlockSpec((tm, w), lambda i, j: (i, 0))
    kvrow = pl.BlockSpec((4 * tm, LANES), lambda i, j: (i, 0))
    out_shapes = (
        jax.ShapeDtypeStruct((n, d), BF16),
        jax.ShapeDtypeStruct((n, 1024), F32),
        jax.ShapeDtypeStruct((n, 1024), BF16),
        jax.ShapeDtypeStruct((n, 1024), BF16),
        jax.ShapeDtypeStruct((4 * n, LANES), F32),
        jax.ShapeDtypeStruct((4 * n, LANES), F32),
        jax.ShapeDtypeStruct((4 * n, LANES), F32),
        jax.ShapeDtypeStruct((n, LANES), F32),
    )
    wspec = pl.BlockSpec((tn, d), lambda i, j: (j, 0))
    out_specs = [row(d), row(1024), row(1024), row(1024), kvrow, kvrow, kvrow, row(LANES)]
    if emit:
        out_shapes += (jax.ShapeDtypeStruct((n_col_tiles * tn, d), BF16),)
        out_specs.append(wspec)
    return pl.pallas_call(
        kern,
        grid=(n // tm, n_col_tiles),
        in_specs=[row(d), full(gmix), wspec, full(gsgu), full(gq), full(wsg), full(bsg)],
        out_specs=out_specs,
        out_shape=out_shapes,
        scratch_shapes=[pltpu.VMEM((tm, d), BF16), pltpu.VMEM((tm, 1024), F32)],
        compiler_params=_cparams(("arbitrary", "arbitrary")),
        name="inproj",
    )(x, gmix, w_cat, gsgu, gq, wsg, bsg)


def _kvprep_kernel(kvs_ref, kvw_ref, gk_ref, ks_ref, vs_ref, kw_ref, vw_ref):
    i = pl.program_id(1)
    rows = kvs_ref.shape[0] // 4
    col = lambda ref, c: ref[pl.ds(c, rows, stride=4), :]
    lane = lax.broadcasted_iota(jnp.int32, (rows, LANES), 1)
    row = lax.broadcasted_iota(jnp.int32, (rows, LANES), 0)

    @pl.when(i == 0)
    def _():
        aux = jnp.where(lane == PAD_LANE, 1.0, 0.0).astype(BF16)
        zk = jnp.zeros((rows, LANES), BF16)
        for g in range(2):
            ks_ref[g] = jnp.concatenate([zk, aux], axis=1)
            kw_ref[g] = jnp.concatenate([zk, aux], axis=1)
            vs_ref[g] = zk
            vw_ref[g] = zk

    @pl.when(i > 0)
    def _():
        blk = ((i - 1) * rows + row) // SEL_LEN
        onehot = jnp.where(lane == blk, 1.0, 0.0).astype(BF16)
        zaux = jnp.zeros((rows, LANES), BF16)
        for g in range(2):
            ks = (_rms(col(kvs_ref, g)) * gk_ref[1:2, :]).astype(BF16)
            kw = (_rms(col(kvw_ref, g)) * gk_ref[2:3, :]).astype(BF16)
            ks_ref[g] = jnp.concatenate([ks, onehot], axis=1)
            kw_ref[g] = jnp.concatenate([kw, zaux], axis=1)
            vs_ref[g] = col(kvs_ref, 2 + g).astype(BF16)
            vw_ref[g] = col(kvw_ref, 2 + g).astype(BF16)


def _kvprep(kvs, kvw, gk):
    b, t4, _ = kvs.shape
    t = t4 // 4
    rows = KPAD
    nblk = t // rows
    in_map = lambda bi, i: (bi, jnp.maximum(i - 1, 0), 0)
    out_map = lambda bi, i: (bi, 0, i, 0)
    kshape = jax.ShapeDtypeStruct((b, 2, KPAD + t, 2 * LANES), BF16)
    vshape = jax.ShapeDtypeStruct((b, 2, KPAD + t, LANES), BF16)
    return pl.pallas_call(
        _kvprep_kernel,
        grid=(b, nblk + 1),
        in_specs=[pl.BlockSpec((None, 4 * rows, LANES), in_map), pl.BlockSpec((None, 4 * rows, LANES), in_map),
                  pl.BlockSpec(gk.shape, lambda bi, i: (0, 0))],
        out_specs=[pl.BlockSpec((None, 2, rows, 2 * LANES), out_map), pl.BlockSpec((None, 2, rows, LANES), out_map),
                   pl.BlockSpec((None, 2, rows, 2 * LANES), out_map), pl.BlockSpec((None, 2, rows, LANES), out_map)],
        out_shape=(kshape, vshape, kshape, vshape),
        compiler_params=_cparams(("arbitrary", "arbitrary")),
        name="kvprep",
    )(kvs, kvw, gk)


def _compress_core(load_pair, nseg, wk_ref, wv_ref, pos_ref):
    w_refs = (wk_ref, wv_ref)
    accs = [jnp.zeros((2 * nseg + 16, 2 * LANES), F32) for _ in range(2)]
    for s2 in range(CMP_STRIDE // 2):
        for kv in range(2):
            parts = [load_pair(s2, 2 * kv + g) for g in range(2)]
            parts.append(pos_ref[kv, s2])
            lhs = jnp.concatenate(parts, axis=0)
            accs[kv] = accs[kv] + _dot(lhs, w_refs[kv][s2])
    outs = []
    for kv in range(2):
        y = accs[kv]
        post = y[2 * nseg:2 * nseg + 1, :LANES] + y[2 * nseg + 1:2 * nseg + 2, LANES:]
        per_g = []
        for g in range(2):
            y0 = y[g * nseg:(g + 1) * nseg, :LANES]
            y1 = y[g * nseg:(g + 1) * nseg, LANES:]
            per_g.append(y0 + pltpu.roll(y1, nseg - 1, 0) + post)
        outs.append(per_g)
    return outs


def _compress_prompt_kernel(x_ref, wk_ref, wv_ref, pos_ref, gk_ref, kc_ref, vc_ref):
    nseg = x_ref.shape[0] // (4 * CMP_STRIDE)
    load_x = lambda s, col: x_ref[pl.ds(4 * s + col, nseg, stride=4 * CMP_STRIDE), :]
    load_pair = lambda s2, col: jnp.concatenate([load_x(2 * s2, col), load_x(2 * s2 + 1, col)], axis=1).astype(BF16)
    ck, cv = _compress_core(load_pair, nseg, wk_ref, wv_ref, pos_ref)
    row = lax.broadcasted_iota(jnp.int32, (nseg, LANES), 0)
    lane = lax.broadcasted_iota(jnp.int32, (nseg, LANES), 1)
    valid = row < nseg - 1
    aux = jnp.where(valid, jnp.where(lane == row // 8, 1.0, 0.0), jnp.where(lane == PAD_LANE, 1.0, 0.0))
    back = KC_ROWS - KC_FRONT - nseg
    lane_f = lax.broadcasted_iota(jnp.int32, (KC_FRONT, LANES), 1)
    lane_b = lax.broadcasted_iota(jnp.int32, (back, LANES), 1)
    pad_f = jnp.concatenate([jnp.zeros((KC_FRONT, LANES), F32), jnp.where(lane_f == PAD_LANE, 1.0, 0.0)], axis=1)
    pad_b = jnp.concatenate([jnp.zeros((back, LANES), F32), jnp.where(lane_b == PAD_LANE, 1.0, 0.0)], axis=1)
    for g in range(2):
        kn = jnp.where(valid, _rms(ck[g]) * gk_ref[0:1, :], 0.0)
        kc_ref[g, 0:KC_FRONT, :] = pad_f
        kc_ref[g, KC_FRONT:KC_FRONT + nseg, :] = jnp.concatenate([kn, aux], axis=1)
        kc_ref[g, KC_FRONT + nseg:KC_ROWS, :] = pad_b
        vc_ref[g, 0:KC_FRONT, :] = jnp.zeros((KC_FRONT, LANES), F32)
        vc_ref[g, KC_FRONT:KC_FRONT + nseg, :] = jnp.where(valid, cv[g], 0.0)
        vc_ref[g, KC_FRONT + nseg:KC_ROWS, :] = jnp.zeros((back, LANES), F32)


def _compress_prompt(kvc, wk_pairs, wv_pairs, pos_rows, gk):
    b, t4, _ = kvc.shape
    full = lambda a: pl.BlockSpec(a.shape, lambda bi: (0,) * a.ndim)
    return pl.pallas_call(
        _compress_prompt_kernel,
        grid=(b,),
        in_specs=[pl.BlockSpec((None, t4, LANES), lambda bi: (bi, 0, 0)),
                  full(wk_pairs), full(wv_pairs), full(pos_rows), full(gk)],
        out_specs=[pl.BlockSpec((None, 2, KC_ROWS, 2 * LANES), lambda bi: (bi, 0, 0, 0)),
                   pl.BlockSpec((None, 2, KC_ROWS, LANES), lambda bi: (bi, 0, 0, 0))],
        out_shape=(jax.ShapeDtypeStruct((b, 2, KC_ROWS, 2 * LANES), F32),
                   jax.ShapeDtypeStruct((b, 2, KC_ROWS, LANES), F32)),
        compiler_params=_cparams(("arbitrary",)),
        name="compress_prompt",
    )(kvc, wk_pairs, wv_pairs, pos_rows, gk)


def _nsa_prompt_kernel(q_ref, gate_ref, kc_ref, vc_ref, ov_ref, ks_ref, vs_ref, kw_ref, vw_ref,
                       tc_ref, ts_ref, tw_ref, o_ref, m_s, l_s, acc_s, sa_s, sb_s, sw_s):
    qb = pl.program_id(2)
    t0 = qb * QBLOCK
    rq = 4 * QBLOCK
    q = q_ref[...]
    q4 = jnp.concatenate([q[:, r * LANES:(r + 1) * LANES] for r in range(4)], axis=0)
    lane = lax.broadcasted_iota(jnp.int32, (rq, LANES), 1)
    is_pad_lane = lane == PAD_LANE

    def q_aug(mb):
        return jnp.concatenate([q4, mb.astype(BF16)], axis=1)

    qa_pad = q_aug(jnp.where(is_pad_lane, NEG, 0.0))

    far_mask = ((lane < 32) & (lane >= qb - 2)) | is_pad_lane
    qa_cfar = q_aug(jnp.where(far_mask, NEG, 0.0))
    ncmp = 256
    near0 = pl.multiple_of(qb * 8, 8)
    k_far = kc_ref[KC_FRONT:KC_FRONT + ncmp, :].astype(BF16)
    k_near = kc_ref[pl.ds(near0, LANES), :].astype(BF16)
    s_far = _dot_nt(qa_cfar, k_far)
    s_near = _dot_nt(qa_pad, k_near) + tc_ref[...]
    w0 = pl.multiple_of(t0, QBLOCK)
    wlen = WINDOW + QBLOCK
    sw_s[...] = _dot_nt(qa_pad, kw_ref[pl.ds(w0, wlen), :]) + tw_ref[...]
    m = jnp.maximum(jnp.max(s_far, axis=-1, keepdims=True), jnp.max(s_near, axis=-1, keepdims=True))
    p_far = jnp.where(s_far > HALF_NEG, jnp.exp(s_far - m), 0.0)
    p_near = jnp.where(s_near > HALF_NEG, jnp.exp(s_near - m), 0.0)
    l = jnp.sum(p_far, axis=-1, keepdims=True) + jnp.sum(p_near, axis=-1, keepdims=True)
    linv = 1.0 / jnp.where(l > 0.0, l, 1.0)
    v_far = vc_ref[KC_FRONT:KC_FRONT + ncmp, :].astype(BF16)
    v_near = vc_ref[pl.ds(near0, LANES), :].astype(BF16)
    o_c = (_dot(p_far.astype(BF16), v_far) + _dot(p_near.astype(BF16), v_near)) * linv

    s = sw_s[...]
    m = jnp.max(s, axis=-1, keepdims=True)
    p = jnp.exp(s - m)
    l = jnp.sum(p, axis=-1, keepdims=True)
    o_w = _dot(p.astype(BF16), vw_ref[pl.ds(w0, wlen), :]) * (1.0 / l)

    pn_far = p_far * linv
    pn_near = p_near * linv
    ps_far = sum(pn_far[r * QBLOCK:(r + 1) * QBLOCK] for r in range(4))
    ps_near = sum(pn_near[r * QBLOCK:(r + 1) * QBLOCK] for r in range(4))
    ov_far = ov_ref[KC_FRONT:KC_FRONT + ncmp, :].astype(BF16)
    ov_near = ov_ref[pl.ds(near0, LANES), :].astype(BF16)
    imp = _dot_hilo(ps_far, ov_far) + _dot_hilo(ps_near, ov_near)
    nblk = 64
    imp_t = imp.T[:nblk]
    blk = lax.broadcasted_iota(jnp.int32, (nblk, QBLOCK), 0)
    qpos = t0 + lax.broadcasted_iota(jnp.int32, (nblk, QBLOCK), 1)
    cur = qpos // SEL_LEN
    forced = (blk == 0) | (blk == cur) | (blk == cur - 1)
    eligible = blk * SEL_LEN <= qpos
    val = jnp.where(forced, FORCE, jnp.where(eligible, imp_t, -1.0))
    rank = jnp.zeros((nblk, QBLOCK), F32)
    for i in range(nblk):
        ri = val[i:i + 1, :]
        beats = (ri > val) | ((ri == val) & (blk > i))
        rank = rank + jnp.where(beats, 1.0, 0.0)
    mb_t = jnp.where(rank < float(N_SEL), 0.0, NEG)
    row2 = lax.broadcasted_iota(jnp.int32, (LANES - nblk, QBLOCK), 0)
    mb_t = jnp.concatenate([mb_t, jnp.where(row2 == PAD_LANE - nblk, NEG, 0.0)], axis=0)
    mb = mb_t.T
    mb4 = jnp.concatenate([mb] * 4, axis=0)
    qa_snear = q_aug(mb4)
    qa_sfar = q_aug(jnp.where((lane < nblk) & (lane >= 2 * qb - 2), NEG, mb4))

    far_tile = 512
    n_far = (jnp.maximum(qb - 1, 0) * QBLOCK + far_tile - 1) // far_tile

    def far_logits(j):
        r0 = pl.multiple_of(KPAD + j * far_tile, far_tile)
        return _dot_nt(qa_sfar, ks_ref[pl.ds(r0, far_tile), :])

    sn0 = pl.multiple_of(t0 + KPAD - QBLOCK, QBLOCK)
    s = _dot_nt(qa_snear, ks_ref[pl.ds(sn0, 2 * QBLOCK), :]) + ts_ref[...]
    sa_s[...] = far_logits(0)
    m = jnp.max(s, axis=-1, keepdims=True)
    p = jnp.exp(s - m)
    l = jnp.sum(p, axis=-1, keepdims=True)
    acc = _dot(p.astype(BF16), vs_ref[pl.ds(sn0, 2 * QBLOCK), :])

    m_s[...] = m
    l_s[...] = l
    acc_s[...] = acc
    last_tile = ks_ref.shape[0] // far_tile - 2

    def far_update(j, s_ref):
        r0 = pl.multiple_of(KPAD + j * far_tile, far_tile)
        s = s_ref[...]
        m_old = m_s[...]
        m_new = jnp.maximum(m_old, jnp.max(s, axis=-1, keepdims=True))
        alpha = jnp.exp(m_old - m_new)
        p = jnp.exp(s - m_new)
        l_s[...] = alpha * l_s[...] + jnp.sum(p, axis=-1, keepdims=True)
        acc_s[...] = alpha * acc_s[...] + _dot(p.astype(BF16), vs_ref[pl.ds(r0, far_tile), :])
        m_s[...] = m_new

    def far_body(i, carry):
        a = 2 * i
        sb_s[...] = far_logits(a + 1)
        far_update(a, sa_s)
        sa_s[...] = far_logits(jnp.minimum(a + 2, last_tile))
        far_update(a + 1, sb_s)
        return carry

    lax.fori_loop(0, (n_far + 1) // 2, far_body, 0)
    o_s = acc_s[...] * (1.0 / l_s[...])

    gt = gate_ref[...]
    outs = []
    for r in range(4):
        rs = slice(r * QBLOCK, (r + 1) * QBLOCK)
        outs.append(gt[:, 3 * r:3 * r + 1] * o_c[rs] + gt[:, 3 * r + 1:3 * r + 2] * o_s[rs]
                    + gt[:, 3 * r + 2:3 * r + 3] * o_w[rs])
    o_ref[...] = jnp.concatenate(outs, axis=1).astype(BF16)


def _nsa_prompt(q, gates_g, kc, vc, ov, ks, vs, kw, vw, tab_c, tab_s, tab_w):
    b, t, _ = q.shape
    nqb = t // QBLOCK
    kvspec = lambda a: pl.BlockSpec((None, None) + a.shape[2:], lambda bi, g, i: (bi, g, 0, 0))
    tabspec = lambda a: pl.BlockSpec((None,) + a.shape[1:], lambda bi, g, i: (g, 0, 0))
    return pl.pallas_call(
        _nsa_prompt_kernel,
        grid=(b, 2, nqb),
        in_specs=[pl.BlockSpec((None, QBLOCK, 512), lambda bi, g, i: (bi, i, g)),
                  pl.BlockSpec((None, None, QBLOCK, LANES), lambda bi, g, i: (bi, g, i, 0)),
                  kvspec(kc), kvspec(vc), pl.BlockSpec(ov.shape, lambda bi, g, i: (0, 0)),
                  kvspec(ks), kvspec(vs), kvspec(kw), kvspec(vw),
                  tabspec(tab_c), tabspec(tab_s), tabspec(tab_w)],
        out_specs=pl.BlockSpec((None, QBLOCK, 512), lambda bi, g, i: (bi, i, g)),
        out_shape=jax.ShapeDtypeStruct((b, t, 1024), BF16),
        scratch_shapes=[pltpu.VMEM((4 * QBLOCK, 1), F32), pltpu.VMEM((4 * QBLOCK, 1), F32),
                        pltpu.VMEM((4 * QBLOCK, LANES), F32),
                        pltpu.VMEM((4 * QBLOCK, 512), F32), pltpu.VMEM((4 * QBLOCK, 512), F32),
                        pltpu.VMEM((4 * QBLOCK, WINDOW + QBLOCK), F32)],
        compiler_params=_cparams(("arbitrary", "arbitrary", "arbitrary")),
        name="nsa_prompt",
    )(q, gates_g, kc, vc, ov, ks, vs, kw, vw, tab_c, tab_s, tab_w)


def _decode_attend(qf, g0rows, gk_row, k_tiles, v_tiles, bias, key_mask, new_row, bias_new):
    qg = qf * gk_row
    zero = jnp.zeros_like(qg)
    qbd = jnp.concatenate([jnp.where(g0rows, qg, zero), jnp.where(g0rows, zero, qg)], axis=1).astype(BF16)
    one = jnp.ones_like(qg)
    ones_bd = jnp.concatenate([jnp.where(g0rows, one, zero), jnp.where(g0rows, zero, one)], axis=1).astype(BF16)
    s_parts, q_parts = [], []
    for kt in k_tiles:
        k = kt()
        s_parts.append(_dot_nt(qbd, k.astype(BF16)))
        q_parts.append(_dot_nt(ones_bd, (k * k).astype(BF16)))
    s = jnp.concatenate(s_parts, axis=1)
    ssq = jnp.concatenate(q_parts, axis=1)
    s = s * lax.rsqrt(ssq * (1.0 / HEAD_DIM) + EPS) + bias
    if key_mask is not None:
        s = jnp.where(key_mask > 0.5, s, NEG)
    bc = lambda c: jnp.broadcast_to(new_row[c:c + 1, :], qf.shape)
    k_new = jnp.where(g0rows, bc(0), bc(1))
    v_new = jnp.where(g0rows, bc(2), bc(3))
    s_new = jnp.sum(qg * _rms(k_new), axis=-1, keepdims=True) + bias_new
    m = jnp.maximum(jnp.max(s, axis=-1, keepdims=True), s_new)
    p = jnp.exp(s - m)
    p_new = jnp.exp(s_new - m)
    l = jnp.sum(p, axis=-1, keepdims=True) + p_new
    o2 = jnp.zeros((qf.shape[0], 2 * LANES), F32)
    for i, vt in enumerate(v_tiles):
        o2 = o2 + _dot(p[:, i * LANES:(i + 1) * LANES].astype(BF16), vt().astype(BF16))
    o = jnp.where(g0rows, o2[:, :LANES], o2[:, LANES:]) + p_new * v_new
    return o * (1.0 / l)


def _nsa_sample_kernel(pt_ref, cache_c_hbm, cache_s_hbm, *refs, n_pages, n_sel_blocks, group):
    per_b_in, consts, outs = refs[:5], refs[5:-5], refs[-5:-3]
    buf_c, buf_s, sem = refs[-3:]
    step = pl.program_id(0)
    slot = step % 2

    def page_copies(src_step, dst_slot, for_wait):
        copies = []
        for e in range(group):
            for p in range(n_pages):
                pg = 0 if for_wait else pt_ref[src_step * group + e, p]
                copies.append(pltpu.make_async_copy(cache_c_hbm.at[pg], buf_c.at[dst_slot, e, p], sem.at[dst_slot, 0]))
                copies.append(pltpu.make_async_copy(cache_s_hbm.at[pg], buf_s.at[dst_slot, e, p], sem.at[dst_slot, 1]))
        return copies

    @pl.when(step == 0)
    def _():
        for c in page_copies(0, 0, False):
            c.start()

    @pl.when(step + 1 < pl.num_programs(0))
    def _():
        for c in page_copies(step + 1, 1 - slot, False):
            c.start()

    for c in page_copies(step, slot, True):
        c.wait()

    chains = [_nsa_sample_one([buf_c.at[slot, bb, p] for p in range(n_pages)],
                              [buf_s.at[slot, bb, p] for p in range(n_pages)],
                              *[r.at[bb] for r in per_b_in], *consts, *[r.at[bb] for r in outs],
                              n_pages=n_pages, n_sel_blocks=n_sel_blocks) for bb in range(group)]
    while chains:
        chains = [c for c in chains if next(c, "done") != "done"]


def _nsa_sample_one(pages_c, pages_s, win_ref, q_ref, gate_ref, ksn_ref, kwn_ref, wk_ref, wv_ref, pos_ref, gk_ref,
                    bc_ref, bs_ref, bw_ref, bnew_ref, ov_ref, e_ref, perm_ref, o_ref, wout_ref, *,
                    n_pages, n_sel_blocks):
    qf = q_ref[...]
    nh = qf.shape[0]
    g0rows = lax.broadcasted_iota(jnp.int32, (nh, LANES), 0) < nh // 2
    lane = lax.broadcasted_iota(jnp.int32, (nh, LANES), 1)

    page = pages_c[0].shape[0] // 4
    nseg = n_pages * page // CMP_STRIDE
    perm = perm_ref[...]
    regrouped = []
    for pp in range(n_pages // 2):
        per_cp = []
        for cp in range(2):
            blk = jnp.concatenate(
                [jnp.concatenate([pages_c[2 * pp + i][pl.ds(2 * cp + c, page, stride=4), :] for c in range(2)], axis=1)
                 for i in range(2)], axis=0).astype(BF16)
            per_cp.append(_dot(perm, blk).astype(BF16))
        regrouped.append(per_cp)
    rows_per_tap = 2 * page // CMP_STRIDE
    yield

    def tap_rows(s, col):
        lo, hi = s * rows_per_tap, (s + 1) * rows_per_tap
        return jnp.concatenate([regrouped[pp][col // 2][lo:hi, (col % 2) * LANES:(col % 2 + 1) * LANES]
                                for pp in range(n_pages // 2)], axis=0)

    load_pair = lambda s2, col: jnp.concatenate([tap_rows(2 * s2, col), tap_rows(2 * s2 + 1, col)], axis=1)
    ck, cv = _compress_core(load_pair, nseg, wk_ref, wv_ref, pos_ref)
    yield
    kcn = jnp.concatenate([(_rms(ck[g]) * gk_ref[0:1, :]).astype(BF16) for g in range(2)], axis=0)
    s2 = _dot_nt(qf.astype(BF16), kcn)
    s = jnp.where(g0rows, s2[:, :nseg], s2[:, nseg:]) + bc_ref[...]
    m = jnp.max(s, axis=-1, keepdims=True)
    p = jnp.exp(s - m)
    linv = 1.0 / jnp.sum(p, axis=-1, keepdims=True)
    rowv = lax.broadcasted_iota(jnp.int32, (nseg, LANES), 0) < nseg - 1
    pb = p.astype(BF16)
    oc = [_dot(pb, jnp.where(rowv, cv[g], 0.0).astype(BF16)) for g in range(2)]
    o_c = jnp.where(g0rows, oc[0], oc[1]) * linv

    yield
    pn = p * linv
    s0 = jnp.sum(jnp.where(g0rows, pn, 0.0), axis=0, keepdims=True)
    s1 = jnp.sum(jnp.where(g0rows, 0.0, pn), axis=0, keepdims=True)
    psum = jnp.where(g0rows, jnp.broadcast_to(s0, pn.shape), jnp.broadcast_to(s1, pn.shape))
    imp = _dot_hilo(psum, ov_ref[...].astype(BF16))
    cur = n_sel_blocks - 1
    forced = (lane == 0) | (lane == cur) | (lane == cur - 1)
    val = jnp.where(lane >= n_sel_blocks, -2.0, jnp.where(forced, FORCE, imp))
    rank = jnp.zeros_like(val)
    for i in range(n_sel_blocks):
        ci = jnp.broadcast_to(val[:, i:i + 1], val.shape)
        beats = (ci > val) | ((ci == val) & (lane > i))
        rank = rank + jnp.where(beats, 1.0, 0.0)
    sel = jnp.where((rank < float(N_SEL)) & (lane < n_sel_blocks), 1.0, 0.0)
    key_mask = _dot(sel.astype(BF16), e_ref[...])

    yield
    def tiles(ref_list, col):
        def tile(rf, i):
            rows = lambda c: rf[pl.ds(4 * LANES * i + c, LANES, stride=4), :]
            return lambda: jnp.concatenate([rows(col), rows(col + 1)], axis=1)
        return [tile(rf, i) for rf in ref_list for i in range(rf.shape[0] // (4 * LANES))]

    b_new = bnew_ref[:, 0:1]
    o_s = _decode_attend(qf, g0rows, gk_ref[1:2, :], tiles(pages_s, 0), tiles(pages_s, 2),
                         bs_ref[...], key_mask, ksn_ref[...], b_new)
    yield
    o_w = _decode_attend(qf, g0rows, gk_ref[2:3, :], tiles([win_ref], 0), tiles([win_ref], 2),
                         bw_ref[...], None, kwn_ref[...], b_new)
    yield
    gt = gate_ref[...]
    o_ref[...] = gt[:, 0:1] * o_c + gt[:, 1:2] * o_s + gt[:, 2:3] * o_w

    keep = wout_ref.shape[0] - 4
    drop = win_ref.shape[0] - keep
    wout_ref[0:keep, :] = win_ref[drop:drop + keep, :]
    wout_ref[keep:keep + 4, :] = kwn_ref[...]


def _nsa_sample(page_table, cache_c, cache_s, win, q, gates, ks_new, kw_new, wk_pairs, wv_pairs, pos_rows, gk,
                bias_c, bias_s, bias_w, bias_new, ov, expand):
    nb, n_pages = page_table.shape
    page_rows = cache_c.shape[1]
    n_sel_blocks = -(-(n_pages * (page_rows // 4) + 1) // SEL_LEN)
    group = SAMPLE_GROUP if nb % SAMPLE_GROUP == 0 else 1
    kern = functools.partial(_nsa_sample_kernel, n_pages=n_pages, n_sel_blocks=n_sel_blocks, group=group)
    win_rows_out = 4 * min(WINDOW, n_pages * (page_rows // 4) + 1)
    page = page_rows // 4
    assert n_pages % 2 == 0 and page % CMP_STRIDE == 0
    segs = page // CMP_STRIDE
    i_, n_, s_ = np.meshgrid(np.arange(2), np.arange(segs), np.arange(CMP_STRIDE), indexing="ij")
    perm_np = np.zeros((2 * page, 2 * page), np.float32)
    perm_np[(s_ * 2 * segs + i_ * segs + n_).ravel(), (i_ * page + CMP_STRIDE * n_ + s_).ravel()] = 1.0
    perm = jnp.asarray(perm_np, BF16)
    hbm = pl.BlockSpec(memory_space=pl.ANY)
    full = lambda a: pl.BlockSpec(a.shape, lambda bi, pt: (0,) * a.ndim)
    per_b = lambda a: pl.BlockSpec((group,) + a.shape[1:], lambda bi, pt: (bi,) + (0,) * (a.ndim - 1))
    page_buf = pltpu.VMEM((2, group, n_pages, page_rows, LANES), F32)
    grid_spec = pltpu.PrefetchScalarGridSpec(
        num_scalar_prefetch=1,
        grid=(nb // group,),
        scratch_shapes=[page_buf, page_buf, pltpu.SemaphoreType.DMA((2, 2))],
        in_specs=([hbm, hbm]
                  + [per_b(win), per_b(q), per_b(gates), per_b(ks_new), per_b(kw_new),
                     full(wk_pairs), full(wv_pairs), full(pos_rows), full(gk),
                     full(bias_c), full(bias_s), full(bias_w), full(bias_new), full(ov), full(expand), full(perm)]),
        out_specs=[pl.BlockSpec((group,) + q.shape[1:], lambda bi, pt: (bi, 0, 0)),
                   pl.BlockSpec((group, win_rows_out, LANES), lambda bi, pt: (bi, 0, 0))],
    )
    return pl.pallas_call(
        kern,
        grid_spec=grid_spec,
        out_shape=(jax.ShapeDtypeStruct(q.shape, F32), jax.ShapeDtypeStruct((nb, win_rows_out, LANES), F32)),
        compiler_params=_cparams(("arbitrary",)),
        name="nsa_sample",
    )(page_table, cache_c, cache_s, win, q, gates, ks_new, kw_new,
      wk_pairs, wv_pairs, pos_rows, gk, bias_c, bias_s, bias_w, bias_new, ov, expand, perm)


def _merge_kernel(h_ref, oa_ref, ob_ref, wga_ref, wgb_ref, wg_ref, wn_ref, mix_ref, *cast_refs):
    h = h_ref[...]
    wga = wga_ref[...].astype(BF16)
    wgb = wgb_ref[...].astype(BF16)
    wg = wg_ref[...].astype(BF16)
    wn = wn_ref[...].astype(BF16)
    ga = _sigmoid(_dot_nt(h, wga))
    gb = _sigmoid(_dot_nt(h, wgb))
    mix_ref[...] = (ga * _dot(oa_ref[...], wg) + gb * _dot(ob_ref[...], wn)).astype(BF16)
    for ref, val in zip(cast_refs, (wga, wgb, wg, wn)):
        ref[...] = val


def _merge(h, oa, ob, wga, wgb, wg, wn, gate_rows=None):
    n, d = h.shape
    tm = 512 if n % 512 == 0 else n
    tn = 512
    emit = gate_rows is not None
    assert not emit or n == tm
    row = lambda w: pl.BlockSpec((tm, w), lambda i, j: (i, 0))
    col = lambda k: pl.BlockSpec((k, tn), lambda i, j: (0, j))
    colt = pl.BlockSpec((tn, d), lambda i, j: (j, 0))
    out_specs = [pl.BlockSpec((tm, tn), lambda i, j: (i, j))]
    out_shape = [jax.ShapeDtypeStruct((n, d), BF16)]
    gate_specs = [colt, colt]
    if emit:
        assert all(r % 8 == 0 for r in gate_rows)
        gate_specs = [pl.BlockSpec((pl.Element(tn), pl.Element(d)),
                                   lambda i, j, r=r: ((r // 8 + j * (tn // 8)) * 8, 0)) for r in gate_rows]
        out_specs += [colt, colt, col(wg.shape[0]), col(wn.shape[0])]
        out_shape += [jax.ShapeDtypeStruct((d, d), BF16), jax.ShapeDtypeStruct((d, d), BF16),
                      jax.ShapeDtypeStruct(wg.shape, BF16), jax.ShapeDtypeStruct(wn.shape, BF16)]
    return pl.pallas_call(
        _merge_kernel,
        grid=(n // tm, d // tn),
        in_specs=[row(d), row(oa.shape[1]), row(ob.shape[1])] + gate_specs + [col(wg.shape[0]), col(wn.shape[0])],
        out_specs=out_specs,
        out_shape=out_shape,
        compiler_params=_cparams(("arbitrary", "arbitrary")),
        name="merge",
    )(h, oa, ob, wga, wgb, wg, wn)


def _outproj_kernel(x_ref, mix_ref, wout_ref, gmlp_ref, x1_ref, hm_ref):
    x1 = x_ref[...] + _dot(mix_ref[...], wout_ref[...])
    x1_ref[...] = x1
    hm_ref[...] = (_rms(x1) * gmlp_ref[...]).astype(BF16)


def _outproj(x, mix, wout, gmlp):
    n, d = x.shape
    tm = 512 if n % 512 == 0 else n
    row = pl.BlockSpec((tm, d), lambda i: (i, 0))
    return pl.pallas_call(
        _outproj_kernel,
        grid=(n // tm,),
        in_specs=[row, row, pl.BlockSpec(wout.shape, lambda i: (0, 0)), pl.BlockSpec(gmlp.shape, lambda i: (0, 0))],
        out_specs=[row, row],
        out_shape=(jax.ShapeDtypeStruct((n, d), F32), jax.ShapeDtypeStruct((n, d), BF16)),
        compiler_params=_cparams(("arbitrary",)),
        name="outproj",
    )(x, mix, wout, gmlp)


def _ffn_kernel(hm_ref, x1_ref, wup_ref, wdown_ref, y_ref, *cast_refs):
    f = pl.program_id(1)

    @pl.when(f == 0)
    def _():
        y_ref[...] = x1_ref[...]

    wup = wup_ref[...].astype(BF16)
    wdown = wdown_ref[...].astype(BF16)
    hid = jnp.maximum(_dot(hm_ref[...], wup), 0.0)
    y_ref[...] += _dot((hid * hid).astype(BF16), wdown)
    if cast_refs:
        cast_refs[0][...] = wup
        cast_refs[1][...] = wdown


def _ffn(hm, x1, wup, wdown):
    n, d = hm.shape
    dff = wup.shape[1]
    tm = 512 if n % 512 == 0 else n
    emit = wup.dtype != BF16
    assert not emit or n == tm
    tf = 512 if emit else 1024
    row = pl.BlockSpec((tm, d), lambda i, f: (i, 0))
    up_spec = pl.BlockSpec((d, tf), lambda i, f: (0, f))
    down_spec = pl.BlockSpec((tf, d), lambda i, f: (f, 0))
    out_specs = [row]
    out_shape = [jax.ShapeDtypeStruct((n, d), F32)]
    if emit:
        out_specs += [up_spec, down_spec]
        out_shape += [jax.ShapeDtypeStruct(wup.shape, BF16), jax.ShapeDtypeStruct(wdown.shape, BF16)]
    return pl.pallas_call(
        _ffn_kernel,
        grid=(n // tm, dff // tf),
        in_specs=[row, row, up_spec, down_spec],
        out_specs=out_specs,
        out_shape=out_shape,
        compiler_params=_cparams(("arbitrary", "arbitrary")),
        name="ffn",
    )(hm, x1, wup, wdown)


def _bucket(rel, valid):
    n = np.maximum(rel, 0)
    max_exact = N_BUCKETS // 2
    nf = np.maximum(n, 1).astype(np.float32)
    large = max_exact + (np.log(nf / np.float32(max_exact)) / np.float32(math.log(MAX_DIST / max_exact))
                         * np.float32(N_BUCKETS - max_exact)).astype(np.int32)
    large = np.minimum(large, N_BUCKETS - 1)
    return np.where(valid, np.where(n < max_exact, n, large), -1).astype(np.int32)


def _bias_tables_kernel(rb_ref, *refs, shifts):
    n = len(shifts)
    nbk, nh = rb_ref.shape
    for b_ref, o_ref, shift in zip(refs[:n], refs[n:], shifts):
        b = b_ref[...]
        rows = []
        for h in range(nh):
            sh = rb_ref[nbk - 1, h] if shift else 0.0
            acc = jnp.full(b.shape, NEG, F32)
            for k in range(nbk):
                acc = jnp.where(b == k, rb_ref[k, h] - sh, acc)
            if len(o_ref.shape) == 3:
                o_ref[h] = acc
            else:
                rows.append(acc)
        if rows:
            o_ref[...] = jnp.concatenate(rows, axis=0)


def _bias_tables(rel_bias, buckets, shifts):
    nh = rel_bias.shape[1]
    shapes = [jax.ShapeDtypeStruct((nh,) + (b.shape if b.shape[0] > 1 else b.shape[1:]), F32) for b in buckets]
    vm = pl.BlockSpec(memory_space=pltpu.VMEM)
    return pl.pallas_call(
        functools.partial(_bias_tables_kernel, shifts=tuple(shifts)),
        in_specs=[pl.BlockSpec(memory_space=pltpu.SMEM)] + [vm] * len(buckets),
        out_specs=[vm] * len(buckets),
        out_shape=shapes,
        name="bias_tables",
    )(rel_bias, *[jnp.asarray(b) for b in buckets])


def _overlap(nc, ns):
    i = np.arange(nc)[:, None] * CMP_STRIDE
    j = np.arange(ns)[None, :] * SEL_LEN
    return ((i < j + SEL_LEN) & (i + CMP_LEN > j)).astype(np.float32)


def _compress_weights(w, pos):
    s = np.arange(0, CMP_STRIDE, 2)
    top = jnp.concatenate([w[s], w[CMP_STRIDE + s]], axis=2)
    bot = jnp.concatenate([w[s + 1], w[CMP_STRIDE + s + 1]], axis=2)
    tiles = jnp.concatenate([top, bot], axis=1).astype(BF16)
    row_a = jnp.concatenate([pos[s], pos[s + 1]], axis=1)
    row_b = jnp.concatenate([pos[CMP_STRIDE + s], pos[CMP_STRIDE + s + 1]], axis=1)
    rows = jnp.zeros((len(s), 16, 2 * HEAD_DIM), F32).at[:, 0].set(row_a).at[:, 1].set(row_b)
    return tiles, rows.astype(BF16)


def _dense_tail(x, h, oa, ob, wga, wgb, wg, wn, wout, gmlp, wup, wdown, gate_rows=None):
    mix, *cast_m = _merge(h, oa, ob, wga, wgb, wg, wn, gate_rows)
    x1, hm = _outproj(x, mix, wout, gmlp)
    y, *cast_f = _ffn(hm, x1, wup, wdown)
    return y, (cast_m or [wga, wgb, wg, wn]) + (cast_f or [wup, wdown])


def kernel(x_prompt, x_sample, cache_cmp_kv, cache_sel_kv, state_win_kv, page_table, rel_bias, g_mix_norm, w_in,
           g_sgu, w_sgu, b_sgu, g_q, g_k, pos_cmp_k, w_cmp_k, pos_cmp_v, w_cmp_v, w_proj_gmlp, w_proj_nsa, w_out,
           g_mlp_norm, w_up, w_down):
    depth = g_mix_norm.shape[0]
    assert depth == 1
    l = 0
    bsz, seq, d = x_prompt.shape
    nb = x_sample.shape[0]
    assert x_sample.shape[1] == 1 and seq % KPAD == 0
    d_gm = g_sgu.shape[1]
    n_heads = rel_bias.shape[1]
    d_nsa = n_heads * HEAD_DIM
    kvw_cols = 2 * (n_heads // 4) * HEAD_DIM
    n_gate = 3 * n_heads
    c_q = 2 * d_gm
    c_kv = c_q + d_nsa
    c_gate = c_kv + 3 * kvw_cols
    c_ga = c_gate + n_gate
    c_gb = c_ga + d

    w = jnp.swapaxes(w_in[l], 0, 1)
    n_col_tiles = -(-c_ga // 1024)
    wout = w_out[l].astype(BF16)
    gmix = g_mix_norm[l][None]
    gsgu = g_sgu[l][None]
    gq = g_q[l][None]
    gk = g_k[l]
    gmlp = g_mlp_norm[l][None]
    wk_pairs, posk_rows = _compress_weights(w_cmp_k[l], pos_cmp_k[l])
    wv_pairs, posv_rows = _compress_weights(w_cmp_v[l], pos_cmp_v[l])
    pos_rows = jnp.stack([posk_rows, posv_rows])
    n_groups = w_sgu.shape[1]
    b_exp = jnp.repeat(b_sgu[l].T, d_gm // n_groups, axis=1)
    w00 = jnp.repeat(w_sgu[l][:, 0, 0], d_gm // n_groups)[None]
    b00 = jnp.repeat(b_sgu[l][:, 0], d_gm // n_groups)[None]

    n_pool, page = cache_cmp_kv.shape[1], cache_cmp_kv.shape[2]
    n_pages = page_table.shape[1]
    past = n_pages * page
    nwin = state_win_kv.shape[2]
    n_cmp_s = (past + 1 - CMP_LEN) // CMP_STRIDE + 1
    n_sel_s = -(-(past + 1) // SEL_LEN)
    nseg_s = past // CMP_STRIDE
    qi = np.arange(QBLOCK)[:, None]
    rel_s = qi + QBLOCK - np.arange(2 * QBLOCK)[None, :]
    rel_w = qi + WINDOW - np.arange(WINDOW + QBLOCK)[None, :]
    rel_c = qi + (KC_FRONT * CMP_STRIDE - CMP_LEN + 1) - CMP_STRIDE * np.arange(LANES)[None, :]
    srel_c = (past - (np.arange(nseg_s) * CMP_STRIDE + CMP_LEN - 1))[None]
    srel_s = (past - np.arange(past))[None]
    srel_w = (nwin - np.arange(nwin))[None]
    buckets = [
        _bucket(rel_c, rel_c >= 0), _bucket(rel_s, rel_s >= 0), _bucket(rel_w, (rel_w >= 0) & (rel_w < WINDOW)),
        _bucket(srel_c, (srel_c >= 0) & (np.arange(nseg_s)[None] < n_cmp_s)), _bucket(srel_s, srel_s >= 0),
        _bucket(srel_w, srel_w < WINDOW), np.zeros((1, LANES), np.int32)]
    tab_c, tab_s, tab_w, bias_c, bias_s, bias_w, bias_new = _bias_tables(
        rel_bias, buckets, [True, True, False, False, False, False, False])
    tab_c, tab_s, tab_w = [t.reshape(2, (n_heads // 2) * QBLOCK, t.shape[-1]) for t in (tab_c, tab_s, tab_w)]

    xs = x_sample.reshape(nb, d)
    h_s, v_s, oa_s, q_s, kvc_s, kvs_s, kvw_s, gates_s, w_cat = _inproj(
        xs, gmix, w, gsgu, gq, w00, b00, chunked=False, n_col_tiles=n_col_tiles)
    ov_s = np.zeros((nseg_s, LANES), np.float32)
    ov_s[:n_cmp_s, :n_sel_s] = _overlap(n_cmp_s, n_sel_s)
    expand = jnp.asarray(np.arange(LANES)[:, None] == (np.arange(past)[None, :] // SEL_LEN), BF16)
    gates_h = jnp.pad(gates_s[:, :n_gate].reshape(nb, n_heads, 3), ((0, 0), (0, 0), (0, LANES - 3)))
    lin = lambda a: a[l].reshape(a.shape[1], -1, HEAD_DIM)
    win_lin = lin(state_win_kv)
    ob_s, win_new = _nsa_sample(
        page_table, lin(cache_cmp_kv), lin(cache_sel_kv), win_lin,
        q_s.astype(F32).reshape(nb, n_heads, HEAD_DIM), gates_h,
        kvs_s.reshape(nb, 4, HEAD_DIM), kvw_s.reshape(nb, 4, HEAD_DIM), wk_pairs, wv_pairs, pos_rows, gk,
        bias_c, bias_s, bias_w, bias_new, jnp.asarray(ov_s), expand)
    y_s, (wga, wgb, wg, wn, wup, wdown) = _dense_tail(
        xs, h_s, oa_s, ob_s.reshape(nb, d_nsa).astype(BF16), w, w, w_proj_gmlp[l], w_proj_nsa[l], wout, gmlp,
        w_up[l], w_down[l], gate_rows=(c_ga, c_gb))

    xp = x_prompt.reshape(bsz * seq, d)
    h_p, v_p, oa_p, q_p, kvc_p, kvs_p, kvw_p, gates_p = _inproj(
        xp, gmix, w_cat, gsgu, gq, w_sgu[l], b_exp, chunked=True, n_col_tiles=n_col_tiles)
    kvs3 = kvs_p.reshape(bsz, 4 * seq, HEAD_DIM)
    kvw3 = kvw_p.reshape(bsz, 4 * seq, HEAD_DIM)
    kvc3 = kvc_p.reshape(bsz, 4 * seq, HEAD_DIM)
    ks, vs, kw, vw = _kvprep(kvs3, kvw3, gk)
    kc, vc = _compress_prompt(kvc3, wk_pairs, wv_pairs, pos_rows, gk)
    n_cmp = (seq - CMP_LEN) // CMP_STRIDE + 1
    n_selb = seq // SEL_LEN
    ov_p = np.zeros((KC_ROWS, LANES), np.float32)
    ov_p[KC_FRONT:KC_FRONT + n_cmp, :n_selb] = _overlap(n_cmp, n_selb)
    ov_p = jnp.asarray(ov_p)
    gates_g = gates_p[:, :n_gate].reshape(bsz, seq, 2, n_gate // 2).transpose(0, 2, 1, 3)
    gates_g = jnp.pad(gates_g, ((0, 0), (0, 0), (0, 0), (0, LANES - n_gate // 2)))
    ob_p = _nsa_prompt(q_p.reshape(bsz, seq, d_nsa), gates_g, kc, vc, ov_p, ks, vs, kw, vw, tab_c, tab_s, tab_w)
    y_p, _ = _dense_tail(xp, h_p, oa_p, ob_p.reshape(bsz * seq, d_nsa), wga, wgb, wg, wn, wout, gmlp, wup, wdown)

    n_kv = n_heads // 4
    kv6 = lambda a, b_, t_: a.reshape(1, b_, t_, 2, n_kv, HEAD_DIM)
    nw_p = min(WINDOW, seq)
    last = ((seq - 1) // CHUNK) * CHUNK
    nw_s = min(WINDOW, past + 1)
    return (y_p.reshape(bsz, seq, d), y_s.reshape(nb, 1, d),
            kv6(kvc3, bsz, seq), kv6(kvs3, bsz, seq), kv6(kvw3[:, 4 * (seq - nw_p):], bsz, nw_p),
            v_p.reshape(bsz, seq, d_gm)[:, last:][None],
            kv6(kvc_s, nb, 1), kv6(kvs_s, nb, 1), kv6(win_new, nb, nw_s),
            v_s.reshape(1, nb, 1, d_gm))
```

```python
import functools
import math

import numpy as np
import jax
import jax.numpy as jnp
from jax import lax
from jax.experimental import pallas as pl
from jax.experimental.pallas import tpu as pltpu

F32 = jnp.float32
BF16 = jnp.bfloat16

HEAD_DIM = 128
CHUNK = 128
CMP_LEN = 32
CMP_STRIDE = 16
SEL_LEN = 64
N_SEL = 16
WINDOW = 512
N_BUCKETS = 32
MAX_DIST = 128
QBLOCK = 128
EPS = 1e-6
NEG = -1e30
HALF_NEG = -5e29
FORCE = 1e6

LANES = 128
PAD_LANE = 64
KPAD = 512
KC_FRONT = 16
KC_ROWS = 376
SAMPLE_GROUP = 2
VMEM_LIMIT = 56 * 1024 * 1024


def _cparams(sem):
    return pltpu.CompilerParams(dimension_semantics=sem, vmem_limit_bytes=VMEM_LIMIT)


def _dot(a, b):
    return jnp.dot(a, b, preferred_element_type=F32)


def _dot_nt(a, b):
    return lax.dot_general(a, b, (((1,), (1,)), ((), ())), preferred_element_type=F32)


def _dot_hilo(a, b_bf16):
    hi = a.astype(BF16)
    lo = (a - hi.astype(F32)).astype(BF16)
    return _dot(hi, b_bf16) + _dot(lo, b_bf16)


def _rms(x):
    return x * lax.rsqrt(jnp.mean(x * x, axis=-1, keepdims=True) + EPS)


def _gelu(x):
    c = math.sqrt(2.0 / math.pi)
    return 0.5 * x * (1.0 + jnp.tanh(c * (x + 0.044715 * (x * x * x))))


def _sigmoid(x):
    return 1.0 / (1.0 + jnp.exp(-x))


def _inproj_kernel(x_ref, gmix_ref, w_ref, gsgu_ref, gq_ref, wsg_ref, bsg_ref,
                   h_ref, v_ref, oa_ref, q_ref, kvc_ref, kvs_ref, kvw_ref, gate_ref, *rest, chunked, tm):
    h_s, u_s, z_s = rest[-3:]
    j = pl.program_id(1)

    def matmul(slot):
        if len(rest) == 4:
            w = w_ref[...].astype(BF16)
            rest[0][...] = w
            z_s[slot] = _dot_nt(h_s[...], w)
        else:
            z_s[slot] = _dot_nt(h_s[...], w_ref[...])

    def epilogue_u(z):
        u_s[...] = _gelu(z)

    def epilogue_v(z):
        v = _rms(_gelu(z)) * gsgu_ref[...]
        v_ref[...] = v
        if chunked:
            row = lax.broadcasted_iota(jnp.int32, (CHUNK, CHUNK), 0)
            col = lax.broadcasted_iota(jnp.int32, (CHUNK, CHUNK), 1)
            n_groups = v.shape[1] // LANES
            for g in range(n_groups):
                wm = jnp.where(row >= col, wsg_ref[g], 0.0).astype(BF16)
                cs = slice(g * LANES, (g + 1) * LANES)
                for c in range(tm // CHUNK):
                    rs = slice(c * CHUNK, (c + 1) * CHUNK)
                    s = _dot(wm, v[rs, cs].astype(BF16)) + bsg_ref[:, cs]
                    oa_ref[rs, cs] = (u_s[rs, cs] * s).astype(BF16)
        else:
            oa_ref[...] = (u_s[...] * (v * wsg_ref[...] + bsg_ref[...])).astype(BF16)

    def epilogue_q(z):
        scale = HEAD_DIM ** -0.5
        for hd in range(z.shape[1] // HEAD_DIM):
            cs = slice(hd * HEAD_DIM, (hd + 1) * HEAD_DIM)
            q_ref[:, cs] = (_rms(z[:, cs]) * gq_ref[...] * scale).astype(BF16)

    def store_kv(ref, zz):
        for c in range(4):
            ref[pl.ds(c, tm, stride=4), :] = zz[:, c * LANES:(c + 1) * LANES]

    def epilogue_kv(z):
        store_kv(kvc_ref, z[:, :512])
        store_kv(kvs_ref, z[:, 512:])

    def epilogue_kvw(z):
        store_kv(kvw_ref, z[:, :512])
        gate_ref[...] = _sigmoid(z[:, 512:640])

    epilogues = (epilogue_u, epilogue_v, epilogue_q, epilogue_kv, epilogue_kvw)
    for k in range(len(epilogues) + 1):
        @pl.when(j == k)
        def _(k=k):
            if k == 0:
                hb = (_rms(x_ref[...]) * gmix_ref[...]).astype(BF16)
                h_s[...] = hb
                h_ref[...] = hb
            if k < len(epilogues):
                matmul(k % 2)
            if k > 0:
                epilogues[k - 1](z_s[(k - 1) % 2])


def _inproj(x, gmix, w_cat, gsgu, gq, wsg, bsg, *, chunked, n_col_tiles):
    n, d = x.shape
    tm = 512 if n % 512 == 0 else n
    tn = 1024
    emit = w_cat.dtype != BF16
    assert not emit or n == tm
    kern = functools.partial(_inproj_kernel, chunked=chunked, tm=tm)
    full = lambda a: pl.BlockSpec(a.shape, lambda i, j: (0,) * a.ndim)
    row = lambda w: pl.BlockSpec((tm, w), lambda i, j: (i, 0))
    kvrow = pl.BlockSpec((4 * tm, LANES), lambda i, j: (i, 0))
    out_shapes = (
        jax.ShapeDtypeStruct((n, d), BF16),
        jax.ShapeDtypeStruct((n, 1024), F32),
        jax.ShapeDtypeStruct((n, 1024), BF16),
        jax.ShapeDtypeStruct((n, 1024), BF16),
        jax.ShapeDtypeStruct((4 * n, LANES), F32),
        jax.ShapeDtypeStruct((4 * n, LANES), F32),
        jax.ShapeDtypeStruct((4 * n, LANES), F32),
        jax.ShapeDtypeStruct((n, LANES), F32),
    )
    assert n_col_tiles == 5
    wspec = pl.BlockSpec((tn, d), lambda i, j: (jnp.minimum(j, n_col_tiles - 1), 0))
    out_specs = [row(d), row(1024), row(1024), row(1024), kvrow, kvrow, kvrow, row(LANES)]
    if emit:
        out_shapes += (jax.ShapeDtypeStruct((n_col_tiles * tn, d), BF16),)
        out_specs.append(wspec)
    return pl.pallas_call(
        kern,
        grid=(n // tm, n_col_tiles + 1),
        in_specs=[row(d), full(gmix), wspec, full(gsgu), full(gq), full(wsg), full(bsg)],
        out_specs=out_specs,
        out_shape=out_shapes,
        scratch_shapes=[pltpu.VMEM((tm, d), BF16), pltpu.VMEM((tm, 1024), F32), pltpu.VMEM((2, tm, tn), F32)],
        compiler_params=_cparams(("arbitrary", "arbitrary")),
        name="inproj",
    )(x, gmix, w_cat, gsgu, gq, wsg, bsg)


def _kvprep_kernel(kvs_ref, kvw_ref, gk_ref, ks_ref, vs_ref, kw_ref, vw_ref):
    i = pl.program_id(1)
    rows = kvs_ref.shape[0] // 4
    col = lambda ref, c: ref[pl.ds(c, rows, stride=4), :]
    lane = lax.broadcasted_iota(jnp.int32, (rows, LANES), 1)
    row = lax.broadcasted_iota(jnp.int32, (rows, LANES), 0)

    @pl.when(i == 0)
    def _():
        aux = jnp.where(lane == PAD_LANE, 1.0, 0.0).astype(BF16)
        zk = jnp.zeros((rows, LANES), BF16)
        for g in range(2):
            ks_ref[g] = jnp.concatenate([zk, aux], axis=1)
            kw_ref[g] = jnp.concatenate([zk, aux], axis=1)
            vs_ref[g] = zk
            vw_ref[g] = zk

    @pl.when(i > 0)
    def _():
        blk = ((i - 1) * rows + row) // SEL_LEN
        onehot = jnp.where(lane == blk, 1.0, 0.0).astype(BF16)
        zaux = jnp.zeros((rows, LANES), BF16)
        for g in range(2):
            ks = (_rms(col(kvs_ref, g)) * gk_ref[1:2, :]).astype(BF16)
            kw = (_rms(col(kvw_ref, g)) * gk_ref[2:3, :]).astype(BF16)
            ks_ref[g] = jnp.concatenate([ks, onehot], axis=1)
            kw_ref[g] = jnp.concatenate([kw, zaux], axis=1)
            vs_ref[g] = col(kvs_ref, 2 + g).astype(BF16)
            vw_ref[g] = col(kvw_ref, 2 + g).astype(BF16)


def _kvprep(kvs, kvw, gk):
    b, t4, _ = kvs.shape
    t = t4 // 4
    rows = KPAD
    nblk = t // rows
    in_map = lambda bi, i: (bi, jnp.maximum(i - 1, 0), 0)
    out_map = lambda bi, i: (bi, 0, i, 0)
    kshape = jax.ShapeDtypeStruct((b, 2, KPAD + t, 2 * LANES), BF16)
    vshape = jax.ShapeDtypeStruct((b, 2, KPAD + t, LANES), BF16)
    return pl.pallas_call(
        _kvprep_kernel,
        grid=(b, nblk + 1),
        in_specs=[pl.BlockSpec((None, 4 * rows, LANES), in_map), pl.BlockSpec((None, 4 * rows, LANES), in_map),
                  pl.BlockSpec(gk.shape, lambda bi, i: (0, 0))],
        out_specs=[pl.BlockSpec((None, 2, rows, 2 * LANES), out_map), pl.BlockSpec((None, 2, rows, LANES), out_map),
                   pl.BlockSpec((None, 2, rows, 2 * LANES), out_map), pl.BlockSpec((None, 2, rows, LANES), out_map)],
        out_shape=(kshape, vshape, kshape, vshape),
        compiler_params=_cparams(("arbitrary", "arbitrary")),
        name="kvprep",
    )(kvs, kvw, gk)


def _compress_core(load_pair, nseg, wk_ref, wv_ref, pos_ref):
    w_refs = (wk_ref, wv_ref)
    accs = [jnp.zeros((2 * nseg + 16, 2 * LANES), F32) for _ in range(2)]
    for s2 in range(CMP_STRIDE // 2):
        for kv in range(2):
            parts = [load_pair(s2, 2 * kv + g) for g in range(2)]
            parts.append(pos_ref[kv, s2])
            lhs = jnp.concatenate(parts, axis=0)
            accs[kv] = accs[kv] + _dot(lhs, w_refs[kv][s2])
    outs = []
    for kv in range(2):
        y = accs[kv]
        post = y[2 * nseg:2 * nseg + 1, :LANES] + y[2 * nseg + 1:2 * nseg + 2, LANES:]
        per_g = []
        for g in range(2):
            y0 = y[g * nseg:(g + 1) * nseg, :LANES]
            y1 = y[g * nseg:(g + 1) * nseg, LANES:]
            per_g.append(y0 + pltpu.roll(y1, nseg - 1, 0) + post)
        outs.append(per_g)
    return outs


def _compress_prompt_kernel(x_ref, wk_ref, wv_ref, pos_ref, gk_ref, kc_ref, vc_ref):
    nseg = x_ref.shape[0] // (4 * CMP_STRIDE)
    load_x = lambda s, col: x_ref[pl.ds(4 * s + col, nseg, stride=4 * CMP_STRIDE), :]
    load_pair = lambda s2, col: jnp.concatenate([load_x(2 * s2, col), load_x(2 * s2 + 1, col)], axis=1).astype(BF16)
    ck, cv = _compress_core(load_pair, nseg, wk_ref, wv_ref, pos_ref)
    row = lax.broadcasted_iota(jnp.int32, (nseg, LANES), 0)
    lane = lax.broadcasted_iota(jnp.int32, (nseg, LANES), 1)
    valid = row < nseg - 1
    aux = jnp.where(valid, jnp.where(lane == row // 8, 1.0, 0.0), jnp.where(lane == PAD_LANE, 1.0, 0.0))
    back = KC_ROWS - KC_FRONT - nseg
    lane_f = lax.broadcasted_iota(jnp.int32, (KC_FRONT, LANES), 1)
    lane_b = lax.broadcasted_iota(jnp.int32, (back, LANES), 1)
    pad_f = jnp.concatenate([jnp.zeros((KC_FRONT, LANES), F32), jnp.where(lane_f == PAD_LANE, 1.0, 0.0)], axis=1)
    pad_b = jnp.concatenate([jnp.zeros((back, LANES), F32), jnp.where(lane_b == PAD_LANE, 1.0, 0.0)], axis=1)
    for g in range(2):
        kn = jnp.where(valid, _rms(ck[g]) * gk_ref[0:1, :], 0.0)
        kc_ref[g, 0:KC_FRONT, :] = pad_f
        kc_ref[g, KC_FRONT:KC_FRONT + nseg, :] = jnp.concatenate([kn, aux], axis=1)
        kc_ref[g, KC_FRONT + nseg:KC_ROWS, :] = pad_b
        vc_ref[g, 0:KC_FRONT, :] = jnp.zeros((KC_FRONT, LANES), F32)
        vc_ref[g, KC_FRONT:KC_FRONT + nseg, :] = jnp.where(valid, cv[g], 0.0)
        vc_ref[g, KC_FRONT + nseg:KC_ROWS, :] = jnp.zeros((back, LANES), F32)


def _compress_prompt(kvc, wk_pairs, wv_pairs, pos_rows, gk):
    b, t4, _ = kvc.shape
    full = lambda a: pl.BlockSpec(a.shape, lambda bi: (0,) * a.ndim)
    return pl.pallas_call(
        _compress_prompt_kernel,
        grid=(b,),
        in_specs=[pl.BlockSpec((None, t4, LANES), lambda bi: (bi, 0, 0)),
                  full(wk_pairs), full(wv_pairs), full(pos_rows), full(gk)],
        out_specs=[pl.BlockSpec((None, 2, KC_ROWS, 2 * LANES), lambda bi: (bi, 0, 0, 0)),
                   pl.BlockSpec((None, 2, KC_ROWS, LANES), lambda bi: (bi, 0, 0, 0))],
        out_shape=(jax.ShapeDtypeStruct((b, 2, KC_ROWS, 2 * LANES), F32),
                   jax.ShapeDtypeStruct((b, 2, KC_ROWS, LANES), F32)),
        compiler_params=_cparams(("arbitrary",)),
        name="compress_prompt",
    )(kvc, wk_pairs, wv_pairs, pos_rows, gk)


def _nsa_prompt_kernel(q_ref, gate_ref, kc_ref, vc_ref, ov_ref, ks_ref, vs_ref, kw_ref, vw_ref,
                       tc_ref, ts_ref, tw_ref, o_ref, m_s, l_s, acc_s, sa_s, sb_s, sw_s):
    qb = pl.program_id(2)
    t0 = qb * QBLOCK
    rq = 4 * QBLOCK
    q = q_ref[...]
    q4 = jnp.concatenate([q[:, r * LANES:(r + 1) * LANES] for r in range(4)], axis=0)
    lane = lax.broadcasted_iota(jnp.int32, (rq, LANES), 1)
    is_pad_lane = lane == PAD_LANE

    def q_aug(mb):
        return jnp.concatenate([q4, mb.astype(BF16)], axis=1)

    qa_pad = q_aug(jnp.where(is_pad_lane, NEG, 0.0))

    far_mask = ((lane < 32) & (lane >= qb - 2)) | is_pad_lane
    qa_cfar = q_aug(jnp.where(far_mask, NEG, 0.0))
    ncmp = 256
    near0 = pl.multiple_of(qb * 8, 8)
    k_far = kc_ref[KC_FRONT:KC_FRONT + ncmp, :].astype(BF16)
    k_near = kc_ref[pl.ds(near0, LANES), :].astype(BF16)
    s_far = _dot_nt(qa_cfar, k_far)
    s_near = _dot_nt(qa_pad, k_near) + tc_ref[...]
    w0 = pl.multiple_of(t0, QBLOCK)
    wlen = WINDOW + QBLOCK
    sw_s[...] = _dot_nt(qa_pad, kw_ref[pl.ds(w0, wlen), :]) + tw_ref[...]
    m = jnp.maximum(jnp.max(s_far, axis=-1, keepdims=True), jnp.max(s_near, axis=-1, keepdims=True))
    m = jnp.maximum(m, HALF_NEG)
    p_far = jnp.exp(s_far - m)
    p_near = jnp.exp(s_near - m)
    l = jnp.sum(p_far, axis=-1, keepdims=True) + jnp.sum(p_near, axis=-1, keepdims=True)
    linv = 1.0 / jnp.where(l > 0.0, l, 1.0)
    v_far = vc_ref[KC_FRONT:KC_FRONT + ncmp, :].astype(BF16)
    v_near = vc_ref[pl.ds(near0, LANES), :].astype(BF16)
    pb_far = p_far.astype(BF16)
    pb_near = p_near.astype(BF16)
    o_c = (_dot(pb_far, v_far) + _dot(pb_near, v_near)) * linv

    s = sw_s[...]
    m = jnp.max(s, axis=-1, keepdims=True)
    p = jnp.exp(s - m)
    l = jnp.sum(p, axis=-1, keepdims=True)
    o_w = _dot(p.astype(BF16), vw_ref[pl.ds(w0, wlen), :]) * (1.0 / l)

    ov_far = ov_ref[KC_FRONT:KC_FRONT + ncmp, :].astype(BF16)
    ov_near = ov_ref[pl.ds(near0, LANES), :].astype(BF16)
    imp4 = (_dot(pb_far, ov_far) + _dot(pb_near, ov_near)) * linv
    imp = sum(imp4[r * QBLOCK:(r + 1) * QBLOCK] for r in range(4))
    nblk = 64
    imp_t = imp.T[:nblk]
    blk = lax.broadcasted_iota(jnp.int32, (nblk, QBLOCK), 0)
    qpos = t0 + lax.broadcasted_iota(jnp.int32, (nblk, QBLOCK), 1)
    cur = qpos // SEL_LEN
    forced = (blk == 0) | (blk == cur) | (blk == cur - 1)
    eligible = blk * SEL_LEN <= qpos
    val = jnp.where(forced, FORCE, jnp.where(eligible, imp_t, -1.0))
    slab = 8
    vals = [val[v * slab:(v + 1) * slab] for v in range(nblk // slab)]
    ranks = [jnp.zeros((slab, QBLOCK), F32) for _ in vals]
    row_in_slab = lax.broadcasted_iota(jnp.int32, (slab, QBLOCK), 0)
    for i in range(nblk):
        vi, ii = divmod(i, slab)
        ri = jnp.broadcast_to(vals[vi][ii:ii + 1, :], (slab, QBLOCK))
        for v in range(len(vals)):
            if v < vi:
                beats = ri > vals[v]
            elif v > vi:
                beats = ri >= vals[v]
            else:
                beats = (ri > vals[v]) | ((ri == vals[v]) & (row_in_slab > ii))
            ranks[v] = ranks[v] + jnp.where(beats, 1.0, 0.0)
    rank = jnp.concatenate(ranks, axis=0)
    mb_t = jnp.where(rank < float(N_SEL), 0.0, NEG)
    row2 = lax.broadcasted_iota(jnp.int32, (LANES - nblk, QBLOCK), 0)
    mb_t = jnp.concatenate([mb_t, jnp.where(row2 == PAD_LANE - nblk, NEG, 0.0)], axis=0)
    mb = mb_t.T
    mb4 = jnp.concatenate([mb] * 4, axis=0)
    qa_snear = q_aug(mb4)
    qa_sfar = q_aug(jnp.where((lane < nblk) & (lane >= 2 * qb - 2), NEG, mb4))

    far_tile = 512
    n_far = (jnp.maximum(qb - 1, 0) * QBLOCK + far_tile - 1) // far_tile

    def far_logits(j):
        r0 = pl.multiple_of(KPAD + j * far_tile, far_tile)
        return _dot_nt(qa_sfar, ks_ref[pl.ds(r0, far_tile), :])

    sn0 = pl.multiple_of(t0 + KPAD - QBLOCK, QBLOCK)
    s = _dot_nt(qa_snear, ks_ref[pl.ds(sn0, 2 * QBLOCK), :]) + ts_ref[...]
    sa_s[...] = far_logits(0)
    m = jnp.max(s, axis=-1, keepdims=True)
    p = jnp.exp(s - m)
    l = jnp.sum(p, axis=-1, keepdims=True)
    acc = _dot(p.astype(BF16), vs_ref[pl.ds(sn0, 2 * QBLOCK), :])

    m_s[...] = m
    l_s[...] = l
    acc_s[...] = acc
    last_tile = ks_ref.shape[0] // far_tile - 2

    def far_update(j, s_ref):
        r0 = pl.multiple_of(KPAD + j * far_tile, far_tile)
        s = s_ref[...]
        m_old = m_s[...]
        m_new = jnp.maximum(m_old, jnp.max(s, axis=-1, keepdims=True))
        alpha = jnp.exp(m_old - m_new)
        p = jnp.exp(s - m_new)
        l_s[...] = alpha * l_s[...] + jnp.sum(p, axis=-1, keepdims=True)
        acc_s[...] = alpha * acc_s[...] + _dot(p.astype(BF16), vs_ref[pl.ds(r0, far_tile), :])
        m_s[...] = m_new

    def far_body(i, carry):
        a = 2 * i
        sb_s[...] = far_logits(a + 1)
        far_update(a, sa_s)
        sa_s[...] = far_logits(jnp.minimum(a + 2, last_tile))
        far_update(a + 1, sb_s)
        return carry

    lax.fori_loop(0, (n_far + 1) // 2, far_body, 0)
    o_s = acc_s[...] * (1.0 / l_s[...])

    gt = gate_ref[...]
    outs = []
    for r in range(4):
        rs = slice(r * QBLOCK, (r + 1) * QBLOCK)
        outs.append(gt[:, 3 * r:3 * r + 1] * o_c[rs] + gt[:, 3 * r + 1:3 * r + 2] * o_s[rs]
                    + gt[:, 3 * r + 2:3 * r + 3] * o_w[rs])
    o_ref[...] = jnp.concatenate(outs, axis=1).astype(BF16)


def _nsa_prompt(q, gates_g, kc, vc, ov, ks, vs, kw, vw, tab_c, tab_s, tab_w):
    b, t, _ = q.shape
    nqb = t // QBLOCK
    kvspec = lambda a: pl.BlockSpec((None, None) + a.shape[2:], lambda bi, g, i: (bi, g, 0, 0))
    tabspec = lambda a: pl.BlockSpec((None,) + a.shape[1:], lambda bi, g, i: (g, 0, 0))
    return pl.pallas_call(
        _nsa_prompt_kernel,
        grid=(b, 2, nqb),
        in_specs=[pl.BlockSpec((None, QBLOCK, 512), lambda bi, g, i: (bi, i, g)),
                  pl.BlockSpec((None, None, QBLOCK, LANES), lambda bi, g, i: (bi, g, i, 0)),
                  kvspec(kc), kvspec(vc), pl.BlockSpec(ov.shape, lambda bi, g, i: (0, 0)),
                  kvspec(ks), kvspec(vs), kvspec(kw), kvspec(vw),
                  tabspec(tab_c), tabspec(tab_s), tabspec(tab_w)],
        out_specs=pl.BlockSpec((None, QBLOCK, 512), lambda bi, g, i: (bi, i, g)),
        out_shape=jax.ShapeDtypeStruct((b, t, 1024), BF16),
        scratch_shapes=[pltpu.VMEM((4 * QBLOCK, 1), F32), pltpu.VMEM((4 * QBLOCK, 1), F32),
                        pltpu.VMEM((4 * QBLOCK, LANES), F32),
                        pltpu.VMEM((4 * QBLOCK, 512), F32), pltpu.VMEM((4 * QBLOCK, 512), F32),
                        pltpu.VMEM((4 * QBLOCK, WINDOW + QBLOCK), F32)],
        compiler_params=_cparams(("arbitrary", "arbitrary", "arbitrary")),
        name="nsa_prompt",
    )(q, gates_g, kc, vc, ov, ks, vs, kw, vw, tab_c, tab_s, tab_w)


def _decode_attend(qf, g0rows, gk_row, k_tiles, v_tiles, bias, key_mask, new_row, bias_new):
    qg = qf * gk_row
    zero = jnp.zeros_like(qg)
    qbd = jnp.concatenate([jnp.where(g0rows, qg, zero), jnp.where(g0rows, zero, qg)], axis=1).astype(BF16)
    one = jnp.ones_like(qg)
    ones_bd = jnp.concatenate([jnp.where(g0rows, one, zero), jnp.where(g0rows, zero, one)], axis=1).astype(BF16)
    s_parts, q_parts = [], []
    for kt in k_tiles:
        k = kt()
        s_parts.append(_dot_nt(qbd, k.astype(BF16)))
        q_parts.append(_dot_nt(ones_bd, (k * k).astype(BF16)))
    s = jnp.concatenate(s_parts, axis=1)
    ssq = jnp.concatenate(q_parts, axis=1)
    s = s * lax.rsqrt(ssq * (1.0 / HEAD_DIM) + EPS) + bias
    if key_mask is not None:
        s = jnp.where(key_mask > 0.5, s, NEG)
    bc = lambda c: jnp.broadcast_to(new_row[c:c + 1, :], qf.shape)
    k_new = jnp.where(g0rows, bc(0), bc(1))
    v_new = jnp.where(g0rows, bc(2), bc(3))
    s_new = jnp.sum(qg * _rms(k_new), axis=-1, keepdims=True) + bias_new
    m = jnp.maximum(jnp.max(s, axis=-1, keepdims=True), s_new)
    p = jnp.exp(s - m)
    p_new = jnp.exp(s_new - m)
    l = jnp.sum(p, axis=-1, keepdims=True) + p_new
    o2 = jnp.zeros((qf.shape[0], 2 * LANES), F32)
    for i, vt in enumerate(v_tiles):
        o2 = o2 + _dot(p[:, i * LANES:(i + 1) * LANES].astype(BF16), vt().astype(BF16))
    o = jnp.where(g0rows, o2[:, :LANES], o2[:, LANES:]) + p_new * v_new
    return o * (1.0 / l)


def _nsa_sample_kernel(pt_ref, cache_c_hbm, cache_s_hbm, *refs, n_pages, n_sel_blocks, group):
    per_b_in, consts, outs = refs[:5], refs[5:-5], refs[-5:-3]
    buf_c, buf_s, sem = refs[-3:]
    step = pl.program_id(0)
    slot = step % 2

    def page_copies(src_step, dst_slot, for_wait):
        copies = []
        for e in range(group):
            for p in range(n_pages):
                pg = 0 if for_wait else pt_ref[src_step * group + e, p]
                copies.append(pltpu.make_async_copy(cache_c_hbm.at[pg], buf_c.at[dst_slot, e, p], sem.at[dst_slot, 0]))
                copies.append(pltpu.make_async_copy(cache_s_hbm.at[pg], buf_s.at[dst_slot, e, p], sem.at[dst_slot, 1]))
        return copies

    @pl.when(step == 0)
    def _():
        for c in page_copies(0, 0, False):
            c.start()

    @pl.when(step + 1 < pl.num_programs(0))
    def _():
        for c in page_copies(step + 1, 1 - slot, False):
            c.start()

    for c in page_copies(step, slot, True):
        c.wait()

    chains = [_nsa_sample_one([buf_c.at[slot, bb, p] for p in range(n_pages)],
                              [buf_s.at[slot, bb, p] for p in range(n_pages)],
                              *[r.at[bb] for r in per_b_in], *consts, *[r.at[bb] for r in outs],
                              n_pages=n_pages, n_sel_blocks=n_sel_blocks) for bb in range(group)]
    while chains:
        chains = [c for c in chains if next(c, "done") != "done"]


def _nsa_sample_one(pages_c, pages_s, win_ref, q_ref, gate_ref, ksn_ref, kwn_ref, wk_ref, wv_ref, pos_ref, gk_ref,
                    bc_ref, bs_ref, bw_ref, bnew_ref, ov_ref, e_ref, perm_ref, o_ref, wout_ref, *,
                    n_pages, n_sel_blocks):
    qf = q_ref[...]
    nh = qf.shape[0]
    g0rows = lax.broadcasted_iota(jnp.int32, (nh, LANES), 0) < nh // 2
    lane = lax.broadcasted_iota(jnp.int32, (nh, LANES), 1)

    page = pages_c[0].shape[0] // 4
    nseg = n_pages * page // CMP_STRIDE
    perm = perm_ref[...]
    regrouped = []
    for pp in range(n_pages // 2):
        per_cp = []
        for cp in range(2):
            blk = jnp.concatenate(
                [jnp.concatenate([pages_c[2 * pp + i][pl.ds(2 * cp + c, page, stride=4), :] for c in range(2)], axis=1)
                 for i in range(2)], axis=0).astype(BF16)
            per_cp.append(_dot(perm, blk).astype(BF16))
        regrouped.append(per_cp)
    rows_per_tap = 2 * page // CMP_STRIDE
    yield

    def tap_rows(s, col):
        lo, hi = s * rows_per_tap, (s + 1) * rows_per_tap
        return jnp.concatenate([regrouped[pp][col // 2][lo:hi, (col % 2) * LANES:(col % 2 + 1) * LANES]
                                for pp in range(n_pages // 2)], axis=0)

    load_pair = lambda s2, col: jnp.concatenate([tap_rows(2 * s2, col), tap_rows(2 * s2 + 1, col)], axis=1)
    ck, cv = _compress_core(load_pair, nseg, wk_ref, wv_ref, pos_ref)
    yield
    kcn = jnp.concatenate([(_rms(ck[g]) * gk_ref[0:1, :]).astype(BF16) for g in range(2)], axis=0)
    s2 = _dot_nt(qf.astype(BF16), kcn)
    s = jnp.where(g0rows, s2[:, :nseg], s2[:, nseg:]) + bc_ref[...]
    m = jnp.max(s, axis=-1, keepdims=True)
    p = jnp.exp(s - m)
    linv = 1.0 / jnp.sum(p, axis=-1, keepdims=True)
    rowv = lax.broadcasted_iota(jnp.int32, (nseg, LANES), 0) < nseg - 1
    pb = p.astype(BF16)
    oc = [_dot(pb, jnp.where(rowv, cv[g], 0.0).astype(BF16)) for g in range(2)]
    o_c = jnp.where(g0rows, oc[0], oc[1]) * linv

    yield
    pn = p * linv
    s0 = jnp.sum(jnp.where(g0rows, pn, 0.0), axis=0, keepdims=True)
    s1 = jnp.sum(jnp.where(g0rows, 0.0, pn), axis=0, keepdims=True)
    psum = jnp.where(g0rows, jnp.broadcast_to(s0, pn.shape), jnp.broadcast_to(s1, pn.shape))
    imp = _dot_hilo(psum, ov_ref[...].astype(BF16))
    cur = n_sel_blocks - 1
    forced = (lane == 0) | (lane == cur) | (lane == cur - 1)
    val = jnp.where(lane >= n_sel_blocks, -2.0, jnp.where(forced, FORCE, imp))
    rank = jnp.zeros_like(val)
    for i in range(n_sel_blocks):
        ci = jnp.broadcast_to(val[:, i:i + 1], val.shape)
        beats = (ci > val) | ((ci == val) & (lane > i))
        rank = rank + jnp.where(beats, 1.0, 0.0)
    sel = jnp.where((rank < float(N_SEL)) & (lane < n_sel_blocks), 1.0, 0.0)
    key_mask = _dot(sel.astype(BF16), e_ref[...])

    yield
    def tiles(ref_list, col):
        def tile(rf, i):
            rows = lambda c: rf[pl.ds(4 * LANES * i + c, LANES, stride=4), :]
            return lambda: jnp.concatenate([rows(col), rows(col + 1)], axis=1)
        return [tile(rf, i) for rf in ref_list for i in range(rf.shape[0] // (4 * LANES))]

    b_new = bnew_ref[:, 0:1]
    o_s = _decode_attend(qf, g0rows, gk_ref[1:2, :], tiles(pages_s, 0), tiles(pages_s, 2),
                         bs_ref[...], key_mask, ksn_ref[...], b_new)
    yield
    o_w = _decode_attend(qf, g0rows, gk_ref[2:3, :], tiles([win_ref], 0), tiles([win_ref], 2),
                         bw_ref[...], None, kwn_ref[...], b_new)
    yield
    gt = gate_ref[...]
    o_ref[...] = gt[:, 0:1] * o_c + gt[:, 1:2] * o_s + gt[:, 2:3] * o_w

    keep = wout_ref.shape[0] - 4
    drop = win_ref.shape[0] - keep
    wout_ref[0:keep, :] = win_ref[drop:drop + keep, :]
    wout_ref[keep:keep + 4, :] = kwn_ref[...]


def _nsa_sample(page_table, cache_c, cache_s, win, q, gates, ks_new, kw_new, wk_pairs, wv_pairs, pos_rows, gk,
                bias_c, bias_s, bias_w, bias_new, ov, expand):
    nb, n_pages = page_table.shape
    page_rows = cache_c.shape[1]
    n_sel_blocks = -(-(n_pages * (page_rows // 4) + 1) // SEL_LEN)
    group = SAMPLE_GROUP if nb % SAMPLE_GROUP == 0 else 1
    kern = functools.partial(_nsa_sample_kernel, n_pages=n_pages, n_sel_blocks=n_sel_blocks, group=group)
    win_rows_out = 4 * min(WINDOW, n_pages * (page_rows // 4) + 1)
    page = page_rows // 4
    assert n_pages % 2 == 0 and page % CMP_STRIDE == 0
    segs = page // CMP_STRIDE
    i_, n_, s_ = np.meshgrid(np.arange(2), np.arange(segs), np.arange(CMP_STRIDE), indexing="ij")
    perm_np = np.zeros((2 * page, 2 * page), np.float32)
    perm_np[(s_ * 2 * segs + i_ * segs + n_).ravel(), (i_ * page + CMP_STRIDE * n_ + s_).ravel()] = 1.0
    perm = jnp.asarray(perm_np, BF16)
    hbm = pl.BlockSpec(memory_space=pl.ANY)
    full = lambda a: pl.BlockSpec(a.shape, lambda bi, pt: (0,) * a.ndim)
    per_b = lambda a: pl.BlockSpec((group,) + a.shape[1:], lambda bi, pt: (bi,) + (0,) * (a.ndim - 1))
    page_buf = pltpu.VMEM((2, group, n_pages, page_rows, LANES), F32)
    grid_spec = pltpu.PrefetchScalarGridSpec(
        num_scalar_prefetch=1,
        grid=(nb // group,),
        scratch_shapes=[page_buf, page_buf, pltpu.SemaphoreType.DMA((2, 2))],
        in_specs=([hbm, hbm]
                  + [per_b(win), per_b(q), per_b(gates), per_b(ks_new), per_b(kw_new),
                     full(wk_pairs), full(wv_pairs), full(pos_rows), full(gk),
                     full(bias_c), full(bias_s), full(bias_w), full(bias_new), full(ov), full(expand), full(perm)]),
        out_specs=[pl.BlockSpec((group,) + q.shape[1:], lambda bi, pt: (bi, 0, 0)),
                   pl.BlockSpec((group, win_rows_out, LANES), lambda bi, pt: (bi, 0, 0))],
    )
    return pl.pallas_call(
        kern,
        grid_spec=grid_spec,
        out_shape=(jax.ShapeDtypeStruct(q.shape, F32), jax.ShapeDtypeStruct((nb, win_rows_out, LANES), F32)),
        compiler_params=_cparams(("arbitrary",)),
        name="nsa_sample",
    )(page_table, cache_c, cache_s, win, q, gates, ks_new, kw_new,
      wk_pairs, wv_pairs, pos_rows, gk, bias_c, bias_s, bias_w, bias_new, ov, expand, perm)


def _merge_kernel(h_ref, oa_ref, ob_ref, wga_ref, wgb_ref, wg_ref, wn_ref, mix_ref, *cast_refs):
    h = h_ref[...]
    wga = wga_ref[...].astype(BF16)
    wgb = wgb_ref[...].astype(BF16)
    wg = wg_ref[...].astype(BF16)
    wn = wn_ref[...].astype(BF16)
    ga = _sigmoid(_dot_nt(h, wga))
    gb = _sigmoid(_dot_nt(h, wgb))
    mix_ref[...] = (ga * _dot(oa_ref[...], wg) + gb * _dot(ob_ref[...], wn)).astype(BF16)
    for ref, val in zip(cast_refs, (wga, wgb, wg, wn)):
        ref[...] = val


def _merge(h, oa, ob, wga, wgb, wg, wn, gate_rows=None):
    n, d = h.shape
    tm = 512 if n % 512 == 0 else n
    tn = 512
    emit = gate_rows is not None
    assert not emit or n == tm
    row = lambda w: pl.BlockSpec((tm, w), lambda i, j: (i, 0))
    col = lambda k: pl.BlockSpec((k, tn), lambda i, j: (0, j))
    colt = pl.BlockSpec((tn, d), lambda i, j: (j, 0))
    out_specs = [pl.BlockSpec((tm, tn), lambda i, j: (i, j))]
    out_shape = [jax.ShapeDtypeStruct((n, d), BF16)]
    gate_specs = [colt, colt]
    if emit:
        assert all(r % 8 == 0 for r in gate_rows)
        gate_specs = [pl.BlockSpec((pl.Element(tn), pl.Element(d)),
                                   lambda i, j, r=r: ((r // 8 + j * (tn // 8)) * 8, 0)) for r in gate_rows]
        out_specs += [colt, colt, col(wg.shape[0]), col(wn.shape[0])]
        out_shape += [jax.ShapeDtypeStruct((d, d), BF16), jax.ShapeDtypeStruct((d, d), BF16),
                      jax.ShapeDtypeStruct(wg.shape, BF16), jax.ShapeDtypeStruct(wn.shape, BF16)]
    return pl.pallas_call(
        _merge_kernel,
        grid=(n // tm, d // tn),
        in_specs=[row(d), row(oa.shape[1]), row(ob.shape[1])] + gate_specs + [col(wg.shape[0]), col(wn.shape[0])],
        out_specs=out_specs,
        out_shape=out_shape,
        compiler_params=_cparams(("arbitrary", "arbitrary")),
        name="merge",
    )(h, oa, ob, wga, wgb, wg, wn)


def _outproj_kernel(x_ref, mix_ref, wout_ref, gmlp_ref, x1_ref, hm_ref):
    x1 = x_ref[...] + _dot(mix_ref[...], wout_ref[...])
    x1_ref[...] = x1
    hm_ref[...] = (_rms(x1) * gmlp_ref[...]).astype(BF16)


def _outproj(x, mix, wout, gmlp):
    n, d = x.shape
    tm = 512 if n % 512 == 0 else n
    row = pl.BlockSpec((tm, d), lambda i: (i, 0))
    return pl.pallas_call(
        _outproj_kernel,
        grid=(n // tm,),
        in_specs=[row, row, pl.BlockSpec(wout.shape, lambda i: (0, 0)), pl.BlockSpec(gmlp.shape, lambda i: (0, 0))],
        out_specs=[row, row],
        out_shape=(jax.ShapeDtypeStruct((n, d), F32), jax.ShapeDtypeStruct((n, d), BF16)),
        compiler_params=_cparams(("arbitrary",)),
        name="outproj",
    )(x, mix, wout, gmlp)


def _ffn_kernel(hm_ref, x1_ref, wup_ref, wdown_ref, y_ref, *cast_refs):
    f = pl.program_id(1)

    @pl.when(f == 0)
    def _():
        y_ref[...] = x1_ref[...]

    wup = wup_ref[...].astype(BF16)
    wdown = wdown_ref[...].astype(BF16)
    hid = jnp.maximum(_dot(hm_ref[...], wup), 0.0)
    y_ref[...] += _dot((hid * hid).astype(BF16), wdown)
    if cast_refs:
        cast_refs[0][...] = wup
        cast_refs[1][...] = wdown


def _ffn(hm, x1, wup, wdown):
    n, d = hm.shape
    dff = wup.shape[1]
    tm = 512 if n % 512 == 0 else n
    emit = wup.dtype != BF16
    assert not emit or n == tm
    tf = 512 if emit else 1024
    row = pl.BlockSpec((tm, d), lambda i, f: (i, 0))
    up_spec = pl.BlockSpec((d, tf), lambda i, f: (0, f))
    down_spec = pl.BlockSpec((tf, d), lambda i, f: (f, 0))
    out_specs = [row]
    out_shape = [jax.ShapeDtypeStruct((n, d), F32)]
    if emit:
        out_specs += [up_spec, down_spec]
        out_shape += [jax.ShapeDtypeStruct(wup.shape, BF16), jax.ShapeDtypeStruct(wdown.shape, BF16)]
    return pl.pallas_call(
        _ffn_kernel,
        grid=(n // tm, dff // tf),
        in_specs=[row, row, up_spec, down_spec],
        out_specs=out_specs,
        out_shape=out_shape,
        compiler_params=_cparams(("arbitrary", "arbitrary")),
        name="ffn",
    )(hm, x1, wup, wdown)


def _bucket(rel, valid):
    n = np.maximum(rel, 0)
    max_exact = N_BUCKETS // 2
    nf = np.maximum(n, 1).astype(np.float32)
    large = max_exact + (np.log(nf / np.float32(max_exact)) / np.float32(math.log(MAX_DIST / max_exact))
                         * np.float32(N_BUCKETS - max_exact)).astype(np.int32)
    large = np.minimum(large, N_BUCKETS - 1)
    return np.where(valid, np.where(n < max_exact, n, large), -1).astype(np.int32)


def _bias_tables_kernel(rb_ref, *refs, shifts):
    n = len(shifts)
    nbk, nh = rb_ref.shape
    for b_ref, o_ref, shift in zip(refs[:n], refs[n:], shifts):
        b = b_ref[...]
        rows = []
        for h in range(nh):
            sh = rb_ref[nbk - 1, h] if shift else 0.0
            acc = jnp.full(b.shape, NEG, F32)
            for k in range(nbk):
                acc = jnp.where(b == k, rb_ref[k, h] - sh, acc)
            if len(o_ref.shape) == 3:
                o_ref[h] = acc
            else:
                rows.append(acc)
        if rows:
            o_ref[...] = jnp.concatenate(rows, axis=0)


def _bias_tables(rel_bias, buckets, shifts):
    nh = rel_bias.shape[1]
    shapes = [jax.ShapeDtypeStruct((nh,) + (b.shape if b.shape[0] > 1 else b.shape[1:]), F32) for b in buckets]
    vm = pl.BlockSpec(memory_space=pltpu.VMEM)
    return pl.pallas_call(
        functools.partial(_bias_tables_kernel, shifts=tuple(shifts)),
        in_specs=[pl.BlockSpec(memory_space=pltpu.SMEM)] + [vm] * len(buckets),
        out_specs=[vm] * len(buckets),
        out_shape=shapes,
        name="bias_tables",
    )(rel_bias, *[jnp.asarray(b) for b in buckets])


def _overlap(nc, ns):
    i = np.arange(nc)[:, None] * CMP_STRIDE
    j = np.arange(ns)[None, :] * SEL_LEN
    return ((i < j + SEL_LEN) & (i + CMP_LEN > j)).astype(np.float32)


def _compress_weights(w, pos):
    s = np.arange(0, CMP_STRIDE, 2)
    top = jnp.concatenate([w[s], w[CMP_STRIDE + s]], axis=2)
    bot = jnp.concatenate([w[s + 1], w[CMP_STRIDE + s + 1]], axis=2)
    tiles = jnp.concatenate([top, bot], axis=1).astype(BF16)
    row_a = jnp.concatenate([pos[s], pos[s + 1]], axis=1)
    row_b = jnp.concatenate([pos[CMP_STRIDE + s], pos[CMP_STRIDE + s + 1]], axis=1)
    rows = jnp.zeros((len(s), 16, 2 * HEAD_DIM), F32).at[:, 0].set(row_a).at[:, 1].set(row_b)
    return tiles, rows.astype(BF16)


def _dense_tail(x, h, oa, ob, wga, wgb, wg, wn, wout, gmlp, wup, wdown, gate_rows=None):
    mix, *cast_m = _merge(h, oa, ob, wga, wgb, wg, wn, gate_rows)
    x1, hm = _outproj(x, mix, wout, gmlp)
    y, *cast_f = _ffn(hm, x1, wup, wdown)
    return y, (cast_m or [wga, wgb, wg, wn]) + (cast_f or [wup, wdown])


def kernel(x_prompt, x_sample, cache_cmp_kv, cache_sel_kv, state_win_kv, page_table, rel_bias, g_mix_norm, w_in,
           g_sgu, w_sgu, b_sgu, g_q, g_k, pos_cmp_k, w_cmp_k, pos_cmp_v, w_cmp_v, w_proj_gmlp, w_proj_nsa, w_out,
           g_mlp_norm, w_up, w_down):
    depth = g_mix_norm.shape[0]
    assert depth == 1
    l = 0
    bsz, seq, d = x_prompt.shape
    nb = x_sample.shape[0]
    assert x_sample.shape[1] == 1 and seq % KPAD == 0
    d_gm = g_sgu.shape[1]
    n_heads = rel_bias.shape[1]
    d_nsa = n_heads * HEAD_DIM
    kvw_cols = 2 * (n_heads // 4) * HEAD_DIM
    n_gate = 3 * n_heads
    c_q = 2 * d_gm
    c_kv = c_q + d_nsa
    c_gate = c_kv + 3 * kvw_cols
    c_ga = c_gate + n_gate
    c_gb = c_ga + d

    w = jnp.swapaxes(w_in[l], 0, 1)
    n_col_tiles = -(-c_ga // 1024)
    wout = w_out[l].astype(BF16)
    gmix = g_mix_norm[l][None]
    gsgu = g_sgu[l][None]
    gq = g_q[l][None]
    gk = g_k[l]
    gmlp = g_mlp_norm[l][None]
    wk_pairs, posk_rows = _compress_weights(w_cmp_k[l], pos_cmp_k[l])
    wv_pairs, posv_rows = _compress_weights(w_cmp_v[l], pos_cmp_v[l])
    pos_rows = jnp.stack([posk_rows, posv_rows])
    n_groups = w_sgu.shape[1]
    b_exp = jnp.repeat(b_sgu[l].T, d_gm // n_groups, axis=1)
    w00 = jnp.repeat(w_sgu[l][:, 0, 0], d_gm // n_groups)[None]
    b00 = jnp.repeat(b_sgu[l][:, 0], d_gm // n_groups)[None]

    n_pool, page = cache_cmp_kv.shape[1], cache_cmp_kv.shape[2]
    n_pages = page_table.shape[1]
    past = n_pages * page
    nwin = state_win_kv.shape[2]
    n_cmp_s = (past + 1 - CMP_LEN) // CMP_STRIDE + 1
    n_sel_s = -(-(past + 1) // SEL_LEN)
    nseg_s = past // CMP_STRIDE
    qi = np.arange(QBLOCK)[:, None]
    rel_s = qi + QBLOCK - np.arange(2 * QBLOCK)[None, :]
    rel_w = qi + WINDOW - np.arange(WINDOW + QBLOCK)[None, :]
    rel_c = qi + (KC_FRONT * CMP_STRIDE - CMP_LEN + 1) - CMP_STRIDE * np.arange(LANES)[None, :]
    srel_c = (past - (np.arange(nseg_s) * CMP_STRIDE + CMP_LEN - 1))[None]
    srel_s = (past - np.arange(past))[None]
    srel_w = (nwin - np.arange(nwin))[None]
    buckets = [
        _bucket(rel_c, rel_c >= 0), _bucket(rel_s, rel_s >= 0), _bucket(rel_w, (rel_w >= 0) & (rel_w < WINDOW)),
        _bucket(srel_c, (srel_c >= 0) & (np.arange(nseg_s)[None] < n_cmp_s)), _bucket(srel_s, srel_s >= 0),
        _bucket(srel_w, srel_w < WINDOW), np.zeros((1, LANES), np.int32)]
    tab_c, tab_s, tab_w, bias_c, bias_s, bias_w, bias_new = _bias_tables(
        rel_bias, buckets, [True, True, False, False, False, False, False])
    tab_c, tab_s, tab_w = [t.reshape(2, (n_heads // 2) * QBLOCK, t.shape[-1]) for t in (tab_c, tab_s, tab_w)]

    xs = x_sample.reshape(nb, d)
    h_s, v_s, oa_s, q_s, kvc_s, kvs_s, kvw_s, gates_s, w_cat = _inproj(
        xs, gmix, w, gsgu, gq, w00, b00, chunked=False, n_col_tiles=n_col_tiles)
    ov_s = np.zeros((nseg_s, LANES), np.float32)
    ov_s[:n_cmp_s, :n_sel_s] = _overlap(n_cmp_s, n_sel_s)
    expand = jnp.asarray(np.arange(LANES)[:, None] == (np.arange(past)[None, :] // SEL_LEN), BF16)
    gates_h = jnp.pad(gates_s[:, :n_gate].reshape(nb, n_heads, 3), ((0, 0), (0, 0), (0, LANES - 3)))
    lin = lambda a: a[l].reshape(a.shape[1], -1, HEAD_DIM)
    win_lin = lin(state_win_kv)
    ob_s, win_new = _nsa_sample(
        page_table, lin(cache_cmp_kv), lin(cache_sel_kv), win_lin,
        q_s.astype(F32).reshape(nb, n_heads, HEAD_DIM), gates_h,
        kvs_s.reshape(nb, 4, HEAD_DIM), kvw_s.reshape(nb, 4, HEAD_DIM), wk_pairs, wv_pairs, pos_rows, gk,
        bias_c, bias_s, bias_w, bias_new, jnp.asarray(ov_s), expand)
    y_s, (wga, wgb, wg, wn, wup, wdown) = _dense_tail(
        xs, h_s, oa_s, ob_s.reshape(nb, d_nsa).astype(BF16), w, w, w_proj_gmlp[l], w_proj_nsa[l], wout, gmlp,
        w_up[l], w_down[l], gate_rows=(c_ga, c_gb))

    xp = x_prompt.reshape(bsz * seq, d)
    h_p, v_p, oa_p, q_p, kvc_p, kvs_p, kvw_p, gates_p = _inproj(
        xp, gmix, w_cat, gsgu, gq, w_sgu[l], b_exp, chunked=True, n_col_tiles=n_col_tiles)
    kvs3 = kvs_p.reshape(bsz, 4 * seq, HEAD_DIM)
    kvw3 = kvw_p.reshape(bsz, 4 * seq, HEAD_DIM)
    kvc3 = kvc_p.reshape(bsz, 4 * seq, HEAD_DIM)
    ks, vs, kw, vw = _kvprep(kvs3, kvw3, gk)
    kc, vc = _compress_prompt(kvc3, wk_pairs, wv_pairs, pos_rows, gk)
    n_cmp = (seq - CMP_LEN) // CMP_STRIDE + 1
    n_selb = seq // SEL_LEN
    ov_p = np.zeros((KC_ROWS, LANES), np.float32)
    ov_p[KC_FRONT:KC_FRONT + n_cmp, :n_selb] = _overlap(n_cmp, n_selb)
    ov_p = jnp.asarray(ov_p)
    gates_g = gates_p[:, :n_gate].reshape(bsz, seq, 2, n_gate // 2).transpose(0, 2, 1, 3)
    gates_g = jnp.pad(gates_g, ((0, 0), (0, 0), (0, 0), (0, LANES - n_gate // 2)))
    ob_p = _nsa_prompt(q_p.reshape(bsz, seq, d_nsa), gates_g, kc, vc, ov_p, ks, vs, kw, vw, tab_c, tab_s, tab_w)
    y_p, _ = _dense_tail(xp, h_p, oa_p, ob_p.reshape(bsz * seq, d_nsa), wga, wgb, wg, wn, wout, gmlp, wup, wdown)

    n_kv = n_heads // 4
    kv6 = lambda a, b_, t_: a.reshape(1, b_, t_, 2, n_kv, HEAD_DIM)
    nw_p = min(WINDOW, seq)
    last = ((seq - 1) // CHUNK) * CHUNK
    nw_s = min(WINDOW, past + 1)
    return (y_p.reshape(bsz, seq, d), y_s.reshape(nb, 1, d),
            kv6(kvc3, bsz, seq), kv6(kvs3, bsz, seq), kv6(kvw3[:, 4 * (seq - nw_p):], bsz, nw_p),
            v_p.reshape(bsz, seq, d_gm)[:, last:][None],
            kv6(kvc_s, nb, 1), kv6(kvs_s, nb, 1), kv6(win_new, nb, nw_s),
            v_s.reshape(1, nb, 1, d_gm))
```

```python
import functools
import math

import numpy as np
import jax
import jax.numpy as jnp
from jax import lax
from jax.experimental import pallas as pl
from jax.experimental.pallas import tpu as pltpu

F32 = jnp.float32
BF16 = jnp.bfloat16

HEAD_DIM = 128
CHUNK = 128
CMP_LEN = 32
CMP_STRIDE = 16
SEL_LEN = 64
N_SEL = 16
WINDOW = 512
N_BUCKETS = 32
MAX_DIST = 128
QBLOCK = 128
EPS = 1e-6
NEG = -1e30
HALF_NEG = -5e29
FORCE = 1e6

LANES = 128
PAD_LANE = 64
KPAD = 512
KC_FRONT = 16
KC_ROWS = 376
SAMPLE_GROUP = 2
VMEM_LIMIT = 56 * 1024 * 1024


def _cparams(sem):
    return pltpu.CompilerParams(dimension_semantics=sem, vmem_limit_bytes=VMEM_LIMIT)


def _dot(a, b):
    return jnp.dot(a, b, preferred_element_type=F32)


def _dot_nt(a, b):
    return lax.dot_general(a, b, (((1,), (1,)), ((), ())), preferred_element_type=F32)


def _dot_hilo(a, b_bf16):
    hi = a.astype(BF16)
    lo = (a - hi.astype(F32)).astype(BF16)
    return _dot(hi, b_bf16) + _dot(lo, b_bf16)


def _rms(x):
    return x * lax.rsqrt(jnp.mean(x * x, axis=-1, keepdims=True) + EPS)


def _gelu(x):
    c = math.sqrt(2.0 / math.pi)
    return 0.5 * x * (1.0 + jnp.tanh(c * (x + 0.044715 * (x * x * x))))


def _sigmoid(x):
    return 1.0 / (1.0 + jnp.exp(-x))


def _inproj_kernel(x_ref, gmix_ref, w_ref, gsgu_ref, gq_ref, wsg_ref, bsg_ref,
                   h_ref, v_ref, oa_ref, q_ref, kvc_ref, kvs_ref, kvw_ref, gate_ref, *rest, chunked, tm):
    h_s, u_s = rest[-2:]
    j = pl.program_id(1)

    @pl.when(j == 0)
    def _():
        hb = (_rms(x_ref[...]) * gmix_ref[...]).astype(BF16)
        h_s[...] = hb
        h_ref[...] = hb

    if len(rest) == 3:
        w = w_ref[...].astype(BF16)
        rest[0][...] = w
        z = _dot_nt(h_s[...], w)
    else:
        z = _dot_nt(h_s[...], w_ref[...])

    @pl.when(j == 0)
    def _():
        u_s[...] = _gelu(z)

    @pl.when(j == 1)
    def _():
        v = _rms(_gelu(z)) * gsgu_ref[...]
        v_ref[...] = v
        if chunked:
            row = lax.broadcasted_iota(jnp.int32, (CHUNK, CHUNK), 0)
            col = lax.broadcasted_iota(jnp.int32, (CHUNK, CHUNK), 1)
            n_groups = v.shape[1] // LANES
            for g in range(n_groups):
                wm = jnp.where(row >= col, wsg_ref[g], 0.0).astype(BF16)
                cs = slice(g * LANES, (g + 1) * LANES)
                for c in range(tm // CHUNK):
                    rs = slice(c * CHUNK, (c + 1) * CHUNK)
                    s = _dot(wm, v[rs, cs].astype(BF16)) + bsg_ref[:, cs]
                    oa_ref[rs, cs] = (u_s[rs, cs] * s).astype(BF16)
        else:
            oa_ref[...] = (u_s[...] * (v * wsg_ref[...] + bsg_ref[...])).astype(BF16)

    @pl.when(j == 2)
    def _():
        scale = HEAD_DIM ** -0.5
        for hd in range(z.shape[1] // HEAD_DIM):
            cs = slice(hd * HEAD_DIM, (hd + 1) * HEAD_DIM)
            q_ref[:, cs] = (_rms(z[:, cs]) * gq_ref[...] * scale).astype(BF16)

    def store_kv(ref, zz):
        for c in range(4):
            ref[pl.ds(c, tm, stride=4), :] = zz[:, c * LANES:(c + 1) * LANES]

    @pl.when(j == 3)
    def _():
        store_kv(kvc_ref, z[:, :512])
        store_kv(kvs_ref, z[:, 512:])

    @pl.when(j == 4)
    def _():
        store_kv(kvw_ref, z[:, :512])
        gate_ref[...] = _sigmoid(z[:, 512:640])


def _inproj(x, gmix, w_cat, gsgu, gq, wsg, bsg, *, chunked, n_col_tiles):
    n, d = x.shape
    tm = 512 if n % 512 == 0 else n
    tn = 1024
    emit = w_cat.dtype != BF16
    assert not emit or n == tm
    kern = functools.partial(_inproj_kernel, chunked=chunked, tm=tm)
    full = lambda a: pl.BlockSpec(a.shape, lambda i, j: (0,) * a.ndim)
    row = lambda w: pl.BlockSpec((tm, w), lambda i, j: (i, 0))
    kvrow = pl.BlockSpec((4 * tm, LANES), lambda i, j: (i, 0))
    out_shapes = (
        jax.ShapeDtypeStruct((n, d), BF16),
        jax.ShapeDtypeStruct((n, 1024), F32),
        jax.ShapeDtypeStruct((n, 1024), BF16),
        jax.ShapeDtypeStruct((n, 1024), BF16),
        jax.ShapeDtypeStruct((4 * n, LANES), F32),
        jax.ShapeDtypeStruct((4 * n, LANES), F32),
        jax.ShapeDtypeStruct((4 * n, LANES), F32),
        jax.ShapeDtypeStruct((n, LANES), F32),
    )
    wspec = pl.BlockSpec((tn, d), lambda i, j: (j, 0))
    out_specs = [row(d), row(1024), row(1024), row(1024), kvrow, kvrow, kvrow, row(LANES)]
    if emit:
        out_shapes += (jax.ShapeDtypeStruct((n_col_tiles * tn, d), BF16),)
        out_specs.append(wspec)
    return pl.pallas_call(
        kern,
        grid=(n // tm, n_col_tiles),
        in_specs=[row(d), full(gmix), wspec, full(gsgu), full(gq), full(wsg), full(bsg)],
        out_specs=out_specs,
        out_shape=out_shapes,
        scratch_shapes=[pltpu.VMEM((tm, d), BF16), pltpu.VMEM((tm, 1024), F32)],
        compiler_params=_cparams(("arbitrary", "arbitrary")),
        name="inproj",
    )(x, gmix, w_cat, gsgu, gq, wsg, bsg)


def _kvprep_kernel(kvs_ref, kvw_ref, gk_ref, ks_ref, vs_ref, kw_ref, vw_ref):
    i = pl.program_id(1)
    rows = kvs_ref.shape[0] // 4
    col = lambda ref, c: ref[pl.ds(c, rows, stride=4), :]
    lane = lax.broadcasted_iota(jnp.int32, (rows, LANES), 1)
    row = lax.broadcasted_iota(jnp.int32, (rows, LANES), 0)

    @pl.when(i == 0)
    def _():
        aux = jnp.where(lane == PAD_LANE, 1.0, 0.0).astype(BF16)
        zk = jnp.zeros((rows, LANES), BF16)
        for g in range(2):
            ks_ref[g] = jnp.concatenate([zk, aux], axis=1)
            kw_ref[g] = jnp.concatenate([zk, aux], axis=1)
            vs_ref[g] = zk
            vw_ref[g] = zk

    @pl.when(i > 0)
    def _():
        blk = ((i - 1) * rows + row) // SEL_LEN
        onehot = jnp.where(lane == blk, 1.0, 0.0).astype(BF16)
        zaux = jnp.zeros((rows, LANES), BF16)
        for g in range(2):
            ks = (_rms(col(kvs_ref, g)) * gk_ref[1:2, :]).astype(BF16)
            kw = (_rms(col(kvw_ref, g)) * gk_ref[2:3, :]).astype(BF16)
            ks_ref[g] = jnp.concatenate([ks, onehot], axis=1)
            kw_ref[g] = jnp.concatenate([kw, zaux], axis=1)
            vs_ref[g] = col(kvs_ref, 2 + g).astype(BF16)
            vw_ref[g] = col(kvw_ref, 2 + g).astype(BF16)


def _kvprep(kvs, kvw, gk):
    b, t4, _ = kvs.shape
    t = t4 // 4
    rows = KPAD
    nblk = t // rows
    in_map = lambda bi, i: (bi, jnp.maximum(i - 1, 0), 0)
    out_map = lambda bi, i: (bi, 0, i, 0)
    kshape = jax.ShapeDtypeStruct((b, 2, KPAD + t, 2 * LANES), BF16)
    vshape = jax.ShapeDtypeStruct((b, 2, KPAD + t, LANES), BF16)
    return pl.pallas_call(
        _kvprep_kernel,
        grid=(b, nblk + 1),
        in_specs=[pl.BlockSpec((None, 4 * rows, LANES), in_map), pl.BlockSpec((None, 4 * rows, LANES), in_map),
                  pl.BlockSpec(gk.shape, lambda bi, i: (0, 0))],
        out_specs=[pl.BlockSpec((None, 2, rows, 2 * LANES), out_map), pl.BlockSpec((None, 2, rows, LANES), out_map),
                   pl.BlockSpec((None, 2, rows, 2 * LANES), out_map), pl.BlockSpec((None, 2, rows, LANES), out_map)],
        out_shape=(kshape, vshape, kshape, vshape),
        compiler_params=_cparams(("arbitrary", "arbitrary")),
        name="kvprep",
    )(kvs, kvw, gk)


def _compress_core(load_pair, nseg, wk_ref, wv_ref, pos_ref):
    w_refs = (wk_ref, wv_ref)
    accs = [jnp.zeros((2 * nseg + 16, 2 * LANES), F32) for _ in range(2)]
    for s2 in range(CMP_STRIDE // 2):
        for kv in range(2):
            parts = [load_pair(s2, 2 * kv + g) for g in range(2)]
            parts.append(pos_ref[kv, s2])
            lhs = jnp.concatenate(parts, axis=0)
            accs[kv] = accs[kv] + _dot(lhs, w_refs[kv][s2])
    outs = []
    for kv in range(2):
        y = accs[kv]
        post = y[2 * nseg:2 * nseg + 1, :LANES] + y[2 * nseg + 1:2 * nseg + 2, LANES:]
        per_g = []
        for g in range(2):
            y0 = y[g * nseg:(g + 1) * nseg, :LANES]
            y1 = y[g * nseg:(g + 1) * nseg, LANES:]
            per_g.append(y0 + pltpu.roll(y1, nseg - 1, 0) + post)
        outs.append(per_g)
    return outs


def _compress_prompt_kernel(x_ref, wk_ref, wv_ref, pos_ref, gk_ref, kc_ref, vc_ref):
    nseg = x_ref.shape[0] // (4 * CMP_STRIDE)
    load_x = lambda s, col: x_ref[pl.ds(4 * s + col, nseg, stride=4 * CMP_STRIDE), :]
    load_pair = lambda s2, col: jnp.concatenate([load_x(2 * s2, col), load_x(2 * s2 + 1, col)], axis=1).astype(BF16)
    ck, cv = _compress_core(load_pair, nseg, wk_ref, wv_ref, pos_ref)
    row = lax.broadcasted_iota(jnp.int32, (nseg, LANES), 0)
    lane = lax.broadcasted_iota(jnp.int32, (nseg, LANES), 1)
    valid = row < nseg - 1
    aux = jnp.where(valid, jnp.where(lane == row // 8, 1.0, 0.0), jnp.where(lane == PAD_LANE, 1.0, 0.0))
    back = KC_ROWS - KC_FRONT - nseg
    lane_f = lax.broadcasted_iota(jnp.int32, (KC_FRONT, LANES), 1)
    lane_b = lax.broadcasted_iota(jnp.int32, (back, LANES), 1)
    pad_f = jnp.concatenate([jnp.zeros((KC_FRONT, LANES), F32), jnp.where(lane_f == PAD_LANE, 1.0, 0.0)], axis=1)
    pad_b = jnp.concatenate([jnp.zeros((back, LANES), F32), jnp.where(lane_b == PAD_LANE, 1.0, 0.0)], axis=1)
    for g in range(2):
        kn = jnp.where(valid, _rms(ck[g]) * gk_ref[0:1, :], 0.0)
        kc_ref[g, 0:KC_FRONT, :] = pad_f
        kc_ref[g, KC_FRONT:KC_FRONT + nseg, :] = jnp.concatenate([kn, aux], axis=1)
        kc_ref[g, KC_FRONT + nseg:KC_ROWS, :] = pad_b
        vc_ref[g, 0:KC_FRONT, :] = jnp.zeros((KC_FRONT, LANES), F32)
        vc_ref[g, KC_FRONT:KC_FRONT + nseg, :] = jnp.where(valid, cv[g], 0.0)
        vc_ref[g, KC_FRONT + nseg:KC_ROWS, :] = jnp.zeros((back, LANES), F32)


def _compress_prompt(kvc, wk_pairs, wv_pairs, pos_rows, gk):
    b, t4, _ = kvc.shape
    full = lambda a: pl.BlockSpec(a.shape, lambda bi: (0,) * a.ndim)
    return pl.pallas_call(
        _compress_prompt_kernel,
        grid=(b,),
        in_specs=[pl.BlockSpec((None, t4, LANES), lambda bi: (bi, 0, 0)),
                  full(wk_pairs), full(wv_pairs), full(pos_rows), full(gk)],
        out_specs=[pl.BlockSpec((None, 2, KC_ROWS, 2 * LANES), lambda bi: (bi, 0, 0, 0)),
                   pl.BlockSpec((None, 2, KC_ROWS, LANES), lambda bi: (bi, 0, 0, 0))],
        out_shape=(jax.ShapeDtypeStruct((b, 2, KC_ROWS, 2 * LANES), F32),
                   jax.ShapeDtypeStruct((b, 2, KC_ROWS, LANES), F32)),
        compiler_params=_cparams(("arbitrary",)),
        name="compress_prompt",
    )(kvc, wk_pairs, wv_pairs, pos_rows, gk)


def _nsa_prompt_kernel(q_ref, gate_ref, kc_ref, vc_ref, ov_ref, ks_ref, vs_ref, kw_ref, vw_ref,
                       tc_ref, ts_ref, tw_ref, o_ref, m_s, l_s, acc_s, sa_s, sb_s, sw_s):
    qb = pl.program_id(2)
    t0 = qb * QBLOCK
    rq = 4 * QBLOCK
    q = q_ref[...]
    q4 = jnp.concatenate([q[:, r * LANES:(r + 1) * LANES] for r in range(4)], axis=0)
    lane = lax.broadcasted_iota(jnp.int32, (rq, LANES), 1)
    is_pad_lane = lane == PAD_LANE

    def q_aug(mb):
        return jnp.concatenate([q4, mb.astype(BF16)], axis=1)

    qa_pad = q_aug(jnp.where(is_pad_lane, NEG, 0.0))

    far_mask = ((lane < 32) & (lane >= qb - 2)) | is_pad_lane
    qa_cfar = q_aug(jnp.where(far_mask, NEG, 0.0))
    ncmp = 256
    near0 = pl.multiple_of(qb * 8, 8)
    k_far = kc_ref[KC_FRONT:KC_FRONT + ncmp, :].astype(BF16)
    k_near = kc_ref[pl.ds(near0, LANES), :].astype(BF16)
    s_far = _dot_nt(qa_cfar, k_far)
    s_near = _dot_nt(qa_pad, k_near) + tc_ref[...]
    w0 = pl.multiple_of(t0, QBLOCK)
    wlen = WINDOW + QBLOCK
    sw_s[...] = _dot_nt(qa_pad, kw_ref[pl.ds(w0, wlen), :]) + tw_ref[...]
    m = jnp.maximum(jnp.max(s_far, axis=-1, keepdims=True), jnp.max(s_near, axis=-1, keepdims=True))
    m = jnp.maximum(m, HALF_NEG)
    p_far = jnp.exp(s_far - m)
    p_near = jnp.exp(s_near - m)
    l = jnp.sum(p_far, axis=-1, keepdims=True) + jnp.sum(p_near, axis=-1, keepdims=True)
    linv = 1.0 / jnp.where(l > 0.0, l, 1.0)
    v_far = vc_ref[KC_FRONT:KC_FRONT + ncmp, :].astype(BF16)
    v_near = vc_ref[pl.ds(near0, LANES), :].astype(BF16)
    pb_far = p_far.astype(BF16)
    pb_near = p_near.astype(BF16)
    o_c = (_dot(pb_far, v_far) + _dot(pb_near, v_near)) * linv

    s = sw_s[...]
    m = jnp.max(s, axis=-1, keepdims=True)
    p = jnp.exp(s - m)
    l = jnp.sum(p, axis=-1, keepdims=True)
    o_w = _dot(p.astype(BF16), vw_ref[pl.ds(w0, wlen), :]) * (1.0 / l)

    ov_far = ov_ref[KC_FRONT:KC_FRONT + ncmp, :].astype(BF16)
    ov_near = ov_ref[pl.ds(near0, LANES), :].astype(BF16)
    imp4 = (_dot(pb_far, ov_far) + _dot(pb_near, ov_near)) * linv
    imp = sum(imp4[r * QBLOCK:(r + 1) * QBLOCK] for r in range(4))
    nblk = 64
    imp_t = imp.T[:nblk]
    blk = lax.broadcasted_iota(jnp.int32, (nblk, QBLOCK), 0)
    qpos = t0 + lax.broadcasted_iota(jnp.int32, (nblk, QBLOCK), 1)
    cur = qpos // SEL_LEN
    forced = (blk == 0) | (blk == cur) | (blk == cur - 1)
    eligible = blk * SEL_LEN <= qpos
    val = jnp.where(forced, FORCE, jnp.where(eligible, imp_t, -1.0))
    slab = 8
    vals = [val[v * slab:(v + 1) * slab] for v in range(nblk // slab)]
    ranks = [jnp.zeros((slab, QBLOCK), F32) for _ in vals]
    row_in_slab = lax.broadcasted_iota(jnp.int32, (slab, QBLOCK), 0)
    for i in range(nblk):
        vi, ii = divmod(i, slab)
        ri = jnp.broadcast_to(vals[vi][ii:ii + 1, :], (slab, QBLOCK))
        for v in range(len(vals)):
            if v < vi:
                beats = ri > vals[v]
            elif v > vi:
                beats = ri >= vals[v]
            else:
                beats = (ri > vals[v]) | ((ri == vals[v]) & (row_in_slab > ii))
            ranks[v] = ranks[v] + jnp.where(beats, 1.0, 0.0)
    rank = jnp.concatenate(ranks, axis=0)
    mb_t = jnp.where(rank < float(N_SEL), 0.0, NEG)
    row2 = lax.broadcasted_iota(jnp.int32, (LANES - nblk, QBLOCK), 0)
    mb_t = jnp.concatenate([mb_t, jnp.where(row2 == PAD_LANE - nblk, NEG, 0.0)], axis=0)
    mb = mb_t.T
    mb4 = jnp.concatenate([mb] * 4, axis=0)
    qa_snear = q_aug(mb4)
    qa_sfar = q_aug(jnp.where((lane < nblk) & (lane >= 2 * qb - 2), NEG, mb4))

    far_tile = 512
    n_far = (jnp.maximum(qb - 1, 0) * QBLOCK + far_tile - 1) // far_tile

    def far_logits(j):
        r0 = pl.multiple_of(KPAD + j * far_tile, far_tile)
        return _dot_nt(qa_sfar, ks_ref[pl.ds(r0, far_tile), :])

    sn0 = pl.multiple_of(t0 + KPAD - QBLOCK, QBLOCK)
    s = _dot_nt(qa_snear, ks_ref[pl.ds(sn0, 2 * QBLOCK), :]) + ts_ref[...]
    sa_s[...] = far_logits(0)
    m = jnp.max(s, axis=-1, keepdims=True)
    p = jnp.exp(s - m)
    l = jnp.sum(p, axis=-1, keepdims=True)
    acc = _dot(p.astype(BF16), vs_ref[pl.ds(sn0, 2 * QBLOCK), :])

    m_s[...] = m
    l_s[...] = l
    acc_s[...] = acc
    last_tile = ks_ref.shape[0] // far_tile - 2

    def far_update(j, s_ref):
        r0 = pl.multiple_of(KPAD + j * far_tile, far_tile)
        s = s_ref[...]
        m_old = m_s[...]
        m_new = jnp.maximum(m_old, jnp.max(s, axis=-1, keepdims=True))
        alpha = jnp.exp(m_old - m_new)
        p = jnp.exp(s - m_new)
        l_s[...] = alpha * l_s[...] + jnp.sum(p, axis=-1, keepdims=True)
        acc_s[...] = alpha * acc_s[...] + _dot(p.astype(BF16), vs_ref[pl.ds(r0, far_tile), :])
        m_s[...] = m_new

    def far_body(i, carry):
        a = 2 * i
        sb_s[...] = far_logits(a + 1)
        far_update(a, sa_s)
        sa_s[...] = far_logits(jnp.minimum(a + 2, last_tile))
        far_update(a + 1, sb_s)
        return carry

    lax.fori_loop(0, (n_far + 1) // 2, far_body, 0)
    o_s = acc_s[...] * (1.0 / l_s[...])

    gt = gate_ref[...]
    outs = []
    for r in range(4):
        rs = slice(r * QBLOCK, (r + 1) * QBLOCK)
        outs.append(gt[:, 3 * r:3 * r + 1] * o_c[rs] + gt[:, 3 * r + 1:3 * r + 2] * o_s[rs]
                    + gt[:, 3 * r + 2:3 * r + 3] * o_w[rs])
    o_ref[...] = jnp.concatenate(outs, axis=1).astype(BF16)


def _nsa_prompt(q, gates_g, kc, vc, ov, ks, vs, kw, vw, tab_c, tab_s, tab_w):
    b, t, _ = q.shape
    nqb = t // QBLOCK
    kvspec = lambda a: pl.BlockSpec((None, None) + a.shape[2:], lambda bi, g, i: (bi, g, 0, 0))
    tabspec = lambda a: pl.BlockSpec((None,) + a.shape[1:], lambda bi, g, i: (g, 0, 0))
    return pl.pallas_call(
        _nsa_prompt_kernel,
        grid=(b, 2, nqb),
        in_specs=[pl.BlockSpec((None, QBLOCK, 512), lambda bi, g, i: (bi, i, g)),
                  pl.BlockSpec((None, None, QBLOCK, LANES), lambda bi, g, i: (bi, g, i, 0)),
                  kvspec(kc), kvspec(vc), pl.BlockSpec(ov.shape, lambda bi, g, i: (0, 0)),
                  kvspec(ks), kvspec(vs), kvspec(kw), kvspec(vw),
                  tabspec(tab_c), tabspec(tab_s), tabspec(tab_w)],
        out_specs=pl.BlockSpec((None, QBLOCK, 512), lambda bi, g, i: (bi, i, g)),
        out_shape=jax.ShapeDtypeStruct((b, t, 1024), BF16),
        scratch_shapes=[pltpu.VMEM((4 * QBLOCK, 1), F32), pltpu.VMEM((4 * QBLOCK, 1), F32),
                        pltpu.VMEM((4 * QBLOCK, LANES), F32),
                        pltpu.VMEM((4 * QBLOCK, 512), F32), pltpu.VMEM((4 * QBLOCK, 512), F32),
                        pltpu.VMEM((4 * QBLOCK, WINDOW + QBLOCK), F32)],
        compiler_params=_cparams(("arbitrary", "arbitrary", "arbitrary")),
        name="nsa_prompt",
    )(q, gates_g, kc, vc, ov, ks, vs, kw, vw, tab_c, tab_s, tab_w)


def _decode_logits(qf, g0rows, gk_row, k_tiles, bias):
    qg = qf * gk_row
    zero = jnp.zeros_like(qg)
    qbd = jnp.concatenate([jnp.where(g0rows, qg, zero), jnp.where(g0rows, zero, qg)], axis=1).astype(BF16)
    one = jnp.ones_like(qg)
    ones_bd = jnp.concatenate([jnp.where(g0rows, one, zero), jnp.where(g0rows, zero, one)], axis=1).astype(BF16)
    s_parts, q_parts = [], []
    for kt in k_tiles:
        k = kt()
        s_parts.append(_dot_nt(qbd, k.astype(BF16)))
        q_parts.append(_dot_nt(ones_bd, (k * k).astype(BF16)))
    s = jnp.concatenate(s_parts, axis=1)
    ssq = jnp.concatenate(q_parts, axis=1)
    return s * lax.rsqrt(ssq * (1.0 / HEAD_DIM) + EPS) + bias, qg


def _decode_finish(s, qg, g0rows, v_tiles, key_mask, new_row, bias_new):
    if key_mask is not None:
        s = jnp.where(key_mask > 0.5, s, NEG)
    bc = lambda c: jnp.broadcast_to(new_row[c:c + 1, :], qg.shape)
    k_new = jnp.where(g0rows, bc(0), bc(1))
    v_new = jnp.where(g0rows, bc(2), bc(3))
    s_new = jnp.sum(qg * _rms(k_new), axis=-1, keepdims=True) + bias_new
    m = jnp.maximum(jnp.max(s, axis=-1, keepdims=True), s_new)
    p = jnp.exp(s - m)
    p_new = jnp.exp(s_new - m)
    l = jnp.sum(p, axis=-1, keepdims=True) + p_new
    o2 = jnp.zeros((qg.shape[0], 2 * LANES), F32)
    for i, vt in enumerate(v_tiles):
        o2 = o2 + _dot(p[:, i * LANES:(i + 1) * LANES].astype(BF16), vt().astype(BF16))
    o = jnp.where(g0rows, o2[:, :LANES], o2[:, LANES:]) + p_new * v_new
    return o * (1.0 / l)


def _nsa_sample_kernel(pt_ref, cache_c_hbm, cache_s_hbm, *refs, n_pages, n_sel_blocks, group):
    per_b_in, consts, outs = refs[:5], refs[5:-5], refs[-5:-3]
    buf_c, buf_s, sem = refs[-3:]
    step = pl.program_id(0)
    slot = step % 2

    def page_copies(src_step, dst_slot, for_wait):
        copies = []
        for e in range(group):
            for p in range(n_pages):
                pg = 0 if for_wait else pt_ref[src_step * group + e, p]
                copies.append(pltpu.make_async_copy(cache_c_hbm.at[pg], buf_c.at[dst_slot, e, p], sem.at[dst_slot, 0]))
                copies.append(pltpu.make_async_copy(cache_s_hbm.at[pg], buf_s.at[dst_slot, e, p], sem.at[dst_slot, 1]))
        return copies

    @pl.when(step == 0)
    def _():
        for c in page_copies(0, 0, False):
            c.start()

    @pl.when(step + 1 < pl.num_programs(0))
    def _():
        for c in page_copies(step + 1, 1 - slot, False):
            c.start()

    for c in page_copies(step, slot, True):
        c.wait()

    chains = [_nsa_sample_one([buf_c.at[slot, bb, p] for p in range(n_pages)],
                              [buf_s.at[slot, bb, p] for p in range(n_pages)],
                              *[r.at[bb] for r in per_b_in], *consts, *[r.at[bb] for r in outs],
                              n_pages=n_pages, n_sel_blocks=n_sel_blocks, chain_first=bb % 2 == 0)
              for bb in range(group)]
    while chains:
        chains = [c for c in chains if next(c, "done") != "done"]


def _nsa_sample_one(pages_c, pages_s, win_ref, q_ref, gate_ref, ksn_ref, kwn_ref, wk_ref, wv_ref, pos_ref, gk_ref,
                    bc_ref, bs_ref, bw_ref, bnew_ref, ov_ref, e_ref, perm_ref, o_ref, wout_ref, *,
                    n_pages, n_sel_blocks, chain_first):
    qf = q_ref[...]
    nh = qf.shape[0]
    g0rows = lax.broadcasted_iota(jnp.int32, (nh, LANES), 0) < nh // 2
    lane = lax.broadcasted_iota(jnp.int32, (nh, LANES), 1)

    page = pages_c[0].shape[0] // 4
    nseg = n_pages * page // CMP_STRIDE
    perm = perm_ref[...]
    regrouped = []
    for pp in range(n_pages // 2):
        per_cp = []
        for cp in range(2):
            blk = jnp.concatenate(
                [jnp.concatenate([pages_c[2 * pp + i][pl.ds(2 * cp + c, page, stride=4), :] for c in range(2)], axis=1)
                 for i in range(2)], axis=0).astype(BF16)
            per_cp.append(_dot(perm, blk).astype(BF16))
        regrouped.append(per_cp)
    rows_per_tap = 2 * page // CMP_STRIDE
    yield

    def tap_rows(s, col):
        lo, hi = s * rows_per_tap, (s + 1) * rows_per_tap
        return jnp.concatenate([regrouped[pp][col // 2][lo:hi, (col % 2) * LANES:(col % 2 + 1) * LANES]
                                for pp in range(n_pages // 2)], axis=0)

    load_pair = lambda s2, col: jnp.concatenate([tap_rows(2 * s2, col), tap_rows(2 * s2 + 1, col)], axis=1)
    ck, cv = _compress_core(load_pair, nseg, wk_ref, wv_ref, pos_ref)
    yield

    def selection_chain():
        kcn = jnp.concatenate([(_rms(ck[g]) * gk_ref[0:1, :]).astype(BF16) for g in range(2)], axis=0)
        s2 = _dot_nt(qf.astype(BF16), kcn)
        s = jnp.where(g0rows, s2[:, :nseg], s2[:, nseg:]) + bc_ref[...]
        m = jnp.max(s, axis=-1, keepdims=True)
        p = jnp.exp(s - m)
        linv = 1.0 / jnp.sum(p, axis=-1, keepdims=True)
        rowv = lax.broadcasted_iota(jnp.int32, (nseg, LANES), 0) < nseg - 1
        pb = p.astype(BF16)
        oc = [_dot(pb, jnp.where(rowv, cv[g], 0.0).astype(BF16)) for g in range(2)]
        o_c = jnp.where(g0rows, oc[0], oc[1]) * linv
        pn = p * linv
        s0 = jnp.sum(jnp.where(g0rows, pn, 0.0), axis=0, keepdims=True)
        s1 = jnp.sum(jnp.where(g0rows, 0.0, pn), axis=0, keepdims=True)
        psum = jnp.where(g0rows, jnp.broadcast_to(s0, pn.shape), jnp.broadcast_to(s1, pn.shape))
        imp = _dot_hilo(psum, ov_ref[...].astype(BF16))
        cur = n_sel_blocks - 1
        forced = (lane == 0) | (lane == cur) | (lane == cur - 1)
        val = jnp.where(lane >= n_sel_blocks, -2.0, jnp.where(forced, FORCE, imp))
        rank = jnp.zeros_like(val)
        for i in range(n_sel_blocks):
            ci = jnp.broadcast_to(val[:, i:i + 1], val.shape)
            beats = (ci > val) | ((ci == val) & (lane > i))
            rank = rank + jnp.where(beats, 1.0, 0.0)
        sel = jnp.where((rank < float(N_SEL)) & (lane < n_sel_blocks), 1.0, 0.0)
        return o_c, _dot(sel.astype(BF16), e_ref[...])

    def tiles(ref_list, col):
        def tile(rf, i):
            rows = lambda c: rf[pl.ds(4 * LANES * i + c, LANES, stride=4), :]
            return lambda: jnp.concatenate([rows(col), rows(col + 1)], axis=1)
        return [tile(rf, i) for rf in ref_list for i in range(rf.shape[0] // (4 * LANES))]

    b_new = bnew_ref[:, 0:1]

    def mask_free_work():
        s_sel, qg_sel = _decode_logits(qf, g0rows, gk_ref[1:2, :], tiles(pages_s, 0), bs_ref[...])
        s_win, qg_win = _decode_logits(qf, g0rows, gk_ref[2:3, :], tiles([win_ref], 0), bw_ref[...])
        o_w = _decode_finish(s_win, qg_win, g0rows, tiles([win_ref], 2), None, kwn_ref[...], b_new)
        return s_sel, qg_sel, o_w

    if chain_first:
        o_c, key_mask = selection_chain()
        yield
        s_sel, qg_sel, o_w = mask_free_work()
    else:
        s_sel, qg_sel, o_w = mask_free_work()
        yield
        o_c, key_mask = selection_chain()
    yield
    o_s = _decode_finish(s_sel, qg_sel, g0rows, tiles(pages_s, 2), key_mask, ksn_ref[...], b_new)
    yield
    gt = gate_ref[...]
    o_ref[...] = gt[:, 0:1] * o_c + gt[:, 1:2] * o_s + gt[:, 2:3] * o_w

    keep = wout_ref.shape[0] - 4
    drop = win_ref.shape[0] - keep
    wout_ref[0:keep, :] = win_ref[drop:drop + keep, :]
    wout_ref[keep:keep + 4, :] = kwn_ref[...]


def _nsa_sample(page_table, cache_c, cache_s, win, q, gates, ks_new, kw_new, wk_pairs, wv_pairs, pos_rows, gk,
                bias_c, bias_s, bias_w, bias_new, ov, expand):
    nb, n_pages = page_table.shape
    page_rows = cache_c.shape[1]
    n_sel_blocks = -(-(n_pages * (page_rows // 4) + 1) // SEL_LEN)
    group = SAMPLE_GROUP if nb % SAMPLE_GROUP == 0 else 1
    kern = functools.partial(_nsa_sample_kernel, n_pages=n_pages, n_sel_blocks=n_sel_blocks, group=group)
    win_rows_out = 4 * min(WINDOW, n_pages * (page_rows // 4) + 1)
    page = page_rows // 4
    assert n_pages % 2 == 0 and page % CMP_STRIDE == 0
    segs = page // CMP_STRIDE
    i_, n_, s_ = np.meshgrid(np.arange(2), np.arange(segs), np.arange(CMP_STRIDE), indexing="ij")
    perm_np = np.zeros((2 * page, 2 * page), np.float32)
    perm_np[(s_ * 2 * segs + i_ * segs + n_).ravel(), (i_ * page + CMP_STRIDE * n_ + s_).ravel()] = 1.0
    perm = jnp.asarray(perm_np, BF16)
    hbm = pl.BlockSpec(memory_space=pl.ANY)
    full = lambda a: pl.BlockSpec(a.shape, lambda bi, pt: (0,) * a.ndim)
    per_b = lambda a: pl.BlockSpec((group,) + a.shape[1:], lambda bi, pt: (bi,) + (0,) * (a.ndim - 1))
    page_buf = pltpu.VMEM((2, group, n_pages, page_rows, LANES), F32)
    grid_spec = pltpu.PrefetchScalarGridSpec(
        num_scalar_prefetch=1,
        grid=(nb // group,),
        scratch_shapes=[page_buf, page_buf, pltpu.SemaphoreType.DMA((2, 2))],
        in_specs=([hbm, hbm]
                  + [per_b(win), per_b(q), per_b(gates), per_b(ks_new), per_b(kw_new),
                     full(wk_pairs), full(wv_pairs), full(pos_rows), full(gk),
                     full(bias_c), full(bias_s), full(bias_w), full(bias_new), full(ov), full(expand), full(perm)]),
        out_specs=[pl.BlockSpec((group,) + q.shape[1:], lambda bi, pt: (bi, 0, 0)),
                   pl.BlockSpec((group, win_rows_out, LANES), lambda bi, pt: (bi, 0, 0))],
    )
    return pl.pallas_call(
        kern,
        grid_spec=grid_spec,
        out_shape=(jax.ShapeDtypeStruct(q.shape, F32), jax.ShapeDtypeStruct((nb, win_rows_out, LANES), F32)),
        compiler_params=_cparams(("arbitrary",)),
        name="nsa_sample",
    )(page_table, cache_c, cache_s, win, q, gates, ks_new, kw_new,
      wk_pairs, wv_pairs, pos_rows, gk, bias_c, bias_s, bias_w, bias_new, ov, expand, perm)


def _merge_kernel(h_ref, oa_ref, ob_ref, wga_ref, wgb_ref, wg_ref, wn_ref, mix_ref, *cast_refs):
    h = h_ref[...]
    wga = wga_ref[...].astype(BF16)
    wgb = wgb_ref[...].astype(BF16)
    wg = wg_ref[...].astype(BF16)
    wn = wn_ref[...].astype(BF16)
    ga = _sigmoid(_dot_nt(h, wga))
    gb = _sigmoid(_dot_nt(h, wgb))
    mix_ref[...] = (ga * _dot(oa_ref[...], wg) + gb * _dot(ob_ref[...], wn)).astype(BF16)
    for ref, val in zip(cast_refs, (wga, wgb, wg, wn)):
        ref[...] = val


def _merge(h, oa, ob, wga, wgb, wg, wn, gate_rows=None):
    n, d = h.shape
    tm = 512 if n % 512 == 0 else n
    tn = 512
    emit = gate_rows is not None
    assert not emit or n == tm
    row = lambda w: pl.BlockSpec((tm, w), lambda i, j: (i, 0))
    col = lambda k: pl.BlockSpec((k, tn), lambda i, j: (0, j))
    colt = pl.BlockSpec((tn, d), lambda i, j: (j, 0))
    out_specs = [pl.BlockSpec((tm, tn), lambda i, j: (i, j))]
    out_shape = [jax.ShapeDtypeStruct((n, d), BF16)]
    gate_specs = [colt, colt]
    if emit:
        assert all(r % 8 == 0 for r in gate_rows)
        gate_specs = [pl.BlockSpec((pl.Element(tn), pl.Element(d)),
                                   lambda i, j, r=r: ((r // 8 + j * (tn // 8)) * 8, 0)) for r in gate_rows]
        out_specs += [colt, colt, col(wg.shape[0]), col(wn.shape[0])]
        out_shape += [jax.ShapeDtypeStruct((d, d), BF16), jax.ShapeDtypeStruct((d, d), BF16),
                      jax.ShapeDtypeStruct(wg.shape, BF16), jax.ShapeDtypeStruct(wn.shape, BF16)]
    return pl.pallas_call(
        _merge_kernel,
        grid=(n // tm, d // tn),
        in_specs=[row(d), row(oa.shape[1]), row(ob.shape[1])] + gate_specs + [col(wg.shape[0]), col(wn.shape[0])],
        out_specs=out_specs,
        out_shape=out_shape,
        compiler_params=_cparams(("arbitrary", "arbitrary")),
        name="merge",
    )(h, oa, ob, wga, wgb, wg, wn)


def _outproj_kernel(x_ref, mix_ref, wout_ref, gmlp_ref, x1_ref, hm_ref):
    x1 = x_ref[...] + _dot(mix_ref[...], wout_ref[...])
    x1_ref[...] = x1
    hm_ref[...] = (_rms(x1) * gmlp_ref[...]).astype(BF16)


def _outproj(x, mix, wout, gmlp):
    n, d = x.shape
    tm = 512 if n % 512 == 0 else n
    row = pl.BlockSpec((tm, d), lambda i: (i, 0))
    return pl.pallas_call(
        _outproj_kernel,
        grid=(n // tm,),
        in_specs=[row, row, pl.BlockSpec(wout.shape, lambda i: (0, 0)), pl.BlockSpec(gmlp.shape, lambda i: (0, 0))],
        out_specs=[row, row],
        out_shape=(jax.ShapeDtypeStruct((n, d), F32), jax.ShapeDtypeStruct((n, d), BF16)),
        compiler_params=_cparams(("arbitrary",)),
        name="outproj",
    )(x, mix, wout, gmlp)


def _ffn_kernel(hm_ref, x1_ref, wup_ref, wdown_ref, y_ref, *cast_refs):
    f = pl.program_id(1)

    @pl.when(f == 0)
    def _():
        y_ref[...] = x1_ref[...]

    wup = wup_ref[...].astype(BF16)
    wdown = wdown_ref[...].astype(BF16)
    hid = jnp.maximum(_dot(hm_ref[...], wup), 0.0)
    y_ref[...] += _dot((hid * hid).astype(BF16), wdown)
    if cast_refs:
        cast_refs[0][...] = wup
        cast_refs[1][...] = wdown


def _ffn(hm, x1, wup, wdown):
    n, d = hm.shape
    dff = wup.shape[1]
    tm = 512 if n % 512 == 0 else n
    emit = wup.dtype != BF16
    assert not emit or n == tm
    tf = 512 if emit else 1024
    row = pl.BlockSpec((tm, d), lambda i, f: (i, 0))
    up_spec = pl.BlockSpec((d, tf), lambda i, f: (0, f))
    down_spec = pl.BlockSpec((tf, d), lambda i, f: (f, 0))
    out_specs = [row]
    out_shape = [jax.ShapeDtypeStruct((n, d), F32)]
    if emit:
        out_specs += [up_spec, down_spec]
        out_shape += [jax.ShapeDtypeStruct(wup.shape, BF16), jax.ShapeDtypeStruct(wdown.shape, BF16)]
    return pl.pallas_call(
        _ffn_kernel,
        grid=(n // tm, dff // tf),
        in_specs=[row, row, up_spec, down_spec],
        out_specs=out_specs,
        out_shape=out_shape,
        compiler_params=_cparams(("arbitrary", "arbitrary")),
        name="ffn",
    )(hm, x1, wup, wdown)


def _bucket(rel, valid):
    n = np.maximum(rel, 0)
    max_exact = N_BUCKETS // 2
    nf = np.maximum(n, 1).astype(np.float32)
    large = max_exact + (np.log(nf / np.float32(max_exact)) / np.float32(math.log(MAX_DIST / max_exact))
                         * np.float32(N_BUCKETS - max_exact)).astype(np.int32)
    large = np.minimum(large, N_BUCKETS - 1)
    return np.where(valid, np.where(n < max_exact, n, large), -1).astype(np.int32)


def _bias_tables_kernel(rb_ref, *refs, shifts):
    n = len(shifts)
    nbk, nh = rb_ref.shape
    for b_ref, o_ref, shift in zip(refs[:n], refs[n:], shifts):
        b = b_ref[...]
        rows = []
        for h in range(nh):
            sh = rb_ref[nbk - 1, h] if shift else 0.0
            acc = jnp.full(b.shape, NEG, F32)
            for k in range(nbk):
                acc = jnp.where(b == k, rb_ref[k, h] - sh, acc)
            if len(o_ref.shape) == 3:
                o_ref[h] = acc
            else:
                rows.append(acc)
        if rows:
            o_ref[...] = jnp.concatenate(rows, axis=0)


def _bias_tables(rel_bias, buckets, shifts):
    nh = rel_bias.shape[1]
    shapes = [jax.ShapeDtypeStruct((nh,) + (b.shape if b.shape[0] > 1 else b.shape[1:]), F32) for b in buckets]
    vm = pl.BlockSpec(memory_space=pltpu.VMEM)
    return pl.pallas_call(
        functools.partial(_bias_tables_kernel, shifts=tuple(shifts)),
        in_specs=[pl.BlockSpec(memory_space=pltpu.SMEM)] + [vm] * len(buckets),
        out_specs=[vm] * len(buckets),
        out_shape=shapes,
        name="bias_tables",
    )(rel_bias, *[jnp.asarray(b) for b in buckets])


def _overlap(nc, ns):
    i = np.arange(nc)[:, None] * CMP_STRIDE
    j = np.arange(ns)[None, :] * SEL_LEN
    return ((i < j + SEL_LEN) & (i + CMP_LEN > j)).astype(np.float32)


def _compress_weights(w, pos):
    s = np.arange(0, CMP_STRIDE, 2)
    top = jnp.concatenate([w[s], w[CMP_STRIDE + s]], axis=2)
    bot = jnp.concatenate([w[s + 1], w[CMP_STRIDE + s + 1]], axis=2)
    tiles = jnp.concatenate([top, bot], axis=1).astype(BF16)
    row_a = jnp.concatenate([pos[s], pos[s + 1]], axis=1)
    row_b = jnp.concatenate([pos[CMP_STRIDE + s], pos[CMP_STRIDE + s + 1]], axis=1)
    rows = jnp.zeros((len(s), 16, 2 * HEAD_DIM), F32).at[:, 0].set(row_a).at[:, 1].set(row_b)
    return tiles, rows.astype(BF16)


def _dense_tail(x, h, oa, ob, wga, wgb, wg, wn, wout, gmlp, wup, wdown, gate_rows=None):
    mix, *cast_m = _merge(h, oa, ob, wga, wgb, wg, wn, gate_rows)
    x1, hm = _outproj(x, mix, wout, gmlp)
    y, *cast_f = _ffn(hm, x1, wup, wdown)
    return y, (cast_m or [wga, wgb, wg, wn]) + (cast_f or [wup, wdown])


def kernel(x_prompt, x_sample, cache_cmp_kv, cache_sel_kv, state_win_kv, page_table, rel_bias, g_mix_norm, w_in,
           g_sgu, w_sgu, b_sgu, g_q, g_k, pos_cmp_k, w_cmp_k, pos_cmp_v, w_cmp_v, w_proj_gmlp, w_proj_nsa, w_out,
           g_mlp_norm, w_up, w_down):
    depth = g_mix_norm.shape[0]
    assert depth == 1
    l = 0
    bsz, seq, d = x_prompt.shape
    nb = x_sample.shape[0]
    assert x_sample.shape[1] == 1 and seq % KPAD == 0
    d_gm = g_sgu.shape[1]
    n_heads = rel_bias.shape[1]
    d_nsa = n_heads * HEAD_DIM
    kvw_cols = 2 * (n_heads // 4) * HEAD_DIM
    n_gate = 3 * n_heads
    c_q = 2 * d_gm
    c_kv = c_q + d_nsa
    c_gate = c_kv + 3 * kvw_cols
    c_ga = c_gate + n_gate
    c_gb = c_ga + d

    w = jnp.swapaxes(w_in[l], 0, 1)
    n_col_tiles = -(-c_ga // 1024)
    wout = w_out[l].astype(BF16)
    gmix = g_mix_norm[l][None]
    gsgu = g_sgu[l][None]
    gq = g_q[l][None]
    gk = g_k[l]
    gmlp = g_mlp_norm[l][None]
    wk_pairs, posk_rows = _compress_weights(w_cmp_k[l], pos_cmp_k[l])
    wv_pairs, posv_rows = _compress_weights(w_cmp_v[l], pos_cmp_v[l])
    pos_rows = jnp.stack([posk_rows, posv_rows])
    n_groups = w_sgu.shape[1]
    b_exp = jnp.repeat(b_sgu[l].T, d_gm // n_groups, axis=1)
    w00 = jnp.repeat(w_sgu[l][:, 0, 0], d_gm // n_groups)[None]
    b00 = jnp.repeat(b_sgu[l][:, 0], d_gm // n_groups)[None]

    n_pool, page = cache_cmp_kv.shape[1], cache_cmp_kv.shape[2]
    n_pages = page_table.shape[1]
    past = n_pages * page
    nwin = state_win_kv.shape[2]
    n_cmp_s = (past + 1 - CMP_LEN) // CMP_STRIDE + 1
    n_sel_s = -(-(past + 1) // SEL_LEN)
    nseg_s = past // CMP_STRIDE
    qi = np.arange(QBLOCK)[:, None]
    rel_s = qi + QBLOCK - np.arange(2 * QBLOCK)[None, :]
    rel_w = qi + WINDOW - np.arange(WINDOW + QBLOCK)[None, :]
    rel_c = qi + (KC_FRONT * CMP_STRIDE - CMP_LEN + 1) - CMP_STRIDE * np.arange(LANES)[None, :]
    srel_c = (past - (np.arange(nseg_s) * CMP_STRIDE + CMP_LEN - 1))[None]
    srel_s = (past - np.arange(past))[None]
    srel_w = (nwin - np.arange(nwin))[None]
    buckets = [
        _bucket(rel_c, rel_c >= 0), _bucket(rel_s, rel_s >= 0), _bucket(rel_w, (rel_w >= 0) & (rel_w < WINDOW)),
        _bucket(srel_c, (srel_c >= 0) & (np.arange(nseg_s)[None] < n_cmp_s)), _bucket(srel_s, srel_s >= 0),
        _bucket(srel_w, srel_w < WINDOW), np.zeros((1, LANES), np.int32)]
    tab_c, tab_s, tab_w, bias_c, bias_s, bias_w, bias_new = _bias_tables(
        rel_bias, buckets, [True, True, False, False, False, False, False])
    tab_c, tab_s, tab_w = [t.reshape(2, (n_heads // 2) * QBLOCK, t.shape[-1]) for t in (tab_c, tab_s, tab_w)]

    xs = x_sample.reshape(nb, d)
    h_s, v_s, oa_s, q_s, kvc_s, kvs_s, kvw_s, gates_s, w_cat = _inproj(
        xs, gmix, w, gsgu, gq, w00, b00, chunked=False, n_col_tiles=n_col_tiles)
    ov_s = np.zeros((nseg_s, LANES), np.float32)
    ov_s[:n_cmp_s, :n_sel_s] = _overlap(n_cmp_s, n_sel_s)
    expand = jnp.asarray(np.arange(LANES)[:, None] == (np.arange(past)[None, :] // SEL_LEN), BF16)
    gates_h = jnp.pad(gates_s[:, :n_gate].reshape(nb, n_heads, 3), ((0, 0), (0, 0), (0, LANES - 3)))
    lin = lambda a: a[l].reshape(a.shape[1], -1, HEAD_DIM)
    win_lin = lin(state_win_kv)
    ob_s, win_new = _nsa_sample(
        page_table, lin(cache_cmp_kv), lin(cache_sel_kv), win_lin,
        q_s.astype(F32).reshape(nb, n_heads, HEAD_DIM), gates_h,
        kvs_s.reshape(nb, 4, HEAD_DIM), kvw_s.reshape(nb, 4, HEAD_DIM), wk_pairs, wv_pairs, pos_rows, gk,
        bias_c, bias_s, bias_w, bias_new, jnp.asarray(ov_s), expand)
    y_s, (wga, wgb, wg, wn, wup, wdown) = _dense_tail(
        xs, h_s, oa_s, ob_s.reshape(nb, d_nsa).astype(BF16), w, w, w_proj_gmlp[l], w_proj_nsa[l], wout, gmlp,
        w_up[l], w_down[l], gate_rows=(c_ga, c_gb))

    xp = x_prompt.reshape(bsz * seq, d)
    h_p, v_p, oa_p, q_p, kvc_p, kvs_p, kvw_p, gates_p = _inproj(
        xp, gmix, w_cat, gsgu, gq, w_sgu[l], b_exp, chunked=True, n_col_tiles=n_col_tiles)
    kvs3 = kvs_p.reshape(bsz, 4 * seq, HEAD_DIM)
    kvw3 = kvw_p.reshape(bsz, 4 * seq, HEAD_DIM)
    kvc3 = kvc_p.reshape(bsz, 4 * seq, HEAD_DIM)
    ks, vs, kw, vw = _kvprep(kvs3, kvw3, gk)
    kc, vc = _compress_prompt(kvc3, wk_pairs, wv_pairs, pos_rows, gk)
    n_cmp = (seq - CMP_LEN) // CMP_STRIDE + 1
    n_selb = seq // SEL_LEN
    ov_p = np.zeros((KC_ROWS, LANES), np.float32)
    ov_p[KC_FRONT:KC_FRONT + n_cmp, :n_selb] = _overlap(n_cmp, n_selb)
    ov_p = jnp.asarray(ov_p)
    gates_g = gates_p[:, :n_gate].reshape(bsz, seq, 2, n_gate // 2).transpose(0, 2, 1, 3)
    gates_g = jnp.pad(gates_g, ((0, 0), (0, 0), (0, 0), (0, LANES - n_gate // 2)))
    ob_p = _nsa_prompt(q_p.reshape(bsz, seq, d_nsa), gates_g, kc, vc, ov_p, ks, vs, kw, vw, tab_c, tab_s, tab_w)
    y_p, _ = _dense_tail(xp, h_p, oa_p, ob_p.reshape(bsz * seq, d_nsa), wga, wgb, wg, wn, wout, gmlp, wup, wdown)

    n_kv = n_heads // 4
    kv6 = lambda a, b_, t_: a.reshape(1, b_, t_, 2, n_kv, HEAD_DIM)
    nw_p = min(WINDOW, seq)
    last = ((seq - 1) // CHUNK) * CHUNK
    nw_s = min(WINDOW, past + 1)
    return (y_p.reshape(bsz, seq, d), y_s.reshape(nb, 1, d),
            kv6(kvc3, bsz, seq), kv6(kvs3, bsz, seq), kv6(kvw3[:, 4 * (seq - nw_p):], bsz, nw_p),
            v_p.reshape(bsz, seq, d_gm)[:, last:][None],
            kv6(kvc_s, nb, 1), kv6(kvs_s, nb, 1), kv6(win_new, nb, nw_s),
            v_s.reshape(1, nb, 1, d_gm))
```

```python
import functools
import math

import numpy as np
import jax
import jax.numpy as jnp
from jax import lax
from jax.experimental import pallas as pl
from jax.experimental.pallas import tpu as pltpu

F32 = jnp.float32
BF16 = jnp.bfloat16

HEAD_DIM = 128
CHUNK = 128
CMP_LEN = 32
CMP_STRIDE = 16
SEL_LEN = 64
N_SEL = 16
WINDOW = 512
N_BUCKETS = 32
MAX_DIST = 128
QBLOCK = 128
EPS = 1e-6
NEG = -1e30
HALF_NEG = -5e29
FORCE = 1e6

LANES = 128
PAD_LANE = 64
KPAD = 512
KC_FRONT = 16
KC_ROWS = 376
SAMPLE_GROUP = 2
VMEM_LIMIT = 56 * 1024 * 1024


def _cparams(sem):
    return pltpu.CompilerParams(dimension_semantics=sem, vmem_limit_bytes=VMEM_LIMIT)


def _dot(a, b):
    return jnp.dot(a, b, preferred_element_type=F32)


def _dot_nt(a, b):
    return lax.dot_general(a, b, (((1,), (1,)), ((), ())), preferred_element_type=F32)


def _dot_hilo(a, b_bf16):
    hi = a.astype(BF16)
    lo = (a - hi.astype(F32)).astype(BF16)
    return _dot(hi, b_bf16) + _dot(lo, b_bf16)


def _rms(x):
    return x * lax.rsqrt(jnp.mean(x * x, axis=-1, keepdims=True) + EPS)


def _gelu(x):
    c = math.sqrt(2.0 / math.pi)
    return 0.5 * x * (1.0 + jnp.tanh(c * (x + 0.044715 * (x * x * x))))


def _sigmoid(x):
    return 1.0 / (1.0 + jnp.exp(-x))


def _inproj_kernel(x_ref, gmix_ref, w_ref, gsgu_ref, gq_ref, wsg_ref, bsg_ref,
                   h_ref, v_ref, oa_ref, q_ref, kvc_ref, kvs_ref, kvw_ref, gate_ref, *rest, chunked, tm):
    h_s, u_s = rest[-2:]
    j = pl.program_id(1)

    @pl.when(j == 0)
    def _():
        hb = (_rms(x_ref[...]) * gmix_ref[...]).astype(BF16)
        h_s[...] = hb
        h_ref[...] = hb

    if len(rest) == 3:
        w = w_ref[...].astype(BF16)
        rest[0][...] = w
        z = _dot_nt(h_s[...], w)
    else:
        z = _dot_nt(h_s[...], w_ref[...])

    @pl.when(j == 0)
    def _():
        u_s[...] = _gelu(z)

    @pl.when(j == 1)
    def _():
        v = _rms(_gelu(z)) * gsgu_ref[...]
        v_ref[...] = v
        if chunked:
            row = lax.broadcasted_iota(jnp.int32, (CHUNK, CHUNK), 0)
            col = lax.broadcasted_iota(jnp.int32, (CHUNK, CHUNK), 1)
            n_groups = v.shape[1] // LANES
            for g in range(n_groups):
                wm = jnp.where(row >= col, wsg_ref[g], 0.0).astype(BF16)
                cs = slice(g * LANES, (g + 1) * LANES)
                for c in range(tm // CHUNK):
                    rs = slice(c * CHUNK, (c + 1) * CHUNK)
                    s = _dot(wm, v[rs, cs].astype(BF16)) + bsg_ref[:, cs]
                    oa_ref[rs, cs] = (u_s[rs, cs] * s).astype(BF16)
        else:
            oa_ref[...] = (u_s[...] * (v * wsg_ref[...] + bsg_ref[...])).astype(BF16)

    @pl.when(j == 2)
    def _():
        scale = HEAD_DIM ** -0.5
        for hd in range(z.shape[1] // HEAD_DIM):
            cs = slice(hd * HEAD_DIM, (hd + 1) * HEAD_DIM)
            q_ref[:, cs] = (_rms(z[:, cs]) * gq_ref[...] * scale).astype(BF16)

    def store_kv(ref, zz):
        for c in range(4):
            ref[pl.ds(c, tm, stride=4), :] = zz[:, c * LANES:(c + 1) * LANES]

    @pl.when(j == 3)
    def _():
        store_kv(kvc_ref, z[:, :512])
        store_kv(kvs_ref, z[:, 512:])

    @pl.when(j == 4)
    def _():
        store_kv(kvw_ref, z[:, :512])
        gate_ref[...] = _sigmoid(z[:, 512:640])


def _inproj(x, gmix, w_cat, gsgu, gq, wsg, bsg, *, chunked, n_col_tiles):
    n, d = x.shape
    tm = 512 if n % 512 == 0 else n
    tn = 1024
    emit = w_cat.dtype != BF16
    assert not emit or n == tm
    kern = functools.partial(_inproj_kernel, chunked=chunked, tm=tm)
    full = lambda a: pl.BlockSpec(a.shape, lambda i, j: (0,) * a.ndim)
    row = lambda w: pl.BlockSpec((tm, w), lambda i, j: (i, 0))
    kvrow = pl.BlockSpec((4 * tm, LANES), lambda i, j: (i, 0))
    out_shapes = (
        jax.ShapeDtypeStruct((n, d), BF16),
        jax.ShapeDtypeStruct((n, 1024), F32),
        jax.ShapeDtypeStruct((n, 1024), BF16),
        jax.ShapeDtypeStruct((n, 1024), BF16),
        jax.ShapeDtypeStruct((4 * n, LANES), F32),
        jax.ShapeDtypeStruct((4 * n, LANES), F32),
        jax.ShapeDtypeStruct((4 * n, LANES), F32),
        jax.ShapeDtypeStruct((n, LANES), F32),
    )
    wspec = pl.BlockSpec((tn, d), lambda i, j: (j, 0))
    out_specs = [row(d), row(1024), row(1024), row(1024), kvrow, kvrow, kvrow, row(LANES)]
    if emit:
        out_shapes += (jax.ShapeDtypeStruct((n_col_tiles * tn, d), BF16),)
        out_specs.append(wspec)
    return pl.pallas_call(
        kern,
        grid=(n // tm, n_col_tiles),
        in_specs=[row(d), full(gmix), wspec, full(gsgu), full(gq), full(wsg), full(bsg)],
        out_specs=out_specs,
        out_shape=out_shapes,
        scratch_shapes=[pltpu.VMEM((tm, d), BF16), pltpu.VMEM((tm, 1024), F32)],
        compiler_params=_cparams(("arbitrary", "arbitrary")),
        name="inproj",
    )(x, gmix, w_cat, gsgu, gq, wsg, bsg)


def _kvprep_kernel(kvs_ref, kvw_ref, gk_ref, ks_ref, vs_ref, kw_ref, vw_ref):
    i = pl.program_id(1)
    rows = kvs_ref.shape[0] // 4
    col = lambda ref, c: ref[pl.ds(c, rows, stride=4), :]
    lane = lax.broadcasted_iota(jnp.int32, (rows, LANES), 1)
    row = lax.broadcasted_iota(jnp.int32, (rows, LANES), 0)

    @pl.when(i == 0)
    def _():
        aux = jnp.where(lane == PAD_LANE, 1.0, 0.0).astype(BF16)
        zk = jnp.zeros((rows, LANES), BF16)
        for g in range(2):
            ks_ref[g] = jnp.concatenate([zk, aux], axis=1)
            kw_ref[g] = jnp.concatenate([zk, aux], axis=1)
            vs_ref[g] = zk
            vw_ref[g] = zk

    @pl.when(i > 0)
    def _():
        blk = ((i - 1) * rows + row) // SEL_LEN
        onehot = jnp.where(lane == blk, 1.0, 0.0).astype(BF16)
        zaux = jnp.zeros((rows, LANES), BF16)
        for g in range(2):
            ks = (_rms(col(kvs_ref, g)) * gk_ref[1:2, :]).astype(BF16)
            kw = (_rms(col(kvw_ref, g)) * gk_ref[2:3, :]).astype(BF16)
            ks_ref[g] = jnp.concatenate([ks, onehot], axis=1)
            kw_ref[g] = jnp.concatenate([kw, zaux], axis=1)
            vs_ref[g] = col(kvs_ref, 2 + g).astype(BF16)
            vw_ref[g] = col(kvw_ref, 2 + g).astype(BF16)


def _kvprep(kvs, kvw, gk):
    b, t4, _ = kvs.shape
    t = t4 // 4
    rows = KPAD
    nblk = t // rows
    in_map = lambda bi, i: (bi, jnp.maximum(i - 1, 0), 0)
    out_map = lambda bi, i: (bi, 0, i, 0)
    kshape = jax.ShapeDtypeStruct((b, 2, KPAD + t, 2 * LANES), BF16)
    vshape = jax.ShapeDtypeStruct((b, 2, KPAD + t, LANES), BF16)
    return pl.pallas_call(
        _kvprep_kernel,
        grid=(b, nblk + 1),
        in_specs=[pl.BlockSpec((None, 4 * rows, LANES), in_map), pl.BlockSpec((None, 4 * rows, LANES), in_map),
                  pl.BlockSpec(gk.shape, lambda bi, i: (0, 0))],
        out_specs=[pl.BlockSpec((None, 2, rows, 2 * LANES), out_map), pl.BlockSpec((None, 2, rows, LANES), out_map),
                   pl.BlockSpec((None, 2, rows, 2 * LANES), out_map), pl.BlockSpec((None, 2, rows, LANES), out_map)],
        out_shape=(kshape, vshape, kshape, vshape),
        compiler_params=_cparams(("arbitrary", "arbitrary")),
        name="kvprep",
    )(kvs, kvw, gk)


def _compress_core(load_pair, nseg, wk_ref, wv_ref, pos_ref):
    w_refs = (wk_ref, wv_ref)
    accs = [jnp.zeros((2 * nseg + 16, 2 * LANES), F32) for _ in range(2)]
    for s2 in range(CMP_STRIDE // 2):
        for kv in range(2):
            parts = [load_pair(s2, 2 * kv + g) for g in range(2)]
            parts.append(pos_ref[kv, s2])
            lhs = jnp.concatenate(parts, axis=0)
            accs[kv] = accs[kv] + _dot(lhs, w_refs[kv][s2])
    outs = []
    for kv in range(2):
        y = accs[kv]
        post = y[2 * nseg:2 * nseg + 1, :LANES] + y[2 * nseg + 1:2 * nseg + 2, LANES:]
        per_g = []
        for g in range(2):
            y0 = y[g * nseg:(g + 1) * nseg, :LANES]
            y1 = y[g * nseg:(g + 1) * nseg, LANES:]
            per_g.append(y0 + pltpu.roll(y1, nseg - 1, 0) + post)
        outs.append(per_g)
    return outs


def _compress_prompt_kernel(x_ref, wk_ref, wv_ref, pos_ref, gk_ref, kc_ref, vc_ref):
    nseg = x_ref.shape[0] // (4 * CMP_STRIDE)
    load_x = lambda s, col: x_ref[pl.ds(4 * s + col, nseg, stride=4 * CMP_STRIDE), :]
    load_pair = lambda s2, col: jnp.concatenate([load_x(2 * s2, col), load_x(2 * s2 + 1, col)], axis=1).astype(BF16)
    ck, cv = _compress_core(load_pair, nseg, wk_ref, wv_ref, pos_ref)
    row = lax.broadcasted_iota(jnp.int32, (nseg, LANES), 0)
    lane = lax.broadcasted_iota(jnp.int32, (nseg, LANES), 1)
    valid = row < nseg - 1
    aux = jnp.where(valid, jnp.where(lane == row // 8, 1.0, 0.0), jnp.where(lane == PAD_LANE, 1.0, 0.0))
    back = KC_ROWS - KC_FRONT - nseg
    lane_f = lax.broadcasted_iota(jnp.int32, (KC_FRONT, LANES), 1)
    lane_b = lax.broadcasted_iota(jnp.int32, (back, LANES), 1)
    pad_f = jnp.concatenate([jnp.zeros((KC_FRONT, LANES), F32), jnp.where(lane_f == PAD_LANE, 1.0, 0.0)], axis=1)
    pad_b = jnp.concatenate([jnp.zeros((back, LANES), F32), jnp.where(lane_b == PAD_LANE, 1.0, 0.0)], axis=1)
    for g in range(2):
        kn = jnp.where(valid, _rms(ck[g]) * gk_ref[0:1, :], 0.0)
        kc_ref[g, 0:KC_FRONT, :] = pad_f
        kc_ref[g, KC_FRONT:KC_FRONT + nseg, :] = jnp.concatenate([kn, aux], axis=1)
        kc_ref[g, KC_FRONT + nseg:KC_ROWS, :] = pad_b
        vc_ref[g, 0:KC_FRONT, :] = jnp.zeros((KC_FRONT, LANES), F32)
        vc_ref[g, KC_FRONT:KC_FRONT + nseg, :] = jnp.where(valid, cv[g], 0.0)
        vc_ref[g, KC_FRONT + nseg:KC_ROWS, :] = jnp.zeros((back, LANES), F32)


def _compress_prompt(kvc, wk_pairs, wv_pairs, pos_rows, gk):
    b, t4, _ = kvc.shape
    full = lambda a: pl.BlockSpec(a.shape, lambda bi: (0,) * a.ndim)
    return pl.pallas_call(
        _compress_prompt_kernel,
        grid=(b,),
        in_specs=[pl.BlockSpec((None, t4, LANES), lambda bi: (bi, 0, 0)),
                  full(wk_pairs), full(wv_pairs), full(pos_rows), full(gk)],
        out_specs=[pl.BlockSpec((None, 2, KC_ROWS, 2 * LANES), lambda bi: (bi, 0, 0, 0)),
                   pl.BlockSpec((None, 2, KC_ROWS, LANES), lambda bi: (bi, 0, 0, 0))],
        out_shape=(jax.ShapeDtypeStruct((b, 2, KC_ROWS, 2 * LANES), F32),
                   jax.ShapeDtypeStruct((b, 2, KC_ROWS, LANES), F32)),
        compiler_params=_cparams(("arbitrary",)),
        name="compress_prompt",
    )(kvc, wk_pairs, wv_pairs, pos_rows, gk)


def _nsa_prompt_kernel(q_ref, gate_ref, kc_ref, vc_ref, ov_ref, ks_ref, vs_ref, kw_ref, vw_ref,
                       tc_ref, ts_ref, tw_ref, o_ref, m_s, l_s, acc_s, sa_s, sb_s, sw_s):
    qb = pl.program_id(2)
    t0 = qb * QBLOCK
    rq = 4 * QBLOCK
    q = q_ref[...]
    q4 = jnp.concatenate([q[:, r * LANES:(r + 1) * LANES] for r in range(4)], axis=0)
    lane = lax.broadcasted_iota(jnp.int32, (rq, LANES), 1)
    is_pad_lane = lane == PAD_LANE

    def q_aug(mb):
        return jnp.concatenate([q4, mb.astype(BF16)], axis=1)

    qa_pad = q_aug(jnp.where(is_pad_lane, NEG, 0.0))

    far_mask = ((lane < 32) & (lane >= qb - 2)) | is_pad_lane
    qa_cfar = q_aug(jnp.where(far_mask, NEG, 0.0))
    ncmp = 256
    near0 = pl.multiple_of(qb * 8, 8)
    k_far = kc_ref[KC_FRONT:KC_FRONT + ncmp, :].astype(BF16)
    k_near = kc_ref[pl.ds(near0, LANES), :].astype(BF16)
    s_far = _dot_nt(qa_cfar, k_far)
    s_near = _dot_nt(qa_pad, k_near) + tc_ref[...]
    w0 = pl.multiple_of(t0, QBLOCK)
    wlen = WINDOW + QBLOCK
    sw_s[...] = _dot_nt(qa_pad, kw_ref[pl.ds(w0, wlen), :]) + tw_ref[...]
    m = jnp.maximum(jnp.max(s_far, axis=-1, keepdims=True), jnp.max(s_near, axis=-1, keepdims=True))
    m = jnp.maximum(m, HALF_NEG)
    p_far = jnp.exp(s_far - m)
    p_near = jnp.exp(s_near - m)
    l = jnp.sum(p_far, axis=-1, keepdims=True) + jnp.sum(p_near, axis=-1, keepdims=True)
    linv = 1.0 / jnp.where(l > 0.0, l, 1.0)
    v_far = vc_ref[KC_FRONT:KC_FRONT + ncmp, :].astype(BF16)
    v_near = vc_ref[pl.ds(near0, LANES), :].astype(BF16)
    pb_far = p_far.astype(BF16)
    pb_near = p_near.astype(BF16)
    o_c = (_dot(pb_far, v_far) + _dot(pb_near, v_near)) * linv

    s = sw_s[...]
    m = jnp.max(s, axis=-1, keepdims=True)
    p = jnp.exp(s - m)
    l = jnp.sum(p, axis=-1, keepdims=True)
    o_w = _dot(p.astype(BF16), vw_ref[pl.ds(w0, wlen), :]) * (1.0 / l)

    ov_far = ov_ref[KC_FRONT:KC_FRONT + ncmp, :].astype(BF16)
    ov_near = ov_ref[pl.ds(near0, LANES), :].astype(BF16)
    imp4 = (_dot(pb_far, ov_far) + _dot(pb_near, ov_near)) * linv
    imp = sum(imp4[r * QBLOCK:(r + 1) * QBLOCK] for r in range(4))
    nblk = 64
    imp_t = imp.T[:nblk]
    blk = lax.broadcasted_iota(jnp.int32, (nblk, QBLOCK), 0)
    qpos = t0 + lax.broadcasted_iota(jnp.int32, (nblk, QBLOCK), 1)
    cur = qpos // SEL_LEN
    forced = (blk == 0) | (blk == cur) | (blk == cur - 1)
    eligible = blk * SEL_LEN <= qpos
    val = jnp.where(forced, FORCE, jnp.where(eligible, imp_t, -1.0))
    slab = 8
    vals = [val[v * slab:(v + 1) * slab] for v in range(nblk // slab)]
    ranks = [jnp.zeros((slab, QBLOCK), F32) for _ in vals]
    row_in_slab = lax.broadcasted_iota(jnp.int32, (slab, QBLOCK), 0)
    for i in range(nblk):
        vi, ii = divmod(i, slab)
        ri = jnp.broadcast_to(vals[vi][ii:ii + 1, :], (slab, QBLOCK))
        for v in range(len(vals)):
            if v < vi:
                beats = ri > vals[v]
            elif v > vi:
                beats = ri >= vals[v]
            else:
                beats = (ri > vals[v]) | ((ri == vals[v]) & (row_in_slab > ii))
            ranks[v] = ranks[v] + jnp.where(beats, 1.0, 0.0)
    rank = jnp.concatenate(ranks, axis=0)
    mb_t = jnp.where(rank < float(N_SEL), 0.0, NEG)
    row2 = lax.broadcasted_iota(jnp.int32, (LANES - nblk, QBLOCK), 0)
    mb_t = jnp.concatenate([mb_t, jnp.where(row2 == PAD_LANE - nblk, NEG, 0.0)], axis=0)
    mb = mb_t.T
    mb4 = jnp.concatenate([mb] * 4, axis=0)
    qa_snear = q_aug(mb4)
    qa_sfar = q_aug(jnp.where((lane < nblk) & (lane >= 2 * qb - 2), NEG, mb4))

    far_tile = 512
    n_far = (jnp.maximum(qb - 1, 0) * QBLOCK + far_tile - 1) // far_tile

    def far_logits(j):
        r0 = pl.multiple_of(KPAD + j * far_tile, far_tile)
        return _dot_nt(qa_sfar, ks_ref[pl.ds(r0, far_tile), :])

    sn0 = pl.multiple_of(t0 + KPAD - QBLOCK, QBLOCK)
    s = _dot_nt(qa_snear, ks_ref[pl.ds(sn0, 2 * QBLOCK), :]) + ts_ref[...]
    sa_s[...] = far_logits(0)
    m = jnp.max(s, axis=-1, keepdims=True)
    p = jnp.exp(s - m)
    l = jnp.sum(p, axis=-1, keepdims=True)
    acc = _dot(p.astype(BF16), vs_ref[pl.ds(sn0, 2 * QBLOCK), :])

    m_s[...] = m
    l_s[...] = l
    acc_s[...] = acc
    last_tile = ks_ref.shape[0] // far_tile - 2

    def far_update(j, s_ref):
        r0 = pl.multiple_of(KPAD + j * far_tile, far_tile)
        s = s_ref[...]
        m_old = m_s[...]
        m_new = jnp.maximum(m_old, jnp.max(s, axis=-1, keepdims=True))
        alpha = jnp.exp(m_old - m_new)
        p = jnp.exp(s - m_new)
        l_s[...] = alpha * l_s[...] + jnp.sum(p, axis=-1, keepdims=True)
        acc_s[...] = alpha * acc_s[...] + _dot(p.astype(BF16), vs_ref[pl.ds(r0, far_tile), :])
        m_s[...] = m_new

    def far_body(i, carry):
        a = 2 * i
        sb_s[...] = far_logits(a + 1)
        far_update(a, sa_s)
        sa_s[...] = far_logits(jnp.minimum(a + 2, last_tile))
        far_update(a + 1, sb_s)
        return carry

    lax.fori_loop(0, n_far // 2, far_body, 0)

    @pl.when(n_far % 2 == 1)
    def _():
        far_update(n_far - 1, sa_s)

    o_s = acc_s[...] * (1.0 / l_s[...])

    gt = gate_ref[...]
    outs = []
    for r in range(4):
        rs = slice(r * QBLOCK, (r + 1) * QBLOCK)
        outs.append(gt[:, 3 * r:3 * r + 1] * o_c[rs] + gt[:, 3 * r + 1:3 * r + 2] * o_s[rs]
                    + gt[:, 3 * r + 2:3 * r + 3] * o_w[rs])
    o_ref[...] = jnp.concatenate(outs, axis=1).astype(BF16)


def _nsa_prompt(q, gates_g, kc, vc, ov, ks, vs, kw, vw, tab_c, tab_s, tab_w):
    b, t, _ = q.shape
    nqb = t // QBLOCK
    kvspec = lambda a: pl.BlockSpec((None, None) + a.shape[2:], lambda bi, g, i: (bi, g, 0, 0))
    tabspec = lambda a: pl.BlockSpec((None,) + a.shape[1:], lambda bi, g, i: (g, 0, 0))
    return pl.pallas_call(
        _nsa_prompt_kernel,
        grid=(b, 2, nqb),
        in_specs=[pl.BlockSpec((None, QBLOCK, 512), lambda bi, g, i: (bi, i, g)),
                  pl.BlockSpec((None, None, QBLOCK, LANES), lambda bi, g, i: (bi, g, i, 0)),
                  kvspec(kc), kvspec(vc), pl.BlockSpec(ov.shape, lambda bi, g, i: (0, 0)),
                  kvspec(ks), kvspec(vs), kvspec(kw), kvspec(vw),
                  tabspec(tab_c), tabspec(tab_s), tabspec(tab_w)],
        out_specs=pl.BlockSpec((None, QBLOCK, 512), lambda bi, g, i: (bi, i, g)),
        out_shape=jax.ShapeDtypeStruct((b, t, 1024), BF16),
        scratch_shapes=[pltpu.VMEM((4 * QBLOCK, 1), F32), pltpu.VMEM((4 * QBLOCK, 1), F32),
                        pltpu.VMEM((4 * QBLOCK, LANES), F32),
                        pltpu.VMEM((4 * QBLOCK, 512), F32), pltpu.VMEM((4 * QBLOCK, 512), F32),
                        pltpu.VMEM((4 * QBLOCK, WINDOW + QBLOCK), F32)],
        compiler_params=_cparams(("arbitrary", "arbitrary", "arbitrary")),
        name="nsa_prompt",
    )(q, gates_g, kc, vc, ov, ks, vs, kw, vw, tab_c, tab_s, tab_w)


def _decode_logits(qf, g0rows, gk_row, k_tiles, bias):
    qg = qf * gk_row
    zero = jnp.zeros_like(qg)
    qbd = jnp.concatenate([jnp.where(g0rows, qg, zero), jnp.where(g0rows, zero, qg)], axis=1).astype(BF16)
    one = jnp.ones_like(qg)
    ones_bd = jnp.concatenate([jnp.where(g0rows, one, zero), jnp.where(g0rows, zero, one)], axis=1).astype(BF16)
    s_parts, q_parts = [], []
    for kt in k_tiles:
        k = kt()
        s_parts.append(_dot_nt(qbd, k.astype(BF16)))
        q_parts.append(_dot_nt(ones_bd, (k * k).astype(BF16)))
    s = jnp.concatenate(s_parts, axis=1)
    ssq = jnp.concatenate(q_parts, axis=1)
    return s * lax.rsqrt(ssq * (1.0 / HEAD_DIM) + EPS) + bias, qg


def _decode_finish(s, qg, g0rows, v_tiles, key_mask, new_row, bias_new):
    if key_mask is not None:
        s = jnp.where(key_mask > 0.5, s, NEG)
    bc = lambda c: jnp.broadcast_to(new_row[c:c + 1, :], qg.shape)
    k_new = jnp.where(g0rows, bc(0), bc(1))
    v_new = jnp.where(g0rows, bc(2), bc(3))
    s_new = jnp.sum(qg * _rms(k_new), axis=-1, keepdims=True) + bias_new
    m = jnp.maximum(jnp.max(s, axis=-1, keepdims=True), s_new)
    p = jnp.exp(s - m)
    p_new = jnp.exp(s_new - m)
    l = jnp.sum(p, axis=-1, keepdims=True) + p_new
    o2 = jnp.zeros((qg.shape[0], 2 * LANES), F32)
    for i, vt in enumerate(v_tiles):
        o2 = o2 + _dot(p[:, i * LANES:(i + 1) * LANES].astype(BF16), vt().astype(BF16))
    o = jnp.where(g0rows, o2[:, :LANES], o2[:, LANES:]) + p_new * v_new
    return o * (1.0 / l)


def _nsa_sample_kernel(pt_ref, cache_c_hbm, cache_s_hbm, *refs, n_pages, n_sel_blocks, group):
    per_b_in, consts, outs = refs[:5], refs[5:-5], refs[-5:-3]
    buf_c, buf_s, sem = refs[-3:]
    step = pl.program_id(0)
    slot = step % 2

    def page_copies(src_step, dst_slot, for_wait):
        copies = []
        for e in range(group):
            for p in range(n_pages):
                pg = 0 if for_wait else pt_ref[src_step * group + e, p]
                copies.append(pltpu.make_async_copy(cache_c_hbm.at[pg], buf_c.at[dst_slot, e, p], sem.at[dst_slot, 0]))
                copies.append(pltpu.make_async_copy(cache_s_hbm.at[pg], buf_s.at[dst_slot, e, p], sem.at[dst_slot, 1]))
        return copies

    @pl.when(step == 0)
    def _():
        for c in page_copies(0, 0, False):
            c.start()

    @pl.when(step + 1 < pl.num_programs(0))
    def _():
        for c in page_copies(step + 1, 1 - slot, False):
            c.start()

    for c in page_copies(step, slot, True):
        c.wait()

    chains = [_nsa_sample_one([buf_c.at[slot, bb, p] for p in range(n_pages)],
                              [buf_s.at[slot, bb, p] for p in range(n_pages)],
                              *[r.at[bb] for r in per_b_in], *consts, *[r.at[bb] for r in outs],
                              n_pages=n_pages, n_sel_blocks=n_sel_blocks, chain_first=bb % 2 == 0)
              for bb in range(group)]
    while chains:
        chains = [c for c in chains if next(c, "done") != "done"]


def _nsa_sample_one(pages_c, pages_s, win_ref, q_ref, gate_ref, ksn_ref, kwn_ref, wk_ref, wv_ref, pos_ref, gk_ref,
                    bc_ref, bs_ref, bw_ref, bnew_ref, ov_ref, e_ref, perm_ref, o_ref, wout_ref, *,
                    n_pages, n_sel_blocks, chain_first):
    qf = q_ref[...]
    nh = qf.shape[0]
    g0rows = lax.broadcasted_iota(jnp.int32, (nh, LANES), 0) < nh // 2
    lane = lax.broadcasted_iota(jnp.int32, (nh, LANES), 1)

    page = pages_c[0].shape[0] // 4
    nseg = n_pages * page // CMP_STRIDE
    perm = perm_ref[...]
    regrouped = []
    for pp in range(n_pages // 2):
        per_cp = []
        for cp in range(2):
            blk = jnp.concatenate(
                [jnp.concatenate([pages_c[2 * pp + i][pl.ds(2 * cp + c, page, stride=4), :] for c in range(2)], axis=1)
                 for i in range(2)], axis=0).astype(BF16)
            per_cp.append(_dot(perm, blk).astype(BF16))
        regrouped.append(per_cp)
    rows_per_tap = 2 * page // CMP_STRIDE
    yield

    def tap_rows(s, col):
        lo, hi = s * rows_per_tap, (s + 1) * rows_per_tap
        return jnp.concatenate([regrouped[pp][col // 2][lo:hi, (col % 2) * LANES:(col % 2 + 1) * LANES]
                                for pp in range(n_pages // 2)], axis=0)

    load_pair = lambda s2, col: jnp.concatenate([tap_rows(2 * s2, col), tap_rows(2 * s2 + 1, col)], axis=1)
    ck, cv = _compress_core(load_pair, nseg, wk_ref, wv_ref, pos_ref)
    yield

    def selection_chain():
        kcn = jnp.concatenate([(_rms(ck[g]) * gk_ref[0:1, :]).astype(BF16) for g in range(2)], axis=0)
        s2 = _dot_nt(qf.astype(BF16), kcn)
        s = jnp.where(g0rows, s2[:, :nseg], s2[:, nseg:]) + bc_ref[...]
        m = jnp.max(s, axis=-1, keepdims=True)
        p = jnp.exp(s - m)
        linv = 1.0 / jnp.sum(p, axis=-1, keepdims=True)
        rowv = lax.broadcasted_iota(jnp.int32, (nseg, LANES), 0) < nseg - 1
        pb = p.astype(BF16)
        oc = [_dot(pb, jnp.where(rowv, cv[g], 0.0).astype(BF16)) for g in range(2)]
        o_c = jnp.where(g0rows, oc[0], oc[1]) * linv
        pn = p * linv
        s0 = jnp.sum(jnp.where(g0rows, pn, 0.0), axis=0, keepdims=True)
        s1 = jnp.sum(jnp.where(g0rows, 0.0, pn), axis=0, keepdims=True)
        psum = jnp.where(g0rows, jnp.broadcast_to(s0, pn.shape), jnp.broadcast_to(s1, pn.shape))
        imp = _dot_hilo(psum, ov_ref[...].astype(BF16))
        cur = n_sel_blocks - 1
        forced = (lane == 0) | (lane == cur) | (lane == cur - 1)
        val = jnp.where(lane >= n_sel_blocks, -2.0, jnp.where(forced, FORCE, imp))
        rank = jnp.zeros_like(val)
        for i in range(n_sel_blocks):
            ci = jnp.broadcast_to(val[:, i:i + 1], val.shape)
            beats = (ci > val) | ((ci == val) & (lane > i))
            rank = rank + jnp.where(beats, 1.0, 0.0)
        sel = jnp.where((rank < float(N_SEL)) & (lane < n_sel_blocks), 1.0, 0.0)
        return o_c, _dot(sel.astype(BF16), e_ref[...])

    def tiles(ref_list, col):
        def tile(rf, i):
            rows = lambda c: rf[pl.ds(4 * LANES * i + c, LANES, stride=4), :]
            return lambda: jnp.concatenate([rows(col), rows(col + 1)], axis=1)
        return [tile(rf, i) for rf in ref_list for i in range(rf.shape[0] // (4 * LANES))]

    b_new = bnew_ref[:, 0:1]

    def mask_free_work():
        s_sel, qg_sel = _decode_logits(qf, g0rows, gk_ref[1:2, :], tiles(pages_s, 0), bs_ref[...])
        s_win, qg_win = _decode_logits(qf, g0rows, gk_ref[2:3, :], tiles([win_ref], 0), bw_ref[...])
        o_w = _decode_finish(s_win, qg_win, g0rows, tiles([win_ref], 2), None, kwn_ref[...], b_new)
        return s_sel, qg_sel, o_w

    if chain_first:
        o_c, key_mask = selection_chain()
        yield
        s_sel, qg_sel, o_w = mask_free_work()
    else:
        s_sel, qg_sel, o_w = mask_free_work()
        yield
        o_c, key_mask = selection_chain()
    yield
    o_s = _decode_finish(s_sel, qg_sel, g0rows, tiles(pages_s, 2), key_mask, ksn_ref[...], b_new)
    yield
    gt = gate_ref[...]
    o_ref[...] = gt[:, 0:1] * o_c + gt[:, 1:2] * o_s + gt[:, 2:3] * o_w

    keep = wout_ref.shape[0] - 4
    drop = win_ref.shape[0] - keep
    wout_ref[0:keep, :] = win_ref[drop:drop + keep, :]
    wout_ref[keep:keep + 4, :] = kwn_ref[...]


def _nsa_sample(page_table, cache_c, cache_s, win, q, gates, ks_new, kw_new, wk_pairs, wv_pairs, pos_rows, gk,
                bias_c, bias_s, bias_w, bias_new, ov, expand):
    nb, n_pages = page_table.shape
    page_rows = cache_c.shape[1]
    n_sel_blocks = -(-(n_pages * (page_rows // 4) + 1) // SEL_LEN)
    group = SAMPLE_GROUP if nb % SAMPLE_GROUP == 0 else 1
    kern = functools.partial(_nsa_sample_kernel, n_pages=n_pages, n_sel_blocks=n_sel_blocks, group=group)
    win_rows_out = 4 * min(WINDOW, n_pages * (page_rows // 4) + 1)
    page = page_rows // 4
    assert n_pages % 2 == 0 and page % CMP_STRIDE == 0
    segs = page // CMP_STRIDE
    i_, n_, s_ = np.meshgrid(np.arange(2), np.arange(segs), np.arange(CMP_STRIDE), indexing="ij")
    perm_np = np.zeros((2 * page, 2 * page), np.float32)
    perm_np[(s_ * 2 * segs + i_ * segs + n_).ravel(), (i_ * page + CMP_STRIDE * n_ + s_).ravel()] = 1.0
    perm = jnp.asarray(perm_np, BF16)
    hbm = pl.BlockSpec(memory_space=pl.ANY)
    full = lambda a: pl.BlockSpec(a.shape, lambda bi, pt: (0,) * a.ndim)
    per_b = lambda a: pl.BlockSpec((group,) + a.shape[1:], lambda bi, pt: (bi,) + (0,) * (a.ndim - 1))
    page_buf = pltpu.VMEM((2, group, n_pages, page_rows, LANES), F32)
    grid_spec = pltpu.PrefetchScalarGridSpec(
        num_scalar_prefetch=1,
        grid=(nb // group,),
        scratch_shapes=[page_buf, page_buf, pltpu.SemaphoreType.DMA((2, 2))],
        in_specs=([hbm, hbm]
                  + [per_b(win), per_b(q), per_b(gates), per_b(ks_new), per_b(kw_new),
                     full(wk_pairs), full(wv_pairs), full(pos_rows), full(gk),
                     full(bias_c), full(bias_s), full(bias_w), full(bias_new), full(ov), full(expand), full(perm)]),
        out_specs=[pl.BlockSpec((group,) + q.shape[1:], lambda bi, pt: (bi, 0, 0)),
                   pl.BlockSpec((group, win_rows_out, LANES), lambda bi, pt: (bi, 0, 0))],
    )
    return pl.pallas_call(
        kern,
        grid_spec=grid_spec,
        out_shape=(jax.ShapeDtypeStruct(q.shape, F32), jax.ShapeDtypeStruct((nb, win_rows_out, LANES), F32)),
        compiler_params=_cparams(("arbitrary",)),
        name="nsa_sample",
    )(page_table, cache_c, cache_s, win, q, gates, ks_new, kw_new,
      wk_pairs, wv_pairs, pos_rows, gk, bias_c, bias_s, bias_w, bias_new, ov, expand, perm)


def _merge_kernel(h_ref, oa_ref, ob_ref, wga_ref, wgb_ref, wg_ref, wn_ref, mix_ref, *cast_refs):
    h = h_ref[...]
    wga = wga_ref[...].astype(BF16)
    wgb = wgb_ref[...].astype(BF16)
    wg = wg_ref[...].astype(BF16)
    wn = wn_ref[...].astype(BF16)
    ga = _sigmoid(_dot_nt(h, wga))
    gb = _sigmoid(_dot_nt(h, wgb))
    mix_ref[...] = (ga * _dot(oa_ref[...], wg) + gb * _dot(ob_ref[...], wn)).astype(BF16)
    for ref, val in zip(cast_refs, (wga, wgb, wg, wn)):
        ref[...] = val


def _merge(h, oa, ob, wga, wgb, wg, wn, gate_rows=None):
    n, d = h.shape
    tm = 512 if n % 512 == 0 else n
    tn = 512
    emit = gate_rows is not None
    assert not emit or n == tm
    row = lambda w: pl.BlockSpec((tm, w), lambda i, j: (i, 0))
    col = lambda k: pl.BlockSpec((k, tn), lambda i, j: (0, j))
    colt = pl.BlockSpec((tn, d), lambda i, j: (j, 0))
    out_specs = [pl.BlockSpec((tm, tn), lambda i, j: (i, j))]
    out_shape = [jax.ShapeDtypeStruct((n, d), BF16)]
    gate_specs = [colt, colt]
    if emit:
        assert all(r % 8 == 0 for r in gate_rows)
        gate_specs = [pl.BlockSpec((pl.Element(tn), pl.Element(d)),
                                   lambda i, j, r=r: ((r // 8 + j * (tn // 8)) * 8, 0)) for r in gate_rows]
        out_specs += [colt, colt, col(wg.shape[0]), col(wn.shape[0])]
        out_shape += [jax.ShapeDtypeStruct((d, d), BF16), jax.ShapeDtypeStruct((d, d), BF16),
                      jax.ShapeDtypeStruct(wg.shape, BF16), jax.ShapeDtypeStruct(wn.shape, BF16)]
    return pl.pallas_call(
        _merge_kernel,
        grid=(n // tm, d // tn),
        in_specs=[row(d), row(oa.shape[1]), row(ob.shape[1])] + gate_specs + [col(wg.shape[0]), col(wn.shape[0])],
        out_specs=out_specs,
        out_shape=out_shape,
        compiler_params=_cparams(("arbitrary", "arbitrary")),
        name="merge",
    )(h, oa, ob, wga, wgb, wg, wn)


def _outproj_kernel(x_ref, mix_ref, wout_ref, gmlp_ref, x1_ref, hm_ref):
    x1 = x_ref[...] + _dot(mix_ref[...], wout_ref[...])
    x1_ref[...] = x1
    hm_ref[...] = (_rms(x1) * gmlp_ref[...]).astype(BF16)


def _outproj(x, mix, wout, gmlp):
    n, d = x.shape
    tm = 512 if n % 512 == 0 else n
    row = pl.BlockSpec((tm, d), lambda i: (i, 0))
    return pl.pallas_call(
        _outproj_kernel,
        grid=(n // tm,),
        in_specs=[row, row, pl.BlockSpec(wout.shape, lambda i: (0, 0)), pl.BlockSpec(gmlp.shape, lambda i: (0, 0))],
        out_specs=[row, row],
        out_shape=(jax.ShapeDtypeStruct((n, d), F32), jax.ShapeDtypeStruct((n, d), BF16)),
        compiler_params=_cparams(("arbitrary",)),
        name="outproj",
    )(x, mix, wout, gmlp)


def _ffn_kernel(hm_ref, x1_ref, wup_ref, wdown_ref, y_ref, *cast_refs):
    f = pl.program_id(1)

    @pl.when(f == 0)
    def _():
        y_ref[...] = x1_ref[...]

    wup = wup_ref[...].astype(BF16)
    wdown = wdown_ref[...].astype(BF16)
    hid = jnp.maximum(_dot(hm_ref[...], wup), 0.0)
    y_ref[...] += _dot((hid * hid).astype(BF16), wdown)
    if cast_refs:
        cast_refs[0][...] = wup
        cast_refs[1][...] = wdown


def _ffn(hm, x1, wup, wdown):
    n, d = hm.shape
    dff = wup.shape[1]
    tm = 1024 if n % 1024 == 0 else n
    emit = wup.dtype != BF16
    assert not emit or n == tm
    tf = 512
    row = pl.BlockSpec((tm, d), lambda i, f: (i, 0))
    up_spec = pl.BlockSpec((d, tf), lambda i, f: (0, f))
    down_spec = pl.BlockSpec((tf, d), lambda i, f: (f, 0))
    out_specs = [row]
    out_shape = [jax.ShapeDtypeStruct((n, d), F32)]
    if emit:
        out_specs += [up_spec, down_spec]
        out_shape += [jax.ShapeDtypeStruct(wup.shape, BF16), jax.ShapeDtypeStruct(wdown.shape, BF16)]
    return pl.pallas_call(
        _ffn_kernel,
        grid=(n // tm, dff // tf),
        in_specs=[row, row, up_spec, down_spec],
        out_specs=out_specs,
        out_shape=out_shape,
        compiler_params=_cparams(("arbitrary", "arbitrary")),
        name="ffn",
    )(hm, x1, wup, wdown)


def _bucket(rel, valid):
    n = np.maximum(rel, 0)
    max_exact = N_BUCKETS // 2
    nf = np.maximum(n, 1).astype(np.float32)
    large = max_exact + (np.log(nf / np.float32(max_exact)) / np.float32(math.log(MAX_DIST / max_exact))
                         * np.float32(N_BUCKETS - max_exact)).astype(np.int32)
    large = np.minimum(large, N_BUCKETS - 1)
    return np.where(valid, np.where(n < max_exact, n, large), -1).astype(np.int32)


def _bias_tables_kernel(rb_ref, *refs, shifts):
    n = len(shifts)
    nbk, nh = rb_ref.shape
    for b_ref, o_ref, shift in zip(refs[:n], refs[n:], shifts):
        b = b_ref[...]
        rows = []
        for h in range(nh):
            sh = rb_ref[nbk - 1, h] if shift else 0.0
            acc = jnp.full(b.shape, NEG, F32)
            for k in range(nbk):
                acc = jnp.where(b == k, rb_ref[k, h] - sh, acc)
            if len(o_ref.shape) == 3:
                o_ref[h] = acc
            else:
                rows.append(acc)
        if rows:
            o_ref[...] = jnp.concatenate(rows, axis=0)


def _bias_tables(rel_bias, buckets, shifts):
    nh = rel_bias.shape[1]
    shapes = [jax.ShapeDtypeStruct((nh,) + (b.shape if b.shape[0] > 1 else b.shape[1:]), F32) for b in buckets]
    vm = pl.BlockSpec(memory_space=pltpu.VMEM)
    return pl.pallas_call(
        functools.partial(_bias_tables_kernel, shifts=tuple(shifts)),
        in_specs=[pl.BlockSpec(memory_space=pltpu.SMEM)] + [vm] * len(buckets),
        out_specs=[vm] * len(buckets),
        out_shape=shapes,
        name="bias_tables",
    )(rel_bias, *[jnp.asarray(b) for b in buckets])


def _overlap(nc, ns):
    i = np.arange(nc)[:, None] * CMP_STRIDE
    j = np.arange(ns)[None, :] * SEL_LEN
    return ((i < j + SEL_LEN) & (i + CMP_LEN > j)).astype(np.float32)


def _compress_weights(w, pos):
    s = np.arange(0, CMP_STRIDE, 2)
    top = jnp.concatenate([w[s], w[CMP_STRIDE + s]], axis=2)
    bot = jnp.concatenate([w[s + 1], w[CMP_STRIDE + s + 1]], axis=2)
    tiles = jnp.concatenate([top, bot], axis=1).astype(BF16)
    row_a = jnp.concatenate([pos[s], pos[s + 1]], axis=1)
    row_b = jnp.concatenate([pos[CMP_STRIDE + s], pos[CMP_STRIDE + s + 1]], axis=1)
    rows = jnp.zeros((len(s), 16, 2 * HEAD_DIM), F32).at[:, 0].set(row_a).at[:, 1].set(row_b)
    return tiles, rows.astype(BF16)


def _dense_tail(x, h, oa, ob, wga, wgb, wg, wn, wout, gmlp, wup, wdown, gate_rows=None):
    mix, *cast_m = _merge(h, oa, ob, wga, wgb, wg, wn, gate_rows)
    x1, hm = _outproj(x, mix, wout, gmlp)
    y, *cast_f = _ffn(hm, x1, wup, wdown)
    return y, (cast_m or [wga, wgb, wg, wn]) + (cast_f or [wup, wdown])


def kernel(x_prompt, x_sample, cache_cmp_kv, cache_sel_kv, state_win_kv, page_table, rel_bias, g_mix_norm, w_in,
           g_sgu, w_sgu, b_sgu, g_q, g_k, pos_cmp_k, w_cmp_k, pos_cmp_v, w_cmp_v, w_proj_gmlp, w_proj_nsa, w_out,
           g_mlp_norm, w_up, w_down):
    depth = g_mix_norm.shape[0]
    assert depth == 1
    l = 0
    bsz, seq, d = x_prompt.shape
    nb = x_sample.shape[0]
    assert x_sample.shape[1] == 1 and seq % KPAD == 0
    d_gm = g_sgu.shape[1]
    n_heads = rel_bias.shape[1]
    d_nsa = n_heads * HEAD_DIM
    kvw_cols = 2 * (n_heads // 4) * HEAD_DIM
    n_gate = 3 * n_heads
    c_q = 2 * d_gm
    c_kv = c_q + d_nsa
    c_gate = c_kv + 3 * kvw_cols
    c_ga = c_gate + n_gate
    c_gb = c_ga + d

    w = jnp.swapaxes(w_in[l], 0, 1)
    n_col_tiles = -(-c_ga // 1024)
    wout = w_out[l].astype(BF16)
    gmix = g_mix_norm[l][None]
    gsgu = g_sgu[l][None]
    gq = g_q[l][None]
    gk = g_k[l]
    gmlp = g_mlp_norm[l][None]
    wk_pairs, posk_rows = _compress_weights(w_cmp_k[l], pos_cmp_k[l])
    wv_pairs, posv_rows = _compress_weights(w_cmp_v[l], pos_cmp_v[l])
    pos_rows = jnp.stack([posk_rows, posv_rows])
    n_groups = w_sgu.shape[1]
    b_exp = jnp.repeat(b_sgu[l].T, d_gm // n_groups, axis=1)
    w00 = jnp.repeat(w_sgu[l][:, 0, 0], d_gm // n_groups)[None]
    b00 = jnp.repeat(b_sgu[l][:, 0], d_gm // n_groups)[None]

    n_pool, page = cache_cmp_kv.shape[1], cache_cmp_kv.shape[2]
    n_pages = page_table.shape[1]
    past = n_pages * page
    nwin = state_win_kv.shape[2]
    n_cmp_s = (past + 1 - CMP_LEN) // CMP_STRIDE + 1
    n_sel_s = -(-(past + 1) // SEL_LEN)
    nseg_s = past // CMP_STRIDE
    qi = np.arange(QBLOCK)[:, None]
    rel_s = qi + QBLOCK - np.arange(2 * QBLOCK)[None, :]
    rel_w = qi + WINDOW - np.arange(WINDOW + QBLOCK)[None, :]
    rel_c = qi + (KC_FRONT * CMP_STRIDE - CMP_LEN + 1) - CMP_STRIDE * np.arange(LANES)[None, :]
    srel_c = (past - (np.arange(nseg_s) * CMP_STRIDE + CMP_LEN - 1))[None]
    srel_s = (past - np.arange(past))[None]
    srel_w = (nwin - np.arange(nwin))[None]
    buckets = [
        _bucket(rel_c, rel_c >= 0), _bucket(rel_s, rel_s >= 0), _bucket(rel_w, (rel_w >= 0) & (rel_w < WINDOW)),
        _bucket(srel_c, (srel_c >= 0) & (np.arange(nseg_s)[None] < n_cmp_s)), _bucket(srel_s, srel_s >= 0),
        _bucket(srel_w, srel_w < WINDOW), np.zeros((1, LANES), np.int32)]
    tab_c, tab_s, tab_w, bias_c, bias_s, bias_w, bias_new = _bias_tables(
        rel_bias, buckets, [True, True, False, False, False, False, False])
    tab_c, tab_s, tab_w = [t.reshape(2, (n_heads // 2) * QBLOCK, t.shape[-1]) for t in (tab_c, tab_s, tab_w)]

    xs = x_sample.reshape(nb, d)
    h_s, v_s, oa_s, q_s, kvc_s, kvs_s, kvw_s, gates_s, w_cat = _inproj(
        xs, gmix, w, gsgu, gq, w00, b00, chunked=False, n_col_tiles=n_col_tiles)
    ov_s = np.zeros((nseg_s, LANES), np.float32)
    ov_s[:n_cmp_s, :n_sel_s] = _overlap(n_cmp_s, n_sel_s)
    expand = jnp.asarray(np.arange(LANES)[:, None] == (np.arange(past)[None, :] // SEL_LEN), BF16)
    gates_h = jnp.pad(gates_s[:, :n_gate].reshape(nb, n_heads, 3), ((0, 0), (0, 0), (0, LANES - 3)))
    lin = lambda a: a[l].reshape(a.shape[1], -1, HEAD_DIM)
    win_lin = lin(state_win_kv)
    ob_s, win_new = _nsa_sample(
        page_table, lin(cache_cmp_kv), lin(cache_sel_kv), win_lin,
        q_s.astype(F32).reshape(nb, n_heads, HEAD_DIM), gates_h,
        kvs_s.reshape(nb, 4, HEAD_DIM), kvw_s.reshape(nb, 4, HEAD_DIM), wk_pairs, wv_pairs, pos_rows, gk,
        bias_c, bias_s, bias_w, bias_new, jnp.asarray(ov_s), expand)
    y_s, (wga, wgb, wg, wn, wup, wdown) = _dense_tail(
        xs, h_s, oa_s, ob_s.reshape(nb, d_nsa).astype(BF16), w, w, w_proj_gmlp[l], w_proj_nsa[l], wout, gmlp,
        w_up[l], w_down[l], gate_rows=(c_ga, c_gb))

    xp = x_prompt.reshape(bsz * seq, d)
    h_p, v_p, oa_p, q_p, kvc_p, kvs_p, kvw_p, gates_p = _inproj(
        xp, gmix, w_cat, gsgu, gq, w_sgu[l], b_exp, chunked=True, n_col_tiles=n_col_tiles)
    kvs3 = kvs_p.reshape(bsz, 4 * seq, HEAD_DIM)
    kvw3 = kvw_p.reshape(bsz, 4 * seq, HEAD_DIM)
    kvc3 = kvc_p.reshape(bsz, 4 * seq, HEAD_DIM)
    ks, vs, kw, vw = _kvprep(kvs3, kvw3, gk)
    kc, vc = _compress_prompt(kvc3, wk_pairs, wv_pairs, pos_rows, gk)
    n_cmp = (seq - CMP_LEN) // CMP_STRIDE + 1
    n_selb = seq // SEL_LEN
    ov_p = np.zeros((KC_ROWS, LANES), np.float32)
    ov_p[KC_FRONT:KC_FRONT + n_cmp, :n_selb] = _overlap(n_cmp, n_selb)
    ov_p = jnp.asarray(ov_p)
    gates_g = gates_p[:, :n_gate].reshape(bsz, seq, 2, n_gate // 2).transpose(0, 2, 1, 3)
    gates_g = jnp.pad(gates_g, ((0, 0), (0, 0), (0, 0), (0, LANES - n_gate // 2)))
    ob_p = _nsa_prompt(q_p.reshape(bsz, seq, d_nsa), gates_g, kc, vc, ov_p, ks, vs, kw, vw, tab_c, tab_s, tab_w)
    y_p, _ = _dense_tail(xp, h_p, oa_p, ob_p.reshape(bsz * seq, d_nsa), wga, wgb, wg, wn, wout, gmlp, wup, wdown)

    n_kv = n_heads // 4
    kv6 = lambda a, b_, t_: a.reshape(1, b_, t_, 2, n_kv, HEAD_DIM)
    nw_p = min(WINDOW, seq)
    last = ((seq - 1) // CHUNK) * CHUNK
    nw_s = min(WINDOW, past + 1)
    return (y_p.reshape(bsz, seq, d), y_s.reshape(nb, 1, d),
            kv6(kvc3, bsz, seq), kv6(kvs3, bsz, seq), kv6(kvw3[:, 4 * (seq - nw_p):], bsz, nw_p),
            v_p.reshape(bsz, seq, d_gm)[:, last:][None],
            kv6(kvc_s, nb, 1), kv6(kvs_s, nb, 1), kv6(win_new, nb, nw_s),
            v_s.reshape(1, nb, 1, d_gm))
```

```python
import functools
import math

import numpy as np
import jax
import jax.numpy as jnp
from jax import lax
from jax.experimental import pallas as pl
from jax.experimental.pallas import tpu as pltpu

F32 = jnp.float32
BF16 = jnp.bfloat16

HEAD_DIM = 128
CHUNK = 128
CMP_LEN = 32
CMP_STRIDE = 16
SEL_LEN = 64
N_SEL = 16
WINDOW = 512
N_BUCKETS = 32
MAX_DIST = 128
QBLOCK = 128
EPS = 1e-6
NEG = -1e30
HALF_NEG = -5e29
FORCE = 1e6

LANES = 128
PAD_LANE = 64
KPAD = 512
KC_FRONT = 16
KC_ROWS = 376
SAMPLE_GROUP = 2
VMEM_LIMIT = 56 * 1024 * 1024


def _cparams(sem):
    return pltpu.CompilerParams(dimension_semantics=sem, vmem_limit_bytes=VMEM_LIMIT)


def _dot(a, b):
    return jnp.dot(a, b, preferred_element_type=F32)


def _dot_nt(a, b):
    return lax.dot_general(a, b, (((1,), (1,)), ((), ())), preferred_element_type=F32)


def _dot_hilo(a, b_bf16):
    hi = a.astype(BF16)
    lo = (a - hi.astype(F32)).astype(BF16)
    return _dot(hi, b_bf16) + _dot(lo, b_bf16)


def _rms(x):
    return x * lax.rsqrt(jnp.mean(x * x, axis=-1, keepdims=True) + EPS)


def _gelu(x):
    c = math.sqrt(2.0 / math.pi)
    return 0.5 * x * (1.0 + jnp.tanh(c * (x + 0.044715 * (x * x * x))))


def _sigmoid(x):
    return 1.0 / (1.0 + jnp.exp(-x))


def _inproj_kernel(x_ref, gmix_ref, w_ref, gsgu_ref, gq_ref, wsg_ref, bsg_ref,
                   h_ref, v_ref, oa_ref, q_ref, kvc_ref, kvs_ref, kvw_ref, gate_ref, *rest, chunked, tm):
    h_s, u_s = rest[-2:]
    j = pl.program_id(1)

    half = w_ref.shape[0] // 2

    def z_cols(lo, hi):
        if len(rest) == 3:
            w = w_ref[lo:hi, :].astype(BF16)
            rest[0][lo:hi, :] = w
            return _dot_nt(h_s[...], w)
        return _dot_nt(h_s[...], w_ref[lo:hi, :])

    @pl.when(j == 0)
    def _():
        hb = (_rms(x_ref[...]) * gmix_ref[...]).astype(BF16)
        h_s[...] = hb
        h_ref[...] = hb
        za = z_cols(0, half)
        zb = z_cols(half, 2 * half)
        u_s[:, :half] = _gelu(za)
        u_s[:, half:] = _gelu(zb)

    @pl.when(j == 1)
    def _():
        za = z_cols(0, half)
        zb = z_cols(half, 2 * half)
        ga = _gelu(za)
        gb = _gelu(zb)
        ms = (jnp.sum(ga * ga, axis=-1, keepdims=True) + jnp.sum(gb * gb, axis=-1, keepdims=True)) / (2 * half)
        rstd = lax.rsqrt(ms + EPS)
        for c0, gz in ((0, ga), (half, gb)):
            v = gz * rstd * gsgu_ref[:, c0:c0 + half]
            v_ref[:, c0:c0 + half] = v
            if chunked:
                row = lax.broadcasted_iota(jnp.int32, (CHUNK, CHUNK), 0)
                col = lax.broadcasted_iota(jnp.int32, (CHUNK, CHUNK), 1)
                for g in range(half // LANES):
                    wm = jnp.where(row >= col, wsg_ref[c0 // LANES + g], 0.0).astype(BF16)
                    cs = slice(g * LANES, (g + 1) * LANES)
                    os_ = slice(c0 + g * LANES, c0 + (g + 1) * LANES)
                    for c in range(tm // CHUNK):
                        rs = slice(c * CHUNK, (c + 1) * CHUNK)
                        s = _dot(wm, v[rs, cs].astype(BF16)) + bsg_ref[:, os_]
                        oa_ref[rs, os_] = (u_s[rs, os_] * s).astype(BF16)
            else:
                cs = slice(c0, c0 + half)
                oa_ref[:, cs] = (u_s[:, cs] * (v * wsg_ref[:, cs] + bsg_ref[:, cs])).astype(BF16)

    @pl.when(j == 2)
    def _():
        scale = HEAD_DIM ** -0.5
        for c0 in (0, half):
            z = z_cols(c0, c0 + half)
            for hd in range(half // HEAD_DIM):
                cs = slice(hd * HEAD_DIM, (hd + 1) * HEAD_DIM)
                q_ref[:, c0 + hd * HEAD_DIM:c0 + (hd + 1) * HEAD_DIM] = (
                    _rms(z[:, cs]) * gq_ref[...] * scale).astype(BF16)

    def store_kv(ref, zz):
        for c in range(4):
            ref[pl.ds(c, tm, stride=4), :] = zz[:, c * LANES:(c + 1) * LANES]

    @pl.when(j == 3)
    def _():
        store_kv(kvc_ref, z_cols(0, half))
        store_kv(kvs_ref, z_cols(half, 2 * half))

    @pl.when(j == 4)
    def _():
        store_kv(kvw_ref, z_cols(0, half))
        gate_ref[...] = _sigmoid(z_cols(half, half + LANES))
        if len(rest) == 3:
            rest[0][half + LANES:, :] = jnp.zeros((half - LANES, w_ref.shape[1]), BF16)


def _inproj(x, gmix, w_cat, gsgu, gq, wsg, bsg, *, chunked, n_col_tiles):
    n, d = x.shape
    tm = 512 if n % 512 == 0 else n
    tn = 1024
    emit = w_cat.dtype != BF16
    assert not emit or n == tm
    kern = functools.partial(_inproj_kernel, chunked=chunked, tm=tm)
    full = lambda a: pl.BlockSpec(a.shape, lambda i, j: (0,) * a.ndim)
    row = lambda w: pl.BlockSpec((tm, w), lambda i, j: (i, 0))
    kvrow = pl.BlockSpec((4 * tm, LANES), lambda i, j: (i, 0))
    out_shapes = (
        jax.ShapeDtypeStruct((n, d), BF16),
        jax.ShapeDtypeStruct((n, 1024), F32),
        jax.ShapeDtypeStruct((n, 1024), BF16),
        jax.ShapeDtypeStruct((n, 1024), BF16),
        jax.ShapeDtypeStruct((4 * n, LANES), F32),
        jax.ShapeDtypeStruct((4 * n, LANES), F32),
        jax.ShapeDtypeStruct((4 * n, LANES), F32),
        jax.ShapeDtypeStruct((n, LANES), F32),
    )
    wspec = pl.BlockSpec((tn, d), lambda i, j: (j, 0))
    out_specs = [row(d), row(1024), row(1024), row(1024), kvrow, kvrow, kvrow, row(LANES)]
    if emit:
        out_shapes += (jax.ShapeDtypeStruct((n_col_tiles * tn, d), BF16),)
        out_specs.append(wspec)
    return pl.pallas_call(
        kern,
        grid=(n // tm, n_col_tiles),
        in_specs=[row(d), full(gmix), wspec, full(gsgu), full(gq), full(wsg), full(bsg)],
        out_specs=out_specs,
        out_shape=out_shapes,
        scratch_shapes=[pltpu.VMEM((tm, d), BF16), pltpu.VMEM((tm, 1024), F32)],
        compiler_params=_cparams(("arbitrary", "arbitrary")),
        name="inproj",
    )(x, gmix, w_cat, gsgu, gq, wsg, bsg)


def _kvprep_kernel(kvs_ref, kvw_ref, gk_ref, ks_ref, vs_ref, kw_ref, vw_ref):
    i = pl.program_id(1)
    rows = kvs_ref.shape[0] // 4
    col = lambda ref, c: ref[pl.ds(c, rows, stride=4), :]
    lane = lax.broadcasted_iota(jnp.int32, (rows, LANES), 1)
    row = lax.broadcasted_iota(jnp.int32, (rows, LANES), 0)

    @pl.when(i == 0)
    def _():
        aux = jnp.where(lane == PAD_LANE, 1.0, 0.0).astype(BF16)
        zk = jnp.zeros((rows, LANES), BF16)
        for g in range(2):
            ks_ref[g] = jnp.concatenate([zk, aux], axis=1)
            kw_ref[g] = jnp.concatenate([zk, aux], axis=1)
            vs_ref[g] = zk
            vw_ref[g] = zk

    @pl.when(i > 0)
    def _():
        blk = ((i - 1) * rows + row) // SEL_LEN
        onehot = jnp.where(lane == blk, 1.0, 0.0).astype(BF16)
        zaux = jnp.zeros((rows, LANES), BF16)
        for g in range(2):
            ks = (_rms(col(kvs_ref, g)) * gk_ref[1:2, :]).astype(BF16)
            kw = (_rms(col(kvw_ref, g)) * gk_ref[2:3, :]).astype(BF16)
            ks_ref[g] = jnp.concatenate([ks, onehot], axis=1)
            kw_ref[g] = jnp.concatenate([kw, zaux], axis=1)
            vs_ref[g] = col(kvs_ref, 2 + g).astype(BF16)
            vw_ref[g] = col(kvw_ref, 2 + g).astype(BF16)


def _kvprep(kvs, kvw, gk):
    b, t4, _ = kvs.shape
    t = t4 // 4
    rows = KPAD
    nblk = t // rows
    in_map = lambda bi, i: (bi, jnp.maximum(i - 1, 0), 0)
    out_map = lambda bi, i: (bi, 0, i, 0)
    kshape = jax.ShapeDtypeStruct((b, 2, KPAD + t, 2 * LANES), BF16)
    vshape = jax.ShapeDtypeStruct((b, 2, KPAD + t, LANES), BF16)
    return pl.pallas_call(
        _kvprep_kernel,
        grid=(b, nblk + 1),
        in_specs=[pl.BlockSpec((None, 4 * rows, LANES), in_map), pl.BlockSpec((None, 4 * rows, LANES), in_map),
                  pl.BlockSpec(gk.shape, lambda bi, i: (0, 0))],
        out_specs=[pl.BlockSpec((None, 2, rows, 2 * LANES), out_map), pl.BlockSpec((None, 2, rows, LANES), out_map),
                   pl.BlockSpec((None, 2, rows, 2 * LANES), out_map), pl.BlockSpec((None, 2, rows, LANES), out_map)],
        out_shape=(kshape, vshape, kshape, vshape),
        compiler_params=_cparams(("arbitrary", "arbitrary")),
        name="kvprep",
    )(kvs, kvw, gk)


def _compress_core(load_pair, nseg, wk_ref, wv_ref, pos_ref):
    w_refs = (wk_ref, wv_ref)
    accs = [jnp.zeros((2 * nseg + 16, 2 * LANES), F32) for _ in range(2)]
    for s2 in range(CMP_STRIDE // 2):
        for kv in range(2):
            parts = [load_pair(s2, 2 * kv + g) for g in range(2)]
            parts.append(pos_ref[kv, s2])
            lhs = jnp.concatenate(parts, axis=0)
            accs[kv] = accs[kv] + _dot(lhs, w_refs[kv][s2])
    outs = []
    for kv in range(2):
        y = accs[kv]
        post = y[2 * nseg:2 * nseg + 1, :LANES] + y[2 * nseg + 1:2 * nseg + 2, LANES:]
        per_g = []
        for g in range(2):
            y0 = y[g * nseg:(g + 1) * nseg, :LANES]
            y1 = y[g * nseg:(g + 1) * nseg, LANES:]
            per_g.append(y0 + pltpu.roll(y1, nseg - 1, 0) + post)
        outs.append(per_g)
    return outs


def _compress_prompt_kernel(x_ref, wk_ref, wv_ref, pos_ref, gk_ref, kc_ref, vc_ref):
    nseg = x_ref.shape[0] // (4 * CMP_STRIDE)
    load_x = lambda s, col: x_ref[pl.ds(4 * s + col, nseg, stride=4 * CMP_STRIDE), :]
    load_pair = lambda s2, col: jnp.concatenate([load_x(2 * s2, col), load_x(2 * s2 + 1, col)], axis=1).astype(BF16)
    ck, cv = _compress_core(load_pair, nseg, wk_ref, wv_ref, pos_ref)
    row = lax.broadcasted_iota(jnp.int32, (nseg, LANES), 0)
    lane = lax.broadcasted_iota(jnp.int32, (nseg, LANES), 1)
    valid = row < nseg - 1
    aux = jnp.where(valid, jnp.where(lane == row // 8, 1.0, 0.0), jnp.where(lane == PAD_LANE, 1.0, 0.0))
    back = KC_ROWS - KC_FRONT - nseg
    lane_f = lax.broadcasted_iota(jnp.int32, (KC_FRONT, LANES), 1)
    lane_b = lax.broadcasted_iota(jnp.int32, (back, LANES), 1)
    pad_f = jnp.concatenate([jnp.zeros((KC_FRONT, LANES), F32), jnp.where(lane_f == PAD_LANE, 1.0, 0.0)], axis=1)
    pad_b = jnp.concatenate([jnp.zeros((back, LANES), F32), jnp.where(lane_b == PAD_LANE, 1.0, 0.0)], axis=1)
    for g in range(2):
        kn = jnp.where(valid, _rms(ck[g]) * gk_ref[0:1, :], 0.0)
        kc_ref[g, 0:KC_FRONT, :] = pad_f
        kc_ref[g, KC_FRONT:KC_FRONT + nseg, :] = jnp.concatenate([kn, aux], axis=1)
        kc_ref[g, KC_FRONT + nseg:KC_ROWS, :] = pad_b
        vc_ref[g, 0:KC_FRONT, :] = jnp.zeros((KC_FRONT, LANES), F32)
        vc_ref[g, KC_FRONT:KC_FRONT + nseg, :] = jnp.where(valid, cv[g], 0.0)
        vc_ref[g, KC_FRONT + nseg:KC_ROWS, :] = jnp.zeros((back, LANES), F32)


def _compress_prompt(kvc, wk_pairs, wv_pairs, pos_rows, gk):
    b, t4, _ = kvc.shape
    full = lambda a: pl.BlockSpec(a.shape, lambda bi: (0,) * a.ndim)
    return pl.pallas_call(
        _compress_prompt_kernel,
        grid=(b,),
        in_specs=[pl.BlockSpec((None, t4, LANES), lambda bi: (bi, 0, 0)),
                  full(wk_pairs), full(wv_pairs), full(pos_rows), full(gk)],
        out_specs=[pl.BlockSpec((None, 2, KC_ROWS, 2 * LANES), lambda bi: (bi, 0, 0, 0)),
                   pl.BlockSpec((None, 2, KC_ROWS, LANES), lambda bi: (bi, 0, 0, 0))],
        out_shape=(jax.ShapeDtypeStruct((b, 2, KC_ROWS, 2 * LANES), F32),
                   jax.ShapeDtypeStruct((b, 2, KC_ROWS, LANES), F32)),
        compiler_params=_cparams(("arbitrary",)),
        name="compress_prompt",
    )(kvc, wk_pairs, wv_pairs, pos_rows, gk)


def _nsa_prompt_kernel(q_ref, gate_ref, kc_ref, vc_ref, ov_ref, ks_ref, vs_ref, kw_ref, vw_ref,
                       tc_ref, ts_ref, tw_ref, o_ref, m_s, l_s, acc_s, sa_s, sb_s, sw_s):
    qb = pl.program_id(2)
    t0 = qb * QBLOCK
    rq = 4 * QBLOCK
    q = q_ref[...]
    q4 = jnp.concatenate([q[:, r * LANES:(r + 1) * LANES] for r in range(4)], axis=0)
    lane = lax.broadcasted_iota(jnp.int32, (rq, LANES), 1)
    is_pad_lane = lane == PAD_LANE

    def q_aug(mb):
        return jnp.concatenate([q4, mb.astype(BF16)], axis=1)

    qa_pad = q_aug(jnp.where(is_pad_lane, NEG, 0.0))

    far_mask = ((lane < 32) & (lane >= qb - 2)) | is_pad_lane
    qa_cfar = q_aug(jnp.where(far_mask, NEG, 0.0))
    ncmp = 256
    near0 = pl.multiple_of(qb * 8, 8)
    k_far = kc_ref[KC_FRONT:KC_FRONT + ncmp, :].astype(BF16)
    k_near = kc_ref[pl.ds(near0, LANES), :].astype(BF16)
    s_far = _dot_nt(qa_cfar, k_far)
    s_near = _dot_nt(qa_pad, k_near) + tc_ref[...]
    w0 = pl.multiple_of(t0, QBLOCK)
    wlen = WINDOW + QBLOCK
    sw_s[...] = _dot_nt(qa_pad, kw_ref[pl.ds(w0, wlen), :]) + tw_ref[...]
    m = jnp.maximum(jnp.max(s_far, axis=-1, keepdims=True), jnp.max(s_near, axis=-1, keepdims=True))
    m = jnp.maximum(m, HALF_NEG)
    p_far = jnp.exp(s_far - m)
    p_near = jnp.exp(s_near - m)
    l = jnp.sum(p_far, axis=-1, keepdims=True) + jnp.sum(p_near, axis=-1, keepdims=True)
    linv = 1.0 / jnp.where(l > 0.0, l, 1.0)
    v_far = vc_ref[KC_FRONT:KC_FRONT + ncmp, :].astype(BF16)
    v_near = vc_ref[pl.ds(near0, LANES), :].astype(BF16)
    pb_far = p_far.astype(BF16)
    pb_near = p_near.astype(BF16)
    o_c = (_dot(pb_far, v_far) + _dot(pb_near, v_near)) * linv

    s = sw_s[...]
    m = jnp.max(s, axis=-1, keepdims=True)
    p = jnp.exp(s - m)
    l = jnp.sum(p, axis=-1, keepdims=True)
    o_w = _dot(p.astype(BF16), vw_ref[pl.ds(w0, wlen), :]) * (1.0 / l)

    ov_far = ov_ref[KC_FRONT:KC_FRONT + ncmp, :].astype(BF16)
    ov_near = ov_ref[pl.ds(near0, LANES), :].astype(BF16)
    imp4 = (_dot(pb_far, ov_far) + _dot(pb_near, ov_near)) * linv
    imp = sum(imp4[r * QBLOCK:(r + 1) * QBLOCK] for r in range(4))
    nblk = 64
    imp_t = imp.T[:nblk]
    blk = lax.broadcasted_iota(jnp.int32, (nblk, QBLOCK), 0)
    qpos = t0 + lax.broadcasted_iota(jnp.int32, (nblk, QBLOCK), 1)
    cur = qpos // SEL_LEN
    forced = (blk == 0) | (blk == cur) | (blk == cur - 1)
    eligible = blk * SEL_LEN <= qpos
    val = jnp.where(forced, FORCE, jnp.where(eligible, imp_t, -1.0))
    slab = 8
    vals = [val[v * slab:(v + 1) * slab] for v in range(nblk // slab)]
    ranks = [jnp.zeros((slab, QBLOCK), F32) for _ in vals]
    row_in_slab = lax.broadcasted_iota(jnp.int32, (slab, QBLOCK), 0)
    for i in range(nblk):
        vi, ii = divmod(i, slab)
        ri = jnp.broadcast_to(vals[vi][ii:ii + 1, :], (slab, QBLOCK))
        for v in range(len(vals)):
            if v < vi:
                beats = ri > vals[v]
            elif v > vi:
                beats = ri >= vals[v]
            else:
                beats = (ri > vals[v]) | ((ri == vals[v]) & (row_in_slab > ii))
            ranks[v] = ranks[v] + jnp.where(beats, 1.0, 0.0)
    rank = jnp.concatenate(ranks, axis=0)
    mb_t = jnp.where(rank < float(N_SEL), 0.0, NEG)
    row2 = lax.broadcasted_iota(jnp.int32, (LANES - nblk, QBLOCK), 0)
    mb_t = jnp.concatenate([mb_t, jnp.where(row2 == PAD_LANE - nblk, NEG, 0.0)], axis=0)
    mb = mb_t.T
    mb4 = jnp.concatenate([mb] * 4, axis=0)
    qa_snear = q_aug(mb4)
    qa_sfar = q_aug(jnp.where((lane < nblk) & (lane >= 2 * qb - 2), NEG, mb4))

    far_tile = 512
    n_far = (jnp.maximum(qb - 1, 0) * QBLOCK + far_tile - 1) // far_tile

    def far_logits(j):
        r0 = pl.multiple_of(KPAD + j * far_tile, far_tile)
        return _dot_nt(qa_sfar, ks_ref[pl.ds(r0, far_tile), :])

    sn0 = pl.multiple_of(t0 + KPAD - QBLOCK, QBLOCK)
    s = _dot_nt(qa_snear, ks_ref[pl.ds(sn0, 2 * QBLOCK), :]) + ts_ref[...]
    sa_s[...] = far_logits(0)
    m = jnp.max(s, axis=-1, keepdims=True)
    p = jnp.exp(s - m)
    l = jnp.sum(p, axis=-1, keepdims=True)
    acc = _dot(p.astype(BF16), vs_ref[pl.ds(sn0, 2 * QBLOCK), :])

    m_s[...] = m
    l_s[...] = l
    acc_s[...] = acc
    last_tile = ks_ref.shape[0] // far_tile - 2

    def far_update(j, s_ref):
        r0 = pl.multiple_of(KPAD + j * far_tile, far_tile)
        s = s_ref[...]
        m_old = m_s[...]
        m_new = jnp.maximum(m_old, jnp.max(s, axis=-1, keepdims=True))
        alpha = jnp.exp(m_old - m_new)
        p = jnp.exp(s - m_new)
        l_s[...] = alpha * l_s[...] + jnp.sum(p, axis=-1, keepdims=True)
        acc_s[...] = alpha * acc_s[...] + _dot(p.astype(BF16), vs_ref[pl.ds(r0, far_tile), :])
        m_s[...] = m_new

    def far_body(i, carry):
        a = 2 * i
        sb_s[...] = far_logits(a + 1)
        far_update(a, sa_s)
        sa_s[...] = far_logits(jnp.minimum(a + 2, last_tile))
        far_update(a + 1, sb_s)
        return carry

    lax.fori_loop(0, n_far // 2, far_body, 0)

    @pl.when(n_far % 2 == 1)
    def _():
        far_update(n_far - 1, sa_s)

    o_s = acc_s[...] * (1.0 / l_s[...])

    gt = gate_ref[...]
    outs = []
    for r in range(4):
        rs = slice(r * QBLOCK, (r + 1) * QBLOCK)
        outs.append(gt[:, 3 * r:3 * r + 1] * o_c[rs] + gt[:, 3 * r + 1:3 * r + 2] * o_s[rs]
                    + gt[:, 3 * r + 2:3 * r + 3] * o_w[rs])
    o_ref[...] = jnp.concatenate(outs, axis=1).astype(BF16)


def _nsa_prompt(q, gates_g, kc, vc, ov, ks, vs, kw, vw, tab_c, tab_s, tab_w):
    b, t, _ = q.shape
    nqb = t // QBLOCK
    kvspec = lambda a: pl.BlockSpec((None, None) + a.shape[2:], lambda bi, g, i: (bi, g, 0, 0))
    tabspec = lambda a: pl.BlockSpec((None,) + a.shape[1:], lambda bi, g, i: (g, 0, 0))
    return pl.pallas_call(
        _nsa_prompt_kernel,
        grid=(b, 2, nqb),
        in_specs=[pl.BlockSpec((None, QBLOCK, 512), lambda bi, g, i: (bi, i, g)),
                  pl.BlockSpec((None, None, QBLOCK, LANES), lambda bi, g, i: (bi, g, i, 0)),
                  kvspec(kc), kvspec(vc), pl.BlockSpec(ov.shape, lambda bi, g, i: (0, 0)),
                  kvspec(ks), kvspec(vs), kvspec(kw), kvspec(vw),
                  tabspec(tab_c), tabspec(tab_s), tabspec(tab_w)],
        out_specs=pl.BlockSpec((None, QBLOCK, 512), lambda bi, g, i: (bi, i, g)),
        out_shape=jax.ShapeDtypeStruct((b, t, 1024), BF16),
        scratch_shapes=[pltpu.VMEM((4 * QBLOCK, 1), F32), pltpu.VMEM((4 * QBLOCK, 1), F32),
                        pltpu.VMEM((4 * QBLOCK, LANES), F32),
                        pltpu.VMEM((4 * QBLOCK, 512), F32), pltpu.VMEM((4 * QBLOCK, 512), F32),
                        pltpu.VMEM((4 * QBLOCK, WINDOW + QBLOCK), F32)],
        compiler_params=_cparams(("arbitrary", "arbitrary", "arbitrary")),
        name="nsa_prompt",
    )(q, gates_g, kc, vc, ov, ks, vs, kw, vw, tab_c, tab_s, tab_w)


def _decode_logits(qf, g0rows, gk_row, k_tiles, bias):
    qg = qf * gk_row
    zero = jnp.zeros_like(qg)
    qbd = jnp.concatenate([jnp.where(g0rows, qg, zero), jnp.where(g0rows, zero, qg)], axis=1).astype(BF16)
    one = jnp.ones_like(qg)
    ones_bd = jnp.concatenate([jnp.where(g0rows, one, zero), jnp.where(g0rows, zero, one)], axis=1).astype(BF16)
    s_parts, q_parts = [], []
    for kt in k_tiles:
        k = kt()
        s_parts.append(_dot_nt(qbd, k.astype(BF16)))
        q_parts.append(_dot_nt(ones_bd, (k * k).astype(BF16)))
    s = jnp.concatenate(s_parts, axis=1)
    ssq = jnp.concatenate(q_parts, axis=1)
    return s * lax.rsqrt(ssq * (1.0 / HEAD_DIM) + EPS) + bias, qg


def _decode_finish(s, qg, g0rows, v_tiles, key_mask, new_row, bias_new):
    if key_mask is not None:
        s = jnp.where(key_mask > 0.5, s, NEG)
    bc = lambda c: jnp.broadcast_to(new_row[c:c + 1, :], qg.shape)
    k_new = jnp.where(g0rows, bc(0), bc(1))
    v_new = jnp.where(g0rows, bc(2), bc(3))
    s_new = jnp.sum(qg * _rms(k_new), axis=-1, keepdims=True) + bias_new
    m = jnp.maximum(jnp.max(s, axis=-1, keepdims=True), s_new)
    p = jnp.exp(s - m)
    p_new = jnp.exp(s_new - m)
    l = jnp.sum(p, axis=-1, keepdims=True) + p_new
    o2 = jnp.zeros((qg.shape[0], 2 * LANES), F32)
    for i, vt in enumerate(v_tiles):
        o2 = o2 + _dot(p[:, i * LANES:(i + 1) * LANES].astype(BF16), vt().astype(BF16))
    o = jnp.where(g0rows, o2[:, :LANES], o2[:, LANES:]) + p_new * v_new
    return o * (1.0 / l)


def _nsa_sample_kernel(pt_ref, cache_c_hbm, cache_s_hbm, *refs, n_pages, n_sel_blocks, group):
    per_b_in, consts, outs = refs[:5], refs[5:-5], refs[-5:-3]
    buf_c, buf_s, sem = refs[-3:]
    step = pl.program_id(0)
    slot = step % 2

    def page_copies(src_step, dst_slot, for_wait):
        copies = []
        for e in range(group):
            for p in range(n_pages):
                pg = 0 if for_wait else pt_ref[src_step * group + e, p]
                copies.append(pltpu.make_async_copy(cache_c_hbm.at[pg], buf_c.at[dst_slot, e, p], sem.at[dst_slot, 0]))
                copies.append(pltpu.make_async_copy(cache_s_hbm.at[pg], buf_s.at[dst_slot, e, p], sem.at[dst_slot, 1]))
        return copies

    @pl.when(step == 0)
    def _():
        for c in page_copies(0, 0, False):
            c.start()

    @pl.when(step + 1 < pl.num_programs(0))
    def _():
        for c in page_copies(step + 1, 1 - slot, False):
            c.start()

    for c in page_copies(step, slot, True):
        c.wait()

    chains = [_nsa_sample_one([buf_c.at[slot, bb, p] for p in range(n_pages)],
                              [buf_s.at[slot, bb, p] for p in range(n_pages)],
                              *[r.at[bb] for r in per_b_in], *consts, *[r.at[bb] for r in outs],
                              n_pages=n_pages, n_sel_blocks=n_sel_blocks, chain_first=bb % 2 == 0)
              for bb in range(group)]
    while chains:
        chains = [c for c in chains if next(c, "done") != "done"]


def _nsa_sample_one(pages_c, pages_s, win_ref, q_ref, gate_ref, ksn_ref, kwn_ref, wk_ref, wv_ref, pos_ref, gk_ref,
                    bc_ref, bs_ref, bw_ref, bnew_ref, ov_ref, e_ref, perm_ref, o_ref, wout_ref, *,
                    n_pages, n_sel_blocks, chain_first):
    qf = q_ref[...]
    nh = qf.shape[0]
    g0rows = lax.broadcasted_iota(jnp.int32, (nh, LANES), 0) < nh // 2
    lane = lax.broadcasted_iota(jnp.int32, (nh, LANES), 1)

    page = pages_c[0].shape[0] // 4
    nseg = n_pages * page // CMP_STRIDE
    perm = perm_ref[...]
    regrouped = []
    for pp in range(n_pages // 2):
        per_cp = []
        for cp in range(2):
            blk = jnp.concatenate(
                [jnp.concatenate([pages_c[2 * pp + i][pl.ds(2 * cp + c, page, stride=4), :] for c in range(2)], axis=1)
                 for i in range(2)], axis=0).astype(BF16)
            per_cp.append(_dot(perm, blk).astype(BF16))
        regrouped.append(per_cp)
    rows_per_tap = 2 * page // CMP_STRIDE
    yield

    def tap_rows(s, col):
        lo, hi = s * rows_per_tap, (s + 1) * rows_per_tap
        return jnp.concatenate([regrouped[pp][col // 2][lo:hi, (col % 2) * LANES:(col % 2 + 1) * LANES]
                                for pp in range(n_pages // 2)], axis=0)

    load_pair = lambda s2, col: jnp.concatenate([tap_rows(2 * s2, col), tap_rows(2 * s2 + 1, col)], axis=1)
    ck, cv = _compress_core(load_pair, nseg, wk_ref, wv_ref, pos_ref)
    yield

    def selection_chain():
        kcn = jnp.concatenate([(_rms(ck[g]) * gk_ref[0:1, :]).astype(BF16) for g in range(2)], axis=0)
        s2 = _dot_nt(qf.astype(BF16), kcn)
        s = jnp.where(g0rows, s2[:, :nseg], s2[:, nseg:]) + bc_ref[...]
        m = jnp.max(s, axis=-1, keepdims=True)
        p = jnp.exp(s - m)
        linv = 1.0 / jnp.sum(p, axis=-1, keepdims=True)
        rowv = lax.broadcasted_iota(jnp.int32, (nseg, LANES), 0) < nseg - 1
        pb = p.astype(BF16)
        oc = [_dot(pb, jnp.where(rowv, cv[g], 0.0).astype(BF16)) for g in range(2)]
        o_c = jnp.where(g0rows, oc[0], oc[1]) * linv
        pn = p * linv
        s0 = jnp.sum(jnp.where(g0rows, pn, 0.0), axis=0, keepdims=True)
        s1 = jnp.sum(jnp.where(g0rows, 0.0, pn), axis=0, keepdims=True)
        psum = jnp.where(g0rows, jnp.broadcast_to(s0, pn.shape), jnp.broadcast_to(s1, pn.shape))
        imp = _dot_hilo(psum, ov_ref[...].astype(BF16))
        cur = n_sel_blocks - 1
        forced = (lane == 0) | (lane == cur) | (lane == cur - 1)
        val = jnp.where(lane >= n_sel_blocks, -2.0, jnp.where(forced, FORCE, imp))
        rank = jnp.zeros_like(val)
        for i in range(n_sel_blocks):
            ci = jnp.broadcast_to(val[:, i:i + 1], val.shape)
            beats = (ci > val) | ((ci == val) & (lane > i))
            rank = rank + jnp.where(beats, 1.0, 0.0)
        sel = jnp.where((rank < float(N_SEL)) & (lane < n_sel_blocks), 1.0, 0.0)
        return o_c, _dot(sel.astype(BF16), e_ref[...])

    def tiles(ref_list, col):
        def tile(rf, i):
            rows = lambda c: rf[pl.ds(4 * LANES * i + c, LANES, stride=4), :]
            return lambda: jnp.concatenate([rows(col), rows(col + 1)], axis=1)
        return [tile(rf, i) for rf in ref_list for i in range(rf.shape[0] // (4 * LANES))]

    b_new = bnew_ref[:, 0:1]

    def mask_free_work():
        s_sel, qg_sel = _decode_logits(qf, g0rows, gk_ref[1:2, :], tiles(pages_s, 0), bs_ref[...])
        s_win, qg_win = _decode_logits(qf, g0rows, gk_ref[2:3, :], tiles([win_ref], 0), bw_ref[...])
        o_w = _decode_finish(s_win, qg_win, g0rows, tiles([win_ref], 2), None, kwn_ref[...], b_new)
        return s_sel, qg_sel, o_w

    if chain_first:
        o_c, key_mask = selection_chain()
        yield
        s_sel, qg_sel, o_w = mask_free_work()
    else:
        s_sel, qg_sel, o_w = mask_free_work()
        yield
        o_c, key_mask = selection_chain()
    yield
    o_s = _decode_finish(s_sel, qg_sel, g0rows, tiles(pages_s, 2), key_mask, ksn_ref[...], b_new)
    yield
    gt = gate_ref[...]
    o_ref[...] = gt[:, 0:1] * o_c + gt[:, 1:2] * o_s + gt[:, 2:3] * o_w

    keep = wout_ref.shape[0] - 4
    drop = win_ref.shape[0] - keep
    wout_ref[0:keep, :] = win_ref[drop:drop + keep, :]
    wout_ref[keep:keep + 4, :] = kwn_ref[...]


def _nsa_sample(page_table, cache_c, cache_s, win, q, gates, ks_new, kw_new, wk_pairs, wv_pairs, pos_rows, gk,
                bias_c, bias_s, bias_w, bias_new, ov, expand):
    nb, n_pages = page_table.shape
    page_rows = cache_c.shape[1]
    n_sel_blocks = -(-(n_pages * (page_rows // 4) + 1) // SEL_LEN)
    group = SAMPLE_GROUP if nb % SAMPLE_GROUP == 0 else 1
    kern = functools.partial(_nsa_sample_kernel, n_pages=n_pages, n_sel_blocks=n_sel_blocks, group=group)
    win_rows_out = 4 * min(WINDOW, n_pages * (page_rows // 4) + 1)
    page = page_rows // 4
    assert n_pages % 2 == 0 and page % CMP_STRIDE == 0
    segs = page // CMP_STRIDE
    i_, n_, s_ = np.meshgrid(np.arange(2), np.arange(segs), np.arange(CMP_STRIDE), indexing="ij")
    perm_np = np.zeros((2 * page, 2 * page), np.float32)
    perm_np[(s_ * 2 * segs + i_ * segs + n_).ravel(), (i_ * page + CMP_STRIDE * n_ + s_).ravel()] = 1.0
    perm = jnp.asarray(perm_np, BF16)
    hbm = pl.BlockSpec(memory_space=pl.ANY)
    full = lambda a: pl.BlockSpec(a.shape, lambda bi, pt: (0,) * a.ndim)
    per_b = lambda a: pl.BlockSpec((group,) + a.shape[1:], lambda bi, pt: (bi,) + (0,) * (a.ndim - 1))
    page_buf = pltpu.VMEM((2, group, n_pages, page_rows, LANES), F32)
    grid_spec = pltpu.PrefetchScalarGridSpec(
        num_scalar_prefetch=1,
        grid=(nb // group,),
        scratch_shapes=[page_buf, page_buf, pltpu.SemaphoreType.DMA((2, 2))],
        in_specs=([hbm, hbm]
                  + [per_b(win), per_b(q), per_b(gates), per_b(ks_new), per_b(kw_new),
                     full(wk_pairs), full(wv_pairs), full(pos_rows), full(gk),
                     full(bias_c), full(bias_s), full(bias_w), full(bias_new), full(ov), full(expand), full(perm)]),
        out_specs=[pl.BlockSpec((group,) + q.shape[1:], lambda bi, pt: (bi, 0, 0)),
                   pl.BlockSpec((group, win_rows_out, LANES), lambda bi, pt: (bi, 0, 0))],
    )
    return pl.pallas_call(
        kern,
        grid_spec=grid_spec,
        out_shape=(jax.ShapeDtypeStruct(q.shape, F32), jax.ShapeDtypeStruct((nb, win_rows_out, LANES), F32)),
        compiler_params=_cparams(("arbitrary",)),
        name="nsa_sample",
    )(page_table, cache_c, cache_s, win, q, gates, ks_new, kw_new,
      wk_pairs, wv_pairs, pos_rows, gk, bias_c, bias_s, bias_w, bias_new, ov, expand, perm)


def _merge_kernel(h_ref, oa_ref, ob_ref, wga_ref, wgb_ref, wg_ref, wn_ref, mix_ref, *cast_refs):
    h = h_ref[...]
    wga = wga_ref[...].astype(BF16)
    wgb = wgb_ref[...].astype(BF16)
    wg = wg_ref[...].astype(BF16)
    wn = wn_ref[...].astype(BF16)
    ga = _sigmoid(_dot_nt(h, wga))
    gb = _sigmoid(_dot_nt(h, wgb))
    mix_ref[...] = (ga * _dot(oa_ref[...], wg) + gb * _dot(ob_ref[...], wn)).astype(BF16)
    for ref, val in zip(cast_refs, (wga, wgb, wg, wn)):
        ref[...] = val


def _merge(h, oa, ob, wga, wgb, wg, wn, gate_rows=None):
    n, d = h.shape
    tm = 512 if n % 512 == 0 else n
    tn = 512
    emit = gate_rows is not None
    assert not emit or n == tm
    row = lambda w: pl.BlockSpec((tm, w), lambda i, j: (i, 0))
    col = lambda k: pl.BlockSpec((k, tn), lambda i, j: (0, j))
    colt = pl.BlockSpec((tn, d), lambda i, j: (j, 0))
    out_specs = [pl.BlockSpec((tm, tn), lambda i, j: (i, j))]
    out_shape = [jax.ShapeDtypeStruct((n, d), BF16)]
    gate_specs = [colt, colt]
    if emit:
        assert all(r % 8 == 0 for r in gate_rows)
        gate_specs = [pl.BlockSpec((pl.Element(tn), pl.Element(d)),
                                   lambda i, j, r=r: ((r // 8 + j * (tn // 8)) * 8, 0)) for r in gate_rows]
        out_specs += [colt, colt, col(wg.shape[0]), col(wn.shape[0])]
        out_shape += [jax.ShapeDtypeStruct((d, d), BF16), jax.ShapeDtypeStruct((d, d), BF16),
                      jax.ShapeDtypeStruct(wg.shape, BF16), jax.ShapeDtypeStruct(wn.shape, BF16)]
    return pl.pallas_call(
        _merge_kernel,
        grid=(n // tm, d // tn),
        in_specs=[row(d), row(oa.shape[1]), row(ob.shape[1])] + gate_specs + [col(wg.shape[0]), col(wn.shape[0])],
        out_specs=out_specs,
        out_shape=out_shape,
        compiler_params=_cparams(("arbitrary", "arbitrary")),
        name="merge",
    )(h, oa, ob, wga, wgb, wg, wn)


def _outproj_kernel(x_ref, mix_ref, wout_ref, gmlp_ref, x1_ref, hm_ref):
    x1 = x_ref[...] + _dot(mix_ref[...], wout_ref[...])
    x1_ref[...] = x1
    hm_ref[...] = (_rms(x1) * gmlp_ref[...]).astype(BF16)


def _outproj(x, mix, wout, gmlp):
    n, d = x.shape
    tm = 512 if n % 512 == 0 else n
    row = pl.BlockSpec((tm, d), lambda i: (i, 0))
    return pl.pallas_call(
        _outproj_kernel,
        grid=(n // tm,),
        in_specs=[row, row, pl.BlockSpec(wout.shape, lambda i: (0, 0)), pl.BlockSpec(gmlp.shape, lambda i: (0, 0))],
        out_specs=[row, row],
        out_shape=(jax.ShapeDtypeStruct((n, d), F32), jax.ShapeDtypeStruct((n, d), BF16)),
        compiler_params=_cparams(("arbitrary",)),
        name="outproj",
    )(x, mix, wout, gmlp)


def _ffn_kernel(hm_ref, x1_ref, wup_ref, wdown_ref, y_ref, *cast_refs):
    f = pl.program_id(1)

    @pl.when(f == 0)
    def _():
        y_ref[...] = x1_ref[...]

    wup = wup_ref[...].astype(BF16)
    wdown = wdown_ref[...].astype(BF16)
    hid = jnp.maximum(_dot(hm_ref[...], wup), 0.0)
    y_ref[...] += _dot((hid * hid).astype(BF16), wdown)
    if cast_refs:
        cast_refs[0][...] = wup
        cast_refs[1][...] = wdown


def _ffn(hm, x1, wup, wdown):
    n, d = hm.shape
    dff = wup.shape[1]
    tm = 512 if n % 512 == 0 else n
    emit = wup.dtype != BF16
    assert not emit or n == tm
    tf = 512 if emit else 1024
    row = pl.BlockSpec((tm, d), lambda i, f: (i, 0))
    up_spec = pl.BlockSpec((d, tf), lambda i, f: (0, f))
    down_spec = pl.BlockSpec((tf, d), lambda i, f: (f, 0))
    out_specs = [row]
    out_shape = [jax.ShapeDtypeStruct((n, d), F32)]
    if emit:
        out_specs += [up_spec, down_spec]
        out_shape += [jax.ShapeDtypeStruct(wup.shape, BF16), jax.ShapeDtypeStruct(wdown.shape, BF16)]
    return pl.pallas_call(
        _ffn_kernel,
        grid=(n // tm, dff // tf),
        in_specs=[row, row, up_spec, down_spec],
        out_specs=out_specs,
        out_shape=out_shape,
        compiler_params=_cparams(("arbitrary", "arbitrary")),
        name="ffn",
    )(hm, x1, wup, wdown)


def _bucket(rel, valid):
    n = np.maximum(rel, 0)
    max_exact = N_BUCKETS // 2
    nf = np.maximum(n, 1).astype(np.float32)
    large = max_exact + (np.log(nf / np.float32(max_exact)) / np.float32(math.log(MAX_DIST / max_exact))
                         * np.float32(N_BUCKETS - max_exact)).astype(np.int32)
    large = np.minimum(large, N_BUCKETS - 1)
    return np.where(valid, np.where(n < max_exact, n, large), -1).astype(np.int32)


def _bias_tables_kernel(rb_ref, *refs, shifts):
    n = len(shifts)
    nbk, nh = rb_ref.shape
    for b_ref, o_ref, shift in zip(refs[:n], refs[n:], shifts):
        b = b_ref[...]
        rows = []
        for h in range(nh):
            sh = rb_ref[nbk - 1, h] if shift else 0.0
            acc = jnp.full(b.shape, NEG, F32)
            for k in range(nbk):
                acc = jnp.where(b == k, rb_ref[k, h] - sh, acc)
            if len(o_ref.shape) == 3:
                o_ref[h] = acc
            else:
                rows.append(acc)
        if rows:
            o_ref[...] = jnp.concatenate(rows, axis=0)


def _bias_tables(rel_bias, buckets, shifts):
    nh = rel_bias.shape[1]
    shapes = [jax.ShapeDtypeStruct((nh,) + (b.shape if b.shape[0] > 1 else b.shape[1:]), F32) for b in buckets]
    vm = pl.BlockSpec(memory_space=pltpu.VMEM)
    return pl.pallas_call(
        functools.partial(_bias_tables_kernel, shifts=tuple(shifts)),
        in_specs=[pl.BlockSpec(memory_space=pltpu.SMEM)] + [vm] * len(buckets),
        out_specs=[vm] * len(buckets),
        out_shape=shapes,
        name="bias_tables",
    )(rel_bias, *[jnp.asarray(b) for b in buckets])


def _overlap(nc, ns):
    i = np.arange(nc)[:, None] * CMP_STRIDE
    j = np.arange(ns)[None, :] * SEL_LEN
    return ((i < j + SEL_LEN) & (i + CMP_LEN > j)).astype(np.float32)


def _compress_weights(w, pos):
    s = np.arange(0, CMP_STRIDE, 2)
    top = jnp.concatenate([w[s], w[CMP_STRIDE + s]], axis=2)
    bot = jnp.concatenate([w[s + 1], w[CMP_STRIDE + s + 1]], axis=2)
    tiles = jnp.concatenate([top, bot], axis=1).astype(BF16)
    row_a = jnp.concatenate([pos[s], pos[s + 1]], axis=1)
    row_b = jnp.concatenate([pos[CMP_STRIDE + s], pos[CMP_STRIDE + s + 1]], axis=1)
    rows = jnp.zeros((len(s), 16, 2 * HEAD_DIM), F32).at[:, 0].set(row_a).at[:, 1].set(row_b)
    return tiles, rows.astype(BF16)


def _dense_tail(x, h, oa, ob, wga, wgb, wg, wn, wout, gmlp, wup, wdown, gate_rows=None):
    mix, *cast_m = _merge(h, oa, ob, wga, wgb, wg, wn, gate_rows)
    x1, hm = _outproj(x, mix, wout, gmlp)
    y, *cast_f = _ffn(hm, x1, wup, wdown)
    return y, (cast_m or [wga, wgb, wg, wn]) + (cast_f or [wup, wdown])


def kernel(x_prompt, x_sample, cache_cmp_kv, cache_sel_kv, state_win_kv, page_table, rel_bias, g_mix_norm, w_in,
           g_sgu, w_sgu, b_sgu, g_q, g_k, pos_cmp_k, w_cmp_k, pos_cmp_v, w_cmp_v, w_proj_gmlp, w_proj_nsa, w_out,
           g_mlp_norm, w_up, w_down):
    depth = g_mix_norm.shape[0]
    assert depth == 1
    l = 0
    bsz, seq, d = x_prompt.shape
    nb = x_sample.shape[0]
    assert x_sample.shape[1] == 1 and seq % KPAD == 0
    d_gm = g_sgu.shape[1]
    n_heads = rel_bias.shape[1]
    d_nsa = n_heads * HEAD_DIM
    kvw_cols = 2 * (n_heads // 4) * HEAD_DIM
    n_gate = 3 * n_heads
    c_q = 2 * d_gm
    c_kv = c_q + d_nsa
    c_gate = c_kv + 3 * kvw_cols
    c_ga = c_gate + n_gate
    c_gb = c_ga + d

    w = jnp.swapaxes(w_in[l], 0, 1)
    n_col_tiles = -(-c_ga // 1024)
    wout = w_out[l].astype(BF16)
    gmix = g_mix_norm[l][None]
    gsgu = g_sgu[l][None]
    gq = g_q[l][None]
    gk = g_k[l]
    gmlp = g_mlp_norm[l][None]
    wk_pairs, posk_rows = _compress_weights(w_cmp_k[l], pos_cmp_k[l])
    wv_pairs, posv_rows = _compress_weights(w_cmp_v[l], pos_cmp_v[l])
    pos_rows = jnp.stack([posk_rows, posv_rows])
    n_groups = w_sgu.shape[1]
    b_exp = jnp.repeat(b_sgu[l].T, d_gm // n_groups, axis=1)
    w00 = jnp.repeat(w_sgu[l][:, 0, 0], d_gm // n_groups)[None]
    b00 = jnp.repeat(b_sgu[l][:, 0], d_gm // n_groups)[None]

    n_pool, page = cache_cmp_kv.shape[1], cache_cmp_kv.shape[2]
    n_pages = page_table.shape[1]
    past = n_pages * page
    nwin = state_win_kv.shape[2]
    n_cmp_s = (past + 1 - CMP_LEN) // CMP_STRIDE + 1
    n_sel_s = -(-(past + 1) // SEL_LEN)
    nseg_s = past // CMP_STRIDE
    qi = np.arange(QBLOCK)[:, None]
    rel_s = qi + QBLOCK - np.arange(2 * QBLOCK)[None, :]
    rel_w = qi + WINDOW - np.arange(WINDOW + QBLOCK)[None, :]
    rel_c = qi + (KC_FRONT * CMP_STRIDE - CMP_LEN + 1) - CMP_STRIDE * np.arange(LANES)[None, :]
    srel_c = (past - (np.arange(nseg_s) * CMP_STRIDE + CMP_LEN - 1))[None]
    srel_s = (past - np.arange(past))[None]
    srel_w = (nwin - np.arange(nwin))[None]
    buckets = [
        _bucket(rel_c, rel_c >= 0), _bucket(rel_s, rel_s >= 0), _bucket(rel_w, (rel_w >= 0) & (rel_w < WINDOW)),
        _bucket(srel_c, (srel_c >= 0) & (np.arange(nseg_s)[None] < n_cmp_s)), _bucket(srel_s, srel_s >= 0),
        _bucket(srel_w, srel_w < WINDOW), np.zeros((1, LANES), np.int32)]
    tab_c, tab_s, tab_w, bias_c, bias_s, bias_w, bias_new = _bias_tables(
        rel_bias, buckets, [True, True, False, False, False, False, False])
    tab_c, tab_s, tab_w = [t.reshape(2, (n_heads // 2) * QBLOCK, t.shape[-1]) for t in (tab_c, tab_s, tab_w)]

    xs = x_sample.reshape(nb, d)
    h_s, v_s, oa_s, q_s, kvc_s, kvs_s, kvw_s, gates_s, w_cat = _inproj(
        xs, gmix, w, gsgu, gq, w00, b00, chunked=False, n_col_tiles=n_col_tiles)
    ov_s = np.zeros((nseg_s, LANES), np.float32)
    ov_s[:n_cmp_s, :n_sel_s] = _overlap(n_cmp_s, n_sel_s)
    expand = jnp.asarray(np.arange(LANES)[:, None] == (np.arange(past)[None, :] // SEL_LEN), BF16)
    gates_h = jnp.pad(gates_s[:, :n_gate].reshape(nb, n_heads, 3), ((0, 0), (0, 0), (0, LANES - 3)))
    lin = lambda a: a[l].reshape(a.shape[1], -1, HEAD_DIM)
    win_lin = lin(state_win_kv)
    ob_s, win_new = _nsa_sample(
        page_table, lin(cache_cmp_kv), lin(cache_sel_kv), win_lin,
        q_s.astype(F32).reshape(nb, n_heads, HEAD_DIM), gates_h,
        kvs_s.reshape(nb, 4, HEAD_DIM), kvw_s.reshape(nb, 4, HEAD_DIM), wk_pairs, wv_pairs, pos_rows, gk,
        bias_c, bias_s, bias_w, bias_new, jnp.asarray(ov_s), expand)
    y_s, (wga, wgb, wg, wn, wup, wdown) = _dense_tail(
        xs, h_s, oa_s, ob_s.reshape(nb, d_nsa).astype(BF16), w, w, w_proj_gmlp[l], w_proj_nsa[l], wout, gmlp,
        w_up[l], w_down[l], gate_rows=(c_ga, c_gb))

    xp = x_prompt.reshape(bsz * seq, d)
    h_p, v_p, oa_p, q_p, kvc_p, kvs_p, kvw_p, gates_p = _inproj(
        xp, gmix, w_cat, gsgu, gq, w_sgu[l], b_exp, chunked=True, n_col_tiles=n_col_tiles)
    kvs3 = kvs_p.reshape(bsz, 4 * seq, HEAD_DIM)
    kvw3 = kvw_p.reshape(bsz, 4 * seq, HEAD_DIM)
    kvc3 = kvc_p.reshape(bsz, 4 * seq, HEAD_DIM)
    ks, vs, kw, vw = _kvprep(kvs3, kvw3, gk)
    kc, vc = _compress_prompt(kvc3, wk_pairs, wv_pairs, pos_rows, gk)
    n_cmp = (seq - CMP_LEN) // CMP_STRIDE + 1
    n_selb = seq // SEL_LEN
    ov_p = np.zeros((KC_ROWS, LANES), np.float32)
    ov_p[KC_FRONT:KC_FRONT + n_cmp, :n_selb] = _overlap(n_cmp, n_selb)
    ov_p = jnp.asarray(ov_p)
    gates_g = gates_p[:, :n_gate].reshape(bsz, seq, 2, n_gate // 2).transpose(0, 2, 1, 3)
    gates_g = jnp.pad(gates_g, ((0, 0), (0, 0), (0, 0), (0, LANES - n_gate // 2)))
    ob_p = _nsa_prompt(q_p.reshape(bsz, seq, d_nsa), gates_g, kc, vc, ov_p, ks, vs, kw, vw, tab_c, tab_s, tab_w)
    y_p, _ = _dense_tail(xp, h_p, oa_p, ob_p.reshape(bsz * seq, d_nsa), wga, wgb, wg, wn, wout, gmlp, wup, wdown)

    n_kv = n_heads // 4
    kv6 = lambda a, b_, t_: a.reshape(1, b_, t_, 2, n_kv, HEAD_DIM)
    nw_p = min(WINDOW, seq)
    last = ((seq - 1) // CHUNK) * CHUNK
    nw_s = min(WINDOW, past + 1)
    return (y_p.reshape(bsz, seq, d), y_s.reshape(nb, 1, d),
            kv6(kvc3, bsz, seq), kv6(kvs3, bsz, seq), kv6(kvw3[:, 4 * (seq - nw_p):], bsz, nw_p),
            v_p.reshape(bsz, seq, d_gm)[:, last:][None],
            kv6(kvc_s, nb, 1), kv6(kvs_s, nb, 1), kv6(win_new, nb, nw_s),
            v_s.reshape(1, nb, 1, d_gm))
```

```python
import functools
import math

import numpy as np
import jax
import jax.numpy as jnp
from jax import lax
from jax.experimental import pallas as pl
from jax.experimental.pallas import tpu as pltpu

F32 = jnp.float32
BF16 = jnp.bfloat16

HEAD_DIM = 128
CHUNK = 128
CMP_LEN = 32
CMP_STRIDE = 16
SEL_LEN = 64
N_SEL = 16
WINDOW = 512
N_BUCKETS = 32
MAX_DIST = 128
QBLOCK = 128
EPS = 1e-6
NEG = -1e30
HALF_NEG = -5e29
FORCE = 1e6

LANES = 128
PAD_LANE = 64
KPAD = 512
KC_FRONT = 16
KC_ROWS = 376
SAMPLE_GROUP = 2
VMEM_LIMIT = 56 * 1024 * 1024


def _cparams(sem):
    return pltpu.CompilerParams(dimension_semantics=sem, vmem_limit_bytes=VMEM_LIMIT)


def _dot(a, b):
    return jnp.dot(a, b, preferred_element_type=F32)


def _dot_nt(a, b):
    return lax.dot_general(a, b, (((1,), (1,)), ((), ())), preferred_element_type=F32)


def _dot_hilo(a, b_bf16):
    hi = a.astype(BF16)
    lo = (a - hi.astype(F32)).astype(BF16)
    return _dot(hi, b_bf16) + _dot(lo, b_bf16)


def _rms(x):
    return x * lax.rsqrt(jnp.mean(x * x, axis=-1, keepdims=True) + EPS)


def _gelu(x):
    c = math.sqrt(2.0 / math.pi)
    return 0.5 * x * (1.0 + jnp.tanh(c * (x + 0.044715 * (x * x * x))))


def _sigmoid(x):
    return 1.0 / (1.0 + jnp.exp(-x))


def _inproj_kernel(x_ref, gmix_ref, w_ref, gsgu_ref, gq_ref, wsg_ref, bsg_ref,
                   h_ref, v_ref, oa_ref, q_ref, kvc_ref, kvs_ref, kvw_ref, gate_ref, *rest, chunked, tm):
    h_s, u_s = rest[-2:]
    j = pl.program_id(1)

    half = w_ref.shape[0] // 2

    def z_cols(lo, hi):
        if len(rest) == 3:
            w = w_ref[lo:hi, :].astype(BF16)
            rest[0][lo:hi, :] = w
            return _dot_nt(h_s[...], w)
        return _dot_nt(h_s[...], w_ref[lo:hi, :])

    @pl.when(j == 0)
    def _():
        hb = (_rms(x_ref[...]) * gmix_ref[...]).astype(BF16)
        h_s[...] = hb
        h_ref[...] = hb
        za = z_cols(0, half)
        zb = z_cols(half, 2 * half)
        u_s[:, :half] = _gelu(za)
        u_s[:, half:] = _gelu(zb)

    @pl.when(j == 1)
    def _():
        za = z_cols(0, half)
        zb = z_cols(half, 2 * half)
        ga = _gelu(za)
        gb = _gelu(zb)
        ms = (jnp.sum(ga * ga, axis=-1, keepdims=True) + jnp.sum(gb * gb, axis=-1, keepdims=True)) / (2 * half)
        rstd = lax.rsqrt(ms + EPS)
        for c0, gz in ((0, ga), (half, gb)):
            v = gz * rstd * gsgu_ref[:, c0:c0 + half]
            v_ref[:, c0:c0 + half] = v
            if chunked:
                row = lax.broadcasted_iota(jnp.int32, (CHUNK, CHUNK), 0)
                col = lax.broadcasted_iota(jnp.int32, (CHUNK, CHUNK), 1)
                for g in range(half // LANES):
                    wm = jnp.where(row >= col, wsg_ref[c0 // LANES + g], 0.0).astype(BF16)
                    cs = slice(g * LANES, (g + 1) * LANES)
                    os_ = slice(c0 + g * LANES, c0 + (g + 1) * LANES)
                    for c in range(tm // CHUNK):
                        rs = slice(c * CHUNK, (c + 1) * CHUNK)
                        s = _dot(wm, v[rs, cs].astype(BF16)) + bsg_ref[:, os_]
                        oa_ref[rs, os_] = (u_s[rs, os_] * s).astype(BF16)
            else:
                cs = slice(c0, c0 + half)
                oa_ref[:, cs] = (u_s[:, cs] * (v * wsg_ref[:, cs] + bsg_ref[:, cs])).astype(BF16)

    @pl.when(j == 2)
    def _():
        scale = HEAD_DIM ** -0.5
        for c0 in (0, half):
            z = z_cols(c0, c0 + half)
            for hd in range(half // HEAD_DIM):
                cs = slice(hd * HEAD_DIM, (hd + 1) * HEAD_DIM)
                q_ref[:, c0 + hd * HEAD_DIM:c0 + (hd + 1) * HEAD_DIM] = (
                    _rms(z[:, cs]) * gq_ref[...] * scale).astype(BF16)

    def store_kv(ref, zz):
        for c in range(4):
            ref[pl.ds(c, tm, stride=4), :] = zz[:, c * LANES:(c + 1) * LANES]

    @pl.when(j == 3)
    def _():
        store_kv(kvc_ref, z_cols(0, half))
        store_kv(kvs_ref, z_cols(half, 2 * half))

    @pl.when(j == 4)
    def _():
        store_kv(kvw_ref, z_cols(0, half))
        gate_ref[...] = _sigmoid(z_cols(half, half + LANES))
        if len(rest) == 3:
            rest[0][half + LANES:, :] = jnp.zeros((half - LANES, w_ref.shape[1]), BF16)


def _inproj(x, gmix, w_cat, gsgu, gq, wsg, bsg, *, chunked, n_col_tiles):
    n, d = x.shape
    tm = 512 if n % 512 == 0 else n
    tn = 1024
    emit = w_cat.dtype != BF16
    assert not emit or n == tm
    kern = functools.partial(_inproj_kernel, chunked=chunked, tm=tm)
    full = lambda a: pl.BlockSpec(a.shape, lambda i, j: (0,) * a.ndim)
    row = lambda w: pl.BlockSpec((tm, w), lambda i, j: (i, 0))
    kvrow = pl.BlockSpec((4 * tm, LANES), lambda i, j: (i, 0))
    out_shapes = (
        jax.ShapeDtypeStruct((n, d), BF16),
        jax.ShapeDtypeStruct((n, 1024), F32),
        jax.ShapeDtypeStruct((n, 1024), BF16),
        jax.ShapeDtypeStruct((n, 1024), BF16),
        jax.ShapeDtypeStruct((4 * n, LANES), F32),
        jax.ShapeDtypeStruct((4 * n, LANES), F32),
        jax.ShapeDtypeStruct((4 * n, LANES), F32),
        jax.ShapeDtypeStruct((n, LANES), F32),
    )
    wspec = pl.BlockSpec((tn, d), lambda i, j: (j, 0))
    out_specs = [row(d), row(1024), row(1024), row(1024), kvrow, kvrow, kvrow, row(LANES)]
    if emit:
        out_shapes += (jax.ShapeDtypeStruct((n_col_tiles * tn, d), BF16),)
        out_specs.append(wspec)
    return pl.pallas_call(
        kern,
        grid=(n // tm, n_col_tiles),
        in_specs=[row(d), full(gmix), wspec, full(gsgu), full(gq), full(wsg), full(bsg)],
        out_specs=out_specs,
        out_shape=out_shapes,
        scratch_shapes=[pltpu.VMEM((tm, d), BF16), pltpu.VMEM((tm, 1024), F32)],
        compiler_params=_cparams(("arbitrary", "arbitrary")),
        name="inproj",
    )(x, gmix, w_cat, gsgu, gq, wsg, bsg)


def _kvprep_kernel(kvs_ref, kvw_ref, gk_ref, ks_ref, vs_ref, kw_ref, vw_ref):
    i = pl.program_id(1)
    rows = kvs_ref.shape[0] // 4
    col = lambda ref, c: ref[pl.ds(c, rows, stride=4), :]
    lane = lax.broadcasted_iota(jnp.int32, (rows, LANES), 1)
    row = lax.broadcasted_iota(jnp.int32, (rows, LANES), 0)

    @pl.when(i == 0)
    def _():
        aux = jnp.where(lane == PAD_LANE, 1.0, 0.0).astype(BF16)
        zk = jnp.zeros((rows, LANES), BF16)
        for g in range(2):
            ks_ref[g] = jnp.concatenate([zk, aux], axis=1)
            kw_ref[g] = jnp.concatenate([zk, aux], axis=1)
            vs_ref[g] = zk
            vw_ref[g] = zk

    @pl.when(i > 0)
    def _():
        blk = ((i - 1) * rows + row) // SEL_LEN
        onehot = jnp.where(lane == blk, 1.0, 0.0).astype(BF16)
        zaux = jnp.zeros((rows, LANES), BF16)
        for g in range(2):
            ks = (_rms(col(kvs_ref, g)) * gk_ref[1:2, :]).astype(BF16)
            kw = (_rms(col(kvw_ref, g)) * gk_ref[2:3, :]).astype(BF16)
            ks_ref[g] = jnp.concatenate([ks, onehot], axis=1)
            kw_ref[g] = jnp.concatenate([kw, zaux], axis=1)
            vs_ref[g] = col(kvs_ref, 2 + g).astype(BF16)
            vw_ref[g] = col(kvw_ref, 2 + g).astype(BF16)


def _kvprep(kvs, kvw, gk):
    b, t4, _ = kvs.shape
    t = t4 // 4
    rows = KPAD
    nblk = t // rows
    in_map = lambda bi, i: (bi, jnp.maximum(i - 1, 0), 0)
    out_map = lambda bi, i: (bi, 0, i, 0)
    kshape = jax.ShapeDtypeStruct((b, 2, KPAD + t, 2 * LANES), BF16)
    vshape = jax.ShapeDtypeStruct((b, 2, KPAD + t, LANES), BF16)
    return pl.pallas_call(
        _kvprep_kernel,
        grid=(b, nblk + 1),
        in_specs=[pl.BlockSpec((None, 4 * rows, LANES), in_map), pl.BlockSpec((None, 4 * rows, LANES), in_map),
                  pl.BlockSpec(gk.shape, lambda bi, i: (0, 0))],
        out_specs=[pl.BlockSpec((None, 2, rows, 2 * LANES), out_map), pl.BlockSpec((None, 2, rows, LANES), out_map),
                   pl.BlockSpec((None, 2, rows, 2 * LANES), out_map), pl.BlockSpec((None, 2, rows, LANES), out_map)],
        out_shape=(kshape, vshape, kshape, vshape),
        compiler_params=_cparams(("arbitrary", "arbitrary")),
        name="kvprep",
    )(kvs, kvw, gk)


def _compress_core(load_pair, nseg, wk_ref, wv_ref, pos_ref):
    w_refs = (wk_ref, wv_ref)
    accs = [jnp.zeros((2 * nseg + 16, 2 * LANES), F32) for _ in range(2)]
    for s2 in range(CMP_STRIDE // 2):
        for kv in range(2):
            parts = [load_pair(s2, 2 * kv + g) for g in range(2)]
            parts.append(pos_ref[kv, s2])
            lhs = jnp.concatenate(parts, axis=0)
            accs[kv] = accs[kv] + _dot(lhs, w_refs[kv][s2])
    outs = []
    for kv in range(2):
        y = accs[kv]
        post = y[2 * nseg:2 * nseg + 1, :LANES] + y[2 * nseg + 1:2 * nseg + 2, LANES:]
        per_g = []
        for g in range(2):
            y0 = y[g * nseg:(g + 1) * nseg, :LANES]
            y1 = y[g * nseg:(g + 1) * nseg, LANES:]
            per_g.append(y0 + pltpu.roll(y1, nseg - 1, 0) + post)
        outs.append(per_g)
    return outs


def _compress_prompt_kernel(x_ref, wk_ref, wv_ref, pos_ref, gk_ref, kc_ref, vc_ref):
    nseg = x_ref.shape[0] // (4 * CMP_STRIDE)
    load_x = lambda s, col: x_ref[pl.ds(4 * s + col, nseg, stride=4 * CMP_STRIDE), :]
    load_pair = lambda s2, col: jnp.concatenate([load_x(2 * s2, col), load_x(2 * s2 + 1, col)], axis=1).astype(BF16)
    ck, cv = _compress_core(load_pair, nseg, wk_ref, wv_ref, pos_ref)
    row = lax.broadcasted_iota(jnp.int32, (nseg, LANES), 0)
    lane = lax.broadcasted_iota(jnp.int32, (nseg, LANES), 1)
    valid = row < nseg - 1
    aux = jnp.where(valid, jnp.where(lane == row // 8, 1.0, 0.0), jnp.where(lane == PAD_LANE, 1.0, 0.0))
    back = KC_ROWS - KC_FRONT - nseg
    lane_f = lax.broadcasted_iota(jnp.int32, (KC_FRONT, LANES), 1)
    lane_b = lax.broadcasted_iota(jnp.int32, (back, LANES), 1)
    pad_f = jnp.concatenate([jnp.zeros((KC_FRONT, LANES), F32), jnp.where(lane_f == PAD_LANE, 1.0, 0.0)], axis=1)
    pad_b = jnp.concatenate([jnp.zeros((back, LANES), F32), jnp.where(lane_b == PAD_LANE, 1.0, 0.0)], axis=1)
    for g in range(2):
        kn = jnp.where(valid, _rms(ck[g]) * gk_ref[0:1, :], 0.0)
        kc_ref[g, 0:KC_FRONT, :] = pad_f
        kc_ref[g, KC_FRONT:KC_FRONT + nseg, :] = jnp.concatenate([kn, aux], axis=1)
        kc_ref[g, KC_FRONT + nseg:KC_ROWS, :] = pad_b
        vc_ref[g, 0:KC_FRONT, :] = jnp.zeros((KC_FRONT, LANES), F32)
        vc_ref[g, KC_FRONT:KC_FRONT + nseg, :] = jnp.where(valid, cv[g], 0.0)
        vc_ref[g, KC_FRONT + nseg:KC_ROWS, :] = jnp.zeros((back, LANES), F32)


def _compress_prompt(kvc, wk_pairs, wv_pairs, pos_rows, gk):
    b, t4, _ = kvc.shape
    full = lambda a: pl.BlockSpec(a.shape, lambda bi: (0,) * a.ndim)
    return pl.pallas_call(
        _compress_prompt_kernel,
        grid=(b,),
        in_specs=[pl.BlockSpec((None, t4, LANES), lambda bi: (bi, 0, 0)),
                  full(wk_pairs), full(wv_pairs), full(pos_rows), full(gk)],
        out_specs=[pl.BlockSpec((None, 2, KC_ROWS, 2 * LANES), lambda bi: (bi, 0, 0, 0)),
                   pl.BlockSpec((None, 2, KC_ROWS, LANES), lambda bi: (bi, 0, 0, 0))],
        out_shape=(jax.ShapeDtypeStruct((b, 2, KC_ROWS, 2 * LANES), F32),
                   jax.ShapeDtypeStruct((b, 2, KC_ROWS, LANES), F32)),
        compiler_params=_cparams(("arbitrary",)),
        name="compress_prompt",
    )(kvc, wk_pairs, wv_pairs, pos_rows, gk)


def _nsa_prompt_kernel(q_ref, gate_ref, kc_ref, vc_ref, ov_ref, ks_ref, vs_ref, kw_ref, vw_ref,
                       tc_ref, ts_ref, tw_ref, o_ref, m_s, l_s, acc_s, sa_s, sb_s, sw_s):
    qb = pl.program_id(2)
    t0 = qb * QBLOCK
    rq = 4 * QBLOCK
    q = q_ref[...]
    q4 = jnp.concatenate([q[:, r * LANES:(r + 1) * LANES] for r in range(4)], axis=0)
    lane = lax.broadcasted_iota(jnp.int32, (rq, LANES), 1)
    is_pad_lane = lane == PAD_LANE

    def q_aug(mb):
        return jnp.concatenate([q4, mb.astype(BF16)], axis=1)

    qa_pad = q_aug(jnp.where(is_pad_lane, NEG, 0.0))

    far_mask = ((lane < 32) & (lane >= qb - 2)) | is_pad_lane
    qa_cfar = q_aug(jnp.where(far_mask, NEG, 0.0))
    ncmp = 256
    near0 = pl.multiple_of(qb * 8, 8)
    k_far = kc_ref[KC_FRONT:KC_FRONT + ncmp, :].astype(BF16)
    k_near = kc_ref[pl.ds(near0, LANES), :].astype(BF16)
    s_far = _dot_nt(qa_cfar, k_far)
    s_near = _dot_nt(qa_pad, k_near) + tc_ref[...]
    w0 = pl.multiple_of(t0, QBLOCK)
    wlen = WINDOW + QBLOCK
    sw_s[...] = _dot_nt(qa_pad, kw_ref[pl.ds(w0, wlen), :]) + tw_ref[...]
    m = jnp.maximum(jnp.max(s_far, axis=-1, keepdims=True), jnp.max(s_near, axis=-1, keepdims=True))
    m = jnp.maximum(m, HALF_NEG)
    p_far = jnp.exp(s_far - m)
    p_near = jnp.exp(s_near - m)
    l = jnp.sum(p_far, axis=-1, keepdims=True) + jnp.sum(p_near, axis=-1, keepdims=True)
    linv = 1.0 / jnp.where(l > 0.0, l, 1.0)
    v_far = vc_ref[KC_FRONT:KC_FRONT + ncmp, :].astype(BF16)
    v_near = vc_ref[pl.ds(near0, LANES), :].astype(BF16)
    pb_far = p_far.astype(BF16)
    pb_near = p_near.astype(BF16)
    o_c = (_dot(pb_far, v_far) + _dot(pb_near, v_near)) * linv

    s = sw_s[...]
    m = jnp.max(s, axis=-1, keepdims=True)
    p = jnp.exp(s - m)
    l = jnp.sum(p, axis=-1, keepdims=True)
    o_w = _dot(p.astype(BF16), vw_ref[pl.ds(w0, wlen), :]) * (1.0 / l)

    ov_far = ov_ref[KC_FRONT:KC_FRONT + ncmp, :].astype(BF16)
    ov_near = ov_ref[pl.ds(near0, LANES), :].astype(BF16)
    imp4 = (_dot(pb_far, ov_far) + _dot(pb_near, ov_near)) * linv
    imp = sum(imp4[r * QBLOCK:(r + 1) * QBLOCK] for r in range(4))
    nblk = 64
    imp_t = imp.T[:nblk]
    blk = lax.broadcasted_iota(jnp.int32, (nblk, QBLOCK), 0)
    qpos = t0 + lax.broadcasted_iota(jnp.int32, (nblk, QBLOCK), 1)
    cur = qpos // SEL_LEN
    forced = (blk == 0) | (blk == cur) | (blk == cur - 1)
    eligible = blk * SEL_LEN <= qpos
    val = jnp.where(forced, FORCE, jnp.where(eligible, imp_t, -1.0))
    slab = 8
    vals = [val[v * slab:(v + 1) * slab] for v in range(nblk // slab)]
    ranks = [jnp.zeros((slab, QBLOCK), F32) for _ in vals]
    row_in_slab = lax.broadcasted_iota(jnp.int32, (slab, QBLOCK), 0)
    for i in range(nblk):
        vi, ii = divmod(i, slab)
        ri = jnp.broadcast_to(vals[vi][ii:ii + 1, :], (slab, QBLOCK))
        for v in range(len(vals)):
            if v < vi:
                beats = ri > vals[v]
            elif v > vi:
                beats = ri >= vals[v]
            else:
                beats = (ri > vals[v]) | ((ri == vals[v]) & (row_in_slab > ii))
            ranks[v] = ranks[v] + jnp.where(beats, 1.0, 0.0)
    rank = jnp.concatenate(ranks, axis=0)
    mb_t = jnp.where(rank < float(N_SEL), 0.0, NEG)
    row2 = lax.broadcasted_iota(jnp.int32, (LANES - nblk, QBLOCK), 0)
    mb_t = jnp.concatenate([mb_t, jnp.where(row2 == PAD_LANE - nblk, NEG, 0.0)], axis=0)
    mb = mb_t.T
    mb4 = jnp.concatenate([mb] * 4, axis=0)
    qa_snear = q_aug(mb4)
    qa_sfar = q_aug(jnp.where((lane < nblk) & (lane >= 2 * qb - 2), NEG, mb4))

    far_tile = 512
    n_far = (jnp.maximum(qb - 1, 0) * QBLOCK + far_tile - 1) // far_tile

    def far_logits(j):
        r0 = pl.multiple_of(KPAD + j * far_tile, far_tile)
        return _dot_nt(qa_sfar, ks_ref[pl.ds(r0, far_tile), :])

    sn0 = pl.multiple_of(t0 + KPAD - QBLOCK, QBLOCK)
    s = _dot_nt(qa_snear, ks_ref[pl.ds(sn0, 2 * QBLOCK), :]) + ts_ref[...]
    sa_s[...] = far_logits(0)
    m = jnp.max(s, axis=-1, keepdims=True)
    p = jnp.exp(s - m)
    l = jnp.sum(p, axis=-1, keepdims=True)
    acc = _dot(p.astype(BF16), vs_ref[pl.ds(sn0, 2 * QBLOCK), :])

    m_s[...] = m
    l_s[...] = l
    acc_s[...] = acc
    last_tile = ks_ref.shape[0] // far_tile - 2

    def far_update(j, s_ref):
        r0 = pl.multiple_of(KPAD + j * far_tile, far_tile)
        s = s_ref[...]
        m_old = m_s[...]
        m_new = jnp.maximum(m_old, jnp.max(s, axis=-1, keepdims=True))
        alpha = jnp.exp(m_old - m_new)
        p = jnp.exp(s - m_new)
        l_s[...] = alpha * l_s[...] + jnp.sum(p, axis=-1, keepdims=True)
        acc_s[...] = alpha * acc_s[...] + _dot(p.astype(BF16), vs_ref[pl.ds(r0, far_tile), :])
        m_s[...] = m_new

    def far_body(i, carry):
        a = 2 * i
        sb_s[...] = far_logits(a + 1)
        far_update(a, sa_s)
        sa_s[...] = far_logits(jnp.minimum(a + 2, last_tile))
        far_update(a + 1, sb_s)
        return carry

    lax.fori_loop(0, n_far // 2, far_body, 0)

    @pl.when(n_far % 2 == 1)
    def _():
        far_update(n_far - 1, sa_s)

    o_s = acc_s[...] * (1.0 / l_s[...])

    gt = gate_ref[...]
    first_group = pl.program_id(1) == 0
    gate = lambda c: jnp.where(first_group, gt[:, c:c + 1], gt[:, 12 + c:12 + c + 1])
    outs = []
    for r in range(4):
        rs = slice(r * QBLOCK, (r + 1) * QBLOCK)
        outs.append(gate(3 * r) * o_c[rs] + gate(3 * r + 1) * o_s[rs] + gate(3 * r + 2) * o_w[rs])
    o_ref[...] = jnp.concatenate(outs, axis=1).astype(BF16)


def _nsa_prompt(q, gates_g, kc, vc, ov, ks, vs, kw, vw, tab_c, tab_s, tab_w):
    b, t, _ = q.shape
    nqb = t // QBLOCK
    kvspec = lambda a: pl.BlockSpec((None, None) + a.shape[2:], lambda bi, g, i: (bi, g, 0, 0))
    tabspec = lambda a: pl.BlockSpec((None,) + a.shape[1:], lambda bi, g, i: (g, 0, 0))
    return pl.pallas_call(
        _nsa_prompt_kernel,
        grid=(b, 2, nqb),
        in_specs=[pl.BlockSpec((None, QBLOCK, 512), lambda bi, g, i: (bi, i, g)),
                  pl.BlockSpec((None, QBLOCK, LANES), lambda bi, g, i: (bi, i, 0)),
                  kvspec(kc), kvspec(vc), pl.BlockSpec(ov.shape, lambda bi, g, i: (0, 0)),
                  kvspec(ks), kvspec(vs), kvspec(kw), kvspec(vw),
                  tabspec(tab_c), tabspec(tab_s), tabspec(tab_w)],
        out_specs=pl.BlockSpec((None, QBLOCK, 512), lambda bi, g, i: (bi, i, g)),
        out_shape=jax.ShapeDtypeStruct((b, t, 1024), BF16),
        scratch_shapes=[pltpu.VMEM((4 * QBLOCK, 1), F32), pltpu.VMEM((4 * QBLOCK, 1), F32),
                        pltpu.VMEM((4 * QBLOCK, LANES), F32),
                        pltpu.VMEM((4 * QBLOCK, 512), F32), pltpu.VMEM((4 * QBLOCK, 512), F32),
                        pltpu.VMEM((4 * QBLOCK, WINDOW + QBLOCK), F32)],
        compiler_params=_cparams(("arbitrary", "arbitrary", "arbitrary")),
        name="nsa_prompt",
    )(q, gates_g, kc, vc, ov, ks, vs, kw, vw, tab_c, tab_s, tab_w)


def _decode_logits(qf, g0rows, gk_row, k_tiles, bias):
    qg = qf * gk_row
    zero = jnp.zeros_like(qg)
    qbd = jnp.concatenate([jnp.where(g0rows, qg, zero), jnp.where(g0rows, zero, qg)], axis=1).astype(BF16)
    one = jnp.ones_like(qg)
    ones_bd = jnp.concatenate([jnp.where(g0rows, one, zero), jnp.where(g0rows, zero, one)], axis=1).astype(BF16)
    s_parts, q_parts = [], []
    for kt in k_tiles:
        k = kt()
        s_parts.append(_dot_nt(qbd, k.astype(BF16)))
        q_parts.append(_dot_nt(ones_bd, (k * k).astype(BF16)))
    s = jnp.concatenate(s_parts, axis=1)
    ssq = jnp.concatenate(q_parts, axis=1)
    return s * lax.rsqrt(ssq * (1.0 / HEAD_DIM) + EPS) + bias, qg


def _decode_finish(s, qg, g0rows, v_tiles, key_mask, new_row, bias_new):
    if key_mask is not None:
        s = jnp.where(key_mask > 0.5, s, NEG)
    bc = lambda c: jnp.broadcast_to(new_row[c:c + 1, :], qg.shape)
    k_new = jnp.where(g0rows, bc(0), bc(1))
    v_new = jnp.where(g0rows, bc(2), bc(3))
    s_new = jnp.sum(qg * _rms(k_new), axis=-1, keepdims=True) + bias_new
    m = jnp.maximum(jnp.max(s, axis=-1, keepdims=True), s_new)
    p = jnp.exp(s - m)
    p_new = jnp.exp(s_new - m)
    l = jnp.sum(p, axis=-1, keepdims=True) + p_new
    o2 = jnp.zeros((qg.shape[0], 2 * LANES), F32)
    for i, vt in enumerate(v_tiles):
        o2 = o2 + _dot(p[:, i * LANES:(i + 1) * LANES].astype(BF16), vt().astype(BF16))
    o = jnp.where(g0rows, o2[:, :LANES], o2[:, LANES:]) + p_new * v_new
    return o * (1.0 / l)


def _nsa_sample_kernel(pt_ref, cache_c_hbm, cache_s_hbm, *refs, n_pages, n_sel_blocks, group):
    per_b_in, consts, outs = refs[:5], refs[5:-5], refs[-5:-3]
    buf_c, buf_s, sem = refs[-3:]
    step = pl.program_id(0)
    slot = step % 2

    def page_copies(src_step, dst_slot, for_wait):
        copies = []
        for e in range(group):
            for p in range(n_pages):
                pg = 0 if for_wait else pt_ref[src_step * group + e, p]
                copies.append(pltpu.make_async_copy(cache_c_hbm.at[pg], buf_c.at[dst_slot, e, p], sem.at[dst_slot, 0]))
                copies.append(pltpu.make_async_copy(cache_s_hbm.at[pg], buf_s.at[dst_slot, e, p], sem.at[dst_slot, 1]))
        return copies

    @pl.when(step == 0)
    def _():
        for c in page_copies(0, 0, False):
            c.start()

    @pl.when(step + 1 < pl.num_programs(0))
    def _():
        for c in page_copies(step + 1, 1 - slot, False):
            c.start()

    for c in page_copies(step, slot, True):
        c.wait()

    chains = [_nsa_sample_one([buf_c.at[slot, bb, p] for p in range(n_pages)],
                              [buf_s.at[slot, bb, p] for p in range(n_pages)],
                              *[r.at[bb] for r in per_b_in], *consts, *[r.at[bb] for r in outs],
                              n_pages=n_pages, n_sel_blocks=n_sel_blocks, chain_first=bb % 2 == 0)
              for bb in range(group)]
    while chains:
        chains = [c for c in chains if next(c, "done") != "done"]


def _nsa_sample_one(pages_c, pages_s, win_ref, q_ref, gate_ref, ksn_ref, kwn_ref, wk_ref, wv_ref, pos_ref, gk_ref,
                    bc_ref, bs_ref, bw_ref, bnew_ref, ov_ref, e_ref, perm_ref, o_ref, wout_ref, *,
                    n_pages, n_sel_blocks, chain_first):
    qf = q_ref[...]
    nh = qf.shape[0]
    g0rows = lax.broadcasted_iota(jnp.int32, (nh, LANES), 0) < nh // 2
    lane = lax.broadcasted_iota(jnp.int32, (nh, LANES), 1)

    page = pages_c[0].shape[0] // 4
    nseg = n_pages * page // CMP_STRIDE
    perm = perm_ref[...]
    regrouped = []
    for pp in range(n_pages // 2):
        per_cp = []
        for cp in range(2):
            blk = jnp.concatenate(
                [jnp.concatenate([pages_c[2 * pp + i][pl.ds(2 * cp + c, page, stride=4), :] for c in range(2)], axis=1)
                 for i in range(2)], axis=0).astype(BF16)
            per_cp.append(_dot(perm, blk).astype(BF16))
        regrouped.append(per_cp)
    rows_per_tap = 2 * page // CMP_STRIDE
    yield

    def tap_rows(s, col):
        lo, hi = s * rows_per_tap, (s + 1) * rows_per_tap
        return jnp.concatenate([regrouped[pp][col // 2][lo:hi, (col % 2) * LANES:(col % 2 + 1) * LANES]
                                for pp in range(n_pages // 2)], axis=0)

    load_pair = lambda s2, col: jnp.concatenate([tap_rows(2 * s2, col), tap_rows(2 * s2 + 1, col)], axis=1)
    ck, cv = _compress_core(load_pair, nseg, wk_ref, wv_ref, pos_ref)
    yield

    def selection_chain():
        kcn = jnp.concatenate([(_rms(ck[g]) * gk_ref[0:1, :]).astype(BF16) for g in range(2)], axis=0)
        s2 = _dot_nt(qf.astype(BF16), kcn)
        s = jnp.where(g0rows, s2[:, :nseg], s2[:, nseg:]) + bc_ref[...]
        m = jnp.max(s, axis=-1, keepdims=True)
        p = jnp.exp(s - m)
        linv = 1.0 / jnp.sum(p, axis=-1, keepdims=True)
        rowv = lax.broadcasted_iota(jnp.int32, (nseg, LANES), 0) < nseg - 1
        pb = p.astype(BF16)
        oc = [_dot(pb, jnp.where(rowv, cv[g], 0.0).astype(BF16)) for g in range(2)]
        o_c = jnp.where(g0rows, oc[0], oc[1]) * linv
        pn = p * linv
        s0 = jnp.sum(jnp.where(g0rows, pn, 0.0), axis=0, keepdims=True)
        s1 = jnp.sum(jnp.where(g0rows, 0.0, pn), axis=0, keepdims=True)
        psum = jnp.where(g0rows, jnp.broadcast_to(s0, pn.shape), jnp.broadcast_to(s1, pn.shape))
        imp = _dot_hilo(psum, ov_ref[...].astype(BF16))
        cur = n_sel_blocks - 1
        forced = (lane == 0) | (lane == cur) | (lane == cur - 1)
        val = jnp.where(lane >= n_sel_blocks, -2.0, jnp.where(forced, FORCE, imp))
        rank = jnp.zeros_like(val)
        for i in range(n_sel_blocks):
            ci = jnp.broadcast_to(val[:, i:i + 1], val.shape)
            beats = (ci > val) | ((ci == val) & (lane > i))
            rank = rank + jnp.where(beats, 1.0, 0.0)
        sel = jnp.where((rank < float(N_SEL)) & (lane < n_sel_blocks), 1.0, 0.0)
        return o_c, _dot(sel.astype(BF16), e_ref[...])

    def tiles(ref_list, col):
        def tile(rf, i):
            rows = lambda c: rf[pl.ds(4 * LANES * i + c, LANES, stride=4), :]
            return lambda: jnp.concatenate([rows(col), rows(col + 1)], axis=1)
        return [tile(rf, i) for rf in ref_list for i in range(rf.shape[0] // (4 * LANES))]

    b_new = bnew_ref[:, 0:1]

    def mask_free_work():
        s_sel, qg_sel = _decode_logits(qf, g0rows, gk_ref[1:2, :], tiles(pages_s, 0), bs_ref[...])
        s_win, qg_win = _decode_logits(qf, g0rows, gk_ref[2:3, :], tiles([win_ref], 0), bw_ref[...])
        o_w = _decode_finish(s_win, qg_win, g0rows, tiles([win_ref], 2), None, kwn_ref[...], b_new)
        return s_sel, qg_sel, o_w

    if chain_first:
        o_c, key_mask = selection_chain()
        yield
        s_sel, qg_sel, o_w = mask_free_work()
    else:
        s_sel, qg_sel, o_w = mask_free_work()
        yield
        o_c, key_mask = selection_chain()
    yield
    o_s = _decode_finish(s_sel, qg_sel, g0rows, tiles(pages_s, 2), key_mask, ksn_ref[...], b_new)
    yield
    gt = gate_ref[...]
    o_ref[...] = gt[:, 0:1] * o_c + gt[:, 1:2] * o_s + gt[:, 2:3] * o_w

    keep = wout_ref.shape[0] - 4
    drop = win_ref.shape[0] - keep
    wout_ref[0:keep, :] = win_ref[drop:drop + keep, :]
    wout_ref[keep:keep + 4, :] = kwn_ref[...]


def _nsa_sample(page_table, cache_c, cache_s, win, q, gates, ks_new, kw_new, wk_pairs, wv_pairs, pos_rows, gk,
                bias_c, bias_s, bias_w, bias_new, ov, expand):
    nb, n_pages = page_table.shape
    page_rows = cache_c.shape[1]
    n_sel_blocks = -(-(n_pages * (page_rows // 4) + 1) // SEL_LEN)
    group = SAMPLE_GROUP if nb % SAMPLE_GROUP == 0 else 1
    kern = functools.partial(_nsa_sample_kernel, n_pages=n_pages, n_sel_blocks=n_sel_blocks, group=group)
    win_rows_out = 4 * min(WINDOW, n_pages * (page_rows // 4) + 1)
    page = page_rows // 4
    assert n_pages % 2 == 0 and page % CMP_STRIDE == 0
    segs = page // CMP_STRIDE
    i_, n_, s_ = np.meshgrid(np.arange(2), np.arange(segs), np.arange(CMP_STRIDE), indexing="ij")
    perm_np = np.zeros((2 * page, 2 * page), np.float32)
    perm_np[(s_ * 2 * segs + i_ * segs + n_).ravel(), (i_ * page + CMP_STRIDE * n_ + s_).ravel()] = 1.0
    perm = jnp.asarray(perm_np, BF16)
    hbm = pl.BlockSpec(memory_space=pl.ANY)
    full = lambda a: pl.BlockSpec(a.shape, lambda bi, pt: (0,) * a.ndim)
    per_b = lambda a: pl.BlockSpec((group,) + a.shape[1:], lambda bi, pt: (bi,) + (0,) * (a.ndim - 1))
    page_buf = pltpu.VMEM((2, group, n_pages, page_rows, LANES), F32)
    grid_spec = pltpu.PrefetchScalarGridSpec(
        num_scalar_prefetch=1,
        grid=(nb // group,),
        scratch_shapes=[page_buf, page_buf, pltpu.SemaphoreType.DMA((2, 2))],
        in_specs=([hbm, hbm]
                  + [per_b(win), per_b(q), per_b(gates), per_b(ks_new), per_b(kw_new),
                     full(wk_pairs), full(wv_pairs), full(pos_rows), full(gk),
                     full(bias_c), full(bias_s), full(bias_w), full(bias_new), full(ov), full(expand), full(perm)]),
        out_specs=[pl.BlockSpec((group,) + q.shape[1:], lambda bi, pt: (bi, 0, 0)),
                   pl.BlockSpec((group, win_rows_out, LANES), lambda bi, pt: (bi, 0, 0))],
    )
    return pl.pallas_call(
        kern,
        grid_spec=grid_spec,
        out_shape=(jax.ShapeDtypeStruct(q.shape, F32), jax.ShapeDtypeStruct((nb, win_rows_out, LANES), F32)),
        compiler_params=_cparams(("arbitrary",)),
        name="nsa_sample",
    )(page_table, cache_c, cache_s, win, q, gates, ks_new, kw_new,
      wk_pairs, wv_pairs, pos_rows, gk, bias_c, bias_s, bias_w, bias_new, ov, expand, perm)


def _merge_kernel(h_ref, oa_ref, ob_ref, wga_ref, wgb_ref, wg_ref, wn_ref, mix_ref, *cast_refs):
    h = h_ref[...]
    wga = wga_ref[...].astype(BF16)
    wgb = wgb_ref[...].astype(BF16)
    wg = wg_ref[...].astype(BF16)
    wn = wn_ref[...].astype(BF16)
    ga = _sigmoid(_dot_nt(h, wga))
    gb = _sigmoid(_dot_nt(h, wgb))
    mix_ref[...] = (ga * _dot(oa_ref[...], wg) + gb * _dot(ob_ref[...], wn)).astype(BF16)
    for ref, val in zip(cast_refs, (wga, wgb, wg, wn)):
        ref[...] = val


def _merge(h, oa, ob, wga, wgb, wg, wn, gate_rows=None):
    n, d = h.shape
    tm = 512 if n % 512 == 0 else n
    tn = 512
    emit = gate_rows is not None
    assert not emit or n == tm
    row = lambda w: pl.BlockSpec((tm, w), lambda i, j: (i, 0))
    col = lambda k: pl.BlockSpec((k, tn), lambda i, j: (0, j))
    colt = pl.BlockSpec((tn, d), lambda i, j: (j, 0))
    out_specs = [pl.BlockSpec((tm, tn), lambda i, j: (i, j))]
    out_shape = [jax.ShapeDtypeStruct((n, d), BF16)]
    gate_specs = [colt, colt]
    if emit:
        assert all(r % 8 == 0 for r in gate_rows)
        gate_specs = [pl.BlockSpec((pl.Element(tn), pl.Element(d)),
                                   lambda i, j, r=r: ((r // 8 + j * (tn // 8)) * 8, 0)) for r in gate_rows]
        out_specs += [colt, colt, col(wg.shape[0]), col(wn.shape[0])]
        out_shape += [jax.ShapeDtypeStruct((d, d), BF16), jax.ShapeDtypeStruct((d, d), BF16),
                      jax.ShapeDtypeStruct(wg.shape, BF16), jax.ShapeDtypeStruct(wn.shape, BF16)]
    return pl.pallas_call(
        _merge_kernel,
        grid=(n // tm, d // tn),
        in_specs=[row(d), row(oa.shape[1]), row(ob.shape[1])] + gate_specs + [col(wg.shape[0]), col(wn.shape[0])],
        out_specs=out_specs,
        out_shape=out_shape,
        compiler_params=_cparams(("arbitrary", "arbitrary")),
        name="merge",
    )(h, oa, ob, wga, wgb, wg, wn)


def _outproj_kernel(x_ref, mix_ref, wout_ref, gmlp_ref, x1_ref, hm_ref, *cast_refs):
    if cast_refs:
        wout = wout_ref[...].astype(BF16)
        cast_refs[0][...] = wout
        x1 = x_ref[...] + _dot(mix_ref[...], wout)
    else:
        x1 = x_ref[...] + _dot(mix_ref[...], wout_ref[...])
    x1_ref[...] = x1
    hm_ref[...] = (_rms(x1) * gmlp_ref[...]).astype(BF16)


def _outproj(x, mix, wout, gmlp):
    n, d = x.shape
    tm = 512 if n % 512 == 0 else n
    emit = wout.dtype != BF16
    assert not emit or n == tm
    row = pl.BlockSpec((tm, d), lambda i: (i, 0))
    wspec = pl.BlockSpec(wout.shape, lambda i: (0, 0))
    out_specs = [row, row]
    out_shape = [jax.ShapeDtypeStruct((n, d), F32), jax.ShapeDtypeStruct((n, d), BF16)]
    if emit:
        out_specs.append(wspec)
        out_shape.append(jax.ShapeDtypeStruct(wout.shape, BF16))
    return pl.pallas_call(
        _outproj_kernel,
        grid=(n // tm,),
        in_specs=[row, row, wspec, pl.BlockSpec(gmlp.shape, lambda i: (0, 0))],
        out_specs=out_specs,
        out_shape=out_shape,
        compiler_params=_cparams(("arbitrary",)),
        name="outproj",
    )(x, mix, wout, gmlp)


def _ffn_kernel(hm_ref, x1_ref, wup_ref, wdown_ref, y_ref, *cast_refs):
    f = pl.program_id(1)

    @pl.when(f == 0)
    def _():
        y_ref[...] = x1_ref[...]

    wup = wup_ref[...].astype(BF16)
    wdown = wdown_ref[...].astype(BF16)
    hid = jnp.maximum(_dot(hm_ref[...], wup), 0.0)
    y_ref[...] += _dot((hid * hid).astype(BF16), wdown)
    if cast_refs:
        cast_refs[0][...] = wup
        cast_refs[1][...] = wdown


def _ffn(hm, x1, wup, wdown):
    n, d = hm.shape
    dff = wup.shape[1]
    tm = 512 if n % 512 == 0 else n
    emit = wup.dtype != BF16
    assert not emit or n == tm
    tf = 512 if emit else 1024
    row = pl.BlockSpec((tm, d), lambda i, f: (i, 0))
    up_spec = pl.BlockSpec((d, tf), lambda i, f: (0, f))
    down_spec = pl.BlockSpec((tf, d), lambda i, f: (f, 0))
    out_specs = [row]
    out_shape = [jax.ShapeDtypeStruct((n, d), F32)]
    if emit:
        out_specs += [up_spec, down_spec]
        out_shape += [jax.ShapeDtypeStruct(wup.shape, BF16), jax.ShapeDtypeStruct(wdown.shape, BF16)]
    return pl.pallas_call(
        _ffn_kernel,
        grid=(n // tm, dff // tf),
        in_specs=[row, row, up_spec, down_spec],
        out_specs=out_specs,
        out_shape=out_shape,
        compiler_params=_cparams(("arbitrary", "arbitrary")),
        name="ffn",
    )(hm, x1, wup, wdown)


def _bucket(rel, valid):
    n = np.maximum(rel, 0)
    max_exact = N_BUCKETS // 2
    nf = np.maximum(n, 1).astype(np.float32)
    large = max_exact + (np.log(nf / np.float32(max_exact)) / np.float32(math.log(MAX_DIST / max_exact))
                         * np.float32(N_BUCKETS - max_exact)).astype(np.int32)
    large = np.minimum(large, N_BUCKETS - 1)
    return np.where(valid, np.where(n < max_exact, n, large), -1).astype(np.int32)


def _bias_tables_kernel(rb_ref, *refs, shifts):
    n = len(shifts)
    nbk, nh = rb_ref.shape
    for b_ref, o_ref, shift in zip(refs[:n], refs[n:], shifts):
        b = b_ref[...]
        rows = []
        for h in range(nh):
            sh = rb_ref[nbk - 1, h] if shift else 0.0
            acc = jnp.full(b.shape, NEG, F32)
            for k in range(nbk):
                acc = jnp.where(b == k, rb_ref[k, h] - sh, acc)
            if len(o_ref.shape) == 3:
                o_ref[h] = acc
            else:
                rows.append(acc)
        if rows:
            o_ref[...] = jnp.concatenate(rows, axis=0)


def _bias_tables(rel_bias, buckets, shifts):
    nh = rel_bias.shape[1]
    shapes = [jax.ShapeDtypeStruct((nh,) + (b.shape if b.shape[0] > 1 else b.shape[1:]), F32) for b in buckets]
    vm = pl.BlockSpec(memory_space=pltpu.VMEM)
    return pl.pallas_call(
        functools.partial(_bias_tables_kernel, shifts=tuple(shifts)),
        in_specs=[pl.BlockSpec(memory_space=pltpu.SMEM)] + [vm] * len(buckets),
        out_specs=[vm] * len(buckets),
        out_shape=shapes,
        name="bias_tables",
    )(rel_bias, *[jnp.asarray(b) for b in buckets])


def _overlap(nc, ns):
    i = np.arange(nc)[:, None] * CMP_STRIDE
    j = np.arange(ns)[None, :] * SEL_LEN
    return ((i < j + SEL_LEN) & (i + CMP_LEN > j)).astype(np.float32)


def _compress_weights(w, pos):
    s = np.arange(0, CMP_STRIDE, 2)
    top = jnp.concatenate([w[s], w[CMP_STRIDE + s]], axis=2)
    bot = jnp.concatenate([w[s + 1], w[CMP_STRIDE + s + 1]], axis=2)
    tiles = jnp.concatenate([top, bot], axis=1).astype(BF16)
    row_a = jnp.concatenate([pos[s], pos[s + 1]], axis=1)
    row_b = jnp.concatenate([pos[CMP_STRIDE + s], pos[CMP_STRIDE + s + 1]], axis=1)
    rows = jnp.zeros((len(s), 16, 2 * HEAD_DIM), F32).at[:, 0].set(row_a).at[:, 1].set(row_b)
    return tiles, rows.astype(BF16)


def _dense_tail(x, h, oa, ob, wga, wgb, wg, wn, wout, gmlp, wup, wdown, gate_rows=None):
    mix, *cast_m = _merge(h, oa, ob, wga, wgb, wg, wn, gate_rows)
    x1, hm, *cast_o = _outproj(x, mix, wout, gmlp)
    y, *cast_f = _ffn(hm, x1, wup, wdown)
    return y, (cast_m or [wga, wgb, wg, wn]) + (cast_o or [wout]) + (cast_f or [wup, wdown])


def kernel(x_prompt, x_sample, cache_cmp_kv, cache_sel_kv, state_win_kv, page_table, rel_bias, g_mix_norm, w_in,
           g_sgu, w_sgu, b_sgu, g_q, g_k, pos_cmp_k, w_cmp_k, pos_cmp_v, w_cmp_v, w_proj_gmlp, w_proj_nsa, w_out,
           g_mlp_norm, w_up, w_down):
    depth = g_mix_norm.shape[0]
    assert depth == 1
    l = 0
    bsz, seq, d = x_prompt.shape
    nb = x_sample.shape[0]
    assert x_sample.shape[1] == 1 and seq % KPAD == 0
    d_gm = g_sgu.shape[1]
    n_heads = rel_bias.shape[1]
    d_nsa = n_heads * HEAD_DIM
    kvw_cols = 2 * (n_heads // 4) * HEAD_DIM
    n_gate = 3 * n_heads
    c_q = 2 * d_gm
    c_kv = c_q + d_nsa
    c_gate = c_kv + 3 * kvw_cols
    c_ga = c_gate + n_gate
    c_gb = c_ga + d

    w = jnp.swapaxes(w_in[l], 0, 1)
    n_col_tiles = -(-c_ga // 1024)
    gmix = g_mix_norm[l][None]
    gsgu = g_sgu[l][None]
    gq = g_q[l][None]
    gk = g_k[l]
    gmlp = g_mlp_norm[l][None]
    wk_pairs, posk_rows = _compress_weights(w_cmp_k[l], pos_cmp_k[l])
    wv_pairs, posv_rows = _compress_weights(w_cmp_v[l], pos_cmp_v[l])
    pos_rows = jnp.stack([posk_rows, posv_rows])
    n_groups = w_sgu.shape[1]
    b_exp = jnp.repeat(b_sgu[l].T, d_gm // n_groups, axis=1)
    w00 = jnp.repeat(w_sgu[l][:, 0, 0], d_gm // n_groups)[None]
    b00 = jnp.repeat(b_sgu[l][:, 0], d_gm // n_groups)[None]

    n_pool, page = cache_cmp_kv.shape[1], cache_cmp_kv.shape[2]
    n_pages = page_table.shape[1]
    past = n_pages * page
    nwin = state_win_kv.shape[2]
    n_cmp_s = (past + 1 - CMP_LEN) // CMP_STRIDE + 1
    n_sel_s = -(-(past + 1) // SEL_LEN)
    nseg_s = past // CMP_STRIDE
    qi = np.arange(QBLOCK)[:, None]
    rel_s = qi + QBLOCK - np.arange(2 * QBLOCK)[None, :]
    rel_w = qi + WINDOW - np.arange(WINDOW + QBLOCK)[None, :]
    rel_c = qi + (KC_FRONT * CMP_STRIDE - CMP_LEN + 1) - CMP_STRIDE * np.arange(LANES)[None, :]
    srel_c = (past - (np.arange(nseg_s) * CMP_STRIDE + CMP_LEN - 1))[None]
    srel_s = (past - np.arange(past))[None]
    srel_w = (nwin - np.arange(nwin))[None]
    buckets = [
        _bucket(rel_c, rel_c >= 0), _bucket(rel_s, rel_s >= 0), _bucket(rel_w, (rel_w >= 0) & (rel_w < WINDOW)),
        _bucket(srel_c, (srel_c >= 0) & (np.arange(nseg_s)[None] < n_cmp_s)), _bucket(srel_s, srel_s >= 0),
        _bucket(srel_w, srel_w < WINDOW), np.zeros((1, LANES), np.int32)]
    tab_c, tab_s, tab_w, bias_c, bias_s, bias_w, bias_new = _bias_tables(
        rel_bias, buckets, [True, True, False, False, False, False, False])
    tab_c, tab_s, tab_w = [t.reshape(2, (n_heads // 2) * QBLOCK, t.shape[-1]) for t in (tab_c, tab_s, tab_w)]

    xs = x_sample.reshape(nb, d)
    h_s, v_s, oa_s, q_s, kvc_s, kvs_s, kvw_s, gates_s, w_cat = _inproj(
        xs, gmix, w, gsgu, gq, w00, b00, chunked=False, n_col_tiles=n_col_tiles)
    ov_s = np.zeros((nseg_s, LANES), np.float32)
    ov_s[:n_cmp_s, :n_sel_s] = _overlap(n_cmp_s, n_sel_s)
    expand = jnp.asarray(np.arange(LANES)[:, None] == (np.arange(past)[None, :] // SEL_LEN), BF16)
    gates_h = jnp.pad(gates_s[:, :n_gate].reshape(nb, n_heads, 3), ((0, 0), (0, 0), (0, LANES - 3)))
    lin = lambda a: a[l].reshape(a.shape[1], -1, HEAD_DIM)
    win_lin = lin(state_win_kv)
    ob_s, win_new = _nsa_sample(
        page_table, lin(cache_cmp_kv), lin(cache_sel_kv), win_lin,
        q_s.astype(F32).reshape(nb, n_heads, HEAD_DIM), gates_h,
        kvs_s.reshape(nb, 4, HEAD_DIM), kvw_s.reshape(nb, 4, HEAD_DIM), wk_pairs, wv_pairs, pos_rows, gk,
        bias_c, bias_s, bias_w, bias_new, jnp.asarray(ov_s), expand)
    y_s, (wga, wgb, wg, wn, wout, wup, wdown) = _dense_tail(
        xs, h_s, oa_s, ob_s.reshape(nb, d_nsa).astype(BF16), w, w, w_proj_gmlp[l], w_proj_nsa[l], w_out[l], gmlp,
        w_up[l], w_down[l], gate_rows=(c_ga, c_gb))

    xp = x_prompt.reshape(bsz * seq, d)
    h_p, v_p, oa_p, q_p, kvc_p, kvs_p, kvw_p, gates_p = _inproj(
        xp, gmix, w_cat, gsgu, gq, w_sgu[l], b_exp, chunked=True, n_col_tiles=n_col_tiles)
    kvs3 = kvs_p.reshape(bsz, 4 * seq, HEAD_DIM)
    kvw3 = kvw_p.reshape(bsz, 4 * seq, HEAD_DIM)
    kvc3 = kvc_p.reshape(bsz, 4 * seq, HEAD_DIM)
    ks, vs, kw, vw = _kvprep(kvs3, kvw3, gk)
    kc, vc = _compress_prompt(kvc3, wk_pairs, wv_pairs, pos_rows, gk)
    n_cmp = (seq - CMP_LEN) // CMP_STRIDE + 1
    n_selb = seq // SEL_LEN
    ov_p = np.zeros((KC_ROWS, LANES), np.float32)
    ov_p[KC_FRONT:KC_FRONT + n_cmp, :n_selb] = _overlap(n_cmp, n_selb)
    ov_p = jnp.asarray(ov_p)
    ob_p = _nsa_prompt(q_p.reshape(bsz, seq, d_nsa), gates_p.reshape(bsz, seq, LANES), kc, vc, ov_p, ks, vs, kw, vw,
                       tab_c, tab_s, tab_w)
    y_p, _ = _dense_tail(xp, h_p, oa_p, ob_p.reshape(bsz * seq, d_nsa), wga, wgb, wg, wn, wout, gmlp, wup, wdown)

    n_kv = n_heads // 4
    kv6 = lambda a, b_, t_: a.reshape(1, b_, t_, 2, n_kv, HEAD_DIM)
    nw_p = min(WINDOW, seq)
    last = ((seq - 1) // CHUNK) * CHUNK
    nw_s = min(WINDOW, past + 1)
    return (y_p.reshape(bsz, seq, d), y_s.reshape(nb, 1, d),
            kv6(kvc3, bsz, seq), kv6(kvs3, bsz, seq), kv6(kvw3[:, 4 * (seq - nw_p):], bsz, nw_p),
            v_p.reshape(bsz, seq, d_gm)[:, last:][None],
            kv6(kvc_s, nb, 1), kv6(kvs_s, nb, 1), kv6(win_new, nb, nw_s),
            v_s.reshape(1, nb, 1, d_gm))
```

```python
import functools
import math

import numpy as np
import jax
import jax.numpy as jnp
from jax import lax
from jax.experimental import pallas as pl
from jax.experimental.pallas import tpu as pltpu

F32 = jnp.float32
BF16 = jnp.bfloat16

HEAD_DIM = 128
CHUNK = 128
CMP_LEN = 32
CMP_STRIDE = 16
SEL_LEN = 64
N_SEL = 16
WINDOW = 512
N_BUCKETS = 32
MAX_DIST = 128
QBLOCK = 128
EPS = 1e-6
NEG = -1e30
HALF_NEG = -5e29
FORCE = 1e6

LANES = 128
PAD_LANE = 64
KPAD = 512
KC_FRONT = 16
KC_ROWS = 376
SAMPLE_GROUP = 2
VMEM_LIMIT = 56 * 1024 * 1024


def _cparams(sem):
    return pltpu.CompilerParams(dimension_semantics=sem, vmem_limit_bytes=VMEM_LIMIT)


def _dot(a, b):
    return jnp.dot(a, b, preferred_element_type=F32)


def _dot_nt(a, b):
    return lax.dot_general(a, b, (((1,), (1,)), ((), ())), preferred_element_type=F32)


def _dot_hilo(a, b_bf16):
    hi = a.astype(BF16)
    lo = (a - hi.astype(F32)).astype(BF16)
    return _dot(hi, b_bf16) + _dot(lo, b_bf16)


def _rms(x):
    return x * lax.rsqrt(jnp.mean(x * x, axis=-1, keepdims=True) + EPS)


def _gelu(x):
    c = math.sqrt(2.0 / math.pi)
    return 0.5 * x * (1.0 + jnp.tanh(c * (x + 0.044715 * (x * x * x))))


def _sigmoid(x):
    return 1.0 / (1.0 + jnp.exp(-x))


def _inproj_kernel(x_ref, gmix_ref, w_ref, gsgu_ref, gq_ref, wsg_ref, bsg_ref,
                   h_ref, v_ref, oa_ref, q_ref, kvc_ref, kvs_ref, kvw_ref, gate_ref, *rest, chunked, tm):
    h_s, u_s = rest[-2:]
    j = pl.program_id(1)

    half = w_ref.shape[0] // 2

    def z_cols(lo, hi):
        if len(rest) == 3:
            w = w_ref[lo:hi, :].astype(BF16)
            rest[0][lo:hi, :] = w
            return _dot_nt(h_s[...], w)
        return _dot_nt(h_s[...], w_ref[lo:hi, :])

    @pl.when(j == 0)
    def _():
        hb = (_rms(x_ref[...]) * gmix_ref[...]).astype(BF16)
        h_s[...] = hb
        h_ref[...] = hb
        za = z_cols(0, half)
        zb = z_cols(half, 2 * half)
        u_s[:, :half] = _gelu(za)
        u_s[:, half:] = _gelu(zb)

    @pl.when(j == 1)
    def _():
        za = z_cols(0, half)
        zb = z_cols(half, 2 * half)
        ga = _gelu(za)
        gb = _gelu(zb)
        ms = (jnp.sum(ga * ga, axis=-1, keepdims=True) + jnp.sum(gb * gb, axis=-1, keepdims=True)) / (2 * half)
        rstd = lax.rsqrt(ms + EPS)
        for c0, gz in ((0, ga), (half, gb)):
            v = gz * rstd * gsgu_ref[:, c0:c0 + half]
            v_ref[:, c0:c0 + half] = v
            if chunked:
                row = lax.broadcasted_iota(jnp.int32, (CHUNK, CHUNK), 0)
                col = lax.broadcasted_iota(jnp.int32, (CHUNK, CHUNK), 1)
                for g in range(half // LANES):
                    wm = jnp.where(row >= col, wsg_ref[c0 // LANES + g], 0.0).astype(BF16)
                    cs = slice(g * LANES, (g + 1) * LANES)
                    os_ = slice(c0 + g * LANES, c0 + (g + 1) * LANES)
                    for c in range(tm // CHUNK):
                        rs = slice(c * CHUNK, (c + 1) * CHUNK)
                        s = _dot(wm, v[rs, cs].astype(BF16)) + bsg_ref[:, os_]
                        oa_ref[rs, os_] = (u_s[rs, os_] * s).astype(BF16)
            else:
                cs = slice(c0, c0 + half)
                oa_ref[:, cs] = (u_s[:, cs] * (v * wsg_ref[:, cs] + bsg_ref[:, cs])).astype(BF16)

    @pl.when(j == 2)
    def _():
        scale = HEAD_DIM ** -0.5
        for c0 in (0, half):
            z = z_cols(c0, c0 + half)
            for hd in range(half // HEAD_DIM):
                cs = slice(hd * HEAD_DIM, (hd + 1) * HEAD_DIM)
                q_ref[:, c0 + hd * HEAD_DIM:c0 + (hd + 1) * HEAD_DIM] = (
                    _rms(z[:, cs]) * gq_ref[...] * scale).astype(BF16)

    def store_kv(ref, zz):
        for c in range(4):
            ref[pl.ds(c, tm, stride=4), :] = zz[:, c * LANES:(c + 1) * LANES]

    @pl.when(j == 3)
    def _():
        store_kv(kvc_ref, z_cols(0, half))
        store_kv(kvs_ref, z_cols(half, 2 * half))

    @pl.when(j == 4)
    def _():
        store_kv(kvw_ref, z_cols(0, half))
        gates = _sigmoid(z_cols(half, half + LANES))
        gate_ref[:, :LANES] = gates
        gate_ref[:, LANES:] = pltpu.roll(gates, LANES - 12, 1)
        if len(rest) == 3:
            rest[0][half + LANES:, :] = jnp.zeros((half - LANES, w_ref.shape[1]), BF16)


def _inproj(x, gmix, w_cat, gsgu, gq, wsg, bsg, *, chunked, n_col_tiles):
    n, d = x.shape
    tm = 512 if n % 512 == 0 else n
    tn = 1024
    emit = w_cat.dtype != BF16
    assert not emit or n == tm
    kern = functools.partial(_inproj_kernel, chunked=chunked, tm=tm)
    full = lambda a: pl.BlockSpec(a.shape, lambda i, j: (0,) * a.ndim)
    row = lambda w: pl.BlockSpec((tm, w), lambda i, j: (i, 0))
    kvrow = pl.BlockSpec((4 * tm, LANES), lambda i, j: (i, 0))
    out_shapes = (
        jax.ShapeDtypeStruct((n, d), BF16),
        jax.ShapeDtypeStruct((n, 1024), F32),
        jax.ShapeDtypeStruct((n, 1024), BF16),
        jax.ShapeDtypeStruct((n, 1024), BF16),
        jax.ShapeDtypeStruct((4 * n, LANES), F32),
        jax.ShapeDtypeStruct((4 * n, LANES), F32),
        jax.ShapeDtypeStruct((4 * n, LANES), F32),
        jax.ShapeDtypeStruct((n, 2 * LANES), F32),
    )
    wspec = pl.BlockSpec((tn, d), lambda i, j: (j, 0))
    out_specs = [row(d), row(1024), row(1024), row(1024), kvrow, kvrow, kvrow, row(2 * LANES)]
    if emit:
        out_shapes += (jax.ShapeDtypeStruct((n_col_tiles * tn, d), BF16),)
        out_specs.append(wspec)
    return pl.pallas_call(
        kern,
        grid=(n // tm, n_col_tiles),
        in_specs=[row(d), full(gmix), wspec, full(gsgu), full(gq), full(wsg), full(bsg)],
        out_specs=out_specs,
        out_shape=out_shapes,
        scratch_shapes=[pltpu.VMEM((tm, d), BF16), pltpu.VMEM((tm, 1024), F32)],
        compiler_params=_cparams(("arbitrary", "arbitrary")),
        name="inproj",
    )(x, gmix, w_cat, gsgu, gq, wsg, bsg)


def _kvprep_kernel(kvs_ref, kvw_ref, gk_ref, ks_ref, vs_ref, kw_ref, vw_ref):
    i = pl.program_id(1)
    rows = kvs_ref.shape[0] // 4
    col = lambda ref, c: ref[pl.ds(c, rows, stride=4), :]
    lane = lax.broadcasted_iota(jnp.int32, (rows, LANES), 1)
    row = lax.broadcasted_iota(jnp.int32, (rows, LANES), 0)

    @pl.when(i == 0)
    def _():
        aux = jnp.where(lane == PAD_LANE, 1.0, 0.0).astype(BF16)
        zk = jnp.zeros((rows, LANES), BF16)
        for g in range(2):
            ks_ref[g] = jnp.concatenate([zk, aux], axis=1)
            kw_ref[g] = jnp.concatenate([zk, aux], axis=1)
            vs_ref[g] = zk
            vw_ref[g] = zk

    @pl.when(i > 0)
    def _():
        blk = ((i - 1) * rows + row) // SEL_LEN
        onehot = jnp.where(lane == blk, 1.0, 0.0).astype(BF16)
        zaux = jnp.zeros((rows, LANES), BF16)
        for g in range(2):
            ks = (_rms(col(kvs_ref, g)) * gk_ref[1:2, :]).astype(BF16)
            kw = (_rms(col(kvw_ref, g)) * gk_ref[2:3, :]).astype(BF16)
            ks_ref[g] = jnp.concatenate([ks, onehot], axis=1)
            kw_ref[g] = jnp.concatenate([kw, zaux], axis=1)
            vs_ref[g] = col(kvs_ref, 2 + g).astype(BF16)
            vw_ref[g] = col(kvw_ref, 2 + g).astype(BF16)


def _kvprep(kvs, kvw, gk):
    b, t4, _ = kvs.shape
    t = t4 // 4
    rows = KPAD
    nblk = t // rows
    in_map = lambda bi, i: (bi, jnp.maximum(i - 1, 0), 0)
    out_map = lambda bi, i: (bi, 0, i, 0)
    kshape = jax.ShapeDtypeStruct((b, 2, KPAD + t, 2 * LANES), BF16)
    vshape = jax.ShapeDtypeStruct((b, 2, KPAD + t, LANES), BF16)
    return pl.pallas_call(
        _kvprep_kernel,
        grid=(b, nblk + 1),
        in_specs=[pl.BlockSpec((None, 4 * rows, LANES), in_map), pl.BlockSpec((None, 4 * rows, LANES), in_map),
                  pl.BlockSpec(gk.shape, lambda bi, i: (0, 0))],
        out_specs=[pl.BlockSpec((None, 2, rows, 2 * LANES), out_map), pl.BlockSpec((None, 2, rows, LANES), out_map),
                   pl.BlockSpec((None, 2, rows, 2 * LANES), out_map), pl.BlockSpec((None, 2, rows, LANES), out_map)],
        out_shape=(kshape, vshape, kshape, vshape),
        compiler_params=_cparams(("arbitrary", "arbitrary")),
        name="kvprep",
    )(kvs, kvw, gk)


def _compress_core(load_pair, nseg, wk_ref, wv_ref, pos_ref):
    w_refs = (wk_ref, wv_ref)
    accs = [jnp.zeros((2 * nseg + 16, 2 * LANES), F32) for _ in range(2)]
    for s2 in range(CMP_STRIDE // 2):
        for kv in range(2):
            parts = [load_pair(s2, 2 * kv + g) for g in range(2)]
            parts.append(pos_ref[kv, s2])
            lhs = jnp.concatenate(parts, axis=0)
            accs[kv] = accs[kv] + _dot(lhs, w_refs[kv][s2])
    outs = []
    for kv in range(2):
        y = accs[kv]
        post = y[2 * nseg:2 * nseg + 1, :LANES] + y[2 * nseg + 1:2 * nseg + 2, LANES:]
        per_g = []
        for g in range(2):
            y0 = y[g * nseg:(g + 1) * nseg, :LANES]
            y1 = y[g * nseg:(g + 1) * nseg, LANES:]
            per_g.append(y0 + pltpu.roll(y1, nseg - 1, 0) + post)
        outs.append(per_g)
    return outs


def _compress_prompt_kernel(x_ref, wk_ref, wv_ref, pos_ref, gk_ref, kc_ref, vc_ref):
    nseg = x_ref.shape[0] // (4 * CMP_STRIDE)
    load_x = lambda s, col: x_ref[pl.ds(4 * s + col, nseg, stride=4 * CMP_STRIDE), :]
    load_pair = lambda s2, col: jnp.concatenate([load_x(2 * s2, col), load_x(2 * s2 + 1, col)], axis=1).astype(BF16)
    ck, cv = _compress_core(load_pair, nseg, wk_ref, wv_ref, pos_ref)
    row = lax.broadcasted_iota(jnp.int32, (nseg, LANES), 0)
    lane = lax.broadcasted_iota(jnp.int32, (nseg, LANES), 1)
    valid = row < nseg - 1
    aux = jnp.where(valid, jnp.where(lane == row // 8, 1.0, 0.0), jnp.where(lane == PAD_LANE, 1.0, 0.0))
    back = KC_ROWS - KC_FRONT - nseg
    lane_f = lax.broadcasted_iota(jnp.int32, (KC_FRONT, LANES), 1)
    lane_b = lax.broadcasted_iota(jnp.int32, (back, LANES), 1)
    pad_f = jnp.concatenate([jnp.zeros((KC_FRONT, LANES), F32), jnp.where(lane_f == PAD_LANE, 1.0, 0.0)], axis=1)
    pad_b = jnp.concatenate([jnp.zeros((back, LANES), F32), jnp.where(lane_b == PAD_LANE, 1.0, 0.0)], axis=1)
    for g in range(2):
        kn = jnp.where(valid, _rms(ck[g]) * gk_ref[0:1, :], 0.0)
        kc_ref[g, 0:KC_FRONT, :] = pad_f
        kc_ref[g, KC_FRONT:KC_FRONT + nseg, :] = jnp.concatenate([kn, aux], axis=1)
        kc_ref[g, KC_FRONT + nseg:KC_ROWS, :] = pad_b
        vc_ref[g, 0:KC_FRONT, :] = jnp.zeros((KC_FRONT, LANES), F32)
        vc_ref[g, KC_FRONT:KC_FRONT + nseg, :] = jnp.where(valid, cv[g], 0.0)
        vc_ref[g, KC_FRONT + nseg:KC_ROWS, :] = jnp.zeros((back, LANES), F32)


def _compress_prompt(kvc, wk_pairs, wv_pairs, pos_rows, gk):
    b, t4, _ = kvc.shape
    full = lambda a: pl.BlockSpec(a.shape, lambda bi: (0,) * a.ndim)
    return pl.pallas_call(
        _compress_prompt_kernel,
        grid=(b,),
        in_specs=[pl.BlockSpec((None, t4, LANES), lambda bi: (bi, 0, 0)),
                  full(wk_pairs), full(wv_pairs), full(pos_rows), full(gk)],
        out_specs=[pl.BlockSpec((None, 2, KC_ROWS, 2 * LANES), lambda bi: (bi, 0, 0, 0)),
                   pl.BlockSpec((None, 2, KC_ROWS, LANES), lambda bi: (bi, 0, 0, 0))],
        out_shape=(jax.ShapeDtypeStruct((b, 2, KC_ROWS, 2 * LANES), F32),
                   jax.ShapeDtypeStruct((b, 2, KC_ROWS, LANES), F32)),
        compiler_params=_cparams(("arbitrary",)),
        name="compress_prompt",
    )(kvc, wk_pairs, wv_pairs, pos_rows, gk)


def _nsa_prompt_kernel(q_ref, gate_ref, kc_ref, vc_ref, ov_ref, ks_ref, vs_ref, kw_ref, vw_ref,
                       tc_ref, ts_ref, tw_ref, o_ref, m_s, l_s, acc_s, sa_s, sb_s, sw_s):
    qb = pl.program_id(2)
    t0 = qb * QBLOCK
    rq = 4 * QBLOCK
    q = q_ref[...]
    q4 = jnp.concatenate([q[:, r * LANES:(r + 1) * LANES] for r in range(4)], axis=0)
    lane = lax.broadcasted_iota(jnp.int32, (rq, LANES), 1)
    is_pad_lane = lane == PAD_LANE

    def q_aug(mb):
        return jnp.concatenate([q4, mb.astype(BF16)], axis=1)

    qa_pad = q_aug(jnp.where(is_pad_lane, NEG, 0.0))

    far_mask = ((lane < 32) & (lane >= qb - 2)) | is_pad_lane
    qa_cfar = q_aug(jnp.where(far_mask, NEG, 0.0))
    ncmp = 256
    near0 = pl.multiple_of(qb * 8, 8)
    k_far = kc_ref[KC_FRONT:KC_FRONT + ncmp, :].astype(BF16)
    k_near = kc_ref[pl.ds(near0, LANES), :].astype(BF16)
    s_far = _dot_nt(qa_cfar, k_far)
    s_near = _dot_nt(qa_pad, k_near) + tc_ref[...]
    w0 = pl.multiple_of(t0, QBLOCK)
    wlen = WINDOW + QBLOCK
    sw_s[...] = _dot_nt(qa_pad, kw_ref[pl.ds(w0, wlen), :]) + tw_ref[...]
    m = jnp.maximum(jnp.max(s_far, axis=-1, keepdims=True), jnp.max(s_near, axis=-1, keepdims=True))
    m = jnp.maximum(m, HALF_NEG)
    p_far = jnp.exp(s_far - m)
    p_near = jnp.exp(s_near - m)
    l = jnp.sum(p_far, axis=-1, keepdims=True) + jnp.sum(p_near, axis=-1, keepdims=True)
    linv = 1.0 / jnp.where(l > 0.0, l, 1.0)
    v_far = vc_ref[KC_FRONT:KC_FRONT + ncmp, :].astype(BF16)
    v_near = vc_ref[pl.ds(near0, LANES), :].astype(BF16)
    pb_far = p_far.astype(BF16)
    pb_near = p_near.astype(BF16)
    o_c = (_dot(pb_far, v_far) + _dot(pb_near, v_near)) * linv

    s = sw_s[...]
    m = jnp.max(s, axis=-1, keepdims=True)
    p = jnp.exp(s - m)
    l = jnp.sum(p, axis=-1, keepdims=True)
    o_w = _dot(p.astype(BF16), vw_ref[pl.ds(w0, wlen), :]) * (1.0 / l)

    ov_far = ov_ref[KC_FRONT:KC_FRONT + ncmp, :].astype(BF16)
    ov_near = ov_ref[pl.ds(near0, LANES), :].astype(BF16)
    imp4 = (_dot(pb_far, ov_far) + _dot(pb_near, ov_near)) * linv
    imp = sum(imp4[r * QBLOCK:(r + 1) * QBLOCK] for r in range(4))
    nblk = 64
    imp_t = imp.T[:nblk]
    blk = lax.broadcasted_iota(jnp.int32, (nblk, QBLOCK), 0)
    qpos = t0 + lax.broadcasted_iota(jnp.int32, (nblk, QBLOCK), 1)
    cur = qpos // SEL_LEN
    forced = (blk == 0) | (blk == cur) | (blk == cur - 1)
    eligible = blk * SEL_LEN <= qpos
    val = jnp.where(forced, FORCE, jnp.where(eligible, imp_t, -1.0))
    slab = 8
    vals = [val[v * slab:(v + 1) * slab] for v in range(nblk // slab)]
    ranks = [jnp.zeros((slab, QBLOCK), F32) for _ in vals]
    row_in_slab = lax.broadcasted_iota(jnp.int32, (slab, QBLOCK), 0)
    for i in range(nblk):
        vi, ii = divmod(i, slab)
        ri = jnp.broadcast_to(vals[vi][ii:ii + 1, :], (slab, QBLOCK))
        for v in range(len(vals)):
            if v < vi:
                beats = ri > vals[v]
            elif v > vi:
                beats = ri >= vals[v]
            else:
                beats = (ri > vals[v]) | ((ri == vals[v]) & (row_in_slab > ii))
            ranks[v] = ranks[v] + jnp.where(beats, 1.0, 0.0)
    rank = jnp.concatenate(ranks, axis=0)
    mb_t = jnp.where(rank < float(N_SEL), 0.0, NEG)
    row2 = lax.broadcasted_iota(jnp.int32, (LANES - nblk, QBLOCK), 0)
    mb_t = jnp.concatenate([mb_t, jnp.where(row2 == PAD_LANE - nblk, NEG, 0.0)], axis=0)
    mb = mb_t.T
    mb4 = jnp.concatenate([mb] * 4, axis=0)
    qa_snear = q_aug(mb4)
    qa_sfar = q_aug(jnp.where((lane < nblk) & (lane >= 2 * qb - 2), NEG, mb4))

    far_tile = 512
    n_far = (jnp.maximum(qb - 1, 0) * QBLOCK + far_tile - 1) // far_tile

    def far_logits(j):
        r0 = pl.multiple_of(KPAD + j * far_tile, far_tile)
        return _dot_nt(qa_sfar, ks_ref[pl.ds(r0, far_tile), :])

    sn0 = pl.multiple_of(t0 + KPAD - QBLOCK, QBLOCK)
    s = _dot_nt(qa_snear, ks_ref[pl.ds(sn0, 2 * QBLOCK), :]) + ts_ref[...]
    sa_s[...] = far_logits(0)
    m = jnp.max(s, axis=-1, keepdims=True)
    p = jnp.exp(s - m)
    l = jnp.sum(p, axis=-1, keepdims=True)
    acc = _dot(p.astype(BF16), vs_ref[pl.ds(sn0, 2 * QBLOCK), :])

    m_s[...] = m
    l_s[...] = l
    acc_s[...] = acc
    last_tile = ks_ref.shape[0] // far_tile - 2

    def far_update(j, s_ref):
        r0 = pl.multiple_of(KPAD + j * far_tile, far_tile)
        s = s_ref[...]
        m_old = m_s[...]
        m_new = jnp.maximum(m_old, jnp.max(s, axis=-1, keepdims=True))
        alpha = jnp.exp(m_old - m_new)
        p = jnp.exp(s - m_new)
        l_s[...] = alpha * l_s[...] + jnp.sum(p, axis=-1, keepdims=True)
        acc_s[...] = alpha * acc_s[...] + _dot(p.astype(BF16), vs_ref[pl.ds(r0, far_tile), :])
        m_s[...] = m_new

    def far_body(i, carry):
        a = 2 * i
        sb_s[...] = far_logits(a + 1)
        far_update(a, sa_s)
        sa_s[...] = far_logits(jnp.minimum(a + 2, last_tile))
        far_update(a + 1, sb_s)
        return carry

    lax.fori_loop(0, n_far // 2, far_body, 0)

    @pl.when(n_far % 2 == 1)
    def _():
        far_update(n_far - 1, sa_s)

    o_s = acc_s[...] * (1.0 / l_s[...])

    gt = gate_ref[...]
    gate = lambda c: gt[:, c:c + 1]
    outs = []
    for r in range(4):
        rs = slice(r * QBLOCK, (r + 1) * QBLOCK)
        outs.append(gate(3 * r) * o_c[rs] + gate(3 * r + 1) * o_s[rs] + gate(3 * r + 2) * o_w[rs])
    o_ref[...] = jnp.concatenate(outs, axis=1).astype(BF16)


def _nsa_prompt(q, gates_g, kc, vc, ov, ks, vs, kw, vw, tab_c, tab_s, tab_w):
    b, t, _ = q.shape
    nqb = t // QBLOCK
    kvspec = lambda a: pl.BlockSpec((None, None) + a.shape[2:], lambda bi, g, i: (bi, g, 0, 0))
    tabspec = lambda a: pl.BlockSpec((None,) + a.shape[1:], lambda bi, g, i: (g, 0, 0))
    return pl.pallas_call(
        _nsa_prompt_kernel,
        grid=(b, 2, nqb),
        in_specs=[pl.BlockSpec((None, QBLOCK, 512), lambda bi, g, i: (bi, i, g)),
                  pl.BlockSpec((None, QBLOCK, LANES), lambda bi, g, i: (bi, i, g)),
                  kvspec(kc), kvspec(vc), pl.BlockSpec(ov.shape, lambda bi, g, i: (0, 0)),
                  kvspec(ks), kvspec(vs), kvspec(kw), kvspec(vw),
                  tabspec(tab_c), tabspec(tab_s), tabspec(tab_w)],
        out_specs=pl.BlockSpec((None, QBLOCK, 512), lambda bi, g, i: (bi, i, g)),
        out_shape=jax.ShapeDtypeStruct((b, t, 1024), BF16),
        scratch_shapes=[pltpu.VMEM((4 * QBLOCK, 1), F32), pltpu.VMEM((4 * QBLOCK, 1), F32),
                        pltpu.VMEM((4 * QBLOCK, LANES), F32),
                        pltpu.VMEM((4 * QBLOCK, 512), F32), pltpu.VMEM((4 * QBLOCK, 512), F32),
                        pltpu.VMEM((4 * QBLOCK, WINDOW + QBLOCK), F32)],
        compiler_params=_cparams(("arbitrary", "arbitrary", "arbitrary")),
        name="nsa_prompt",
    )(q, gates_g, kc, vc, ov, ks, vs, kw, vw, tab_c, tab_s, tab_w)


def _decode_logits(qf, g0rows, gk_row, k_tiles, bias):
    qg = qf * gk_row
    zero = jnp.zeros_like(qg)
    qbd = jnp.concatenate([jnp.where(g0rows, qg, zero), jnp.where(g0rows, zero, qg)], axis=1).astype(BF16)
    one = jnp.ones_like(qg)
    ones_bd = jnp.concatenate([jnp.where(g0rows, one, zero), jnp.where(g0rows, zero, one)], axis=1).astype(BF16)
    s_parts, q_parts = [], []
    for kt in k_tiles:
        k = kt()
        s_parts.append(_dot_nt(qbd, k.astype(BF16)))
        q_parts.append(_dot_nt(ones_bd, (k * k).astype(BF16)))
    s = jnp.concatenate(s_parts, axis=1)
    ssq = jnp.concatenate(q_parts, axis=1)
    return s * lax.rsqrt(ssq * (1.0 / HEAD_DIM) + EPS) + bias, qg


def _decode_finish(s, qg, g0rows, v_tiles, key_mask, new_row, bias_new):
    if key_mask is not None:
        s = jnp.where(key_mask > 0.5, s, NEG)
    bc = lambda c: jnp.broadcast_to(new_row[c:c + 1, :], qg.shape)
    k_new = jnp.where(g0rows, bc(0), bc(1))
    v_new = jnp.where(g0rows, bc(2), bc(3))
    s_new = jnp.sum(qg * _rms(k_new), axis=-1, keepdims=True) + bias_new
    m = jnp.maximum(jnp.max(s, axis=-1, keepdims=True), s_new)
    p = jnp.exp(s - m)
    p_new = jnp.exp(s_new - m)
    l = jnp.sum(p, axis=-1, keepdims=True) + p_new
    o2 = jnp.zeros((qg.shape[0], 2 * LANES), F32)
    for i, vt in enumerate(v_tiles):
        o2 = o2 + _dot(p[:, i * LANES:(i + 1) * LANES].astype(BF16), vt().astype(BF16))
    o = jnp.where(g0rows, o2[:, :LANES], o2[:, LANES:]) + p_new * v_new
    return o * (1.0 / l)


def _nsa_sample_kernel(pt_ref, cache_c_hbm, cache_s_hbm, *refs, n_pages, n_sel_blocks, group):
    per_b_in, consts, outs = refs[:5], refs[5:-5], refs[-5:-3]
    buf_c, buf_s, sem = refs[-3:]
    step = pl.program_id(0)
    slot = step % 2

    def page_copies(src_step, dst_slot, for_wait):
        copies = []
        for e in range(group):
            for p in range(n_pages):
                pg = 0 if for_wait else pt_ref[src_step * group + e, p]
                copies.append(pltpu.make_async_copy(cache_c_hbm.at[pg], buf_c.at[dst_slot, e, p], sem.at[dst_slot, 0]))
                copies.append(pltpu.make_async_copy(cache_s_hbm.at[pg], buf_s.at[dst_slot, e, p], sem.at[dst_slot, 1]))
        return copies

    @pl.when(step == 0)
    def _():
        for c in page_copies(0, 0, False):
            c.start()

    @pl.when(step + 1 < pl.num_programs(0))
    def _():
        for c in page_copies(step + 1, 1 - slot, False):
            c.start()

    for c in page_copies(step, slot, True):
        c.wait()

    chains = [_nsa_sample_one([buf_c.at[slot, bb, p] for p in range(n_pages)],
                              [buf_s.at[slot, bb, p] for p in range(n_pages)],
                              *[r.at[bb] for r in per_b_in], *consts, *[r.at[bb] for r in outs],
                              n_pages=n_pages, n_sel_blocks=n_sel_blocks, chain_first=bb % 2 == 0)
              for bb in range(group)]
    while chains:
        chains = [c for c in chains if next(c, "done") != "done"]


def _nsa_sample_one(pages_c, pages_s, win_ref, q_ref, gate_ref, ksn_ref, kwn_ref, wk_ref, wv_ref, pos_ref, gk_ref,
                    bc_ref, bs_ref, bw_ref, bnew_ref, ov_ref, e_ref, perm_ref, o_ref, wout_ref, *,
                    n_pages, n_sel_blocks, chain_first):
    qf = q_ref[...]
    nh = qf.shape[0]
    g0rows = lax.broadcasted_iota(jnp.int32, (nh, LANES), 0) < nh // 2
    lane = lax.broadcasted_iota(jnp.int32, (nh, LANES), 1)

    page = pages_c[0].shape[0] // 4
    nseg = n_pages * page // CMP_STRIDE
    perm = perm_ref[...]
    regrouped = []
    for pp in range(n_pages // 2):
        per_cp = []
        for cp in range(2):
            blk = jnp.concatenate(
                [jnp.concatenate([pages_c[2 * pp + i][pl.ds(2 * cp + c, page, stride=4), :] for c in range(2)], axis=1)
                 for i in range(2)], axis=0).astype(BF16)
            per_cp.append(_dot(perm, blk).astype(BF16))
        regrouped.append(per_cp)
    rows_per_tap = 2 * page // CMP_STRIDE
    yield

    def tap_rows(s, col):
        lo, hi = s * rows_per_tap, (s + 1) * rows_per_tap
        return jnp.concatenate([regrouped[pp][col // 2][lo:hi, (col % 2) * LANES:(col % 2 + 1) * LANES]
                                for pp in range(n_pages // 2)], axis=0)

    load_pair = lambda s2, col: jnp.concatenate([tap_rows(2 * s2, col), tap_rows(2 * s2 + 1, col)], axis=1)
    ck, cv = _compress_core(load_pair, nseg, wk_ref, wv_ref, pos_ref)
    yield

    def selection_chain():
        kcn = jnp.concatenate([(_rms(ck[g]) * gk_ref[0:1, :]).astype(BF16) for g in range(2)], axis=0)
        s2 = _dot_nt(qf.astype(BF16), kcn)
        s = jnp.where(g0rows, s2[:, :nseg], s2[:, nseg:]) + bc_ref[...]
        m = jnp.max(s, axis=-1, keepdims=True)
        p = jnp.exp(s - m)
        linv = 1.0 / jnp.sum(p, axis=-1, keepdims=True)
        rowv = lax.broadcasted_iota(jnp.int32, (nseg, LANES), 0) < nseg - 1
        pb = p.astype(BF16)
        oc = [_dot(pb, jnp.where(rowv, cv[g], 0.0).astype(BF16)) for g in range(2)]
        o_c = jnp.where(g0rows, oc[0], oc[1]) * linv
        pn = p * linv
        s0 = jnp.sum(jnp.where(g0rows, pn, 0.0), axis=0, keepdims=True)
        s1 = jnp.sum(jnp.where(g0rows, 0.0, pn), axis=0, keepdims=True)
        psum = jnp.where(g0rows, jnp.broadcast_to(s0, pn.shape), jnp.broadcast_to(s1, pn.shape))
        imp = _dot_hilo(psum, ov_ref[...].astype(BF16))
        cur = n_sel_blocks - 1
        forced = (lane == 0) | (lane == cur) | (lane == cur - 1)
        val = jnp.where(lane >= n_sel_blocks, -2.0, jnp.where(forced, FORCE, imp))
        rank = jnp.zeros_like(val)
        for i in range(n_sel_blocks):
            ci = jnp.broadcast_to(val[:, i:i + 1], val.shape)
            beats = (ci > val) | ((ci == val) & (lane > i))
            rank = rank + jnp.where(beats, 1.0, 0.0)
        sel = jnp.where((rank < float(N_SEL)) & (lane < n_sel_blocks), 1.0, 0.0)
        return o_c, _dot(sel.astype(BF16), e_ref[...])

    def tiles(ref_list, col):
        def tile(rf, i):
            rows = lambda c: rf[pl.ds(4 * LANES * i + c, LANES, stride=4), :]
            return lambda: jnp.concatenate([rows(col), rows(col + 1)], axis=1)
        return [tile(rf, i) for rf in ref_list for i in range(rf.shape[0] // (4 * LANES))]

    b_new = bnew_ref[:, 0:1]

    def mask_free_work():
        s_sel, qg_sel = _decode_logits(qf, g0rows, gk_ref[1:2, :], tiles(pages_s, 0), bs_ref[...])
        s_win, qg_win = _decode_logits(qf, g0rows, gk_ref[2:3, :], tiles([win_ref], 0), bw_ref[...])
        o_w = _decode_finish(s_win, qg_win, g0rows, tiles([win_ref], 2), None, kwn_ref[...], b_new)
        return s_sel, qg_sel, o_w

    if chain_first:
        o_c, key_mask = selection_chain()
        yield
        s_sel, qg_sel, o_w = mask_free_work()
    else:
        s_sel, qg_sel, o_w = mask_free_work()
        yield
        o_c, key_mask = selection_chain()
    yield
    o_s = _decode_finish(s_sel, qg_sel, g0rows, tiles(pages_s, 2), key_mask, ksn_ref[...], b_new)
    yield
    gt = gate_ref[...]
    o_ref[...] = gt[:, 0:1] * o_c + gt[:, 1:2] * o_s + gt[:, 2:3] * o_w

    keep = wout_ref.shape[0] - 4
    drop = win_ref.shape[0] - keep
    wout_ref[0:keep, :] = win_ref[drop:drop + keep, :]
    wout_ref[keep:keep + 4, :] = kwn_ref[...]


def _nsa_sample(page_table, cache_c, cache_s, win, q, gates, ks_new, kw_new, wk_pairs, wv_pairs, pos_rows, gk,
                bias_c, bias_s, bias_w, bias_new, ov, expand):
    nb, n_pages = page_table.shape
    page_rows = cache_c.shape[1]
    n_sel_blocks = -(-(n_pages * (page_rows // 4) + 1) // SEL_LEN)
    group = SAMPLE_GROUP if nb % SAMPLE_GROUP == 0 else 1
    kern = functools.partial(_nsa_sample_kernel, n_pages=n_pages, n_sel_blocks=n_sel_blocks, group=group)
    win_rows_out = 4 * min(WINDOW, n_pages * (page_rows // 4) + 1)
    page = page_rows // 4
    assert n_pages % 2 == 0 and page % CMP_STRIDE == 0
    segs = page // CMP_STRIDE
    i_, n_, s_ = np.meshgrid(np.arange(2), np.arange(segs), np.arange(CMP_STRIDE), indexing="ij")
    perm_np = np.zeros((2 * page, 2 * page), np.float32)
    perm_np[(s_ * 2 * segs + i_ * segs + n_).ravel(), (i_ * page + CMP_STRIDE * n_ + s_).ravel()] = 1.0
    perm = jnp.asarray(perm_np, BF16)
    hbm = pl.BlockSpec(memory_space=pl.ANY)
    full = lambda a: pl.BlockSpec(a.shape, lambda bi, pt: (0,) * a.ndim)
    per_b = lambda a: pl.BlockSpec((group,) + a.shape[1:], lambda bi, pt: (bi,) + (0,) * (a.ndim - 1))
    page_buf = pltpu.VMEM((2, group, n_pages, page_rows, LANES), F32)
    grid_spec = pltpu.PrefetchScalarGridSpec(
        num_scalar_prefetch=1,
        grid=(nb // group,),
        scratch_shapes=[page_buf, page_buf, pltpu.SemaphoreType.DMA((2, 2))],
        in_specs=([hbm, hbm]
                  + [per_b(win), per_b(q), per_b(gates), per_b(ks_new), per_b(kw_new),
                     full(wk_pairs), full(wv_pairs), full(pos_rows), full(gk),
                     full(bias_c), full(bias_s), full(bias_w), full(bias_new), full(ov), full(expand), full(perm)]),
        out_specs=[pl.BlockSpec((group,) + q.shape[1:], lambda bi, pt: (bi, 0, 0)),
                   pl.BlockSpec((group, win_rows_out, LANES), lambda bi, pt: (bi, 0, 0))],
    )
    return pl.pallas_call(
        kern,
        grid_spec=grid_spec,
        out_shape=(jax.ShapeDtypeStruct(q.shape, F32), jax.ShapeDtypeStruct((nb, win_rows_out, LANES), F32)),
        compiler_params=_cparams(("arbitrary",)),
        name="nsa_sample",
    )(page_table, cache_c, cache_s, win, q, gates, ks_new, kw_new,
      wk_pairs, wv_pairs, pos_rows, gk, bias_c, bias_s, bias_w, bias_new, ov, expand, perm)


def _merge_kernel(h_ref, oa_ref, ob_ref, wga_ref, wgb_ref, wg_ref, wn_ref, mix_ref, *cast_refs):
    h = h_ref[...]
    wga = wga_ref[...].astype(BF16)
    wgb = wgb_ref[...].astype(BF16)
    wg = wg_ref[...].astype(BF16)
    wn = wn_ref[...].astype(BF16)
    ga = _sigmoid(_dot_nt(h, wga))
    gb = _sigmoid(_dot_nt(h, wgb))
    mix_ref[...] = (ga * _dot(oa_ref[...], wg) + gb * _dot(ob_ref[...], wn)).astype(BF16)
    for ref, val in zip(cast_refs, (wga, wgb, wg, wn)):
        ref[...] = val


def _merge(h, oa, ob, wga, wgb, wg, wn, gate_rows=None):
    n, d = h.shape
    tm = 512 if n % 512 == 0 else n
    tn = 512
    emit = gate_rows is not None
    assert not emit or n == tm
    row = lambda w: pl.BlockSpec((tm, w), lambda i, j: (i, 0))
    col = lambda k: pl.BlockSpec((k, tn), lambda i, j: (0, j))
    colt = pl.BlockSpec((tn, d), lambda i, j: (j, 0))
    out_specs = [pl.BlockSpec((tm, tn), lambda i, j: (i, j))]
    out_shape = [jax.ShapeDtypeStruct((n, d), BF16)]
    gate_specs = [colt, colt]
    if emit:
        assert all(r % 8 == 0 for r in gate_rows)
        gate_specs = [pl.BlockSpec((pl.Element(tn), pl.Element(d)),
                                   lambda i, j, r=r: ((r // 8 + j * (tn // 8)) * 8, 0)) for r in gate_rows]
        out_specs += [colt, colt, col(wg.shape[0]), col(wn.shape[0])]
        out_shape += [jax.ShapeDtypeStruct((d, d), BF16), jax.ShapeDtypeStruct((d, d), BF16),
                      jax.ShapeDtypeStruct(wg.shape, BF16), jax.ShapeDtypeStruct(wn.shape, BF16)]
    return pl.pallas_call(
        _merge_kernel,
        grid=(n // tm, d // tn),
        in_specs=[row(d), row(oa.shape[1]), row(ob.shape[1])] + gate_specs + [col(wg.shape[0]), col(wn.shape[0])],
        out_specs=out_specs,
        out_shape=out_shape,
        compiler_params=_cparams(("arbitrary", "arbitrary")),
        name="merge",
    )(h, oa, ob, wga, wgb, wg, wn)


def _outproj_kernel(x_ref, mix_ref, wout_ref, gmlp_ref, x1_ref, hm_ref, *cast_refs):
    if cast_refs:
        wout = wout_ref[...].astype(BF16)
        cast_refs[0][...] = wout
        x1 = x_ref[...] + _dot(mix_ref[...], wout)
    else:
        x1 = x_ref[...] + _dot(mix_ref[...], wout_ref[...])
    x1_ref[...] = x1
    hm_ref[...] = (_rms(x1) * gmlp_ref[...]).astype(BF16)


def _outproj(x, mix, wout, gmlp):
    n, d = x.shape
    tm = 512 if n % 512 == 0 else n
    emit = wout.dtype != BF16
    assert not emit or n == tm
    row = pl.BlockSpec((tm, d), lambda i: (i, 0))
    wspec = pl.BlockSpec(wout.shape, lambda i: (0, 0))
    out_specs = [row, row]
    out_shape = [jax.ShapeDtypeStruct((n, d), F32), jax.ShapeDtypeStruct((n, d), BF16)]
    if emit:
        out_specs.append(wspec)
        out_shape.append(jax.ShapeDtypeStruct(wout.shape, BF16))
    return pl.pallas_call(
        _outproj_kernel,
        grid=(n // tm,),
        in_specs=[row, row, wspec, pl.BlockSpec(gmlp.shape, lambda i: (0, 0))],
        out_specs=out_specs,
        out_shape=out_shape,
        compiler_params=_cparams(("arbitrary",)),
        name="outproj",
    )(x, mix, wout, gmlp)


def _ffn_kernel(hm_ref, x1_ref, wup_ref, wdown_ref, y_ref, *cast_refs):
    f = pl.program_id(1)

    @pl.when(f == 0)
    def _():
        y_ref[...] = x1_ref[...]

    wup = wup_ref[...].astype(BF16)
    wdown = wdown_ref[...].astype(BF16)
    hid = jnp.maximum(_dot(hm_ref[...], wup), 0.0)
    y_ref[...] += _dot((hid * hid).astype(BF16), wdown)
    if cast_refs:
        cast_refs[0][...] = wup
        cast_refs[1][...] = wdown


def _ffn(hm, x1, wup, wdown):
    n, d = hm.shape
    dff = wup.shape[1]
    tm = 512 if n % 512 == 0 else n
    emit = wup.dtype != BF16
    assert not emit or n == tm
    tf = 512 if emit else 1024
    row = pl.BlockSpec((tm, d), lambda i, f: (i, 0))
    up_spec = pl.BlockSpec((d, tf), lambda i, f: (0, f))
    down_spec = pl.BlockSpec((tf, d), lambda i, f: (f, 0))
    out_specs = [row]
    out_shape = [jax.ShapeDtypeStruct((n, d), F32)]
    if emit:
        out_specs += [up_spec, down_spec]
        out_shape += [jax.ShapeDtypeStruct(wup.shape, BF16), jax.ShapeDtypeStruct(wdown.shape, BF16)]
    return pl.pallas_call(
        _ffn_kernel,
        grid=(n // tm, dff // tf),
        in_specs=[row, row, up_spec, down_spec],
        out_specs=out_specs,
        out_shape=out_shape,
        compiler_params=_cparams(("arbitrary", "arbitrary")),
        name="ffn",
    )(hm, x1, wup, wdown)


def _bucket(rel, valid):
    n = np.maximum(rel, 0)
    max_exact = N_BUCKETS // 2
    nf = np.maximum(n, 1).astype(np.float32)
    large = max_exact + (np.log(nf / np.float32(max_exact)) / np.float32(math.log(MAX_DIST / max_exact))
                         * np.float32(N_BUCKETS - max_exact)).astype(np.int32)
    large = np.minimum(large, N_BUCKETS - 1)
    return np.where(valid, np.where(n < max_exact, n, large), -1).astype(np.int32)


def _bias_tables_kernel(rb_ref, *refs, shifts):
    n = len(shifts)
    nbk, nh = rb_ref.shape
    for b_ref, o_ref, shift in zip(refs[:n], refs[n:], shifts):
        b = b_ref[...]
        rows = []
        for h in range(nh):
            sh = rb_ref[nbk - 1, h] if shift else 0.0
            acc = jnp.full(b.shape, NEG, F32)
            for k in range(nbk):
                acc = jnp.where(b == k, rb_ref[k, h] - sh, acc)
            if len(o_ref.shape) == 3:
                o_ref[h] = acc
            else:
                rows.append(acc)
        if rows:
            o_ref[...] = jnp.concatenate(rows, axis=0)


def _bias_tables(rel_bias, buckets, shifts):
    nh = rel_bias.shape[1]
    shapes = [jax.ShapeDtypeStruct((nh,) + (b.shape if b.shape[0] > 1 else b.shape[1:]), F32) for b in buckets]
    vm = pl.BlockSpec(memory_space=pltpu.VMEM)
    return pl.pallas_call(
        functools.partial(_bias_tables_kernel, shifts=tuple(shifts)),
        in_specs=[pl.BlockSpec(memory_space=pltpu.SMEM)] + [vm] * len(buckets),
        out_specs=[vm] * len(buckets),
        out_shape=shapes,
        name="bias_tables",
    )(rel_bias, *[jnp.asarray(b) for b in buckets])


def _overlap(nc, ns):
    i = np.arange(nc)[:, None] * CMP_STRIDE
    j = np.arange(ns)[None, :] * SEL_LEN
    return ((i < j + SEL_LEN) & (i + CMP_LEN > j)).astype(np.float32)


def _compress_weights(w, pos):
    s = np.arange(0, CMP_STRIDE, 2)
    top = jnp.concatenate([w[s], w[CMP_STRIDE + s]], axis=2)
    bot = jnp.concatenate([w[s + 1], w[CMP_STRIDE + s + 1]], axis=2)
    tiles = jnp.concatenate([top, bot], axis=1).astype(BF16)
    row_a = jnp.concatenate([pos[s], pos[s + 1]], axis=1)
    row_b = jnp.concatenate([pos[CMP_STRIDE + s], pos[CMP_STRIDE + s + 1]], axis=1)
    rows = jnp.zeros((len(s), 16, 2 * HEAD_DIM), F32).at[:, 0].set(row_a).at[:, 1].set(row_b)
    return tiles, rows.astype(BF16)


def _dense_tail(x, h, oa, ob, wga, wgb, wg, wn, wout, gmlp, wup, wdown, gate_rows=None):
    mix, *cast_m = _merge(h, oa, ob, wga, wgb, wg, wn, gate_rows)
    x1, hm, *cast_o = _outproj(x, mix, wout, gmlp)
    y, *cast_f = _ffn(hm, x1, wup, wdown)
    return y, (cast_m or [wga, wgb, wg, wn]) + (cast_o or [wout]) + (cast_f or [wup, wdown])


def kernel(x_prompt, x_sample, cache_cmp_kv, cache_sel_kv, state_win_kv, page_table, rel_bias, g_mix_norm, w_in,
           g_sgu, w_sgu, b_sgu, g_q, g_k, pos_cmp_k, w_cmp_k, pos_cmp_v, w_cmp_v, w_proj_gmlp, w_proj_nsa, w_out,
           g_mlp_norm, w_up, w_down):
    depth = g_mix_norm.shape[0]
    assert depth == 1
    l = 0
    bsz, seq, d = x_prompt.shape
    nb = x_sample.shape[0]
    assert x_sample.shape[1] == 1 and seq % KPAD == 0
    d_gm = g_sgu.shape[1]
    n_heads = rel_bias.shape[1]
    d_nsa = n_heads * HEAD_DIM
    kvw_cols = 2 * (n_heads // 4) * HEAD_DIM
    n_gate = 3 * n_heads
    c_q = 2 * d_gm
    c_kv = c_q + d_nsa
    c_gate = c_kv + 3 * kvw_cols
    c_ga = c_gate + n_gate
    c_gb = c_ga + d

    w = jnp.swapaxes(w_in[l], 0, 1)
    n_col_tiles = -(-c_ga // 1024)
    gmix = g_mix_norm[l][None]
    gsgu = g_sgu[l][None]
    gq = g_q[l][None]
    gk = g_k[l]
    gmlp = g_mlp_norm[l][None]
    wk_pairs, posk_rows = _compress_weights(w_cmp_k[l], pos_cmp_k[l])
    wv_pairs, posv_rows = _compress_weights(w_cmp_v[l], pos_cmp_v[l])
    pos_rows = jnp.stack([posk_rows, posv_rows])
    n_groups = w_sgu.shape[1]
    b_exp = jnp.repeat(b_sgu[l].T, d_gm // n_groups, axis=1)
    w00 = jnp.repeat(w_sgu[l][:, 0, 0], d_gm // n_groups)[None]
    b00 = jnp.repeat(b_sgu[l][:, 0], d_gm // n_groups)[None]

    n_pool, page = cache_cmp_kv.shape[1], cache_cmp_kv.shape[2]
    n_pages = page_table.shape[1]
    past = n_pages * page
    nwin = state_win_kv.shape[2]
    n_cmp_s = (past + 1 - CMP_LEN) // CMP_STRIDE + 1
    n_sel_s = -(-(past + 1) // SEL_LEN)
    nseg_s = past // CMP_STRIDE
    qi = np.arange(QBLOCK)[:, None]
    rel_s = qi + QBLOCK - np.arange(2 * QBLOCK)[None, :]
    rel_w = qi + WINDOW - np.arange(WINDOW + QBLOCK)[None, :]
    rel_c = qi + (KC_FRONT * CMP_STRIDE - CMP_LEN + 1) - CMP_STRIDE * np.arange(LANES)[None, :]
    srel_c = (past - (np.arange(nseg_s) * CMP_STRIDE + CMP_LEN - 1))[None]
    srel_s = (past - np.arange(past))[None]
    srel_w = (nwin - np.arange(nwin))[None]
    buckets = [
        _bucket(rel_c, rel_c >= 0), _bucket(rel_s, rel_s >= 0), _bucket(rel_w, (rel_w >= 0) & (rel_w < WINDOW)),
        _bucket(srel_c, (srel_c >= 0) & (np.arange(nseg_s)[None] < n_cmp_s)), _bucket(srel_s, srel_s >= 0),
        _bucket(srel_w, srel_w < WINDOW), np.zeros((1, LANES), np.int32)]
    tab_c, tab_s, tab_w, bias_c, bias_s, bias_w, bias_new = _bias_tables(
        rel_bias, buckets, [True, True, False, False, False, False, False])
    tab_c, tab_s, tab_w = [t.reshape(2, (n_heads // 2) * QBLOCK, t.shape[-1]) for t in (tab_c, tab_s, tab_w)]

    xs = x_sample.reshape(nb, d)
    h_s, v_s, oa_s, q_s, kvc_s, kvs_s, kvw_s, gates_s, w_cat = _inproj(
        xs, gmix, w, gsgu, gq, w00, b00, chunked=False, n_col_tiles=n_col_tiles)
    ov_s = np.zeros((nseg_s, LANES), np.float32)
    ov_s[:n_cmp_s, :n_sel_s] = _overlap(n_cmp_s, n_sel_s)
    expand = jnp.asarray(np.arange(LANES)[:, None] == (np.arange(past)[None, :] // SEL_LEN), BF16)
    gates_h = jnp.pad(gates_s[:, :n_gate].reshape(nb, n_heads, 3), ((0, 0), (0, 0), (0, LANES - 3)))
    lin = lambda a: a[l].reshape(a.shape[1], -1, HEAD_DIM)
    win_lin = lin(state_win_kv)
    ob_s, win_new = _nsa_sample(
        page_table, lin(cache_cmp_kv), lin(cache_sel_kv), win_lin,
        q_s.astype(F32).reshape(nb, n_heads, HEAD_DIM), gates_h,
        kvs_s.reshape(nb, 4, HEAD_DIM), kvw_s.reshape(nb, 4, HEAD_DIM), wk_pairs, wv_pairs, pos_rows, gk,
        bias_c, bias_s, bias_w, bias_new, jnp.asarray(ov_s), expand)
    y_s, (wga, wgb, wg, wn, wout, wup, wdown) = _dense_tail(
        xs, h_s, oa_s, ob_s.reshape(nb, d_nsa).astype(BF16), w, w, w_proj_gmlp[l], w_proj_nsa[l], w_out[l], gmlp,
        w_up[l], w_down[l], gate_rows=(c_ga, c_gb))

    xp = x_prompt.reshape(bsz * seq, d)
    h_p, v_p, oa_p, q_p, kvc_p, kvs_p, kvw_p, gates_p = _inproj(
        xp, gmix, w_cat, gsgu, gq, w_sgu[l], b_exp, chunked=True, n_col_tiles=n_col_tiles)
    kvs3 = kvs_p.reshape(bsz, 4 * seq, HEAD_DIM)
    kvw3 = kvw_p.reshape(bsz, 4 * seq, HEAD_DIM)
    kvc3 = kvc_p.reshape(bsz, 4 * seq, HEAD_DIM)
    ks, vs, kw, vw = _kvprep(kvs3, kvw3, gk)
    kc, vc = _compress_prompt(kvc3, wk_pairs, wv_pairs, pos_rows, gk)
    n_cmp = (seq - CMP_LEN) // CMP_STRIDE + 1
    n_selb = seq // SEL_LEN
    ov_p = np.zeros((KC_ROWS, LANES), np.float32)
    ov_p[KC_FRONT:KC_FRONT + n_cmp, :n_selb] = _overlap(n_cmp, n_selb)
    ov_p = jnp.asarray(ov_p)
    ob_p = _nsa_prompt(q_p.reshape(bsz, seq, d_nsa), gates_p.reshape(bsz, seq, 2 * LANES), kc, vc, ov_p, ks, vs, kw, vw,
                       tab_c, tab_s, tab_w)
    y_p, _ = _dense_tail(xp, h_p, oa_p, ob_p.reshape(bsz * seq, d_nsa), wga, wgb, wg, wn, wout, gmlp, wup, wdown)

    n_kv = n_heads // 4
    kv6 = lambda a, b_, t_: a.reshape(1, b_, t_, 2, n_kv, HEAD_DIM)
    nw_p = min(WINDOW, seq)
    last = ((seq - 1) // CHUNK) * CHUNK
    nw_s = min(WINDOW, past + 1)
    return (y_p.reshape(bsz, seq, d), y_s.reshape(nb, 1, d),
            kv6(kvc3, bsz, seq), kv6(kvs3, bsz, seq), kv6(kvw3[:, 4 * (seq - nw_p):], bsz, nw_p),
            v_p.reshape(bsz, seq, d_gm)[:, last:][None],
            kv6(kvc_s, nb, 1), kv6(kvs_s, nb, 1), kv6(win_new, nb, nw_s),
            v_s.reshape(1, nb, 1, d_gm))
```

```python
import functools
import math

import numpy as np
import jax
import jax.numpy as jnp
from jax import lax
from jax.experimental import pallas as pl
from jax.experimental.pallas import tpu as pltpu

F32 = jnp.float32
BF16 = jnp.bfloat16

HEAD_DIM = 128
CHUNK = 128
CMP_LEN = 32
CMP_STRIDE = 16
SEL_LEN = 64
N_SEL = 16
WINDOW = 512
N_BUCKETS = 32
MAX_DIST = 128
QBLOCK = 128
EPS = 1e-6
NEG = -1e30
HALF_NEG = -5e29
FORCE = 1e6

LANES = 128
PAD_LANE = 64
KPAD = 512
KC_FRONT = 16
KC_ROWS = 376
SAMPLE_GROUP = 2
VMEM_LIMIT = 56 * 1024 * 1024


def _cparams(sem):
    return pltpu.CompilerParams(dimension_semantics=sem, vmem_limit_bytes=VMEM_LIMIT)


def _dot(a, b):
    return jnp.dot(a, b, preferred_element_type=F32)


def _dot_nt(a, b):
    return lax.dot_general(a, b, (((1,), (1,)), ((), ())), preferred_element_type=F32)


def _dot_hilo(a, b_bf16):
    hi = a.astype(BF16)
    lo = (a - hi.astype(F32)).astype(BF16)
    return _dot(hi, b_bf16) + _dot(lo, b_bf16)


def _rms(x):
    return x * lax.rsqrt(jnp.mean(x * x, axis=-1, keepdims=True) + EPS)


def _gelu(x):
    c = math.sqrt(2.0 / math.pi)
    return 0.5 * x * (1.0 + jnp.tanh(c * (x + 0.044715 * (x * x * x))))


def _sigmoid(x):
    return 1.0 / (1.0 + jnp.exp(-x))


def _inproj_kernel(x_ref, gmix_ref, w_ref, gsgu_ref, gq_ref, wsg_ref, bsg_ref,
                   h_ref, v_ref, oa_ref, q_ref, kvc_ref, kvs_ref, kvw_ref, gate_ref, *rest, chunked, tm):
    h_s, u_s = rest[-2:]
    j = pl.program_id(1)

    half = w_ref.shape[0] // 2

    def z_cols(lo, hi):
        if len(rest) == 3:
            w = w_ref[lo:hi, :].astype(BF16)
            rest[0][lo:hi, :] = w
            return _dot_nt(h_s[...], w)
        return _dot_nt(h_s[...], w_ref[lo:hi, :])

    @pl.when(j == 0)
    def _():
        hb = (_rms(x_ref[...]) * gmix_ref[...]).astype(BF16)
        h_s[...] = hb
        h_ref[...] = hb
        za = z_cols(0, half)
        zb = z_cols(half, 2 * half)
        u_s[:, :half] = _gelu(za)
        u_s[:, half:] = _gelu(zb)

    @pl.when(j == 1)
    def _():
        za = z_cols(0, half)
        zb = z_cols(half, 2 * half)
        ga = _gelu(za)
        gb = _gelu(zb)
        ms = (jnp.sum(ga * ga, axis=-1, keepdims=True) + jnp.sum(gb * gb, axis=-1, keepdims=True)) / (2 * half)
        rstd = lax.rsqrt(ms + EPS)
        for c0, gz in ((0, ga), (half, gb)):
            v = gz * rstd * gsgu_ref[:, c0:c0 + half]
            v_ref[:, c0:c0 + half] = v
            if chunked:
                row = lax.broadcasted_iota(jnp.int32, (CHUNK, CHUNK), 0)
                col = lax.broadcasted_iota(jnp.int32, (CHUNK, CHUNK), 1)
                for g in range(half // LANES):
                    wm = jnp.where(row >= col, wsg_ref[c0 // LANES + g], 0.0).astype(BF16)
                    cs = slice(g * LANES, (g + 1) * LANES)
                    os_ = slice(c0 + g * LANES, c0 + (g + 1) * LANES)
                    for c in range(tm // CHUNK):
                        rs = slice(c * CHUNK, (c + 1) * CHUNK)
                        s = _dot(wm, v[rs, cs].astype(BF16)) + bsg_ref[:, os_]
                        oa_ref[rs, os_] = (u_s[rs, os_] * s).astype(BF16)
            else:
                cs = slice(c0, c0 + half)
                oa_ref[:, cs] = (u_s[:, cs] * (v * wsg_ref[:, cs] + bsg_ref[:, cs])).astype(BF16)

    @pl.when(j == 2)
    def _():
        scale = HEAD_DIM ** -0.5
        for c0 in (0, half):
            z = z_cols(c0, c0 + half)
            for hd in range(half // HEAD_DIM):
                cs = slice(hd * HEAD_DIM, (hd + 1) * HEAD_DIM)
                q_ref[:, c0 + hd * HEAD_DIM:c0 + (hd + 1) * HEAD_DIM] = (
                    _rms(z[:, cs]) * gq_ref[...] * scale).astype(BF16)

    def store_kv(ref, zz):
        for c in range(4):
            ref[pl.ds(c, tm, stride=4), :] = zz[:, c * LANES:(c + 1) * LANES]

    @pl.when(j == 3)
    def _():
        store_kv(kvc_ref, z_cols(0, half))
        store_kv(kvs_ref, z_cols(half, 2 * half))

    @pl.when(j == 4)
    def _():
        store_kv(kvw_ref, z_cols(0, half))
        gates = _sigmoid(z_cols(half, half + LANES))
        gate_ref[:, :LANES] = gates
        gate_ref[:, LANES:] = pltpu.roll(gates, LANES - 12, 1)
        if len(rest) == 3:
            rest[0][half + LANES:, :] = jnp.zeros((half - LANES, w_ref.shape[1]), BF16)


def _inproj(x, gmix, w_cat, gsgu, gq, wsg, bsg, *, chunked, n_col_tiles):
    n, d = x.shape
    tm = 512 if n % 512 == 0 else n
    tn = 1024
    emit = w_cat.dtype != BF16
    assert not emit or n == tm
    kern = functools.partial(_inproj_kernel, chunked=chunked, tm=tm)
    full = lambda a: pl.BlockSpec(a.shape, lambda i, j: (0,) * a.ndim)
    row = lambda w: pl.BlockSpec((tm, w), lambda i, j: (i, 0))
    kvrow = pl.BlockSpec((4 * tm, LANES), lambda i, j: (i, 0))
    out_shapes = (
        jax.ShapeDtypeStruct((n, d), BF16),
        jax.ShapeDtypeStruct((n, 1024), F32),
        jax.ShapeDtypeStruct((n, 1024), BF16),
        jax.ShapeDtypeStruct((n, 1024), BF16),
        jax.ShapeDtypeStruct((4 * n, LANES), F32),
        jax.ShapeDtypeStruct((4 * n, LANES), F32),
        jax.ShapeDtypeStruct((4 * n, LANES), F32),
        jax.ShapeDtypeStruct((n, 2 * LANES), F32),
    )
    wspec = pl.BlockSpec((tn, d), lambda i, j: (j, 0))
    out_specs = [row(d), row(1024), row(1024), row(1024), kvrow, kvrow, kvrow, row(2 * LANES)]
    if emit:
        out_shapes += (jax.ShapeDtypeStruct((n_col_tiles * tn, d), BF16),)
        out_specs.append(wspec)
    return pl.pallas_call(
        kern,
        grid=(n // tm, n_col_tiles),
        in_specs=[row(d), full(gmix), wspec, full(gsgu), full(gq), full(wsg), full(bsg)],
        out_specs=out_specs,
        out_shape=out_shapes,
        scratch_shapes=[pltpu.VMEM((tm, d), BF16), pltpu.VMEM((tm, 1024), F32)],
        compiler_params=_cparams(("arbitrary", "arbitrary")),
        name="inproj",
    )(x, gmix, w_cat, gsgu, gq, wsg, bsg)


def _kvprep_kernel(kvs_ref, kvw_ref, gk_ref, ks_ref, vs_ref, kw_ref, vw_ref):
    i = pl.program_id(1)
    rows = kvs_ref.shape[0] // 4
    col = lambda ref, c: ref[pl.ds(c, rows, stride=4), :]
    lane = lax.broadcasted_iota(jnp.int32, (rows, LANES), 1)
    row = lax.broadcasted_iota(jnp.int32, (rows, LANES), 0)

    @pl.when(i == 0)
    def _():
        aux = jnp.where(lane == PAD_LANE, 1.0, 0.0).astype(BF16)
        zk = jnp.zeros((rows, LANES), BF16)
        for g in range(2):
            ks_ref[g] = jnp.concatenate([zk, aux], axis=1)
            kw_ref[g] = jnp.concatenate([zk, aux], axis=1)
            vs_ref[g] = zk
            vw_ref[g] = zk

    @pl.when(i > 0)
    def _():
        blk = ((i - 1) * rows + row) // SEL_LEN
        onehot = jnp.where(lane == blk, 1.0, 0.0).astype(BF16)
        zaux = jnp.zeros((rows, LANES), BF16)
        for g in range(2):
            ks = (_rms(col(kvs_ref, g)) * gk_ref[1:2, :]).astype(BF16)
            kw = (_rms(col(kvw_ref, g)) * gk_ref[2:3, :]).astype(BF16)
            ks_ref[g] = jnp.concatenate([ks, onehot], axis=1)
            kw_ref[g] = jnp.concatenate([kw, zaux], axis=1)
            vs_ref[g] = col(kvs_ref, 2 + g).astype(BF16)
            vw_ref[g] = col(kvw_ref, 2 + g).astype(BF16)


def _kvprep(kvs, kvw, gk):
    b, t4, _ = kvs.shape
    t = t4 // 4
    rows = KPAD
    nblk = t // rows
    in_map = lambda bi, i: (bi, jnp.maximum(i - 1, 0), 0)
    out_map = lambda bi, i: (bi, 0, i, 0)
    kshape = jax.ShapeDtypeStruct((b, 2, KPAD + t, 2 * LANES), BF16)
    vshape = jax.ShapeDtypeStruct((b, 2, KPAD + t, LANES), BF16)
    return pl.pallas_call(
        _kvprep_kernel,
        grid=(b, nblk + 1),
        in_specs=[pl.BlockSpec((None, 4 * rows, LANES), in_map), pl.BlockSpec((None, 4 * rows, LANES), in_map),
                  pl.BlockSpec(gk.shape, lambda bi, i: (0, 0))],
        out_specs=[pl.BlockSpec((None, 2, rows, 2 * LANES), out_map), pl.BlockSpec((None, 2, rows, LANES), out_map),
                   pl.BlockSpec((None, 2, rows, 2 * LANES), out_map), pl.BlockSpec((None, 2, rows, LANES), out_map)],
        out_shape=(kshape, vshape, kshape, vshape),
        compiler_params=_cparams(("arbitrary", "arbitrary")),
        name="kvprep",
    )(kvs, kvw, gk)


def _compress_core(load_pair, nseg, wk_ref, wv_ref, pos_ref):
    w_refs = (wk_ref, wv_ref)
    accs = [jnp.zeros((2 * nseg + 16, 2 * LANES), F32) for _ in range(2)]
    for s2 in range(CMP_STRIDE // 2):
        for kv in range(2):
            parts = [load_pair(s2, 2 * kv + g) for g in range(2)]
            parts.append(pos_ref[kv, s2])
            lhs = jnp.concatenate(parts, axis=0)
            accs[kv] = accs[kv] + _dot(lhs, w_refs[kv][s2])
    outs = []
    for kv in range(2):
        y = accs[kv]
        post = y[2 * nseg:2 * nseg + 1, :LANES] + y[2 * nseg + 1:2 * nseg + 2, LANES:]
        per_g = []
        for g in range(2):
            y0 = y[g * nseg:(g + 1) * nseg, :LANES]
            y1 = y[g * nseg:(g + 1) * nseg, LANES:]
            per_g.append(y0 + pltpu.roll(y1, nseg - 1, 0) + post)
        outs.append(per_g)
    return outs


def _compress_prompt_kernel(x_ref, wk_ref, wv_ref, pos_ref, gk_ref, kc_ref, vc_ref):
    nseg = x_ref.shape[0] // (4 * CMP_STRIDE)
    load_x = lambda s, col: x_ref[pl.ds(4 * s + col, nseg, stride=4 * CMP_STRIDE), :]
    load_pair = lambda s2, col: jnp.concatenate([load_x(2 * s2, col), load_x(2 * s2 + 1, col)], axis=1).astype(BF16)
    ck, cv = _compress_core(load_pair, nseg, wk_ref, wv_ref, pos_ref)
    row = lax.broadcasted_iota(jnp.int32, (nseg, LANES), 0)
    lane = lax.broadcasted_iota(jnp.int32, (nseg, LANES), 1)
    valid = row < nseg - 1
    aux = jnp.where(valid, jnp.where(lane == row // 8, 1.0, 0.0), jnp.where(lane == PAD_LANE, 1.0, 0.0))
    back = KC_ROWS - KC_FRONT - nseg
    lane_f = lax.broadcasted_iota(jnp.int32, (KC_FRONT, LANES), 1)
    lane_b = lax.broadcasted_iota(jnp.int32, (back, LANES), 1)
    pad_f = jnp.concatenate([jnp.zeros((KC_FRONT, LANES), F32), jnp.where(lane_f == PAD_LANE, 1.0, 0.0)], axis=1)
    pad_b = jnp.concatenate([jnp.zeros((back, LANES), F32), jnp.where(lane_b == PAD_LANE, 1.0, 0.0)], axis=1)
    for g in range(2):
        kn = jnp.where(valid, _rms(ck[g]) * gk_ref[0:1, :], 0.0)
        kc_ref[g, 0:KC_FRONT, :] = pad_f
        kc_ref[g, KC_FRONT:KC_FRONT + nseg, :] = jnp.concatenate([kn, aux], axis=1)
        kc_ref[g, KC_FRONT + nseg:KC_ROWS, :] = pad_b
        vc_ref[g, 0:KC_FRONT, :] = jnp.zeros((KC_FRONT, LANES), F32)
        vc_ref[g, KC_FRONT:KC_FRONT + nseg, :] = jnp.where(valid, cv[g], 0.0)
        vc_ref[g, KC_FRONT + nseg:KC_ROWS, :] = jnp.zeros((back, LANES), F32)


def _compress_prompt(kvc, wk_pairs, wv_pairs, pos_rows, gk):
    b, t4, _ = kvc.shape
    full = lambda a: pl.BlockSpec(a.shape, lambda bi: (0,) * a.ndim)
    return pl.pallas_call(
        _compress_prompt_kernel,
        grid=(b,),
        in_specs=[pl.BlockSpec((None, t4, LANES), lambda bi: (bi, 0, 0)),
                  full(wk_pairs), full(wv_pairs), full(pos_rows), full(gk)],
        out_specs=[pl.BlockSpec((None, 2, KC_ROWS, 2 * LANES), lambda bi: (bi, 0, 0, 0)),
                   pl.BlockSpec((None, 2, KC_ROWS, LANES), lambda bi: (bi, 0, 0, 0))],
        out_shape=(jax.ShapeDtypeStruct((b, 2, KC_ROWS, 2 * LANES), F32),
                   jax.ShapeDtypeStruct((b, 2, KC_ROWS, LANES), F32)),
        compiler_params=_cparams(("arbitrary",)),
        name="compress_prompt",
    )(kvc, wk_pairs, wv_pairs, pos_rows, gk)


def _nsa_prompt_kernel(q_ref, gate_ref, kc_ref, vc_ref, ov_ref, ks_ref, vs_ref, kw_ref, vw_ref,
                       tc_ref, ts_ref, tw_ref, o_ref, m_s, l_s, acc_s, sa_s, sb_s, sw_s):
    qb = pl.program_id(2)
    t0 = qb * QBLOCK
    rq = 4 * QBLOCK
    q = q_ref[...]
    q4 = jnp.concatenate([q[:, r * LANES:(r + 1) * LANES] for r in range(4)], axis=0)
    lane = lax.broadcasted_iota(jnp.int32, (rq, LANES), 1)
    is_pad_lane = lane == PAD_LANE

    def q_aug(mb):
        return jnp.concatenate([q4, mb.astype(BF16)], axis=1)

    qa_pad = q_aug(jnp.where(is_pad_lane, NEG, 0.0))

    far_mask = ((lane < 32) & (lane >= qb - 2)) | is_pad_lane
    qa_cfar = q_aug(jnp.where(far_mask, NEG, 0.0))
    ncmp = 256
    near0 = pl.multiple_of(qb * 8, 8)
    k_far = kc_ref[KC_FRONT:KC_FRONT + ncmp, :].astype(BF16)
    k_near = kc_ref[pl.ds(near0, LANES), :].astype(BF16)
    s_far = _dot_nt(qa_cfar, k_far)
    s_near = _dot_nt(qa_pad, k_near) + tc_ref[...]
    w0 = pl.multiple_of(t0, QBLOCK)
    wlen = WINDOW + QBLOCK
    sw_s[...] = _dot_nt(qa_pad, kw_ref[pl.ds(w0, wlen), :]) + tw_ref[...]
    m = jnp.maximum(jnp.max(s_far, axis=-1, keepdims=True), jnp.max(s_near, axis=-1, keepdims=True))
    m = jnp.maximum(m, HALF_NEG)
    p_far = jnp.exp(s_far - m)
    p_near = jnp.exp(s_near - m)
    l = jnp.sum(p_far, axis=-1, keepdims=True) + jnp.sum(p_near, axis=-1, keepdims=True)
    linv = 1.0 / jnp.where(l > 0.0, l, 1.0)
    v_far = vc_ref[KC_FRONT:KC_FRONT + ncmp, :].astype(BF16)
    v_near = vc_ref[pl.ds(near0, LANES), :].astype(BF16)
    pb_far = p_far.astype(BF16)
    pb_near = p_near.astype(BF16)
    o_c = (_dot(pb_far, v_far) + _dot(pb_near, v_near)) * linv

    s = sw_s[...]
    m = jnp.max(s, axis=-1, keepdims=True)
    p = jnp.exp(s - m)
    l = jnp.sum(p, axis=-1, keepdims=True)
    o_w = _dot(p.astype(BF16), vw_ref[pl.ds(w0, wlen), :]) * (1.0 / l)

    ov_far = ov_ref[KC_FRONT:KC_FRONT + ncmp, :].astype(BF16)
    ov_near = ov_ref[pl.ds(near0, LANES), :].astype(BF16)
    imp4 = (_dot(pb_far, ov_far) + _dot(pb_near, ov_near)) * linv
    imp = sum(imp4[r * QBLOCK:(r + 1) * QBLOCK] for r in range(4))
    nblk = 64
    imp_t = imp.T[:nblk]
    blk = lax.broadcasted_iota(jnp.int32, (nblk, QBLOCK), 0)
    qpos = t0 + lax.broadcasted_iota(jnp.int32, (nblk, QBLOCK), 1)
    cur = qpos // SEL_LEN
    forced = (blk == 0) | (blk == cur) | (blk == cur - 1)
    eligible = blk * SEL_LEN <= qpos
    val = jnp.where(forced, FORCE, jnp.where(eligible, imp_t, -1.0))
    slab = 8
    vals = [val[v * slab:(v + 1) * slab] for v in range(nblk // slab)]
    ranks = [jnp.zeros((slab, QBLOCK), F32) for _ in vals]
    row_in_slab = lax.broadcasted_iota(jnp.int32, (slab, QBLOCK), 0)
    for i in range(nblk):
        vi, ii = divmod(i, slab)
        ri = jnp.broadcast_to(vals[vi][ii:ii + 1, :], (slab, QBLOCK))
        for v in range(len(vals)):
            if v < vi:
                beats = ri > vals[v]
            elif v > vi:
                beats = ri >= vals[v]
            else:
                beats = (ri > vals[v]) | ((ri == vals[v]) & (row_in_slab > ii))
            ranks[v] = ranks[v] + jnp.where(beats, 1.0, 0.0)
    rank = jnp.concatenate(ranks, axis=0)
    mb_t = jnp.where(rank < float(N_SEL), 0.0, NEG)
    row2 = lax.broadcasted_iota(jnp.int32, (LANES - nblk, QBLOCK), 0)
    mb_t = jnp.concatenate([mb_t, jnp.where(row2 == PAD_LANE - nblk, NEG, 0.0)], axis=0)
    mb = mb_t.T
    mb4 = jnp.concatenate([mb] * 4, axis=0)
    qa_snear = q_aug(mb4)
    qa_sfar = q_aug(jnp.where((lane < nblk) & (lane >= 2 * qb - 2), NEG, mb4))

    far_tile = 512
    n_far = (jnp.maximum(qb - 1, 0) * QBLOCK + far_tile - 1) // far_tile

    def far_logits(j):
        r0 = pl.multiple_of(KPAD + j * far_tile, far_tile)
        return _dot_nt(qa_sfar, ks_ref[pl.ds(r0, far_tile), :])

    sn0 = pl.multiple_of(t0 + KPAD - QBLOCK, QBLOCK)
    s = _dot_nt(qa_snear, ks_ref[pl.ds(sn0, 2 * QBLOCK), :]) + ts_ref[...]
    sa_s[...] = far_logits(0)
    m = jnp.max(s, axis=-1, keepdims=True)
    p = jnp.exp(s - m)
    l = jnp.sum(p, axis=-1, keepdims=True)
    acc = _dot(p.astype(BF16), vs_ref[pl.ds(sn0, 2 * QBLOCK), :])

    m_s[...] = m
    l_s[...] = l
    acc_s[...] = acc
    last_tile = ks_ref.shape[0] // far_tile - 2

    def far_update(j, s_ref):
        r0 = pl.multiple_of(KPAD + j * far_tile, far_tile)
        s = s_ref[...]
        m_old = m_s[...]
        m_new = jnp.maximum(m_old, jnp.max(s, axis=-1, keepdims=True))
        alpha = jnp.exp(m_old - m_new)
        p = jnp.exp(s - m_new)
        l_s[...] = alpha * l_s[...] + jnp.sum(p, axis=-1, keepdims=True)
        acc_s[...] = alpha * acc_s[...] + _dot(p.astype(BF16), vs_ref[pl.ds(r0, far_tile), :])
        m_s[...] = m_new

    def far_body(i, carry):
        a = 2 * i
        sb_s[...] = far_logits(a + 1)
        far_update(a, sa_s)
        sa_s[...] = far_logits(jnp.minimum(a + 2, last_tile))
        far_update(a + 1, sb_s)
        return carry

    lax.fori_loop(0, n_far // 2, far_body, 0)

    @pl.when(n_far % 2 == 1)
    def _():
        far_update(n_far - 1, sa_s)

    o_s = acc_s[...] * (1.0 / l_s[...])

    gt = gate_ref[...]
    gate = lambda c: gt[:, c:c + 1]
    outs = []
    for r in range(4):
        rs = slice(r * QBLOCK, (r + 1) * QBLOCK)
        outs.append(gate(3 * r) * o_c[rs] + gate(3 * r + 1) * o_s[rs] + gate(3 * r + 2) * o_w[rs])
    o_ref[...] = jnp.concatenate(outs, axis=1).astype(BF16)


def _nsa_prompt(q, gates_g, kc, vc, ov, ks, vs, kw, vw, tab_c, tab_s, tab_w):
    b, t, _ = q.shape
    nqb = t // QBLOCK
    kvspec = lambda a: pl.BlockSpec((None, None) + a.shape[2:], lambda bi, g, i: (bi, g, 0, 0))
    tabspec = lambda a: pl.BlockSpec((None,) + a.shape[1:], lambda bi, g, i: (g, 0, 0))
    return pl.pallas_call(
        _nsa_prompt_kernel,
        grid=(b, 2, nqb),
        in_specs=[pl.BlockSpec((None, QBLOCK, 512), lambda bi, g, i: (bi, i, g)),
                  pl.BlockSpec((None, QBLOCK, LANES), lambda bi, g, i: (bi, i, g)),
                  kvspec(kc), kvspec(vc), pl.BlockSpec(ov.shape, lambda bi, g, i: (0, 0)),
                  kvspec(ks), kvspec(vs), kvspec(kw), kvspec(vw),
                  tabspec(tab_c), tabspec(tab_s), tabspec(tab_w)],
        out_specs=pl.BlockSpec((None, QBLOCK, 512), lambda bi, g, i: (bi, i, g)),
        out_shape=jax.ShapeDtypeStruct((b, t, 1024), BF16),
        scratch_shapes=[pltpu.VMEM((4 * QBLOCK, 1), F32), pltpu.VMEM((4 * QBLOCK, 1), F32),
                        pltpu.VMEM((4 * QBLOCK, LANES), F32),
                        pltpu.VMEM((4 * QBLOCK, 512), F32), pltpu.VMEM((4 * QBLOCK, 512), F32),
                        pltpu.VMEM((4 * QBLOCK, WINDOW + QBLOCK), F32)],
        compiler_params=_cparams(("arbitrary", "arbitrary", "arbitrary")),
        name="nsa_prompt",
    )(q, gates_g, kc, vc, ov, ks, vs, kw, vw, tab_c, tab_s, tab_w)


def _decode_logits(qf, g0rows, gk_row, k_tiles, bias):
    qg = qf * gk_row
    zero = jnp.zeros_like(qg)
    qbd = jnp.concatenate([jnp.where(g0rows, qg, zero), jnp.where(g0rows, zero, qg)], axis=1).astype(BF16)
    one = jnp.ones_like(qg)
    ones_bd = jnp.concatenate([jnp.where(g0rows, one, zero), jnp.where(g0rows, zero, one)], axis=1).astype(BF16)
    s_parts, q_parts = [], []
    for kt in k_tiles:
        k = kt()
        s_parts.append(_dot_nt(qbd, k.astype(BF16)))
        q_parts.append(_dot_nt(ones_bd, (k * k).astype(BF16)))
    s = jnp.concatenate(s_parts, axis=1)
    ssq = jnp.concatenate(q_parts, axis=1)
    return s * lax.rsqrt(ssq * (1.0 / HEAD_DIM) + EPS) + bias, qg


def _decode_finish(s, qg, g0rows, v_tiles, key_mask, new_row, bias_new):
    if key_mask is not None:
        s = jnp.where(key_mask > 0.5, s, NEG)
    bc = lambda c: jnp.broadcast_to(new_row[c:c + 1, :], qg.shape)
    k_new = jnp.where(g0rows, bc(0), bc(1))
    v_new = jnp.where(g0rows, bc(2), bc(3))
    s_new = jnp.sum(qg * _rms(k_new), axis=-1, keepdims=True) + bias_new
    m = jnp.maximum(jnp.max(s, axis=-1, keepdims=True), s_new)
    p = jnp.exp(s - m)
    p_new = jnp.exp(s_new - m)
    l = jnp.sum(p, axis=-1, keepdims=True) + p_new
    o2 = jnp.zeros((qg.shape[0], 2 * LANES), F32)
    for i, vt in enumerate(v_tiles):
        o2 = o2 + _dot(p[:, i * LANES:(i + 1) * LANES].astype(BF16), vt().astype(BF16))
    o = jnp.where(g0rows, o2[:, :LANES], o2[:, LANES:]) + p_new * v_new
    return o * (1.0 / l)


def _nsa_sample_kernel(pt_ref, cache_c_hbm, cache_s_hbm, *refs, n_pages, n_sel_blocks, group):
    per_b_in, consts, outs = refs[:5], refs[5:-5], refs[-5:-3]
    buf_c, buf_s, sem = refs[-3:]
    step = pl.program_id(0)
    slot = step % 2

    def page_copies(src_step, dst_slot, for_wait):
        copies = []
        for e in range(group):
            for p in range(n_pages):
                pg = 0 if for_wait else pt_ref[src_step * group + e, p]
                copies.append(pltpu.make_async_copy(cache_c_hbm.at[pg], buf_c.at[dst_slot, e, p], sem.at[dst_slot, 0]))
                copies.append(pltpu.make_async_copy(cache_s_hbm.at[pg], buf_s.at[dst_slot, e, p], sem.at[dst_slot, 1]))
        return copies

    @pl.when(step == 0)
    def _():
        for c in page_copies(0, 0, False):
            c.start()

    @pl.when(step + 1 < pl.num_programs(0))
    def _():
        for c in page_copies(step + 1, 1 - slot, False):
            c.start()

    for c in page_copies(step, slot, True):
        c.wait()

    chains = [_nsa_sample_one([buf_c.at[slot, bb, p] for p in range(n_pages)],
                              [buf_s.at[slot, bb, p] for p in range(n_pages)],
                              *[r.at[bb] for r in per_b_in], *consts, *[r.at[bb] for r in outs],
                              n_pages=n_pages, n_sel_blocks=n_sel_blocks, chain_first=bb % 2 == 0)
              for bb in range(group)]
    while chains:
        chains = [c for c in chains if next(c, "done") != "done"]


def _nsa_sample_one(pages_c, pages_s, win_ref, q_ref, gate_ref, ksn_ref, kwn_ref, wk_ref, wv_ref, pos_ref, gk_ref,
                    bc_ref, bs_ref, bw_ref, bnew_ref, ov_ref, e_ref, perm_ref, o_ref, wout_ref, *,
                    n_pages, n_sel_blocks, chain_first):
    qf = q_ref[...]
    nh = qf.shape[0]
    g0rows = lax.broadcasted_iota(jnp.int32, (nh, LANES), 0) < nh // 2
    lane = lax.broadcasted_iota(jnp.int32, (nh, LANES), 1)

    page = pages_c[0].shape[0] // 4
    nseg = n_pages * page // CMP_STRIDE
    perm = perm_ref[...]
    regrouped = []
    for pp in range(n_pages // 2):
        per_cp = []
        for cp in range(2):
            blk = jnp.concatenate(
                [jnp.concatenate([pages_c[2 * pp + i][pl.ds(2 * cp + c, page, stride=4), :] for c in range(2)], axis=1)
                 for i in range(2)], axis=0).astype(BF16)
            per_cp.append(_dot(perm, blk).astype(BF16))
        regrouped.append(per_cp)
    rows_per_tap = 2 * page // CMP_STRIDE
    yield

    def tap_rows(s, col):
        lo, hi = s * rows_per_tap, (s + 1) * rows_per_tap
        return jnp.concatenate([regrouped[pp][col // 2][lo:hi, (col % 2) * LANES:(col % 2 + 1) * LANES]
                                for pp in range(n_pages // 2)], axis=0)

    load_pair = lambda s2, col: jnp.concatenate([tap_rows(2 * s2, col), tap_rows(2 * s2 + 1, col)], axis=1)
    ck, cv = _compress_core(load_pair, nseg, wk_ref, wv_ref, pos_ref)
    yield

    def selection_chain():
        kcn = jnp.concatenate([(_rms(ck[g]) * gk_ref[0:1, :]).astype(BF16) for g in range(2)], axis=0)
        s2 = _dot_nt(qf.astype(BF16), kcn)
        s = jnp.where(g0rows, s2[:, :nseg], s2[:, nseg:]) + bc_ref[...]
        m = jnp.max(s, axis=-1, keepdims=True)
        p = jnp.exp(s - m)
        linv = 1.0 / jnp.sum(p, axis=-1, keepdims=True)
        rowv = lax.broadcasted_iota(jnp.int32, (nseg, LANES), 0) < nseg - 1
        pb = p.astype(BF16)
        oc = [_dot(pb, jnp.where(rowv, cv[g], 0.0).astype(BF16)) for g in range(2)]
        o_c = jnp.where(g0rows, oc[0], oc[1]) * linv
        pn = p * linv
        s0 = jnp.sum(jnp.where(g0rows, pn, 0.0), axis=0, keepdims=True)
        s1 = jnp.sum(jnp.where(g0rows, 0.0, pn), axis=0, keepdims=True)
        psum = jnp.where(g0rows, jnp.broadcast_to(s0, pn.shape), jnp.broadcast_to(s1, pn.shape))
        imp = _dot_hilo(psum, ov_ref[...].astype(BF16))
        cur = n_sel_blocks - 1
        forced = (lane == 0) | (lane == cur) | (lane == cur - 1)
        val = jnp.where(lane >= n_sel_blocks, -2.0, jnp.where(forced, FORCE, imp))
        rank = jnp.zeros_like(val)
        for i in range(n_sel_blocks):
            ci = jnp.broadcast_to(val[:, i:i + 1], val.shape)
            beats = (ci > val) | ((ci == val) & (lane > i))
            rank = rank + jnp.where(beats, 1.0, 0.0)
        sel = jnp.where((rank < float(N_SEL)) & (lane < n_sel_blocks), 1.0, 0.0)
        return o_c, _dot(sel.astype(BF16), e_ref[...])

    def tiles(ref_list, col):
        def tile(rf, i):
            rows = lambda c: rf[pl.ds(4 * LANES * i + c, LANES, stride=4), :]
            return lambda: jnp.concatenate([rows(col), rows(col + 1)], axis=1)
        return [tile(rf, i) for rf in ref_list for i in range(rf.shape[0] // (4 * LANES))]

    b_new = bnew_ref[:, 0:1]

    def mask_free_work():
        s_sel, qg_sel = _decode_logits(qf, g0rows, gk_ref[1:2, :], tiles(pages_s, 0), bs_ref[...])
        s_win, qg_win = _decode_logits(qf, g0rows, gk_ref[2:3, :], tiles([win_ref], 0), bw_ref[...])
        o_w = _decode_finish(s_win, qg_win, g0rows, tiles([win_ref], 2), None, kwn_ref[...], b_new)
        return s_sel, qg_sel, o_w

    if chain_first:
        o_c, key_mask = selection_chain()
        yield
        s_sel, qg_sel, o_w = mask_free_work()
    else:
        s_sel, qg_sel, o_w = mask_free_work()
        yield
        o_c, key_mask = selection_chain()
    yield
    o_s = _decode_finish(s_sel, qg_sel, g0rows, tiles(pages_s, 2), key_mask, ksn_ref[...], b_new)
    yield
    gt = gate_ref[...]
    o_ref[...] = gt[:, 0:1] * o_c + gt[:, 1:2] * o_s + gt[:, 2:3] * o_w

    keep = wout_ref.shape[0] - 4
    drop = win_ref.shape[0] - keep
    wout_ref[0:keep, :] = win_ref[drop:drop + keep, :]
    wout_ref[keep:keep + 4, :] = kwn_ref[...]


def _nsa_sample(page_table, cache_c, cache_s, win, q, gates, ks_new, kw_new, wk_pairs, wv_pairs, pos_rows, gk,
                bias_c, bias_s, bias_w, bias_new, ov, expand):
    nb, n_pages = page_table.shape
    page_rows = cache_c.shape[1]
    n_sel_blocks = -(-(n_pages * (page_rows // 4) + 1) // SEL_LEN)
    group = SAMPLE_GROUP if nb % SAMPLE_GROUP == 0 else 1
    kern = functools.partial(_nsa_sample_kernel, n_pages=n_pages, n_sel_blocks=n_sel_blocks, group=group)
    win_rows_out = 4 * min(WINDOW, n_pages * (page_rows // 4) + 1)
    page = page_rows // 4
    assert n_pages % 2 == 0 and page % CMP_STRIDE == 0
    segs = page // CMP_STRIDE
    i_, n_, s_ = np.meshgrid(np.arange(2), np.arange(segs), np.arange(CMP_STRIDE), indexing="ij")
    perm_np = np.zeros((2 * page, 2 * page), np.float32)
    perm_np[(s_ * 2 * segs + i_ * segs + n_).ravel(), (i_ * page + CMP_STRIDE * n_ + s_).ravel()] = 1.0
    perm = jnp.asarray(perm_np, BF16)
    hbm = pl.BlockSpec(memory_space=pl.ANY)
    full = lambda a: pl.BlockSpec(a.shape, lambda bi, pt: (0,) * a.ndim)
    per_b = lambda a: pl.BlockSpec((group,) + a.shape[1:], lambda bi, pt: (bi,) + (0,) * (a.ndim - 1))
    page_buf = pltpu.VMEM((2, group, n_pages, page_rows, LANES), F32)
    grid_spec = pltpu.PrefetchScalarGridSpec(
        num_scalar_prefetch=1,
        grid=(nb // group,),
        scratch_shapes=[page_buf, page_buf, pltpu.SemaphoreType.DMA((2, 2))],
        in_specs=([hbm, hbm]
                  + [per_b(win), per_b(q), per_b(gates), per_b(ks_new), per_b(kw_new),
                     full(wk_pairs), full(wv_pairs), full(pos_rows), full(gk),
                     full(bias_c), full(bias_s), full(bias_w), full(bias_new), full(ov), full(expand), full(perm)]),
        out_specs=[pl.BlockSpec((group,) + q.shape[1:], lambda bi, pt: (bi, 0, 0)),
                   pl.BlockSpec((group, win_rows_out, LANES), lambda bi, pt: (bi, 0, 0))],
    )
    return pl.pallas_call(
        kern,
        grid_spec=grid_spec,
        out_shape=(jax.ShapeDtypeStruct(q.shape, F32), jax.ShapeDtypeStruct((nb, win_rows_out, LANES), F32)),
        compiler_params=_cparams(("arbitrary",)),
        name="nsa_sample",
    )(page_table, cache_c, cache_s, win, q, gates, ks_new, kw_new,
      wk_pairs, wv_pairs, pos_rows, gk, bias_c, bias_s, bias_w, bias_new, ov, expand, perm)


def _merge_kernel(h_ref, oa_ref, ob_ref, wga_ref, wgb_ref, wg_ref, wn_ref, mix_ref, *cast_refs):
    h = h_ref[...]
    wga = wga_ref[...].astype(BF16)
    wgb = wgb_ref[...].astype(BF16)
    wg = wg_ref[...].astype(BF16)
    wn = wn_ref[...].astype(BF16)
    ga = _sigmoid(_dot_nt(h, wga))
    gb = _sigmoid(_dot_nt(h, wgb))
    mix_ref[...] = (ga * _dot(oa_ref[...], wg) + gb * _dot(ob_ref[...], wn)).astype(BF16)
    for ref, val in zip(cast_refs, (wga, wgb, wg, wn)):
        ref[...] = val


def _merge(h, oa, ob, wga, wgb, wg, wn, gate_rows=None):
    n, d = h.shape
    tm = 512 if n % 512 == 0 else n
    tn = 512
    emit = gate_rows is not None
    assert not emit or n == tm
    row = lambda w: pl.BlockSpec((tm, w), lambda i, j: (i, 0))
    col = lambda k: pl.BlockSpec((k, tn), lambda i, j: (0, j))
    colt = pl.BlockSpec((tn, d), lambda i, j: (j, 0))
    out_specs = [pl.BlockSpec((tm, tn), lambda i, j: (i, j))]
    out_shape = [jax.ShapeDtypeStruct((n, d), BF16)]
    gate_specs = [colt, colt]
    if emit:
        assert all(r % 8 == 0 for r in gate_rows)
        gate_specs = [pl.BlockSpec((pl.Element(tn), pl.Element(d)),
                                   lambda i, j, r=r: ((r // 8 + j * (tn // 8)) * 8, 0)) for r in gate_rows]
        out_specs += [colt, colt, col(wg.shape[0]), col(wn.shape[0])]
        out_shape += [jax.ShapeDtypeStruct((d, d), BF16), jax.ShapeDtypeStruct((d, d), BF16),
                      jax.ShapeDtypeStruct(wg.shape, BF16), jax.ShapeDtypeStruct(wn.shape, BF16)]
    return pl.pallas_call(
        _merge_kernel,
        grid=(n // tm, d // tn),
        in_specs=[row(d), row(oa.shape[1]), row(ob.shape[1])] + gate_specs + [col(wg.shape[0]), col(wn.shape[0])],
        out_specs=out_specs,
        out_shape=out_shape,
        compiler_params=_cparams(("arbitrary", "arbitrary")),
        name="merge",
    )(h, oa, ob, wga, wgb, wg, wn)


def _outproj_kernel(x_ref, mix_ref, wout_ref, gmlp_ref, x1_ref, hm_ref, *cast_refs):
    if cast_refs:
        wout = wout_ref[...].astype(BF16)
        cast_refs[0][...] = wout
        x1 = x_ref[...] + _dot(mix_ref[...], wout)
    else:
        x1 = x_ref[...] + _dot(mix_ref[...], wout_ref[...])
    x1_ref[...] = x1
    hm_ref[...] = (_rms(x1) * gmlp_ref[...]).astype(BF16)


def _outproj(x, mix, wout, gmlp):
    n, d = x.shape
    tm = 512 if n % 512 == 0 else n
    emit = wout.dtype != BF16
    assert not emit or n == tm
    row = pl.BlockSpec((tm, d), lambda i: (i, 0))
    wspec = pl.BlockSpec(wout.shape, lambda i: (0, 0))
    out_specs = [row, row]
    out_shape = [jax.ShapeDtypeStruct((n, d), F32), jax.ShapeDtypeStruct((n, d), BF16)]
    if emit:
        out_specs.append(wspec)
        out_shape.append(jax.ShapeDtypeStruct(wout.shape, BF16))
    return pl.pallas_call(
        _outproj_kernel,
        grid=(n // tm,),
        in_specs=[row, row, wspec, pl.BlockSpec(gmlp.shape, lambda i: (0, 0))],
        out_specs=out_specs,
        out_shape=out_shape,
        compiler_params=_cparams(("arbitrary",)),
        name="outproj",
    )(x, mix, wout, gmlp)


def _ffn_kernel(hm_ref, x1_ref, wup_ref, wdown_ref, y_ref, *cast_refs):
    f = pl.program_id(1)

    @pl.when(f == 0)
    def _():
        y_ref[...] = x1_ref[...]

    wup = wup_ref[...].astype(BF16)
    wdown = wdown_ref[...].astype(BF16)
    hid = jnp.maximum(_dot(hm_ref[...], wup), 0.0)
    y_ref[...] += _dot((hid * hid).astype(BF16), wdown)
    if cast_refs:
        cast_refs[0][...] = wup
        cast_refs[1][...] = wdown


def _ffn(hm, x1, wup, wdown):
    n, d = hm.shape
    dff = wup.shape[1]
    tm = 512 if n % 512 == 0 else n
    emit = wup.dtype != BF16
    assert not emit or n == tm
    tf = 512 if emit else 1024
    row = pl.BlockSpec((tm, d), lambda i, f: (i, 0))
    up_spec = pl.BlockSpec((d, tf), lambda i, f: (0, f))
    down_spec = pl.BlockSpec((tf, d), lambda i, f: (f, 0))
    out_specs = [row]
    out_shape = [jax.ShapeDtypeStruct((n, d), F32)]
    if emit:
        out_specs += [up_spec, down_spec]
        out_shape += [jax.ShapeDtypeStruct(wup.shape, BF16), jax.ShapeDtypeStruct(wdown.shape, BF16)]
    return pl.pallas_call(
        _ffn_kernel,
        grid=(n // tm, dff // tf),
        in_specs=[row, row, up_spec, down_spec],
        out_specs=out_specs,
        out_shape=out_shape,
        compiler_params=_cparams(("arbitrary", "arbitrary")),
        name="ffn",
    )(hm, x1, wup, wdown)


def _bucket(rel, valid):
    n = np.maximum(rel, 0)
    max_exact = N_BUCKETS // 2
    nf = np.maximum(n, 1).astype(np.float32)
    large = max_exact + (np.log(nf / np.float32(max_exact)) / np.float32(math.log(MAX_DIST / max_exact))
                         * np.float32(N_BUCKETS - max_exact)).astype(np.int32)
    large = np.minimum(large, N_BUCKETS - 1)
    return np.where(valid, np.where(n < max_exact, n, large), -1).astype(np.int32)


def _bias_tables_kernel(rb_ref, *refs, shifts):
    n = len(shifts)
    nbk, nh = rb_ref.shape
    for b_ref, o_ref, shift in zip(refs[:n], refs[n:], shifts):
        b = b_ref[...]
        rows = []
        for h in range(nh):
            sh = rb_ref[nbk - 1, h] if shift else 0.0
            acc = jnp.full(b.shape, NEG, F32)
            for k in range(nbk):
                acc = jnp.where(b == k, rb_ref[k, h] - sh, acc)
            if len(o_ref.shape) == 3:
                o_ref[h] = acc
            else:
                rows.append(acc)
        if rows:
            o_ref[...] = jnp.concatenate(rows, axis=0)


def _bias_tables(rel_bias, buckets, shifts):
    nh = rel_bias.shape[1]
    shapes = [jax.ShapeDtypeStruct((nh,) + (b.shape if b.shape[0] > 1 else b.shape[1:]), F32) for b in buckets]
    vm = pl.BlockSpec(memory_space=pltpu.VMEM)
    return pl.pallas_call(
        functools.partial(_bias_tables_kernel, shifts=tuple(shifts)),
        in_specs=[pl.BlockSpec(memory_space=pltpu.SMEM)] + [vm] * len(buckets),
        out_specs=[vm] * len(buckets),
        out_shape=shapes,
        name="bias_tables",
    )(rel_bias, *[jnp.asarray(b) for b in buckets])


def _overlap(nc, ns):
    i = np.arange(nc)[:, None] * CMP_STRIDE
    j = np.arange(ns)[None, :] * SEL_LEN
    return ((i < j + SEL_LEN) & (i + CMP_LEN > j)).astype(np.float32)


def _compress_weights(w, pos):
    n2 = CMP_STRIDE // 2
    w4 = w.reshape(2, n2, 2, HEAD_DIM, HEAD_DIM)
    top = jnp.concatenate([w4[0, :, 0], w4[1, :, 0]], axis=2)
    bot = jnp.concatenate([w4[0, :, 1], w4[1, :, 1]], axis=2)
    tiles = jnp.concatenate([top, bot], axis=1).astype(BF16)
    p4 = pos.reshape(2, n2, 2, HEAD_DIM)
    row_a = jnp.concatenate([p4[0, :, 0], p4[0, :, 1]], axis=1)
    row_b = jnp.concatenate([p4[1, :, 0], p4[1, :, 1]], axis=1)
    rows = jnp.concatenate([row_a[:, None], row_b[:, None], jnp.zeros((n2, 14, 2 * HEAD_DIM), F32)], axis=1)
    return tiles, rows.astype(BF16)


def _dense_tail(x, h, oa, ob, wga, wgb, wg, wn, wout, gmlp, wup, wdown, gate_rows=None):
    mix, *cast_m = _merge(h, oa, ob, wga, wgb, wg, wn, gate_rows)
    x1, hm, *cast_o = _outproj(x, mix, wout, gmlp)
    y, *cast_f = _ffn(hm, x1, wup, wdown)
    return y, (cast_m or [wga, wgb, wg, wn]) + (cast_o or [wout]) + (cast_f or [wup, wdown])


def kernel(x_prompt, x_sample, cache_cmp_kv, cache_sel_kv, state_win_kv, page_table, rel_bias, g_mix_norm, w_in,
           g_sgu, w_sgu, b_sgu, g_q, g_k, pos_cmp_k, w_cmp_k, pos_cmp_v, w_cmp_v, w_proj_gmlp, w_proj_nsa, w_out,
           g_mlp_norm, w_up, w_down):
    depth = g_mix_norm.shape[0]
    assert depth == 1
    l = 0
    bsz, seq, d = x_prompt.shape
    nb = x_sample.shape[0]
    assert x_sample.shape[1] == 1 and seq % KPAD == 0
    d_gm = g_sgu.shape[1]
    n_heads = rel_bias.shape[1]
    d_nsa = n_heads * HEAD_DIM
    kvw_cols = 2 * (n_heads // 4) * HEAD_DIM
    n_gate = 3 * n_heads
    c_q = 2 * d_gm
    c_kv = c_q + d_nsa
    c_gate = c_kv + 3 * kvw_cols
    c_ga = c_gate + n_gate
    c_gb = c_ga + d

    w = jnp.swapaxes(w_in[l], 0, 1)
    n_col_tiles = -(-c_ga // 1024)
    gmix = g_mix_norm[l][None]
    gsgu = g_sgu[l][None]
    gq = g_q[l][None]
    gk = g_k[l]
    gmlp = g_mlp_norm[l][None]
    wk_pairs, posk_rows = _compress_weights(w_cmp_k[l], pos_cmp_k[l])
    wv_pairs, posv_rows = _compress_weights(w_cmp_v[l], pos_cmp_v[l])
    pos_rows = jnp.stack([posk_rows, posv_rows])
    n_groups = w_sgu.shape[1]
    b_exp = jnp.repeat(b_sgu[l].T, d_gm // n_groups, axis=1)
    w00 = jnp.repeat(w_sgu[l][:, 0, 0], d_gm // n_groups)[None]
    b00 = jnp.repeat(b_sgu[l][:, 0], d_gm // n_groups)[None]

    n_pool, page = cache_cmp_kv.shape[1], cache_cmp_kv.shape[2]
    n_pages = page_table.shape[1]
    past = n_pages * page
    nwin = state_win_kv.shape[2]
    n_cmp_s = (past + 1 - CMP_LEN) // CMP_STRIDE + 1
    n_sel_s = -(-(past + 1) // SEL_LEN)
    nseg_s = past // CMP_STRIDE
    qi = np.arange(QBLOCK)[:, None]
    rel_s = qi + QBLOCK - np.arange(2 * QBLOCK)[None, :]
    rel_w = qi + WINDOW - np.arange(WINDOW + QBLOCK)[None, :]
    rel_c = qi + (KC_FRONT * CMP_STRIDE - CMP_LEN + 1) - CMP_STRIDE * np.arange(LANES)[None, :]
    srel_c = (past - (np.arange(nseg_s) * CMP_STRIDE + CMP_LEN - 1))[None]
    srel_s = (past - np.arange(past))[None]
    srel_w = (nwin - np.arange(nwin))[None]
    buckets = [
        _bucket(rel_c, rel_c >= 0), _bucket(rel_s, rel_s >= 0), _bucket(rel_w, (rel_w >= 0) & (rel_w < WINDOW)),
        _bucket(srel_c, (srel_c >= 0) & (np.arange(nseg_s)[None] < n_cmp_s)), _bucket(srel_s, srel_s >= 0),
        _bucket(srel_w, srel_w < WINDOW), np.zeros((1, LANES), np.int32)]
    tab_c, tab_s, tab_w, bias_c, bias_s, bias_w, bias_new = _bias_tables(
        rel_bias, buckets, [True, True, False, False, False, False, False])
    tab_c, tab_s, tab_w = [t.reshape(2, (n_heads // 2) * QBLOCK, t.shape[-1]) for t in (tab_c, tab_s, tab_w)]

    xs = x_sample.reshape(nb, d)
    h_s, v_s, oa_s, q_s, kvc_s, kvs_s, kvw_s, gates_s, w_cat = _inproj(
        xs, gmix, w, gsgu, gq, w00, b00, chunked=False, n_col_tiles=n_col_tiles)
    ov_s = np.zeros((nseg_s, LANES), np.float32)
    ov_s[:n_cmp_s, :n_sel_s] = _overlap(n_cmp_s, n_sel_s)
    expand = jnp.asarray(np.arange(LANES)[:, None] == (np.arange(past)[None, :] // SEL_LEN), BF16)
    gates_h = jnp.pad(gates_s[:, :n_gate].reshape(nb, n_heads, 3), ((0, 0), (0, 0), (0, LANES - 3)))
    lin = lambda a: a[l].reshape(a.shape[1], -1, HEAD_DIM)
    win_lin = lin(state_win_kv)
    ob_s, win_new = _nsa_sample(
        page_table, lin(cache_cmp_kv), lin(cache_sel_kv), win_lin,
        q_s.astype(F32).reshape(nb, n_heads, HEAD_DIM), gates_h,
        kvs_s.reshape(nb, 4, HEAD_DIM), kvw_s.reshape(nb, 4, HEAD_DIM), wk_pairs, wv_pairs, pos_rows, gk,
        bias_c, bias_s, bias_w, bias_new, jnp.asarray(ov_s), expand)
    y_s, (wga, wgb, wg, wn, wout, wup, wdown) = _dense_tail(
        xs, h_s, oa_s, ob_s.reshape(nb, d_nsa).astype(BF16), w, w, w_proj_gmlp[l], w_proj_nsa[l], w_out[l], gmlp,
        w_up[l], w_down[l], gate_rows=(c_ga, c_gb))

    xp = x_prompt.reshape(bsz * seq, d)
    h_p, v_p, oa_p, q_p, kvc_p, kvs_p, kvw_p, gates_p = _inproj(
        xp, gmix, w_cat, gsgu, gq, w_sgu[l], b_exp, chunked=True, n_col_tiles=n_col_tiles)
    kvs3 = kvs_p.reshape(bsz, 4 * seq, HEAD_DIM)
    kvw3 = kvw_p.reshape(bsz, 4 * seq, HEAD_DIM)
    kvc3 = kvc_p.reshape(bsz, 4 * seq, HEAD_DIM)
    ks, vs, kw, vw = _kvprep(kvs3, kvw3, gk)
    kc, vc = _compress_prompt(kvc3, wk_pairs, wv_pairs, pos_rows, gk)
    n_cmp = (seq - CMP_LEN) // CMP_STRIDE + 1
    n_selb = seq // SEL_LEN
    ov_p = np.zeros((KC_ROWS, LANES), np.float32)
    ov_p[KC_FRONT:KC_FRONT + n_cmp, :n_selb] = _overlap(n_cmp, n_selb)
    ov_p = jnp.asarray(ov_p)
    ob_p = _nsa_prompt(q_p.reshape(bsz, seq, d_nsa), gates_p.reshape(bsz, seq, 2 * LANES), kc, vc, ov_p, ks, vs, kw, vw,
                       tab_c, tab_s, tab_w)
    y_p, _ = _dense_tail(xp, h_p, oa_p, ob_p.reshape(bsz * seq, d_nsa), wga, wgb, wg, wn, wout, gmlp, wup, wdown)

    n_kv = n_heads // 4
    kv6 = lambda a, b_, t_: a.reshape(1, b_, t_, 2, n_kv, HEAD_DIM)
    nw_p = min(WINDOW, seq)
    last = ((seq - 1) // CHUNK) * CHUNK
    nw_s = min(WINDOW, past + 1)
    return (y_p.reshape(bsz, seq, d), y_s.reshape(nb, 1, d),
            kv6(kvc3, bsz, seq), kv6(kvs3, bsz, seq), kv6(kvw3[:, 4 * (seq - nw_p):], bsz, nw_p),
            v_p.reshape(bsz, seq, d_gm)[:, last:][None],
            kv6(kvc_s, nb, 1), kv6(kvs_s, nb, 1), kv6(win_new, nb, nw_s),
            v_s.reshape(1, nb, 1, d_gm))
```

```python
import functools
import math

import numpy as np
import jax
import jax.numpy as jnp
from jax import lax
from jax.experimental import pallas as pl
from jax.experimental.pallas import tpu as pltpu

F32 = jnp.float32
BF16 = jnp.bfloat16

HEAD_DIM = 128
CHUNK = 128
CMP_LEN = 32
CMP_STRIDE = 16
SEL_LEN = 64
N_SEL = 16
WINDOW = 512
N_BUCKETS = 32
MAX_DIST = 128
QBLOCK = 128
EPS = 1e-6
NEG = -1e30
HALF_NEG = -5e29
FORCE = 1e6

LANES = 128
PAD_LANE = 64
KPAD = 512
KC_FRONT = 16
KC_ROWS = 376
SAMPLE_GROUP = 2
VMEM_LIMIT = 56 * 1024 * 1024


def _cparams(sem):
    return pltpu.CompilerParams(dimension_semantics=sem, vmem_limit_bytes=VMEM_LIMIT)


def _dot(a, b):
    return jnp.dot(a, b, preferred_element_type=F32)


def _dot_nt(a, b):
    return lax.dot_general(a, b, (((1,), (1,)), ((), ())), preferred_element_type=F32)


def _dot_hilo(a, b_bf16):
    hi = a.astype(BF16)
    lo = (a - hi.astype(F32)).astype(BF16)
    return _dot(hi, b_bf16) + _dot(lo, b_bf16)


def _rms(x):
    return x * lax.rsqrt(jnp.mean(x * x, axis=-1, keepdims=True) + EPS)


def _gelu(x):
    c = math.sqrt(2.0 / math.pi)
    return 0.5 * x * (1.0 + jnp.tanh(c * (x + 0.044715 * (x * x * x))))


def _sigmoid(x):
    return 1.0 / (1.0 + jnp.exp(-x))


def _inproj_kernel(x_ref, gmix_ref, w_ref, gsgu_ref, gq_ref, wsg_ref, bsg_ref,
                   h_ref, v_ref, oa_ref, q_ref, kvc_ref, kvs_ref, kvw_ref, gate_ref, *rest, chunked, tm):
    h_s, u_s = rest[-2:]
    j = pl.program_id(1)

    half = w_ref.shape[0] // 2

    def z_cols(lo, hi):
        if len(rest) == 3:
            w = w_ref[lo:hi, :].astype(BF16)
            rest[0][lo:hi, :] = w
            return _dot_nt(h_s[...], w)
        return _dot_nt(h_s[...], w_ref[lo:hi, :])

    @pl.when(j == 0)
    def _():
        hb = (_rms(x_ref[...]) * gmix_ref[...]).astype(BF16)
        h_s[...] = hb
        h_ref[...] = hb
        za = z_cols(0, half)
        zb = z_cols(half, 2 * half)
        u_s[:, :half] = _gelu(za)
        u_s[:, half:] = _gelu(zb)

    @pl.when(j == 1)
    def _():
        za = z_cols(0, half)
        zb = z_cols(half, 2 * half)
        ga = _gelu(za)
        gb = _gelu(zb)
        ms = (jnp.sum(ga * ga, axis=-1, keepdims=True) + jnp.sum(gb * gb, axis=-1, keepdims=True)) / (2 * half)
        rstd = lax.rsqrt(ms + EPS)
        for c0, gz in ((0, ga), (half, gb)):
            v = gz * rstd * gsgu_ref[:, c0:c0 + half]
            v_ref[:, c0:c0 + half] = v
            if chunked:
                row = lax.broadcasted_iota(jnp.int32, (CHUNK, CHUNK), 0)
                col = lax.broadcasted_iota(jnp.int32, (CHUNK, CHUNK), 1)
                for g in range(half // LANES):
                    wm = jnp.where(row >= col, wsg_ref[c0 // LANES + g], 0.0).astype(BF16)
                    cs = slice(g * LANES, (g + 1) * LANES)
                    os_ = slice(c0 + g * LANES, c0 + (g + 1) * LANES)
                    for c in range(tm // CHUNK):
                        rs = slice(c * CHUNK, (c + 1) * CHUNK)
                        s = _dot(wm, v[rs, cs].astype(BF16)) + bsg_ref[:, os_]
                        oa_ref[rs, os_] = (u_s[rs, os_] * s).astype(BF16)
            else:
                cs = slice(c0, c0 + half)
                oa_ref[:, cs] = (u_s[:, cs] * (v * wsg_ref[:, cs] + bsg_ref[:, cs])).astype(BF16)

    @pl.when(j == 2)
    def _():
        scale = HEAD_DIM ** -0.5
        for c0 in (0, half):
            z = z_cols(c0, c0 + half)
            for hd in range(half // HEAD_DIM):
                cs = slice(hd * HEAD_DIM, (hd + 1) * HEAD_DIM)
                q_ref[:, c0 + hd * HEAD_DIM:c0 + (hd + 1) * HEAD_DIM] = (
                    _rms(z[:, cs]) * gq_ref[...] * scale).astype(BF16)

    def store_kv(ref, zz):
        for c in range(4):
            ref[pl.ds(c, tm, stride=4), :] = zz[:, c * LANES:(c + 1) * LANES]

    @pl.when(j == 3)
    def _():
        store_kv(kvc_ref, z_cols(0, half))
        store_kv(kvs_ref, z_cols(half, 2 * half))

    @pl.when(j == 4)
    def _():
        store_kv(kvw_ref, z_cols(0, half))
        gates = _sigmoid(z_cols(half, half + LANES))
        gate_ref[:, :LANES] = gates
        gate_ref[:, LANES:] = pltpu.roll(gates, LANES - 12, 1)
        if len(rest) == 3:
            rest[0][half + LANES:, :] = jnp.zeros((half - LANES, w_ref.shape[1]), BF16)


def _inproj(x, gmix, w_cat, gsgu, gq, wsg, bsg, *, chunked, n_col_tiles):
    n, d = x.shape
    tm = 512 if n % 512 == 0 else n
    tn = 1024
    emit = w_cat.dtype != BF16
    assert not emit or n == tm
    kern = functools.partial(_inproj_kernel, chunked=chunked, tm=tm)
    full = lambda a: pl.BlockSpec(a.shape, lambda i, j: (0,) * a.ndim)
    row = lambda w: pl.BlockSpec((tm, w), lambda i, j: (i, 0))
    kvrow = pl.BlockSpec((4 * tm, LANES), lambda i, j: (i, 0))
    out_shapes = (
        jax.ShapeDtypeStruct((n, d), BF16),
        jax.ShapeDtypeStruct((n, 1024), F32),
        jax.ShapeDtypeStruct((n, 1024), BF16),
        jax.ShapeDtypeStruct((n, 1024), BF16),
        jax.ShapeDtypeStruct((4 * n, LANES), F32),
        jax.ShapeDtypeStruct((4 * n, LANES), F32),
        jax.ShapeDtypeStruct((4 * n, LANES), F32),
        jax.ShapeDtypeStruct((n, 2 * LANES), F32),
    )
    wspec = pl.BlockSpec((tn, d), lambda i, j: (j, 0))
    out_specs = [row(d), row(1024), row(1024), row(1024), kvrow, kvrow, kvrow, row(2 * LANES)]
    if emit:
        out_shapes += (jax.ShapeDtypeStruct((n_col_tiles * tn, d), BF16),)
        out_specs.append(wspec)
    return pl.pallas_call(
        kern,
        grid=(n // tm, n_col_tiles),
        in_specs=[row(d), full(gmix), wspec, full(gsgu), full(gq), full(wsg), full(bsg)],
        out_specs=out_specs,
        out_shape=out_shapes,
        scratch_shapes=[pltpu.VMEM((tm, d), BF16), pltpu.VMEM((tm, 1024), F32)],
        compiler_params=_cparams(("arbitrary", "arbitrary")),
        name="inproj",
    )(x, gmix, w_cat, gsgu, gq, wsg, bsg)


def _kvprep_kernel(kvs_ref, kvw_ref, gk_ref, ks_ref, vs_ref, kw_ref, vw_ref):
    i = pl.program_id(1)
    rows = kvs_ref.shape[0] // 4
    col = lambda ref, c: ref[pl.ds(c, rows, stride=4), :]
    lane = lax.broadcasted_iota(jnp.int32, (rows, LANES), 1)
    row = lax.broadcasted_iota(jnp.int32, (rows, LANES), 0)

    @pl.when(i == 0)
    def _():
        aux = jnp.where(lane == PAD_LANE, 1.0, 0.0).astype(BF16)
        zk = jnp.zeros((rows, LANES), BF16)
        for g in range(2):
            ks_ref[g] = jnp.concatenate([zk, aux], axis=1)
            kw_ref[g] = jnp.concatenate([zk, aux], axis=1)
            vs_ref[g] = zk
            vw_ref[g] = zk

    @pl.when(i > 0)
    def _():
        blk = ((i - 1) * rows + row) // SEL_LEN
        onehot = jnp.where(lane == blk, 1.0, 0.0).astype(BF16)
        zaux = jnp.zeros((rows, LANES), BF16)
        for g in range(2):
            ks = (_rms(col(kvs_ref, g)) * gk_ref[1:2, :]).astype(BF16)
            kw = (_rms(col(kvw_ref, g)) * gk_ref[2:3, :]).astype(BF16)
            ks_ref[g] = jnp.concatenate([ks, onehot], axis=1)
            kw_ref[g] = jnp.concatenate([kw, zaux], axis=1)
            vs_ref[g] = col(kvs_ref, 2 + g).astype(BF16)
            vw_ref[g] = col(kvw_ref, 2 + g).astype(BF16)


def _kvprep(kvs, kvw, gk):
    b, t4, _ = kvs.shape
    t = t4 // 4
    rows = KPAD
    nblk = t // rows
    in_map = lambda bi, i: (bi, jnp.maximum(i - 1, 0), 0)
    out_map = lambda bi, i: (bi, 0, i, 0)
    kshape = jax.ShapeDtypeStruct((b, 2, KPAD + t, 2 * LANES), BF16)
    vshape = jax.ShapeDtypeStruct((b, 2, KPAD + t, LANES), BF16)
    return pl.pallas_call(
        _kvprep_kernel,
        grid=(b, nblk + 1),
        in_specs=[pl.BlockSpec((None, 4 * rows, LANES), in_map), pl.BlockSpec((None, 4 * rows, LANES), in_map),
                  pl.BlockSpec(gk.shape, lambda bi, i: (0, 0))],
        out_specs=[pl.BlockSpec((None, 2, rows, 2 * LANES), out_map), pl.BlockSpec((None, 2, rows, LANES), out_map),
                   pl.BlockSpec((None, 2, rows, 2 * LANES), out_map), pl.BlockSpec((None, 2, rows, LANES), out_map)],
        out_shape=(kshape, vshape, kshape, vshape),
        compiler_params=_cparams(("arbitrary", "arbitrary")),
        name="kvprep",
    )(kvs, kvw, gk)


def _compress_core(load_pair, nseg, wk_ref, wv_ref, pos_ref):
    w_refs = (wk_ref, wv_ref)
    accs = [jnp.zeros((2 * nseg + 16, 2 * LANES), F32) for _ in range(2)]
    for s2 in range(CMP_STRIDE // 2):
        for kv in range(2):
            parts = [load_pair(s2, 2 * kv + g) for g in range(2)]
            parts.append(pos_ref[kv, s2])
            lhs = jnp.concatenate(parts, axis=0)
            accs[kv] = accs[kv] + _dot(lhs, w_refs[kv][s2])
    outs = []
    for kv in range(2):
        y = accs[kv]
        post = y[2 * nseg:2 * nseg + 1, :LANES] + y[2 * nseg + 1:2 * nseg + 2, LANES:]
        per_g = []
        for g in range(2):
            y0 = y[g * nseg:(g + 1) * nseg, :LANES]
            y1 = y[g * nseg:(g + 1) * nseg, LANES:]
            per_g.append(y0 + pltpu.roll(y1, nseg - 1, 0) + post)
        outs.append(per_g)
    return outs


def _compress_prompt_kernel(x_ref, wk_ref, wv_ref, pos_ref, gk_ref, kc_ref, vc_ref):
    nseg = x_ref.shape[0] // (4 * CMP_STRIDE)
    load_x = lambda s, col: x_ref[pl.ds(4 * s + col, nseg, stride=4 * CMP_STRIDE), :]
    load_pair = lambda s2, col: jnp.concatenate([load_x(2 * s2, col), load_x(2 * s2 + 1, col)], axis=1).astype(BF16)
    ck, cv = _compress_core(load_pair, nseg, wk_ref, wv_ref, pos_ref)
    row = lax.broadcasted_iota(jnp.int32, (nseg, LANES), 0)
    lane = lax.broadcasted_iota(jnp.int32, (nseg, LANES), 1)
    valid = row < nseg - 1
    aux = jnp.where(valid, jnp.where(lane == row // 8, 1.0, 0.0), jnp.where(lane == PAD_LANE, 1.0, 0.0))
    back = KC_ROWS - KC_FRONT - nseg
    lane_f = lax.broadcasted_iota(jnp.int32, (KC_FRONT, LANES), 1)
    lane_b = lax.broadcasted_iota(jnp.int32, (back, LANES), 1)
    pad_f = jnp.concatenate([jnp.zeros((KC_FRONT, LANES), F32), jnp.where(lane_f == PAD_LANE, 1.0, 0.0)], axis=1)
    pad_b = jnp.concatenate([jnp.zeros((back, LANES), F32), jnp.where(lane_b == PAD_LANE, 1.0, 0.0)], axis=1)
    for g in range(2):
        kn = jnp.where(valid, _rms(ck[g]) * gk_ref[0:1, :], 0.0)
        kc_ref[g, 0:KC_FRONT, :] = pad_f
        kc_ref[g, KC_FRONT:KC_FRONT + nseg, :] = jnp.concatenate([kn, aux], axis=1)
        kc_ref[g, KC_FRONT + nseg:KC_ROWS, :] = pad_b
        vc_ref[g, 0:KC_FRONT, :] = jnp.zeros((KC_FRONT, LANES), F32)
        vc_ref[g, KC_FRONT:KC_FRONT + nseg, :] = jnp.where(valid, cv[g], 0.0)
        vc_ref[g, KC_FRONT + nseg:KC_ROWS, :] = jnp.zeros((back, LANES), F32)


def _compress_prompt(kvc, wk_pairs, wv_pairs, pos_rows, gk):
    b, t4, _ = kvc.shape
    full = lambda a: pl.BlockSpec(a.shape, lambda bi: (0,) * a.ndim)
    return pl.pallas_call(
        _compress_prompt_kernel,
        grid=(b,),
        in_specs=[pl.BlockSpec((None, t4, LANES), lambda bi: (bi, 0, 0)),
                  full(wk_pairs), full(wv_pairs), full(pos_rows), full(gk)],
        out_specs=[pl.BlockSpec((None, 2, KC_ROWS, 2 * LANES), lambda bi: (bi, 0, 0, 0)),
                   pl.BlockSpec((None, 2, KC_ROWS, LANES), lambda bi: (bi, 0, 0, 0))],
        out_shape=(jax.ShapeDtypeStruct((b, 2, KC_ROWS, 2 * LANES), F32),
                   jax.ShapeDtypeStruct((b, 2, KC_ROWS, LANES), F32)),
        compiler_params=_cparams(("arbitrary",)),
        name="compress_prompt",
    )(kvc, wk_pairs, wv_pairs, pos_rows, gk)


def _nsa_prompt_kernel(q_ref, gate_ref, kc_ref, vc_ref, ov_ref, ks_ref, vs_ref, kw_ref, vw_ref,
                       tc_ref, ts_ref, tw_ref, o_ref, m_s, l_s, acc_s, sa_s, sb_s, sw_s):
    qb = pl.program_id(2)
    t0 = qb * QBLOCK
    rq = 4 * QBLOCK
    q = q_ref[...]
    q4 = jnp.concatenate([q[:, r * LANES:(r + 1) * LANES] for r in range(4)], axis=0)
    lane = lax.broadcasted_iota(jnp.int32, (rq, LANES), 1)
    is_pad_lane = lane == PAD_LANE

    def q_aug(mb):
        return jnp.concatenate([q4, mb.astype(BF16)], axis=1)

    qa_pad = q_aug(jnp.where(is_pad_lane, NEG, 0.0))

    far_mask = ((lane < 32) & (lane >= qb - 2)) | is_pad_lane
    qa_cfar = q_aug(jnp.where(far_mask, NEG, 0.0))
    ncmp = 256
    near0 = pl.multiple_of(qb * 8, 8)
    k_far = kc_ref[KC_FRONT:KC_FRONT + ncmp, :].astype(BF16)
    k_near = kc_ref[pl.ds(near0, LANES), :].astype(BF16)
    s_far = _dot_nt(qa_cfar, k_far)
    s_near = _dot_nt(qa_pad, k_near) + tc_ref[...]
    w0 = pl.multiple_of(t0, QBLOCK)
    wlen = WINDOW + QBLOCK
    sw_s[...] = _dot_nt(qa_pad, kw_ref[pl.ds(w0, wlen), :]) + tw_ref[...]
    m = jnp.maximum(jnp.max(s_far, axis=-1, keepdims=True), jnp.max(s_near, axis=-1, keepdims=True))
    m = jnp.maximum(m, HALF_NEG)
    p_far = jnp.exp(s_far - m)
    p_near = jnp.exp(s_near - m)
    l = jnp.sum(p_far, axis=-1, keepdims=True) + jnp.sum(p_near, axis=-1, keepdims=True)
    linv = 1.0 / jnp.where(l > 0.0, l, 1.0)
    v_far = vc_ref[KC_FRONT:KC_FRONT + ncmp, :].astype(BF16)
    v_near = vc_ref[pl.ds(near0, LANES), :].astype(BF16)
    pb_far = p_far.astype(BF16)
    pb_near = p_near.astype(BF16)
    o_c = (_dot(pb_far, v_far) + _dot(pb_near, v_near)) * linv

    s = sw_s[...]
    m = jnp.max(s, axis=-1, keepdims=True)
    p = jnp.exp(s - m)
    l = jnp.sum(p, axis=-1, keepdims=True)
    o_w = _dot(p.astype(BF16), vw_ref[pl.ds(w0, wlen), :]) * (1.0 / l)

    ov_far = ov_ref[KC_FRONT:KC_FRONT + ncmp, :].astype(BF16)
    ov_near = ov_ref[pl.ds(near0, LANES), :].astype(BF16)
    imp4 = (_dot(pb_far, ov_far) + _dot(pb_near, ov_near)) * linv
    imp = sum(imp4[r * QBLOCK:(r + 1) * QBLOCK] for r in range(4))
    nblk = 64
    imp_t = imp.T[:nblk]
    blk = lax.broadcasted_iota(jnp.int32, (nblk, QBLOCK), 0)
    qpos = t0 + lax.broadcasted_iota(jnp.int32, (nblk, QBLOCK), 1)
    cur = qpos // SEL_LEN
    forced = (blk == 0) | (blk == cur) | (blk == cur - 1)
    eligible = blk * SEL_LEN <= qpos
    val = jnp.where(forced, FORCE, jnp.where(eligible, imp_t, -1.0))
    slab = 8
    vals = [val[v * slab:(v + 1) * slab] for v in range(nblk // slab)]
    ranks = [jnp.zeros((slab, QBLOCK), F32) for _ in vals]
    row_in_slab = lax.broadcasted_iota(jnp.int32, (slab, QBLOCK), 0)
    for i in range(nblk):
        vi, ii = divmod(i, slab)
        ri = jnp.broadcast_to(vals[vi][ii:ii + 1, :], (slab, QBLOCK))
        for v in range(len(vals)):
            if v < vi:
                beats = ri > vals[v]
            elif v > vi:
                beats = ri >= vals[v]
            else:
                beats = (ri > vals[v]) | ((ri == vals[v]) & (row_in_slab > ii))
            ranks[v] = ranks[v] + jnp.where(beats, 1.0, 0.0)
    rank = jnp.concatenate(ranks, axis=0)
    mb_t = jnp.where(rank < float(N_SEL), 0.0, NEG)
    row2 = lax.broadcasted_iota(jnp.int32, (LANES - nblk, QBLOCK), 0)
    mb_t = jnp.concatenate([mb_t, jnp.where(row2 == PAD_LANE - nblk, NEG, 0.0)], axis=0)
    mb = mb_t.T
    mb4 = jnp.concatenate([mb] * 4, axis=0)
    qa_snear = q_aug(mb4)
    qa_sfar = q_aug(jnp.where((lane < nblk) & (lane >= 2 * qb - 2), NEG, mb4))

    far_tile = 512
    n_far = (jnp.maximum(qb - 1, 0) * QBLOCK + far_tile - 1) // far_tile

    def far_logits(j):
        r0 = pl.multiple_of(KPAD + j * far_tile, far_tile)
        return _dot_nt(qa_sfar, ks_ref[pl.ds(r0, far_tile), :])

    sn0 = pl.multiple_of(t0 + KPAD - QBLOCK, QBLOCK)
    s = _dot_nt(qa_snear, ks_ref[pl.ds(sn0, 2 * QBLOCK), :]) + ts_ref[...]
    sa_s[...] = far_logits(0)
    m = jnp.max(s, axis=-1, keepdims=True)
    p = jnp.exp(s - m)
    l = jnp.sum(p, axis=-1, keepdims=True)
    acc = _dot(p.astype(BF16), vs_ref[pl.ds(sn0, 2 * QBLOCK), :])

    m_s[...] = m
    l_s[...] = l
    acc_s[...] = acc
    last_tile = ks_ref.shape[0] // far_tile - 2

    def far_update(j, s_ref):
        r0 = pl.multiple_of(KPAD + j * far_tile, far_tile)
        s = s_ref[...]
        m_old = m_s[...]
        m_new = jnp.maximum(m_old, jnp.max(s, axis=-1, keepdims=True))
        alpha = jnp.exp(m_old - m_new)
        p = jnp.exp(s - m_new)
        l_s[...] = alpha * l_s[...] + jnp.sum(p, axis=-1, keepdims=True)
        acc_s[...] = alpha * acc_s[...] + _dot(p.astype(BF16), vs_ref[pl.ds(r0, far_tile), :])
        m_s[...] = m_new

    def far_body(i, carry):
        a = 2 * i
        sb_s[...] = far_logits(a + 1)
        far_update(a, sa_s)
        sa_s[...] = far_logits(jnp.minimum(a + 2, last_tile))
        far_update(a + 1, sb_s)
        return carry

    lax.fori_loop(0, n_far // 2, far_body, 0)

    @pl.when(n_far % 2 == 1)
    def _():
        far_update(n_far - 1, sa_s)

    o_s = acc_s[...] * (1.0 / l_s[...])

    gt = gate_ref[...]
    gate = lambda c: gt[:, c:c + 1]
    outs = []
    for r in range(4):
        rs = slice(r * QBLOCK, (r + 1) * QBLOCK)
        outs.append(gate(3 * r) * o_c[rs] + gate(3 * r + 1) * o_s[rs] + gate(3 * r + 2) * o_w[rs])
    o_ref[...] = jnp.concatenate(outs, axis=1).astype(BF16)


def _nsa_prompt(q, gates_g, kc, vc, ov, ks, vs, kw, vw, tab_c, tab_s, tab_w):
    b, t, _ = q.shape
    nqb = t // QBLOCK
    kvspec = lambda a: pl.BlockSpec((None, None) + a.shape[2:], lambda bi, g, i: (bi, g, 0, 0))
    tabspec = lambda a: pl.BlockSpec((None,) + a.shape[1:], lambda bi, g, i: (g, 0, 0))
    return pl.pallas_call(
        _nsa_prompt_kernel,
        grid=(b, 2, nqb),
        in_specs=[pl.BlockSpec((None, QBLOCK, 512), lambda bi, g, i: (bi, i, g)),
                  pl.BlockSpec((None, QBLOCK, LANES), lambda bi, g, i: (bi, i, g)),
                  kvspec(kc), kvspec(vc), pl.BlockSpec(ov.shape, lambda bi, g, i: (0, 0)),
                  kvspec(ks), kvspec(vs), kvspec(kw), kvspec(vw),
                  tabspec(tab_c), tabspec(tab_s), tabspec(tab_w)],
        out_specs=pl.BlockSpec((None, QBLOCK, 512), lambda bi, g, i: (bi, i, g)),
        out_shape=jax.ShapeDtypeStruct((b, t, 1024), BF16),
        scratch_shapes=[pltpu.VMEM((4 * QBLOCK, 1), F32), pltpu.VMEM((4 * QBLOCK, 1), F32),
                        pltpu.VMEM((4 * QBLOCK, LANES), F32),
                        pltpu.VMEM((4 * QBLOCK, 512), F32), pltpu.VMEM((4 * QBLOCK, 512), F32),
                        pltpu.VMEM((4 * QBLOCK, WINDOW + QBLOCK), F32)],
        compiler_params=_cparams(("arbitrary", "arbitrary", "arbitrary")),
        name="nsa_prompt",
    )(q, gates_g, kc, vc, ov, ks, vs, kw, vw, tab_c, tab_s, tab_w)


def _decode_logits(qf, g0rows, gk_row, k_tiles, bias):
    qg = qf * gk_row
    zero = jnp.zeros_like(qg)
    qbd = jnp.concatenate([jnp.where(g0rows, qg, zero), jnp.where(g0rows, zero, qg)], axis=1).astype(BF16)
    one = jnp.ones_like(qg)
    ones_bd = jnp.concatenate([jnp.where(g0rows, one, zero), jnp.where(g0rows, zero, one)], axis=1).astype(BF16)
    s_parts, q_parts = [], []
    for kt in k_tiles:
        k = kt()
        s_parts.append(_dot_nt(qbd, k.astype(BF16)))
        q_parts.append(_dot_nt(ones_bd, (k * k).astype(BF16)))
    s = jnp.concatenate(s_parts, axis=1)
    ssq = jnp.concatenate(q_parts, axis=1)
    return s * lax.rsqrt(ssq * (1.0 / HEAD_DIM) + EPS) + bias, qg


def _decode_finish(s, qg, g0rows, v_tiles, key_mask, new_row, bias_new):
    if key_mask is not None:
        s = jnp.where(key_mask > 0.5, s, NEG)
    bc = lambda c: jnp.broadcast_to(new_row[c:c + 1, :], qg.shape)
    k_new = jnp.where(g0rows, bc(0), bc(1))
    v_new = jnp.where(g0rows, bc(2), bc(3))
    s_new = jnp.sum(qg * _rms(k_new), axis=-1, keepdims=True) + bias_new
    m = jnp.maximum(jnp.max(s, axis=-1, keepdims=True), s_new)
    p = jnp.exp(s - m)
    p_new = jnp.exp(s_new - m)
    l = jnp.sum(p, axis=-1, keepdims=True) + p_new
    o2 = jnp.zeros((qg.shape[0], 2 * LANES), F32)
    for i, vt in enumerate(v_tiles):
        o2 = o2 + _dot(p[:, i * LANES:(i + 1) * LANES].astype(BF16), vt().astype(BF16))
    o = jnp.where(g0rows, o2[:, :LANES], o2[:, LANES:]) + p_new * v_new
    return o * (1.0 / l)


def _nsa_sample_kernel(pt_ref, cache_c_hbm, cache_s_hbm, *refs, n_pages, n_sel_blocks, group):
    per_b_in, consts, outs = refs[:5], refs[5:-5], refs[-5:-3]
    buf_c, buf_s, sem = refs[-3:]
    step = pl.program_id(0)
    slot = step % 2

    def page_copies(src_step, dst_slot, for_wait):
        copies = []
        for e in range(group):
            for p in range(n_pages):
                pg = 0 if for_wait else pt_ref[src_step * group + e, p]
                copies.append(pltpu.make_async_copy(cache_c_hbm.at[pg], buf_c.at[dst_slot, e, p], sem.at[dst_slot, 0]))
                copies.append(pltpu.make_async_copy(cache_s_hbm.at[pg], buf_s.at[dst_slot, e, p], sem.at[dst_slot, 1]))
        return copies

    @pl.when(step == 0)
    def _():
        for i, c in enumerate(page_copies(0, 0, False)):
            c.start(priority=i % 2)

    @pl.when(step + 1 < pl.num_programs(0))
    def _():
        for i, c in enumerate(page_copies(step + 1, 1 - slot, False)):
            c.start(priority=i % 2)

    for c in page_copies(step, slot, True):
        c.wait()

    chains = [_nsa_sample_one([buf_c.at[slot, bb, p] for p in range(n_pages)],
                              [buf_s.at[slot, bb, p] for p in range(n_pages)],
                              *[r.at[bb] for r in per_b_in], *consts, *[r.at[bb] for r in outs],
                              n_pages=n_pages, n_sel_blocks=n_sel_blocks, chain_first=bb % 2 == 0)
              for bb in range(group)]
    while chains:
        chains = [c for c in chains if next(c, "done") != "done"]


def _nsa_sample_one(pages_c, pages_s, win_ref, q_ref, gate_ref, ksn_ref, kwn_ref, wk_ref, wv_ref, pos_ref, gk_ref,
                    bc_ref, bs_ref, bw_ref, bnew_ref, ov_ref, e_ref, perm_ref, o_ref, wout_ref, *,
                    n_pages, n_sel_blocks, chain_first):
    qf = q_ref[...]
    nh = qf.shape[0]
    g0rows = lax.broadcasted_iota(jnp.int32, (nh, LANES), 0) < nh // 2
    lane = lax.broadcasted_iota(jnp.int32, (nh, LANES), 1)

    page = pages_c[0].shape[0] // 4
    nseg = n_pages * page // CMP_STRIDE
    perm = perm_ref[...]
    regrouped = []
    for pp in range(n_pages // 2):
        per_cp = []
        for cp in range(2):
            blk = jnp.concatenate(
                [jnp.concatenate([pages_c[2 * pp + i][pl.ds(2 * cp + c, page, stride=4), :] for c in range(2)], axis=1)
                 for i in range(2)], axis=0).astype(BF16)
            per_cp.append(_dot(perm, blk).astype(BF16))
        regrouped.append(per_cp)
    rows_per_tap = 2 * page // CMP_STRIDE
    yield

    def tap_rows(s, col):
        lo, hi = s * rows_per_tap, (s + 1) * rows_per_tap
        return jnp.concatenate([regrouped[pp][col // 2][lo:hi, (col % 2) * LANES:(col % 2 + 1) * LANES]
                                for pp in range(n_pages // 2)], axis=0)

    load_pair = lambda s2, col: jnp.concatenate([tap_rows(2 * s2, col), tap_rows(2 * s2 + 1, col)], axis=1)
    ck, cv = _compress_core(load_pair, nseg, wk_ref, wv_ref, pos_ref)
    yield

    def selection_chain():
        kcn = jnp.concatenate([(_rms(ck[g]) * gk_ref[0:1, :]).astype(BF16) for g in range(2)], axis=0)
        s2 = _dot_nt(qf.astype(BF16), kcn)
        s = jnp.where(g0rows, s2[:, :nseg], s2[:, nseg:]) + bc_ref[...]
        m = jnp.max(s, axis=-1, keepdims=True)
        p = jnp.exp(s - m)
        linv = 1.0 / jnp.sum(p, axis=-1, keepdims=True)
        rowv = lax.broadcasted_iota(jnp.int32, (nseg, LANES), 0) < nseg - 1
        pb = p.astype(BF16)
        oc = [_dot(pb, jnp.where(rowv, cv[g], 0.0).astype(BF16)) for g in range(2)]
        o_c = jnp.where(g0rows, oc[0], oc[1]) * linv
        pn = p * linv
        s0 = jnp.sum(jnp.where(g0rows, pn, 0.0), axis=0, keepdims=True)
        s1 = jnp.sum(jnp.where(g0rows, 0.0, pn), axis=0, keepdims=True)
        psum = jnp.where(g0rows, jnp.broadcast_to(s0, pn.shape), jnp.broadcast_to(s1, pn.shape))
        imp = _dot_hilo(psum, ov_ref[...].astype(BF16))
        cur = n_sel_blocks - 1
        forced = (lane == 0) | (lane == cur) | (lane == cur - 1)
        val = jnp.where(lane >= n_sel_blocks, -2.0, jnp.where(forced, FORCE, imp))
        rank = jnp.zeros_like(val)
        for i in range(n_sel_blocks):
            ci = jnp.broadcast_to(val[:, i:i + 1], val.shape)
            beats = (ci > val) | ((ci == val) & (lane > i))
            rank = rank + jnp.where(beats, 1.0, 0.0)
        sel = jnp.where((rank < float(N_SEL)) & (lane < n_sel_blocks), 1.0, 0.0)
        return o_c, _dot(sel.astype(BF16), e_ref[...])

    def tiles(ref_list, col):
        def tile(rf, i):
            rows = lambda c: rf[pl.ds(4 * LANES * i + c, LANES, stride=4), :]
            return lambda: jnp.concatenate([rows(col), rows(col + 1)], axis=1)
        return [tile(rf, i) for rf in ref_list for i in range(rf.shape[0] // (4 * LANES))]

    b_new = bnew_ref[:, 0:1]

    def mask_free_work():
        s_sel, qg_sel = _decode_logits(qf, g0rows, gk_ref[1:2, :], tiles(pages_s, 0), bs_ref[...])
        s_win, qg_win = _decode_logits(qf, g0rows, gk_ref[2:3, :], tiles([win_ref], 0), bw_ref[...])
        o_w = _decode_finish(s_win, qg_win, g0rows, tiles([win_ref], 2), None, kwn_ref[...], b_new)
        return s_sel, qg_sel, o_w

    if chain_first:
        o_c, key_mask = selection_chain()
        yield
        s_sel, qg_sel, o_w = mask_free_work()
    else:
        s_sel, qg_sel, o_w = mask_free_work()
        yield
        o_c, key_mask = selection_chain()
    yield
    o_s = _decode_finish(s_sel, qg_sel, g0rows, tiles(pages_s, 2), key_mask, ksn_ref[...], b_new)
    yield
    gt = gate_ref[...]
    o_ref[...] = gt[:, 0:1] * o_c + gt[:, 1:2] * o_s + gt[:, 2:3] * o_w

    keep = wout_ref.shape[0] - 4
    drop = win_ref.shape[0] - keep
    wout_ref[0:keep, :] = win_ref[drop:drop + keep, :]
    wout_ref[keep:keep + 4, :] = kwn_ref[...]


def _nsa_sample(page_table, cache_c, cache_s, win, q, gates, ks_new, kw_new, wk_pairs, wv_pairs, pos_rows, gk,
                bias_c, bias_s, bias_w, bias_new, ov, expand):
    nb, n_pages = page_table.shape
    page_rows = cache_c.shape[1]
    n_sel_blocks = -(-(n_pages * (page_rows // 4) + 1) // SEL_LEN)
    group = SAMPLE_GROUP if nb % SAMPLE_GROUP == 0 else 1
    kern = functools.partial(_nsa_sample_kernel, n_pages=n_pages, n_sel_blocks=n_sel_blocks, group=group)
    win_rows_out = 4 * min(WINDOW, n_pages * (page_rows // 4) + 1)
    page = page_rows // 4
    assert n_pages % 2 == 0 and page % CMP_STRIDE == 0
    segs = page // CMP_STRIDE
    i_, n_, s_ = np.meshgrid(np.arange(2), np.arange(segs), np.arange(CMP_STRIDE), indexing="ij")
    perm_np = np.zeros((2 * page, 2 * page), np.float32)
    perm_np[(s_ * 2 * segs + i_ * segs + n_).ravel(), (i_ * page + CMP_STRIDE * n_ + s_).ravel()] = 1.0
    perm = jnp.asarray(perm_np, BF16)
    hbm = pl.BlockSpec(memory_space=pl.ANY)
    full = lambda a: pl.BlockSpec(a.shape, lambda bi, pt: (0,) * a.ndim)
    per_b = lambda a: pl.BlockSpec((group,) + a.shape[1:], lambda bi, pt: (bi,) + (0,) * (a.ndim - 1))
    page_buf = pltpu.VMEM((2, group, n_pages, page_rows, LANES), F32)
    grid_spec = pltpu.PrefetchScalarGridSpec(
        num_scalar_prefetch=1,
        grid=(nb // group,),
        scratch_shapes=[page_buf, page_buf, pltpu.SemaphoreType.DMA((2, 2))],
        in_specs=([hbm, hbm]
                  + [per_b(win), per_b(q), per_b(gates), per_b(ks_new), per_b(kw_new),
                     full(wk_pairs), full(wv_pairs), full(pos_rows), full(gk),
                     full(bias_c), full(bias_s), full(bias_w), full(bias_new), full(ov), full(expand), full(perm)]),
        out_specs=[pl.BlockSpec((group,) + q.shape[1:], lambda bi, pt: (bi, 0, 0)),
                   pl.BlockSpec((group, win_rows_out, LANES), lambda bi, pt: (bi, 0, 0))],
    )
    return pl.pallas_call(
        kern,
        grid_spec=grid_spec,
        out_shape=(jax.ShapeDtypeStruct(q.shape, F32), jax.ShapeDtypeStruct((nb, win_rows_out, LANES), F32)),
        compiler_params=_cparams(("arbitrary",)),
        name="nsa_sample",
    )(page_table, cache_c, cache_s, win, q, gates, ks_new, kw_new,
      wk_pairs, wv_pairs, pos_rows, gk, bias_c, bias_s, bias_w, bias_new, ov, expand, perm)


def _merge_kernel(h_ref, oa_ref, ob_ref, wga_ref, wgb_ref, wg_ref, wn_ref, mix_ref, *cast_refs):
    h = h_ref[...]
    wga = wga_ref[...].astype(BF16)
    wgb = wgb_ref[...].astype(BF16)
    wg = wg_ref[...].astype(BF16)
    wn = wn_ref[...].astype(BF16)
    ga = _sigmoid(_dot_nt(h, wga))
    gb = _sigmoid(_dot_nt(h, wgb))
    mix_ref[...] = (ga * _dot(oa_ref[...], wg) + gb * _dot(ob_ref[...], wn)).astype(BF16)
    for ref, val in zip(cast_refs, (wga, wgb, wg, wn)):
        ref[...] = val


def _merge(h, oa, ob, wga, wgb, wg, wn, gate_rows=None):
    n, d = h.shape
    tm = 512 if n % 512 == 0 else n
    tn = 512
    emit = gate_rows is not None
    assert not emit or n == tm
    row = lambda w: pl.BlockSpec((tm, w), lambda i, j: (i, 0))
    col = lambda k: pl.BlockSpec((k, tn), lambda i, j: (0, j))
    colt = pl.BlockSpec((tn, d), lambda i, j: (j, 0))
    out_specs = [pl.BlockSpec((tm, tn), lambda i, j: (i, j))]
    out_shape = [jax.ShapeDtypeStruct((n, d), BF16)]
    gate_specs = [colt, colt]
    if emit:
        assert all(r % 8 == 0 for r in gate_rows)
        gate_specs = [pl.BlockSpec((pl.Element(tn), pl.Element(d)),
                                   lambda i, j, r=r: ((r // 8 + j * (tn // 8)) * 8, 0)) for r in gate_rows]
        out_specs += [colt, colt, col(wg.shape[0]), col(wn.shape[0])]
        out_shape += [jax.ShapeDtypeStruct((d, d), BF16), jax.ShapeDtypeStruct((d, d), BF16),
                      jax.ShapeDtypeStruct(wg.shape, BF16), jax.ShapeDtypeStruct(wn.shape, BF16)]
    return pl.pallas_call(
        _merge_kernel,
        grid=(n // tm, d // tn),
        in_specs=[row(d), row(oa.shape[1]), row(ob.shape[1])] + gate_specs + [col(wg.shape[0]), col(wn.shape[0])],
        out_specs=out_specs,
        out_shape=out_shape,
        compiler_params=_cparams(("arbitrary", "arbitrary")),
        name="merge",
    )(h, oa, ob, wga, wgb, wg, wn)


def _outproj_kernel(x_ref, mix_ref, wout_ref, gmlp_ref, x1_ref, hm_ref, *cast_refs):
    if cast_refs:
        wout = wout_ref[...].astype(BF16)
        cast_refs[0][...] = wout
        x1 = x_ref[...] + _dot(mix_ref[...], wout)
    else:
        x1 = x_ref[...] + _dot(mix_ref[...], wout_ref[...])
    x1_ref[...] = x1
    hm_ref[...] = (_rms(x1) * gmlp_ref[...]).astype(BF16)


def _outproj(x, mix, wout, gmlp):
    n, d = x.shape
    tm = 512 if n % 512 == 0 else n
    emit = wout.dtype != BF16
    assert not emit or n == tm
    row = pl.BlockSpec((tm, d), lambda i: (i, 0))
    wspec = pl.BlockSpec(wout.shape, lambda i: (0, 0))
    out_specs = [row, row]
    out_shape = [jax.ShapeDtypeStruct((n, d), F32), jax.ShapeDtypeStruct((n, d), BF16)]
    if emit:
        out_specs.append(wspec)
        out_shape.append(jax.ShapeDtypeStruct(wout.shape, BF16))
    return pl.pallas_call(
        _outproj_kernel,
        grid=(n // tm,),
        in_specs=[row, row, wspec, pl.BlockSpec(gmlp.shape, lambda i: (0, 0))],
        out_specs=out_specs,
        out_shape=out_shape,
        compiler_params=_cparams(("arbitrary",)),
        name="outproj",
    )(x, mix, wout, gmlp)


def _ffn_kernel(hm_ref, x1_ref, wup_ref, wdown_ref, y_ref, *cast_refs):
    f = pl.program_id(1)

    @pl.when(f == 0)
    def _():
        y_ref[...] = x1_ref[...]

    wup = wup_ref[...].astype(BF16)
    wdown = wdown_ref[...].astype(BF16)
    hid = jnp.maximum(_dot(hm_ref[...], wup), 0.0)
    y_ref[...] += _dot((hid * hid).astype(BF16), wdown)
    if cast_refs:
        cast_refs[0][...] = wup
        cast_refs[1][...] = wdown


def _ffn(hm, x1, wup, wdown):
    n, d = hm.shape
    dff = wup.shape[1]
    tm = 512 if n % 512 == 0 else n
    emit = wup.dtype != BF16
    assert not emit or n == tm
    tf = 512 if emit else 1024
    row = pl.BlockSpec((tm, d), lambda i, f: (i, 0))
    up_spec = pl.BlockSpec((d, tf), lambda i, f: (0, f))
    down_spec = pl.BlockSpec((tf, d), lambda i, f: (f, 0))
    out_specs = [row]
    out_shape = [jax.ShapeDtypeStruct((n, d), F32)]
    if emit:
        out_specs += [up_spec, down_spec]
        out_shape += [jax.ShapeDtypeStruct(wup.shape, BF16), jax.ShapeDtypeStruct(wdown.shape, BF16)]
    return pl.pallas_call(
        _ffn_kernel,
        grid=(n // tm, dff // tf),
        in_specs=[row, row, up_spec, down_spec],
        out_specs=out_specs,
        out_shape=out_shape,
        compiler_params=_cparams(("arbitrary", "arbitrary")),
        name="ffn",
    )(hm, x1, wup, wdown)


def _bucket(rel, valid):
    n = np.maximum(rel, 0)
    max_exact = N_BUCKETS // 2
    nf = np.maximum(n, 1).astype(np.float32)
    large = max_exact + (np.log(nf / np.float32(max_exact)) / np.float32(math.log(MAX_DIST / max_exact))
                         * np.float32(N_BUCKETS - max_exact)).astype(np.int32)
    large = np.minimum(large, N_BUCKETS - 1)
    return np.where(valid, np.where(n < max_exact, n, large), -1).astype(np.int32)


def _bias_tables_kernel(rb_ref, *refs, shifts):
    n = len(shifts)
    nbk, nh = rb_ref.shape
    for b_ref, o_ref, shift in zip(refs[:n], refs[n:], shifts):
        b = b_ref[...]
        rows = []
        for h in range(nh):
            sh = rb_ref[nbk - 1, h] if shift else 0.0
            acc = jnp.full(b.shape, NEG, F32)
            for k in range(nbk):
                acc = jnp.where(b == k, rb_ref[k, h] - sh, acc)
            if len(o_ref.shape) == 3:
                o_ref[h] = acc
            else:
                rows.append(acc)
        if rows:
            o_ref[...] = jnp.concatenate(rows, axis=0)


def _bias_tables(rel_bias, buckets, shifts):
    nh = rel_bias.shape[1]
    shapes = [jax.ShapeDtypeStruct((nh,) + (b.shape if b.shape[0] > 1 else b.shape[1:]), F32) for b in buckets]
    vm = pl.BlockSpec(memory_space=pltpu.VMEM)
    return pl.pallas_call(
        functools.partial(_bias_tables_kernel, shifts=tuple(shifts)),
        in_specs=[pl.BlockSpec(memory_space=pltpu.SMEM)] + [vm] * len(buckets),
        out_specs=[vm] * len(buckets),
        out_shape=shapes,
        name="bias_tables",
    )(rel_bias, *[jnp.asarray(b) for b in buckets])


def _overlap(nc, ns):
    i = np.arange(nc)[:, None] * CMP_STRIDE
    j = np.arange(ns)[None, :] * SEL_LEN
    return ((i < j + SEL_LEN) & (i + CMP_LEN > j)).astype(np.float32)


def _compress_weights(w, pos):
    n2 = CMP_STRIDE // 2
    w4 = w.reshape(2, n2, 2, HEAD_DIM, HEAD_DIM)
    top = jnp.concatenate([w4[0, :, 0], w4[1, :, 0]], axis=2)
    bot = jnp.concatenate([w4[0, :, 1], w4[1, :, 1]], axis=2)
    tiles = jnp.concatenate([top, bot], axis=1).astype(BF16)
    p4 = pos.reshape(2, n2, 2, HEAD_DIM)
    row_a = jnp.concatenate([p4[0, :, 0], p4[0, :, 1]], axis=1)
    row_b = jnp.concatenate([p4[1, :, 0], p4[1, :, 1]], axis=1)
    rows = jnp.concatenate([row_a[:, None], row_b[:, None], jnp.zeros((n2, 14, 2 * HEAD_DIM), F32)], axis=1)
    return tiles, rows.astype(BF16)


def _dense_tail(x, h, oa, ob, wga, wgb, wg, wn, wout, gmlp, wup, wdown, gate_rows=None):
    mix, *cast_m = _merge(h, oa, ob, wga, wgb, wg, wn, gate_rows)
    x1, hm, *cast_o = _outproj(x, mix, wout, gmlp)
    y, *cast_f = _ffn(hm, x1, wup, wdown)
    return y, (cast_m or [wga, wgb, wg, wn]) + (cast_o or [wout]) + (cast_f or [wup, wdown])


def kernel(x_prompt, x_sample, cache_cmp_kv, cache_sel_kv, state_win_kv, page_table, rel_bias, g_mix_norm, w_in,
           g_sgu, w_sgu, b_sgu, g_q, g_k, pos_cmp_k, w_cmp_k, pos_cmp_v, w_cmp_v, w_proj_gmlp, w_proj_nsa, w_out,
           g_mlp_norm, w_up, w_down):
    depth = g_mix_norm.shape[0]
    assert depth == 1
    l = 0
    bsz, seq, d = x_prompt.shape
    nb = x_sample.shape[0]
    assert x_sample.shape[1] == 1 and seq % KPAD == 0
    d_gm = g_sgu.shape[1]
    n_heads = rel_bias.shape[1]
    d_nsa = n_heads * HEAD_DIM
    kvw_cols = 2 * (n_heads // 4) * HEAD_DIM
    n_gate = 3 * n_heads
    c_q = 2 * d_gm
    c_kv = c_q + d_nsa
    c_gate = c_kv + 3 * kvw_cols
    c_ga = c_gate + n_gate
    c_gb = c_ga + d

    w = jnp.swapaxes(w_in[l], 0, 1)
    n_col_tiles = -(-c_ga // 1024)
    gmix = g_mix_norm[l][None]
    gsgu = g_sgu[l][None]
    gq = g_q[l][None]
    gk = g_k[l]
    gmlp = g_mlp_norm[l][None]
    wk_pairs, posk_rows = _compress_weights(w_cmp_k[l], pos_cmp_k[l])
    wv_pairs, posv_rows = _compress_weights(w_cmp_v[l], pos_cmp_v[l])
    pos_rows = jnp.stack([posk_rows, posv_rows])
    n_groups = w_sgu.shape[1]
    b_exp = jnp.repeat(b_sgu[l].T, d_gm // n_groups, axis=1)
    w00 = jnp.repeat(w_sgu[l][:, 0, 0], d_gm // n_groups)[None]
    b00 = jnp.repeat(b_sgu[l][:, 0], d_gm // n_groups)[None]

    n_pool, page = cache_cmp_kv.shape[1], cache_cmp_kv.shape[2]
    n_pages = page_table.shape[1]
    past = n_pages * page
    nwin = state_win_kv.shape[2]
    n_cmp_s = (past + 1 - CMP_LEN) // CMP_STRIDE + 1
    n_sel_s = -(-(past + 1) // SEL_LEN)
    nseg_s = past // CMP_STRIDE
    qi = np.arange(QBLOCK)[:, None]
    rel_s = qi + QBLOCK - np.arange(2 * QBLOCK)[None, :]
    rel_w = qi + WINDOW - np.arange(WINDOW + QBLOCK)[None, :]
    rel_c = qi + (KC_FRONT * CMP_STRIDE - CMP_LEN + 1) - CMP_STRIDE * np.arange(LANES)[None, :]
    srel_c = (past - (np.arange(nseg_s) * CMP_STRIDE + CMP_LEN - 1))[None]
    srel_s = (past - np.arange(past))[None]
    srel_w = (nwin - np.arange(nwin))[None]
    buckets = [
        _bucket(rel_c, rel_c >= 0), _bucket(rel_s, rel_s >= 0), _bucket(rel_w, (rel_w >= 0) & (rel_w < WINDOW)),
        _bucket(srel_c, (srel_c >= 0) & (np.arange(nseg_s)[None] < n_cmp_s)), _bucket(srel_s, srel_s >= 0),
        _bucket(srel_w, srel_w < WINDOW), np.zeros((1, LANES), np.int32)]
    tab_c, tab_s, tab_w, bias_c, bias_s, bias_w, bias_new = _bias_tables(
        rel_bias, buckets, [True, True, False, False, False, False, False])
    tab_c, tab_s, tab_w = [t.reshape(2, (n_heads // 2) * QBLOCK, t.shape[-1]) for t in (tab_c, tab_s, tab_w)]

    xs = x_sample.reshape(nb, d)
    h_s, v_s, oa_s, q_s, kvc_s, kvs_s, kvw_s, gates_s, w_cat = _inproj(
        xs, gmix, w, gsgu, gq, w00, b00, chunked=False, n_col_tiles=n_col_tiles)
    ov_s = np.zeros((nseg_s, LANES), np.float32)
    ov_s[:n_cmp_s, :n_sel_s] = _overlap(n_cmp_s, n_sel_s)
    expand = jnp.asarray(np.arange(LANES)[:, None] == (np.arange(past)[None, :] // SEL_LEN), BF16)
    gates_h = jnp.pad(gates_s[:, :n_gate].reshape(nb, n_heads, 3), ((0, 0), (0, 0), (0, LANES - 3)))
    lin = lambda a: a[l].reshape(a.shape[1], -1, HEAD_DIM)
    win_lin = lin(state_win_kv)
    ob_s, win_new = _nsa_sample(
        page_table, lin(cache_cmp_kv), lin(cache_sel_kv), win_lin,
        q_s.astype(F32).reshape(nb, n_heads, HEAD_DIM), gates_h,
        kvs_s.reshape(nb, 4, HEAD_DIM), kvw_s.reshape(nb, 4, HEAD_DIM), wk_pairs, wv_pairs, pos_rows, gk,
        bias_c, bias_s, bias_w, bias_new, jnp.asarray(ov_s), expand)
    y_s, (wga, wgb, wg, wn, wout, wup, wdown) = _dense_tail(
        xs, h_s, oa_s, ob_s.reshape(nb, d_nsa).astype(BF16), w, w, w_proj_gmlp[l], w_proj_nsa[l], w_out[l], gmlp,
        w_up[l], w_down[l], gate_rows=(c_ga, c_gb))

    xp = x_prompt.reshape(bsz * seq, d)
    h_p, v_p, oa_p, q_p, kvc_p, kvs_p, kvw_p, gates_p = _inproj(
        xp, gmix, w_cat, gsgu, gq, w_sgu[l], b_exp, chunked=True, n_col_tiles=n_col_tiles)
    kvs3 = kvs_p.reshape(bsz, 4 * seq, HEAD_DIM)
    kvw3 = kvw_p.reshape(bsz, 4 * seq, HEAD_DIM)
    kvc3 = kvc_p.reshape(bsz, 4 * seq, HEAD_DIM)
    ks, vs, kw, vw = _kvprep(kvs3, kvw3, gk)
    kc, vc = _compress_prompt(kvc3, wk_pairs, wv_pairs, pos_rows, gk)
    n_cmp = (seq - CMP_LEN) // CMP_STRIDE + 1
    n_selb = seq // SEL_LEN
    ov_p = np.zeros((KC_ROWS, LANES), np.float32)
    ov_p[KC_FRONT:KC_FRONT + n_cmp, :n_selb] = _overlap(n_cmp, n_selb)
    ov_p = jnp.asarray(ov_p)
    ob_p = _nsa_prompt(q_p.reshape(bsz, seq, d_nsa), gates_p.reshape(bsz, seq, 2 * LANES), kc, vc, ov_p, ks, vs, kw, vw,
                       tab_c, tab_s, tab_w)
    y_p, _ = _dense_tail(xp, h_p, oa_p, ob_p.reshape(bsz * seq, d_nsa), wga, wgb, wg, wn, wout, gmlp, wup, wdown)

    n_kv = n_heads // 4
    kv6 = lambda a, b_, t_: a.reshape(1, b_, t_, 2, n_kv, HEAD_DIM)
    nw_p = min(WINDOW, seq)
    last = ((seq - 1) // CHUNK) * CHUNK
    nw_s = min(WINDOW, past + 1)
    return (y_p.reshape(bsz, seq, d), y_s.reshape(nb, 1, d),
            kv6(kvc3, bsz, seq), kv6(kvs3, bsz, seq), kv6(kvw3[:, 4 * (seq - nw_p):], bsz, nw_p),
            v_p.reshape(bsz, seq, d_gm)[:, last:][None],
            kv6(kvc_s, nb, 1), kv6(kvs_s, nb, 1), kv6(win_new, nb, nw_s),
            v_s.reshape(1, nb, 1, d_gm))
```
